```python
import math
import jax, jax.numpy as jnp
from jax import lax
import numpy as np

D_MODEL = 1024
BATCH = 16
SEQ = 2048
DEPTH = 4

HEAD_DIM = 64
N_MIXERS = 4
GROUP_WIDTH = D_MODEL // N_MIXERS
GROUP_HEADS = GROUP_WIDTH // HEAD_DIM
CHUNK = 128
POOL_WINDOWS = (2, 4, 8, 16)
POOL_GROUPS = len(POOL_WINDOWS)
POOL_GROUP_DIM = GROUP_WIDTH // POOL_GROUPS
WINDOW = 128
SWA_Q_HEADS = GROUP_HEADS
SWA_KV_HEADS = 2
SWA_GROUP = SWA_Q_HEADS // SWA_KV_HEADS
SB_HEADS = GROUP_HEADS
SB_BLOCK = 128
N_BUCKETS = 32
MAX_DISTANCE = 128
D_FF = ((8 * D_MODEL // 3 + 255) // 256) * 256
EPS = 1e-6

IN_SIZES = (GROUP_WIDTH, GROUP_WIDTH, GROUP_WIDTH,
            GROUP_WIDTH, SWA_KV_HEADS * HEAD_DIM, SWA_KV_HEADS * HEAD_DIM,
            GROUP_WIDTH, GROUP_WIDTH, GROUP_WIDTH)
D_IN = sum(IN_SIZES)

kernel_name = "hybrid_parallel_headgroup_trunk"


def _split_points():
    pts, acc = [], 0
    for s in IN_SIZES[:-1]:
        acc += s
        pts.append(acc)
    return pts


def _rmsnorm(x, g):
    xf = x.astype(jnp.float32)
    y = xf * lax.rsqrt(jnp.mean(xf * xf, axis=-1, keepdims=True) + EPS)
    return (y * g.astype(jnp.float32)).astype(x.dtype)


def _layernorm_noaffine(x):
    xf = x.astype(jnp.float32)
    mu = jnp.mean(xf, axis=-1, keepdims=True)
    xc = xf - mu
    y = xc * lax.rsqrt(jnp.mean(xc * xc, axis=-1, keepdims=True) + EPS)
    return y.astype(x.dtype)


def _chunked_sgu(u, v, w_s, b_s):
    B, S, _ = u.shape
    nc = S // CHUNK
    u = jax.nn.gelu(u)
    v = _layernorm_noaffine(jax.nn.gelu(v).reshape(B, nc, CHUNK, GROUP_HEADS, HEAD_DIM))
    causal = jnp.tril(jnp.ones((CHUNK, CHUNK), w_s.dtype))
    w = w_s * causal[None]
    mix = jnp.einsum('hts,bnshd->bnthd', w, v) + b_s.T[None, None, :, :, None]
    return u * mix.reshape(B, S, GROUP_WIDTH)


def _multiscale_pool(p, w_pool, scale):
    B, S, _ = p.shape
    pg = p.reshape(B, S, POOL_GROUPS, POOL_GROUP_DIM)
    csum = jnp.cumsum(pg.astype(jnp.float32), axis=1)
    csum = jnp.pad(csum, ((0, 0), (1, 0), (0, 0), (0, 0)))
    t = jnp.arange(S)[:, None]
    win = jnp.array(POOL_WINDOWS, jnp.int32)[None, :]
    start = jnp.maximum(t + 1 - win, 0)
    gidx = jnp.arange(POOL_GROUPS)[None, :]
    window_sum = csum[:, 1:] - csum[:, start, gidx]
    count = (t + 1 - start).astype(jnp.float32)
    pooled = window_sum / count[None, :, :, None]
    y = pooled.astype(p.dtype) - pg
    y = jnp.einsum('bsgc,gcd->bsgd', y, w_pool)
    return y.reshape(B, S, GROUP_WIDTH) * scale


def _t5_bucket(dist):
    max_exact = N_BUCKETS // 2
    df = jnp.maximum(dist, 1).astype(jnp.float32)
    large = max_exact + (jnp.log(df / max_exact) / math.log(MAX_DISTANCE / max_exact)
                         * (N_BUCKETS - max_exact)).astype(jnp.int32)
    large = jnp.minimum(large, N_BUCKETS - 1)
    return jnp.where(dist < max_exact, dist, large)


def _swa_sink_attention(q, k, v, sinks, rel_bias):
    B, S, _ = q.shape
    nb = S // WINDOW
    qb = q.reshape(B, nb, WINDOW, SWA_KV_HEADS, SWA_GROUP, HEAD_DIM)
    kb = k.reshape(B, nb, WINDOW, SWA_KV_HEADS, HEAD_DIM)
    vb = v.reshape(B, nb, WINDOW, SWA_KV_HEADS, HEAD_DIM)
    pad = ((0, 0), (1, 0), (0, 0), (0, 0), (0, 0))
    k2 = jnp.concatenate([jnp.pad(kb, pad)[:, :-1], kb], axis=2)
    v2 = jnp.concatenate([jnp.pad(vb, pad)[:, :-1], vb], axis=2)
    logits = jnp.einsum('bnqhgd,bnkhd->bnhgqk', qb, k2,
                        preferred_element_type=jnp.float32) * (HEAD_DIM ** -0.5)
    dist = (jnp.arange(WINDOW)[:, None] + WINDOW) - jnp.arange(2 * WINDOW)[None, :]
    in_window = (dist >= 0) & (dist < WINDOW)
    bias = rel_bias.astype(jnp.float32)[_t5_bucket(jnp.clip(dist, 0, WINDOW - 1))]
    bias = bias.transpose(2, 0, 1).reshape(SWA_KV_HEADS, SWA_GROUP, WINDOW, 2 * WINDOW)
    not_pad = (jnp.arange(nb)[:, None] > 0) | (jnp.arange(2 * WINDOW)[None, :] >= WINDOW)
    mask = in_window[None] & not_pad[:, None, :]
    logits = jnp.where(mask[None, :, None, None], logits + bias, -1e30)
    sink = jnp.broadcast_to(sinks.astype(jnp.float32).reshape(SWA_KV_HEADS, SWA_GROUP, 1, 1),
                            logits.shape[:-1] + (1,))
    probs = jax.nn.softmax(jnp.concatenate([logits, sink], axis=-1), axis=-1)[..., :-1]
    out = jnp.einsum('bnhgqk,bnkhd->bnqhgd', probs.astype(v.dtype), v2)
    return out.reshape(B, S, GROUP_WIDTH)


def _stick_breaking_attention(q, k, v):
    B, S, _ = q.shape
    nb = S // SB_BLOCK
    kh = k.reshape(B, S, SB_HEADS, HEAD_DIM)
    vh = v.reshape(B, S, SB_HEADS, HEAD_DIM)
    qb = q.reshape(B, nb, SB_BLOCK, SB_HEADS, HEAD_DIM).transpose(1, 0, 2, 3, 4)
    key_pos = jnp.arange(S)
    scale = HEAD_DIM ** -0.5

    def block(args):
        qblk, n = args
        z = jnp.einsum('bqhd,bkhd->bhqk', qblk, kh,
                       preferred_element_type=jnp.float32) * scale
        q_pos = n * SB_BLOCK + jnp.arange(SB_BLOCK)
        causal = key_pos[None, :] < q_pos[:, None]
        log_1m = jnp.where(causal, jax.nn.log_sigmoid(-z), 0.0)
        tail = lax.cumsum(log_1m, axis=3, reverse=True) - log_1m
        w = jnp.where(causal, jnp.exp(jax.nn.log_sigmoid(z) + tail), 0.0)
        return jnp.einsum('bhqk,bkhd->bqhd', w.astype(v.dtype), vh)

    out = lax.map(block, (qb, jnp.arange(nb)))
    return out.transpose(1, 0, 2, 3, 4).reshape(B, S, GROUP_WIDTH)


def _fwd_setup_inputs(seed: int = 0) -> dict:
    key = jax.random.key(seed)
    ks = jax.random.split(key, 16)
    f32 = jnp.float32
    nrm = lambda k, shape, s: jax.random.normal(k, shape, f32) * s
    return {
        "x": jax.random.normal(ks[0], (BATCH, SEQ, D_MODEL), f32),
        "w_in": nrm(ks[1], (DEPTH, D_MODEL, D_IN), D_MODEL ** -0.5),
        "w_out": nrm(ks[2], (DEPTH, D_MODEL, D_MODEL), D_MODEL ** -0.5),
        "sgu_w": nrm(ks[3], (DEPTH, GROUP_HEADS, CHUNK, CHUNK), CHUNK ** -0.5),
        "sgu_b": 1.0 + nrm(ks[4], (DEPTH, GROUP_HEADS, CHUNK), 0.02),
        "pool_w": nrm(ks[5], (DEPTH, POOL_GROUPS, POOL_GROUP_DIM, POOL_GROUP_DIM), POOL_GROUP_DIM ** -0.5),
        "pool_scale": 1.0 + nrm(ks[6], (DEPTH, GROUP_WIDTH), 0.02),
        "swa_sinks": nrm(ks[7], (DEPTH, SWA_Q_HEADS), 1.0),
        "rel_bias": nrm(ks[8], (N_BUCKETS, SWA_Q_HEADS), 0.5),
        "mix_out_gain": 1.0 + nrm(ks[9], (DEPTH, D_MODEL), 0.02),
        "norm_mix": 1.0 + nrm(ks[10], (DEPTH, D_MODEL), 0.02),
        "norm_ffn": 1.0 + nrm(ks[11], (DEPTH, D_MODEL), 0.02),
        "w_gate_up": nrm(ks[12], (DEPTH, D_MODEL, 2 * D_FF), D_MODEL ** -0.5),
        "w_down": nrm(ks[13], (DEPTH, D_FF, D_MODEL), D_FF ** -0.5),
        "norm_final": 1.0 + nrm(ks[14], (D_MODEL,), 0.02),
    }


def _fwd_reference(x, w_in, w_out, sgu_w, sgu_b, pool_w, pool_scale, swa_sinks, rel_bias,
              mix_out_gain, norm_mix, norm_ffn, w_gate_up, w_down, norm_final):
    B, S, _ = x.shape
    splits = _split_points()
    for l in range(DEPTH):
        h = _rmsnorm(x, norm_mix[l])
        proj = h @ w_in[l]
        a_u, a_v, b_in, c_q, c_k, c_v, d_q, d_k, d_v = jnp.split(proj, splits, axis=-1)
        y_a = _chunked_sgu(a_u, a_v, sgu_w[l], sgu_b[l])
        y_b = _multiscale_pool(b_in, pool_w[l], pool_scale[l])
        y_c = _swa_sink_attention(c_q, c_k, c_v, swa_sinks[l], rel_bias)
        y_d = _stick_breaking_attention(d_q, d_k, d_v)
        ycat = jnp.stack([y_a, y_b, y_c, y_d], axis=2)
        ycat = _rmsnorm(ycat, mix_out_gain[l].reshape(N_MIXERS, GROUP_WIDTH))
        x = x + ycat.reshape(B, S, D_MODEL) @ w_out[l]
        h = _rmsnorm(x, norm_ffn[l])
        gate, up = jnp.split(h @ w_gate_up[l], 2, axis=-1)
        x = x + (jax.nn.silu(gate) * up) @ w_down[l]
    return _rmsnorm(x, norm_final)


import jax as _jax
import jax.numpy as _jnp

TWIN_FORMAT = 'train_step'
FWD_PARAMS = ['x', 'w_in', 'w_out', 'sgu_w', 'sgu_b', 'pool_w', 'pool_scale', 'swa_sinks', 'rel_bias', 'mix_out_gain', 'norm_mix', 'norm_ffn', 'w_gate_up', 'w_down', 'norm_final']
TWIN_WEIGHTS = ['w_in', 'w_out', 'sgu_w', 'sgu_b', 'pool_w', 'pool_scale', 'swa_sinks', 'rel_bias', 'mix_out_gain', 'norm_mix', 'norm_ffn', 'w_gate_up', 'w_down', 'norm_final']
TWIN_DIFF_INPUT = 'x'
TWIN_INPUTS = ['x', 'w_in', 'w_out', 'sgu_w', 'sgu_b', 'pool_w', 'pool_scale', 'swa_sinks', 'rel_bias', 'mix_out_gain', 'norm_mix', 'norm_ffn', 'w_gate_up', 'w_down', 'norm_final', 'loss_target', 'm_w_in', 'm_w_out', 'm_sgu_w', 'm_sgu_b', 'm_pool_w', 'm_pool_scale', 'm_swa_sinks', 'm_rel_bias', 'm_mix_out_gain', 'm_norm_mix', 'm_norm_ffn', 'm_w_gate_up', 'm_w_down', 'm_norm_final', 'v_w_in', 'v_w_out', 'v_sgu_w', 'v_sgu_b', 'v_pool_w', 'v_pool_scale', 'v_swa_sinks', 'v_rel_bias', 'v_mix_out_gain', 'v_norm_mix', 'v_norm_ffn', 'v_w_gate_up', 'v_w_down', 'v_norm_final']
TWIN_OUTPUTS = ['loss', 'grad_x', 'grad_w_in', 'grad_w_out', 'grad_sgu_w', 'grad_sgu_b', 'grad_pool_w', 'grad_pool_scale', 'grad_swa_sinks', 'grad_rel_bias', 'grad_mix_out_gain', 'grad_norm_mix', 'grad_norm_ffn', 'grad_w_gate_up', 'grad_w_down', 'grad_norm_final', 'delta_w_in', 'delta_w_out', 'delta_sgu_w', 'delta_sgu_b', 'delta_pool_w', 'delta_pool_scale', 'delta_swa_sinks', 'delta_rel_bias', 'delta_mix_out_gain', 'delta_norm_mix', 'delta_norm_ffn', 'delta_w_gate_up', 'delta_w_down', 'delta_norm_final', 'new_m_w_in', 'new_m_w_out', 'new_m_sgu_w', 'new_m_sgu_b', 'new_m_pool_w', 'new_m_pool_scale', 'new_m_swa_sinks', 'new_m_rel_bias', 'new_m_mix_out_gain', 'new_m_norm_mix', 'new_m_norm_ffn', 'new_m_w_gate_up', 'new_m_w_down', 'new_m_norm_final', 'new_v_w_in', 'new_v_w_out', 'new_v_sgu_w', 'new_v_sgu_b', 'new_v_pool_w', 'new_v_pool_scale', 'new_v_swa_sinks', 'new_v_rel_bias', 'new_v_mix_out_gain', 'new_v_norm_mix', 'new_v_norm_ffn', 'new_v_w_gate_up', 'new_v_w_down', 'new_v_norm_final']
TWIN_LEAF_KINDS = {'loss': 'loss', 'grad_x': 'grad_x', 'grad_w_in': 'grad_w', 'grad_w_out': 'grad_w', 'grad_sgu_w': 'grad_w', 'grad_sgu_b': 'grad_w', 'grad_pool_w': 'grad_w', 'grad_pool_scale': 'grad_w', 'grad_swa_sinks': 'grad_w', 'grad_rel_bias': 'grad_w', 'grad_mix_out_gain': 'grad_w', 'grad_norm_mix': 'grad_w', 'grad_norm_ffn': 'grad_w', 'grad_w_gate_up': 'grad_w', 'grad_w_down': 'grad_w', 'grad_norm_final': 'grad_w', 'delta_w_in': 'delta_w', 'delta_w_out': 'delta_w', 'delta_sgu_w': 'delta_w', 'delta_sgu_b': 'delta_w', 'delta_pool_w': 'delta_w', 'delta_pool_scale': 'delta_w', 'delta_swa_sinks': 'delta_w', 'delta_rel_bias': 'delta_w', 'delta_mix_out_gain': 'delta_w', 'delta_norm_mix': 'delta_w', 'delta_norm_ffn': 'delta_w', 'delta_w_gate_up': 'delta_w', 'delta_w_down': 'delta_w', 'delta_norm_final': 'delta_w', 'new_m_w_in': 'new_m', 'new_m_w_out': 'new_m', 'new_m_sgu_w': 'new_m', 'new_m_sgu_b': 'new_m', 'new_m_pool_w': 'new_m', 'new_m_pool_scale': 'new_m', 'new_m_swa_sinks': 'new_m', 'new_m_rel_bias': 'new_m', 'new_m_mix_out_gain': 'new_m', 'new_m_norm_mix': 'new_m', 'new_m_norm_ffn': 'new_m', 'new_m_w_gate_up': 'new_m', 'new_m_w_down': 'new_m', 'new_m_norm_final': 'new_m', 'new_v_w_in': 'new_v', 'new_v_w_out': 'new_v', 'new_v_sgu_w': 'new_v', 'new_v_sgu_b': 'new_v', 'new_v_pool_w': 'new_v', 'new_v_pool_scale': 'new_v', 'new_v_swa_sinks': 'new_v', 'new_v_rel_bias': 'new_v', 'new_v_mix_out_gain': 'new_v', 'new_v_norm_mix': 'new_v', 'new_v_norm_ffn': 'new_v', 'new_v_w_gate_up': 'new_v', 'new_v_w_down': 'new_v', 'new_v_norm_final': 'new_v'}


def _forward(args):
    return _fwd_reference(*[args[k] for k in FWD_PARAMS])


def _output_shape():
    out = _jax.eval_shape(lambda: _forward(_fwd_setup_inputs(0)))
    return out.shape, out.dtype

N_MICROBATCH = 1
ADAM_LR = 0.001
ADAM_B1 = 0.9
ADAM_B2 = 0.999
ADAM_EPS = 1e-08
ADAM_WD = 0.01
ADAM_STEP = 10
PER_EXAMPLE_BATCH_AXIS = {'x': 0, 'loss_target': 0}
SHARED_INPUTS = []
_WEIGHT_DTYPES = {'w_in': _jnp.float32, 'w_out': _jnp.float32, 'sgu_w': _jnp.float32, 'sgu_b': _jnp.float32, 'pool_w': _jnp.float32, 'pool_scale': _jnp.float32, 'swa_sinks': _jnp.float32, 'rel_bias': _jnp.float32, 'mix_out_gain': _jnp.float32, 'norm_mix': _jnp.float32, 'norm_ffn': _jnp.float32, 'w_gate_up': _jnp.float32, 'w_down': _jnp.float32, 'norm_final': _jnp.float32}
MOMENT_SCALE = {'w_in': 1.010465e-01, 'w_out': 1.229810e-01, 'sgu_w': 4.368352e-02, 'sgu_b': 5.958741e-02, 'pool_w': 1.128364e-01, 'pool_scale': 1.165446e-01, 'swa_sinks': 4.804606e-02, 'rel_bias': 2.112889e-01, 'mix_out_gain': 1.211959e-01, 'norm_mix': 1.457269e-01, 'norm_ffn': 8.741742e-02, 'w_gate_up': 3.688053e-02, 'w_down': 6.020306e-02, 'norm_final': 3.220309e+01}


def _to_microbatches(a, axis):
    t = _jnp.moveaxis(a, axis, 0)
    t = t.reshape((N_MICROBATCH, t.shape[0] // N_MICROBATCH) + t.shape[1:])
    return _jnp.moveaxis(t, 1, axis + 1)


def setup_inputs(seed: int = 0) -> dict:
    inp = _fwd_setup_inputs(seed)
    key = _jax.random.fold_in(_jax.random.key(seed), 7919)
    shape, _ = _output_shape()
    out = dict(inp)
    out["loss_target"] = _jax.random.normal(_jax.random.fold_in(key, 0), shape, _jnp.float32)
    for i, name in enumerate(TWIN_WEIGHTS):
        w = inp[name].astype(_jnp.float32)
        if MOMENT_SCALE is None:
            s = _jnp.sqrt(_jnp.mean(_jnp.square(w)) + 1e-30)
        else:
            s = MOMENT_SCALE[name]
        km, kv = _jax.random.split(_jax.random.fold_in(key, i + 1))
        out[name] = w
        out["m_" + name] = s * _jax.random.normal(km, w.shape, _jnp.float32)
        out["v_" + name] = (s * s) * _jax.random.uniform(kv, w.shape, _jnp.float32, 0.5, 1.5)
    if N_MICROBATCH > 1:
        for name, axis in PER_EXAMPLE_BATCH_AXIS.items():
            out[name] = _to_microbatches(out[name], axis)
    return {'x': out['x'], 'w_in': out['w_in'], 'w_out': out['w_out'], 'sgu_w': out['sgu_w'], 'sgu_b': out['sgu_b'], 'pool_w': out['pool_w'], 'pool_scale': out['pool_scale'], 'swa_sinks': out['swa_sinks'], 'rel_bias': out['rel_bias'], 'mix_out_gain': out['mix_out_gain'], 'norm_mix': out['norm_mix'], 'norm_ffn': out['norm_ffn'], 'w_gate_up': out['w_gate_up'], 'w_down': out['w_down'], 'norm_final': out['norm_final'], 'loss_target': out['loss_target'], 'm_w_in': out['m_w_in'], 'm_w_out': out['m_w_out'], 'm_sgu_w': out['m_sgu_w'], 'm_sgu_b': out['m_sgu_b'], 'm_pool_w': out['m_pool_w'], 'm_pool_scale': out['m_pool_scale'], 'm_swa_sinks': out['m_swa_sinks'], 'm_rel_bias': out['m_rel_bias'], 'm_mix_out_gain': out['m_mix_out_gain'], 'm_norm_mix': out['m_norm_mix'], 'm_norm_ffn': out['m_norm_ffn'], 'm_w_gate_up': out['m_w_gate_up'], 'm_w_down': out['m_w_down'], 'm_norm_final': out['m_norm_final'], 'v_w_in': out['v_w_in'], 'v_w_out': out['v_w_out'], 'v_sgu_w': out['v_sgu_w'], 'v_sgu_b': out['v_sgu_b'], 'v_pool_w': out['v_pool_w'], 'v_pool_scale': out['v_pool_scale'], 'v_swa_sinks': out['v_swa_sinks'], 'v_rel_bias': out['v_rel_bias'], 'v_mix_out_gain': out['v_mix_out_gain'], 'v_norm_mix': out['v_norm_mix'], 'v_norm_ffn': out['v_norm_ffn'], 'v_w_gate_up': out['v_w_gate_up'], 'v_w_down': out['v_w_down'], 'v_norm_final': out['v_norm_final']}


def _loss(weights, diff, rest, loss_target):
    with _jax.named_scope("forward"):
        args = {**rest, TWIN_DIFF_INPUT: diff, **{k: w.astype(_WEIGHT_DTYPES[k]) for k, w in weights.items()}}
        y = _forward(args)
    with _jax.named_scope("loss_head"):
        err = _jnp.square(y.astype(_jnp.float32) - loss_target)
        return 0.5 * _jnp.sum(_jnp.mean(err, axis=-1)) if err.ndim else 0.5 * err


def _adamw(w, g, m, v):
    m = ADAM_B1 * m + (1.0 - ADAM_B1) * g
    v = ADAM_B2 * v + (1.0 - ADAM_B2) * _jnp.square(g)
    m_hat = m / (1.0 - ADAM_B1 ** ADAM_STEP)
    v_hat = v / (1.0 - ADAM_B2 ** ADAM_STEP)
    delta = -ADAM_LR * (m_hat / (_jnp.sqrt(v_hat) + ADAM_EPS) + ADAM_WD * w)
    return delta, m, v


def reference(x, w_in, w_out, sgu_w, sgu_b, pool_w, pool_scale, swa_sinks, rel_bias, mix_out_gain, norm_mix, norm_ffn, w_gate_up, w_down, norm_final, loss_target, m_w_in, m_w_out, m_sgu_w, m_sgu_b, m_pool_w, m_pool_scale, m_swa_sinks, m_rel_bias, m_mix_out_gain, m_norm_mix, m_norm_ffn, m_w_gate_up, m_w_down, m_norm_final, v_w_in, v_w_out, v_sgu_w, v_sgu_b, v_pool_w, v_pool_scale, v_swa_sinks, v_rel_bias, v_mix_out_gain, v_norm_mix, v_norm_ffn, v_w_gate_up, v_w_down, v_norm_final):
    given = dict(x=x, w_in=w_in, w_out=w_out, sgu_w=sgu_w, sgu_b=sgu_b, pool_w=pool_w, pool_scale=pool_scale, swa_sinks=swa_sinks, rel_bias=rel_bias, mix_out_gain=mix_out_gain, norm_mix=norm_mix, norm_ffn=norm_ffn, w_gate_up=w_gate_up, w_down=w_down, norm_final=norm_final, loss_target=loss_target, m_w_in=m_w_in, m_w_out=m_w_out, m_sgu_w=m_sgu_w, m_sgu_b=m_sgu_b, m_pool_w=m_pool_w, m_pool_scale=m_pool_scale, m_swa_sinks=m_swa_sinks, m_rel_bias=m_rel_bias, m_mix_out_gain=m_mix_out_gain, m_norm_mix=m_norm_mix, m_norm_ffn=m_norm_ffn, m_w_gate_up=m_w_gate_up, m_w_down=m_w_down, m_norm_final=m_norm_final, v_w_in=v_w_in, v_w_out=v_w_out, v_sgu_w=v_sgu_w, v_sgu_b=v_sgu_b, v_pool_w=v_pool_w, v_pool_scale=v_pool_scale, v_swa_sinks=v_swa_sinks, v_rel_bias=v_rel_bias, v_mix_out_gain=v_mix_out_gain, v_norm_mix=v_norm_mix, v_norm_ffn=v_norm_ffn, v_w_gate_up=v_w_gate_up, v_w_down=v_w_down, v_norm_final=v_norm_final)
    weights = {n: given[n] for n in TWIN_WEIGHTS}
    shared = {n: given[n] for n in SHARED_INPUTS}
    per_example = {n: given[n] for n in ['x']}
    grad_fn = _jax.value_and_grad(_loss, argnums=(0, 1))

    def one_microbatch(ex, loss_target):
        ex = dict(ex)
        diff = ex.pop(TWIN_DIFF_INPUT)
        return grad_fn(weights, diff, {**shared, **ex}, loss_target)

    if N_MICROBATCH == 1:
        loss, (grad_w, grad_x) = one_microbatch(per_example, given["loss_target"])
    else:
        def body(carry, xs):
            loss_sum, grad_sum = carry
            l_k, (gw_k, gx_k) = one_microbatch(xs[0], xs[1])
            with _jax.named_scope("update"):
                return (loss_sum + l_k, _jax.tree.map(_jnp.add, grad_sum, gw_k)), gx_k

        init = (_jnp.zeros((), _jnp.float32), _jax.tree.map(_jnp.zeros_like, weights))
        (loss, grad_w), grad_x = _jax.lax.scan(body, init, (per_example, given["loss_target"]))
    with _jax.named_scope("update"):
        delta_w, new_m, new_v = {}, {}, {}
        for n in TWIN_WEIGHTS:
            delta_w[n], new_m[n], new_v[n] = _adamw(weights[n], grad_w[n], given["m_" + n], given["v_" + n])
    return (loss, grad_x, *[grad_w[n] for n in TWIN_WEIGHTS], *[delta_w[n] for n in TWIN_WEIGHTS],
            *[new_m[n] for n in TWIN_WEIGHTS], *[new_v[n] for n in TWIN_WEIGHTS])
```

```python
import functools

import jax
import jax.numpy as jnp
from jax import lax
from jax.experimental import pallas as pl
from jax.experimental.pallas import tpu as pltpu

F32 = jnp.float32
BF16 = jnp.bfloat16

N_DEV = 8
DEPTH = 4
D_MODEL = 1024
GROUP_WIDTH = 256
HEAD_DIM = 64
GROUP_HEADS = 4
BLOCK = 128
N_BUCKETS = 32
MAX_DISTANCE = 128
POOL_WINDOWS = (2, 4, 8, 16)
D_FF = 2816
FF_SHARD = D_FF // 4
FF_PAD = 768
D_FF_PAD = 4 * FF_PAD
EPS = 1e-6
ATT_SCALE = HEAD_DIM ** -0.5
ADAM_LR = 0.001
ADAM_B1 = 0.9
ADAM_B2 = 0.999
ADAM_EPS = 1e-08
ADAM_WD = 0.01
ADAM_STEP = 10
VMEM_LIMIT = 56 * 1024 * 1024
MESH_AXES = ("x", "y", "c")


def _cp(*sem):
    return pltpu.CompilerParams(dimension_semantics=sem or None, vmem_limit_bytes=VMEM_LIMIT)


_NT = (((1,), (1,)), ((), ()))
_TN = (((0,), (0,)), ((), ()))


@jax.custom_vjp
def _bdot(a, b):
    return jnp.dot(a.astype(BF16), b.astype(BF16), preferred_element_type=F32)


def _bdot_fwd(a, b):
    return _bdot(a, b), (a.astype(BF16), b.astype(BF16))


def _bdot_bwd(res, ct):
    a, b = res
    c = ct.astype(BF16)
    return (lax.dot_general(c, b, _NT, preferred_element_type=F32),
            lax.dot_general(a, c, _TN, preferred_element_type=F32))


_bdot.defvjp(_bdot_fwd, _bdot_bwd)


@jax.custom_vjp
def _bdot_nt(a, b):
    return lax.dot_general(a.astype(BF16), b.astype(BF16), _NT, preferred_element_type=F32)


def _bdot_nt_fwd(a, b):
    return _bdot_nt(a, b), (a.astype(BF16), b.astype(BF16))


def _bdot_nt_bwd(res, ct):
    a, b = res
    c = ct.astype(BF16)
    return (jnp.dot(c, b, preferred_element_type=F32),
            lax.dot_general(c, a, _TN, preferred_element_type=F32))


_bdot_nt.defvjp(_bdot_nt_fwd, _bdot_nt_bwd)


def _rms(x, g):
    return x * lax.rsqrt(jnp.mean(x * x, axis=-1, keepdims=True) + EPS) * g


def _split_dot(x, u):
    hi = x.astype(BF16)
    lo = (x - hi.astype(F32)).astype(BF16)
    return jnp.dot(hi, u, preferred_element_type=F32) + jnp.dot(lo, u, preferred_element_type=F32)


def _head_mask(h, shape):
    col = lax.broadcasted_iota(jnp.int32, shape, 1)
    return (col >= h * HEAD_DIM) & (col < (h + 1) * HEAD_DIM)


def _norm_mm(x, g, w, out_dtype, name, tm=512):
    T, K = x.shape
    nb, _, tn = w.shape

    def body(x_ref, g_ref, w_ref, o_ref, h_ref):
        @pl.when(pl.program_id(1) == 0)
        def _():
            h_ref[...] = _rms(x_ref[...], g_ref[...]).astype(BF16)
        o_ref[...] = jnp.dot(h_ref[...], w_ref[...], preferred_element_type=F32).astype(o_ref.dtype)

    return pl.pallas_call(
        body, grid=(T // tm, nb), name=name,
        in_specs=[pl.BlockSpec((tm, K), lambda i, j: (i, 0)), pl.BlockSpec((1, K), lambda i, j: (0, 0)),
                  pl.BlockSpec((None, K, tn), lambda i, j: (j, 0, 0))],
        out_specs=[pl.BlockSpec((tm, tn), lambda i, j: (i, j)), pl.BlockSpec((tm, K), lambda i, j: (i, 0))],
        out_shape=[jax.ShapeDtypeStruct((T, nb * tn), out_dtype), jax.ShapeDtypeStruct((T, K), BF16)],
        compiler_params=_cp("parallel", "arbitrary"))(x, g, w)


def _mm_res(res, a, w, name, tm=512, tn=512):
    T, K = a.shape
    N = w.shape[1]

    def body(r_ref, a_ref, w_ref, o_ref):
        o_ref[...] = r_ref[...] + jnp.dot(a_ref[...], w_ref[...], preferred_element_type=F32)

    return pl.pallas_call(
        body, grid=(T // tm, N // tn), name=name,
        in_specs=[pl.BlockSpec((tm, tn), lambda i, j: (i, j)), pl.BlockSpec((tm, K), lambda i, j: (i, 0)),
                  pl.BlockSpec((K, tn), lambda i, j: (0, j))],
        out_specs=pl.BlockSpec((tm, tn), lambda i, j: (i, j)),
        out_shape=jax.ShapeDtypeStruct((T, N), F32),
        compiler_params=_cp("parallel", "parallel"))(res, a, w)


def _mm_nt(a, w, out_dtype, name, tm=512, tn=512):
    T, K = a.shape
    N = w.shape[0]

    def body(a_ref, w_ref, o_ref):
        o_ref[...] = lax.dot_general(a_ref[...].astype(BF16), w_ref[...], _NT,
                                     preferred_element_type=F32).astype(o_ref.dtype)

    return pl.pallas_call(
        body, grid=(T // tm, N // tn), name=name,
        in_specs=[pl.BlockSpec((tm, K), lambda i, j: (i, 0)), pl.BlockSpec((tn, K), lambda i, j: (j, 0))],
        out_specs=pl.BlockSpec((tm, tn), lambda i, j: (i, j)),
        out_shape=jax.ShapeDtypeStruct((T, N), out_dtype),
        compiler_params=_cp("parallel", "parallel"))(a, w)


def _mm_nt_acc(a, w, name, tm=512):
    T = a.shape[0]
    nb, K, tn = w.shape

    def body(a_ref, w_ref, o_ref):
        part = lax.dot_general(a_ref[...], w_ref[...], _NT, preferred_element_type=F32)

        @pl.when(pl.program_id(1) == 0)
        def _():
            o_ref[...] = part

        @pl.when(pl.program_id(1) > 0)
        def _():
            o_ref[...] += part

    return pl.pallas_call(
        body, grid=(T // tm, nb), name=name,
        in_specs=[pl.BlockSpec((tm, tn), lambda i, j: (i, j)), pl.BlockSpec((None, K, tn), lambda i, j: (j, 0, 0))],
        out_specs=pl.BlockSpec((tm, K), lambda i, j: (i, 0)),
        out_shape=jax.ShapeDtypeStruct((T, K), F32),
        compiler_params=_cp("parallel", "arbitrary"))(a, w)


def _mm_tn_cols(lhs, rhs, tn, name, tm=512):
    T, K = lhs.shape
    nb = rhs.shape[1] // tn
    nt = T // tm

    def body(l_ref, r_ref, o_ref, acc):
        part = lax.dot_general(l_ref[...], r_ref[...].astype(BF16), _TN, preferred_element_type=F32)

        @pl.when(pl.program_id(1) == 0)
        def _():
            acc[...] = part

        @pl.when(pl.program_id(1) > 0)
        def _():
            acc[...] += part

        @pl.when(pl.program_id(1) == nt - 1)
        def _():
            o_ref[...] = acc[...].astype(BF16)

    return pl.pallas_call(
        body, grid=(nb, nt), name=name,
        in_specs=[pl.BlockSpec((tm, K), lambda j, t: (t, 0)), pl.BlockSpec((tm, tn), lambda j, t: (t, j))],
        out_specs=pl.BlockSpec((None, K, tn), lambda j, t: (j, 0, 0)),
        out_shape=jax.ShapeDtypeStruct((nb, K, tn), BF16),
        scratch_shapes=[pltpu.VMEM((K, tn), F32)],
        compiler_params=_cp("parallel", "arbitrary"))(lhs, rhs)


def _mm_tn_rows(lhs, rhs, tk, name, tm=512):
    T, Kl = lhs.shape
    N = rhs.shape[1]
    nt = T // tm

    def body(l_ref, r_ref, o_ref, acc):
        part = lax.dot_general(l_ref[...], r_ref[...].astype(BF16), _TN, preferred_element_type=F32)

        @pl.when(pl.program_id(1) == 0)
        def _():
            acc[...] = part

        @pl.when(pl.program_id(1) > 0)
        def _():
            acc[...] += part

        @pl.when(pl.program_id(1) == nt - 1)
        def _():
            o_ref[...] = acc[...].astype(BF16)

    return pl.pallas_call(
        body, grid=(Kl // tk, nt), name=name,
        in_specs=[pl.BlockSpec((tm, tk), lambda l, t: (t, l)), pl.BlockSpec((tm, N), lambda l, t: (t, 0))],
        out_specs=pl.BlockSpec((tk, N), lambda l, t: (l, 0)),
        out_shape=jax.ShapeDtypeStruct((Kl, N), BF16),
        scratch_shapes=[pltpu.VMEM((tk, N), F32)],
        compiler_params=_cp("parallel", "arbitrary"))(lhs, rhs)


def _swiglu_fwd(gu, name, tm=256):
    T = gu.shape[0]

    def body(gu_ref, o_ref):
        gate = gu_ref[:, :D_FF_PAD].astype(F32)
        up = gu_ref[:, D_FF_PAD:].astype(F32)
        o_ref[...] = (jax.nn.silu(gate) * up).astype(BF16)

    return pl.pallas_call(
        body, grid=(T // tm,), name=name,
        in_specs=[pl.BlockSpec((tm, 2 * D_FF_PAD), lambda i: (i, 0))],
        out_specs=pl.BlockSpec((tm, D_FF_PAD), lambda i: (i, 0)),
        out_shape=jax.ShapeDtypeStruct((T, D_FF_PAD), BF16),
        compiler_params=_cp("parallel"))(gu)


def _swiglu_bwd(gu, dact, name, tm=256):
    T = gu.shape[0]

    def body(gu_ref, d_ref, o_ref):
        gate = gu_ref[:, :D_FF_PAD].astype(F32)
        up = gu_ref[:, D_FF_PAD:].astype(F32)
        d = d_ref[...].astype(F32)
        sig = jax.nn.sigmoid(gate)
        silu = gate * sig
        o_ref[:, :D_FF_PAD] = (d * up * (sig + silu * (1.0 - sig))).astype(BF16)
        o_ref[:, D_FF_PAD:] = (d * silu).astype(BF16)

    return pl.pallas_call(
        body, grid=(T // tm,), name=name,
        in_specs=[pl.BlockSpec((tm, 2 * D_FF_PAD), lambda i: (i, 0)), pl.BlockSpec((tm, D_FF_PAD), lambda i: (i, 0))],
        out_specs=pl.BlockSpec((tm, 2 * D_FF_PAD), lambda i: (i, 0)),
        out_shape=jax.ShapeDtypeStruct((T, 2 * D_FF_PAD), BF16),
        compiler_params=_cp("parallel"))(gu, dact)


def _rms_bwd_add(x, g, dh, dres, name, tm=512):
    T, K = x.shape

    def body(x_ref, g_ref, dh_ref, dr_ref, dx_ref, dg_ref):
        _, vjp = jax.vjp(_rms, x_ref[...], g_ref[...])
        dx, dg = vjp(dh_ref[...])
        dx_ref[...] = dr_ref[...] + dx

        @pl.when(pl.program_id(0) == 0)
        def _():
            dg_ref[...] = dg

        @pl.when(pl.program_id(0) > 0)
        def _():
            dg_ref[...] += dg

    row = pl.BlockSpec((tm, K), lambda i: (i, 0))
    vec = pl.BlockSpec((1, K), lambda i: (0, 0))
    return pl.pallas_call(
        body, grid=(T // tm,), name=name,
        in_specs=[row, vec, row, row], out_specs=[row, vec],
        out_shape=[jax.ShapeDtypeStruct((T, K), F32), jax.ShapeDtypeStruct((1, K), F32)],
        compiler_params=_cp("arbitrary"))(x, g, dh, dres)


def _loss_head(x, g, target, name, tm=512):
    T, K = x.shape

    def loss_fn(xv, gv, tv):
        err = _rms(xv, gv) - tv
        return 0.5 * jnp.sum(jnp.mean(err * err, axis=-1, keepdims=True), axis=0, keepdims=True)

    def body(x_ref, g_ref, t_ref, l_ref, dx_ref, dg_ref):
        val, vjp = jax.vjp(lambda xv, gv: loss_fn(xv, gv, t_ref[...]), x_ref[...], g_ref[...])
        dx, dg = vjp(jnp.ones((1, 1), F32))
        dx_ref[...] = dx
        lval = jnp.broadcast_to(val, (1, 128))

        @pl.when(pl.program_id(0) == 0)
        def _():
            dg_ref[...] = dg
            l_ref[...] = lval

        @pl.when(pl.program_id(0) > 0)
        def _():
            dg_ref[...] += dg
            l_ref[...] += lval

    row = pl.BlockSpec((tm, K), lambda i: (i, 0))
    vec = pl.BlockSpec((1, K), lambda i: (0, 0))
    return pl.pallas_call(
        body, grid=(T // tm,), name=name,
        in_specs=[row, vec, row], out_specs=[pl.BlockSpec((1, 128), lambda i: (0, 0)), row, vec],
        out_shape=[jax.ShapeDtypeStruct((1, 128), F32), jax.ShapeDtypeStruct((T, K), F32),
                   jax.ShapeDtypeStruct((1, K), F32)],
        compiler_params=_cp("arbitrary"))(x, g, target)


def _sgu_fn(u, v, wm, bt, gain):
    ug = jax.nn.gelu(u)
    vg = jax.nn.gelu(v)
    row = lax.broadcasted_iota(jnp.int32, (BLOCK, BLOCK), 0)
    col = lax.broadcasted_iota(jnp.int32, (BLOCK, BLOCK), 1)
    tri = row >= col
    outs = []
    for h in range(GROUP_HEADS):
        vh = vg[:, h * HEAD_DIM:(h + 1) * HEAD_DIM]
        xc = vh - jnp.mean(vh, axis=-1, keepdims=True)
        vn = xc * lax.rsqrt(jnp.mean(xc * xc, axis=-1, keepdims=True) + EPS)
        outs.append(_bdot(jnp.where(tri, wm[h], 0.0), vn) + bt[:, h:h + 1])
    return _rms(ug * jnp.concatenate(outs, axis=1), gain)


def _sgu_fwd(proj, wm, bt, gain, name):
    T = proj.shape[0]

    def body(u_ref, v_ref, w_ref, b_ref, g_ref, o_ref):
        o_ref[...] = _sgu_fn(u_ref[...], v_ref[...], w_ref[...], b_ref[...], g_ref[...]).astype(BF16)

    full = lambda shape: pl.BlockSpec(shape, lambda i: (0,) * len(shape))
    return pl.pallas_call(
        body, grid=(T // BLOCK,), name=name,
        in_specs=[pl.BlockSpec((BLOCK, GROUP_WIDTH), lambda i: (i, 0)), pl.BlockSpec((BLOCK, GROUP_WIDTH), lambda i: (i, 1)),
                  full(wm.shape), full(bt.shape), full(gain.shape)],
        out_specs=pl.BlockSpec((BLOCK, GROUP_WIDTH), lambda i: (i, 0)),
        out_shape=jax.ShapeDtypeStruct((T, GROUP_WIDTH), BF16),
        compiler_params=_cp("parallel"))(proj, proj, wm, bt, gain)


def _acc_out(first, refs, vals):
    @pl.when(first)
    def _():
        for r, v in zip(refs, vals):
            r[...] = v

    @pl.when(jnp.logical_not(first))
    def _():
        for r, v in zip(refs, vals):
            r[...] += v


def _sgu_bwd(proj, wm, bt, gain, dy, name):
    T = proj.shape[0]

    def body(u_ref, v_ref, w_ref, b_ref, g_ref, dy_ref, duv_ref, dw_ref, db_ref, dg_ref):
        _, vjp = jax.vjp(_sgu_fn, u_ref[...], v_ref[...], w_ref[...], b_ref[...], g_ref[...])
        du, dv, dw, db, dg = vjp(dy_ref[...])
        duv_ref[:, :GROUP_WIDTH] = du.astype(BF16)
        duv_ref[:, GROUP_WIDTH:] = dv.astype(BF16)
        _acc_out(pl.program_id(0) == 0, (dw_ref, db_ref, dg_ref), (dw, db, dg))

    full = lambda shape: pl.BlockSpec(shape, lambda i: (0,) * len(shape))
    return pl.pallas_call(
        body, grid=(T // BLOCK,), name=name,
        in_specs=[pl.BlockSpec((BLOCK, GROUP_WIDTH), lambda i: (i, 0)), pl.BlockSpec((BLOCK, GROUP_WIDTH), lambda i: (i, 1)),
                  full(wm.shape), full(bt.shape), full(gain.shape),
                  pl.BlockSpec((BLOCK, GROUP_WIDTH), lambda i: (i, 0))],
        out_specs=[pl.BlockSpec((BLOCK, 2 * GROUP_WIDTH), lambda i: (i, 0)), full(wm.shape), full(bt.shape), full(gain.shape)],
        out_shape=[jax.ShapeDtypeStruct((T, 2 * GROUP_WIDTH), BF16), jax.ShapeDtypeStruct(wm.shape, F32),
                   jax.ShapeDtypeStruct(bt.shape, F32), jax.ShapeDtypeStruct(gain.shape, F32)],
        compiler_params=_cp("arbitrary"))(proj, proj, wm, bt, gain, dy)


def _pool_consts(seq):
    t = lax.broadcasted_iota(jnp.int32, (seq, GROUP_WIDTH), 0)
    grp = lax.broadcasted_iota(jnp.int32, (seq, GROUP_WIDTH), 1) // (GROUP_WIDTH // len(POOL_WINDOWS))
    win = jnp.where(grp == 0, POOL_WINDOWS[0], jnp.where(grp == 1, POOL_WINDOWS[1],
                    jnp.where(grp == 2, POOL_WINDOWS[2], POOL_WINDOWS[3])))
    count = jnp.minimum(t + 1, win).astype(F32)
    return t, grp, count


def _by_group(grp, vals):
    return jnp.where(grp == 0, vals[0], jnp.where(grp == 1, vals[1], jnp.where(grp == 2, vals[2], vals[3])))


def _window_sums(x, t, seq, back):
    def shift(a, k):
        if back:
            return jnp.where(t >= k, pltpu.roll(a, k, 0), 0.0)
        return jnp.where(t < seq - k, pltpu.roll(a, seq - k, 0), 0.0)
    sums = []
    a, k = x, 1
    for _ in POOL_WINDOWS:
        a = a + shift(a, k)
        sums.append(a)
        k *= 2
    return sums


def _pool_tail(y, wbd, scale, gain):
    return _rms(_bdot(y, wbd) * scale, gain)


def _pool_fwd(proj, wbd, scale, gain, seq, name):
    T = proj.shape[0]

    def body(p_ref, w_ref, s_ref, g_ref, o_ref):
        p = p_ref[...]
        t, grp, count = _pool_consts(seq)
        y = _by_group(grp, _window_sums(p, t, seq, True)) / count - p
        o_ref[...] = _pool_tail(y, w_ref[...], s_ref[...], g_ref[...]).astype(BF16)

    full = lambda shape: pl.BlockSpec(shape, lambda b: (0,) * len(shape))
    return pl.pallas_call(
        body, grid=(T // seq,), name=name,
        in_specs=[pl.BlockSpec((seq, GROUP_WIDTH), lambda b: (b, 2)), full(wbd.shape), full(scale.shape), full(gain.shape)],
        out_specs=pl.BlockSpec((seq, GROUP_WIDTH), lambda b: (b, 0)),
        out_shape=jax.ShapeDtypeStruct((T, GROUP_WIDTH), BF16),
        compiler_params=_cp("parallel"))(proj, wbd, scale, gain)


def _pool_bwd(proj, wbd, scale, gain, dy, seq, name):
    T = proj.shape[0]

    def body(p_ref, w_ref, s_ref, g_ref, dy_ref, dp_ref, dw_ref, ds_ref, dg_ref):
        p = p_ref[...]
        t, grp, count = _pool_consts(seq)
        y = _by_group(grp, _window_sums(p, t, seq, True)) / count - p
        _, vjp = jax.vjp(_pool_tail, y, w_ref[...], s_ref[...], g_ref[...])
        d_y, dw, ds, dg = vjp(dy_ref[...])
        dp = _by_group(grp, _window_sums(d_y / count, t, seq, False)) - d_y
        dp_ref[...] = dp.astype(BF16)
        _acc_out(pl.program_id(0) == 0, (dw_ref, ds_ref, dg_ref), (dw, ds, dg))

    full = lambda shape: pl.BlockSpec(shape, lambda b: (0,) * len(shape))
    return pl.pallas_call(
        body, grid=(T // seq,), name=name,
        in_specs=[pl.BlockSpec((seq, GROUP_WIDTH), lambda b: (b, 2)), full(wbd.shape), full(scale.shape), full(gain.shape),
                  pl.BlockSpec((seq, GROUP_WIDTH), lambda b: (b, 1))],
        out_specs=[pl.BlockSpec((seq, GROUP_WIDTH), lambda b: (b, 0)), full(wbd.shape), full(scale.shape), full(gain.shape)],
        out_shape=[jax.ShapeDtypeStruct((T, GROUP_WIDTH), BF16), jax.ShapeDtypeStruct(wbd.shape, F32),
                   jax.ShapeDtypeStruct(scale.shape, F32), jax.ShapeDtypeStruct(gain.shape, F32)],
        compiler_params=_cp("arbitrary"))(proj, wbd, scale, gain, dy)


def _swa_fn(q, kv_prev, kv_cur, sinks, tab, gain, first):
    half = GROUP_WIDTH // 2
    k2 = jnp.concatenate([kv_prev[:, :half], kv_cur[:, :half]], axis=0)
    v2 = jnp.concatenate([kv_prev[:, half:], kv_cur[:, half:]], axis=0)
    qi = lax.broadcasted_iota(jnp.int32, (BLOCK, 2 * BLOCK), 0)
    kj = lax.broadcasted_iota(jnp.int32, (BLOCK, 2 * BLOCK), 1)
    dist = qi + BLOCK - kj
    mask = (dist >= 0) & (dist < BLOCK) & ((kj >= BLOCK) | jnp.logical_not(first))
    outs = []
    for h in range(GROUP_HEADS):
        kvh = h // 2
        kh = k2[:, kvh * HEAD_DIM:(kvh + 1) * HEAD_DIM]
        vh = v2[:, kvh * HEAD_DIM:(kvh + 1) * HEAD_DIM]
        logits = _bdot_nt(q[:, h * HEAD_DIM:(h + 1) * HEAD_DIM], kh) * ATT_SCALE + tab[h]
        logits = jnp.where(mask, logits, -1e30)
        sink = sinks[:, h:h + 1]
        m = lax.stop_gradient(jnp.maximum(jnp.max(logits, axis=1, keepdims=True), sink))
        p = jnp.exp(logits - m)
        probs = p / (jnp.sum(p, axis=1, keepdims=True) + jnp.exp(sink - m))
        outs.append(_bdot(probs, vh))
    return _rms(jnp.concatenate(outs, axis=1), gain)


def _swa_specs(nblk):
    q = pl.BlockSpec((BLOCK, GROUP_WIDTH), lambda b, i: (b * nblk + i, 3))
    cur = pl.BlockSpec((BLOCK, GROUP_WIDTH), lambda b, i: (b * nblk + i, 4))
    prev = pl.BlockSpec((BLOCK, GROUP_WIDTH), lambda b, i: (b * nblk + jnp.maximum(i - 1, 0), 4))
    return q, prev, cur


def _swa_fwd(proj, sinks, tab, gain, seq, name):
    T = proj.shape[0]
    nblk = seq // BLOCK

    def body(q_ref, kp_ref, kc_ref, s_ref, t_ref, g_ref, o_ref):
        o_ref[...] = _swa_fn(q_ref[...], kp_ref[...], kc_ref[...], s_ref[...], t_ref[...], g_ref[...],
                             pl.program_id(1) == 0).astype(BF16)

    full = lambda shape: pl.BlockSpec(shape, lambda b, i: (0,) * len(shape))
    return pl.pallas_call(
        body, grid=(T // seq, nblk), name=name,
        in_specs=[*_swa_specs(nblk), full(sinks.shape), full(tab.shape), full(gain.shape)],
        out_specs=pl.BlockSpec((BLOCK, GROUP_WIDTH), lambda b, i: (b * nblk + i, 0)),
        out_shape=jax.ShapeDtypeStruct((T, GROUP_WIDTH), BF16),
        compiler_params=_cp("parallel", "parallel"))(proj, proj, proj, sinks, tab, gain)


def _swa_bwd(proj, sinks, tab, gain, dy, seq, name):
    T = proj.shape[0]
    nblk = seq // BLOCK

    def body(q_ref, kp_ref, kc_ref, s_ref, t_ref, g_ref, dy_ref, dq_ref, dkp_ref, dkc_ref, ds_ref, dt_ref, dg_ref):
        first = pl.program_id(1) == 0
        fn = functools.partial(_swa_fn, first=first)
        _, vjp = jax.vjp(fn, q_ref[...], kp_ref[...], kc_ref[...], s_ref[...], t_ref[...], g_ref[...])
        dq, dkp, dkc, ds, dt, dg = vjp(dy_ref[...])
        dq_ref[...] = dq.astype(BF16)
        dkp_ref[...] = dkp
        dkc_ref[...] = dkc
        _acc_out((pl.program_id(0) == 0) & first, (ds_ref, dt_ref, dg_ref), (ds, dt, dg))

    full = lambda shape: pl.BlockSpec(shape, lambda b, i: (0,) * len(shape))
    blk = lambda c: pl.BlockSpec((BLOCK, GROUP_WIDTH), lambda b, i: (b * nblk + i, c))
    return pl.pallas_call(
        body, grid=(T // seq, nblk), name=name,
        in_specs=[*_swa_specs(nblk), full(sinks.shape), full(tab.shape), full(gain.shape), blk(2)],
        out_specs=[blk(0), blk(0), blk(0), full(sinks.shape), full(tab.shape), full(gain.shape)],
        out_shape=[jax.ShapeDtypeStruct((T, GROUP_WIDTH), BF16), jax.ShapeDtypeStruct((T, GROUP_WIDTH), F32),
                   jax.ShapeDtypeStruct((T, GROUP_WIDTH), F32), jax.ShapeDtypeStruct(sinks.shape, F32),
                   jax.ShapeDtypeStruct(tab.shape, F32), jax.ShapeDtypeStruct(gain.shape, F32)],
        compiler_params=_cp("arbitrary", "arbitrary"))(proj, proj, proj, sinks, tab, gain, dy)


def _shift_add(cur, prev, seq, name):
    T = cur.shape[0]
    nblk = seq // BLOCK

    def body(c_ref, p_ref, o_ref):
        last = pl.program_id(1) == nblk - 1
        o_ref[...] = (c_ref[...] + jnp.where(last, 0.0, p_ref[...])).astype(BF16)

    return pl.pallas_call(
        body, grid=(T // seq, nblk), name=name,
        in_specs=[pl.BlockSpec((BLOCK, GROUP_WIDTH), lambda b, i: (b * nblk + i, 0)),
                  pl.BlockSpec((BLOCK, GROUP_WIDTH), lambda b, i: (b * nblk + jnp.minimum(i + 1, nblk - 1), 0))],
        out_specs=pl.BlockSpec((BLOCK, GROUP_WIDTH), lambda b, i: (b * nblk + i, 0)),
        out_shape=jax.ShapeDtypeStruct((T, GROUP_WIDTH), BF16),
        compiler_params=_cp("parallel", "parallel"))(cur, prev)


def _t5_bucket(dist):
    max_exact = N_BUCKETS // 2
    df = jnp.maximum(dist, 1).astype(F32)
    large = max_exact + (jnp.log(df / max_exact) / jnp.log(jnp.float32(MAX_DISTANCE / max_exact))
                         * (N_BUCKETS - max_exact)).astype(jnp.int32)
    return jnp.where(dist < max_exact, dist, jnp.minimum(large, N_BUCKETS - 1))


def _bucket_map():
    dist = (jnp.arange(BLOCK)[:, None] + BLOCK) - jnp.arange(2 * BLOCK)[None, :]
    return _t5_bucket(jnp.clip(dist, 0, BLOCK - 1)).astype(jnp.int32)


def _bias_table(rel_bias, buckets, name):
    def body(rb_ref, bk_ref, o_ref):
        bk = bk_ref[...]
        rb = rb_ref[...]
        for h in range(GROUP_HEADS):
            acc = jnp.zeros((BLOCK, 2 * BLOCK), F32)
            for b in range(N_BUCKETS):
                acc = jnp.where(bk == b, rb[b:b + 1, h:h + 1], acc)
            o_ref[h] = acc

    return pl.pallas_call(body, name=name, out_shape=jax.ShapeDtypeStruct((GROUP_HEADS, BLOCK, 2 * BLOCK), F32),
                          compiler_params=_cp())(rel_bias, buckets)


def _bias_table_bwd(dtab, buckets, name):
    def body(dt_ref, bk_ref, o_ref):
        bk = bk_ref[...]
        row = lax.broadcasted_iota(jnp.int32, (N_BUCKETS, GROUP_HEADS), 0)
        col = lax.broadcasted_iota(jnp.int32, (N_BUCKETS, GROUP_HEADS), 1)
        acc = jnp.zeros((N_BUCKETS, GROUP_HEADS), F32)
        for h in range(GROUP_HEADS):
            dt = dt_ref[h]
            for b in range(N_BUCKETS):
                s = jnp.sum(jnp.where(bk == b, dt, 0.0), keepdims=True)
                acc = acc + jnp.where((row == b) & (col == h), s, 0.0)
        o_ref[...] = acc

    return pl.pallas_call(body, name=name, out_shape=jax.ShapeDtypeStruct((N_BUCKETS, GROUP_HEADS), F32),
                          compiler_params=_cp())(dtab, buckets)


def _sb_tile(qh, kh, q0, k0):
    z = lax.dot_general(qh, kh, _NT, preferred_element_type=F32)
    row = lax.broadcasted_iota(jnp.int32, (BLOCK, BLOCK), 0)
    col = lax.broadcasted_iota(jnp.int32, (BLOCK, BLOCK), 1)
    causal = (k0 + col) < (q0 + row)
    ls_neg = jax.nn.log_sigmoid(-z)
    return jnp.where(causal, ls_neg, 0.0), ls_neg + z, causal


def _tri(strict_upper_src):
    r = lax.broadcasted_iota(jnp.int32, (BLOCK, BLOCK), 0)
    c = lax.broadcasted_iota(jnp.int32, (BLOCK, BLOCK), 1)
    cond = {"gt": r > c, "le": r <= c, "lt": r < c}[strict_upper_src]
    return jnp.where(cond, 1.0, 0.0).astype(BF16)


def _sb_fwd(proj, gain, seq, name):
    T = proj.shape[0]
    nblk = seq // BLOCK

    def body(q_ref, k_ref, v_ref, g_ref, o_ref, raw_ref, bt_ref):
        i = pl.program_id(1)
        q = q_ref[...]
        u_gt = _tri("gt")
        lane = lax.broadcasted_iota(jnp.int32, (BLOCK, BLOCK), 1)
        outs = []
        btot = jnp.zeros((BLOCK, BLOCK), F32)
        for h in range(GROUP_HEADS):
            hs = slice(h * HEAD_DIM, (h + 1) * HEAD_DIM)
            qh = (q[:, hs] * ATT_SCALE).astype(BF16)

            def step(jj, carry, hs=hs, qh=qh):
                acc, cb = carry
                j = i - jj
                ks = pl.multiple_of(j * BLOCK, BLOCK)
                kh = k_ref[pl.ds(ks, BLOCK), hs].astype(BF16)
                vh = v_ref[pl.ds(ks, BLOCK), hs].astype(BF16)
                b, a, causal = _sb_tile(qh, kh, i * BLOCK, j * BLOCK)
                tail = _split_dot(b, u_gt) + cb
                w = jnp.where(causal, jnp.exp(a + tail), 0.0)
                acc = acc + jnp.dot(w.astype(BF16), vh, preferred_element_type=F32)
                return acc, cb + jnp.sum(b, axis=1, keepdims=True)

            acc, cb = lax.fori_loop(0, i + 1, step, (jnp.zeros((BLOCK, HEAD_DIM), F32), jnp.zeros((BLOCK, 1), F32)))
            outs.append(acc)
            btot = jnp.where(lane == h, cb, btot)
        raw = jnp.concatenate(outs, axis=1)
        raw_ref[...] = raw
        bt_ref[...] = btot
        o_ref[...] = _rms(raw, g_ref[...]).astype(BF16)

    return pl.pallas_call(
        body, grid=(T // seq, nblk), name=name,
        in_specs=[pl.BlockSpec((BLOCK, GROUP_WIDTH), lambda b, i: (b * nblk + i, 5)),
                  pl.BlockSpec((seq, GROUP_WIDTH), lambda b, i: (b, 6)),
                  pl.BlockSpec((seq, GROUP_WIDTH), lambda b, i: (b, 7)),
                  pl.BlockSpec(gain.shape, lambda b, i: (0, 0))],
        out_specs=[pl.BlockSpec((BLOCK, GROUP_WIDTH), lambda b, i: (b * nblk + i, 0)),
                   pl.BlockSpec((BLOCK, GROUP_WIDTH), lambda b, i: (b * nblk + i, 0)),
                   pl.BlockSpec((BLOCK, BLOCK), lambda b, i: (b * nblk + i, 0))],
        out_shape=[jax.ShapeDtypeStruct((T, GROUP_WIDTH), BF16), jax.ShapeDtypeStruct((T, GROUP_WIDTH), F32),
                   jax.ShapeDtypeStruct((T, BLOCK), F32)],
        compiler_params=_cp("parallel", "parallel"))(proj, proj, proj, gain)


def _sb_bwd(proj, gain, raw, btot, dy, seq, name):
    T = proj.shape[0]
    nblk = seq // BLOCK

    def body(q_ref, k_ref, v_ref, g_ref, raw_ref, bt_ref, dy_ref, dq_ref, dk_ref, dv_ref, dg_ref):
        i = pl.program_id(1)

        @pl.when(i == 0)
        def _():
            dk_ref[...] = jnp.zeros_like(dk_ref)
            dv_ref[...] = jnp.zeros_like(dv_ref)

        rawv = raw_ref[...]
        _, vjp = jax.vjp(_rms, rawv, g_ref[...])
        do, dg = vjp(dy_ref[...])
        _acc_out((pl.program_id(0) == 0) & (i == 0), (dg_ref,), (dg,))
        q = q_ref[...]
        bt = bt_ref[...]
        u_le = _tri("le")
        u_lt = _tri("lt")
        dqs = []
        for h in range(GROUP_HEADS):
            hs = slice(h * HEAD_DIM, (h + 1) * HEAD_DIM)
            qh = (q[:, hs] * ATT_SCALE).astype(BF16)
            doh = do[:, hs].astype(BF16)
            bth = bt[:, h:h + 1]

            def step(j, carry, hs=hs, qh=qh, doh=doh, bth=bth):
                dq, cb, cg = carry
                ks = pl.multiple_of(j * BLOCK, BLOCK)
                kh = k_ref[pl.ds(ks, BLOCK), hs].astype(BF16)
                vh = v_ref[pl.ds(ks, BLOCK), hs].astype(BF16)
                b, a, causal = _sb_tile(qh, kh, i * BLOCK, j * BLOCK)
                tail = bth - (_split_dot(b, u_le) + cb)
                w = jnp.where(causal, jnp.exp(a + tail), 0.0)
                sig = jnp.exp(a)
                g = w * lax.dot_general(doh, vh, _NT, preferred_element_type=F32)
                gpre = _split_dot(g, u_lt) + cg
                dz = jnp.where(causal, g * (1.0 - sig) - gpre * sig, 0.0).astype(BF16)
                dq = dq + jnp.dot(dz, kh, preferred_element_type=F32)
                dk_ref[pl.ds(ks, BLOCK), hs] += lax.dot_general(dz, qh, _TN, preferred_element_type=F32)
                dv_ref[pl.ds(ks, BLOCK), hs] += lax.dot_general(w.astype(BF16), doh, _TN, preferred_element_type=F32)
                return dq, cb + jnp.sum(b, axis=1, keepdims=True), cg + jnp.sum(g, axis=1, keepdims=True)

            zero = jnp.zeros((BLOCK, 1), F32)
            dq, _, _ = lax.fori_loop(0, i + 1, step, (jnp.zeros((BLOCK, HEAD_DIM), F32), zero, zero))
            dqs.append(dq * ATT_SCALE)
        dq_ref[...] = jnp.concatenate(dqs, axis=1).astype(BF16)

    blk = lambda c: pl.BlockSpec((BLOCK, GROUP_WIDTH), lambda b, i: (b * nblk + i, c))
    seqblk = lambda c: pl.BlockSpec((seq, GROUP_WIDTH), lambda b, i: (b, c))
    vec = pl.BlockSpec(gain.shape, lambda b, i: (0, 0))
    return pl.pallas_call(
        body, grid=(T // seq, nblk), name=name,
        in_specs=[blk(5), seqblk(6), seqblk(7), vec, blk(0), pl.BlockSpec((BLOCK, BLOCK), lambda b, i: (b * nblk + i, 0)),
                  blk(3)],
        out_specs=[blk(0), seqblk(0), seqblk(0), vec],
        out_shape=[jax.ShapeDtypeStruct((T, GROUP_WIDTH), BF16), jax.ShapeDtypeStruct((T, GROUP_WIDTH), F32),
                   jax.ShapeDtypeStruct((T, GROUP_WIDTH), F32), jax.ShapeDtypeStruct(gain.shape, F32)],
        compiler_params=_cp("arbitrary", "arbitrary"))(proj, proj, proj, gain, raw, btot, dy)


def _layer_params(l, sgu_w, sgu_b, pool_w, pool_scale, swa_sinks, mix_out_gain, norm_mix, norm_ffn):
    gains = mix_out_gain[l].reshape(4, 1, GROUP_WIDTH)
    return dict(
        wm=sgu_w[l], bt=sgu_b[l].T,
        wbd=jax.scipy.linalg.block_diag(*[pool_w[l, g] for g in range(len(POOL_WINDOWS))]),
        scale=pool_scale[l][None], sinks=swa_sinks[l][None],
        gain=[gains[m] for m in range(4)], norm_mix=norm_mix[l][None], norm_ffn=norm_ffn[l][None])


def _layer_fwd(l, x, p, w, tab, seq):
    win, wout, wgu, wd = w
    proj, h1 = _norm_mm(x, p["norm_mix"], win, F32, f"in_proj_{l}")
    ya = _sgu_fwd(proj, p["wm"], p["bt"], p["gain"][0], f"sgu_fwd_{l}")
    yb = _pool_fwd(proj, p["wbd"], p["scale"], p["gain"][1], seq, f"pool_fwd_{l}")
    yc = _swa_fwd(proj, p["sinks"], tab, p["gain"][2], seq, f"swa_fwd_{l}")
    yd, raw, btot = _sb_fwd(proj, p["gain"][3], seq, f"sb_fwd_{l}")
    ycat = jnp.concatenate([ya, yb, yc, yd], axis=1)
    xm = _mm_res(x, ycat, wout, f"out_proj_{l}")
    gu, h2 = _norm_mm(xm, p["norm_ffn"], wgu, BF16, f"ffn_up_{l}")
    act = _swiglu_fwd(gu, f"swiglu_fwd_{l}")
    xo = _mm_res(xm, act, wd, f"ffn_down_{l}")
    return xo, (x, proj, h1, ycat, raw, btot, xm, gu, h2, act)


def _layer_bwd(l, dxo, saved, p, w, tab, seq):
    win, wout, wgu, wd = w
    x, proj, h1, ycat, raw, btot, xm, gu, h2, act = saved
    dact = _mm_nt(dxo, wd, BF16, f"ffn_down_dx_{l}")
    g_wd = _mm_tn_rows(act, dxo, FF_PAD, f"ffn_down_dw_{l}")
    dgu = _swiglu_bwd(gu, dact, f"swiglu_bwd_{l}")
    g_wgu = _mm_tn_cols(h2, dgu, FF_PAD, f"ffn_up_dw_{l}")
    dh2 = _mm_nt_acc(dgu, wgu, f"ffn_up_dx_{l}")
    dxm, g_norm_ffn = _rms_bwd_add(xm, p["norm_ffn"], dh2, dxo, f"norm_ffn_bwd_{l}")
    dycat = _mm_nt(dxm, wout, F32, f"out_proj_dx_{l}")
    g_wout = _mm_tn_rows(ycat, dxm, GROUP_WIDTH, f"out_proj_dw_{l}")
    duv, g_wm, g_bt, g_ga = _sgu_bwd(proj, p["wm"], p["bt"], p["gain"][0], dycat, f"sgu_bwd_{l}")
    dp, g_wbd, g_scale, g_gb = _pool_bwd(proj, p["wbd"], p["scale"], p["gain"][1], dycat, seq, f"pool_bwd_{l}")
    dq, dkp, dkc, g_sinks, g_tab, g_gc = _swa_bwd(proj, p["sinks"], tab, p["gain"][2], dycat, seq, f"swa_bwd_{l}")
    dkv = _shift_add(dkc, dkp, seq, f"swa_dkv_{l}")
    dqd, dkd, dvd, g_gd = _sb_bwd(proj, p["gain"][3], raw, btot, dycat, seq, f"sb_bwd_{l}")
    dproj = jnp.concatenate([duv, dp, dq, dkv, dqd, dkd.astype(BF16), dvd.astype(BF16)], axis=1)
    g_win = _mm_tn_cols(h1, dproj, GROUP_WIDTH, f"in_proj_dw_{l}")
    dh1 = _mm_nt_acc(dproj, win, f"in_proj_dx_{l}")
    dx, g_norm_mix = _rms_bwd_add(x, p["norm_mix"], dh1, dxm, f"norm_mix_bwd_{l}")
    ng = len(POOL_WINDOWS)
    gd = GROUP_WIDTH // ng
    small = dict(
        sgu_w=g_wm, sgu_b=g_bt.T,
        pool_w=jnp.stack([g_wbd[g * gd:(g + 1) * gd, g * gd:(g + 1) * gd] for g in range(ng)]),
        pool_scale=g_scale[0], swa_sinks=g_sinks[0],
        mix_out_gain=jnp.concatenate([g_ga[0], g_gb[0], g_gc[0], g_gd[0]]),
        norm_mix=g_norm_mix[0], norm_ffn=g_norm_ffn[0])
    return dx, (g_win, g_wout, g_wgu, g_wd), small, g_tab


def _local_step(x, target, weights, sgu_w, sgu_b, pool_w, pool_scale, swa_sinks, rel_bias, mix_out_gain,
                norm_mix, norm_ffn, norm_final, seq):
    buckets = _bucket_map()
    tab = _bias_table(rel_bias, buckets, "bias_table")
    params = [_layer_params(l, sgu_w, sgu_b, pool_w, pool_scale, swa_sinks, mix_out_gain, norm_mix, norm_ffn)
              for l in range(DEPTH)]
    saved = []
    for l in range(DEPTH):
        x, s = _layer_fwd(l, x, params[l], weights[l], tab, seq)
        saved.append(s)
    loss, dx, g_final = _loss_head(x, norm_final[None], target, "loss_head")
    big, small, g_tab = [None] * DEPTH, [None] * DEPTH, None
    for l in reversed(range(DEPTH)):
        dx, big[l], small[l], t = _layer_bwd(l, dx, saved[l], params[l], weights[l], tab, seq)
        g_tab = t if g_tab is None else g_tab + t
    g_small = {k: jnp.stack([small[l][k] for l in range(DEPTH)]) for k in small[0]}
    g_small["rel_bias"] = _bias_table_bwd(g_tab, buckets, "bias_table_bwd")
    g_small["norm_final"] = g_final[0]
    return loss, dx, big, g_small


def _cast_pad(w, cols, name):
    L, r, c = w.shape

    def body(w_ref, o_ref):
        if cols != c:
            o_ref[...] = jnp.zeros_like(o_ref)
        o_ref[:, :c] = w_ref[...].astype(BF16)

    return pl.pallas_call(
        body, grid=(L,), name=name,
        in_specs=[pl.BlockSpec((None, r, c), lambda l: (l, 0, 0))],
        out_specs=pl.BlockSpec((None, r, cols), lambda l: (l, 0, 0)),
        out_shape=jax.ShapeDtypeStruct((L, r, cols), BF16),
        compiler_params=_cp("parallel"))(w)


def _my_place():
    x, y, c = (lax.axis_index(a) for a in MESH_AXES)
    return x, y, c, 4 * x + 2 * y + c


ANY = pl.BlockSpec(memory_space=pl.ANY)
MESH = pl.DeviceIdType.MESH
DOWN_ROWS = D_FF // N_DEV


def _down_rows(ref, k):
    return ref.at[:, k // 2, pl.ds((k % 2) * DOWN_ROWS, DOWN_ROWS), :]


def _all_gather_weights(s_in, s_out, s_gu, s_down):
    L = s_in.shape[0]
    shards = (s_in, s_out, s_gu, s_down)
    down_full = jnp.zeros((L, 4, FF_PAD, D_MODEL), BF16)
    n = len(shards)

    def body(i0, i1, i2, i3, _, o0, o1, o2, o3, send_sems, recv_sems, local_sems):
        srcs = (i0, i1, i2, i3)
        outs = (o0, o1, o2, o3)
        x, y, c, _me = _my_place()
        chips = [(1 - x, y), (x, 1 - y), (1 - x, 1 - y)]

        def place(t, dev):
            k = 4 * dev[0] + 2 * dev[1] + dev[2]
            return _down_rows(outs[t], k) if t == 3 else outs[t].at[:, k]

        def copy(slot, t, block, to, from_shard):
            return pltpu.make_async_remote_copy(
                src_ref=srcs[t] if from_shard else place(t, block), dst_ref=place(t, block),
                send_sem=send_sems.at[slot, t], recv_sem=recv_sems.at[slot, t], device_id=to, device_id_type=MESH)

        me, sibling = (x, y, c), (x, y, 1 - c)
        mine = [pltpu.make_async_copy(srcs[t], place(t, me), local_sems.at[t]) for t in range(n)]
        for cp in mine:
            cp.start()
        first = [copy(0, t, me, sibling, True) for t in range(n)]
        first += [copy(1 + j, t, me, (*chip, c), True) for j, chip in enumerate(chips) for t in range(n)]
        for cp in first:
            cp.start()
        passed = []
        for j, chip in enumerate(chips):
            for t in range(n):
                copy(1 + j, t, (*chip, c), me, False).wait_recv()
                cp = copy(4 + j, t, (*chip, c), sibling, False)
                cp.start()
                passed.append(cp)
        for t in range(n):
            copy(0, t, sibling, me, False).wait_recv()
        for j, chip in enumerate(chips):
            for t in range(n):
                copy(4 + j, t, (*chip, 1 - c), me, False).wait_recv()
        for cp in first + passed:
            cp.wait_send()
        for cp in mine:
            cp.wait()

    shapes = [jax.ShapeDtypeStruct((L, N_DEV) + s.shape[1:], BF16) for s in shards[:3]]
    shapes.append(jax.ShapeDtypeStruct(down_full.shape, BF16))
    return pl.pallas_call(
        body, name="all_gather_weights", out_shape=shapes,
        in_specs=[ANY] * 5, out_specs=[ANY] * 4, input_output_aliases={4: 3},
        scratch_shapes=[pltpu.SemaphoreType.DMA((7, n)), pltpu.SemaphoreType.DMA((7, n)), pltpu.SemaphoreType.DMA((n,))],
        )(*shards, down_full)


def _peer(x, y, c, d):
    return (x ^ (d >> 2), y ^ ((d >> 1) & 1), c ^ (d & 1))


def _reduce_scatter_grads(big):
    L = len(big)
    n = 4
    out_rows = D_MODEL // N_DEV

    def body(*refs):
        ins = refs[:L * n]
        outs = refs[L * n:L * n + n]
        send_sems, recv_sems, local_sems = refs[L * n + n:]
        x, y, c, me = _my_place()

        def block(l, t, k):
            g = ins[l * n + t]
            if t == 1:
                return g.at[pl.ds(k * out_rows, out_rows), :]
            if t == 3:
                return g.at[pl.ds((k // 2) * FF_PAD + (k % 2) * DOWN_ROWS, DOWN_ROWS), :]
            return g.at[k]

        local = [pltpu.make_async_copy(block(l, t, me), outs[t].at[l, me], local_sems.at[l, t])
                 for l in range(L) for t in range(n)]
        for cp in local:
            cp.start()
        sends = []
        for d in range(1, N_DEV):
            px, py, pc = _peer(x, y, c, d)
            pk = 4 * px + 2 * py + pc
            for l in range(L):
                for t in range(n):
                    sends.append(pltpu.make_async_remote_copy(
                        src_ref=block(l, t, pk), dst_ref=outs[t].at[l, me],
                        send_sem=send_sems.at[d - 1, l, t], recv_sem=recv_sems.at[d - 1, l, t],
                        device_id=(px, py, pc), device_id_type=MESH))
        for cp in sends:
            cp.start()
        for d in range(1, N_DEV):
            px, py, pc = _peer(x, y, c, d)
            pk = 4 * px + 2 * py + pc
            for l in range(L):
                for t in range(n):
                    pltpu.make_async_remote_copy(
                        src_ref=block(l, t, me), dst_ref=outs[t].at[l, pk],
                        send_sem=send_sems.at[d - 1, l, t], recv_sem=recv_sems.at[d - 1, l, t],
                        device_id=(px, py, pc), device_id_type=MESH).wait_recv()
        for cp in sends:
            cp.wait_send()
        for cp in local:
            cp.wait()

    shapes = [jax.ShapeDtypeStruct((L, N_DEV, D_MODEL, GROUP_WIDTH), BF16),
              jax.ShapeDtypeStruct((L, N_DEV, out_rows, D_MODEL), BF16),
              jax.ShapeDtypeStruct((L, N_DEV, D_MODEL, FF_PAD), BF16),
              jax.ShapeDtypeStruct((L, N_DEV, DOWN_ROWS, D_MODEL), BF16)]
    flat = [g for layer in big for g in layer]
    return pl.pallas_call(
        body, name="reduce_scatter_grads", out_shape=shapes,
        in_specs=[ANY] * len(flat), out_specs=[ANY] * n,
        scratch_shapes=[pltpu.SemaphoreType.DMA((N_DEV - 1, L, n)), pltpu.SemaphoreType.DMA((N_DEV - 1, L, n)),
                        pltpu.SemaphoreType.DMA((L, n))],
        )(*flat)


def _all_reduce_small(part):
    rows = part.shape[0]

    def body(p_ref, o_ref, buf, send_sems, recv_sems):
        x, y, c, me = _my_place()
        buf[me] = p_ref[...]
        sends = []
        for d in range(1, N_DEV):
            peer = _peer(x, y, c, d)
            sends.append(pltpu.make_async_remote_copy(
                src_ref=p_ref, dst_ref=buf.at[me], send_sem=send_sems.at[d - 1], recv_sem=recv_sems.at[d - 1],
                device_id=peer, device_id_type=MESH))
        for cp in sends:
            cp.start()
        for d in range(1, N_DEV):
            px, py, pc = _peer(x, y, c, d)
            pltpu.make_async_remote_copy(
                src_ref=p_ref, dst_ref=buf.at[4 * px + 2 * py + pc], send_sem=send_sems.at[d - 1],
                recv_sem=recv_sems.at[d - 1], device_id=(px, py, pc), device_id_type=MESH).wait_recv()
        for cp in sends:
            cp.wait_send()
        total = buf[0]
        for k in range(1, N_DEV):
            total = total + buf[k]
        o_ref[...] = total

    return pl.pallas_call(
        body, name="all_reduce_small", out_shape=jax.ShapeDtypeStruct((rows, 128), F32),
        in_specs=[pl.BlockSpec(memory_space=pltpu.VMEM)], out_specs=pl.BlockSpec(memory_space=pltpu.VMEM),
        scratch_shapes=[pltpu.VMEM((N_DEV, rows, 128), F32), pltpu.SemaphoreType.DMA((N_DEV - 1,)),
                        pltpu.SemaphoreType.DMA((N_DEV - 1,))],
        compiler_params=pltpu.CompilerParams(vmem_limit_bytes=VMEM_LIMIT))(part)


def _adamw(w, g, m, v):
    m = ADAM_B1 * m + (1.0 - ADAM_B1) * g
    v = ADAM_B2 * v + (1.0 - ADAM_B2) * jnp.square(g)
    m_hat = m / (1.0 - ADAM_B1 ** ADAM_STEP)
    v_hat = v / (1.0 - ADAM_B2 ** ADAM_STEP)
    delta = -ADAM_LR * (m_hat / (jnp.sqrt(v_hat) + ADAM_EPS) + ADAM_WD * w)
    return delta, m, v


def _adamw_sharded(parts, w, m, v, tr, name):
    L, r, c = w.shape
    cp = parts.shape[-1]

    def body(p_ref, w_ref, m_ref, v_ref, g_ref, d_ref, nm_ref, nv_ref):
        g = p_ref[0, :, :c].astype(F32)
        for k in range(1, N_DEV):
            g = g + p_ref[k, :, :c].astype(F32)
        delta, nm, nv = _adamw(w_ref[...], g, m_ref[...], v_ref[...])
        g_ref[...] = g
        d_ref[...] = delta
        nm_ref[...] = nm
        nv_ref[...] = nv

    blk = pl.BlockSpec((None, tr, c), lambda l, i: (l, i, 0))
    out = jax.ShapeDtypeStruct((L, r, c), F32)
    return pl.pallas_call(
        body, grid=(L, r // tr), name=name,
        in_specs=[pl.BlockSpec((None, N_DEV, tr, cp), lambda l, i: (l, 0, i, 0)), blk, blk, blk],
        out_specs=[blk] * 4, out_shape=[out] * 4,
        compiler_params=_cp("parallel", "parallel"))(parts, w, m, v)


def _adamw_small(g, w, m, v, name):
    def body(g_ref, w_ref, m_ref, v_ref, d_ref, nm_ref, nv_ref):
        delta, nm, nv = _adamw(w_ref[...], g_ref[...], m_ref[...], v_ref[...])
        d_ref[...] = delta
        nm_ref[...] = nm
        nv_ref[...] = nv

    out = jax.ShapeDtypeStruct(g.shape, F32)
    return pl.pallas_call(body, name=name, out_shape=[out] * 3, compiler_params=_cp())(g, w, m, v)


SMALL = ("sgu_w", "sgu_b", "pool_w", "pool_scale", "swa_sinks", "rel_bias", "mix_out_gain", "norm_mix", "norm_ffn",
         "norm_final")


def _pack(parts, rows):
    flat = jnp.concatenate([p.reshape(-1) for p in parts])
    return jnp.pad(flat, (0, rows * 128 - flat.shape[0])).reshape(rows, 128)


def _unpack(buf, like):
    flat = buf.reshape(-1)
    out, at = [], 0
    for a in like:
        out.append(flat[at:at + a.size].reshape(a.shape))
        at += a.size
    return out


def kernel(x, w_in, w_out, sgu_w, sgu_b, pool_w, pool_scale, swa_sinks, rel_bias, mix_out_gain, norm_mix, norm_ffn, w_gate_up, w_down, norm_final, loss_target, m_w_in, m_w_out, m_sgu_w, m_sgu_b, m_pool_w, m_pool_scale, m_swa_sinks, m_rel_bias, m_mix_out_gain, m_norm_mix, m_norm_ffn, m_w_gate_up, m_w_down, m_norm_final, v_w_in, v_w_out, v_sgu_w, v_sgu_b, v_pool_w, v_pool_scale, v_swa_sinks, v_rel_bias, v_mix_out_gain, v_norm_mix, v_norm_ffn, v_w_gate_up, v_w_down, v_norm_final):
    bl, seq, _ = x.shape
    L = w_in.shape[0]
    full_in, full_out, full_gu, full_down = _all_gather_weights(
        _cast_pad(w_in, GROUP_WIDTH, "shard_w_in"), _cast_pad(w_out, D_MODEL, "shard_w_out"),
        _cast_pad(w_gate_up, FF_PAD, "shard_w_gate_up"), _cast_pad(w_down, D_MODEL, "shard_w_down"))
    weights = [(full_in[l], full_out[l].reshape(D_MODEL, D_MODEL), full_gu[l], full_down[l].reshape(D_FF_PAD, D_MODEL))
               for l in range(L)]
    small_w = dict(sgu_w=sgu_w, sgu_b=sgu_b, pool_w=pool_w, pool_scale=pool_scale, swa_sinks=swa_sinks,
                   rel_bias=rel_bias, mix_out_gain=mix_out_gain, norm_mix=norm_mix, norm_ffn=norm_ffn,
                   norm_final=norm_final)
    small_m = dict(sgu_w=m_sgu_w, sgu_b=m_sgu_b, pool_w=m_pool_w, pool_scale=m_pool_scale, swa_sinks=m_swa_sinks,
                   rel_bias=m_rel_bias, mix_out_gain=m_mix_out_gain, norm_mix=m_norm_mix, norm_ffn=m_norm_ffn,
                   norm_final=m_norm_final)
    small_v = dict(sgu_w=v_sgu_w, sgu_b=v_sgu_b, pool_w=v_pool_w, pool_scale=v_pool_scale, swa_sinks=v_swa_sinks,
                   rel_bias=v_rel_bias, mix_out_gain=v_mix_out_gain, norm_mix=v_norm_mix, norm_ffn=v_norm_ffn,
                   norm_final=v_norm_final)
    loss, dx, big, g_small = _local_step(
        x.reshape(bl * seq, D_MODEL), loss_target.reshape(bl * seq, D_MODEL), weights, sgu_w, sgu_b, pool_w, pool_scale,
        swa_sinks, rel_bias, mix_out_gain, norm_mix, norm_ffn, norm_final, seq)
    p_in, p_out, p_gu, p_down = _reduce_scatter_grads(big)
    outs_in = _adamw_sharded(p_in, w_in, m_w_in, v_w_in, 256, "adamw_w_in")
    outs_out = _adamw_sharded(p_out, w_out, m_w_out, v_w_out, D_MODEL // N_DEV, "adamw_w_out")
    outs_gu = _adamw_sharded(p_gu, w_gate_up, m_w_gate_up, v_w_gate_up, 256, "adamw_w_gate_up")
    outs_down = _adamw_sharded(p_down, w_down, m_w_down, v_w_down, DOWN_ROWS // 2, "adamw_w_down")
    sizes = sum(small_w[k].size for k in SMALL) + 128
    rows = -(-sizes // 1024) * 8
    packed = _pack([g_small[k] for k in SMALL] + [loss[0]], rows)
    total = _all_reduce_small(packed)
    like = [small_w[k] for k in SMALL]
    d_s, m_s, v_s = _adamw_small(total, _pack(like, rows), _pack([small_m[k] for k in SMALL], rows),
                                 _pack([small_v[k] for k in SMALL], rows), "adamw_small")
    g_list = dict(zip(SMALL, _unpack(total, like)))
    d_list = dict(zip(SMALL, _unpack(d_s, like)))
    m_list = dict(zip(SMALL, _unpack(m_s, like)))
    v_list = dict(zip(SMALL, _unpack(v_s, like)))
    loss_total = total.reshape(-1)[sum(a.size for a in like)]
    big_outs = dict(w_in=outs_in, w_out=outs_out, w_gate_up=outs_gu, w_down=outs_down)
    order = ("w_in", "w_out", "sgu_w", "sgu_b", "pool_w", "pool_scale", "swa_sinks", "rel_bias", "mix_out_gain",
             "norm_mix", "norm_ffn", "w_gate_up", "w_down", "norm_final")
    result = [loss_total, dx.reshape(bl, seq, D_MODEL)]
    for which, small in enumerate((g_list, d_list, m_list, v_list)):
        for name in order:
            result.append(big_outs[name][which] if name in big_outs else small[name])
    return tuple(result)
```

```python
import functools

import jax
import jax.numpy as jnp
from jax import lax
from jax.experimental import pallas as pl
from jax.experimental.pallas import tpu as pltpu

F32 = jnp.float32
BF16 = jnp.bfloat16

N_DEV = 8
DEPTH = 4
D_MODEL = 1024
GROUP_WIDTH = 256
HEAD_DIM = 64
GROUP_HEADS = 4
BLOCK = 128
N_BUCKETS = 32
MAX_DISTANCE = 128
POOL_WINDOWS = (2, 4, 8, 16)
D_FF = 2816
FF_SHARD = D_FF // 4
FF_PAD = 768
D_FF_PAD = 4 * FF_PAD
EPS = 1e-6
ATT_SCALE = HEAD_DIM ** -0.5
ADAM_LR = 0.001
ADAM_B1 = 0.9
ADAM_B2 = 0.999
ADAM_EPS = 1e-08
ADAM_WD = 0.01
ADAM_STEP = 10
VMEM_LIMIT = 56 * 1024 * 1024
MESH_AXES = ("x", "y", "c")


def _cp(*sem):
    return pltpu.CompilerParams(dimension_semantics=sem or None, vmem_limit_bytes=VMEM_LIMIT)


_NT = (((1,), (1,)), ((), ()))
_TN = (((0,), (0,)), ((), ()))


@jax.custom_vjp
def _bdot(a, b):
    return jnp.dot(a.astype(BF16), b.astype(BF16), preferred_element_type=F32)


def _bdot_fwd(a, b):
    return _bdot(a, b), (a.astype(BF16), b.astype(BF16))


def _bdot_bwd(res, ct):
    a, b = res
    c = ct.astype(BF16)
    return (lax.dot_general(c, b, _NT, preferred_element_type=F32),
            lax.dot_general(a, c, _TN, preferred_element_type=F32))


_bdot.defvjp(_bdot_fwd, _bdot_bwd)


@jax.custom_vjp
def _bdot_nt(a, b):
    return lax.dot_general(a.astype(BF16), b.astype(BF16), _NT, preferred_element_type=F32)


def _bdot_nt_fwd(a, b):
    return _bdot_nt(a, b), (a.astype(BF16), b.astype(BF16))


def _bdot_nt_bwd(res, ct):
    a, b = res
    c = ct.astype(BF16)
    return (jnp.dot(c, b, preferred_element_type=F32),
            lax.dot_general(c, a, _TN, preferred_element_type=F32))


_bdot_nt.defvjp(_bdot_nt_fwd, _bdot_nt_bwd)


def _rms(x, g):
    return x * lax.rsqrt(jnp.mean(x * x, axis=-1, keepdims=True) + EPS) * g


def _split_dot(x, u):
    hi = x.astype(BF16)
    lo = (x - hi.astype(F32)).astype(BF16)
    return jnp.dot(hi, u, preferred_element_type=F32) + jnp.dot(lo, u, preferred_element_type=F32)


def _head_mask(h, shape):
    col = lax.broadcasted_iota(jnp.int32, shape, 1)
    return (col >= h * HEAD_DIM) & (col < (h + 1) * HEAD_DIM)


def _norm_mm(x, g, w, out_dtype, name, tm=512):
    T, K = x.shape
    nb, _, tn = w.shape

    def body(x_ref, g_ref, w_ref, o_ref, h_ref):
        @pl.when(pl.program_id(1) == 0)
        def _():
            h_ref[...] = _rms(x_ref[...], g_ref[...]).astype(BF16)
        o_ref[...] = jnp.dot(h_ref[...], w_ref[...], preferred_element_type=F32).astype(o_ref.dtype)

    return pl.pallas_call(
        body, grid=(T // tm, nb), name=name,
        in_specs=[pl.BlockSpec((tm, K), lambda i, j: (i, 0)), pl.BlockSpec((1, K), lambda i, j: (0, 0)),
                  pl.BlockSpec((None, K, tn), lambda i, j: (j, 0, 0))],
        out_specs=[pl.BlockSpec((tm, tn), lambda i, j: (i, j)), pl.BlockSpec((tm, K), lambda i, j: (i, 0))],
        out_shape=[jax.ShapeDtypeStruct((T, nb * tn), out_dtype), jax.ShapeDtypeStruct((T, K), BF16)],
        compiler_params=_cp("parallel", "arbitrary"))(x, g, w)


def _mm_res(res, a, w, name, tm=512, tn=512):
    T, K = a.shape
    N = w.shape[1]

    def body(r_ref, a_ref, w_ref, o_ref):
        o_ref[...] = r_ref[...] + jnp.dot(a_ref[...], w_ref[...], preferred_element_type=F32)

    return pl.pallas_call(
        body, grid=(T // tm, N // tn), name=name,
        in_specs=[pl.BlockSpec((tm, tn), lambda i, j: (i, j)), pl.BlockSpec((tm, K), lambda i, j: (i, 0)),
                  pl.BlockSpec((K, tn), lambda i, j: (0, j))],
        out_specs=pl.BlockSpec((tm, tn), lambda i, j: (i, j)),
        out_shape=jax.ShapeDtypeStruct((T, N), F32),
        compiler_params=_cp("parallel", "parallel"))(res, a, w)


def _mm_nt(a, w, out_dtype, name, tm=512, tn=512):
    T, K = a.shape
    N = w.shape[0]

    def body(a_ref, w_ref, o_ref):
        o_ref[...] = lax.dot_general(a_ref[...].astype(BF16), w_ref[...], _NT,
                                     preferred_element_type=F32).astype(o_ref.dtype)

    return pl.pallas_call(
        body, grid=(T // tm, N // tn), name=name,
        in_specs=[pl.BlockSpec((tm, K), lambda i, j: (i, 0)), pl.BlockSpec((tn, K), lambda i, j: (j, 0))],
        out_specs=pl.BlockSpec((tm, tn), lambda i, j: (i, j)),
        out_shape=jax.ShapeDtypeStruct((T, N), out_dtype),
        compiler_params=_cp("parallel", "parallel"))(a, w)


def _mm_nt_acc(a, w, name, tm=512):
    T = a.shape[0]
    nb, K, tn = w.shape

    def body(a_ref, w_ref, o_ref):
        part = lax.dot_general(a_ref[...], w_ref[...], _NT, preferred_element_type=F32)

        @pl.when(pl.program_id(1) == 0)
        def _():
            o_ref[...] = part

        @pl.when(pl.program_id(1) > 0)
        def _():
            o_ref[...] += part

    return pl.pallas_call(
        body, grid=(T // tm, nb), name=name,
        in_specs=[pl.BlockSpec((tm, tn), lambda i, j: (i, j)), pl.BlockSpec((None, K, tn), lambda i, j: (j, 0, 0))],
        out_specs=pl.BlockSpec((tm, K), lambda i, j: (i, 0)),
        out_shape=jax.ShapeDtypeStruct((T, K), F32),
        compiler_params=_cp("parallel", "arbitrary"))(a, w)


def _mm_tn_cols(lhs, rhs, tn, name, tm=512):
    T, K = lhs.shape
    nb = rhs.shape[1] // tn
    nt = T // tm

    def body(l_ref, r_ref, o_ref, acc):
        part = lax.dot_general(l_ref[...], r_ref[...].astype(BF16), _TN, preferred_element_type=F32)

        @pl.when(pl.program_id(1) == 0)
        def _():
            acc[...] = part

        @pl.when(pl.program_id(1) > 0)
        def _():
            acc[...] += part

        @pl.when(pl.program_id(1) == nt - 1)
        def _():
            o_ref[...] = acc[...].astype(BF16)

    return pl.pallas_call(
        body, grid=(nb, nt), name=name,
        in_specs=[pl.BlockSpec((tm, K), lambda j, t: (t, 0)), pl.BlockSpec((tm, tn), lambda j, t: (t, j))],
        out_specs=pl.BlockSpec((None, K, tn), lambda j, t: (j, 0, 0)),
        out_shape=jax.ShapeDtypeStruct((nb, K, tn), BF16),
        scratch_shapes=[pltpu.VMEM((K, tn), F32)],
        compiler_params=_cp("parallel", "arbitrary"))(lhs, rhs)


def _mm_tn_rows(lhs, rhs, tk, name, tm=512):
    T, Kl = lhs.shape
    N = rhs.shape[1]
    nt = T // tm

    def body(l_ref, r_ref, o_ref, acc):
        part = lax.dot_general(l_ref[...], r_ref[...].astype(BF16), _TN, preferred_element_type=F32)

        @pl.when(pl.program_id(1) == 0)
        def _():
            acc[...] = part

        @pl.when(pl.program_id(1) > 0)
        def _():
            acc[...] += part

        @pl.when(pl.program_id(1) == nt - 1)
        def _():
            o_ref[...] = acc[...].astype(BF16)

    return pl.pallas_call(
        body, grid=(Kl // tk, nt), name=name,
        in_specs=[pl.BlockSpec((tm, tk), lambda l, t: (t, l)), pl.BlockSpec((tm, N), lambda l, t: (t, 0))],
        out_specs=pl.BlockSpec((tk, N), lambda l, t: (l, 0)),
        out_shape=jax.ShapeDtypeStruct((Kl, N), BF16),
        scratch_shapes=[pltpu.VMEM((tk, N), F32)],
        compiler_params=_cp("parallel", "arbitrary"))(lhs, rhs)


def _swiglu_fwd(gu, name, tm=256):
    T = gu.shape[0]

    def body(gu_ref, o_ref):
        gate = gu_ref[:, :D_FF_PAD].astype(F32)
        up = gu_ref[:, D_FF_PAD:].astype(F32)
        o_ref[...] = (jax.nn.silu(gate) * up).astype(BF16)

    return pl.pallas_call(
        body, grid=(T // tm,), name=name,
        in_specs=[pl.BlockSpec((tm, 2 * D_FF_PAD), lambda i: (i, 0))],
        out_specs=pl.BlockSpec((tm, D_FF_PAD), lambda i: (i, 0)),
        out_shape=jax.ShapeDtypeStruct((T, D_FF_PAD), BF16),
        compiler_params=_cp("parallel"))(gu)


def _swiglu_bwd(gu, dact, name, tm=256):
    T = gu.shape[0]

    def body(gu_ref, d_ref, o_ref):
        gate = gu_ref[:, :D_FF_PAD].astype(F32)
        up = gu_ref[:, D_FF_PAD:].astype(F32)
        d = d_ref[...].astype(F32)
        sig = jax.nn.sigmoid(gate)
        silu = gate * sig
        o_ref[:, :D_FF_PAD] = (d * up * (sig + silu * (1.0 - sig))).astype(BF16)
        o_ref[:, D_FF_PAD:] = (d * silu).astype(BF16)

    return pl.pallas_call(
        body, grid=(T // tm,), name=name,
        in_specs=[pl.BlockSpec((tm, 2 * D_FF_PAD), lambda i: (i, 0)), pl.BlockSpec((tm, D_FF_PAD), lambda i: (i, 0))],
        out_specs=pl.BlockSpec((tm, 2 * D_FF_PAD), lambda i: (i, 0)),
        out_shape=jax.ShapeDtypeStruct((T, 2 * D_FF_PAD), BF16),
        compiler_params=_cp("parallel"))(gu, dact)


def _rms_bwd_add(x, g, dh, dres, name, tm=512):
    T, K = x.shape

    def body(x_ref, g_ref, dh_ref, dr_ref, dx_ref, dg_ref):
        _, vjp = jax.vjp(_rms, x_ref[...], g_ref[...])
        dx, dg = vjp(dh_ref[...])
        dx_ref[...] = dr_ref[...] + dx

        @pl.when(pl.program_id(0) == 0)
        def _():
            dg_ref[...] = dg

        @pl.when(pl.program_id(0) > 0)
        def _():
            dg_ref[...] += dg

    row = pl.BlockSpec((tm, K), lambda i: (i, 0))
    vec = pl.BlockSpec((1, K), lambda i: (0, 0))
    return pl.pallas_call(
        body, grid=(T // tm,), name=name,
        in_specs=[row, vec, row, row], out_specs=[row, vec],
        out_shape=[jax.ShapeDtypeStruct((T, K), F32), jax.ShapeDtypeStruct((1, K), F32)],
        compiler_params=_cp("arbitrary"))(x, g, dh, dres)


def _loss_head(x, g, target, name, tm=512):
    T, K = x.shape

    def loss_fn(xv, gv, tv):
        err = _rms(xv, gv) - tv
        return 0.5 * jnp.sum(jnp.mean(err * err, axis=-1, keepdims=True), axis=0, keepdims=True)

    def body(x_ref, g_ref, t_ref, l_ref, dx_ref, dg_ref):
        val, vjp = jax.vjp(lambda xv, gv: loss_fn(xv, gv, t_ref[...]), x_ref[...], g_ref[...])
        dx, dg = vjp(jnp.ones((1, 1), F32))
        dx_ref[...] = dx
        lval = jnp.broadcast_to(val, (1, 128))

        @pl.when(pl.program_id(0) == 0)
        def _():
            dg_ref[...] = dg
            l_ref[...] = lval

        @pl.when(pl.program_id(0) > 0)
        def _():
            dg_ref[...] += dg
            l_ref[...] += lval

    row = pl.BlockSpec((tm, K), lambda i: (i, 0))
    vec = pl.BlockSpec((1, K), lambda i: (0, 0))
    return pl.pallas_call(
        body, grid=(T // tm,), name=name,
        in_specs=[row, vec, row], out_specs=[pl.BlockSpec((1, 128), lambda i: (0, 0)), row, vec],
        out_shape=[jax.ShapeDtypeStruct((1, 128), F32), jax.ShapeDtypeStruct((T, K), F32),
                   jax.ShapeDtypeStruct((1, K), F32)],
        compiler_params=_cp("arbitrary"))(x, g, target)


def _sgu_fn(u, v, wm, bt, gain):
    ug = jax.nn.gelu(u)
    vg = jax.nn.gelu(v)
    row = lax.broadcasted_iota(jnp.int32, (BLOCK, BLOCK), 0)
    col = lax.broadcasted_iota(jnp.int32, (BLOCK, BLOCK), 1)
    tri = row >= col
    outs = []
    for h in range(GROUP_HEADS):
        vh = vg[:, h * HEAD_DIM:(h + 1) * HEAD_DIM]
        xc = vh - jnp.mean(vh, axis=-1, keepdims=True)
        vn = xc * lax.rsqrt(jnp.mean(xc * xc, axis=-1, keepdims=True) + EPS)
        outs.append(_bdot(jnp.where(tri, wm[h], 0.0), vn) + bt[:, h:h + 1])
    return _rms(ug * jnp.concatenate(outs, axis=1), gain)


def _sgu_fwd(proj, wm, bt, gain, name):
    T = proj.shape[0]

    def body(u_ref, v_ref, w_ref, b_ref, g_ref, o_ref):
        o_ref[...] = _sgu_fn(u_ref[...], v_ref[...], w_ref[...], b_ref[...], g_ref[...]).astype(BF16)

    full = lambda shape: pl.BlockSpec(shape, lambda i: (0,) * len(shape))
    return pl.pallas_call(
        body, grid=(T // BLOCK,), name=name,
        in_specs=[pl.BlockSpec((BLOCK, GROUP_WIDTH), lambda i: (i, 0)), pl.BlockSpec((BLOCK, GROUP_WIDTH), lambda i: (i, 1)),
                  full(wm.shape), full(bt.shape), full(gain.shape)],
        out_specs=pl.BlockSpec((BLOCK, GROUP_WIDTH), lambda i: (i, 0)),
        out_shape=jax.ShapeDtypeStruct((T, GROUP_WIDTH), BF16),
        compiler_params=_cp("parallel"))(proj, proj, wm, bt, gain)


def _acc_out(first, refs, vals):
    @pl.when(first)
    def _():
        for r, v in zip(refs, vals):
            r[...] = v

    @pl.when(jnp.logical_not(first))
    def _():
        for r, v in zip(refs, vals):
            r[...] += v


def _sgu_bwd(proj, wm, bt, gain, dy, name):
    T = proj.shape[0]

    def body(u_ref, v_ref, w_ref, b_ref, g_ref, dy_ref, duv_ref, dw_ref, db_ref, dg_ref):
        _, vjp = jax.vjp(_sgu_fn, u_ref[...], v_ref[...], w_ref[...], b_ref[...], g_ref[...])
        du, dv, dw, db, dg = vjp(dy_ref[...])
        duv_ref[:, :GROUP_WIDTH] = du.astype(BF16)
        duv_ref[:, GROUP_WIDTH:] = dv.astype(BF16)
        _acc_out(pl.program_id(0) == 0, (dw_ref, db_ref, dg_ref), (dw, db, dg))

    full = lambda shape: pl.BlockSpec(shape, lambda i: (0,) * len(shape))
    return pl.pallas_call(
        body, grid=(T // BLOCK,), name=name,
        in_specs=[pl.BlockSpec((BLOCK, GROUP_WIDTH), lambda i: (i, 0)), pl.BlockSpec((BLOCK, GROUP_WIDTH), lambda i: (i, 1)),
                  full(wm.shape), full(bt.shape), full(gain.shape),
                  pl.BlockSpec((BLOCK, GROUP_WIDTH), lambda i: (i, 0))],
        out_specs=[pl.BlockSpec((BLOCK, 2 * GROUP_WIDTH), lambda i: (i, 0)), full(wm.shape), full(bt.shape), full(gain.shape)],
        out_shape=[jax.ShapeDtypeStruct((T, 2 * GROUP_WIDTH), BF16), jax.ShapeDtypeStruct(wm.shape, F32),
                   jax.ShapeDtypeStruct(bt.shape, F32), jax.ShapeDtypeStruct(gain.shape, F32)],
        compiler_params=_cp("arbitrary"))(proj, proj, wm, bt, gain, dy)


def _pool_consts(seq):
    t = lax.broadcasted_iota(jnp.int32, (seq, GROUP_WIDTH), 0)
    grp = lax.broadcasted_iota(jnp.int32, (seq, GROUP_WIDTH), 1) // (GROUP_WIDTH // len(POOL_WINDOWS))
    win = jnp.where(grp == 0, POOL_WINDOWS[0], jnp.where(grp == 1, POOL_WINDOWS[1],
                    jnp.where(grp == 2, POOL_WINDOWS[2], POOL_WINDOWS[3])))
    count = jnp.minimum(t + 1, win).astype(F32)
    return t, grp, count


def _by_group(grp, vals):
    return jnp.where(grp == 0, vals[0], jnp.where(grp == 1, vals[1], jnp.where(grp == 2, vals[2], vals[3])))


def _window_sums(x, t, seq, back):
    def shift(a, k):
        if back:
            return jnp.where(t >= k, pltpu.roll(a, k, 0), 0.0)
        return jnp.where(t < seq - k, pltpu.roll(a, seq - k, 0), 0.0)
    sums = []
    a, k = x, 1
    for _ in POOL_WINDOWS:
        a = a + shift(a, k)
        sums.append(a)
        k *= 2
    return sums


def _pool_tail(y, wbd, scale, gain):
    return _rms(_bdot(y, wbd) * scale, gain)


def _pool_fwd(proj, wbd, scale, gain, seq, name):
    T = proj.shape[0]

    def body(p_ref, w_ref, s_ref, g_ref, o_ref):
        p = p_ref[...]
        t, grp, count = _pool_consts(seq)
        y = _by_group(grp, _window_sums(p, t, seq, True)) / count - p
        o_ref[...] = _pool_tail(y, w_ref[...], s_ref[...], g_ref[...]).astype(BF16)

    full = lambda shape: pl.BlockSpec(shape, lambda b: (0,) * len(shape))
    return pl.pallas_call(
        body, grid=(T // seq,), name=name,
        in_specs=[pl.BlockSpec((seq, GROUP_WIDTH), lambda b: (b, 2)), full(wbd.shape), full(scale.shape), full(gain.shape)],
        out_specs=pl.BlockSpec((seq, GROUP_WIDTH), lambda b: (b, 0)),
        out_shape=jax.ShapeDtypeStruct((T, GROUP_WIDTH), BF16),
        compiler_params=_cp("parallel"))(proj, wbd, scale, gain)


def _pool_bwd(proj, wbd, scale, gain, dy, seq, name):
    T = proj.shape[0]

    def body(p_ref, w_ref, s_ref, g_ref, dy_ref, dp_ref, dw_ref, ds_ref, dg_ref):
        p = p_ref[...]
        t, grp, count = _pool_consts(seq)
        y = _by_group(grp, _window_sums(p, t, seq, True)) / count - p
        _, vjp = jax.vjp(_pool_tail, y, w_ref[...], s_ref[...], g_ref[...])
        d_y, dw, ds, dg = vjp(dy_ref[...])
        dp = _by_group(grp, _window_sums(d_y / count, t, seq, False)) - d_y
        dp_ref[...] = dp.astype(BF16)
        _acc_out(pl.program_id(0) == 0, (dw_ref, ds_ref, dg_ref), (dw, ds, dg))

    full = lambda shape: pl.BlockSpec(shape, lambda b: (0,) * len(shape))
    return pl.pallas_call(
        body, grid=(T // seq,), name=name,
        in_specs=[pl.BlockSpec((seq, GROUP_WIDTH), lambda b: (b, 2)), full(wbd.shape), full(scale.shape), full(gain.shape),
                  pl.BlockSpec((seq, GROUP_WIDTH), lambda b: (b, 1))],
        out_specs=[pl.BlockSpec((seq, GROUP_WIDTH), lambda b: (b, 0)), full(wbd.shape), full(scale.shape), full(gain.shape)],
        out_shape=[jax.ShapeDtypeStruct((T, GROUP_WIDTH), BF16), jax.ShapeDtypeStruct(wbd.shape, F32),
                   jax.ShapeDtypeStruct(scale.shape, F32), jax.ShapeDtypeStruct(gain.shape, F32)],
        compiler_params=_cp("arbitrary"))(proj, wbd, scale, gain, dy)


def _swa_fn(q, kv_prev, kv_cur, sinks, tab, gain, first):
    half = GROUP_WIDTH // 2
    k2 = jnp.concatenate([kv_prev[:, :half], kv_cur[:, :half]], axis=0)
    v2 = jnp.concatenate([kv_prev[:, half:], kv_cur[:, half:]], axis=0)
    qi = lax.broadcasted_iota(jnp.int32, (BLOCK, 2 * BLOCK), 0)
    kj = lax.broadcasted_iota(jnp.int32, (BLOCK, 2 * BLOCK), 1)
    dist = qi + BLOCK - kj
    mask = (dist >= 0) & (dist < BLOCK) & ((kj >= BLOCK) | jnp.logical_not(first))
    outs = []
    for h in range(GROUP_HEADS):
        kvh = h // 2
        kh = k2[:, kvh * HEAD_DIM:(kvh + 1) * HEAD_DIM]
        vh = v2[:, kvh * HEAD_DIM:(kvh + 1) * HEAD_DIM]
        logits = _bdot_nt(q[:, h * HEAD_DIM:(h + 1) * HEAD_DIM], kh) * ATT_SCALE + tab[h]
        logits = jnp.where(mask, logits, -1e30)
        sink = sinks[:, h:h + 1]
        m = lax.stop_gradient(jnp.maximum(jnp.max(logits, axis=1, keepdims=True), sink))
        p = jnp.exp(logits - m)
        probs = p / (jnp.sum(p, axis=1, keepdims=True) + jnp.exp(sink - m))
        outs.append(_bdot(probs, vh))
    return _rms(jnp.concatenate(outs, axis=1), gain)


def _swa_specs(nblk):
    q = pl.BlockSpec((BLOCK, GROUP_WIDTH), lambda b, i: (b * nblk + i, 3))
    cur = pl.BlockSpec((BLOCK, GROUP_WIDTH), lambda b, i: (b * nblk + i, 4))
    prev = pl.BlockSpec((BLOCK, GROUP_WIDTH), lambda b, i: (b * nblk + jnp.maximum(i - 1, 0), 4))
    return q, prev, cur


def _swa_fwd(proj, sinks, tab, gain, seq, name):
    T = proj.shape[0]
    nblk = seq // BLOCK

    def body(q_ref, kp_ref, kc_ref, s_ref, t_ref, g_ref, o_ref):
        o_ref[...] = _swa_fn(q_ref[...], kp_ref[...], kc_ref[...], s_ref[...], t_ref[...], g_ref[...],
                             pl.program_id(1) == 0).astype(BF16)

    full = lambda shape: pl.BlockSpec(shape, lambda b, i: (0,) * len(shape))
    return pl.pallas_call(
        body, grid=(T // seq, nblk), name=name,
        in_specs=[*_swa_specs(nblk), full(sinks.shape), full(tab.shape), full(gain.shape)],
        out_specs=pl.BlockSpec((BLOCK, GROUP_WIDTH), lambda b, i: (b * nblk + i, 0)),
        out_shape=jax.ShapeDtypeStruct((T, GROUP_WIDTH), BF16),
        compiler_params=_cp("parallel", "parallel"))(proj, proj, proj, sinks, tab, gain)


def _swa_bwd(proj, sinks, tab, gain, dy, seq, name):
    T = proj.shape[0]
    nblk = seq // BLOCK

    def body(q_ref, kp_ref, kc_ref, s_ref, t_ref, g_ref, dy_ref, dq_ref, dkp_ref, dkc_ref, ds_ref, dt_ref, dg_ref):
        first = pl.program_id(1) == 0
        fn = functools.partial(_swa_fn, first=first)
        _, vjp = jax.vjp(fn, q_ref[...], kp_ref[...], kc_ref[...], s_ref[...], t_ref[...], g_ref[...])
        dq, dkp, dkc, ds, dt, dg = vjp(dy_ref[...])
        dq_ref[...] = dq.astype(BF16)
        dkp_ref[...] = dkp
        dkc_ref[...] = dkc
        _acc_out((pl.program_id(0) == 0) & first, (ds_ref, dt_ref, dg_ref), (ds, dt, dg))

    full = lambda shape: pl.BlockSpec(shape, lambda b, i: (0,) * len(shape))
    blk = lambda c: pl.BlockSpec((BLOCK, GROUP_WIDTH), lambda b, i: (b * nblk + i, c))
    return pl.pallas_call(
        body, grid=(T // seq, nblk), name=name,
        in_specs=[*_swa_specs(nblk), full(sinks.shape), full(tab.shape), full(gain.shape), blk(2)],
        out_specs=[blk(0), blk(0), blk(0), full(sinks.shape), full(tab.shape), full(gain.shape)],
        out_shape=[jax.ShapeDtypeStruct((T, GROUP_WIDTH), BF16), jax.ShapeDtypeStruct((T, GROUP_WIDTH), F32),
                   jax.ShapeDtypeStruct((T, GROUP_WIDTH), F32), jax.ShapeDtypeStruct(sinks.shape, F32),
                   jax.ShapeDtypeStruct(tab.shape, F32), jax.ShapeDtypeStruct(gain.shape, F32)],
        compiler_params=_cp("arbitrary", "arbitrary"))(proj, proj, proj, sinks, tab, gain, dy)


def _shift_add(cur, prev, seq, name):
    T = cur.shape[0]
    nblk = seq // BLOCK

    def body(c_ref, p_ref, o_ref):
        last = pl.program_id(1) == nblk - 1
        o_ref[...] = (c_ref[...] + jnp.where(last, 0.0, p_ref[...])).astype(BF16)

    return pl.pallas_call(
        body, grid=(T // seq, nblk), name=name,
        in_specs=[pl.BlockSpec((BLOCK, GROUP_WIDTH), lambda b, i: (b * nblk + i, 0)),
                  pl.BlockSpec((BLOCK, GROUP_WIDTH), lambda b, i: (b * nblk + jnp.minimum(i + 1, nblk - 1), 0))],
        out_specs=pl.BlockSpec((BLOCK, GROUP_WIDTH), lambda b, i: (b * nblk + i, 0)),
        out_shape=jax.ShapeDtypeStruct((T, GROUP_WIDTH), BF16),
        compiler_params=_cp("parallel", "parallel"))(cur, prev)


def _t5_bucket(dist):
    max_exact = N_BUCKETS // 2
    df = jnp.maximum(dist, 1).astype(F32)
    large = max_exact + (jnp.log(df / max_exact) / jnp.log(jnp.float32(MAX_DISTANCE / max_exact))
                         * (N_BUCKETS - max_exact)).astype(jnp.int32)
    return jnp.where(dist < max_exact, dist, jnp.minimum(large, N_BUCKETS - 1))


def _bucket_map():
    dist = (jnp.arange(BLOCK)[:, None] + BLOCK) - jnp.arange(2 * BLOCK)[None, :]
    return _t5_bucket(jnp.clip(dist, 0, BLOCK - 1)).astype(jnp.int32)


def _bias_table(rel_bias, buckets, name):
    def body(rb_ref, bk_ref, o_ref):
        bk = bk_ref[...]
        rb = rb_ref[...]
        for h in range(GROUP_HEADS):
            acc = jnp.zeros((BLOCK, 2 * BLOCK), F32)
            for b in range(N_BUCKETS):
                acc = jnp.where(bk == b, rb[b:b + 1, h:h + 1], acc)
            o_ref[h] = acc

    return pl.pallas_call(body, name=name, out_shape=jax.ShapeDtypeStruct((GROUP_HEADS, BLOCK, 2 * BLOCK), F32),
                          compiler_params=_cp())(rel_bias, buckets)


def _bias_table_bwd(dtab, buckets, name):
    def body(dt_ref, bk_ref, o_ref):
        bk = bk_ref[...]
        row = lax.broadcasted_iota(jnp.int32, (N_BUCKETS, GROUP_HEADS), 0)
        col = lax.broadcasted_iota(jnp.int32, (N_BUCKETS, GROUP_HEADS), 1)
        acc = jnp.zeros((N_BUCKETS, GROUP_HEADS), F32)
        for h in range(GROUP_HEADS):
            dt = dt_ref[h]
            for b in range(N_BUCKETS):
                s = jnp.sum(jnp.where(bk == b, dt, 0.0), keepdims=True)
                acc = acc + jnp.where((row == b) & (col == h), s, 0.0)
        o_ref[...] = acc

    return pl.pallas_call(body, name=name, out_shape=jax.ShapeDtypeStruct((N_BUCKETS, GROUP_HEADS), F32),
                          compiler_params=_cp())(dtab, buckets)


HEAD_ROWS = GROUP_HEADS * BLOCK


def _stack_heads(x):
    return jnp.concatenate([jnp.where(_head_mask(h, x.shape), x, 0.0) for h in range(GROUP_HEADS)], axis=0).astype(BF16)


def _sb_tile(qs, kb, q0, k0):
    z = lax.dot_general(qs, kb, _NT, preferred_element_type=F32)
    row = lax.broadcasted_iota(jnp.int32, (HEAD_ROWS, BLOCK), 0) & (BLOCK - 1)
    col = lax.broadcasted_iota(jnp.int32, (HEAD_ROWS, BLOCK), 1)
    causal = (k0 + col) < (q0 + row)
    ls_neg = -(jnp.maximum(z, 0.0) + jnp.log(1.0 + jnp.exp(-jnp.abs(z))))
    return jnp.where(causal, ls_neg, 0.0), ls_neg + z, causal


SB_DEAD = -104.0
SB_FIRST_LANE = GROUP_HEADS


def _tri(strict_upper_src):
    r = lax.broadcasted_iota(jnp.int32, (BLOCK, BLOCK), 0)
    c = lax.broadcasted_iota(jnp.int32, (BLOCK, BLOCK), 1)
    cond = {"gt": r > c, "le": r <= c, "lt": r < c}[strict_upper_src]
    return jnp.where(cond, 1.0, 0.0).astype(BF16)


def _sb_fwd(proj, gain, seq, name):
    T = proj.shape[0]
    nblk = seq // BLOCK

    def body(q_ref, k_ref, v_ref, g_ref, o_ref, raw_ref, bt_ref):
        i = pl.program_id(1)
        q = q_ref[...]
        u_gt = _tri("gt")
        lane = lax.broadcasted_iota(jnp.int32, (BLOCK, BLOCK), 1)
        heads = [slice(h * HEAD_DIM, (h + 1) * HEAD_DIM) for h in range(GROUP_HEADS)]
        rows = [slice(h * BLOCK, (h + 1) * BLOCK) for h in range(GROUP_HEADS)]
        qs = _stack_heads(q * ATT_SCALE)

        def live(carry):
            j, _, cb = carry
            return (j >= 0) & (jnp.max(cb) > SB_DEAD)

        def step(carry):
            j, accs, cb = carry
            ks = pl.multiple_of(j * BLOCK, BLOCK)
            kb = k_ref[pl.ds(ks, BLOCK), :].astype(BF16)
            vb = v_ref[pl.ds(ks, BLOCK), :].astype(BF16)
            b, a, causal = _sb_tile(qs, kb, i * BLOCK, j * BLOCK)
            tail = _split_dot(b, u_gt) + cb
            w = jnp.where(causal, jnp.exp(a + tail), 0.0).astype(BF16)
            accs = tuple(accs[h] + jnp.dot(w[rows[h]], vb[:, hs], preferred_element_type=F32)
                         for h, hs in enumerate(heads))
            return j - 1, accs, cb + jnp.sum(b, axis=1, keepdims=True)

        zero_acc = tuple(jnp.zeros((BLOCK, HEAD_DIM), F32) for _ in heads)
        j_end, accs, cb = lax.while_loop(live, step, (i, zero_acc, jnp.zeros((HEAD_ROWS, 1), F32)))
        side = jnp.where(lane == SB_FIRST_LANE, (j_end + 1).astype(F32), 0.0)
        for h in range(GROUP_HEADS):
            side = jnp.where(lane == h, cb[rows[h]], side)
        raw = jnp.concatenate(accs, axis=1)
        raw_ref[...] = raw
        bt_ref[...] = side
        o_ref[...] = _rms(raw, g_ref[...]).astype(BF16)

    return pl.pallas_call(
        body, grid=(T // seq, nblk), name=name,
        in_specs=[pl.BlockSpec((BLOCK, GROUP_WIDTH), lambda b, i: (b * nblk + i, 5)),
                  pl.BlockSpec((seq, GROUP_WIDTH), lambda b, i: (b, 6)),
                  pl.BlockSpec((seq, GROUP_WIDTH), lambda b, i: (b, 7)),
                  pl.BlockSpec(gain.shape, lambda b, i: (0, 0))],
        out_specs=[pl.BlockSpec((BLOCK, GROUP_WIDTH), lambda b, i: (b * nblk + i, 0)),
                   pl.BlockSpec((BLOCK, GROUP_WIDTH), lambda b, i: (b * nblk + i, 0)),
                   pl.BlockSpec((BLOCK, BLOCK), lambda b, i: (b * nblk + i, 0))],
        out_shape=[jax.ShapeDtypeStruct((T, GROUP_WIDTH), BF16), jax.ShapeDtypeStruct((T, GROUP_WIDTH), F32),
                   jax.ShapeDtypeStruct((T, BLOCK), F32)],
        compiler_params=_cp("parallel", "parallel"))(proj, proj, proj, gain)


def _sb_bwd(proj, gain, raw, btot, dy, seq, name):
    T = proj.shape[0]
    nblk = seq // BLOCK

    def body(q_ref, k_ref, v_ref, g_ref, raw_ref, bt_ref, dy_ref, dq_ref, dk_ref, dv_ref, dg_ref):
        i = pl.program_id(1)

        @pl.when(i == 0)
        def _():
            dk_ref[...] = jnp.zeros_like(dk_ref)
            dv_ref[...] = jnp.zeros_like(dv_ref)

        rawv = raw_ref[...]
        _, vjp = jax.vjp(_rms, rawv, g_ref[...])
        do, dg = vjp(dy_ref[...])
        _acc_out((pl.program_id(0) == 0) & (i == 0), (dg_ref,), (dg,))
        q = q_ref[...]
        bt = bt_ref[...]
        u_le = _tri("le")
        u_lt = _tri("lt")
        heads = [slice(h * HEAD_DIM, (h + 1) * HEAD_DIM) for h in range(GROUP_HEADS)]
        rows = [slice(h * BLOCK, (h + 1) * BLOCK) for h in range(GROUP_HEADS)]
        qs = _stack_heads(q * ATT_SCALE)
        dos = _stack_heads(do)
        bts = jnp.concatenate([bt[:, h:h + 1] for h in range(GROUP_HEADS)], axis=0)
        first = jnp.max(bt[:, SB_FIRST_LANE:SB_FIRST_LANE + 1]).astype(jnp.int32)
        first = jnp.minimum(jnp.maximum(first, 0), i)

        def step(j, carry):
            dqs, cb, cg = carry
            ks = pl.multiple_of(j * BLOCK, BLOCK)
            kb = k_ref[pl.ds(ks, BLOCK), :].astype(BF16)
            vb = v_ref[pl.ds(ks, BLOCK), :].astype(BF16)
            b, a, causal = _sb_tile(qs, kb, i * BLOCK, j * BLOCK)
            tail = bts - (_split_dot(b, u_le) + cb)
            w = jnp.where(causal, jnp.exp(a + tail), 0.0)
            sig = jnp.exp(a)
            g = w * lax.dot_general(dos, vb, _NT, preferred_element_type=F32)
            gpre = _split_dot(g, u_lt) + cg
            dz = jnp.where(causal, g * (1.0 - sig) - gpre * sig, 0.0).astype(BF16)
            dqs = tuple(dqs[h] + jnp.dot(dz[rows[h]], kb[:, hs], preferred_element_type=F32)
                        for h, hs in enumerate(heads))
            dk_ref[pl.ds(ks, BLOCK), :] += lax.dot_general(dz, qs, _TN, preferred_element_type=F32)
            dv_ref[pl.ds(ks, BLOCK), :] += lax.dot_general(w.astype(BF16), dos, _TN, preferred_element_type=F32)
            return dqs, cb + jnp.sum(b, axis=1, keepdims=True), cg + jnp.sum(g, axis=1, keepdims=True)

        zero_dq = tuple(jnp.zeros((BLOCK, HEAD_DIM), F32) for _ in heads)
        zero = jnp.zeros((HEAD_ROWS, 1), F32)
        dqs, _, _ = lax.fori_loop(first, i + 1, step, (zero_dq, zero, zero))
        dq_ref[...] = (jnp.concatenate(dqs, axis=1) * ATT_SCALE).astype(BF16)

    blk = lambda c: pl.BlockSpec((BLOCK, GROUP_WIDTH), lambda b, i: (b * nblk + i, c))
    seqblk = lambda c: pl.BlockSpec((seq, GROUP_WIDTH), lambda b, i: (b, c))
    vec = pl.BlockSpec(gain.shape, lambda b, i: (0, 0))
    return pl.pallas_call(
        body, grid=(T // seq, nblk), name=name,
        in_specs=[blk(5), seqblk(6), seqblk(7), vec, blk(0), pl.BlockSpec((BLOCK, BLOCK), lambda b, i: (b * nblk + i, 0)),
                  blk(3)],
        out_specs=[blk(0), seqblk(0), seqblk(0), vec],
        out_shape=[jax.ShapeDtypeStruct((T, GROUP_WIDTH), BF16), jax.ShapeDtypeStruct((T, GROUP_WIDTH), F32),
                   jax.ShapeDtypeStruct((T, GROUP_WIDTH), F32), jax.ShapeDtypeStruct(gain.shape, F32)],
        compiler_params=_cp("arbitrary", "arbitrary"))(proj, proj, proj, gain, raw, btot, dy)


def _layer_params(l, sgu_w, sgu_b, pool_w, pool_scale, swa_sinks, mix_out_gain, norm_mix, norm_ffn):
    gains = mix_out_gain[l].reshape(4, 1, GROUP_WIDTH)
    return dict(
        wm=sgu_w[l], bt=sgu_b[l].T,
        wbd=jax.scipy.linalg.block_diag(*[pool_w[l, g] for g in range(len(POOL_WINDOWS))]),
        scale=pool_scale[l][None], sinks=swa_sinks[l][None],
        gain=[gains[m] for m in range(4)], norm_mix=norm_mix[l][None], norm_ffn=norm_ffn[l][None])


def _layer_fwd(l, x, p, w, tab, seq):
    win, wout, wgu, wd = w
    proj, h1 = _norm_mm(x, p["norm_mix"], win, F32, f"in_proj_{l}")
    ya = _sgu_fwd(proj, p["wm"], p["bt"], p["gain"][0], f"sgu_fwd_{l}")
    yb = _pool_fwd(proj, p["wbd"], p["scale"], p["gain"][1], seq, f"pool_fwd_{l}")
    yc = _swa_fwd(proj, p["sinks"], tab, p["gain"][2], seq, f"swa_fwd_{l}")
    yd, raw, btot = _sb_fwd(proj, p["gain"][3], seq, f"sb_fwd_{l}")
    ycat = jnp.concatenate([ya, yb, yc, yd], axis=1)
    xm = _mm_res(x, ycat, wout, f"out_proj_{l}")
    gu, h2 = _norm_mm(xm, p["norm_ffn"], wgu, BF16, f"ffn_up_{l}")
    act = _swiglu_fwd(gu, f"swiglu_fwd_{l}")
    xo = _mm_res(xm, act, wd, f"ffn_down_{l}")
    return xo, (x, proj, h1, ycat, raw, btot, xm, gu, h2, act)


def _layer_bwd(l, dxo, saved, p, w, tab, seq):
    win, wout, wgu, wd = w
    x, proj, h1, ycat, raw, btot, xm, gu, h2, act = saved
    dact = _mm_nt(dxo, wd, BF16, f"ffn_down_dx_{l}")
    g_wd = _mm_tn_rows(act, dxo, FF_PAD, f"ffn_down_dw_{l}")
    dgu = _swiglu_bwd(gu, dact, f"swiglu_bwd_{l}")
    g_wgu = _mm_tn_cols(h2, dgu, FF_PAD, f"ffn_up_dw_{l}")
    dh2 = _mm_nt_acc(dgu, wgu, f"ffn_up_dx_{l}")
    dxm, g_norm_ffn = _rms_bwd_add(xm, p["norm_ffn"], dh2, dxo, f"norm_ffn_bwd_{l}")
    dycat = _mm_nt(dxm, wout, F32, f"out_proj_dx_{l}")
    g_wout = _mm_tn_rows(ycat, dxm, GROUP_WIDTH, f"out_proj_dw_{l}")
    duv, g_wm, g_bt, g_ga = _sgu_bwd(proj, p["wm"], p["bt"], p["gain"][0], dycat, f"sgu_bwd_{l}")
    dp, g_wbd, g_scale, g_gb = _pool_bwd(proj, p["wbd"], p["scale"], p["gain"][1], dycat, seq, f"pool_bwd_{l}")
    dq, dkp, dkc, g_sinks, g_tab, g_gc = _swa_bwd(proj, p["sinks"], tab, p["gain"][2], dycat, seq, f"swa_bwd_{l}")
    dkv = _shift_add(dkc, dkp, seq, f"swa_dkv_{l}")
    dqd, dkd, dvd, g_gd = _sb_bwd(proj, p["gain"][3], raw, btot, dycat, seq, f"sb_bwd_{l}")
    dproj = jnp.concatenate([duv, dp, dq, dkv, dqd, dkd.astype(BF16), dvd.astype(BF16)], axis=1)
    g_win = _mm_tn_cols(h1, dproj, GROUP_WIDTH, f"in_proj_dw_{l}")
    dh1 = _mm_nt_acc(dproj, win, f"in_proj_dx_{l}")
    dx, g_norm_mix = _rms_bwd_add(x, p["norm_mix"], dh1, dxm, f"norm_mix_bwd_{l}")
    ng = len(POOL_WINDOWS)
    gd = GROUP_WIDTH // ng
    small = dict(
        sgu_w=g_wm, sgu_b=g_bt.T,
        pool_w=jnp.stack([g_wbd[g * gd:(g + 1) * gd, g * gd:(g + 1) * gd] for g in range(ng)]),
        pool_scale=g_scale[0], swa_sinks=g_sinks[0],
        mix_out_gain=jnp.concatenate([g_ga[0], g_gb[0], g_gc[0], g_gd[0]]),
        norm_mix=g_norm_mix[0], norm_ffn=g_norm_ffn[0])
    return dx, (g_win, g_wout, g_wgu, g_wd), small, g_tab


def _local_step(x, target, weights, sgu_w, sgu_b, pool_w, pool_scale, swa_sinks, rel_bias, mix_out_gain,
                norm_mix, norm_ffn, norm_final, seq):
    buckets = _bucket_map()
    tab = _bias_table(rel_bias, buckets, "bias_table")
    params = [_layer_params(l, sgu_w, sgu_b, pool_w, pool_scale, swa_sinks, mix_out_gain, norm_mix, norm_ffn)
              for l in range(DEPTH)]
    saved = []
    for l in range(DEPTH):
        x, s = _layer_fwd(l, x, params[l], weights[l], tab, seq)
        saved.append(s)
    loss, dx, g_final = _loss_head(x, norm_final[None], target, "loss_head")
    big, small, g_tab = [None] * DEPTH, [None] * DEPTH, None
    for l in reversed(range(DEPTH)):
        dx, big[l], small[l], t = _layer_bwd(l, dx, saved[l], params[l], weights[l], tab, seq)
        g_tab = t if g_tab is None else g_tab + t
    g_small = {k: jnp.stack([small[l][k] for l in range(DEPTH)]) for k in small[0]}
    g_small["rel_bias"] = _bias_table_bwd(g_tab, buckets, "bias_table_bwd")
    g_small["norm_final"] = g_final[0]
    return loss, dx, big, g_small


def _cast_pad(w, cols, name):
    L, r, c = w.shape

    def body(w_ref, o_ref):
        if cols != c:
            o_ref[...] = jnp.zeros_like(o_ref)
        o_ref[:, :c] = w_ref[...].astype(BF16)

    return pl.pallas_call(
        body, grid=(L,), name=name,
        in_specs=[pl.BlockSpec((None, r, c), lambda l: (l, 0, 0))],
        out_specs=pl.BlockSpec((None, r, cols), lambda l: (l, 0, 0)),
        out_shape=jax.ShapeDtypeStruct((L, r, cols), BF16),
        compiler_params=_cp("parallel"))(w)


def _my_place():
    x, y, c = (lax.axis_index(a) for a in MESH_AXES)
    return x, y, c, 4 * x + 2 * y + c


ANY = pl.BlockSpec(memory_space=pl.ANY)
MESH = pl.DeviceIdType.MESH
DOWN_ROWS = D_FF // N_DEV


def _down_rows(ref, k):
    return ref.at[:, k // 2, pl.ds((k % 2) * DOWN_ROWS, DOWN_ROWS), :]


def _all_gather_weights(s_in, s_out, s_gu, s_down):
    L = s_in.shape[0]
    shards = (s_in, s_out, s_gu, s_down)
    down_full = jnp.zeros((L, 4, FF_PAD, D_MODEL), BF16)
    n = len(shards)

    def body(i0, i1, i2, i3, _, o0, o1, o2, o3, send_sems, recv_sems, local_sems):
        srcs = (i0, i1, i2, i3)
        outs = (o0, o1, o2, o3)
        x, y, c, _me = _my_place()
        chips = [(1 - x, y), (x, 1 - y), (1 - x, 1 - y)]

        def place(t, dev):
            k = 4 * dev[0] + 2 * dev[1] + dev[2]
            return _down_rows(outs[t], k) if t == 3 else outs[t].at[:, k]

        def copy(slot, t, block, to, from_shard):
            return pltpu.make_async_remote_copy(
                src_ref=srcs[t] if from_shard else place(t, block), dst_ref=place(t, block),
                send_sem=send_sems.at[slot, t], recv_sem=recv_sems.at[slot, t], device_id=to, device_id_type=MESH)

        me, sibling = (x, y, c), (x, y, 1 - c)
        mine = [pltpu.make_async_copy(srcs[t], place(t, me), local_sems.at[t]) for t in range(n)]
        for cp in mine:
            cp.start()
        first = [copy(0, t, me, sibling, True) for t in range(n)]
        first += [copy(1 + j, t, me, (*chip, c), True) for j, chip in enumerate(chips) for t in range(n)]
        for cp in first:
            cp.start()
        passed = []
        for j, chip in enumerate(chips):
            for t in range(n):
                copy(1 + j, t, (*chip, c), me, False).wait_recv()
                cp = copy(4 + j, t, (*chip, c), sibling, False)
                cp.start()
                passed.append(cp)
        for t in range(n):
            copy(0, t, sibling, me, False).wait_recv()
        for j, chip in enumerate(chips):
            for t in range(n):
                copy(4 + j, t, (*chip, 1 - c), me, False).wait_recv()
        for cp in first + passed:
            cp.wait_send()
        for cp in mine:
            cp.wait()

    shapes = [jax.ShapeDtypeStruct((L, N_DEV) + s.shape[1:], BF16) for s in shards[:3]]
    shapes.append(jax.ShapeDtypeStruct(down_full.shape, BF16))
    return pl.pallas_call(
        body, name="all_gather_weights", out_shape=shapes,
        in_specs=[ANY] * 5, out_specs=[ANY] * 4, input_output_aliases={4: 3},
        scratch_shapes=[pltpu.SemaphoreType.DMA((7, n)), pltpu.SemaphoreType.DMA((7, n)), pltpu.SemaphoreType.DMA((n,))],
        )(*shards, down_full)


def _peer(x, y, c, d):
    return (x ^ (d >> 2), y ^ ((d >> 1) & 1), c ^ (d & 1))


def _reduce_scatter_grads(big):
    L = len(big)
    n = 4
    out_rows = D_MODEL // N_DEV

    def body(*refs):
        ins = refs[:L * n]
        outs = refs[L * n:L * n + n]
        send_sems, recv_sems, local_sems = refs[L * n + n:]
        x, y, c, me = _my_place()

        def block(l, t, k):
            g = ins[l * n + t]
            if t == 1:
                return g.at[pl.ds(k * out_rows, out_rows), :]
            if t == 3:
                return g.at[pl.ds((k // 2) * FF_PAD + (k % 2) * DOWN_ROWS, DOWN_ROWS), :]
            return g.at[k]

        local = [pltpu.make_async_copy(block(l, t, me), outs[t].at[l, me], local_sems.at[l, t])
                 for l in range(L) for t in range(n)]
        for cp in local:
            cp.start()
        sends = []
        for d in range(1, N_DEV):
            px, py, pc = _peer(x, y, c, d)
            pk = 4 * px + 2 * py + pc
            for l in range(L):
                for t in range(n):
                    sends.append(pltpu.make_async_remote_copy(
                        src_ref=block(l, t, pk), dst_ref=outs[t].at[l, me],
                        send_sem=send_sems.at[d - 1, l, t], recv_sem=recv_sems.at[d - 1, l, t],
                        device_id=(px, py, pc), device_id_type=MESH))
        for cp in sends:
            cp.start()
        for d in range(1, N_DEV):
            px, py, pc = _peer(x, y, c, d)
            pk = 4 * px + 2 * py + pc
            for l in range(L):
                for t in range(n):
                    pltpu.make_async_remote_copy(
                        src_ref=block(l, t, me), dst_ref=outs[t].at[l, pk],
                        send_sem=send_sems.at[d - 1, l, t], recv_sem=recv_sems.at[d - 1, l, t],
                        device_id=(px, py, pc), device_id_type=MESH).wait_recv()
        for cp in sends:
            cp.wait_send()
        for cp in local:
            cp.wait()

    shapes = [jax.ShapeDtypeStruct((L, N_DEV, D_MODEL, GROUP_WIDTH), BF16),
              jax.ShapeDtypeStruct((L, N_DEV, out_rows, D_MODEL), BF16),
              jax.ShapeDtypeStruct((L, N_DEV, D_MODEL, FF_PAD), BF16),
              jax.ShapeDtypeStruct((L, N_DEV, DOWN_ROWS, D_MODEL), BF16)]
    flat = [g for layer in big for g in layer]
    return pl.pallas_call(
        body, name="reduce_scatter_grads", out_shape=shapes,
        in_specs=[ANY] * len(flat), out_specs=[ANY] * n,
        scratch_shapes=[pltpu.SemaphoreType.DMA((N_DEV - 1, L, n)), pltpu.SemaphoreType.DMA((N_DEV - 1, L, n)),
                        pltpu.SemaphoreType.DMA((L, n))],
        )(*flat)


def _all_reduce_small(part):
    rows = part.shape[0]

    def body(p_ref, o_ref, buf, send_sems, recv_sems):
        x, y, c, me = _my_place()
        buf[me] = p_ref[...]
        sends = []
        for d in range(1, N_DEV):
            peer = _peer(x, y, c, d)
            sends.append(pltpu.make_async_remote_copy(
                src_ref=p_ref, dst_ref=buf.at[me], send_sem=send_sems.at[d - 1], recv_sem=recv_sems.at[d - 1],
                device_id=peer, device_id_type=MESH))
        for cp in sends:
            cp.start()
        for d in range(1, N_DEV):
            px, py, pc = _peer(x, y, c, d)
            pltpu.make_async_remote_copy(
                src_ref=p_ref, dst_ref=buf.at[4 * px + 2 * py + pc], send_sem=send_sems.at[d - 1],
                recv_sem=recv_sems.at[d - 1], device_id=(px, py, pc), device_id_type=MESH).wait_recv()
        for cp in sends:
            cp.wait_send()
        total = buf[0]
        for k in range(1, N_DEV):
            total = total + buf[k]
        o_ref[...] = total

    return pl.pallas_call(
        body, name="all_reduce_small", out_shape=jax.ShapeDtypeStruct((rows, 128), F32),
        in_specs=[pl.BlockSpec(memory_space=pltpu.VMEM)], out_specs=pl.BlockSpec(memory_space=pltpu.VMEM),
        scratch_shapes=[pltpu.VMEM((N_DEV, rows, 128), F32), pltpu.SemaphoreType.DMA((N_DEV - 1,)),
                        pltpu.SemaphoreType.DMA((N_DEV - 1,))],
        compiler_params=pltpu.CompilerParams(vmem_limit_bytes=VMEM_LIMIT))(part)


def _adamw(w, g, m, v):
    m = ADAM_B1 * m + (1.0 - ADAM_B1) * g
    v = ADAM_B2 * v + (1.0 - ADAM_B2) * jnp.square(g)
    m_hat = m / (1.0 - ADAM_B1 ** ADAM_STEP)
    v_hat = v / (1.0 - ADAM_B2 ** ADAM_STEP)
    delta = -ADAM_LR * (m_hat / (jnp.sqrt(v_hat) + ADAM_EPS) + ADAM_WD * w)
    return delta, m, v


def _adamw_sharded(parts, w, m, v, tr, name):
    L, r, c = w.shape
    cp = parts.shape[-1]

    def body(p_ref, w_ref, m_ref, v_ref, g_ref, d_ref, nm_ref, nv_ref):
        g = p_ref[0, :, :c].astype(F32)
        for k in range(1, N_DEV):
            g = g + p_ref[k, :, :c].astype(F32)
        delta, nm, nv = _adamw(w_ref[...], g, m_ref[...], v_ref[...])
        g_ref[...] = g
        d_ref[...] = delta
        nm_ref[...] = nm
        nv_ref[...] = nv

    blk = pl.BlockSpec((None, tr, c), lambda l, i: (l, i, 0))
    out = jax.ShapeDtypeStruct((L, r, c), F32)
    return pl.pallas_call(
        body, grid=(L, r // tr), name=name,
        in_specs=[pl.BlockSpec((None, N_DEV, tr, cp), lambda l, i: (l, 0, i, 0)), blk, blk, blk],
        out_specs=[blk] * 4, out_shape=[out] * 4,
        compiler_params=_cp("parallel", "parallel"))(parts, w, m, v)


def _adamw_small(g, w, m, v, name):
    def body(g_ref, w_ref, m_ref, v_ref, d_ref, nm_ref, nv_ref):
        delta, nm, nv = _adamw(w_ref[...], g_ref[...], m_ref[...], v_ref[...])
        d_ref[...] = delta
        nm_ref[...] = nm
        nv_ref[...] = nv

    out = jax.ShapeDtypeStruct(g.shape, F32)
    return pl.pallas_call(body, name=name, out_shape=[out] * 3, compiler_params=_cp())(g, w, m, v)


SMALL = ("sgu_w", "sgu_b", "pool_w", "pool_scale", "swa_sinks", "rel_bias", "mix_out_gain", "norm_mix", "norm_ffn",
         "norm_final")


def _pack(parts, rows):
    flat = jnp.concatenate([p.reshape(-1) for p in parts])
    return jnp.pad(flat, (0, rows * 128 - flat.shape[0])).reshape(rows, 128)


def _unpack(buf, like):
    flat = buf.reshape(-1)
    out, at = [], 0
    for a in like:
        out.append(flat[at:at + a.size].reshape(a.shape))
        at += a.size
    return out


def kernel(x, w_in, w_out, sgu_w, sgu_b, pool_w, pool_scale, swa_sinks, rel_bias, mix_out_gain, norm_mix, norm_ffn, w_gate_up, w_down, norm_final, loss_target, m_w_in, m_w_out, m_sgu_w, m_sgu_b, m_pool_w, m_pool_scale, m_swa_sinks, m_rel_bias, m_mix_out_gain, m_norm_mix, m_norm_ffn, m_w_gate_up, m_w_down, m_norm_final, v_w_in, v_w_out, v_sgu_w, v_sgu_b, v_pool_w, v_pool_scale, v_swa_sinks, v_rel_bias, v_mix_out_gain, v_norm_mix, v_norm_ffn, v_w_gate_up, v_w_down, v_norm_final):
    bl, seq, _ = x.shape
    L = w_in.shape[0]
    full_in, full_out, full_gu, full_down = _all_gather_weights(
        _cast_pad(w_in, GROUP_WIDTH, "shard_w_in"), _cast_pad(w_out, D_MODEL, "shard_w_out"),
        _cast_pad(w_gate_up, FF_PAD, "shard_w_gate_up"), _cast_pad(w_down, D_MODEL, "shard_w_down"))
    weights = [(full_in[l], full_out[l].reshape(D_MODEL, D_MODEL), full_gu[l], full_down[l].reshape(D_FF_PAD, D_MODEL))
               for l in range(L)]
    small_w = dict(sgu_w=sgu_w, sgu_b=sgu_b, pool_w=pool_w, pool_scale=pool_scale, swa_sinks=swa_sinks,
                   rel_bias=rel_bias, mix_out_gain=mix_out_gain, norm_mix=norm_mix, norm_ffn=norm_ffn,
                   norm_final=norm_final)
    small_m = dict(sgu_w=m_sgu_w, sgu_b=m_sgu_b, pool_w=m_pool_w, pool_scale=m_pool_scale, swa_sinks=m_swa_sinks,
                   rel_bias=m_rel_bias, mix_out_gain=m_mix_out_gain, norm_mix=m_norm_mix, norm_ffn=m_norm_ffn,
                   norm_final=m_norm_final)
    small_v = dict(sgu_w=v_sgu_w, sgu_b=v_sgu_b, pool_w=v_pool_w, pool_scale=v_pool_scale, swa_sinks=v_swa_sinks,
                   rel_bias=v_rel_bias, mix_out_gain=v_mix_out_gain, norm_mix=v_norm_mix, norm_ffn=v_norm_ffn,
                   norm_final=v_norm_final)
    loss, dx, big, g_small = _local_step(
        x.reshape(bl * seq, D_MODEL), loss_target.reshape(bl * seq, D_MODEL), weights, sgu_w, sgu_b, pool_w, pool_scale,
        swa_sinks, rel_bias, mix_out_gain, norm_mix, norm_ffn, norm_final, seq)
    p_in, p_out, p_gu, p_down = _reduce_scatter_grads(big)
    outs_in = _adamw_sharded(p_in, w_in, m_w_in, v_w_in, 256, "adamw_w_in")
    outs_out = _adamw_sharded(p_out, w_out, m_w_out, v_w_out, D_MODEL // N_DEV, "adamw_w_out")
    outs_gu = _adamw_sharded(p_gu, w_gate_up, m_w_gate_up, v_w_gate_up, 256, "adamw_w_gate_up")
    outs_down = _adamw_sharded(p_down, w_down, m_w_down, v_w_down, DOWN_ROWS // 2, "adamw_w_down")
    sizes = sum(small_w[k].size for k in SMALL) + 128
    rows = -(-sizes // 1024) * 8
    packed = _pack([g_small[k] for k in SMALL] + [loss[0]], rows)
    total = _all_reduce_small(packed)
    like = [small_w[k] for k in SMALL]
    d_s, m_s, v_s = _adamw_small(total, _pack(like, rows), _pack([small_m[k] for k in SMALL], rows),
                                 _pack([small_v[k] for k in SMALL], rows), "adamw_small")
    g_list = dict(zip(SMALL, _unpack(total, like)))
    d_list = dict(zip(SMALL, _unpack(d_s, like)))
    m_list = dict(zip(SMALL, _unpack(m_s, like)))
    v_list = dict(zip(SMALL, _unpack(v_s, like)))
    loss_total = total.reshape(-1)[sum(a.size for a in like)]
    big_outs = dict(w_in=outs_in, w_out=outs_out, w_gate_up=outs_gu, w_down=outs_down)
    order = ("w_in", "w_out", "sgu_w", "sgu_b", "pool_w", "pool_scale", "swa_sinks", "rel_bias", "mix_out_gain",
             "norm_mix", "norm_ffn", "w_gate_up", "w_down", "norm_final")
    result = [loss_total, dx.reshape(bl, seq, D_MODEL)]
    for which, small in enumerate((g_list, d_list, m_list, v_list)):
        for name in order:
            result.append(big_outs[name][which] if name in big_outs else small[name])
    return tuple(result)
```

```python
import functools

import jax
import jax.numpy as jnp
from jax import lax
from jax.experimental import pallas as pl
from jax.experimental.pallas import tpu as pltpu

F32 = jnp.float32
BF16 = jnp.bfloat16

N_DEV = 8
DEPTH = 4
D_MODEL = 1024
GROUP_WIDTH = 256
HEAD_DIM = 64
GROUP_HEADS = 4
BLOCK = 128
N_BUCKETS = 32
MAX_DISTANCE = 128
POOL_WINDOWS = (2, 4, 8, 16)
D_FF = 2816
FF_SHARD = D_FF // 4
FF_PAD = 768
D_FF_PAD = 4 * FF_PAD
EPS = 1e-6
ATT_SCALE = HEAD_DIM ** -0.5
ADAM_LR = 0.001
ADAM_B1 = 0.9
ADAM_B2 = 0.999
ADAM_EPS = 1e-08
ADAM_WD = 0.01
ADAM_STEP = 10
VMEM_LIMIT = 56 * 1024 * 1024
MESH_AXES = ("x", "y", "c")


def _cp(*sem):
    return pltpu.CompilerParams(dimension_semantics=sem or None, vmem_limit_bytes=VMEM_LIMIT)


_NT = (((1,), (1,)), ((), ()))
_TN = (((0,), (0,)), ((), ()))


@jax.custom_vjp
def _bdot(a, b):
    return jnp.dot(a.astype(BF16), b.astype(BF16), preferred_element_type=F32)


def _bdot_fwd(a, b):
    return _bdot(a, b), (a.astype(BF16), b.astype(BF16))


def _bdot_bwd(res, ct):
    a, b = res
    c = ct.astype(BF16)
    return (lax.dot_general(c, b, _NT, preferred_element_type=F32),
            lax.dot_general(a, c, _TN, preferred_element_type=F32))


_bdot.defvjp(_bdot_fwd, _bdot_bwd)


@jax.custom_vjp
def _bdot_nt(a, b):
    return lax.dot_general(a.astype(BF16), b.astype(BF16), _NT, preferred_element_type=F32)


def _bdot_nt_fwd(a, b):
    return _bdot_nt(a, b), (a.astype(BF16), b.astype(BF16))


def _bdot_nt_bwd(res, ct):
    a, b = res
    c = ct.astype(BF16)
    return (jnp.dot(c, b, preferred_element_type=F32),
            lax.dot_general(c, a, _TN, preferred_element_type=F32))


_bdot_nt.defvjp(_bdot_nt_fwd, _bdot_nt_bwd)


def _rms(x, g):
    return x * lax.rsqrt(jnp.mean(x * x, axis=-1, keepdims=True) + EPS) * g


def _split_dot(x, u):
    hi = x.astype(BF16)
    lo = (x - hi.astype(F32)).astype(BF16)
    return jnp.dot(hi, u, preferred_element_type=F32) + jnp.dot(lo, u, preferred_element_type=F32)


def _head_mask(h, shape):
    col = lax.broadcasted_iota(jnp.int32, shape, 1)
    return (col >= h * HEAD_DIM) & (col < (h + 1) * HEAD_DIM)


def _norm_mm(x, g, w, l, jb, out_dtype, name, tm=1024):
    T, K = x.shape
    _, nb, _, tn = w.shape
    tm = min(tm, T)

    def body(x_ref, g_ref, w_ref, o_ref, h_ref):
        @pl.when(pl.program_id(1) == 0)
        def _():
            h_ref[...] = _rms(x_ref[...], g_ref[...]).astype(BF16)
        h = h_ref[...]
        for jj in range(jb):
            o_ref[:, jj * tn:(jj + 1) * tn] = jnp.dot(h, w_ref[jj], preferred_element_type=F32).astype(o_ref.dtype)

    return pl.pallas_call(
        body, grid=(T // tm, nb // jb), name=name,
        in_specs=[pl.BlockSpec((tm, K), lambda i, j: (i, 0)), pl.BlockSpec((1, K), lambda i, j: (0, 0)),
                  pl.BlockSpec((None, jb, K, tn), lambda i, j: (l, j, 0, 0))],
        out_specs=[pl.BlockSpec((tm, jb * tn), lambda i, j: (i, j)), pl.BlockSpec((tm, K), lambda i, j: (i, 0))],
        out_shape=[jax.ShapeDtypeStruct((T, nb * tn), out_dtype), jax.ShapeDtypeStruct((T, K), BF16)],
        compiler_params=_cp("parallel", "arbitrary"))(x, g, w)


def _mm_res(res, a, w, l, name, tn, tm=1024):
    T, K = a.shape
    N = w.shape[2]
    tm = min(tm, T)

    def body(r_ref, a_ref, w_ref, o_ref):
        o_ref[...] = r_ref[...] + jnp.dot(a_ref[...], w_ref[...], preferred_element_type=F32)

    return pl.pallas_call(
        body, grid=(T // tm, N // tn), name=name,
        in_specs=[pl.BlockSpec((tm, tn), lambda i, j: (i, j)), pl.BlockSpec((tm, K), lambda i, j: (i, 0)),
                  pl.BlockSpec((None, K, tn), lambda i, j: (l, 0, j))],
        out_specs=pl.BlockSpec((tm, tn), lambda i, j: (i, j)),
        out_shape=jax.ShapeDtypeStruct((T, N), F32),
        compiler_params=_cp("parallel", "parallel"))(res, a, w)


def _mm_nt(a, w, l, out_dtype, name, tm=1024, tn=1024):
    T, K = a.shape
    N = w.shape[1]
    tm = min(tm, T)

    def body(a_ref, w_ref, o_ref):
        o_ref[...] = lax.dot_general(a_ref[...].astype(BF16), w_ref[...], _NT,
                                     preferred_element_type=F32).astype(o_ref.dtype)

    return pl.pallas_call(
        body, grid=(T // tm, N // tn), name=name,
        in_specs=[pl.BlockSpec((tm, K), lambda i, j: (i, 0)), pl.BlockSpec((None, tn, K), lambda i, j: (l, j, 0))],
        out_specs=pl.BlockSpec((tm, tn), lambda i, j: (i, j)),
        out_shape=jax.ShapeDtypeStruct((T, N), out_dtype),
        compiler_params=_cp("parallel", "parallel"))(a, w)


def _mm_nt_acc(a, w, l, jb, name, tm=1024):
    T = a.shape[0]
    _, nb, K, tn = w.shape
    tm = min(tm, T)

    def body(a_ref, w_ref, o_ref):
        part = lax.dot_general(a_ref[:, :tn], w_ref[0], _NT, preferred_element_type=F32)
        for jj in range(1, jb):
            part += lax.dot_general(a_ref[:, jj * tn:(jj + 1) * tn], w_ref[jj], _NT, preferred_element_type=F32)

        @pl.when(pl.program_id(1) == 0)
        def _():
            o_ref[...] = part

        @pl.when(pl.program_id(1) > 0)
        def _():
            o_ref[...] += part

    return pl.pallas_call(
        body, grid=(T // tm, nb // jb), name=name,
        in_specs=[pl.BlockSpec((tm, jb * tn), lambda i, j: (i, j)),
                  pl.BlockSpec((None, jb, K, tn), lambda i, j: (l, j, 0, 0))],
        out_specs=pl.BlockSpec((tm, K), lambda i, j: (i, 0)),
        out_shape=jax.ShapeDtypeStruct((T, K), F32),
        compiler_params=_cp("parallel", "arbitrary"))(a, w)


def _mm_tn_cols(lhs, rhs, tn, jb, name, tm=1024):
    T, K = lhs.shape
    nb = rhs.shape[1] // tn
    tm = min(tm, T)
    nt = T // tm

    def body(l_ref, r_ref, o_ref, acc):
        part = lax.dot_general(l_ref[...], r_ref[...], _TN, preferred_element_type=F32)

        @pl.when(pl.program_id(1) == 0)
        def _():
            acc[...] = part

        @pl.when(pl.program_id(1) > 0)
        def _():
            acc[...] += part

        @pl.when(pl.program_id(1) == nt - 1)
        def _():
            for jj in range(jb):
                o_ref[jj] = acc[:, jj * tn:(jj + 1) * tn].astype(BF16)

    return pl.pallas_call(
        body, grid=(nb // jb, nt), name=name,
        in_specs=[pl.BlockSpec((tm, K), lambda j, t: (t, 0)), pl.BlockSpec((tm, jb * tn), lambda j, t: (t, j))],
        out_specs=pl.BlockSpec((jb, K, tn), lambda j, t: (j, 0, 0)),
        out_shape=jax.ShapeDtypeStruct((nb, K, tn), BF16),
        scratch_shapes=[pltpu.VMEM((K, jb * tn), F32)],
        compiler_params=_cp("parallel", "arbitrary"))(lhs, rhs)


def _mm_tn_rows(lhs, rhs, tk, name, tm=1024):
    T, Kl = lhs.shape
    N = rhs.shape[1]
    tm = min(tm, T)
    nt = T // tm

    def body(l_ref, r_ref, o_ref, acc):
        part = lax.dot_general(l_ref[...], r_ref[...].astype(BF16), _TN, preferred_element_type=F32)

        @pl.when(pl.program_id(1) == 0)
        def _():
            acc[...] = part

        @pl.when(pl.program_id(1) > 0)
        def _():
            acc[...] += part

        @pl.when(pl.program_id(1) == nt - 1)
        def _():
            o_ref[...] = acc[...].astype(BF16)

    return pl.pallas_call(
        body, grid=(Kl // tk, nt), name=name,
        in_specs=[pl.BlockSpec((tm, tk), lambda l, t: (t, l)), pl.BlockSpec((tm, N), lambda l, t: (t, 0))],
        out_specs=pl.BlockSpec((tk, N), lambda l, t: (l, 0)),
        out_shape=jax.ShapeDtypeStruct((Kl, N), BF16),
        scratch_shapes=[pltpu.VMEM((tk, N), F32)],
        compiler_params=_cp("parallel", "arbitrary"))(lhs, rhs)


def _swiglu_fwd(gu, name, tm=256):
    T = gu.shape[0]

    def body(gu_ref, o_ref):
        gate = gu_ref[:, :D_FF_PAD].astype(F32)
        up = gu_ref[:, D_FF_PAD:].astype(F32)
        o_ref[...] = (jax.nn.silu(gate) * up).astype(BF16)

    return pl.pallas_call(
        body, grid=(T // tm,), name=name,
        in_specs=[pl.BlockSpec((tm, 2 * D_FF_PAD), lambda i: (i, 0))],
        out_specs=pl.BlockSpec((tm, D_FF_PAD), lambda i: (i, 0)),
        out_shape=jax.ShapeDtypeStruct((T, D_FF_PAD), BF16),
        compiler_params=_cp("parallel"))(gu)


def _swiglu_bwd(gu, dact, name, tm=256):
    T = gu.shape[0]

    def body(gu_ref, d_ref, o_ref):
        gate = gu_ref[:, :D_FF_PAD].astype(F32)
        up = gu_ref[:, D_FF_PAD:].astype(F32)
        d = d_ref[...].astype(F32)
        sig = jax.nn.sigmoid(gate)
        silu = gate * sig
        o_ref[:, :D_FF_PAD] = (d * up * (sig + silu * (1.0 - sig))).astype(BF16)
        o_ref[:, D_FF_PAD:] = (d * silu).astype(BF16)

    return pl.pallas_call(
        body, grid=(T // tm,), name=name,
        in_specs=[pl.BlockSpec((tm, 2 * D_FF_PAD), lambda i: (i, 0)), pl.BlockSpec((tm, D_FF_PAD), lambda i: (i, 0))],
        out_specs=pl.BlockSpec((tm, 2 * D_FF_PAD), lambda i: (i, 0)),
        out_shape=jax.ShapeDtypeStruct((T, 2 * D_FF_PAD), BF16),
        compiler_params=_cp("parallel"))(gu, dact)


def _rms_bwd_add(x, g, dh, dres, name, tm=512):
    T, K = x.shape

    def body(x_ref, g_ref, dh_ref, dr_ref, dx_ref, dg_ref):
        _, vjp = jax.vjp(_rms, x_ref[...], g_ref[...])
        dx, dg = vjp(dh_ref[...])
        dx_ref[...] = dr_ref[...] + dx

        @pl.when(pl.program_id(0) == 0)
        def _():
            dg_ref[...] = dg

        @pl.when(pl.program_id(0) > 0)
        def _():
            dg_ref[...] += dg

    row = pl.BlockSpec((tm, K), lambda i: (i, 0))
    vec = pl.BlockSpec((1, K), lambda i: (0, 0))
    return pl.pallas_call(
        body, grid=(T // tm,), name=name,
        in_specs=[row, vec, row, row], out_specs=[row, vec],
        out_shape=[jax.ShapeDtypeStruct((T, K), F32), jax.ShapeDtypeStruct((1, K), F32)],
        compiler_params=_cp("arbitrary"))(x, g, dh, dres)


def _loss_head(x, g, target, name, tm=512):
    T, K = x.shape

    def loss_fn(xv, gv, tv):
        err = _rms(xv, gv) - tv
        return 0.5 * jnp.sum(jnp.mean(err * err, axis=-1, keepdims=True), axis=0, keepdims=True)

    def body(x_ref, g_ref, t_ref, l_ref, dx_ref, dg_ref):
        val, vjp = jax.vjp(lambda xv, gv: loss_fn(xv, gv, t_ref[...]), x_ref[...], g_ref[...])
        dx, dg = vjp(jnp.ones((1, 1), F32))
        dx_ref[...] = dx
        lval = jnp.broadcast_to(val, (1, 128))

        @pl.when(pl.program_id(0) == 0)
        def _():
            dg_ref[...] = dg
            l_ref[...] = lval

        @pl.when(pl.program_id(0) > 0)
        def _():
            dg_ref[...] += dg
            l_ref[...] += lval

    row = pl.BlockSpec((tm, K), lambda i: (i, 0))
    vec = pl.BlockSpec((1, K), lambda i: (0, 0))
    return pl.pallas_call(
        body, grid=(T // tm,), name=name,
        in_specs=[row, vec, row], out_specs=[pl.BlockSpec((1, 128), lambda i: (0, 0)), row, vec],
        out_shape=[jax.ShapeDtypeStruct((1, 128), F32), jax.ShapeDtypeStruct((T, K), F32),
                   jax.ShapeDtypeStruct((1, K), F32)],
        compiler_params=_cp("arbitrary"))(x, g, target)


def _sgu_fn(u, v, wm, bt, gain):
    ug = jax.nn.gelu(u)
    vg = jax.nn.gelu(v)
    row = lax.broadcasted_iota(jnp.int32, (BLOCK, BLOCK), 0)
    col = lax.broadcasted_iota(jnp.int32, (BLOCK, BLOCK), 1)
    tri = row >= col
    outs = []
    for h in range(GROUP_HEADS):
        vh = vg[:, h * HEAD_DIM:(h + 1) * HEAD_DIM]
        xc = vh - jnp.mean(vh, axis=-1, keepdims=True)
        vn = xc * lax.rsqrt(jnp.mean(xc * xc, axis=-1, keepdims=True) + EPS)
        outs.append(_bdot(jnp.where(tri, wm[h], 0.0), vn) + bt[:, h:h + 1])
    return _rms(ug * jnp.concatenate(outs, axis=1), gain)


def _sgu_fwd(proj, wm, bt, gain, name):
    T = proj.shape[0]

    def body(u_ref, v_ref, w_ref, b_ref, g_ref, o_ref):
        o_ref[...] = _sgu_fn(u_ref[...], v_ref[...], w_ref[...], b_ref[...], g_ref[...]).astype(BF16)

    full = lambda shape: pl.BlockSpec(shape, lambda i: (0,) * len(shape))
    return pl.pallas_call(
        body, grid=(T // BLOCK,), name=name,
        in_specs=[pl.BlockSpec((BLOCK, GROUP_WIDTH), lambda i: (i, 0)), pl.BlockSpec((BLOCK, GROUP_WIDTH), lambda i: (i, 1)),
                  full(wm.shape), full(bt.shape), full(gain.shape)],
        out_specs=pl.BlockSpec((BLOCK, GROUP_WIDTH), lambda i: (i, 0)),
        out_shape=jax.ShapeDtypeStruct((T, GROUP_WIDTH), BF16),
        compiler_params=_cp("parallel"))(proj, proj, wm, bt, gain)


def _acc_out(first, refs, vals):
    @pl.when(first)
    def _():
        for r, v in zip(refs, vals):
            r[...] = v

    @pl.when(jnp.logical_not(first))
    def _():
        for r, v in zip(refs, vals):
            r[...] += v


def _sgu_bwd(proj, wm, bt, gain, dy, name):
    T = proj.shape[0]

    def body(u_ref, v_ref, w_ref, b_ref, g_ref, dy_ref, duv_ref, dw_ref, db_ref, dg_ref):
        _, vjp = jax.vjp(_sgu_fn, u_ref[...], v_ref[...], w_ref[...], b_ref[...], g_ref[...])
        du, dv, dw, db, dg = vjp(dy_ref[...])
        duv_ref[:, :GROUP_WIDTH] = du.astype(BF16)
        duv_ref[:, GROUP_WIDTH:] = dv.astype(BF16)
        _acc_out(pl.program_id(0) == 0, (dw_ref, db_ref, dg_ref), (dw, db, dg))

    full = lambda shape: pl.BlockSpec(shape, lambda i: (0,) * len(shape))
    return pl.pallas_call(
        body, grid=(T // BLOCK,), name=name,
        in_specs=[pl.BlockSpec((BLOCK, GROUP_WIDTH), lambda i: (i, 0)), pl.BlockSpec((BLOCK, GROUP_WIDTH), lambda i: (i, 1)),
                  full(wm.shape), full(bt.shape), full(gain.shape),
                  pl.BlockSpec((BLOCK, GROUP_WIDTH), lambda i: (i, 0))],
        out_specs=[pl.BlockSpec((BLOCK, 2 * GROUP_WIDTH), lambda i: (i, 0)), full(wm.shape), full(bt.shape), full(gain.shape)],
        out_shape=[jax.ShapeDtypeStruct((T, 2 * GROUP_WIDTH), BF16), jax.ShapeDtypeStruct(wm.shape, F32),
                   jax.ShapeDtypeStruct(bt.shape, F32), jax.ShapeDtypeStruct(gain.shape, F32)],
        compiler_params=_cp("arbitrary"))(proj, proj, wm, bt, gain, dy)


def _pool_consts(seq):
    t = lax.broadcasted_iota(jnp.int32, (seq, GROUP_WIDTH), 0)
    grp = lax.broadcasted_iota(jnp.int32, (seq, GROUP_WIDTH), 1) // (GROUP_WIDTH // len(POOL_WINDOWS))
    win = jnp.where(grp == 0, POOL_WINDOWS[0], jnp.where(grp == 1, POOL_WINDOWS[1],
                    jnp.where(grp == 2, POOL_WINDOWS[2], POOL_WINDOWS[3])))
    count = jnp.minimum(t + 1, win).astype(F32)
    return t, grp, count


def _by_group(grp, vals):
    return jnp.where(grp == 0, vals[0], jnp.where(grp == 1, vals[1], jnp.where(grp == 2, vals[2], vals[3])))


def _window_sums(x, t, seq, back):
    def shift(a, k):
        if back:
            return jnp.where(t >= k, pltpu.roll(a, k, 0), 0.0)
        return jnp.where(t < seq - k, pltpu.roll(a, seq - k, 0), 0.0)
    sums = []
    a, k = x, 1
    for _ in POOL_WINDOWS:
        a = a + shift(a, k)
        sums.append(a)
        k *= 2
    return sums


def _pool_tail(y, wbd, scale, gain):
    return _rms(_bdot(y, wbd) * scale, gain)


def _pool_fwd(proj, wbd, scale, gain, seq, name):
    T = proj.shape[0]

    def body(p_ref, w_ref, s_ref, g_ref, o_ref):
        p = p_ref[...]
        t, grp, count = _pool_consts(seq)
        y = _by_group(grp, _window_sums(p, t, seq, True)) / count - p
        o_ref[...] = _pool_tail(y, w_ref[...], s_ref[...], g_ref[...]).astype(BF16)

    full = lambda shape: pl.BlockSpec(shape, lambda b: (0,) * len(shape))
    return pl.pallas_call(
        body, grid=(T // seq,), name=name,
        in_specs=[pl.BlockSpec((seq, GROUP_WIDTH), lambda b: (b, 2)), full(wbd.shape), full(scale.shape), full(gain.shape)],
        out_specs=pl.BlockSpec((seq, GROUP_WIDTH), lambda b: (b, 0)),
        out_shape=jax.ShapeDtypeStruct((T, GROUP_WIDTH), BF16),
        compiler_params=_cp("parallel"))(proj, wbd, scale, gain)


def _pool_bwd(proj, wbd, scale, gain, dy, seq, name):
    T = proj.shape[0]

    def body(p_ref, w_ref, s_ref, g_ref, dy_ref, dp_ref, dw_ref, ds_ref, dg_ref):
        p = p_ref[...]
        t, grp, count = _pool_consts(seq)
        y = _by_group(grp, _window_sums(p, t, seq, True)) / count - p
        _, vjp = jax.vjp(_pool_tail, y, w_ref[...], s_ref[...], g_ref[...])
        d_y, dw, ds, dg = vjp(dy_ref[...])
        dp = _by_group(grp, _window_sums(d_y / count, t, seq, False)) - d_y
        dp_ref[...] = dp.astype(BF16)
        _acc_out(pl.program_id(0) == 0, (dw_ref, ds_ref, dg_ref), (dw, ds, dg))

    full = lambda shape: pl.BlockSpec(shape, lambda b: (0,) * len(shape))
    return pl.pallas_call(
        body, grid=(T // seq,), name=name,
        in_specs=[pl.BlockSpec((seq, GROUP_WIDTH), lambda b: (b, 2)), full(wbd.shape), full(scale.shape), full(gain.shape),
                  pl.BlockSpec((seq, GROUP_WIDTH), lambda b: (b, 1))],
        out_specs=[pl.BlockSpec((seq, GROUP_WIDTH), lambda b: (b, 0)), full(wbd.shape), full(scale.shape), full(gain.shape)],
        out_shape=[jax.ShapeDtypeStruct((T, GROUP_WIDTH), BF16), jax.ShapeDtypeStruct(wbd.shape, F32),
                   jax.ShapeDtypeStruct(scale.shape, F32), jax.ShapeDtypeStruct(gain.shape, F32)],
        compiler_params=_cp("arbitrary"))(proj, wbd, scale, gain, dy)


def _swa_fn(q, kv_prev, kv_cur, sinks, tab, gain, first):
    half = GROUP_WIDTH // 2
    k2 = jnp.concatenate([kv_prev[:, :half], kv_cur[:, :half]], axis=0)
    v2 = jnp.concatenate([kv_prev[:, half:], kv_cur[:, half:]], axis=0)
    qi = lax.broadcasted_iota(jnp.int32, (BLOCK, 2 * BLOCK), 0)
    kj = lax.broadcasted_iota(jnp.int32, (BLOCK, 2 * BLOCK), 1)
    dist = qi + BLOCK - kj
    mask = (dist >= 0) & (dist < BLOCK) & ((kj >= BLOCK) | jnp.logical_not(first))
    outs = []
    for h in range(GROUP_HEADS):
        kvh = h // 2
        kh = k2[:, kvh * HEAD_DIM:(kvh + 1) * HEAD_DIM]
        vh = v2[:, kvh * HEAD_DIM:(kvh + 1) * HEAD_DIM]
        logits = _bdot_nt(q[:, h * HEAD_DIM:(h + 1) * HEAD_DIM], kh) * ATT_SCALE + tab[h]
        logits = jnp.where(mask, logits, -1e30)
        sink = sinks[:, h:h + 1]
        m = lax.stop_gradient(jnp.maximum(jnp.max(logits, axis=1, keepdims=True), sink))
        p = jnp.exp(logits - m)
        probs = p / (jnp.sum(p, axis=1, keepdims=True) + jnp.exp(sink - m))
        outs.append(_bdot(probs, vh))
    return _rms(jnp.concatenate(outs, axis=1), gain)


def _swa_specs(nblk):
    q = pl.BlockSpec((BLOCK, GROUP_WIDTH), lambda b, i: (b * nblk + i, 3))
    cur = pl.BlockSpec((BLOCK, GROUP_WIDTH), lambda b, i: (b * nblk + i, 4))
    prev = pl.BlockSpec((BLOCK, GROUP_WIDTH), lambda b, i: (b * nblk + jnp.maximum(i - 1, 0), 4))
    return q, prev, cur


def _swa_fwd(proj, sinks, tab, gain, seq, name):
    T = proj.shape[0]
    nblk = seq // BLOCK

    def body(q_ref, kp_ref, kc_ref, s_ref, t_ref, g_ref, o_ref):
        o_ref[...] = _swa_fn(q_ref[...], kp_ref[...], kc_ref[...], s_ref[...], t_ref[...], g_ref[...],
                             pl.program_id(1) == 0).astype(BF16)

    full = lambda shape: pl.BlockSpec(shape, lambda b, i: (0,) * len(shape))
    return pl.pallas_call(
        body, grid=(T // seq, nblk), name=name,
        in_specs=[*_swa_specs(nblk), full(sinks.shape), full(tab.shape), full(gain.shape)],
        out_specs=pl.BlockSpec((BLOCK, GROUP_WIDTH), lambda b, i: (b * nblk + i, 0)),
        out_shape=jax.ShapeDtypeStruct((T, GROUP_WIDTH), BF16),
        compiler_params=_cp("parallel", "parallel"))(proj, proj, proj, sinks, tab, gain)


def _swa_bwd(proj, sinks, tab, gain, dy, seq, name):
    T = proj.shape[0]
    nblk = seq // BLOCK

    def body(q_ref, kp_ref, kc_ref, s_ref, t_ref, g_ref, dy_ref, dq_ref, dkp_ref, dkc_ref, ds_ref, dt_ref, dg_ref):
        first = pl.program_id(1) == 0
        fn = functools.partial(_swa_fn, first=first)
        _, vjp = jax.vjp(fn, q_ref[...], kp_ref[...], kc_ref[...], s_ref[...], t_ref[...], g_ref[...])
        dq, dkp, dkc, ds, dt, dg = vjp(dy_ref[...])
        dq_ref[...] = dq.astype(BF16)
        dkp_ref[...] = dkp
        dkc_ref[...] = dkc
        _acc_out((pl.program_id(0) == 0) & first, (ds_ref, dt_ref, dg_ref), (ds, dt, dg))

    full = lambda shape: pl.BlockSpec(shape, lambda b, i: (0,) * len(shape))
    blk = lambda c: pl.BlockSpec((BLOCK, GROUP_WIDTH), lambda b, i: (b * nblk + i, c))
    return pl.pallas_call(
        body, grid=(T // seq, nblk), name=name,
        in_specs=[*_swa_specs(nblk), full(sinks.shape), full(tab.shape), full(gain.shape), blk(2)],
        out_specs=[blk(0), blk(0), blk(0), full(sinks.shape), full(tab.shape), full(gain.shape)],
        out_shape=[jax.ShapeDtypeStruct((T, GROUP_WIDTH), BF16), jax.ShapeDtypeStruct((T, GROUP_WIDTH), F32),
                   jax.ShapeDtypeStruct((T, GROUP_WIDTH), F32), jax.ShapeDtypeStruct(sinks.shape, F32),
                   jax.ShapeDtypeStruct(tab.shape, F32), jax.ShapeDtypeStruct(gain.shape, F32)],
        compiler_params=_cp("arbitrary", "arbitrary"))(proj, proj, proj, sinks, tab, gain, dy)


def _shift_add(cur, prev, seq, name):
    T = cur.shape[0]
    nblk = seq // BLOCK

    def body(c_ref, p_ref, o_ref):
        last = pl.program_id(1) == nblk - 1
        o_ref[...] = (c_ref[...] + jnp.where(last, 0.0, p_ref[...])).astype(BF16)

    return pl.pallas_call(
        body, grid=(T // seq, nblk), name=name,
        in_specs=[pl.BlockSpec((BLOCK, GROUP_WIDTH), lambda b, i: (b * nblk + i, 0)),
                  pl.BlockSpec((BLOCK, GROUP_WIDTH), lambda b, i: (b * nblk + jnp.minimum(i + 1, nblk - 1), 0))],
        out_specs=pl.BlockSpec((BLOCK, GROUP_WIDTH), lambda b, i: (b * nblk + i, 0)),
        out_shape=jax.ShapeDtypeStruct((T, GROUP_WIDTH), BF16),
        compiler_params=_cp("parallel", "parallel"))(cur, prev)


def _t5_bucket(dist):
    max_exact = N_BUCKETS // 2
    df = jnp.maximum(dist, 1).astype(F32)
    large = max_exact + (jnp.log(df / max_exact) / jnp.log(jnp.float32(MAX_DISTANCE / max_exact))
                         * (N_BUCKETS - max_exact)).astype(jnp.int32)
    return jnp.where(dist < max_exact, dist, jnp.minimum(large, N_BUCKETS - 1))


def _bucket_map():
    dist = (jnp.arange(BLOCK)[:, None] + BLOCK) - jnp.arange(2 * BLOCK)[None, :]
    return _t5_bucket(jnp.clip(dist, 0, BLOCK - 1)).astype(jnp.int32)


def _bias_table(rel_bias, buckets, name):
    def body(rb_ref, bk_ref, o_ref):
        bk = bk_ref[...]
        rb = rb_ref[...]
        for h in range(GROUP_HEADS):
            acc = jnp.zeros((BLOCK, 2 * BLOCK), F32)
            for b in range(N_BUCKETS):
                acc = jnp.where(bk == b, rb[b:b + 1, h:h + 1], acc)
            o_ref[h] = acc

    return pl.pallas_call(body, name=name, out_shape=jax.ShapeDtypeStruct((GROUP_HEADS, BLOCK, 2 * BLOCK), F32),
                          compiler_params=_cp())(rel_bias, buckets)


def _bias_table_bwd(dtab, buckets, name):
    def body(dt_ref, bk_ref, o_ref):
        bk = bk_ref[...]
        row = lax.broadcasted_iota(jnp.int32, (N_BUCKETS, GROUP_HEADS), 0)
        col = lax.broadcasted_iota(jnp.int32, (N_BUCKETS, GROUP_HEADS), 1)
        acc = jnp.zeros((N_BUCKETS, GROUP_HEADS), F32)
        for h in range(GROUP_HEADS):
            dt = dt_ref[h]
            for b in range(N_BUCKETS):
                s = jnp.sum(jnp.where(bk == b, dt, 0.0), keepdims=True)
                acc = acc + jnp.where((row == b) & (col == h), s, 0.0)
        o_ref[...] = acc

    return pl.pallas_call(body, name=name, out_shape=jax.ShapeDtypeStruct((N_BUCKETS, GROUP_HEADS), F32),
                          compiler_params=_cp())(dtab, buckets)


HEAD_ROWS = GROUP_HEADS * BLOCK


def _stack_heads(x):
    return jnp.concatenate([jnp.where(_head_mask(h, x.shape), x, 0.0) for h in range(GROUP_HEADS)], axis=0).astype(BF16)


def _sb_tile(qs, kb, q0, k0):
    z = lax.dot_general(qs, kb, _NT, preferred_element_type=F32)
    row = lax.broadcasted_iota(jnp.int32, (HEAD_ROWS, BLOCK), 0) & (BLOCK - 1)
    col = lax.broadcasted_iota(jnp.int32, (HEAD_ROWS, BLOCK), 1)
    causal = (k0 + col) < (q0 + row)
    ls_neg = -(jnp.maximum(z, 0.0) + jnp.log(1.0 + jnp.exp(-jnp.abs(z))))
    return jnp.where(causal, ls_neg, 0.0), ls_neg + z, causal


SB_DEAD = -104.0
SB_FIRST_LANE = GROUP_HEADS


def _tri(strict_upper_src):
    r = lax.broadcasted_iota(jnp.int32, (BLOCK, BLOCK), 0)
    c = lax.broadcasted_iota(jnp.int32, (BLOCK, BLOCK), 1)
    cond = {"gt": r > c, "le": r <= c, "lt": r < c}[strict_upper_src]
    return jnp.where(cond, 1.0, 0.0).astype(BF16)


def _sb_fwd(proj, gain, seq, name):
    T = proj.shape[0]
    nblk = seq // BLOCK

    def body(q_ref, k_ref, v_ref, g_ref, o_ref, raw_ref, bt_ref):
        i = pl.program_id(1)
        q = q_ref[...]
        u_gt = _tri("gt")
        lane = lax.broadcasted_iota(jnp.int32, (BLOCK, BLOCK), 1)
        heads = [slice(h * HEAD_DIM, (h + 1) * HEAD_DIM) for h in range(GROUP_HEADS)]
        rows = [slice(h * BLOCK, (h + 1) * BLOCK) for h in range(GROUP_HEADS)]
        qs = _stack_heads(q * ATT_SCALE)

        def live(carry):
            j, _, cb = carry
            return (j >= 0) & (jnp.max(cb) > SB_DEAD)

        def step(carry):
            j, accs, cb = carry
            ks = pl.multiple_of(j * BLOCK, BLOCK)
            kb = k_ref[pl.ds(ks, BLOCK), :].astype(BF16)
            vb = v_ref[pl.ds(ks, BLOCK), :].astype(BF16)
            b, a, causal = _sb_tile(qs, kb, i * BLOCK, j * BLOCK)
            tail = _split_dot(b, u_gt) + cb
            w = jnp.where(causal, jnp.exp(a + tail), 0.0).astype(BF16)
            accs = tuple(accs[h] + jnp.dot(w[rows[h]], vb[:, hs], preferred_element_type=F32)
                         for h, hs in enumerate(heads))
            return j - 1, accs, cb + jnp.sum(b, axis=1, keepdims=True)

        zero_acc = tuple(jnp.zeros((BLOCK, HEAD_DIM), F32) for _ in heads)
        j_end, accs, cb = lax.while_loop(live, step, (i, zero_acc, jnp.zeros((HEAD_ROWS, 1), F32)))
        side = jnp.where(lane == SB_FIRST_LANE, (j_end + 1).astype(F32), 0.0)
        for h in range(GROUP_HEADS):
            side = jnp.where(lane == h, cb[rows[h]], side)
        raw = jnp.concatenate(accs, axis=1)
        raw_ref[...] = raw
        bt_ref[...] = side
        o_ref[...] = _rms(raw, g_ref[...]).astype(BF16)

    return pl.pallas_call(
        body, grid=(T // seq, nblk), name=name,
        in_specs=[pl.BlockSpec((BLOCK, GROUP_WIDTH), lambda b, i: (b * nblk + i, 5)),
                  pl.BlockSpec((seq, GROUP_WIDTH), lambda b, i: (b, 6)),
                  pl.BlockSpec((seq, GROUP_WIDTH), lambda b, i: (b, 7)),
                  pl.BlockSpec(gain.shape, lambda b, i: (0, 0))],
        out_specs=[pl.BlockSpec((BLOCK, GROUP_WIDTH), lambda b, i: (b * nblk + i, 0)),
                   pl.BlockSpec((BLOCK, GROUP_WIDTH), lambda b, i: (b * nblk + i, 0)),
                   pl.BlockSpec((BLOCK, BLOCK), lambda b, i: (b * nblk + i, 0))],
        out_shape=[jax.ShapeDtypeStruct((T, GROUP_WIDTH), BF16), jax.ShapeDtypeStruct((T, GROUP_WIDTH), F32),
                   jax.ShapeDtypeStruct((T, BLOCK), F32)],
        compiler_params=_cp("parallel", "parallel"))(proj, proj, proj, gain)


def _sb_bwd(proj, gain, raw, btot, dy, seq, name):
    T = proj.shape[0]
    nblk = seq // BLOCK

    def body(q_ref, k_ref, v_ref, g_ref, raw_ref, bt_ref, dy_ref, dq_ref, dk_ref, dv_ref, dg_ref):
        i = pl.program_id(1)

        @pl.when(i == 0)
        def _():
            dk_ref[...] = jnp.zeros_like(dk_ref)
            dv_ref[...] = jnp.zeros_like(dv_ref)

        rawv = raw_ref[...]
        _, vjp = jax.vjp(_rms, rawv, g_ref[...])
        do, dg = vjp(dy_ref[...])
        _acc_out((pl.program_id(0) == 0) & (i == 0), (dg_ref,), (dg,))
        q = q_ref[...]
        bt = bt_ref[...]
        u_le = _tri("le")
        u_lt = _tri("lt")
        heads = [slice(h * HEAD_DIM, (h + 1) * HEAD_DIM) for h in range(GROUP_HEADS)]
        rows = [slice(h * BLOCK, (h + 1) * BLOCK) for h in range(GROUP_HEADS)]
        qs = _stack_heads(q * ATT_SCALE)
        dos = _stack_heads(do)
        bts = jnp.concatenate([bt[:, h:h + 1] for h in range(GROUP_HEADS)], axis=0)
        first = jnp.max(bt[:, SB_FIRST_LANE:SB_FIRST_LANE + 1]).astype(jnp.int32)
        first = jnp.minimum(jnp.maximum(first, 0), i)

        def step(j, carry):
            dqs, cb, cg = carry
            ks = pl.multiple_of(j * BLOCK, BLOCK)
            kb = k_ref[pl.ds(ks, BLOCK), :].astype(BF16)
            vb = v_ref[pl.ds(ks, BLOCK), :].astype(BF16)
            b, a, causal = _sb_tile(qs, kb, i * BLOCK, j * BLOCK)
            tail = bts - (_split_dot(b, u_le) + cb)
            w = jnp.where(causal, jnp.exp(a + tail), 0.0)
            sig = jnp.exp(a)
            g = w * lax.dot_general(dos, vb, _NT, preferred_element_type=F32)
            gpre = _split_dot(g, u_lt) + cg
            dz = jnp.where(causal, g * (1.0 - sig) - gpre * sig, 0.0).astype(BF16)
            dqs = tuple(dqs[h] + jnp.dot(dz[rows[h]], kb[:, hs], preferred_element_type=F32)
                        for h, hs in enumerate(heads))
            dk_ref[pl.ds(ks, BLOCK), :] += lax.dot_general(dz, qs, _TN, preferred_element_type=F32)
            dv_ref[pl.ds(ks, BLOCK), :] += lax.dot_general(w.astype(BF16), dos, _TN, preferred_element_type=F32)
            return dqs, cb + jnp.sum(b, axis=1, keepdims=True), cg + jnp.sum(g, axis=1, keepdims=True)

        zero_dq = tuple(jnp.zeros((BLOCK, HEAD_DIM), F32) for _ in heads)
        zero = jnp.zeros((HEAD_ROWS, 1), F32)
        dqs, _, _ = lax.fori_loop(first, i + 1, step, (zero_dq, zero, zero))
        dq_ref[...] = (jnp.concatenate(dqs, axis=1) * ATT_SCALE).astype(BF16)

    blk = lambda c: pl.BlockSpec((BLOCK, GROUP_WIDTH), lambda b, i: (b * nblk + i, c))
    seqblk = lambda c: pl.BlockSpec((seq, GROUP_WIDTH), lambda b, i: (b, c))
    vec = pl.BlockSpec(gain.shape, lambda b, i: (0, 0))
    return pl.pallas_call(
        body, grid=(T // seq, nblk), name=name,
        in_specs=[blk(5), seqblk(6), seqblk(7), vec, blk(0), pl.BlockSpec((BLOCK, BLOCK), lambda b, i: (b * nblk + i, 0)),
                  blk(3)],
        out_specs=[blk(0), seqblk(0), seqblk(0), vec],
        out_shape=[jax.ShapeDtypeStruct((T, GROUP_WIDTH), BF16), jax.ShapeDtypeStruct((T, GROUP_WIDTH), F32),
                   jax.ShapeDtypeStruct((T, GROUP_WIDTH), F32), jax.ShapeDtypeStruct(gain.shape, F32)],
        compiler_params=_cp("arbitrary", "arbitrary"))(proj, proj, proj, gain, raw, btot, dy)


def _layer_params(l, sgu_w, sgu_b, pool_w, pool_scale, swa_sinks, mix_out_gain, norm_mix, norm_ffn):
    gains = mix_out_gain[l].reshape(4, 1, GROUP_WIDTH)
    return dict(
        wm=sgu_w[l], bt=sgu_b[l].T,
        wbd=jax.scipy.linalg.block_diag(*[pool_w[l, g] for g in range(len(POOL_WINDOWS))]),
        scale=pool_scale[l][None], sinks=swa_sinks[l][None],
        gain=[gains[m] for m in range(4)], norm_mix=norm_mix[l][None], norm_ffn=norm_ffn[l][None])


def _layer_fwd(l, x, p, w, tab, seq):
    win, wout, wgu, wd = w
    proj, h1 = _norm_mm(x, p["norm_mix"], win, l, 4, F32, f"in_proj_{l}")
    ya = _sgu_fwd(proj, p["wm"], p["bt"], p["gain"][0], f"sgu_fwd_{l}")
    yb = _pool_fwd(proj, p["wbd"], p["scale"], p["gain"][1], seq, f"pool_fwd_{l}")
    yc = _swa_fwd(proj, p["sinks"], tab, p["gain"][2], seq, f"swa_fwd_{l}")
    yd, raw, btot = _sb_fwd(proj, p["gain"][3], seq, f"sb_fwd_{l}")
    ycat = jnp.concatenate([ya, yb, yc, yd], axis=1)
    xm = _mm_res(x, ycat, wout, l, f"out_proj_{l}", tn=D_MODEL)
    gu, h2 = _norm_mm(xm, p["norm_ffn"], wgu, l, 1, BF16, f"ffn_up_{l}")
    act = _swiglu_fwd(gu, f"swiglu_fwd_{l}")
    xo = _mm_res(xm, act, wd, l, f"ffn_down_{l}", tn=D_MODEL // 2)
    return xo, (x, proj, h1, ycat, raw, btot, xm, gu, h2, act)


def _layer_bwd(l, dxo, saved, p, w, tab, seq):
    win, wout, wgu, wd = w
    x, proj, h1, ycat, raw, btot, xm, gu, h2, act = saved
    dact = _mm_nt(dxo, wd, l, BF16, f"ffn_down_dx_{l}")
    g_wd = _mm_tn_rows(act, dxo, FF_PAD, f"ffn_down_dw_{l}")
    dgu = _swiglu_bwd(gu, dact, f"swiglu_bwd_{l}")
    g_wgu = _mm_tn_cols(h2, dgu, FF_PAD, 1, f"ffn_up_dw_{l}")
    dh2 = _mm_nt_acc(dgu, wgu, l, 1, f"ffn_up_dx_{l}")
    dxm, g_norm_ffn = _rms_bwd_add(xm, p["norm_ffn"], dh2, dxo, f"norm_ffn_bwd_{l}")
    dycat = _mm_nt(dxm, wout, l, F32, f"out_proj_dx_{l}")
    g_wout = _mm_tn_rows(ycat, dxm, D_MODEL, f"out_proj_dw_{l}")
    duv, g_wm, g_bt, g_ga = _sgu_bwd(proj, p["wm"], p["bt"], p["gain"][0], dycat, f"sgu_bwd_{l}")
    dp, g_wbd, g_scale, g_gb = _pool_bwd(proj, p["wbd"], p["scale"], p["gain"][1], dycat, seq, f"pool_bwd_{l}")
    dq, dkp, dkc, g_sinks, g_tab, g_gc = _swa_bwd(proj, p["sinks"], tab, p["gain"][2], dycat, seq, f"swa_bwd_{l}")
    dkv = _shift_add(dkc, dkp, seq, f"swa_dkv_{l}")
    dqd, dkd, dvd, g_gd = _sb_bwd(proj, p["gain"][3], raw, btot, dycat, seq, f"sb_bwd_{l}")
    dproj = jnp.concatenate([duv, dp, dq, dkv, dqd, dkd.astype(BF16), dvd.astype(BF16)], axis=1)
    g_win = _mm_tn_cols(h1, dproj, GROUP_WIDTH, 4, f"in_proj_dw_{l}")
    dh1 = _mm_nt_acc(dproj, win, l, 4, f"in_proj_dx_{l}")
    dx, g_norm_mix = _rms_bwd_add(x, p["norm_mix"], dh1, dxm, f"norm_mix_bwd_{l}")
    ng = len(POOL_WINDOWS)
    gd = GROUP_WIDTH // ng
    small = dict(
        sgu_w=g_wm, sgu_b=g_bt.T,
        pool_w=jnp.stack([g_wbd[g * gd:(g + 1) * gd, g * gd:(g + 1) * gd] for g in range(ng)]),
        pool_scale=g_scale[0], swa_sinks=g_sinks[0],
        mix_out_gain=jnp.concatenate([g_ga[0], g_gb[0], g_gc[0], g_gd[0]]),
        norm_mix=g_norm_mix[0], norm_ffn=g_norm_ffn[0])
    return dx, (g_win, g_wout, g_wgu, g_wd), small, g_tab


def _local_step(x, target, weights, sgu_w, sgu_b, pool_w, pool_scale, swa_sinks, rel_bias, mix_out_gain,
                norm_mix, norm_ffn, norm_final, seq):
    buckets = _bucket_map()
    tab = _bias_table(rel_bias, buckets, "bias_table")
    params = [_layer_params(l, sgu_w, sgu_b, pool_w, pool_scale, swa_sinks, mix_out_gain, norm_mix, norm_ffn)
              for l in range(DEPTH)]
    saved = []
    for l in range(DEPTH):
        x, s = _layer_fwd(l, x, params[l], weights, tab, seq)
        saved.append(s)
    loss, dx, g_final = _loss_head(x, norm_final[None], target, "loss_head")
    big, small, g_tab = [None] * DEPTH, [None] * DEPTH, None
    for l in reversed(range(DEPTH)):
        dx, big[l], small[l], t = _layer_bwd(l, dx, saved[l], params[l], weights, tab, seq)
        g_tab = t if g_tab is None else g_tab + t
    g_small = {k: jnp.stack([small[l][k] for l in range(DEPTH)]) for k in small[0]}
    g_small["rel_bias"] = _bias_table_bwd(g_tab, buckets, "bias_table_bwd")
    g_small["norm_final"] = g_final[0]
    return loss, dx, big, g_small


def _cast_pad(w, cols, name):
    L, r, c = w.shape

    def body(w_ref, o_ref):
        if cols != c:
            o_ref[...] = jnp.zeros_like(o_ref)
        o_ref[:, :c] = w_ref[...].astype(BF16)

    return pl.pallas_call(
        body, grid=(L,), name=name,
        in_specs=[pl.BlockSpec((None, r, c), lambda l: (l, 0, 0))],
        out_specs=pl.BlockSpec((None, r, cols), lambda l: (l, 0, 0)),
        out_shape=jax.ShapeDtypeStruct((L, r, cols), BF16),
        compiler_params=_cp("parallel"))(w)


def _my_place():
    x, y, c = (lax.axis_index(a) for a in MESH_AXES)
    return x, y, c, 4 * x + 2 * y + c


ANY = pl.BlockSpec(memory_space=pl.ANY)
MESH = pl.DeviceIdType.MESH
DOWN_ROWS = D_FF // N_DEV


def _down_rows(ref, k):
    return ref.at[:, k // 2, pl.ds((k % 2) * DOWN_ROWS, DOWN_ROWS), :]


def _all_gather_weights(s_in, s_out, s_gu, s_down):
    L = s_in.shape[0]
    shards = (s_in, s_out, s_gu, s_down)
    down_full = jnp.zeros((L, 4, FF_PAD, D_MODEL), BF16)
    n = len(shards)

    def body(i0, i1, i2, i3, _, o0, o1, o2, o3, send_sems, recv_sems, local_sems):
        srcs = (i0, i1, i2, i3)
        outs = (o0, o1, o2, o3)
        x, y, c, _me = _my_place()
        chips = [(1 - x, y), (x, 1 - y), (1 - x, 1 - y)]

        def place(t, dev):
            k = 4 * dev[0] + 2 * dev[1] + dev[2]
            return _down_rows(outs[t], k) if t == 3 else outs[t].at[:, k]

        def copy(slot, t, block, to, from_shard):
            return pltpu.make_async_remote_copy(
                src_ref=srcs[t] if from_shard else place(t, block), dst_ref=place(t, block),
                send_sem=send_sems.at[slot, t], recv_sem=recv_sems.at[slot, t], device_id=to, device_id_type=MESH)

        me, sibling = (x, y, c), (x, y, 1 - c)
        mine = [pltpu.make_async_copy(srcs[t], place(t, me), local_sems.at[t]) for t in range(n)]
        for cp in mine:
            cp.start()
        first = [copy(0, t, me, sibling, True) for t in range(n)]
        first += [copy(1 + j, t, me, (*chip, c), True) for j, chip in enumerate(chips) for t in range(n)]
        for cp in first:
            cp.start()
        passed = []
        for j, chip in enumerate(chips):
            for t in range(n):
                copy(1 + j, t, (*chip, c), me, False).wait_recv()
                cp = copy(4 + j, t, (*chip, c), sibling, False)
                cp.start()
                passed.append(cp)
        for t in range(n):
            copy(0, t, sibling, me, False).wait_recv()
        for j, chip in enumerate(chips):
            for t in range(n):
                copy(4 + j, t, (*chip, 1 - c), me, False).wait_recv()
        for cp in first + passed:
            cp.wait_send()
        for cp in mine:
            cp.wait()

    shapes = [jax.ShapeDtypeStruct((L, N_DEV) + s.shape[1:], BF16) for s in shards[:3]]
    shapes.append(jax.ShapeDtypeStruct(down_full.shape, BF16))
    return pl.pallas_call(
        body, name="all_gather_weights", out_shape=shapes,
        in_specs=[ANY] * 5, out_specs=[ANY] * 4, input_output_aliases={4: 3},
        scratch_shapes=[pltpu.SemaphoreType.DMA((7, n)), pltpu.SemaphoreType.DMA((7, n)), pltpu.SemaphoreType.DMA((n,))],
        )(*shards, down_full)


def _peer(x, y, c, d):
    return (x ^ (d >> 2), y ^ ((d >> 1) & 1), c ^ (d & 1))


def _reduce_scatter_grads(big):
    L = len(big)
    n = 4
    out_rows = D_MODEL // N_DEV

    def body(*refs):
        ins = refs[:L * n]
        outs = refs[L * n:L * n + n]
        send_sems, recv_sems, local_sems = refs[L * n + n:]
        x, y, c, me = _my_place()

        def block(l, t, k):
            g = ins[l * n + t]
            if t == 1:
                return g.at[pl.ds(k * out_rows, out_rows), :]
            if t == 3:
                return g.at[pl.ds((k // 2) * FF_PAD + (k % 2) * DOWN_ROWS, DOWN_ROWS), :]
            return g.at[k]

        local = [pltpu.make_async_copy(block(l, t, me), outs[t].at[l, me], local_sems.at[l, t])
                 for l in range(L) for t in range(n)]
        for cp in local:
            cp.start()
        sends = []
        for d in range(1, N_DEV):
            px, py, pc = _peer(x, y, c, d)
            pk = 4 * px + 2 * py + pc
            for l in range(L):
                for t in range(n):
                    sends.append(pltpu.make_async_remote_copy(
                        src_ref=block(l, t, pk), dst_ref=outs[t].at[l, me],
                        send_sem=send_sems.at[d - 1, l, t], recv_sem=recv_sems.at[d - 1, l, t],
                        device_id=(px, py, pc), device_id_type=MESH))
        for cp in sends:
            cp.start()
        for d in range(1, N_DEV):
            px, py, pc = _peer(x, y, c, d)
            pk = 4 * px + 2 * py + pc
            for l in range(L):
                for t in range(n):
                    pltpu.make_async_remote_copy(
                        src_ref=block(l, t, me), dst_ref=outs[t].at[l, pk],
                        send_sem=send_sems.at[d - 1, l, t], recv_sem=recv_sems.at[d - 1, l, t],
                        device_id=(px, py, pc), device_id_type=MESH).wait_recv()
        for cp in sends:
            cp.wait_send()
        for cp in local:
            cp.wait()

    shapes = [jax.ShapeDtypeStruct((L, N_DEV, D_MODEL, GROUP_WIDTH), BF16),
              jax.ShapeDtypeStruct((L, N_DEV, out_rows, D_MODEL), BF16),
              jax.ShapeDtypeStruct((L, N_DEV, D_MODEL, FF_PAD), BF16),
              jax.ShapeDtypeStruct((L, N_DEV, DOWN_ROWS, D_MODEL), BF16)]
    flat = [g for layer in big for g in layer]
    return pl.pallas_call(
        body, name="reduce_scatter_grads", out_shape=shapes,
        in_specs=[ANY] * len(flat), out_specs=[ANY] * n,
        scratch_shapes=[pltpu.SemaphoreType.DMA((N_DEV - 1, L, n)), pltpu.SemaphoreType.DMA((N_DEV - 1, L, n)),
                        pltpu.SemaphoreType.DMA((L, n))],
        )(*flat)


def _all_reduce_small(part):
    rows = part.shape[0]

    def body(p_ref, o_ref, buf, send_sems, recv_sems):
        x, y, c, me = _my_place()
        buf[me] = p_ref[...]
        sends = []
        for d in range(1, N_DEV):
            peer = _peer(x, y, c, d)
            sends.append(pltpu.make_async_remote_copy(
                src_ref=p_ref, dst_ref=buf.at[me], send_sem=send_sems.at[d - 1], recv_sem=recv_sems.at[d - 1],
                device_id=peer, device_id_type=MESH))
        for cp in sends:
            cp.start()
        for d in range(1, N_DEV):
            px, py, pc = _peer(x, y, c, d)
            pltpu.make_async_remote_copy(
                src_ref=p_ref, dst_ref=buf.at[4 * px + 2 * py + pc], send_sem=send_sems.at[d - 1],
                recv_sem=recv_sems.at[d - 1], device_id=(px, py, pc), device_id_type=MESH).wait_recv()
        for cp in sends:
            cp.wait_send()
        total = buf[0]
        for k in range(1, N_DEV):
            total = total + buf[k]
        o_ref[...] = total

    return pl.pallas_call(
        body, name="all_reduce_small", out_shape=jax.ShapeDtypeStruct((rows, 128), F32),
        in_specs=[pl.BlockSpec(memory_space=pltpu.VMEM)], out_specs=pl.BlockSpec(memory_space=pltpu.VMEM),
        scratch_shapes=[pltpu.VMEM((N_DEV, rows, 128), F32), pltpu.SemaphoreType.DMA((N_DEV - 1,)),
                        pltpu.SemaphoreType.DMA((N_DEV - 1,))],
        compiler_params=pltpu.CompilerParams(vmem_limit_bytes=VMEM_LIMIT))(part)


def _adamw(w, g, m, v):
    m = ADAM_B1 * m + (1.0 - ADAM_B1) * g
    v = ADAM_B2 * v + (1.0 - ADAM_B2) * jnp.square(g)
    m_hat = m / (1.0 - ADAM_B1 ** ADAM_STEP)
    v_hat = v / (1.0 - ADAM_B2 ** ADAM_STEP)
    delta = -ADAM_LR * (m_hat / (jnp.sqrt(v_hat) + ADAM_EPS) + ADAM_WD * w)
    return delta, m, v


def _adamw_sharded(parts, w, m, v, tr, name):
    L, r, c = w.shape
    cp = parts.shape[-1]

    def body(p_ref, w_ref, m_ref, v_ref, g_ref, d_ref, nm_ref, nv_ref):
        g = p_ref[0, :, :c].astype(F32)
        for k in range(1, N_DEV):
            g = g + p_ref[k, :, :c].astype(F32)
        delta, nm, nv = _adamw(w_ref[...], g, m_ref[...], v_ref[...])
        g_ref[...] = g
        d_ref[...] = delta
        nm_ref[...] = nm
        nv_ref[...] = nv

    blk = pl.BlockSpec((None, tr, c), lambda l, i: (l, i, 0))
    out = jax.ShapeDtypeStruct((L, r, c), F32)
    return pl.pallas_call(
        body, grid=(L, r // tr), name=name,
        in_specs=[pl.BlockSpec((None, N_DEV, tr, cp), lambda l, i: (l, 0, i, 0)), blk, blk, blk],
        out_specs=[blk] * 4, out_shape=[out] * 4,
        compiler_params=_cp("parallel", "parallel"))(parts, w, m, v)


def _adamw_small(g, w, m, v, name):
    def body(g_ref, w_ref, m_ref, v_ref, d_ref, nm_ref, nv_ref):
        delta, nm, nv = _adamw(w_ref[...], g_ref[...], m_ref[...], v_ref[...])
        d_ref[...] = delta
        nm_ref[...] = nm
        nv_ref[...] = nv

    out = jax.ShapeDtypeStruct(g.shape, F32)
    return pl.pallas_call(body, name=name, out_shape=[out] * 3, compiler_params=_cp())(g, w, m, v)


SMALL = ("sgu_w", "sgu_b", "pool_w", "pool_scale", "swa_sinks", "rel_bias", "mix_out_gain", "norm_mix", "norm_ffn",
         "norm_final")


def _pack(parts, rows):
    flat = jnp.concatenate([p.reshape(-1) for p in parts])
    return jnp.pad(flat, (0, rows * 128 - flat.shape[0])).reshape(rows, 128)


def _unpack(buf, like):
    flat = buf.reshape(-1)
    out, at = [], 0
    for a in like:
        out.append(flat[at:at + a.size].reshape(a.shape))
        at += a.size
    return out


def kernel(x, w_in, w_out, sgu_w, sgu_b, pool_w, pool_scale, swa_sinks, rel_bias, mix_out_gain, norm_mix, norm_ffn, w_gate_up, w_down, norm_final, loss_target, m_w_in, m_w_out, m_sgu_w, m_sgu_b, m_pool_w, m_pool_scale, m_swa_sinks, m_rel_bias, m_mix_out_gain, m_norm_mix, m_norm_ffn, m_w_gate_up, m_w_down, m_norm_final, v_w_in, v_w_out, v_sgu_w, v_sgu_b, v_pool_w, v_pool_scale, v_swa_sinks, v_rel_bias, v_mix_out_gain, v_norm_mix, v_norm_ffn, v_w_gate_up, v_w_down, v_norm_final):
    bl, seq, _ = x.shape
    L = w_in.shape[0]
    full_in, full_out, full_gu, full_down = _all_gather_weights(
        _cast_pad(w_in, GROUP_WIDTH, "shard_w_in"), _cast_pad(w_out, D_MODEL, "shard_w_out"),
        _cast_pad(w_gate_up, FF_PAD, "shard_w_gate_up"), _cast_pad(w_down, D_MODEL, "shard_w_down"))
    weights = (full_in, full_out.reshape(L, D_MODEL, D_MODEL), full_gu, full_down.reshape(L, D_FF_PAD, D_MODEL))
    small_w = dict(sgu_w=sgu_w, sgu_b=sgu_b, pool_w=pool_w, pool_scale=pool_scale, swa_sinks=swa_sinks,
                   rel_bias=rel_bias, mix_out_gain=mix_out_gain, norm_mix=norm_mix, norm_ffn=norm_ffn,
                   norm_final=norm_final)
    small_m = dict(sgu_w=m_sgu_w, sgu_b=m_sgu_b, pool_w=m_pool_w, pool_scale=m_pool_scale, swa_sinks=m_swa_sinks,
                   rel_bias=m_rel_bias, mix_out_gain=m_mix_out_gain, norm_mix=m_norm_mix, norm_ffn=m_norm_ffn,
                   norm_final=m_norm_final)
    small_v = dict(sgu_w=v_sgu_w, sgu_b=v_sgu_b, pool_w=v_pool_w, pool_scale=v_pool_scale, swa_sinks=v_swa_sinks,
                   rel_bias=v_rel_bias, mix_out_gain=v_mix_out_gain, norm_mix=v_norm_mix, norm_ffn=v_norm_ffn,
                   norm_final=v_norm_final)
    loss, dx, big, g_small = _local_step(
        x.reshape(bl * seq, D_MODEL), loss_target.reshape(bl * seq, D_MODEL), weights, sgu_w, sgu_b, pool_w, pool_scale,
        swa_sinks, rel_bias, mix_out_gain, norm_mix, norm_ffn, norm_final, seq)
    p_in, p_out, p_gu, p_down = _reduce_scatter_grads(big)
    outs_in = _adamw_sharded(p_in, w_in, m_w_in, v_w_in, 256, "adamw_w_in")
    outs_out = _adamw_sharded(p_out, w_out, m_w_out, v_w_out, D_MODEL // N_DEV, "adamw_w_out")
    outs_gu = _adamw_sharded(p_gu, w_gate_up, m_w_gate_up, v_w_gate_up, 256, "adamw_w_gate_up")
    outs_down = _adamw_sharded(p_down, w_down, m_w_down, v_w_down, DOWN_ROWS // 2, "adamw_w_down")
    sizes = sum(small_w[k].size for k in SMALL) + 128
    rows = -(-sizes // 1024) * 8
    packed = _pack([g_small[k] for k in SMALL] + [loss[0]], rows)
    total = _all_reduce_small(packed)
    like = [small_w[k] for k in SMALL]
    d_s, m_s, v_s = _adamw_small(total, _pack(like, rows), _pack([small_m[k] for k in SMALL], rows),
                                 _pack([small_v[k] for k in SMALL], rows), "adamw_small")
    g_list = dict(zip(SMALL, _unpack(total, like)))
    d_list = dict(zip(SMALL, _unpack(d_s, like)))
    m_list = dict(zip(SMALL, _unpack(m_s, like)))
    v_list = dict(zip(SMALL, _unpack(v_s, like)))
    loss_total = total.reshape(-1)[sum(a.size for a in like)]
    big_outs = dict(w_in=outs_in, w_out=outs_out, w_gate_up=outs_gu, w_down=outs_down)
    order = ("w_in", "w_out", "sgu_w", "sgu_b", "pool_w", "pool_scale", "swa_sinks", "rel_bias", "mix_out_gain",
             "norm_mix", "norm_ffn", "w_gate_up", "w_down", "norm_final")
    result = [loss_total, dx.reshape(bl, seq, D_MODEL)]
    for which, small in enumerate((g_list, d_list, m_list, v_list)):
        for name in order:
            result.append(big_outs[name][which] if name in big_outs else small[name])
    return tuple(result)
```

```python
import functools

import jax
import jax.numpy as jnp
from jax import lax
from jax.experimental import pallas as pl
from jax.experimental.pallas import tpu as pltpu

F32 = jnp.float32
BF16 = jnp.bfloat16

N_DEV = 8
DEPTH = 4
D_MODEL = 1024
GROUP_WIDTH = 256
HEAD_DIM = 64
GROUP_HEADS = 4
BLOCK = 128
N_BUCKETS = 32
MAX_DISTANCE = 128
POOL_WINDOWS = (2, 4, 8, 16)
D_FF = 2816
FF_SHARD = D_FF // 4
FF_PAD = 768
D_FF_PAD = 4 * FF_PAD
EPS = 1e-6
ATT_SCALE = HEAD_DIM ** -0.5
ADAM_LR = 0.001
ADAM_B1 = 0.9
ADAM_B2 = 0.999
ADAM_EPS = 1e-08
ADAM_WD = 0.01
ADAM_STEP = 10
VMEM_LIMIT = 56 * 1024 * 1024
MESH_AXES = ("x", "y", "c")


def _cp(*sem):
    return pltpu.CompilerParams(dimension_semantics=sem or None, vmem_limit_bytes=VMEM_LIMIT)


_NT = (((1,), (1,)), ((), ()))
_TN = (((0,), (0,)), ((), ()))


@jax.custom_vjp
def _bdot(a, b):
    return jnp.dot(a.astype(BF16), b.astype(BF16), preferred_element_type=F32)


def _bdot_fwd(a, b):
    return _bdot(a, b), (a.astype(BF16), b.astype(BF16))


def _bdot_bwd(res, ct):
    a, b = res
    c = ct.astype(BF16)
    return (lax.dot_general(c, b, _NT, preferred_element_type=F32),
            lax.dot_general(a, c, _TN, preferred_element_type=F32))


_bdot.defvjp(_bdot_fwd, _bdot_bwd)


@jax.custom_vjp
def _bdot_nt(a, b):
    return lax.dot_general(a.astype(BF16), b.astype(BF16), _NT, preferred_element_type=F32)


def _bdot_nt_fwd(a, b):
    return _bdot_nt(a, b), (a.astype(BF16), b.astype(BF16))


def _bdot_nt_bwd(res, ct):
    a, b = res
    c = ct.astype(BF16)
    return (jnp.dot(c, b, preferred_element_type=F32),
            lax.dot_general(c, a, _TN, preferred_element_type=F32))


_bdot_nt.defvjp(_bdot_nt_fwd, _bdot_nt_bwd)


def _rms(x, g):
    return x * lax.rsqrt(jnp.mean(x * x, axis=-1, keepdims=True) + EPS) * g


def _split_dot(x, u):
    hi = x.astype(BF16)
    lo = (x - hi.astype(F32)).astype(BF16)
    return jnp.dot(hi, u, preferred_element_type=F32) + jnp.dot(lo, u, preferred_element_type=F32)


def _head_mask(h, shape):
    col = lax.broadcasted_iota(jnp.int32, shape, 1)
    return (col >= h * HEAD_DIM) & (col < (h + 1) * HEAD_DIM)


ANY = pl.BlockSpec(memory_space=pl.ANY)
MESH = pl.DeviceIdType.MESH
DOWN_ROWS = D_FF // N_DEV


def _my_place():
    x, y, c = (lax.axis_index(a) for a in MESH_AXES)
    return x, y, c, 4 * x + 2 * y + c


def _peer(x, y, c, d):
    return (x ^ (d >> 2), y ^ ((d >> 1) & 1), c ^ (d & 1))


class _Item:
    def __init__(self, src, block, dst_shape, place=None, init=None):
        self.src, self.block, self.dst_shape, self.init = src, block, dst_shape, init
        self.place = place or (lambda dst, k: dst.at[k])


def _call(body, args, *, grid, in_specs, out_specs, out_shape, sem, name, scratch_shapes=(), items=()):
    if not items:
        outs = pl.pallas_call(body, grid=grid, in_specs=in_specs, out_specs=out_specs, out_shape=out_shape, name=name,
                              scratch_shapes=list(scratch_shapes), compiler_params=_cp(*sem))(*args)
        return list(outs) if isinstance(outs, (list, tuple)) else [outs]
    n_in, n_out, n_scr, n = len(in_specs), len(out_specs), len(scratch_shapes), len(items)
    inits = [i for i, it in enumerate(items) if it.init is not None]

    def wrapped(*refs):
        core_in, srcs = refs[:n_in], refs[n_in:n_in + n]
        off = n_in + n + len(inits)
        core_out, dsts = refs[off:off + n_out], refs[off + n_out:off + n_out + n]
        scratch = refs[off + n_out + n:]
        send_sems, recv_sems, local_sems = scratch[n_scr:]
        ids = [pl.program_id(a) for a in range(len(grid))]
        first = functools.reduce(jnp.logical_and, [i == 0 for i in ids])
        last = functools.reduce(jnp.logical_and, [i == g - 1 for i, g in zip(ids, grid)])
        x, y, c, me = _my_place()

        def local(i):
            return pltpu.make_async_copy(items[i].block(srcs[i], me), items[i].place(dsts[i], me), local_sems.at[i])

        def remote(d, i, sending):
            px, py, pc = _peer(x, y, c, d)
            pk = 4 * px + 2 * py + pc
            return pltpu.make_async_remote_copy(
                src_ref=items[i].block(srcs[i], pk), dst_ref=items[i].place(dsts[i], me if sending else pk),
                send_sem=send_sems.at[d - 1, i], recv_sem=recv_sems.at[d - 1, i],
                device_id=(px, py, pc), device_id_type=MESH)

        @pl.when(first)
        def _():
            for i in range(n):
                local(i).start()
            for d in range(1, N_DEV):
                for i in range(n):
                    remote(d, i, True).start()

        body(*core_in, *core_out, *scratch[:n_scr])

        @pl.when(last)
        def _():
            for d in range(1, N_DEV):
                for i in range(n):
                    remote(d, i, False).wait_recv()
            for d in range(1, N_DEV):
                for i in range(n):
                    remote(d, i, True).wait_send()
            for i in range(n):
                local(i).wait()

    outs = pl.pallas_call(
        wrapped, grid=grid, name=name,
        in_specs=list(in_specs) + [ANY] * (n + len(inits)), out_specs=list(out_specs) + [ANY] * n,
        out_shape=list(out_shape) + [jax.ShapeDtypeStruct(it.dst_shape, BF16) for it in items],
        input_output_aliases={n_in + n + j: n_out + i for j, i in enumerate(inits)},
        scratch_shapes=list(scratch_shapes) + [pltpu.SemaphoreType.DMA((N_DEV - 1, n)),
                                               pltpu.SemaphoreType.DMA((N_DEV - 1, n)), pltpu.SemaphoreType.DMA((n,))],
        compiler_params=_cp(*(["arbitrary"] * len(grid))),
    )(*args, *[it.src for it in items], *[items[i].init for i in inits])
    return list(outs)


def _norm_mm(x, g, w, l, jb, out_dtype, name, tm=1024, items=()):
    T, K = x.shape
    _, nb, _, tn = w.shape
    tm = min(tm, T)

    def body(x_ref, g_ref, w_ref, o_ref, h_ref):
        @pl.when(pl.program_id(1) == 0)
        def _():
            h_ref[...] = _rms(x_ref[...], g_ref[...]).astype(BF16)
        h = h_ref[...]
        for jj in range(jb):
            o_ref[:, jj * tn:(jj + 1) * tn] = jnp.dot(h, w_ref[jj], preferred_element_type=F32).astype(o_ref.dtype)

    return _call(
        body, (x, g, w), grid=(T // tm, nb // jb), name=name, items=items,
        in_specs=[pl.BlockSpec((tm, K), lambda i, j: (i, 0)), pl.BlockSpec((1, K), lambda i, j: (0, 0)),
                  pl.BlockSpec((None, jb, K, tn), lambda i, j: (l, j, 0, 0))],
        out_specs=[pl.BlockSpec((tm, jb * tn), lambda i, j: (i, j)), pl.BlockSpec((tm, K), lambda i, j: (i, 0))],
        out_shape=[jax.ShapeDtypeStruct((T, nb * tn), out_dtype), jax.ShapeDtypeStruct((T, K), BF16)],
        sem=("parallel", "arbitrary"))


def _mm_res(res, a, w, l, name, tn, tm=1024, items=()):
    T, K = a.shape
    N = w.shape[2]
    tm = min(tm, T)

    def body(r_ref, a_ref, w_ref, o_ref):
        o_ref[...] = r_ref[...] + jnp.dot(a_ref[...], w_ref[...], preferred_element_type=F32)

    return _call(
        body, (res, a, w), grid=(T // tm, N // tn), name=name, items=items,
        in_specs=[pl.BlockSpec((tm, tn), lambda i, j: (i, j)), pl.BlockSpec((tm, K), lambda i, j: (i, 0)),
                  pl.BlockSpec((None, K, tn), lambda i, j: (l, 0, j))],
        out_specs=[pl.BlockSpec((tm, tn), lambda i, j: (i, j))],
        out_shape=[jax.ShapeDtypeStruct((T, N), F32)],
        sem=("parallel", "parallel"))


def _mm_nt(a, w, l, out_dtype, name, tm=1024, tn=1024):
    T, K = a.shape
    N = w.shape[1]
    tm = min(tm, T)

    def body(a_ref, w_ref, o_ref):
        o_ref[...] = lax.dot_general(a_ref[...].astype(BF16), w_ref[...], _NT,
                                     preferred_element_type=F32).astype(o_ref.dtype)

    return pl.pallas_call(
        body, grid=(T // tm, N // tn), name=name,
        in_specs=[pl.BlockSpec((tm, K), lambda i, j: (i, 0)), pl.BlockSpec((None, tn, K), lambda i, j: (l, j, 0))],
        out_specs=pl.BlockSpec((tm, tn), lambda i, j: (i, j)),
        out_shape=jax.ShapeDtypeStruct((T, N), out_dtype),
        compiler_params=_cp("parallel", "parallel"))(a, w)


def _mm_nt_acc(a, w, l, jb, name, tm=1024, items=()):
    T = a.shape[0]
    _, nb, K, tn = w.shape
    tm = min(tm, T)

    def body(a_ref, w_ref, o_ref):
        part = lax.dot_general(a_ref[:, :tn], w_ref[0], _NT, preferred_element_type=F32)
        for jj in range(1, jb):
            part += lax.dot_general(a_ref[:, jj * tn:(jj + 1) * tn], w_ref[jj], _NT, preferred_element_type=F32)

        @pl.when(pl.program_id(1) == 0)
        def _():
            o_ref[...] = part

        @pl.when(pl.program_id(1) > 0)
        def _():
            o_ref[...] += part

    return _call(
        body, (a, w), grid=(T // tm, nb // jb), name=name, items=items,
        in_specs=[pl.BlockSpec((tm, jb * tn), lambda i, j: (i, j)),
                  pl.BlockSpec((None, jb, K, tn), lambda i, j: (l, j, 0, 0))],
        out_specs=[pl.BlockSpec((tm, K), lambda i, j: (i, 0))],
        out_shape=[jax.ShapeDtypeStruct((T, K), F32)],
        sem=("parallel", "arbitrary"))


def _mm_tn_cols(lhs, rhs, tn, jb, name, tm=1024):
    T, K = lhs.shape
    nb = rhs.shape[1] // tn
    tm = min(tm, T)
    nt = T // tm

    def body(l_ref, r_ref, o_ref, acc):
        part = lax.dot_general(l_ref[...], r_ref[...], _TN, preferred_element_type=F32)

        @pl.when(pl.program_id(1) == 0)
        def _():
            acc[...] = part

        @pl.when(pl.program_id(1) > 0)
        def _():
            acc[...] += part

        @pl.when(pl.program_id(1) == nt - 1)
        def _():
            for jj in range(jb):
                o_ref[jj] = acc[:, jj * tn:(jj + 1) * tn].astype(BF16)

    return pl.pallas_call(
        body, grid=(nb // jb, nt), name=name,
        in_specs=[pl.BlockSpec((tm, K), lambda j, t: (t, 0)), pl.BlockSpec((tm, jb * tn), lambda j, t: (t, j))],
        out_specs=pl.BlockSpec((jb, K, tn), lambda j, t: (j, 0, 0)),
        out_shape=jax.ShapeDtypeStruct((nb, K, tn), BF16),
        scratch_shapes=[pltpu.VMEM((K, jb * tn), F32)],
        compiler_params=_cp("parallel", "arbitrary"))(lhs, rhs)


def _mm_tn_rows(lhs, rhs, tk, name, tm=1024):
    T, Kl = lhs.shape
    N = rhs.shape[1]
    tm = min(tm, T)
    nt = T // tm

    def body(l_ref, r_ref, o_ref, acc):
        part = lax.dot_general(l_ref[...], r_ref[...].astype(BF16), _TN, preferred_element_type=F32)

        @pl.when(pl.program_id(1) == 0)
        def _():
            acc[...] = part

        @pl.when(pl.program_id(1) > 0)
        def _():
            acc[...] += part

        @pl.when(pl.program_id(1) == nt - 1)
        def _():
            o_ref[...] = acc[...].astype(BF16)

    return pl.pallas_call(
        body, grid=(Kl // tk, nt), name=name,
        in_specs=[pl.BlockSpec((tm, tk), lambda l, t: (t, l)), pl.BlockSpec((tm, N), lambda l, t: (t, 0))],
        out_specs=pl.BlockSpec((tk, N), lambda l, t: (l, 0)),
        out_shape=jax.ShapeDtypeStruct((Kl, N), BF16),
        scratch_shapes=[pltpu.VMEM((tk, N), F32)],
        compiler_params=_cp("parallel", "arbitrary"))(lhs, rhs)


def _swiglu_fwd(gu, name, tm=256):
    T = gu.shape[0]

    def body(gu_ref, o_ref):
        gate = gu_ref[:, :D_FF_PAD].astype(F32)
        up = gu_ref[:, D_FF_PAD:].astype(F32)
        o_ref[...] = (jax.nn.silu(gate) * up).astype(BF16)

    return pl.pallas_call(
        body, grid=(T // tm,), name=name,
        in_specs=[pl.BlockSpec((tm, 2 * D_FF_PAD), lambda i: (i, 0))],
        out_specs=pl.BlockSpec((tm, D_FF_PAD), lambda i: (i, 0)),
        out_shape=jax.ShapeDtypeStruct((T, D_FF_PAD), BF16),
        compiler_params=_cp("parallel"))(gu)


def _swiglu_bwd(gu, dact, name, tm=256):
    T = gu.shape[0]

    def body(gu_ref, d_ref, o_ref):
        gate = gu_ref[:, :D_FF_PAD].astype(F32)
        up = gu_ref[:, D_FF_PAD:].astype(F32)
        d = d_ref[...].astype(F32)
        sig = jax.nn.sigmoid(gate)
        silu = gate * sig
        o_ref[:, :D_FF_PAD] = (d * up * (sig + silu * (1.0 - sig))).astype(BF16)
        o_ref[:, D_FF_PAD:] = (d * silu).astype(BF16)

    return pl.pallas_call(
        body, grid=(T // tm,), name=name,
        in_specs=[pl.BlockSpec((tm, 2 * D_FF_PAD), lambda i: (i, 0)), pl.BlockSpec((tm, D_FF_PAD), lambda i: (i, 0))],
        out_specs=pl.BlockSpec((tm, 2 * D_FF_PAD), lambda i: (i, 0)),
        out_shape=jax.ShapeDtypeStruct((T, 2 * D_FF_PAD), BF16),
        compiler_params=_cp("parallel"))(gu, dact)


def _rms_bwd_add(x, g, dh, dres, name, tm=512):
    T, K = x.shape

    def body(x_ref, g_ref, dh_ref, dr_ref, dx_ref, dg_ref):
        _, vjp = jax.vjp(_rms, x_ref[...], g_ref[...])
        dx, dg = vjp(dh_ref[...])
        dx_ref[...] = dr_ref[...] + dx

        @pl.when(pl.program_id(0) == 0)
        def _():
            dg_ref[...] = dg

        @pl.when(pl.program_id(0) > 0)
        def _():
            dg_ref[...] += dg

    row = pl.BlockSpec((tm, K), lambda i: (i, 0))
    vec = pl.BlockSpec((1, K), lambda i: (0, 0))
    return pl.pallas_call(
        body, grid=(T // tm,), name=name,
        in_specs=[row, vec, row, row], out_specs=[row, vec],
        out_shape=[jax.ShapeDtypeStruct((T, K), F32), jax.ShapeDtypeStruct((1, K), F32)],
        compiler_params=_cp("arbitrary"))(x, g, dh, dres)


def _loss_head(x, g, target, name, tm=512):
    T, K = x.shape

    def loss_fn(xv, gv, tv):
        err = _rms(xv, gv) - tv
        return 0.5 * jnp.sum(jnp.mean(err * err, axis=-1, keepdims=True), axis=0, keepdims=True)

    def body(x_ref, g_ref, t_ref, l_ref, dx_ref, dg_ref):
        val, vjp = jax.vjp(lambda xv, gv: loss_fn(xv, gv, t_ref[...]), x_ref[...], g_ref[...])
        dx, dg = vjp(jnp.ones((1, 1), F32))
        dx_ref[...] = dx
        lval = jnp.broadcast_to(val, (1, 128))

        @pl.when(pl.program_id(0) == 0)
        def _():
            dg_ref[...] = dg
            l_ref[...] = lval

        @pl.when(pl.program_id(0) > 0)
        def _():
            dg_ref[...] += dg
            l_ref[...] += lval

    row = pl.BlockSpec((tm, K), lambda i: (i, 0))
    vec = pl.BlockSpec((1, K), lambda i: (0, 0))
    return pl.pallas_call(
        body, grid=(T // tm,), name=name,
        in_specs=[row, vec, row], out_specs=[pl.BlockSpec((1, 128), lambda i: (0, 0)), row, vec],
        out_shape=[jax.ShapeDtypeStruct((1, 128), F32), jax.ShapeDtypeStruct((T, K), F32),
                   jax.ShapeDtypeStruct((1, K), F32)],
        compiler_params=_cp("arbitrary"))(x, g, target)


def _sgu_fn(u, v, wm, bt, gain):
    ug = jax.nn.gelu(u)
    vg = jax.nn.gelu(v)
    row = lax.broadcasted_iota(jnp.int32, (BLOCK, BLOCK), 0)
    col = lax.broadcasted_iota(jnp.int32, (BLOCK, BLOCK), 1)
    tri = row >= col
    outs = []
    for h in range(GROUP_HEADS):
        vh = vg[:, h * HEAD_DIM:(h + 1) * HEAD_DIM]
        xc = vh - jnp.mean(vh, axis=-1, keepdims=True)
        vn = xc * lax.rsqrt(jnp.mean(xc * xc, axis=-1, keepdims=True) + EPS)
        outs.append(_bdot(jnp.where(tri, wm[h], 0.0), vn) + bt[:, h:h + 1])
    return _rms(ug * jnp.concatenate(outs, axis=1), gain)


def _sgu_fwd(proj, wm, bt, gain, name):
    T = proj.shape[0]

    def body(u_ref, v_ref, w_ref, b_ref, g_ref, o_ref):
        o_ref[...] = _sgu_fn(u_ref[...], v_ref[...], w_ref[...], b_ref[...], g_ref[...]).astype(BF16)

    full = lambda shape: pl.BlockSpec(shape, lambda i: (0,) * len(shape))
    return pl.pallas_call(
        body, grid=(T // BLOCK,), name=name,
        in_specs=[pl.BlockSpec((BLOCK, GROUP_WIDTH), lambda i: (i, 0)), pl.BlockSpec((BLOCK, GROUP_WIDTH), lambda i: (i, 1)),
                  full(wm.shape), full(bt.shape), full(gain.shape)],
        out_specs=pl.BlockSpec((BLOCK, GROUP_WIDTH), lambda i: (i, 0)),
        out_shape=jax.ShapeDtypeStruct((T, GROUP_WIDTH), BF16),
        compiler_params=_cp("parallel"))(proj, proj, wm, bt, gain)


def _acc_out(first, refs, vals):
    @pl.when(first)
    def _():
        for r, v in zip(refs, vals):
            r[...] = v

    @pl.when(jnp.logical_not(first))
    def _():
        for r, v in zip(refs, vals):
            r[...] += v


def _sgu_bwd(proj, wm, bt, gain, dy, name, items=()):
    T = proj.shape[0]

    def body(u_ref, v_ref, w_ref, b_ref, g_ref, dy_ref, duv_ref, dw_ref, db_ref, dg_ref):
        _, vjp = jax.vjp(_sgu_fn, u_ref[...], v_ref[...], w_ref[...], b_ref[...], g_ref[...])
        du, dv, dw, db, dg = vjp(dy_ref[...])
        duv_ref[:, :GROUP_WIDTH] = du.astype(BF16)
        duv_ref[:, GROUP_WIDTH:] = dv.astype(BF16)
        _acc_out(pl.program_id(0) == 0, (dw_ref, db_ref, dg_ref), (dw, db, dg))

    full = lambda shape: pl.BlockSpec(shape, lambda i: (0,) * len(shape))
    return _call(
        body, (proj, proj, wm, bt, gain, dy), grid=(T // BLOCK,), name=name, items=items,
        in_specs=[pl.BlockSpec((BLOCK, GROUP_WIDTH), lambda i: (i, 0)), pl.BlockSpec((BLOCK, GROUP_WIDTH), lambda i: (i, 1)),
                  full(wm.shape), full(bt.shape), full(gain.shape),
                  pl.BlockSpec((BLOCK, GROUP_WIDTH), lambda i: (i, 0))],
        out_specs=[pl.BlockSpec((BLOCK, 2 * GROUP_WIDTH), lambda i: (i, 0)), full(wm.shape), full(bt.shape), full(gain.shape)],
        out_shape=[jax.ShapeDtypeStruct((T, 2 * GROUP_WIDTH), BF16), jax.ShapeDtypeStruct(wm.shape, F32),
                   jax.ShapeDtypeStruct(bt.shape, F32), jax.ShapeDtypeStruct(gain.shape, F32)],
        sem=("arbitrary",))


def _pool_consts(seq):
    t = lax.broadcasted_iota(jnp.int32, (seq, GROUP_WIDTH), 0)
    grp = lax.broadcasted_iota(jnp.int32, (seq, GROUP_WIDTH), 1) // (GROUP_WIDTH // len(POOL_WINDOWS))
    win = jnp.where(grp == 0, POOL_WINDOWS[0], jnp.where(grp == 1, POOL_WINDOWS[1],
                    jnp.where(grp == 2, POOL_WINDOWS[2], POOL_WINDOWS[3])))
    count = jnp.minimum(t + 1, win).astype(F32)
    return t, grp, count


def _by_group(grp, vals):
    return jnp.where(grp == 0, vals[0], jnp.where(grp == 1, vals[1], jnp.where(grp == 2, vals[2], vals[3])))


def _window_sums(x, t, seq, back):
    def shift(a, k):
        if back:
            return jnp.where(t >= k, pltpu.roll(a, k, 0), 0.0)
        return jnp.where(t < seq - k, pltpu.roll(a, seq - k, 0), 0.0)
    sums = []
    a, k = x, 1
    for _ in POOL_WINDOWS:
        a = a + shift(a, k)
        sums.append(a)
        k *= 2
    return sums


def _pool_tail(y, wbd, scale, gain):
    return _rms(_bdot(y, wbd) * scale, gain)


def _pool_fwd(proj, wbd, scale, gain, seq, name):
    T = proj.shape[0]

    def body(p_ref, w_ref, s_ref, g_ref, o_ref):
        p = p_ref[...]
        t, grp, count = _pool_consts(seq)
        y = _by_group(grp, _window_sums(p, t, seq, True)) / count - p
        o_ref[...] = _pool_tail(y, w_ref[...], s_ref[...], g_ref[...]).astype(BF16)

    full = lambda shape: pl.BlockSpec(shape, lambda b: (0,) * len(shape))
    return pl.pallas_call(
        body, grid=(T // seq,), name=name,
        in_specs=[pl.BlockSpec((seq, GROUP_WIDTH), lambda b: (b, 2)), full(wbd.shape), full(scale.shape), full(gain.shape)],
        out_specs=pl.BlockSpec((seq, GROUP_WIDTH), lambda b: (b, 0)),
        out_shape=jax.ShapeDtypeStruct((T, GROUP_WIDTH), BF16),
        compiler_params=_cp("parallel"))(proj, wbd, scale, gain)


def _pool_bwd(proj, wbd, scale, gain, dy, seq, name):
    T = proj.shape[0]

    def body(p_ref, w_ref, s_ref, g_ref, dy_ref, dp_ref, dw_ref, ds_ref, dg_ref):
        p = p_ref[...]
        t, grp, count = _pool_consts(seq)
        y = _by_group(grp, _window_sums(p, t, seq, True)) / count - p
        _, vjp = jax.vjp(_pool_tail, y, w_ref[...], s_ref[...], g_ref[...])
        d_y, dw, ds, dg = vjp(dy_ref[...])
        dp = _by_group(grp, _window_sums(d_y / count, t, seq, False)) - d_y
        dp_ref[...] = dp.astype(BF16)
        _acc_out(pl.program_id(0) == 0, (dw_ref, ds_ref, dg_ref), (dw, ds, dg))

    full = lambda shape: pl.BlockSpec(shape, lambda b: (0,) * len(shape))
    return pl.pallas_call(
        body, grid=(T // seq,), name=name,
        in_specs=[pl.BlockSpec((seq, GROUP_WIDTH), lambda b: (b, 2)), full(wbd.shape), full(scale.shape), full(gain.shape),
                  pl.BlockSpec((seq, GROUP_WIDTH), lambda b: (b, 1))],
        out_specs=[pl.BlockSpec((seq, GROUP_WIDTH), lambda b: (b, 0)), full(wbd.shape), full(scale.shape), full(gain.shape)],
        out_shape=[jax.ShapeDtypeStruct((T, GROUP_WIDTH), BF16), jax.ShapeDtypeStruct(wbd.shape, F32),
                   jax.ShapeDtypeStruct(scale.shape, F32), jax.ShapeDtypeStruct(gain.shape, F32)],
        compiler_params=_cp("arbitrary"))(proj, wbd, scale, gain, dy)


def _swa_fn(q, kv_prev, kv_cur, sinks, tab, gain, first):
    half = GROUP_WIDTH // 2
    k2 = jnp.concatenate([kv_prev[:, :half], kv_cur[:, :half]], axis=0)
    v2 = jnp.concatenate([kv_prev[:, half:], kv_cur[:, half:]], axis=0)
    qi = lax.broadcasted_iota(jnp.int32, (BLOCK, 2 * BLOCK), 0)
    kj = lax.broadcasted_iota(jnp.int32, (BLOCK, 2 * BLOCK), 1)
    dist = qi + BLOCK - kj
    mask = (dist >= 0) & (dist < BLOCK) & ((kj >= BLOCK) | jnp.logical_not(first))
    outs = []
    for h in range(GROUP_HEADS):
        kvh = h // 2
        kh = k2[:, kvh * HEAD_DIM:(kvh + 1) * HEAD_DIM]
        vh = v2[:, kvh * HEAD_DIM:(kvh + 1) * HEAD_DIM]
        logits = _bdot_nt(q[:, h * HEAD_DIM:(h + 1) * HEAD_DIM], kh) * ATT_SCALE + tab[h]
        logits = jnp.where(mask, logits, -1e30)
        sink = sinks[:, h:h + 1]
        m = lax.stop_gradient(jnp.maximum(jnp.max(logits, axis=1, keepdims=True), sink))
        p = jnp.exp(logits - m)
        probs = p / (jnp.sum(p, axis=1, keepdims=True) + jnp.exp(sink - m))
        outs.append(_bdot(probs, vh))
    return _rms(jnp.concatenate(outs, axis=1), gain)


def _swa_specs(nblk):
    q = pl.BlockSpec((BLOCK, GROUP_WIDTH), lambda b, i: (b * nblk + i, 3))
    cur = pl.BlockSpec((BLOCK, GROUP_WIDTH), lambda b, i: (b * nblk + i, 4))
    prev = pl.BlockSpec((BLOCK, GROUP_WIDTH), lambda b, i: (b * nblk + jnp.maximum(i - 1, 0), 4))
    return q, prev, cur


def _swa_fwd(proj, sinks, tab, gain, seq, name, items=()):
    T = proj.shape[0]
    nblk = seq // BLOCK

    def body(q_ref, kp_ref, kc_ref, s_ref, t_ref, g_ref, o_ref):
        o_ref[...] = _swa_fn(q_ref[...], kp_ref[...], kc_ref[...], s_ref[...], t_ref[...], g_ref[...],
                             pl.program_id(1) == 0).astype(BF16)

    full = lambda shape: pl.BlockSpec(shape, lambda b, i: (0,) * len(shape))
    return _call(
        body, (proj, proj, proj, sinks, tab, gain), grid=(T // seq, nblk), name=name, items=items,
        in_specs=[*_swa_specs(nblk), full(sinks.shape), full(tab.shape), full(gain.shape)],
        out_specs=[pl.BlockSpec((BLOCK, GROUP_WIDTH), lambda b, i: (b * nblk + i, 0))],
        out_shape=[jax.ShapeDtypeStruct((T, GROUP_WIDTH), BF16)],
        sem=("parallel", "parallel"))


def _swa_bwd(proj, sinks, tab, gain, dy, seq, name, items=()):
    T = proj.shape[0]
    nblk = seq // BLOCK

    def body(q_ref, kp_ref, kc_ref, s_ref, t_ref, g_ref, dy_ref, dq_ref, dkp_ref, dkc_ref, ds_ref, dt_ref, dg_ref):
        first = pl.program_id(1) == 0
        fn = functools.partial(_swa_fn, first=first)
        _, vjp = jax.vjp(fn, q_ref[...], kp_ref[...], kc_ref[...], s_ref[...], t_ref[...], g_ref[...])
        dq, dkp, dkc, ds, dt, dg = vjp(dy_ref[...])
        dq_ref[...] = dq.astype(BF16)
        dkp_ref[...] = dkp
        dkc_ref[...] = dkc
        _acc_out((pl.program_id(0) == 0) & first, (ds_ref, dt_ref, dg_ref), (ds, dt, dg))

    full = lambda shape: pl.BlockSpec(shape, lambda b, i: (0,) * len(shape))
    blk = lambda c: pl.BlockSpec((BLOCK, GROUP_WIDTH), lambda b, i: (b * nblk + i, c))
    return _call(
        body, (proj, proj, proj, sinks, tab, gain, dy), grid=(T // seq, nblk), name=name, items=items,
        in_specs=[*_swa_specs(nblk), full(sinks.shape), full(tab.shape), full(gain.shape), blk(2)],
        out_specs=[blk(0), blk(0), blk(0), full(sinks.shape), full(tab.shape), full(gain.shape)],
        out_shape=[jax.ShapeDtypeStruct((T, GROUP_WIDTH), BF16), jax.ShapeDtypeStruct((T, GROUP_WIDTH), F32),
                   jax.ShapeDtypeStruct((T, GROUP_WIDTH), F32), jax.ShapeDtypeStruct(sinks.shape, F32),
                   jax.ShapeDtypeStruct(tab.shape, F32), jax.ShapeDtypeStruct(gain.shape, F32)],
        sem=("arbitrary", "arbitrary"))


def _shift_add(cur, prev, seq, name):
    T = cur.shape[0]
    nblk = seq // BLOCK

    def body(c_ref, p_ref, o_ref):
        last = pl.program_id(1) == nblk - 1
        o_ref[...] = (c_ref[...] + jnp.where(last, 0.0, p_ref[...])).astype(BF16)

    return pl.pallas_call(
        body, grid=(T // seq, nblk), name=name,
        in_specs=[pl.BlockSpec((BLOCK, GROUP_WIDTH), lambda b, i: (b * nblk + i, 0)),
                  pl.BlockSpec((BLOCK, GROUP_WIDTH), lambda b, i: (b * nblk + jnp.minimum(i + 1, nblk - 1), 0))],
        out_specs=pl.BlockSpec((BLOCK, GROUP_WIDTH), lambda b, i: (b * nblk + i, 0)),
        out_shape=jax.ShapeDtypeStruct((T, GROUP_WIDTH), BF16),
        compiler_params=_cp("parallel", "parallel"))(cur, prev)


def _t5_bucket(dist):
    max_exact = N_BUCKETS // 2
    df = jnp.maximum(dist, 1).astype(F32)
    large = max_exact + (jnp.log(df / max_exact) / jnp.log(jnp.float32(MAX_DISTANCE / max_exact))
                         * (N_BUCKETS - max_exact)).astype(jnp.int32)
    return jnp.where(dist < max_exact, dist, jnp.minimum(large, N_BUCKETS - 1))


def _bucket_map():
    dist = (jnp.arange(BLOCK)[:, None] + BLOCK) - jnp.arange(2 * BLOCK)[None, :]
    return _t5_bucket(jnp.clip(dist, 0, BLOCK - 1)).astype(jnp.int32)


def _bias_table(rel_bias, buckets, name):
    def body(rb_ref, bk_ref, o_ref):
        bk = bk_ref[...]
        rb = rb_ref[...]
        for h in range(GROUP_HEADS):
            acc = jnp.zeros((BLOCK, 2 * BLOCK), F32)
            for b in range(N_BUCKETS):
                acc = jnp.where(bk == b, rb[b:b + 1, h:h + 1], acc)
            o_ref[h] = acc

    return pl.pallas_call(body, name=name, out_shape=jax.ShapeDtypeStruct((GROUP_HEADS, BLOCK, 2 * BLOCK), F32),
                          compiler_params=_cp())(rel_bias, buckets)


def _bias_table_bwd(dtab, buckets, name):
    def body(dt_ref, bk_ref, o_ref):
        bk = bk_ref[...]
        row = lax.broadcasted_iota(jnp.int32, (N_BUCKETS, GROUP_HEADS), 0)
        col = lax.broadcasted_iota(jnp.int32, (N_BUCKETS, GROUP_HEADS), 1)
        acc = jnp.zeros((N_BUCKETS, GROUP_HEADS), F32)
        for h in range(GROUP_HEADS):
            dt = dt_ref[h]
            for b in range(N_BUCKETS):
                s = jnp.sum(jnp.where(bk == b, dt, 0.0), keepdims=True)
                acc = acc + jnp.where((row == b) & (col == h), s, 0.0)
        o_ref[...] = acc

    return pl.pallas_call(body, name=name, out_shape=jax.ShapeDtypeStruct((N_BUCKETS, GROUP_HEADS), F32),
                          compiler_params=_cp())(dtab, buckets)


HEAD_ROWS = GROUP_HEADS * BLOCK


def _stack_heads(x):
    return jnp.concatenate([jnp.where(_head_mask(h, x.shape), x, 0.0) for h in range(GROUP_HEADS)], axis=0).astype(BF16)


def _sb_tile(qs, kb, q0, k0):
    z = lax.dot_general(qs, kb, _NT, preferred_element_type=F32)
    row = lax.broadcasted_iota(jnp.int32, (HEAD_ROWS, BLOCK), 0) & (BLOCK - 1)
    col = lax.broadcasted_iota(jnp.int32, (HEAD_ROWS, BLOCK), 1)
    causal = (k0 + col) < (q0 + row)
    ls_neg = -(jnp.maximum(z, 0.0) + jnp.log(1.0 + jnp.exp(-jnp.abs(z))))
    return jnp.where(causal, ls_neg, 0.0), ls_neg + z, causal


SB_DEAD = -104.0
SB_FIRST_LANE = GROUP_HEADS


def _tri(strict_upper_src):
    r = lax.broadcasted_iota(jnp.int32, (BLOCK, BLOCK), 0)
    c = lax.broadcasted_iota(jnp.int32, (BLOCK, BLOCK), 1)
    cond = {"gt": r > c, "le": r <= c, "lt": r < c}[strict_upper_src]
    return jnp.where(cond, 1.0, 0.0).astype(BF16)


def _sb_fwd(proj, gain, seq, name, items=()):
    T = proj.shape[0]
    nblk = seq // BLOCK

    def body(q_ref, k_ref, v_ref, g_ref, o_ref, raw_ref, bt_ref):
        i = pl.program_id(1)
        q = q_ref[...]
        u_gt = _tri("gt")
        lane = lax.broadcasted_iota(jnp.int32, (BLOCK, BLOCK), 1)
        heads = [slice(h * HEAD_DIM, (h + 1) * HEAD_DIM) for h in range(GROUP_HEADS)]
        rows = [slice(h * BLOCK, (h + 1) * BLOCK) for h in range(GROUP_HEADS)]
        qs = _stack_heads(q * ATT_SCALE)

        def live(carry):
            j, _, cb = carry
            return (j >= 0) & (jnp.max(cb) > SB_DEAD)

        def step(carry):
            j, accs, cb = carry
            ks = pl.multiple_of(j * BLOCK, BLOCK)
            kb = k_ref[pl.ds(ks, BLOCK), :].astype(BF16)
            vb = v_ref[pl.ds(ks, BLOCK), :].astype(BF16)
            b, a, causal = _sb_tile(qs, kb, i * BLOCK, j * BLOCK)
            tail = _split_dot(b, u_gt) + cb
            w = jnp.where(causal, jnp.exp(a + tail), 0.0).astype(BF16)
            accs = tuple(accs[h] + jnp.dot(w[rows[h]], vb[:, hs], preferred_element_type=F32)
                         for h, hs in enumerate(heads))
            return j - 1, accs, cb + jnp.sum(b, axis=1, keepdims=True)

        zero_acc = tuple(jnp.zeros((BLOCK, HEAD_DIM), F32) for _ in heads)
        j_end, accs, cb = lax.while_loop(live, step, (i, zero_acc, jnp.zeros((HEAD_ROWS, 1), F32)))
        side = jnp.where(lane == SB_FIRST_LANE, (j_end + 1).astype(F32), 0.0)
        for h in range(GROUP_HEADS):
            side = jnp.where(lane == h, cb[rows[h]], side)
        raw = jnp.concatenate(accs, axis=1)
        raw_ref[...] = raw
        bt_ref[...] = side
        o_ref[...] = _rms(raw, g_ref[...]).astype(BF16)

    return _call(
        body, (proj, proj, proj, gain), grid=(T // seq, nblk), name=name, items=items,
        in_specs=[pl.BlockSpec((BLOCK, GROUP_WIDTH), lambda b, i: (b * nblk + i, 5)),
                  pl.BlockSpec((seq, GROUP_WIDTH), lambda b, i: (b, 6)),
                  pl.BlockSpec((seq, GROUP_WIDTH), lambda b, i: (b, 7)),
                  pl.BlockSpec(gain.shape, lambda b, i: (0, 0))],
        out_specs=[pl.BlockSpec((BLOCK, GROUP_WIDTH), lambda b, i: (b * nblk + i, 0)),
                   pl.BlockSpec((BLOCK, GROUP_WIDTH), lambda b, i: (b * nblk + i, 0)),
                   pl.BlockSpec((BLOCK, BLOCK), lambda b, i: (b * nblk + i, 0))],
        out_shape=[jax.ShapeDtypeStruct((T, GROUP_WIDTH), BF16), jax.ShapeDtypeStruct((T, GROUP_WIDTH), F32),
                   jax.ShapeDtypeStruct((T, BLOCK), F32)],
        sem=("parallel", "parallel"))


def _sb_bwd(proj, gain, raw, btot, dy, seq, name, items=()):
    T = proj.shape[0]
    nblk = seq // BLOCK

    def body(q_ref, k_ref, v_ref, g_ref, raw_ref, bt_ref, dy_ref, dq_ref, dk_ref, dv_ref, dg_ref):
        i = pl.program_id(1)

        @pl.when(i == 0)
        def _():
            dk_ref[...] = jnp.zeros_like(dk_ref)
            dv_ref[...] = jnp.zeros_like(dv_ref)

        rawv = raw_ref[...]
        _, vjp = jax.vjp(_rms, rawv, g_ref[...])
        do, dg = vjp(dy_ref[...])
        _acc_out((pl.program_id(0) == 0) & (i == 0), (dg_ref,), (dg,))
        q = q_ref[...]
        bt = bt_ref[...]
        u_le = _tri("le")
        u_lt = _tri("lt")
        heads = [slice(h * HEAD_DIM, (h + 1) * HEAD_DIM) for h in range(GROUP_HEADS)]
        rows = [slice(h * BLOCK, (h + 1) * BLOCK) for h in range(GROUP_HEADS)]
        qs = _stack_heads(q * ATT_SCALE)
        dos = _stack_heads(do)
        bts = jnp.concatenate([bt[:, h:h + 1] for h in range(GROUP_HEADS)], axis=0)
        first = jnp.max(bt[:, SB_FIRST_LANE:SB_FIRST_LANE + 1]).astype(jnp.int32)
        first = jnp.minimum(jnp.maximum(first, 0), i)

        def step(j, carry):
            dqs, cb, cg = carry
            ks = pl.multiple_of(j * BLOCK, BLOCK)
            kb = k_ref[pl.ds(ks, BLOCK), :].astype(BF16)
            vb = v_ref[pl.ds(ks, BLOCK), :].astype(BF16)
            b, a, causal = _sb_tile(qs, kb, i * BLOCK, j * BLOCK)
            tail = bts - (_split_dot(b, u_le) + cb)
            w = jnp.where(causal, jnp.exp(a + tail), 0.0)
            sig = jnp.exp(a)
            g = w * lax.dot_general(dos, vb, _NT, preferred_element_type=F32)
            gpre = _split_dot(g, u_lt) + cg
            dz = jnp.where(causal, g * (1.0 - sig) - gpre * sig, 0.0).astype(BF16)
            dqs = tuple(dqs[h] + jnp.dot(dz[rows[h]], kb[:, hs], preferred_element_type=F32)
                        for h, hs in enumerate(heads))
            dk_ref[pl.ds(ks, BLOCK), :] += lax.dot_general(dz, qs, _TN, preferred_element_type=F32)
            dv_ref[pl.ds(ks, BLOCK), :] += lax.dot_general(w.astype(BF16), dos, _TN, preferred_element_type=F32)
            return dqs, cb + jnp.sum(b, axis=1, keepdims=True), cg + jnp.sum(g, axis=1, keepdims=True)

        zero_dq = tuple(jnp.zeros((BLOCK, HEAD_DIM), F32) for _ in heads)
        zero = jnp.zeros((HEAD_ROWS, 1), F32)
        dqs, _, _ = lax.fori_loop(first, i + 1, step, (zero_dq, zero, zero))
        dq_ref[...] = (jnp.concatenate(dqs, axis=1) * ATT_SCALE).astype(BF16)

    blk = lambda c: pl.BlockSpec((BLOCK, GROUP_WIDTH), lambda b, i: (b * nblk + i, c))
    seqblk = lambda c: pl.BlockSpec((seq, GROUP_WIDTH), lambda b, i: (b, c))
    vec = pl.BlockSpec(gain.shape, lambda b, i: (0, 0))
    return _call(
        body, (proj, proj, proj, gain, raw, btot, dy), grid=(T // seq, nblk), name=name, items=items,
        in_specs=[blk(5), seqblk(6), seqblk(7), vec, blk(0), pl.BlockSpec((BLOCK, BLOCK), lambda b, i: (b * nblk + i, 0)),
                  blk(3)],
        out_specs=[blk(0), seqblk(0), seqblk(0), vec],
        out_shape=[jax.ShapeDtypeStruct((T, GROUP_WIDTH), BF16), jax.ShapeDtypeStruct((T, GROUP_WIDTH), F32),
                   jax.ShapeDtypeStruct((T, GROUP_WIDTH), F32), jax.ShapeDtypeStruct(gain.shape, F32)],
        sem=("arbitrary", "arbitrary"))


def _layer_params(l, sgu_w, sgu_b, pool_w, pool_scale, swa_sinks, mix_out_gain, norm_mix, norm_ffn):
    gains = mix_out_gain[l].reshape(4, 1, GROUP_WIDTH)
    return dict(
        wm=sgu_w[l], bt=sgu_b[l].T,
        wbd=jax.scipy.linalg.block_diag(*[pool_w[l, g] for g in range(len(POOL_WINDOWS))]),
        scale=pool_scale[l][None], sinks=swa_sinks[l][None],
        gain=[gains[m] for m in range(4)], norm_mix=norm_mix[l][None], norm_ffn=norm_ffn[l][None])


def _down_place(dst, k):
    return dst.at[k // 2, pl.ds((k % 2) * DOWN_ROWS, DOWN_ROWS), :]


def _weight_items(shards, l):
    s_in, s_out, s_gu, s_down = shards
    whole = lambda src, k: src.at[l]
    return dict(
        w_in=_Item(s_in, whole, (N_DEV,) + s_in.shape[1:]), w_out=_Item(s_out, whole, (N_DEV,) + s_out.shape[1:]),
        w_gu=_Item(s_gu, whole, (N_DEV,) + s_gu.shape[1:]),
        w_down=_Item(s_down, whole, (4, FF_PAD, D_MODEL), place=_down_place, init=jnp.zeros((4, FF_PAD, D_MODEL), BF16)))


def _as_weights(g_in, g_out, g_gu, g_down):
    return (g_in[None], g_out.reshape(1, D_MODEL, D_MODEL), g_gu[None], g_down.reshape(1, D_FF_PAD, D_MODEL))


def _layer_fwd(l, x, p, w, tab, seq, nxt):
    win, wout, wgu, wd = w
    ride = lambda key: (nxt[key],) if nxt else ()
    proj, h1, *got_out = _norm_mm(x, p["norm_mix"], win, 0, 4, F32, f"in_proj_{l}", items=ride("w_out"))
    ya = _sgu_fwd(proj, p["wm"], p["bt"], p["gain"][0], f"sgu_fwd_{l}")
    yb = _pool_fwd(proj, p["wbd"], p["scale"], p["gain"][1], seq, f"pool_fwd_{l}")
    yc, *got_down = _swa_fwd(proj, p["sinks"], tab, p["gain"][2], seq, f"swa_fwd_{l}", items=ride("w_down"))
    yd, raw, btot, *got_in = _sb_fwd(proj, p["gain"][3], seq, f"sb_fwd_{l}", items=ride("w_in"))
    ycat = jnp.concatenate([ya, yb, yc, yd], axis=1)
    xm, = _mm_res(x, ycat, wout, 0, f"out_proj_{l}", tn=D_MODEL)
    gu, h2, *got_gu = _norm_mm(xm, p["norm_ffn"], wgu, 0, 1, BF16, f"ffn_up_{l}", items=ride("w_gu"))
    act = _swiglu_fwd(gu, f"swiglu_fwd_{l}")
    xo, = _mm_res(xm, act, wd, 0, f"ffn_down_{l}", tn=D_MODEL // 2)
    w_next = _as_weights(got_in[0], got_out[0], got_gu[0], got_down[0]) if nxt else None
    return xo, (x, proj, h1, ycat, raw, btot, xm, gu, h2, act), w_next


def _layer_bwd(l, dxo, saved, p, w, tab, seq):
    win, wout, wgu, wd = w
    x, proj, h1, ycat, raw, btot, xm, gu, h2, act = saved
    out_rows = D_MODEL // N_DEV
    dact = _mm_nt(dxo, wd, 0, BF16, f"ffn_down_dx_{l}")
    g_wd = _mm_tn_rows(act, dxo, FF_PAD, f"ffn_down_dw_{l}")
    dgu = _swiglu_bwd(gu, dact, f"swiglu_bwd_{l}")
    g_wgu = _mm_tn_cols(h2, dgu, FF_PAD, 1, f"ffn_up_dw_{l}")
    dh2, = _mm_nt_acc(dgu, wgu, 0, 1, f"ffn_up_dx_{l}")
    dxm, g_norm_ffn = _rms_bwd_add(xm, p["norm_ffn"], dh2, dxo, f"norm_ffn_bwd_{l}")
    dycat = _mm_nt(dxm, wout, 0, F32, f"out_proj_dx_{l}")
    g_wout = _mm_tn_rows(ycat, dxm, D_MODEL, f"out_proj_dw_{l}")
    send_out = _Item(g_wout, lambda src, k: src.at[pl.ds(k * out_rows, out_rows), :], (N_DEV, out_rows, D_MODEL))
    duv, g_wm, g_bt, g_ga, p_out = _sgu_bwd(proj, p["wm"], p["bt"], p["gain"][0], dycat, f"sgu_bwd_{l}",
                                            items=(send_out,))
    dp, g_wbd, g_scale, g_gb = _pool_bwd(proj, p["wbd"], p["scale"], p["gain"][1], dycat, seq, f"pool_bwd_{l}")
    send_down = _Item(g_wd, lambda src, k: src.at[pl.ds((k // 2) * FF_PAD + (k % 2) * DOWN_ROWS, DOWN_ROWS), :],
                      (N_DEV, DOWN_ROWS, D_MODEL))
    dq, dkp, dkc, g_sinks, g_tab, g_gc, p_down = _swa_bwd(proj, p["sinks"], tab, p["gain"][2], dycat, seq,
                                                          f"swa_bwd_{l}", items=(send_down,))
    dkv = _shift_add(dkc, dkp, seq, f"swa_dkv_{l}")
    send_gu = _Item(g_wgu, lambda src, k: src.at[k], g_wgu.shape)
    dqd, dkd, dvd, g_gd, p_gu = _sb_bwd(proj, p["gain"][3], raw, btot, dycat, seq, f"sb_bwd_{l}", items=(send_gu,))
    dproj = jnp.concatenate([duv, dp, dq, dkv, dqd, dkd.astype(BF16), dvd.astype(BF16)], axis=1)
    g_win = _mm_tn_cols(h1, dproj, GROUP_WIDTH, 4, f"in_proj_dw_{l}")
    send_in = _Item(g_win, lambda src, k: src.at[k], g_win.shape)
    dh1, p_in = _mm_nt_acc(dproj, win, 0, 4, f"in_proj_dx_{l}", items=(send_in,))
    dx, g_norm_mix = _rms_bwd_add(x, p["norm_mix"], dh1, dxm, f"norm_mix_bwd_{l}")
    ng = len(POOL_WINDOWS)
    gd = GROUP_WIDTH // ng
    small = dict(
        sgu_w=g_wm, sgu_b=g_bt.T,
        pool_w=jnp.stack([g_wbd[g * gd:(g + 1) * gd, g * gd:(g + 1) * gd] for g in range(ng)]),
        pool_scale=g_scale[0], swa_sinks=g_sinks[0],
        mix_out_gain=jnp.concatenate([g_ga[0], g_gb[0], g_gc[0], g_gd[0]]),
        norm_mix=g_norm_mix[0], norm_ffn=g_norm_ffn[0])
    return dx, (p_in, p_out, p_gu, p_down), small, g_tab


def _local_step(x, target, w0, shards, sgu_w, sgu_b, pool_w, pool_scale, swa_sinks, rel_bias, mix_out_gain,
                norm_mix, norm_ffn, norm_final, seq):
    buckets = _bucket_map()
    tab = _bias_table(rel_bias, buckets, "bias_table")
    params = [_layer_params(l, sgu_w, sgu_b, pool_w, pool_scale, swa_sinks, mix_out_gain, norm_mix, norm_ffn)
              for l in range(DEPTH)]
    saved, weights = [], [w0]
    for l in range(DEPTH):
        nxt = _weight_items(shards, l + 1) if l + 1 < DEPTH else None
        x, s, w_next = _layer_fwd(l, x, params[l], weights[l], tab, seq, nxt)
        saved.append(s)
        weights.append(w_next)
    loss, dx, g_final = _loss_head(x, norm_final[None], target, "loss_head")
    big, small, g_tab = [None] * DEPTH, [None] * DEPTH, None
    for l in reversed(range(DEPTH)):
        dx, big[l], small[l], t = _layer_bwd(l, dx, saved[l], params[l], weights[l], tab, seq)
        g_tab = t if g_tab is None else g_tab + t
    g_small = {k: jnp.stack([small[l][k] for l in range(DEPTH)]) for k in small[0]}
    g_small["rel_bias"] = _bias_table_bwd(g_tab, buckets, "bias_table_bwd")
    g_small["norm_final"] = g_final[0]
    return loss, dx, big, g_small


def _cast_pad(w, cols, name):
    L, r, c = w.shape

    def body(w_ref, o_ref):
        if cols != c:
            o_ref[...] = jnp.zeros_like(o_ref)
        o_ref[:, :c] = w_ref[...].astype(BF16)

    return pl.pallas_call(
        body, grid=(L,), name=name,
        in_specs=[pl.BlockSpec((None, r, c), lambda l: (l, 0, 0))],
        out_specs=pl.BlockSpec((None, r, cols), lambda l: (l, 0, 0)),
        out_shape=jax.ShapeDtypeStruct((L, r, cols), BF16),
        compiler_params=_cp("parallel"))(w)


def _down_rows(ref, k):
    return ref.at[:, k // 2, pl.ds((k % 2) * DOWN_ROWS, DOWN_ROWS), :]


def _all_gather_weights(s_in, s_out, s_gu, s_down):
    L = s_in.shape[0]
    shards = (s_in, s_out, s_gu, s_down)
    down_full = jnp.zeros((L, 4, FF_PAD, D_MODEL), BF16)
    n = len(shards)

    def body(i0, i1, i2, i3, _, o0, o1, o2, o3, send_sems, recv_sems, local_sems):
        srcs = (i0, i1, i2, i3)
        outs = (o0, o1, o2, o3)
        x, y, c, _me = _my_place()
        chips = [(1 - x, y), (x, 1 - y), (1 - x, 1 - y)]

        def place(t, dev):
            k = 4 * dev[0] + 2 * dev[1] + dev[2]
            return _down_rows(outs[t], k) if t == 3 else outs[t].at[:, k]

        def copy(slot, t, block, to, from_shard):
            return pltpu.make_async_remote_copy(
                src_ref=srcs[t] if from_shard else place(t, block), dst_ref=place(t, block),
                send_sem=send_sems.at[slot, t], recv_sem=recv_sems.at[slot, t], device_id=to, device_id_type=MESH)

        me, sibling = (x, y, c), (x, y, 1 - c)
        mine = [pltpu.make_async_copy(srcs[t], place(t, me), local_sems.at[t]) for t in range(n)]
        for cp in mine:
            cp.start()
        first = [copy(0, t, me, sibling, True) for t in range(n)]
        first += [copy(1 + j, t, me, (*chip, c), True) for j, chip in enumerate(chips) for t in range(n)]
        for cp in first:
            cp.start()
        passed = []
        for j, chip in enumerate(chips):
            for t in range(n):
                copy(1 + j, t, (*chip, c), me, False).wait_recv()
                cp = copy(4 + j, t, (*chip, c), sibling, False)
                cp.start()
                passed.append(cp)
        for t in range(n):
            copy(0, t, sibling, me, False).wait_recv()
        for j, chip in enumerate(chips):
            for t in range(n):
                copy(4 + j, t, (*chip, 1 - c), me, False).wait_recv()
        for cp in first + passed:
            cp.wait_send()
        for cp in mine:
            cp.wait()

    shapes = [jax.ShapeDtypeStruct((L, N_DEV) + s.shape[1:], BF16) for s in shards[:3]]
    shapes.append(jax.ShapeDtypeStruct(down_full.shape, BF16))
    return pl.pallas_call(
        body, name="all_gather_weights", out_shape=shapes,
        in_specs=[ANY] * 5, out_specs=[ANY] * 4, input_output_aliases={4: 3},
        scratch_shapes=[pltpu.SemaphoreType.DMA((7, n)), pltpu.SemaphoreType.DMA((7, n)), pltpu.SemaphoreType.DMA((n,))],
        )(*shards, down_full)


def _all_reduce_small(part):
    rows = part.shape[0]

    def body(p_ref, o_ref, buf, send_sems, recv_sems):
        x, y, c, me = _my_place()
        buf[me] = p_ref[...]
        sends = []
        for d in range(1, N_DEV):
            peer = _peer(x, y, c, d)
            sends.append(pltpu.make_async_remote_copy(
                src_ref=p_ref, dst_ref=buf.at[me], send_sem=send_sems.at[d - 1], recv_sem=recv_sems.at[d - 1],
                device_id=peer, device_id_type=MESH))
        for cp in sends:
            cp.start()
        for d in range(1, N_DEV):
            px, py, pc = _peer(x, y, c, d)
            pltpu.make_async_remote_copy(
                src_ref=p_ref, dst_ref=buf.at[4 * px + 2 * py + pc], send_sem=send_sems.at[d - 1],
                recv_sem=recv_sems.at[d - 1], device_id=(px, py, pc), device_id_type=MESH).wait_recv()
        for cp in sends:
            cp.wait_send()
        total = buf[0]
        for k in range(1, N_DEV):
            total = total + buf[k]
        o_ref[...] = total

    return pl.pallas_call(
        body, name="all_reduce_small", out_shape=jax.ShapeDtypeStruct((rows, 128), F32),
        in_specs=[pl.BlockSpec(memory_space=pltpu.VMEM)], out_specs=pl.BlockSpec(memory_space=pltpu.VMEM),
        scratch_shapes=[pltpu.VMEM((N_DEV, rows, 128), F32), pltpu.SemaphoreType.DMA((N_DEV - 1,)),
                        pltpu.SemaphoreType.DMA((N_DEV - 1,))],
        compiler_params=pltpu.CompilerParams(vmem_limit_bytes=VMEM_LIMIT))(part)


def _adamw(w, g, m, v):
    m = ADAM_B1 * m + (1.0 - ADAM_B1) * g
    v = ADAM_B2 * v + (1.0 - ADAM_B2) * jnp.square(g)
    m_hat = m / (1.0 - ADAM_B1 ** ADAM_STEP)
    v_hat = v / (1.0 - ADAM_B2 ** ADAM_STEP)
    delta = -ADAM_LR * (m_hat / (jnp.sqrt(v_hat) + ADAM_EPS) + ADAM_WD * w)
    return delta, m, v


def _adamw_sharded(parts, w, m, v, tr, name):
    L, r, c = w.shape
    cp = parts[0].shape[-1]
    nrow = r // tr

    def body(*refs):
        p_refs, (w_ref, m_ref, v_ref, g_ref, d_ref, nm_ref, nv_ref) = refs[:L], refs[L:]
        for k in range(L):
            @pl.when(pl.program_id(0) == k)
            def _(p_ref=p_refs[k]):
                g = p_ref[0, :, :c].astype(F32)
                for dev in range(1, N_DEV):
                    g = g + p_ref[dev, :, :c].astype(F32)
                delta, nm, nv = _adamw(w_ref[...], g, m_ref[...], v_ref[...])
                g_ref[...] = g
                d_ref[...] = delta
                nm_ref[...] = nm
                nv_ref[...] = nv

    def part_spec(k):
        return pl.BlockSpec((N_DEV, tr, cp),
                            lambda l, i: (0, jnp.where(l == k, i, jnp.where(l < k, 0, nrow - 1)), 0))

    blk = pl.BlockSpec((None, tr, c), lambda l, i: (l, i, 0))
    out = jax.ShapeDtypeStruct((L, r, c), F32)
    return pl.pallas_call(
        body, grid=(L, nrow), name=name,
        in_specs=[part_spec(k) for k in range(L)] + [blk, blk, blk],
        out_specs=[blk] * 4, out_shape=[out] * 4,
        compiler_params=_cp("arbitrary", "arbitrary"))(*parts, w, m, v)


def _adamw_small(g, w, m, v, name):
    def body(g_ref, w_ref, m_ref, v_ref, d_ref, nm_ref, nv_ref):
        delta, nm, nv = _adamw(w_ref[...], g_ref[...], m_ref[...], v_ref[...])
        d_ref[...] = delta
        nm_ref[...] = nm
        nv_ref[...] = nv

    out = jax.ShapeDtypeStruct(g.shape, F32)
    return pl.pallas_call(body, name=name, out_shape=[out] * 3, compiler_params=_cp())(g, w, m, v)


SMALL = ("sgu_w", "sgu_b", "pool_w", "pool_scale", "swa_sinks", "rel_bias", "mix_out_gain", "norm_mix", "norm_ffn",
         "norm_final")


def _seg_rows(a):
    return -(-a.size // 128)


def _pack(parts, rows):
    segs = [jnp.pad(p.reshape(-1), (0, _seg_rows(p) * 128 - p.size)).reshape(_seg_rows(p), 128) for p in parts]
    used = sum(s.shape[0] for s in segs)
    return jnp.concatenate(segs + [jnp.zeros((rows - used, 128), F32)], axis=0)


def _unpack(buf, like):
    out, at = [], 0
    for a in like:
        out.append(buf[at:at + _seg_rows(a)].reshape(-1)[:a.size].reshape(a.shape))
        at += _seg_rows(a)
    return out


def kernel(x, w_in, w_out, sgu_w, sgu_b, pool_w, pool_scale, swa_sinks, rel_bias, mix_out_gain, norm_mix, norm_ffn, w_gate_up, w_down, norm_final, loss_target, m_w_in, m_w_out, m_sgu_w, m_sgu_b, m_pool_w, m_pool_scale, m_swa_sinks, m_rel_bias, m_mix_out_gain, m_norm_mix, m_norm_ffn, m_w_gate_up, m_w_down, m_norm_final, v_w_in, v_w_out, v_sgu_w, v_sgu_b, v_pool_w, v_pool_scale, v_swa_sinks, v_rel_bias, v_mix_out_gain, v_norm_mix, v_norm_ffn, v_w_gate_up, v_w_down, v_norm_final):
    bl, seq, _ = x.shape
    L = w_in.shape[0]
    shards = (_cast_pad(w_in, GROUP_WIDTH, "shard_w_in"), _cast_pad(w_out, D_MODEL, "shard_w_out"),
              _cast_pad(w_gate_up, FF_PAD, "shard_w_gate_up"), _cast_pad(w_down, D_MODEL, "shard_w_down"))
    first = _all_gather_weights(*[s[:1] for s in shards])
    w0 = _as_weights(*[f[0] for f in first])
    small_w = dict(sgu_w=sgu_w, sgu_b=sgu_b, pool_w=pool_w, pool_scale=pool_scale, swa_sinks=swa_sinks,
                   rel_bias=rel_bias, mix_out_gain=mix_out_gain, norm_mix=norm_mix, norm_ffn=norm_ffn,
                   norm_final=norm_final)
    small_m = dict(sgu_w=m_sgu_w, sgu_b=m_sgu_b, pool_w=m_pool_w, pool_scale=m_pool_scale, swa_sinks=m_swa_sinks,
                   rel_bias=m_rel_bias, mix_out_gain=m_mix_out_gain, norm_mix=m_norm_mix, norm_ffn=m_norm_ffn,
                   norm_final=m_norm_final)
    small_v = dict(sgu_w=v_sgu_w, sgu_b=v_sgu_b, pool_w=v_pool_w, pool_scale=v_pool_scale, swa_sinks=v_swa_sinks,
                   rel_bias=v_rel_bias, mix_out_gain=v_mix_out_gain, norm_mix=v_norm_mix, norm_ffn=v_norm_ffn,
                   norm_final=v_norm_final)
    loss, dx, big, g_small = _local_step(
        x.reshape(bl * seq, D_MODEL), loss_target.reshape(bl * seq, D_MODEL), w0, shards, sgu_w, sgu_b, pool_w,
        pool_scale, swa_sinks, rel_bias, mix_out_gain, norm_mix, norm_ffn, norm_final, seq)
    p_in, p_out, p_gu, p_down = ([big[l][t] for l in range(L)] for t in range(4))
    outs_in = _adamw_sharded(p_in, w_in, m_w_in, v_w_in, 256, "adamw_w_in")
    outs_out = _adamw_sharded(p_out, w_out, m_w_out, v_w_out, D_MODEL // N_DEV, "adamw_w_out")
    outs_gu = _adamw_sharded(p_gu, w_gate_up, m_w_gate_up, v_w_gate_up, 256, "adamw_w_gate_up")
    outs_down = _adamw_sharded(p_down, w_down, m_w_down, v_w_down, DOWN_ROWS // 2, "adamw_w_down")
    rows = -(-(sum(_seg_rows(small_w[k]) for k in SMALL) + 1) // 8) * 8
    packed = _pack([g_small[k] for k in SMALL] + [loss[0]], rows)
    total = _all_reduce_small(packed)
    like = [small_w[k] for k in SMALL]
    d_s, m_s, v_s = _adamw_small(total, _pack(like, rows), _pack([small_m[k] for k in SMALL], rows),
                                 _pack([small_v[k] for k in SMALL], rows), "adamw_small")
    g_list = dict(zip(SMALL, _unpack(total, like)))
    d_list = dict(zip(SMALL, _unpack(d_s, like)))
    m_list = dict(zip(SMALL, _unpack(m_s, like)))
    v_list = dict(zip(SMALL, _unpack(v_s, like)))
    loss_total = total[sum(_seg_rows(a) for a in like), 0]
    big_outs = dict(w_in=outs_in, w_out=outs_out, w_gate_up=outs_gu, w_down=outs_down)
    order = ("w_in", "w_out", "sgu_w", "sgu_b", "pool_w", "pool_scale", "swa_sinks", "rel_bias", "mix_out_gain",
             "norm_mix", "norm_ffn", "w_gate_up", "w_down", "norm_final")
    result = [loss_total, dx.reshape(bl, seq, D_MODEL)]
    for which, small in enumerate((g_list, d_list, m_list, v_list)):
        for name in order:
            result.append(big_outs[name][which] if name in big_outs else small[name])
    return tuple(result)
```

```python
import functools

import jax
import jax.numpy as jnp
from jax import lax
from jax.experimental import pallas as pl
from jax.experimental.pallas import tpu as pltpu

F32 = jnp.float32
BF16 = jnp.bfloat16

N_DEV = 8
DEPTH = 4
D_MODEL = 1024
GROUP_WIDTH = 256
HEAD_DIM = 64
GROUP_HEADS = 4
BLOCK = 128
N_BUCKETS = 32
MAX_DISTANCE = 128
POOL_WINDOWS = (2, 4, 8, 16)
D_FF = 2816
FF_SHARD = D_FF // 4
FF_PAD = 768
D_FF_PAD = 4 * FF_PAD
EPS = 1e-6
ATT_SCALE = HEAD_DIM ** -0.5
ADAM_LR = 0.001
ADAM_B1 = 0.9
ADAM_B2 = 0.999
ADAM_EPS = 1e-08
ADAM_WD = 0.01
ADAM_STEP = 10
VMEM_LIMIT = 56 * 1024 * 1024
MESH_AXES = ("x", "y", "c")


def _cp(*sem):
    return pltpu.CompilerParams(dimension_semantics=sem or None, vmem_limit_bytes=VMEM_LIMIT)


_NT = (((1,), (1,)), ((), ()))
_TN = (((0,), (0,)), ((), ()))


@jax.custom_vjp
def _bdot(a, b):
    return jnp.dot(a.astype(BF16), b.astype(BF16), preferred_element_type=F32)


def _bdot_fwd(a, b):
    return _bdot(a, b), (a.astype(BF16), b.astype(BF16))


def _bdot_bwd(res, ct):
    a, b = res
    c = ct.astype(BF16)
    return (lax.dot_general(c, b, _NT, preferred_element_type=F32),
            lax.dot_general(a, c, _TN, preferred_element_type=F32))


_bdot.defvjp(_bdot_fwd, _bdot_bwd)


@jax.custom_vjp
def _bdot_nt(a, b):
    return lax.dot_general(a.astype(BF16), b.astype(BF16), _NT, preferred_element_type=F32)


def _bdot_nt_fwd(a, b):
    return _bdot_nt(a, b), (a.astype(BF16), b.astype(BF16))


def _bdot_nt_bwd(res, ct):
    a, b = res
    c = ct.astype(BF16)
    return (jnp.dot(c, b, preferred_element_type=F32),
            lax.dot_general(c, a, _TN, preferred_element_type=F32))


_bdot_nt.defvjp(_bdot_nt_fwd, _bdot_nt_bwd)


def _rms(x, g):
    return x * lax.rsqrt(jnp.mean(x * x, axis=-1, keepdims=True) + EPS) * g


def _split_dot(x, u):
    hi = x.astype(BF16)
    lo = (x - hi.astype(F32)).astype(BF16)
    return jnp.dot(hi, u, preferred_element_type=F32) + jnp.dot(lo, u, preferred_element_type=F32)


def _head_mask(h, shape):
    col = lax.broadcasted_iota(jnp.int32, shape, 1)
    return (col >= h * HEAD_DIM) & (col < (h + 1) * HEAD_DIM)


ANY = pl.BlockSpec(memory_space=pl.ANY)
MESH = pl.DeviceIdType.MESH
DOWN_ROWS = D_FF // N_DEV


def _my_place():
    x, y, c = (lax.axis_index(a) for a in MESH_AXES)
    return x, y, c, 4 * x + 2 * y + c


def _peer(x, y, c, d):
    return (x ^ (d >> 2), y ^ ((d >> 1) & 1), c ^ (d & 1))


class _Item:
    def __init__(self, src, block, dst_shape, place=None, init=None):
        self.src, self.block, self.dst_shape, self.init = src, block, dst_shape, init
        self.place = place or (lambda dst, k: dst.at[k])


def _call(body, args, *, grid, in_specs, out_specs, out_shape, sem, name, scratch_shapes=(), items=()):
    if not items:
        outs = pl.pallas_call(body, grid=grid, in_specs=in_specs, out_specs=out_specs, out_shape=out_shape, name=name,
                              scratch_shapes=list(scratch_shapes), compiler_params=_cp(*sem))(*args)
        return list(outs) if isinstance(outs, (list, tuple)) else [outs]
    n_in, n_out, n_scr, n = len(in_specs), len(out_specs), len(scratch_shapes), len(items)
    inits = [i for i, it in enumerate(items) if it.init is not None]

    def wrapped(*refs):
        core_in, srcs = refs[:n_in], refs[n_in:n_in + n]
        off = n_in + n + len(inits)
        core_out, dsts = refs[off:off + n_out], refs[off + n_out:off + n_out + n]
        scratch = refs[off + n_out + n:]
        send_sems, recv_sems, local_sems = scratch[n_scr:]
        ids = [pl.program_id(a) for a in range(len(grid))]
        first = functools.reduce(jnp.logical_and, [i == 0 for i in ids])
        last = functools.reduce(jnp.logical_and, [i == g - 1 for i, g in zip(ids, grid)])
        x, y, c, me = _my_place()

        def local(i):
            return pltpu.make_async_copy(items[i].block(srcs[i], me), items[i].place(dsts[i], me), local_sems.at[i])

        def remote(d, i, sending):
            px, py, pc = _peer(x, y, c, d)
            pk = 4 * px + 2 * py + pc
            return pltpu.make_async_remote_copy(
                src_ref=items[i].block(srcs[i], pk), dst_ref=items[i].place(dsts[i], me if sending else pk),
                send_sem=send_sems.at[d - 1, i], recv_sem=recv_sems.at[d - 1, i],
                device_id=(px, py, pc), device_id_type=MESH)

        @pl.when(first)
        def _():
            for i in range(n):
                local(i).start()
            for d in range(1, N_DEV):
                for i in range(n):
                    remote(d, i, True).start()

        body(*core_in, *core_out, *scratch[:n_scr])

        @pl.when(last)
        def _():
            for d in range(1, N_DEV):
                for i in range(n):
                    remote(d, i, False).wait_recv()
            for d in range(1, N_DEV):
                for i in range(n):
                    remote(d, i, True).wait_send()
            for i in range(n):
                local(i).wait()

    outs = pl.pallas_call(
        wrapped, grid=grid, name=name,
        in_specs=list(in_specs) + [ANY] * (n + len(inits)), out_specs=list(out_specs) + [ANY] * n,
        out_shape=list(out_shape) + [jax.ShapeDtypeStruct(it.dst_shape, BF16) for it in items],
        input_output_aliases={n_in + n + j: n_out + i for j, i in enumerate(inits)},
        scratch_shapes=list(scratch_shapes) + [pltpu.SemaphoreType.DMA((N_DEV - 1, n)),
                                               pltpu.SemaphoreType.DMA((N_DEV - 1, n)), pltpu.SemaphoreType.DMA((n,))],
        compiler_params=_cp(*(["arbitrary"] * len(grid))),
    )(*args, *[it.src for it in items], *[items[i].init for i in inits])
    return list(outs)


def _norm_mm(x, g, w, l, jb, out_dtype, name, tm=1024, items=()):
    T, K = x.shape
    _, nb, _, tn = w.shape
    tm = min(tm, T)

    def body(x_ref, g_ref, w_ref, o_ref, h_ref):
        @pl.when(pl.program_id(1) == 0)
        def _():
            h_ref[...] = _rms(x_ref[...], g_ref[...]).astype(BF16)
        h = h_ref[...]
        for jj in range(jb):
            o_ref[:, jj * tn:(jj + 1) * tn] = jnp.dot(h, w_ref[jj], preferred_element_type=F32).astype(o_ref.dtype)

    return _call(
        body, (x, g, w), grid=(T // tm, nb // jb), name=name, items=items,
        in_specs=[pl.BlockSpec((tm, K), lambda i, j: (i, 0)), pl.BlockSpec((1, K), lambda i, j: (0, 0)),
                  pl.BlockSpec((None, jb, K, tn), lambda i, j: (l, j, 0, 0))],
        out_specs=[pl.BlockSpec((tm, jb * tn), lambda i, j: (i, j)), pl.BlockSpec((tm, K), lambda i, j: (i, 0))],
        out_shape=[jax.ShapeDtypeStruct((T, nb * tn), out_dtype), jax.ShapeDtypeStruct((T, K), BF16)],
        sem=("parallel", "arbitrary"))


def _mm_res(res, a, w, l, name, tn, tm=1024, items=()):
    T, K = a.shape
    N = w.shape[2]
    tm = min(tm, T)

    def body(r_ref, a_ref, w_ref, o_ref):
        o_ref[...] = r_ref[...] + jnp.dot(a_ref[...], w_ref[...], preferred_element_type=F32)

    return _call(
        body, (res, a, w), grid=(T // tm, N // tn), name=name, items=items,
        in_specs=[pl.BlockSpec((tm, tn), lambda i, j: (i, j)), pl.BlockSpec((tm, K), lambda i, j: (i, 0)),
                  pl.BlockSpec((None, K, tn), lambda i, j: (l, 0, j))],
        out_specs=[pl.BlockSpec((tm, tn), lambda i, j: (i, j))],
        out_shape=[jax.ShapeDtypeStruct((T, N), F32)],
        sem=("parallel", "parallel"))


def _mm_nt(a, w, l, out_dtype, name, tm=1024, tn=1024):
    T, K = a.shape
    N = w.shape[1]
    tm = min(tm, T)

    def body(a_ref, w_ref, o_ref):
        o_ref[...] = lax.dot_general(a_ref[...].astype(BF16), w_ref[...], _NT,
                                     preferred_element_type=F32).astype(o_ref.dtype)

    return pl.pallas_call(
        body, grid=(T // tm, N // tn), name=name,
        in_specs=[pl.BlockSpec((tm, K), lambda i, j: (i, 0)), pl.BlockSpec((None, tn, K), lambda i, j: (l, j, 0))],
        out_specs=pl.BlockSpec((tm, tn), lambda i, j: (i, j)),
        out_shape=jax.ShapeDtypeStruct((T, N), out_dtype),
        compiler_params=_cp("parallel", "parallel"))(a, w)


def _mm_nt_acc(a, w, l, jb, name, tm=1024, items=()):
    T = a.shape[0]
    _, nb, K, tn = w.shape
    tm = min(tm, T)

    def body(a_ref, w_ref, o_ref):
        part = lax.dot_general(a_ref[:, :tn], w_ref[0], _NT, preferred_element_type=F32)
        for jj in range(1, jb):
            part += lax.dot_general(a_ref[:, jj * tn:(jj + 1) * tn], w_ref[jj], _NT, preferred_element_type=F32)

        @pl.when(pl.program_id(1) == 0)
        def _():
            o_ref[...] = part

        @pl.when(pl.program_id(1) > 0)
        def _():
            o_ref[...] += part

    return _call(
        body, (a, w), grid=(T // tm, nb // jb), name=name, items=items,
        in_specs=[pl.BlockSpec((tm, jb * tn), lambda i, j: (i, j)),
                  pl.BlockSpec((None, jb, K, tn), lambda i, j: (l, j, 0, 0))],
        out_specs=[pl.BlockSpec((tm, K), lambda i, j: (i, 0))],
        out_shape=[jax.ShapeDtypeStruct((T, K), F32)],
        sem=("parallel", "arbitrary"))


def _mm_tn_cols(lhs, rhs, tn, jb, name, tm=1024, items=()):
    T, K = lhs.shape
    nb = rhs.shape[1] // tn
    tm = min(tm, T)
    nt = T // tm

    def body(l_ref, r_ref, o_ref, acc):
        part = lax.dot_general(l_ref[...], r_ref[...], _TN, preferred_element_type=F32)

        @pl.when(pl.program_id(1) == 0)
        def _():
            acc[...] = part

        @pl.when(pl.program_id(1) > 0)
        def _():
            acc[...] += part

        @pl.when(pl.program_id(1) == nt - 1)
        def _():
            for jj in range(jb):
                o_ref[jj] = acc[:, jj * tn:(jj + 1) * tn].astype(BF16)

    return _call(
        body, (lhs, rhs), grid=(nb // jb, nt), name=name, items=items,
        in_specs=[pl.BlockSpec((tm, K), lambda j, t: (t, 0)), pl.BlockSpec((tm, jb * tn), lambda j, t: (t, j))],
        out_specs=[pl.BlockSpec((jb, K, tn), lambda j, t: (j, 0, 0))],
        out_shape=[jax.ShapeDtypeStruct((nb, K, tn), BF16)],
        scratch_shapes=[pltpu.VMEM((K, jb * tn), F32)],
        sem=("parallel", "arbitrary"))


def _mm_tn_rows(lhs, rhs, tk, name, tm=1024):
    T, Kl = lhs.shape
    N = rhs.shape[1]
    tm = min(tm, T)
    nt = T // tm

    def body(l_ref, r_ref, o_ref, acc):
        part = lax.dot_general(l_ref[...], r_ref[...].astype(BF16), _TN, preferred_element_type=F32)

        @pl.when(pl.program_id(1) == 0)
        def _():
            acc[...] = part

        @pl.when(pl.program_id(1) > 0)
        def _():
            acc[...] += part

        @pl.when(pl.program_id(1) == nt - 1)
        def _():
            o_ref[...] = acc[...].astype(BF16)

    return pl.pallas_call(
        body, grid=(Kl // tk, nt), name=name,
        in_specs=[pl.BlockSpec((tm, tk), lambda l, t: (t, l)), pl.BlockSpec((tm, N), lambda l, t: (t, 0))],
        out_specs=pl.BlockSpec((tk, N), lambda l, t: (l, 0)),
        out_shape=jax.ShapeDtypeStruct((Kl, N), BF16),
        scratch_shapes=[pltpu.VMEM((tk, N), F32)],
        compiler_params=_cp("parallel", "arbitrary"))(lhs, rhs)


def _swiglu_fwd(gu, name, tm=256, items=()):
    T = gu.shape[0]

    def body(gu_ref, o_ref):
        gate = gu_ref[:, :D_FF_PAD].astype(F32)
        up = gu_ref[:, D_FF_PAD:].astype(F32)
        o_ref[...] = (jax.nn.silu(gate) * up).astype(BF16)

    return _call(
        body, (gu,), grid=(T // tm,), name=name, items=items,
        in_specs=[pl.BlockSpec((tm, 2 * D_FF_PAD), lambda i: (i, 0))],
        out_specs=[pl.BlockSpec((tm, D_FF_PAD), lambda i: (i, 0))],
        out_shape=[jax.ShapeDtypeStruct((T, D_FF_PAD), BF16)],
        sem=("parallel",))


def _swiglu_bwd(gu, dact, name, tm=256):
    T = gu.shape[0]

    def body(gu_ref, d_ref, o_ref):
        gate = gu_ref[:, :D_FF_PAD].astype(F32)
        up = gu_ref[:, D_FF_PAD:].astype(F32)
        d = d_ref[...].astype(F32)
        sig = jax.nn.sigmoid(gate)
        silu = gate * sig
        o_ref[:, :D_FF_PAD] = (d * up * (sig + silu * (1.0 - sig))).astype(BF16)
        o_ref[:, D_FF_PAD:] = (d * silu).astype(BF16)

    return pl.pallas_call(
        body, grid=(T // tm,), name=name,
        in_specs=[pl.BlockSpec((tm, 2 * D_FF_PAD), lambda i: (i, 0)), pl.BlockSpec((tm, D_FF_PAD), lambda i: (i, 0))],
        out_specs=pl.BlockSpec((tm, 2 * D_FF_PAD), lambda i: (i, 0)),
        out_shape=jax.ShapeDtypeStruct((T, 2 * D_FF_PAD), BF16),
        compiler_params=_cp("parallel"))(gu, dact)


def _rms_bwd_add(x, g, dh, dres, name, tm=512, items=()):
    T, K = x.shape

    def body(x_ref, g_ref, dh_ref, dr_ref, dx_ref, dg_ref):
        _, vjp = jax.vjp(_rms, x_ref[...], g_ref[...])
        dx, dg = vjp(dh_ref[...])
        dx_ref[...] = dr_ref[...] + dx

        @pl.when(pl.program_id(0) == 0)
        def _():
            dg_ref[...] = dg

        @pl.when(pl.program_id(0) > 0)
        def _():
            dg_ref[...] += dg

    row = pl.BlockSpec((tm, K), lambda i: (i, 0))
    vec = pl.BlockSpec((1, K), lambda i: (0, 0))
    return _call(
        body, (x, g, dh, dres), grid=(T // tm,), name=name, items=items,
        in_specs=[row, vec, row, row], out_specs=[row, vec],
        out_shape=[jax.ShapeDtypeStruct((T, K), F32), jax.ShapeDtypeStruct((1, K), F32)],
        sem=("arbitrary",))


def _loss_head(x, g, target, name, tm=512):
    T, K = x.shape

    def loss_fn(xv, gv, tv):
        err = _rms(xv, gv) - tv
        return 0.5 * jnp.sum(jnp.mean(err * err, axis=-1, keepdims=True), axis=0, keepdims=True)

    def body(x_ref, g_ref, t_ref, l_ref, dx_ref, dg_ref):
        val, vjp = jax.vjp(lambda xv, gv: loss_fn(xv, gv, t_ref[...]), x_ref[...], g_ref[...])
        dx, dg = vjp(jnp.ones((1, 1), F32))
        dx_ref[...] = dx
        lval = jnp.broadcast_to(val, (1, 128))

        @pl.when(pl.program_id(0) == 0)
        def _():
            dg_ref[...] = dg
            l_ref[...] = lval

        @pl.when(pl.program_id(0) > 0)
        def _():
            dg_ref[...] += dg
            l_ref[...] += lval

    row = pl.BlockSpec((tm, K), lambda i: (i, 0))
    vec = pl.BlockSpec((1, K), lambda i: (0, 0))
    return pl.pallas_call(
        body, grid=(T // tm,), name=name,
        in_specs=[row, vec, row], out_specs=[pl.BlockSpec((1, 128), lambda i: (0, 0)), row, vec],
        out_shape=[jax.ShapeDtypeStruct((1, 128), F32), jax.ShapeDtypeStruct((T, K), F32),
                   jax.ShapeDtypeStruct((1, K), F32)],
        compiler_params=_cp("arbitrary"))(x, g, target)


SGU_ROWS = 4 * BLOCK


def _sgu_fn(u, v, wm, bt, gain):
    ug = jax.nn.gelu(u)
    vg = jax.nn.gelu(v)
    row = lax.broadcasted_iota(jnp.int32, (BLOCK, BLOCK), 0)
    col = lax.broadcasted_iota(jnp.int32, (BLOCK, BLOCK), 1)
    tri = row >= col
    normed = []
    for h in range(GROUP_HEADS):
        vh = vg[:, h * HEAD_DIM:(h + 1) * HEAD_DIM]
        xc = vh - jnp.mean(vh, axis=-1, keepdims=True)
        normed.append(xc * lax.rsqrt(jnp.mean(xc * xc, axis=-1, keepdims=True) + EPS))
    vn = jnp.concatenate(normed, axis=1)
    wcat = jnp.concatenate([jnp.where(tri, wm[h], 0.0) for h in range(GROUP_HEADS)], axis=1)
    bias = jnp.concatenate([jnp.broadcast_to(bt[:, h:h + 1], (BLOCK, HEAD_DIM)) for h in range(GROUP_HEADS)], axis=1)
    mixes = []
    for c in range(u.shape[0] // BLOCK):
        chunk = vn[c * BLOCK:(c + 1) * BLOCK]
        stacked = jnp.concatenate([jnp.where(_head_mask(h, chunk.shape), chunk, 0.0) for h in range(GROUP_HEADS)], axis=0)
        mixes.append(_bdot(wcat, stacked) + bias)
    return _rms(ug * jnp.concatenate(mixes, axis=0), gain)


def _sgu_fwd(proj, wm, bt, gain, name):
    T = proj.shape[0]
    rows = min(SGU_ROWS, T)

    def body(u_ref, v_ref, w_ref, b_ref, g_ref, o_ref):
        o_ref[...] = _sgu_fn(u_ref[...], v_ref[...], w_ref[...], b_ref[...], g_ref[...]).astype(BF16)

    full = lambda shape: pl.BlockSpec(shape, lambda i: (0,) * len(shape))
    return pl.pallas_call(
        body, grid=(T // rows,), name=name,
        in_specs=[pl.BlockSpec((rows, GROUP_WIDTH), lambda i: (i, 0)), pl.BlockSpec((rows, GROUP_WIDTH), lambda i: (i, 1)),
                  full(wm.shape), full(bt.shape), full(gain.shape)],
        out_specs=pl.BlockSpec((rows, GROUP_WIDTH), lambda i: (i, 0)),
        out_shape=jax.ShapeDtypeStruct((T, GROUP_WIDTH), BF16),
        compiler_params=_cp("parallel"))(proj, proj, wm, bt, gain)


def _acc_out(first, refs, vals):
    @pl.when(first)
    def _():
        for r, v in zip(refs, vals):
            r[...] = v

    @pl.when(jnp.logical_not(first))
    def _():
        for r, v in zip(refs, vals):
            r[...] += v


def _sgu_bwd(proj, wm, bt, gain, dy, name, items=()):
    T = proj.shape[0]

    def body(u_ref, v_ref, w_ref, b_ref, g_ref, dy_ref, duv_ref, dw_ref, db_ref, dg_ref):
        _, vjp = jax.vjp(_sgu_fn, u_ref[...], v_ref[...], w_ref[...], b_ref[...], g_ref[...])
        du, dv, dw, db, dg = vjp(dy_ref[...])
        duv_ref[:, :GROUP_WIDTH] = du.astype(BF16)
        duv_ref[:, GROUP_WIDTH:] = dv.astype(BF16)
        _acc_out(pl.program_id(0) == 0, (dw_ref, db_ref, dg_ref), (dw, db, dg))

    full = lambda shape: pl.BlockSpec(shape, lambda i: (0,) * len(shape))
    rows = min(SGU_ROWS, T)
    return _call(
        body, (proj, proj, wm, bt, gain, dy), grid=(T // rows,), name=name, items=items,
        in_specs=[pl.BlockSpec((rows, GROUP_WIDTH), lambda i: (i, 0)), pl.BlockSpec((rows, GROUP_WIDTH), lambda i: (i, 1)),
                  full(wm.shape), full(bt.shape), full(gain.shape),
                  pl.BlockSpec((rows, GROUP_WIDTH), lambda i: (i, 0))],
        out_specs=[pl.BlockSpec((rows, 2 * GROUP_WIDTH), lambda i: (i, 0)), full(wm.shape), full(bt.shape), full(gain.shape)],
        out_shape=[jax.ShapeDtypeStruct((T, 2 * GROUP_WIDTH), BF16), jax.ShapeDtypeStruct(wm.shape, F32),
                   jax.ShapeDtypeStruct(bt.shape, F32), jax.ShapeDtypeStruct(gain.shape, F32)],
        sem=("arbitrary",))


def _pool_consts(seq):
    t = lax.broadcasted_iota(jnp.int32, (seq, GROUP_WIDTH), 0)
    grp = lax.broadcasted_iota(jnp.int32, (seq, GROUP_WIDTH), 1) // (GROUP_WIDTH // len(POOL_WINDOWS))
    win = jnp.where(grp == 0, POOL_WINDOWS[0], jnp.where(grp == 1, POOL_WINDOWS[1],
                    jnp.where(grp == 2, POOL_WINDOWS[2], POOL_WINDOWS[3])))
    count = jnp.minimum(t + 1, win).astype(F32)
    return t, grp, count


def _by_group(grp, vals):
    return jnp.where(grp == 0, vals[0], jnp.where(grp == 1, vals[1], jnp.where(grp == 2, vals[2], vals[3])))


def _window_sums(x, t, seq, back):
    def shift(a, k):
        if back:
            return jnp.where(t >= k, pltpu.roll(a, k, 0), 0.0)
        return jnp.where(t < seq - k, pltpu.roll(a, seq - k, 0), 0.0)
    sums = []
    a, k = x, 1
    for _ in POOL_WINDOWS:
        a = a + shift(a, k)
        sums.append(a)
        k *= 2
    return sums


def _pool_tail(y, wbd, scale, gain):
    return _rms(_bdot(y, wbd) * scale, gain)


def _pool_fwd(proj, wbd, scale, gain, seq, name):
    T = proj.shape[0]

    def body(p_ref, w_ref, s_ref, g_ref, o_ref):
        p = p_ref[...]
        t, grp, count = _pool_consts(seq)
        y = _by_group(grp, _window_sums(p, t, seq, True)) / count - p
        o_ref[...] = _pool_tail(y, w_ref[...], s_ref[...], g_ref[...]).astype(BF16)

    full = lambda shape: pl.BlockSpec(shape, lambda b: (0,) * len(shape))
    return pl.pallas_call(
        body, grid=(T // seq,), name=name,
        in_specs=[pl.BlockSpec((seq, GROUP_WIDTH), lambda b: (b, 2)), full(wbd.shape), full(scale.shape), full(gain.shape)],
        out_specs=pl.BlockSpec((seq, GROUP_WIDTH), lambda b: (b, 0)),
        out_shape=jax.ShapeDtypeStruct((T, GROUP_WIDTH), BF16),
        compiler_params=_cp("parallel"))(proj, wbd, scale, gain)


def _pool_bwd(proj, wbd, scale, gain, dy, seq, name):
    T = proj.shape[0]

    def body(p_ref, w_ref, s_ref, g_ref, dy_ref, dp_ref, dw_ref, ds_ref, dg_ref):
        p = p_ref[...]
        t, grp, count = _pool_consts(seq)
        y = _by_group(grp, _window_sums(p, t, seq, True)) / count - p
        _, vjp = jax.vjp(_pool_tail, y, w_ref[...], s_ref[...], g_ref[...])
        d_y, dw, ds, dg = vjp(dy_ref[...])
        dp = _by_group(grp, _window_sums(d_y / count, t, seq, False)) - d_y
        dp_ref[...] = dp.astype(BF16)
        _acc_out(pl.program_id(0) == 0, (dw_ref, ds_ref, dg_ref), (dw, ds, dg))

    full = lambda shape: pl.BlockSpec(shape, lambda b: (0,) * len(shape))
    return pl.pallas_call(
        body, grid=(T // seq,), name=name,
        in_specs=[pl.BlockSpec((seq, GROUP_WIDTH), lambda b: (b, 2)), full(wbd.shape), full(scale.shape), full(gain.shape),
                  pl.BlockSpec((seq, GROUP_WIDTH), lambda b: (b, 1))],
        out_specs=[pl.BlockSpec((seq, GROUP_WIDTH), lambda b: (b, 0)), full(wbd.shape), full(scale.shape), full(gain.shape)],
        out_shape=[jax.ShapeDtypeStruct((T, GROUP_WIDTH), BF16), jax.ShapeDtypeStruct(wbd.shape, F32),
                   jax.ShapeDtypeStruct(scale.shape, F32), jax.ShapeDtypeStruct(gain.shape, F32)],
        compiler_params=_cp("arbitrary"))(proj, wbd, scale, gain, dy)


def _swa_fn(q, kv_prev, kv_cur, sinks, tab, gain, first):
    half = GROUP_WIDTH // 2
    k2 = jnp.concatenate([kv_prev[:, :half], kv_cur[:, :half]], axis=0)
    v2 = jnp.concatenate([kv_prev[:, half:], kv_cur[:, half:]], axis=0)
    per_query_head = lambda a: jnp.concatenate(
        [a[:, (h // 2) * HEAD_DIM:(h // 2 + 1) * HEAD_DIM] for h in range(GROUP_HEADS)], axis=1)
    qs = jnp.concatenate([jnp.where(_head_mask(h, q.shape), q, 0.0) for h in range(GROUP_HEADS)], axis=0)
    qi = lax.broadcasted_iota(jnp.int32, (HEAD_ROWS, 2 * BLOCK), 0) & (BLOCK - 1)
    kj = lax.broadcasted_iota(jnp.int32, (HEAD_ROWS, 2 * BLOCK), 1)
    dist = qi + BLOCK - kj
    mask = (dist >= 0) & (dist < BLOCK) & ((kj >= BLOCK) | jnp.logical_not(first))
    logits = _bdot_nt(qs, per_query_head(k2)) * ATT_SCALE + tab.reshape(HEAD_ROWS, 2 * BLOCK)
    logits = jnp.where(mask, logits, -1e30)
    sink = jnp.concatenate([jnp.broadcast_to(sinks[:, h:h + 1], (BLOCK, 1)) for h in range(GROUP_HEADS)], axis=0)
    m = lax.stop_gradient(jnp.maximum(jnp.max(logits, axis=1, keepdims=True), sink))
    p = jnp.exp(logits - m)
    probs = p / (jnp.sum(p, axis=1, keepdims=True) + jnp.exp(sink - m))
    out = _bdot(probs, per_query_head(v2))
    y = jnp.zeros_like(q)
    for h in range(GROUP_HEADS):
        y = y + jnp.where(_head_mask(h, q.shape), out[h * BLOCK:(h + 1) * BLOCK], 0.0)
    return _rms(y, gain)


def _swa_specs(nblk):
    q = pl.BlockSpec((BLOCK, GROUP_WIDTH), lambda b, i: (b * nblk + i, 3))
    cur = pl.BlockSpec((BLOCK, GROUP_WIDTH), lambda b, i: (b * nblk + i, 4))
    prev = pl.BlockSpec((BLOCK, GROUP_WIDTH), lambda b, i: (b * nblk + jnp.maximum(i - 1, 0), 4))
    return q, prev, cur


def _swa_fwd(proj, sinks, tab, gain, seq, name, items=()):
    T = proj.shape[0]
    nblk = seq // BLOCK

    def body(q_ref, kp_ref, kc_ref, s_ref, t_ref, g_ref, o_ref):
        o_ref[...] = _swa_fn(q_ref[...], kp_ref[...], kc_ref[...], s_ref[...], t_ref[...], g_ref[...],
                             pl.program_id(1) == 0).astype(BF16)

    full = lambda shape: pl.BlockSpec(shape, lambda b, i: (0,) * len(shape))
    return _call(
        body, (proj, proj, proj, sinks, tab, gain), grid=(T // seq, nblk), name=name, items=items,
        in_specs=[*_swa_specs(nblk), full(sinks.shape), full(tab.shape), full(gain.shape)],
        out_specs=[pl.BlockSpec((BLOCK, GROUP_WIDTH), lambda b, i: (b * nblk + i, 0))],
        out_shape=[jax.ShapeDtypeStruct((T, GROUP_WIDTH), BF16)],
        sem=("parallel", "parallel"))


def _swa_bwd(proj, sinks, tab, gain, dy, seq, name, items=()):
    T = proj.shape[0]
    nblk = seq // BLOCK

    def body(q_ref, kp_ref, kc_ref, s_ref, t_ref, g_ref, dy_ref, dq_ref, dkp_ref, dkc_ref, ds_ref, dt_ref, dg_ref):
        first = pl.program_id(1) == 0
        fn = functools.partial(_swa_fn, first=first)
        _, vjp = jax.vjp(fn, q_ref[...], kp_ref[...], kc_ref[...], s_ref[...], t_ref[...], g_ref[...])
        dq, dkp, dkc, ds, dt, dg = vjp(dy_ref[...])
        dq_ref[...] = dq.astype(BF16)
        dkp_ref[...] = dkp
        dkc_ref[...] = dkc
        _acc_out((pl.program_id(0) == 0) & first, (ds_ref, dt_ref, dg_ref), (ds, dt, dg))

    full = lambda shape: pl.BlockSpec(shape, lambda b, i: (0,) * len(shape))
    blk = lambda c: pl.BlockSpec((BLOCK, GROUP_WIDTH), lambda b, i: (b * nblk + i, c))
    return _call(
        body, (proj, proj, proj, sinks, tab, gain, dy), grid=(T // seq, nblk), name=name, items=items,
        in_specs=[*_swa_specs(nblk), full(sinks.shape), full(tab.shape), full(gain.shape), blk(2)],
        out_specs=[blk(0), blk(0), blk(0), full(sinks.shape), full(tab.shape), full(gain.shape)],
        out_shape=[jax.ShapeDtypeStruct((T, GROUP_WIDTH), BF16), jax.ShapeDtypeStruct((T, GROUP_WIDTH), F32),
                   jax.ShapeDtypeStruct((T, GROUP_WIDTH), F32), jax.ShapeDtypeStruct(sinks.shape, F32),
                   jax.ShapeDtypeStruct(tab.shape, F32), jax.ShapeDtypeStruct(gain.shape, F32)],
        sem=("arbitrary", "arbitrary"))


def _shift_add(cur, prev, seq, name):
    T = cur.shape[0]
    nblk = seq // BLOCK

    def body(c_ref, p_ref, o_ref):
        last = pl.program_id(1) == nblk - 1
        o_ref[...] = (c_ref[...] + jnp.where(last, 0.0, p_ref[...])).astype(BF16)

    return pl.pallas_call(
        body, grid=(T // seq, nblk), name=name,
        in_specs=[pl.BlockSpec((BLOCK, GROUP_WIDTH), lambda b, i: (b * nblk + i, 0)),
                  pl.BlockSpec((BLOCK, GROUP_WIDTH), lambda b, i: (b * nblk + jnp.minimum(i + 1, nblk - 1), 0))],
        out_specs=pl.BlockSpec((BLOCK, GROUP_WIDTH), lambda b, i: (b * nblk + i, 0)),
        out_shape=jax.ShapeDtypeStruct((T, GROUP_WIDTH), BF16),
        compiler_params=_cp("parallel", "parallel"))(cur, prev)


def _t5_bucket(dist):
    max_exact = N_BUCKETS // 2
    df = jnp.maximum(dist, 1).astype(F32)
    large = max_exact + (jnp.log(df / max_exact) / jnp.log(jnp.float32(MAX_DISTANCE / max_exact))
                         * (N_BUCKETS - max_exact)).astype(jnp.int32)
    return jnp.where(dist < max_exact, dist, jnp.minimum(large, N_BUCKETS - 1))


def _bucket_map():
    dist = (jnp.arange(BLOCK)[:, None] + BLOCK) - jnp.arange(2 * BLOCK)[None, :]
    return _t5_bucket(jnp.clip(dist, 0, BLOCK - 1)).astype(jnp.int32)


def _bias_table(rel_bias, buckets, name):
    def body(rb_ref, bk_ref, o_ref):
        bk = bk_ref[...]
        rb = rb_ref[...]
        for h in range(GROUP_HEADS):
            acc = jnp.zeros((BLOCK, 2 * BLOCK), F32)
            for b in range(N_BUCKETS):
                acc = jnp.where(bk == b, rb[b:b + 1, h:h + 1], acc)
            o_ref[h] = acc

    return pl.pallas_call(body, name=name, out_shape=jax.ShapeDtypeStruct((GROUP_HEADS, BLOCK, 2 * BLOCK), F32),
                          compiler_params=_cp())(rel_bias, buckets)


def _bias_table_bwd(dtab, buckets, name):
    def body(dt_ref, bk_ref, o_ref):
        bk = bk_ref[...]
        row = lax.broadcasted_iota(jnp.int32, (N_BUCKETS, GROUP_HEADS), 0)
        col = lax.broadcasted_iota(jnp.int32, (N_BUCKETS, GROUP_HEADS), 1)
        acc = jnp.zeros((N_BUCKETS, GROUP_HEADS), F32)
        for h in range(GROUP_HEADS):
            dt = dt_ref[h]
            for b in range(N_BUCKETS):
                s = jnp.sum(jnp.where(bk == b, dt, 0.0), keepdims=True)
                acc = acc + jnp.where((row == b) & (col == h), s, 0.0)
        o_ref[...] = acc

    return pl.pallas_call(body, name=name, out_shape=jax.ShapeDtypeStruct((N_BUCKETS, GROUP_HEADS), F32),
                          compiler_params=_cp())(dtab, buckets)


HEAD_ROWS = GROUP_HEADS * BLOCK


def _stack_heads(x):
    return jnp.concatenate([jnp.where(_head_mask(h, x.shape), x, 0.0) for h in range(GROUP_HEADS)], axis=0).astype(BF16)


def _sb_tile(qs, kb, q0, k0):
    z = lax.dot_general(qs, kb, _NT, preferred_element_type=F32)
    row = lax.broadcasted_iota(jnp.int32, (HEAD_ROWS, BLOCK), 0) & (BLOCK - 1)
    col = lax.broadcasted_iota(jnp.int32, (HEAD_ROWS, BLOCK), 1)
    causal = (k0 + col) < (q0 + row)
    ls_neg = -(jnp.maximum(z, 0.0) + jnp.log(1.0 + jnp.exp(-jnp.abs(z))))
    return jnp.where(causal, ls_neg, 0.0), ls_neg + z, causal


SB_DEAD = -104.0
SB_FIRST_LANE = GROUP_HEADS


def _tri(strict_upper_src):
    r = lax.broadcasted_iota(jnp.int32, (BLOCK, BLOCK), 0)
    c = lax.broadcasted_iota(jnp.int32, (BLOCK, BLOCK), 1)
    cond = {"gt": r > c, "le": r <= c, "lt": r < c}[strict_upper_src]
    return jnp.where(cond, 1.0, 0.0).astype(BF16)


def _sb_fwd(proj, gain, seq, name, items=()):
    T = proj.shape[0]
    nblk = seq // BLOCK

    def body(q_ref, k_ref, v_ref, g_ref, o_ref, raw_ref, bt_ref):
        i = pl.program_id(1)
        q = q_ref[...]
        u_gt = _tri("gt")
        lane = lax.broadcasted_iota(jnp.int32, (BLOCK, BLOCK), 1)
        heads = [slice(h * HEAD_DIM, (h + 1) * HEAD_DIM) for h in range(GROUP_HEADS)]
        rows = [slice(h * BLOCK, (h + 1) * BLOCK) for h in range(GROUP_HEADS)]
        qs = _stack_heads(q * ATT_SCALE)

        def live(carry):
            j, _, cb = carry
            return (j >= 0) & (jnp.max(cb) > SB_DEAD)

        def step(carry):
            j, accs, cb = carry
            ks = pl.multiple_of(j * BLOCK, BLOCK)
            kb = k_ref[pl.ds(ks, BLOCK), :].astype(BF16)
            vb = v_ref[pl.ds(ks, BLOCK), :].astype(BF16)
            b, a, causal = _sb_tile(qs, kb, i * BLOCK, j * BLOCK)
            tail = _split_dot(b, u_gt) + cb
            w = jnp.where(causal, jnp.exp(a + tail), 0.0).astype(BF16)
            accs = tuple(accs[h] + jnp.dot(w[rows[h]], vb[:, hs], preferred_element_type=F32)
                         for h, hs in enumerate(heads))
            return j - 1, accs, cb + jnp.sum(b, axis=1, keepdims=True)

        zero_acc = tuple(jnp.zeros((BLOCK, HEAD_DIM), F32) for _ in heads)
        j_end, accs, cb = lax.while_loop(live, step, (i, zero_acc, jnp.zeros((HEAD_ROWS, 1), F32)))
        side = jnp.where(lane == SB_FIRST_LANE, (j_end + 1).astype(F32), 0.0)
        for h in range(GROUP_HEADS):
            side = jnp.where(lane == h, cb[rows[h]], side)
        raw = jnp.concatenate(accs, axis=1)
        raw_ref[...] = raw
        bt_ref[...] = side
        o_ref[...] = _rms(raw, g_ref[...]).astype(BF16)

    return _call(
        body, (proj, proj, proj, gain), grid=(T // seq, nblk), name=name, items=items,
        in_specs=[pl.BlockSpec((BLOCK, GROUP_WIDTH), lambda b, i: (b * nblk + i, 5)),
                  pl.BlockSpec((seq, GROUP_WIDTH), lambda b, i: (b, 6)),
                  pl.BlockSpec((seq, GROUP_WIDTH), lambda b, i: (b, 7)),
                  pl.BlockSpec(gain.shape, lambda b, i: (0, 0))],
        out_specs=[pl.BlockSpec((BLOCK, GROUP_WIDTH), lambda b, i: (b * nblk + i, 0)),
                   pl.BlockSpec((BLOCK, GROUP_WIDTH), lambda b, i: (b * nblk + i, 0)),
                   pl.BlockSpec((BLOCK, BLOCK), lambda b, i: (b * nblk + i, 0))],
        out_shape=[jax.ShapeDtypeStruct((T, GROUP_WIDTH), BF16), jax.ShapeDtypeStruct((T, GROUP_WIDTH), F32),
                   jax.ShapeDtypeStruct((T, BLOCK), F32)],
        sem=("parallel", "parallel"))


def _sb_bwd(proj, gain, raw, btot, dy, seq, name, items=()):
    T = proj.shape[0]
    nblk = seq // BLOCK

    def body(q_ref, k_ref, v_ref, g_ref, raw_ref, bt_ref, dy_ref, dq_ref, dk_ref, dv_ref, dg_ref):
        i = pl.program_id(1)

        @pl.when(i == 0)
        def _():
            dk_ref[...] = jnp.zeros_like(dk_ref)
            dv_ref[...] = jnp.zeros_like(dv_ref)

        rawv = raw_ref[...]
        _, vjp = jax.vjp(_rms, rawv, g_ref[...])
        do, dg = vjp(dy_ref[...])
        _acc_out((pl.program_id(0) == 0) & (i == 0), (dg_ref,), (dg,))
        q = q_ref[...]
        bt = bt_ref[...]
        u_le = _tri("le")
        u_lt = _tri("lt")
        heads = [slice(h * HEAD_DIM, (h + 1) * HEAD_DIM) for h in range(GROUP_HEADS)]
        rows = [slice(h * BLOCK, (h + 1) * BLOCK) for h in range(GROUP_HEADS)]
        qs = _stack_heads(q * ATT_SCALE)
        dos = _stack_heads(do)
        bts = jnp.concatenate([bt[:, h:h + 1] for h in range(GROUP_HEADS)], axis=0)
        first = jnp.max(bt[:, SB_FIRST_LANE:SB_FIRST_LANE + 1]).astype(jnp.int32)
        first = jnp.minimum(jnp.maximum(first, 0), i)

        def step(j, carry):
            dqs, cb, cg = carry
            ks = pl.multiple_of(j * BLOCK, BLOCK)
            kb = k_ref[pl.ds(ks, BLOCK), :].astype(BF16)
            vb = v_ref[pl.ds(ks, BLOCK), :].astype(BF16)
            b, a, causal = _sb_tile(qs, kb, i * BLOCK, j * BLOCK)
            tail = bts - (_split_dot(b, u_le) + cb)
            w = jnp.where(causal, jnp.exp(a + tail), 0.0)
            sig = jnp.exp(a)
            g = w * lax.dot_general(dos, vb, _NT, preferred_element_type=F32)
            gpre = _split_dot(g, u_lt) + cg
            dz = jnp.where(causal, g * (1.0 - sig) - gpre * sig, 0.0).astype(BF16)
            dqs = tuple(dqs[h] + jnp.dot(dz[rows[h]], kb[:, hs], preferred_element_type=F32)
                        for h, hs in enumerate(heads))
            dk_ref[pl.ds(ks, BLOCK), :] += lax.dot_general(dz, qs, _TN, preferred_element_type=F32)
            dv_ref[pl.ds(ks, BLOCK), :] += lax.dot_general(w.astype(BF16), dos, _TN, preferred_element_type=F32)
            return dqs, cb + jnp.sum(b, axis=1, keepdims=True), cg + jnp.sum(g, axis=1, keepdims=True)

        zero_dq = tuple(jnp.zeros((BLOCK, HEAD_DIM), F32) for _ in heads)
        zero = jnp.zeros((HEAD_ROWS, 1), F32)
        dqs, _, _ = lax.fori_loop(first, i + 1, step, (zero_dq, zero, zero))
        dq_ref[...] = (jnp.concatenate(dqs, axis=1) * ATT_SCALE).astype(BF16)

    blk = lambda c: pl.BlockSpec((BLOCK, GROUP_WIDTH), lambda b, i: (b * nblk + i, c))
    seqblk = lambda c: pl.BlockSpec((seq, GROUP_WIDTH), lambda b, i: (b, c))
    vec = pl.BlockSpec(gain.shape, lambda b, i: (0, 0))
    return _call(
        body, (proj, proj, proj, gain, raw, btot, dy), grid=(T // seq, nblk), name=name, items=items,
        in_specs=[blk(5), seqblk(6), seqblk(7), vec, blk(0), pl.BlockSpec((BLOCK, BLOCK), lambda b, i: (b * nblk + i, 0)),
                  blk(3)],
        out_specs=[blk(0), seqblk(0), seqblk(0), vec],
        out_shape=[jax.ShapeDtypeStruct((T, GROUP_WIDTH), BF16), jax.ShapeDtypeStruct((T, GROUP_WIDTH), F32),
                   jax.ShapeDtypeStruct((T, GROUP_WIDTH), F32), jax.ShapeDtypeStruct(gain.shape, F32)],
        sem=("arbitrary", "arbitrary"))


def _layer_params(l, sgu_w, sgu_b, pool_w, pool_scale, swa_sinks, mix_out_gain, norm_mix, norm_ffn):
    gains = mix_out_gain[l].reshape(4, 1, GROUP_WIDTH)
    return dict(
        wm=sgu_w[l], bt=sgu_b[l].T,
        wbd=jax.scipy.linalg.block_diag(*[pool_w[l, g] for g in range(len(POOL_WINDOWS))]),
        scale=pool_scale[l][None], sinks=swa_sinks[l][None],
        gain=[gains[m] for m in range(4)], norm_mix=norm_mix[l][None], norm_ffn=norm_ffn[l][None])


def _as_weights(g_in, g_out, g_gu, g_down):
    return (g_in[None], g_out.reshape(1, D_MODEL, D_MODEL), g_gu[None], g_down.reshape(1, D_FF_PAD, D_MODEL))


def _gather_item(src, l, dst_shape, rows=None, down=False, init=None):
    r0, nr = rows or (0, src.shape[1])
    if down:
        place = lambda dst, k: dst.at[k // 2, pl.ds((k % 2) * DOWN_ROWS + r0, nr), :]
    else:
        place = lambda dst, k: dst.at[k, pl.ds(r0, nr), :]
    return _Item(src, lambda s, k: s.at[l, pl.ds(r0, nr), :], dst_shape, place, init)


def _layer_fwd(l, x, p, w, tab, seq, shards):
    win, wout, wgu, wd = w
    nl = l + 1
    s_in, s_out, s_gu, s_down = shards or (None,) * 4
    ride = lambda item: (item(),) if shards else ()
    proj, h1, *g_out = _norm_mm(x, p["norm_mix"], win, 0, 4, F32, f"in_proj_{l}",
                                items=ride(lambda: _gather_item(s_out, nl, (N_DEV,) + s_out.shape[1:])))
    ya = _sgu_fwd(proj, p["wm"], p["bt"], p["gain"][0], f"sgu_fwd_{l}")
    yb = _pool_fwd(proj, p["wbd"], p["scale"], p["gain"][1], seq, f"pool_fwd_{l}")
    half_down = DOWN_ROWS // 2
    yc, *g_down = _swa_fwd(proj, p["sinks"], tab, p["gain"][2], seq, f"swa_fwd_{l}", items=ride(
        lambda: _gather_item(s_down, nl, (4, FF_PAD, D_MODEL), (0, half_down), True,
                             jnp.zeros((4, FF_PAD, D_MODEL), BF16))))
    half_gu = D_MODEL // 2
    yd, raw, btot, *g_gu = _sb_fwd(proj, p["gain"][3], seq, f"sb_fwd_{l}", items=ride(
        lambda: _gather_item(s_gu, nl, (N_DEV,) + s_gu.shape[1:], (0, half_gu))))
    ycat = jnp.concatenate([ya, yb, yc, yd], axis=1)
    xm, = _mm_res(x, ycat, wout, 0, f"out_proj_{l}", tn=D_MODEL)
    gu, h2, *g_gu = _norm_mm(xm, p["norm_ffn"], wgu, 0, 1, BF16, f"ffn_up_{l}", items=ride(
        lambda: _gather_item(s_gu, nl, (N_DEV,) + s_gu.shape[1:], (half_gu, half_gu), init=g_gu[0])))
    act, *g_down = _swiglu_fwd(gu, f"swiglu_fwd_{l}", items=ride(
        lambda: _gather_item(s_down, nl, (4, FF_PAD, D_MODEL), (half_down, half_down), True, g_down[0])))
    xo, *g_in = _mm_res(xm, act, wd, 0, f"ffn_down_{l}", tn=D_MODEL // 2,
                        items=ride(lambda: _gather_item(s_in, nl, (N_DEV,) + s_in.shape[1:])))
    w_next = _as_weights(g_in[0], g_out[0], g_gu[0], g_down[0]) if shards else None
    return xo, (x, proj, h1, ycat, raw, btot, xm, gu, h2, act), w_next


def _layer_bwd(l, dxo, saved, p, w, tab, seq):
    win, wout, wgu, wd = w
    x, proj, h1, ycat, raw, btot, xm, gu, h2, act = saved
    out_rows = D_MODEL // N_DEV
    half = D_MODEL // 2

    def rows_of(g, r0, init=None):
        cut = lambda a, k: a.at[k, pl.ds(r0, half), :]
        return _Item(g, cut, g.shape, cut, init)

    dact = _mm_nt(dxo, wd, 0, BF16, f"ffn_down_dx_{l}")
    g_wd = _mm_tn_rows(act, dxo, FF_PAD, f"ffn_down_dw_{l}")
    dgu = _swiglu_bwd(gu, dact, f"swiglu_bwd_{l}")
    send_down = _Item(g_wd, lambda src, k: src.at[pl.ds((k // 2) * FF_PAD + (k % 2) * DOWN_ROWS, DOWN_ROWS), :],
                      (N_DEV, DOWN_ROWS, D_MODEL))
    g_wgu, p_down = _mm_tn_cols(h2, dgu, FF_PAD, 1, f"ffn_up_dw_{l}", items=(send_down,))
    dh2, p_gu = _mm_nt_acc(dgu, wgu, 0, 1, f"ffn_up_dx_{l}", items=(rows_of(g_wgu, 0),))
    dxm, g_norm_ffn = _rms_bwd_add(xm, p["norm_ffn"], dh2, dxo, f"norm_ffn_bwd_{l}")
    dycat = _mm_nt(dxm, wout, 0, F32, f"out_proj_dx_{l}")
    g_wout = _mm_tn_rows(ycat, dxm, D_MODEL, f"out_proj_dw_{l}")
    send_out = _Item(g_wout, lambda src, k: src.at[pl.ds(k * out_rows, out_rows), :], (N_DEV, out_rows, D_MODEL))
    duv, g_wm, g_bt, g_ga, p_out = _sgu_bwd(proj, p["wm"], p["bt"], p["gain"][0], dycat, f"sgu_bwd_{l}",
                                            items=(send_out,))
    dp, g_wbd, g_scale, g_gb = _pool_bwd(proj, p["wbd"], p["scale"], p["gain"][1], dycat, seq, f"pool_bwd_{l}")
    dq, dkp, dkc, g_sinks, g_tab, g_gc = _swa_bwd(proj, p["sinks"], tab, p["gain"][2], dycat, seq, f"swa_bwd_{l}")
    dkv = _shift_add(dkc, dkp, seq, f"swa_dkv_{l}")
    dqd, dkd, dvd, g_gd, p_gu = _sb_bwd(proj, p["gain"][3], raw, btot, dycat, seq, f"sb_bwd_{l}",
                                        items=(rows_of(g_wgu, half, p_gu),))
    dproj = jnp.concatenate([duv, dp, dq, dkv, dqd, dkd.astype(BF16), dvd.astype(BF16)], axis=1)
    g_win, = _mm_tn_cols(h1, dproj, GROUP_WIDTH, 4, f"in_proj_dw_{l}")
    dh1, p_in = _mm_nt_acc(dproj, win, 0, 4, f"in_proj_dx_{l}", items=(rows_of(g_win, 0),))
    dx, g_norm_mix, p_in = _rms_bwd_add(x, p["norm_mix"], dh1, dxm, f"norm_mix_bwd_{l}",
                                        items=(rows_of(g_win, half, p_in),))
    ng = len(POOL_WINDOWS)
    gd = GROUP_WIDTH // ng
    small = dict(
        sgu_w=g_wm, sgu_b=g_bt.T,
        pool_w=jnp.stack([g_wbd[g * gd:(g + 1) * gd, g * gd:(g + 1) * gd] for g in range(ng)]),
        pool_scale=g_scale[0], swa_sinks=g_sinks[0],
        mix_out_gain=jnp.concatenate([g_ga[0], g_gb[0], g_gc[0], g_gd[0]]),
        norm_mix=g_norm_mix[0], norm_ffn=g_norm_ffn[0])
    return dx, (p_in, p_out, p_gu, p_down), small, g_tab


def _local_step(x, target, w0, shards, sgu_w, sgu_b, pool_w, pool_scale, swa_sinks, rel_bias, mix_out_gain,
                norm_mix, norm_ffn, norm_final, seq):
    buckets = _bucket_map()
    tab = _bias_table(rel_bias, buckets, "bias_table")
    params = [_layer_params(l, sgu_w, sgu_b, pool_w, pool_scale, swa_sinks, mix_out_gain, norm_mix, norm_ffn)
              for l in range(DEPTH)]
    saved, weights = [], [w0]
    for l in range(DEPTH):
        x, s, w_next = _layer_fwd(l, x, params[l], weights[l], tab, seq, shards if l + 1 < DEPTH else None)
        saved.append(s)
        weights.append(w_next)
    loss, dx, g_final = _loss_head(x, norm_final[None], target, "loss_head")
    big, small, g_tab = [None] * DEPTH, [None] * DEPTH, None
    for l in reversed(range(DEPTH)):
        dx, big[l], small[l], t = _layer_bwd(l, dx, saved[l], params[l], weights[l], tab, seq)
        g_tab = t if g_tab is None else g_tab + t
    g_small = {k: jnp.stack([small[l][k] for l in range(DEPTH)]) for k in small[0]}
    g_small["rel_bias"] = _bias_table_bwd(g_tab, buckets, "bias_table_bwd")
    g_small["norm_final"] = g_final[0]
    return loss, dx, big, g_small


def _cast_pad(w, cols, name):
    L, r, c = w.shape

    def body(w_ref, o_ref):
        if cols != c:
            o_ref[...] = jnp.zeros_like(o_ref)
        o_ref[:, :c] = w_ref[...].astype(BF16)

    return pl.pallas_call(
        body, grid=(L,), name=name,
        in_specs=[pl.BlockSpec((None, r, c), lambda l: (l, 0, 0))],
        out_specs=pl.BlockSpec((None, r, cols), lambda l: (l, 0, 0)),
        out_shape=jax.ShapeDtypeStruct((L, r, cols), BF16),
        compiler_params=_cp("parallel"))(w)


def _down_rows(ref, k):
    return ref.at[:, k // 2, pl.ds((k % 2) * DOWN_ROWS, DOWN_ROWS), :]


def _all_gather_weights(s_in, s_out, s_gu, s_down):
    L = s_in.shape[0]
    shards = (s_in, s_out, s_gu, s_down)
    down_full = jnp.zeros((L, 4, FF_PAD, D_MODEL), BF16)
    n = len(shards)

    def body(i0, i1, i2, i3, _, o0, o1, o2, o3, send_sems, recv_sems, local_sems):
        srcs = (i0, i1, i2, i3)
        outs = (o0, o1, o2, o3)
        x, y, c, _me = _my_place()
        chips = [(1 - x, y), (x, 1 - y), (1 - x, 1 - y)]

        def place(t, dev):
            k = 4 * dev[0] + 2 * dev[1] + dev[2]
            return _down_rows(outs[t], k) if t == 3 else outs[t].at[:, k]

        def copy(slot, t, block, to, from_shard):
            return pltpu.make_async_remote_copy(
                src_ref=srcs[t] if from_shard else place(t, block), dst_ref=place(t, block),
                send_sem=send_sems.at[slot, t], recv_sem=recv_sems.at[slot, t], device_id=to, device_id_type=MESH)

        me, sibling = (x, y, c), (x, y, 1 - c)
        mine = [pltpu.make_async_copy(srcs[t], place(t, me), local_sems.at[t]) for t in range(n)]
        for cp in mine:
            cp.start()
        first = [copy(0, t, me, sibling, True) for t in range(n)]
        first += [copy(1 + j, t, me, (*chip, c), True) for j, chip in enumerate(chips) for t in range(n)]
        for cp in first:
            cp.start()
        passed = []
        for j, chip in enumerate(chips):
            for t in range(n):
                copy(1 + j, t, (*chip, c), me, False).wait_recv()
                cp = copy(4 + j, t, (*chip, c), sibling, False)
                cp.start()
                passed.append(cp)
        for t in range(n):
            copy(0, t, sibling, me, False).wait_recv()
        for j, chip in enumerate(chips):
            for t in range(n):
                copy(4 + j, t, (*chip, 1 - c), me, False).wait_recv()
        for cp in first + passed:
            cp.wait_send()
        for cp in mine:
            cp.wait()

    shapes = [jax.ShapeDtypeStruct((L, N_DEV) + s.shape[1:], BF16) for s in shards[:3]]
    shapes.append(jax.ShapeDtypeStruct(down_full.shape, BF16))
    return pl.pallas_call(
        body, name="all_gather_weights", out_shape=shapes,
        in_specs=[ANY] * 5, out_specs=[ANY] * 4, input_output_aliases={4: 3},
        scratch_shapes=[pltpu.SemaphoreType.DMA((7, n)), pltpu.SemaphoreType.DMA((7, n)), pltpu.SemaphoreType.DMA((n,))],
        )(*shards, down_full)


def _all_reduce_small(part):
    rows = part.shape[0]

    def body(p_ref, o_ref, buf, send_sems, recv_sems):
        x, y, c, me = _my_place()
        buf[me] = p_ref[...]
        sends = []
        for d in range(1, N_DEV):
            peer = _peer(x, y, c, d)
            sends.append(pltpu.make_async_remote_copy(
                src_ref=p_ref, dst_ref=buf.at[me], send_sem=send_sems.at[d - 1], recv_sem=recv_sems.at[d - 1],
                device_id=peer, device_id_type=MESH))
        for cp in sends:
            cp.start()
        for d in range(1, N_DEV):
            px, py, pc = _peer(x, y, c, d)
            pltpu.make_async_remote_copy(
                src_ref=p_ref, dst_ref=buf.at[4 * px + 2 * py + pc], send_sem=send_sems.at[d - 1],
                recv_sem=recv_sems.at[d - 1], device_id=(px, py, pc), device_id_type=MESH).wait_recv()
        for cp in sends:
            cp.wait_send()
        total = buf[0]
        for k in range(1, N_DEV):
            total = total + buf[k]
        o_ref[...] = total

    return pl.pallas_call(
        body, name="all_reduce_small", out_shape=jax.ShapeDtypeStruct((rows, 128), F32),
        in_specs=[pl.BlockSpec(memory_space=pltpu.VMEM)], out_specs=pl.BlockSpec(memory_space=pltpu.VMEM),
        scratch_shapes=[pltpu.VMEM((N_DEV, rows, 128), F32), pltpu.SemaphoreType.DMA((N_DEV - 1,)),
                        pltpu.SemaphoreType.DMA((N_DEV - 1,))],
        compiler_params=pltpu.CompilerParams(vmem_limit_bytes=VMEM_LIMIT))(part)


def _adamw(w, g, m, v):
    m = ADAM_B1 * m + (1.0 - ADAM_B1) * g
    v = ADAM_B2 * v + (1.0 - ADAM_B2) * jnp.square(g)
    m_hat = m / (1.0 - ADAM_B1 ** ADAM_STEP)
    v_hat = v / (1.0 - ADAM_B2 ** ADAM_STEP)
    delta = -ADAM_LR * (m_hat / (jnp.sqrt(v_hat) + ADAM_EPS) + ADAM_WD * w)
    return delta, m, v


def _adamw_sharded(parts, w, m, v, tr, name):
    L, r, c = w.shape
    cp = parts[0].shape[-1]
    nrow = r // tr

    def body(*refs):
        p_refs, (w_ref, m_ref, v_ref, g_ref, d_ref, nm_ref, nv_ref) = refs[:L], refs[L:]
        for k in range(L):
            @pl.when(pl.program_id(0) == k)
            def _(p_ref=p_refs[k]):
                g = p_ref[0, :, :c].astype(F32)
                for dev in range(1, N_DEV):
                    g = g + p_ref[dev, :, :c].astype(F32)
                delta, nm, nv = _adamw(w_ref[...], g, m_ref[...], v_ref[...])
                g_ref[...] = g
                d_ref[...] = delta
                nm_ref[...] = nm
                nv_ref[...] = nv

    def part_spec(k):
        return pl.BlockSpec((N_DEV, tr, cp),
                            lambda l, i: (0, jnp.where(l == k, i, jnp.where(l < k, 0, nrow - 1)), 0))

    blk = pl.BlockSpec((None, tr, c), lambda l, i: (l, i, 0))
    out = jax.ShapeDtypeStruct((L, r, c), F32)
    return pl.pallas_call(
        body, grid=(L, nrow), name=name,
        in_specs=[part_spec(k) for k in range(L)] + [blk, blk, blk],
        out_specs=[blk] * 4, out_shape=[out] * 4,
        compiler_params=_cp("arbitrary", "arbitrary"))(*parts, w, m, v)


def _adamw_small(g, w, m, v, name):
    def body(g_ref, w_ref, m_ref, v_ref, d_ref, nm_ref, nv_ref):
        delta, nm, nv = _adamw(w_ref[...], g_ref[...], m_ref[...], v_ref[...])
        d_ref[...] = delta
        nm_ref[...] = nm
        nv_ref[...] = nv

    out = jax.ShapeDtypeStruct(g.shape, F32)
    return pl.pallas_call(body, name=name, out_shape=[out] * 3, compiler_params=_cp())(g, w, m, v)


SMALL = ("sgu_w", "sgu_b", "pool_w", "pool_scale", "swa_sinks", "rel_bias", "mix_out_gain", "norm_mix", "norm_ffn",
         "norm_final")


def _seg_rows(a):
    return -(-a.size // 128)


def _pack(parts, rows):
    segs = [jnp.pad(p.reshape(-1), (0, _seg_rows(p) * 128 - p.size)).reshape(_seg_rows(p), 128) for p in parts]
    used = sum(s.shape[0] for s in segs)
    return jnp.concatenate(segs + [jnp.zeros((rows - used, 128), F32)], axis=0)


def _unpack(buf, like):
    out, at = [], 0
    for a in like:
        out.append(buf[at:at + _seg_rows(a)].reshape(-1)[:a.size].reshape(a.shape))
        at += _seg_rows(a)
    return out


def kernel(x, w_in, w_out, sgu_w, sgu_b, pool_w, pool_scale, swa_sinks, rel_bias, mix_out_gain, norm_mix, norm_ffn, w_gate_up, w_down, norm_final, loss_target, m_w_in, m_w_out, m_sgu_w, m_sgu_b, m_pool_w, m_pool_scale, m_swa_sinks, m_rel_bias, m_mix_out_gain, m_norm_mix, m_norm_ffn, m_w_gate_up, m_w_down, m_norm_final, v_w_in, v_w_out, v_sgu_w, v_sgu_b, v_pool_w, v_pool_scale, v_swa_sinks, v_rel_bias, v_mix_out_gain, v_norm_mix, v_norm_ffn, v_w_gate_up, v_w_down, v_norm_final):
    bl, seq, _ = x.shape
    L = w_in.shape[0]
    shards = (_cast_pad(w_in, GROUP_WIDTH, "shard_w_in"), _cast_pad(w_out, D_MODEL, "shard_w_out"),
              _cast_pad(w_gate_up, FF_PAD, "shard_w_gate_up"), _cast_pad(w_down, D_MODEL, "shard_w_down"))
    first = _all_gather_weights(*[s[:1] for s in shards])
    w0 = _as_weights(*[f[0] for f in first])
    small_w = dict(sgu_w=sgu_w, sgu_b=sgu_b, pool_w=pool_w, pool_scale=pool_scale, swa_sinks=swa_sinks,
                   rel_bias=rel_bias, mix_out_gain=mix_out_gain, norm_mix=norm_mix, norm_ffn=norm_ffn,
                   norm_final=norm_final)
    small_m = dict(sgu_w=m_sgu_w, sgu_b=m_sgu_b, pool_w=m_pool_w, pool_scale=m_pool_scale, swa_sinks=m_swa_sinks,
                   rel_bias=m_rel_bias, mix_out_gain=m_mix_out_gain, norm_mix=m_norm_mix, norm_ffn=m_norm_ffn,
                   norm_final=m_norm_final)
    small_v = dict(sgu_w=v_sgu_w, sgu_b=v_sgu_b, pool_w=v_pool_w, pool_scale=v_pool_scale, swa_sinks=v_swa_sinks,
                   rel_bias=v_rel_bias, mix_out_gain=v_mix_out_gain, norm_mix=v_norm_mix, norm_ffn=v_norm_ffn,
                   norm_final=v_norm_final)
    loss, dx, big, g_small = _local_step(
        x.reshape(bl * seq, D_MODEL), loss_target.reshape(bl * seq, D_MODEL), w0, shards, sgu_w, sgu_b, pool_w,
        pool_scale, swa_sinks, rel_bias, mix_out_gain, norm_mix, norm_ffn, norm_final, seq)
    p_in, p_out, p_gu, p_down = ([big[l][t] for l in range(L)] for t in range(4))
    outs_in = _adamw_sharded(p_in, w_in, m_w_in, v_w_in, 256, "adamw_w_in")
    outs_out = _adamw_sharded(p_out, w_out, m_w_out, v_w_out, D_MODEL // N_DEV, "adamw_w_out")
    outs_gu = _adamw_sharded(p_gu, w_gate_up, m_w_gate_up, v_w_gate_up, 256, "adamw_w_gate_up")
    outs_down = _adamw_sharded(p_down, w_down, m_w_down, v_w_down, DOWN_ROWS // 2, "adamw_w_down")
    rows = -(-(sum(_seg_rows(small_w[k]) for k in SMALL) + 1) // 8) * 8
    packed = _pack([g_small[k] for k in SMALL] + [loss[0]], rows)
    total = _all_reduce_small(packed)
    like = [small_w[k] for k in SMALL]
    d_s, m_s, v_s = _adamw_small(total, _pack(like, rows), _pack([small_m[k] for k in SMALL], rows),
                                 _pack([small_v[k] for k in SMALL], rows), "adamw_small")
    g_list = dict(zip(SMALL, _unpack(total, like)))
    d_list = dict(zip(SMALL, _unpack(d_s, like)))
    m_list = dict(zip(SMALL, _unpack(m_s, like)))
    v_list = dict(zip(SMALL, _unpack(v_s, like)))
    loss_total = total[sum(_seg_rows(a) for a in like), 0]
    big_outs = dict(w_in=outs_in, w_out=outs_out, w_gate_up=outs_gu, w_down=outs_down)
    order = ("w_in", "w_out", "sgu_w", "sgu_b", "pool_w", "pool_scale", "swa_sinks", "rel_bias", "mix_out_gain",
             "norm_mix", "norm_ffn", "w_gate_up", "w_down", "norm_final")
    result = [loss_total, dx.reshape(bl, seq, D_MODEL)]
    for which, small in enumerate((g_list, d_list, m_list, v_list)):
        for name in order:
            result.append(big_outs[name][which] if name in big_outs else small[name])
    return tuple(result)
```

```python
import functools

import jax
import jax.numpy as jnp
from jax import lax
from jax.experimental import pallas as pl
from jax.experimental.pallas import tpu as pltpu

F32 = jnp.float32
BF16 = jnp.bfloat16

N_DEV = 8
DEPTH = 4
D_MODEL = 1024
GROUP_WIDTH = 256
HEAD_DIM = 64
GROUP_HEADS = 4
BLOCK = 128
N_BUCKETS = 32
MAX_DISTANCE = 128
POOL_WINDOWS = (2, 4, 8, 16)
D_FF = 2816
FF_SHARD = D_FF // 4
FF_PAD = 768
D_FF_PAD = 4 * FF_PAD
EPS = 1e-6
ATT_SCALE = HEAD_DIM ** -0.5
ADAM_LR = 0.001
ADAM_B1 = 0.9
ADAM_B2 = 0.999
ADAM_EPS = 1e-08
ADAM_WD = 0.01
ADAM_STEP = 10
VMEM_LIMIT = 56 * 1024 * 1024
MESH_AXES = ("x", "y", "c")


def _cp(*sem):
    return pltpu.CompilerParams(dimension_semantics=sem or None, vmem_limit_bytes=VMEM_LIMIT)


_NT = (((1,), (1,)), ((), ()))
_TN = (((0,), (0,)), ((), ()))


@jax.custom_vjp
def _bdot(a, b):
    return jnp.dot(a.astype(BF16), b.astype(BF16), preferred_element_type=F32)


def _bdot_fwd(a, b):
    return _bdot(a, b), (a.astype(BF16), b.astype(BF16))


def _bdot_bwd(res, ct):
    a, b = res
    c = ct.astype(BF16)
    return (lax.dot_general(c, b, _NT, preferred_element_type=F32),
            lax.dot_general(a, c, _TN, preferred_element_type=F32))


_bdot.defvjp(_bdot_fwd, _bdot_bwd)


@jax.custom_vjp
def _bdot_nt(a, b):
    return lax.dot_general(a.astype(BF16), b.astype(BF16), _NT, preferred_element_type=F32)


def _bdot_nt_fwd(a, b):
    return _bdot_nt(a, b), (a.astype(BF16), b.astype(BF16))


def _bdot_nt_bwd(res, ct):
    a, b = res
    c = ct.astype(BF16)
    return (jnp.dot(c, b, preferred_element_type=F32),
            lax.dot_general(c, a, _TN, preferred_element_type=F32))


_bdot_nt.defvjp(_bdot_nt_fwd, _bdot_nt_bwd)


def _rms(x, g):
    return x * lax.rsqrt(jnp.mean(x * x, axis=-1, keepdims=True) + EPS) * g


def _split_dot(x, u):
    hi = x.astype(BF16)
    lo = (x - hi.astype(F32)).astype(BF16)
    return jnp.dot(hi, u, preferred_element_type=F32) + jnp.dot(lo, u, preferred_element_type=F32)


def _head_mask(h, shape):
    col = lax.broadcasted_iota(jnp.int32, shape, 1)
    return (col >= h * HEAD_DIM) & (col < (h + 1) * HEAD_DIM)


ANY = pl.BlockSpec(memory_space=pl.ANY)
MESH = pl.DeviceIdType.MESH
DOWN_ROWS = D_FF // N_DEV


def _my_place():
    x, y, c = (lax.axis_index(a) for a in MESH_AXES)
    return x, y, c, 4 * x + 2 * y + c


def _peer(x, y, c, d):
    return (x ^ (d >> 2), y ^ ((d >> 1) & 1), c ^ (d & 1))


class _Item:
    def __init__(self, src, block, dst_shape, place=None, init=None):
        self.src, self.block, self.dst_shape, self.init = src, block, dst_shape, init
        self.place = place or (lambda dst, k: dst.at[k])


def _call(body, args, *, grid, in_specs, out_specs, out_shape, sem, name, scratch_shapes=(), items=()):
    if not items:
        outs = pl.pallas_call(body, grid=grid, in_specs=in_specs, out_specs=out_specs, out_shape=out_shape, name=name,
                              scratch_shapes=list(scratch_shapes), compiler_params=_cp(*sem))(*args)
        return list(outs) if isinstance(outs, (list, tuple)) else [outs]
    n_in, n_out, n_scr, n = len(in_specs), len(out_specs), len(scratch_shapes), len(items)
    inits = [i for i, it in enumerate(items) if it.init is not None]

    def wrapped(*refs):
        core_in, srcs = refs[:n_in], refs[n_in:n_in + n]
        off = n_in + n + len(inits)
        core_out, dsts = refs[off:off + n_out], refs[off + n_out:off + n_out + n]
        scratch = refs[off + n_out + n:]
        send_sems, recv_sems, local_sems = scratch[n_scr:]
        ids = [pl.program_id(a) for a in range(len(grid))]
        first = functools.reduce(jnp.logical_and, [i == 0 for i in ids])
        last = functools.reduce(jnp.logical_and, [i == g - 1 for i, g in zip(ids, grid)])
        x, y, c, me = _my_place()

        def local(i):
            return pltpu.make_async_copy(items[i].block(srcs[i], me), items[i].place(dsts[i], me), local_sems.at[i])

        def remote(d, i, sending):
            px, py, pc = _peer(x, y, c, d)
            pk = 4 * px + 2 * py + pc
            return pltpu.make_async_remote_copy(
                src_ref=items[i].block(srcs[i], pk), dst_ref=items[i].place(dsts[i], me if sending else pk),
                send_sem=send_sems.at[d - 1, i], recv_sem=recv_sems.at[d - 1, i],
                device_id=(px, py, pc), device_id_type=MESH)

        @pl.when(first)
        def _():
            for i in range(n):
                local(i).start()
            for d in range(1, N_DEV):
                for i in range(n):
                    remote(d, i, True).start()

        body(*core_in, *core_out, *scratch[:n_scr])

        @pl.when(last)
        def _():
            for d in range(1, N_DEV):
                for i in range(n):
                    remote(d, i, False).wait_recv()
            for d in range(1, N_DEV):
                for i in range(n):
                    remote(d, i, True).wait_send()
            for i in range(n):
                local(i).wait()

    outs = pl.pallas_call(
        wrapped, grid=grid, name=name,
        in_specs=list(in_specs) + [ANY] * (n + len(inits)), out_specs=list(out_specs) + [ANY] * n,
        out_shape=list(out_shape) + [jax.ShapeDtypeStruct(it.dst_shape, it.src.dtype) for it in items],
        input_output_aliases={n_in + n + j: n_out + i for j, i in enumerate(inits)},
        scratch_shapes=list(scratch_shapes) + [pltpu.SemaphoreType.DMA((N_DEV - 1, n)),
                                               pltpu.SemaphoreType.DMA((N_DEV - 1, n)), pltpu.SemaphoreType.DMA((n,))],
        compiler_params=_cp(*(["arbitrary"] * len(grid))),
    )(*args, *[it.src for it in items], *[items[i].init for i in inits])
    return list(outs)


def _norm_mm(x, g, w, l, jb, out_dtype, name, tm=1024, items=()):
    T, K = x.shape
    _, nb, _, tn = w.shape
    tm = min(tm, T)

    def body(x_ref, g_ref, w_ref, o_ref, h_ref):
        @pl.when(pl.program_id(1) == 0)
        def _():
            h_ref[...] = _rms(x_ref[...], g_ref[...]).astype(BF16)
        h = h_ref[...]
        for jj in range(jb):
            o_ref[:, jj * tn:(jj + 1) * tn] = jnp.dot(h, w_ref[jj], preferred_element_type=F32).astype(o_ref.dtype)

    return _call(
        body, (x, g, w), grid=(T // tm, nb // jb), name=name, items=items,
        in_specs=[pl.BlockSpec((tm, K), lambda i, j: (i, 0)), pl.BlockSpec((1, K), lambda i, j: (0, 0)),
                  pl.BlockSpec((None, jb, K, tn), lambda i, j: (l, j, 0, 0))],
        out_specs=[pl.BlockSpec((tm, jb * tn), lambda i, j: (i, j)), pl.BlockSpec((tm, K), lambda i, j: (i, 0))],
        out_shape=[jax.ShapeDtypeStruct((T, nb * tn), out_dtype), jax.ShapeDtypeStruct((T, K), BF16)],
        sem=("parallel", "arbitrary"))


def _mm_res(res, a, w, l, name, tn, tm=1024, items=()):
    T, K = a.shape
    N = w.shape[2]
    tm = min(tm, T)

    def body(r_ref, a_ref, w_ref, o_ref):
        o_ref[...] = r_ref[...] + jnp.dot(a_ref[...], w_ref[...], preferred_element_type=F32)

    return _call(
        body, (res, a, w), grid=(T // tm, N // tn), name=name, items=items,
        in_specs=[pl.BlockSpec((tm, tn), lambda i, j: (i, j)), pl.BlockSpec((tm, K), lambda i, j: (i, 0)),
                  pl.BlockSpec((None, K, tn), lambda i, j: (l, 0, j))],
        out_specs=[pl.BlockSpec((tm, tn), lambda i, j: (i, j))],
        out_shape=[jax.ShapeDtypeStruct((T, N), F32)],
        sem=("parallel", "parallel"))


def _mm_nt(a, w, l, out_dtype, name, tm=1024, tn=1024):
    T, K = a.shape
    N = w.shape[1]
    tm = min(tm, T)

    def body(a_ref, w_ref, o_ref):
        o_ref[...] = lax.dot_general(a_ref[...].astype(BF16), w_ref[...], _NT,
                                     preferred_element_type=F32).astype(o_ref.dtype)

    return pl.pallas_call(
        body, grid=(T // tm, N // tn), name=name,
        in_specs=[pl.BlockSpec((tm, K), lambda i, j: (i, 0)), pl.BlockSpec((None, tn, K), lambda i, j: (l, j, 0))],
        out_specs=pl.BlockSpec((tm, tn), lambda i, j: (i, j)),
        out_shape=jax.ShapeDtypeStruct((T, N), out_dtype),
        compiler_params=_cp("parallel", "parallel"))(a, w)


def _mm_norm_bwd(a, w, l, jb, x, g, dres, name, tm=1024, items=(), block_of=lambda j: j):
    T = a.shape[0]
    _, nb, K, tn = w.shape
    tm = min(tm, T)
    nj = nb // jb
    sub = min(256, tm)

    def body(a_ref, w_ref, x_ref, g_ref, r_ref, o_ref, dg_ref):
        part = lax.dot_general(a_ref[:, :tn], w_ref[0], _NT, preferred_element_type=F32)
        for jj in range(1, jb):
            part += lax.dot_general(a_ref[:, jj * tn:(jj + 1) * tn], w_ref[jj], _NT, preferred_element_type=F32)

        @pl.when(pl.program_id(1) == 0)
        def _():
            o_ref[...] = part

        @pl.when(pl.program_id(1) > 0)
        def _():
            o_ref[...] += part

        @pl.when(pl.program_id(1) == nj - 1)
        def _():
            dg = jnp.zeros((1, K), F32)
            for r in range(tm // sub):
                rows = pl.ds(r * sub, sub)
                _, vjp = jax.vjp(_rms, x_ref[rows, :], g_ref[...])
                dx, dg_r = vjp(o_ref[rows, :])
                o_ref[rows, :] = r_ref[rows, :] + dx
                dg = dg + dg_r
            _acc_out(pl.program_id(0) == 0, (dg_ref,), (dg,))

    row = pl.BlockSpec((tm, K), lambda i, j: (i, 0))
    vec = pl.BlockSpec((1, K), lambda i, j: (0, 0))
    return _call(
        body, (a, w, x, g, dres), grid=(T // tm, nj), name=name, items=items,
        in_specs=[pl.BlockSpec((tm, jb * tn), lambda i, j: (i, j)),
                  pl.BlockSpec((None, jb, K, tn), lambda i, j: (l, block_of(j), 0, 0)), row, vec, row],
        out_specs=[row, vec],
        out_shape=[jax.ShapeDtypeStruct((T, K), F32), jax.ShapeDtypeStruct((1, K), F32)],
        sem=("arbitrary", "arbitrary"))


def _mm_tn_cols(lhs, rhs, tn, jb, name, tm=1024, items=(), block_of=lambda j: j):
    T, K = lhs.shape
    nb = rhs.shape[1] // tn
    tm = min(tm, T)
    nt = T // tm

    def body(l_ref, r_ref, o_ref, acc):
        part = lax.dot_general(l_ref[...], r_ref[...], _TN, preferred_element_type=F32)

        @pl.when(pl.program_id(1) == 0)
        def _():
            acc[...] = part

        @pl.when(pl.program_id(1) > 0)
        def _():
            acc[...] += part

        @pl.when(pl.program_id(1) == nt - 1)
        def _():
            for jj in range(jb):
                o_ref[jj] = acc[:, jj * tn:(jj + 1) * tn].astype(BF16)

    return _call(
        body, (lhs, rhs), grid=(nb // jb, nt), name=name, items=items,
        in_specs=[pl.BlockSpec((tm, K), lambda j, t: (t, 0)), pl.BlockSpec((tm, jb * tn), lambda j, t: (t, j))],
        out_specs=[pl.BlockSpec((jb, K, tn), lambda j, t: (block_of(j), 0, 0))],
        out_shape=[jax.ShapeDtypeStruct((nb, K, tn), BF16)],
        scratch_shapes=[pltpu.VMEM((K, jb * tn), F32)],
        sem=("parallel", "arbitrary"))


def _mm_tn_rows(lhs, rhs, tk, name, tm=1024):
    T, Kl = lhs.shape
    N = rhs.shape[1]
    tm = min(tm, T)
    nt = T // tm

    def body(l_ref, r_ref, o_ref, acc):
        part = lax.dot_general(l_ref[...], r_ref[...].astype(BF16), _TN, preferred_element_type=F32)

        @pl.when(pl.program_id(1) == 0)
        def _():
            acc[...] = part

        @pl.when(pl.program_id(1) > 0)
        def _():
            acc[...] += part

        @pl.when(pl.program_id(1) == nt - 1)
        def _():
            o_ref[...] = acc[...].astype(BF16)

    return pl.pallas_call(
        body, grid=(Kl // tk, nt), name=name,
        in_specs=[pl.BlockSpec((tm, tk), lambda l, t: (t, l)), pl.BlockSpec((tm, N), lambda l, t: (t, 0))],
        out_specs=pl.BlockSpec((tk, N), lambda l, t: (l, 0)),
        out_shape=jax.ShapeDtypeStruct((Kl, N), BF16),
        scratch_shapes=[pltpu.VMEM((tk, N), F32)],
        compiler_params=_cp("parallel", "arbitrary"))(lhs, rhs)


N_FF_CHUNK = D_FF_PAD // FF_PAD


def _ffn_chunk_block(j):
    return (j % 2) * N_FF_CHUNK + j // 2


def _ffn_up_act(x, g, w, name, tm=1024, items=()):
    T, K = x.shape
    tm = min(tm, T)

    def body(x_ref, g_ref, wg_ref, wu_ref, gu_ref, act_ref, h_ref):
        @pl.when(pl.program_id(1) == 0)
        def _():
            h_ref[...] = _rms(x_ref[...], g_ref[...]).astype(BF16)
        h = h_ref[...]
        gate = jnp.dot(h, wg_ref[...], preferred_element_type=F32)
        up = jnp.dot(h, wu_ref[...], preferred_element_type=F32)
        gu_ref[:, :FF_PAD] = gate.astype(BF16)
        gu_ref[:, FF_PAD:] = up.astype(BF16)
        act_ref[...] = (jax.nn.silu(gate) * up).astype(BF16)

    return _call(
        body, (x, g, w, w), grid=(T // tm, N_FF_CHUNK), name=name, items=items,
        in_specs=[pl.BlockSpec((tm, K), lambda i, j: (i, 0)), pl.BlockSpec((1, K), lambda i, j: (0, 0)),
                  pl.BlockSpec((None, None, K, FF_PAD), lambda i, j: (0, j, 0, 0)),
                  pl.BlockSpec((None, None, K, FF_PAD), lambda i, j: (0, j + N_FF_CHUNK, 0, 0))],
        out_specs=[pl.BlockSpec((tm, 2 * FF_PAD), lambda i, j: (i, j)), pl.BlockSpec((tm, FF_PAD), lambda i, j: (i, j)),
                   pl.BlockSpec((tm, K), lambda i, j: (i, 0))],
        out_shape=[jax.ShapeDtypeStruct((T, 2 * D_FF_PAD), BF16), jax.ShapeDtypeStruct((T, D_FF_PAD), BF16),
                   jax.ShapeDtypeStruct((T, K), BF16)],
        sem=("parallel", "arbitrary"))


def _ffn_down_dx(dxo, w, gu, name, tm=1024):
    T, K = dxo.shape
    tm = min(tm, T)

    def body(d_ref, w_ref, gu_ref, o_ref):
        d = lax.dot_general(d_ref[...].astype(BF16), w_ref[...], _NT, preferred_element_type=F32)
        gate = gu_ref[:, :FF_PAD].astype(F32)
        up = gu_ref[:, FF_PAD:].astype(F32)
        sig = jax.nn.sigmoid(gate)
        silu = gate * sig
        o_ref[:, :FF_PAD] = (d * up * (sig + silu * (1.0 - sig))).astype(BF16)
        o_ref[:, FF_PAD:] = (d * silu).astype(BF16)

    return pl.pallas_call(
        body, grid=(T // tm, N_FF_CHUNK), name=name,
        in_specs=[pl.BlockSpec((tm, K), lambda i, j: (i, 0)), pl.BlockSpec((None, FF_PAD, K), lambda i, j: (0, j, 0)),
                  pl.BlockSpec((tm, 2 * FF_PAD), lambda i, j: (i, j))],
        out_specs=pl.BlockSpec((tm, 2 * FF_PAD), lambda i, j: (i, j)),
        out_shape=jax.ShapeDtypeStruct((T, 2 * D_FF_PAD), BF16),
        compiler_params=_cp("parallel", "parallel"))(dxo, w, gu)


def _loss_head(x, g, target, name, tm=512):
    T, K = x.shape

    def loss_fn(xv, gv, tv):
        err = _rms(xv, gv) - tv
        return 0.5 * jnp.sum(jnp.mean(err * err, axis=-1, keepdims=True), axis=0, keepdims=True)

    def body(x_ref, g_ref, t_ref, l_ref, dx_ref, dg_ref):
        val, vjp = jax.vjp(lambda xv, gv: loss_fn(xv, gv, t_ref[...]), x_ref[...], g_ref[...])
        dx, dg = vjp(jnp.ones((1, 1), F32))
        dx_ref[...] = dx
        lval = jnp.broadcast_to(val, (1, 128))

        @pl.when(pl.program_id(0) == 0)
        def _():
            dg_ref[...] = dg
            l_ref[...] = lval

        @pl.when(pl.program_id(0) > 0)
        def _():
            dg_ref[...] += dg
            l_ref[...] += lval

    row = pl.BlockSpec((tm, K), lambda i: (i, 0))
    vec = pl.BlockSpec((1, K), lambda i: (0, 0))
    return pl.pallas_call(
        body, grid=(T // tm,), name=name,
        in_specs=[row, vec, row], out_specs=[pl.BlockSpec((1, 128), lambda i: (0, 0)), row, vec],
        out_shape=[jax.ShapeDtypeStruct((1, 128), F32), jax.ShapeDtypeStruct((T, K), F32),
                   jax.ShapeDtypeStruct((1, K), F32)],
        compiler_params=_cp("arbitrary"))(x, g, target)


SGU_ROWS = 4 * BLOCK


def _sgu_fn(u, v, wm, bt, gain):
    ug = jax.nn.gelu(u)
    vg = jax.nn.gelu(v)
    row = lax.broadcasted_iota(jnp.int32, (BLOCK, BLOCK), 0)
    col = lax.broadcasted_iota(jnp.int32, (BLOCK, BLOCK), 1)
    tri = row >= col
    normed = []
    for h in range(GROUP_HEADS):
        vh = vg[:, h * HEAD_DIM:(h + 1) * HEAD_DIM]
        xc = vh - jnp.mean(vh, axis=-1, keepdims=True)
        normed.append(xc * lax.rsqrt(jnp.mean(xc * xc, axis=-1, keepdims=True) + EPS))
    vn = jnp.concatenate(normed, axis=1)
    wcat = jnp.concatenate([jnp.where(tri, wm[h], 0.0) for h in range(GROUP_HEADS)], axis=1)
    bias = jnp.concatenate([jnp.broadcast_to(bt[:, h:h + 1], (BLOCK, HEAD_DIM)) for h in range(GROUP_HEADS)], axis=1)
    mixes = []
    for c in range(u.shape[0] // BLOCK):
        chunk = vn[c * BLOCK:(c + 1) * BLOCK]
        stacked = jnp.concatenate([jnp.where(_head_mask(h, chunk.shape), chunk, 0.0) for h in range(GROUP_HEADS)], axis=0)
        mixes.append(_bdot(wcat, stacked) + bias)
    return _rms(ug * jnp.concatenate(mixes, axis=0), gain)


def _sgu_fwd(proj, wm, bt, gain, name, items=()):
    T = proj.shape[0]
    rows = min(SGU_ROWS, T)

    def body(u_ref, v_ref, w_ref, b_ref, g_ref, o_ref):
        o_ref[...] = _sgu_fn(u_ref[...], v_ref[...], w_ref[...], b_ref[...], g_ref[...]).astype(BF16)

    full = lambda shape: pl.BlockSpec(shape, lambda i: (0,) * len(shape))
    return _call(
        body, (proj, proj, wm, bt, gain), grid=(T // rows,), name=name, items=items,
        in_specs=[pl.BlockSpec((rows, GROUP_WIDTH), lambda i: (i, 0)), pl.BlockSpec((rows, GROUP_WIDTH), lambda i: (i, 1)),
                  full(wm.shape), full(bt.shape), full(gain.shape)],
        out_specs=[pl.BlockSpec((rows, GROUP_WIDTH), lambda i: (i, 0))],
        out_shape=[jax.ShapeDtypeStruct((T, GROUP_WIDTH), BF16)],
        sem=("parallel",))


def _acc_out(first, refs, vals):
    @pl.when(first)
    def _():
        for r, v in zip(refs, vals):
            r[...] = v

    @pl.when(jnp.logical_not(first))
    def _():
        for r, v in zip(refs, vals):
            r[...] += v


def _sgu_bwd(proj, wm, bt, gain, dy, name, items=()):
    T = proj.shape[0]

    def body(u_ref, v_ref, w_ref, b_ref, g_ref, dy_ref, duv_ref, dw_ref, db_ref, dg_ref):
        _, vjp = jax.vjp(_sgu_fn, u_ref[...], v_ref[...], w_ref[...], b_ref[...], g_ref[...])
        du, dv, dw, db, dg = vjp(dy_ref[...])
        duv_ref[:, :GROUP_WIDTH] = du.astype(BF16)
        duv_ref[:, GROUP_WIDTH:] = dv.astype(BF16)
        _acc_out(pl.program_id(0) == 0, (dw_ref, db_ref, dg_ref), (dw, db, dg))

    full = lambda shape: pl.BlockSpec(shape, lambda i: (0,) * len(shape))
    rows = min(SGU_ROWS, T)
    return _call(
        body, (proj, proj, wm, bt, gain, dy), grid=(T // rows,), name=name, items=items,
        in_specs=[pl.BlockSpec((rows, GROUP_WIDTH), lambda i: (i, 0)), pl.BlockSpec((rows, GROUP_WIDTH), lambda i: (i, 1)),
                  full(wm.shape), full(bt.shape), full(gain.shape),
                  pl.BlockSpec((rows, GROUP_WIDTH), lambda i: (i, 0))],
        out_specs=[pl.BlockSpec((rows, 2 * GROUP_WIDTH), lambda i: (i, 0)), full(wm.shape), full(bt.shape), full(gain.shape)],
        out_shape=[jax.ShapeDtypeStruct((T, 2 * GROUP_WIDTH), BF16), jax.ShapeDtypeStruct(wm.shape, F32),
                   jax.ShapeDtypeStruct(bt.shape, F32), jax.ShapeDtypeStruct(gain.shape, F32)],
        sem=("arbitrary",))


def _pool_consts(seq):
    t = lax.broadcasted_iota(jnp.int32, (seq, GROUP_WIDTH), 0)
    grp = lax.broadcasted_iota(jnp.int32, (seq, GROUP_WIDTH), 1) // (GROUP_WIDTH // len(POOL_WINDOWS))
    win = jnp.where(grp == 0, POOL_WINDOWS[0], jnp.where(grp == 1, POOL_WINDOWS[1],
                    jnp.where(grp == 2, POOL_WINDOWS[2], POOL_WINDOWS[3])))
    count = jnp.minimum(t + 1, win).astype(F32)
    return t, grp, count


def _by_group(grp, vals):
    return jnp.where(grp == 0, vals[0], jnp.where(grp == 1, vals[1], jnp.where(grp == 2, vals[2], vals[3])))


def _window_sums(x, t, seq, back):
    def shift(a, k):
        if back:
            return jnp.where(t >= k, pltpu.roll(a, k, 0), 0.0)
        return jnp.where(t < seq - k, pltpu.roll(a, seq - k, 0), 0.0)
    sums = []
    a, k = x, 1
    for _ in POOL_WINDOWS:
        a = a + shift(a, k)
        sums.append(a)
        k *= 2
    return sums


def _pool_tail(y, wbd, scale, gain):
    return _rms(_bdot(y, wbd) * scale, gain)


def _pool_fwd(proj, wbd, scale, gain, seq, name):
    T = proj.shape[0]

    def body(p_ref, w_ref, s_ref, g_ref, o_ref):
        p = p_ref[...]
        t, grp, count = _pool_consts(seq)
        y = _by_group(grp, _window_sums(p, t, seq, True)) / count - p
        o_ref[...] = _pool_tail(y, w_ref[...], s_ref[...], g_ref[...]).astype(BF16)

    full = lambda shape: pl.BlockSpec(shape, lambda b: (0,) * len(shape))
    return pl.pallas_call(
        body, grid=(T // seq,), name=name,
        in_specs=[pl.BlockSpec((seq, GROUP_WIDTH), lambda b: (b, 2)), full(wbd.shape), full(scale.shape), full(gain.shape)],
        out_specs=pl.BlockSpec((seq, GROUP_WIDTH), lambda b: (b, 0)),
        out_shape=jax.ShapeDtypeStruct((T, GROUP_WIDTH), BF16),
        compiler_params=_cp("parallel"))(proj, wbd, scale, gain)


def _pool_bwd(proj, wbd, scale, gain, dy, seq, name):
    T = proj.shape[0]

    def body(p_ref, w_ref, s_ref, g_ref, dy_ref, dp_ref, dw_ref, ds_ref, dg_ref):
        p = p_ref[...]
        t, grp, count = _pool_consts(seq)
        y = _by_group(grp, _window_sums(p, t, seq, True)) / count - p
        _, vjp = jax.vjp(_pool_tail, y, w_ref[...], s_ref[...], g_ref[...])
        d_y, dw, ds, dg = vjp(dy_ref[...])
        dp = _by_group(grp, _window_sums(d_y / count, t, seq, False)) - d_y
        dp_ref[...] = dp.astype(BF16)
        _acc_out(pl.program_id(0) == 0, (dw_ref, ds_ref, dg_ref), (dw, ds, dg))

    full = lambda shape: pl.BlockSpec(shape, lambda b: (0,) * len(shape))
    return pl.pallas_call(
        body, grid=(T // seq,), name=name,
        in_specs=[pl.BlockSpec((seq, GROUP_WIDTH), lambda b: (b, 2)), full(wbd.shape), full(scale.shape), full(gain.shape),
                  pl.BlockSpec((seq, GROUP_WIDTH), lambda b: (b, 1))],
        out_specs=[pl.BlockSpec((seq, GROUP_WIDTH), lambda b: (b, 0)), full(wbd.shape), full(scale.shape), full(gain.shape)],
        out_shape=[jax.ShapeDtypeStruct((T, GROUP_WIDTH), BF16), jax.ShapeDtypeStruct(wbd.shape, F32),
                   jax.ShapeDtypeStruct(scale.shape, F32), jax.ShapeDtypeStruct(gain.shape, F32)],
        compiler_params=_cp("arbitrary"))(proj, wbd, scale, gain, dy)


def _swa_fn(q, kv_prev, kv_cur, sinks, tab, gain, first):
    half = GROUP_WIDTH // 2
    k2 = jnp.concatenate([kv_prev[:, :half], kv_cur[:, :half]], axis=0)
    v2 = jnp.concatenate([kv_prev[:, half:], kv_cur[:, half:]], axis=0)
    per_query_head = lambda a: jnp.concatenate(
        [a[:, (h // 2) * HEAD_DIM:(h // 2 + 1) * HEAD_DIM] for h in range(GROUP_HEADS)], axis=1)
    qs = jnp.concatenate([jnp.where(_head_mask(h, q.shape), q, 0.0) for h in range(GROUP_HEADS)], axis=0)
    qi = lax.broadcasted_iota(jnp.int32, (HEAD_ROWS, 2 * BLOCK), 0) & (BLOCK - 1)
    kj = lax.broadcasted_iota(jnp.int32, (HEAD_ROWS, 2 * BLOCK), 1)
    dist = qi + BLOCK - kj
    mask = (dist >= 0) & (dist < BLOCK) & ((kj >= BLOCK) | jnp.logical_not(first))
    logits = _bdot_nt(qs, per_query_head(k2)) * ATT_SCALE + tab.reshape(HEAD_ROWS, 2 * BLOCK)
    logits = jnp.where(mask, logits, -1e30)
    sink = jnp.concatenate([jnp.broadcast_to(sinks[:, h:h + 1], (BLOCK, 1)) for h in range(GROUP_HEADS)], axis=0)
    m = lax.stop_gradient(jnp.maximum(jnp.max(logits, axis=1, keepdims=True), sink))
    p = jnp.exp(logits - m)
    probs = p / (jnp.sum(p, axis=1, keepdims=True) + jnp.exp(sink - m))
    out = _bdot(probs, per_query_head(v2))
    y = jnp.zeros_like(q)
    for h in range(GROUP_HEADS):
        y = y + jnp.where(_head_mask(h, q.shape), out[h * BLOCK:(h + 1) * BLOCK], 0.0)
    return _rms(y, gain)


def _swa_specs(nblk):
    q = pl.BlockSpec((BLOCK, GROUP_WIDTH), lambda b, i: (b * nblk + i, 3))
    cur = pl.BlockSpec((BLOCK, GROUP_WIDTH), lambda b, i: (b * nblk + i, 4))
    prev = pl.BlockSpec((BLOCK, GROUP_WIDTH), lambda b, i: (b * nblk + jnp.maximum(i - 1, 0), 4))
    return q, prev, cur


def _swa_fwd(proj, sinks, tab, gain, seq, name, items=()):
    T = proj.shape[0]
    nblk = seq // BLOCK

    def body(q_ref, kp_ref, kc_ref, s_ref, t_ref, g_ref, o_ref):
        o_ref[...] = _swa_fn(q_ref[...], kp_ref[...], kc_ref[...], s_ref[...], t_ref[...], g_ref[...],
                             pl.program_id(1) == 0).astype(BF16)

    full = lambda shape: pl.BlockSpec(shape, lambda b, i: (0,) * len(shape))
    return _call(
        body, (proj, proj, proj, sinks, tab, gain), grid=(T // seq, nblk), name=name, items=items,
        in_specs=[*_swa_specs(nblk), full(sinks.shape), full(tab.shape), full(gain.shape)],
        out_specs=[pl.BlockSpec((BLOCK, GROUP_WIDTH), lambda b, i: (b * nblk + i, 0))],
        out_shape=[jax.ShapeDtypeStruct((T, GROUP_WIDTH), BF16)],
        sem=("parallel", "parallel"))


def _swa_bwd(proj, sinks, tab, gain, dy, seq, name, items=()):
    T = proj.shape[0]
    nblk = seq // BLOCK

    def body(q_ref, kp_ref, kc_ref, s_ref, t_ref, g_ref, dy_ref, dq_ref, dkp_ref, dkc_ref, ds_ref, dt_ref, dg_ref):
        first = pl.program_id(1) == 0
        fn = functools.partial(_swa_fn, first=first)
        _, vjp = jax.vjp(fn, q_ref[...], kp_ref[...], kc_ref[...], s_ref[...], t_ref[...], g_ref[...])
        dq, dkp, dkc, ds, dt, dg = vjp(dy_ref[...])
        dq_ref[...] = dq.astype(BF16)
        dkp_ref[...] = dkp
        dkc_ref[...] = dkc
        _acc_out((pl.program_id(0) == 0) & first, (ds_ref, dt_ref, dg_ref), (ds, dt, dg))

    full = lambda shape: pl.BlockSpec(shape, lambda b, i: (0,) * len(shape))
    blk = lambda c: pl.BlockSpec((BLOCK, GROUP_WIDTH), lambda b, i: (b * nblk + i, c))
    return _call(
        body, (proj, proj, proj, sinks, tab, gain, dy), grid=(T // seq, nblk), name=name, items=items,
        in_specs=[*_swa_specs(nblk), full(sinks.shape), full(tab.shape), full(gain.shape), blk(2)],
        out_specs=[blk(0), blk(0), blk(0), full(sinks.shape), full(tab.shape), full(gain.shape)],
        out_shape=[jax.ShapeDtypeStruct((T, GROUP_WIDTH), BF16), jax.ShapeDtypeStruct((T, GROUP_WIDTH), F32),
                   jax.ShapeDtypeStruct((T, GROUP_WIDTH), F32), jax.ShapeDtypeStruct(sinks.shape, F32),
                   jax.ShapeDtypeStruct(tab.shape, F32), jax.ShapeDtypeStruct(gain.shape, F32)],
        sem=("arbitrary", "arbitrary"))


def _shift_add(cur, prev, seq, name):
    T = cur.shape[0]
    nblk = seq // BLOCK

    def body(c_ref, p_ref, o_ref):
        last = pl.program_id(1) == nblk - 1
        o_ref[...] = (c_ref[...] + jnp.where(last, 0.0, p_ref[...])).astype(BF16)

    return pl.pallas_call(
        body, grid=(T // seq, nblk), name=name,
        in_specs=[pl.BlockSpec((BLOCK, GROUP_WIDTH), lambda b, i: (b * nblk + i, 0)),
                  pl.BlockSpec((BLOCK, GROUP_WIDTH), lambda b, i: (b * nblk + jnp.minimum(i + 1, nblk - 1), 0))],
        out_specs=pl.BlockSpec((BLOCK, GROUP_WIDTH), lambda b, i: (b * nblk + i, 0)),
        out_shape=jax.ShapeDtypeStruct((T, GROUP_WIDTH), BF16),
        compiler_params=_cp("parallel", "parallel"))(cur, prev)


def _t5_bucket(dist):
    max_exact = N_BUCKETS // 2
    df = jnp.maximum(dist, 1).astype(F32)
    large = max_exact + (jnp.log(df / max_exact) / jnp.log(jnp.float32(MAX_DISTANCE / max_exact))
                         * (N_BUCKETS - max_exact)).astype(jnp.int32)
    return jnp.where(dist < max_exact, dist, jnp.minimum(large, N_BUCKETS - 1))


def _bucket_map():
    dist = (jnp.arange(BLOCK)[:, None] + BLOCK) - jnp.arange(2 * BLOCK)[None, :]
    return _t5_bucket(jnp.clip(dist, 0, BLOCK - 1)).astype(jnp.int32)


def _bias_table(rel_bias, buckets, name):
    def body(rb_ref, bk_ref, o_ref):
        bk = bk_ref[...]
        rb = rb_ref[...]
        for h in range(GROUP_HEADS):
            acc = jnp.zeros((BLOCK, 2 * BLOCK), F32)
            for b in range(N_BUCKETS):
                acc = jnp.where(bk == b, rb[b:b + 1, h:h + 1], acc)
            o_ref[h] = acc

    return pl.pallas_call(body, name=name, out_shape=jax.ShapeDtypeStruct((GROUP_HEADS, BLOCK, 2 * BLOCK), F32),
                          compiler_params=_cp())(rel_bias, buckets)


def _bias_table_bwd(dtab, buckets, name):
    def body(dt_ref, bk_ref, o_ref):
        bk = bk_ref[...]
        row = lax.broadcasted_iota(jnp.int32, (N_BUCKETS, GROUP_HEADS), 0)
        col = lax.broadcasted_iota(jnp.int32, (N_BUCKETS, GROUP_HEADS), 1)
        acc = jnp.zeros((N_BUCKETS, GROUP_HEADS), F32)
        for h in range(GROUP_HEADS):
            dt = dt_ref[h]
            for b in range(N_BUCKETS):
                s = jnp.sum(jnp.where(bk == b, dt, 0.0), keepdims=True)
                acc = acc + jnp.where((row == b) & (col == h), s, 0.0)
        o_ref[...] = acc

    return pl.pallas_call(body, name=name, out_shape=jax.ShapeDtypeStruct((N_BUCKETS, GROUP_HEADS), F32),
                          compiler_params=_cp())(dtab, buckets)


HEAD_ROWS = GROUP_HEADS * BLOCK


def _stack_heads(x):
    return jnp.concatenate([jnp.where(_head_mask(h, x.shape), x, 0.0) for h in range(GROUP_HEADS)], axis=0).astype(BF16)


def _sb_tile(qs, kb, q0, k0):
    z = lax.dot_general(qs, kb, _NT, preferred_element_type=F32)
    row = lax.broadcasted_iota(jnp.int32, (HEAD_ROWS, BLOCK), 0) & (BLOCK - 1)
    col = lax.broadcasted_iota(jnp.int32, (HEAD_ROWS, BLOCK), 1)
    causal = (k0 + col) < (q0 + row)
    ls_neg = -(jnp.maximum(z, 0.0) + jnp.log(1.0 + jnp.exp(-jnp.abs(z))))
    return jnp.where(causal, ls_neg, 0.0), ls_neg + z, causal


SB_DEAD = -104.0
SB_FIRST_LANE = GROUP_HEADS


def _tri(strict_upper_src):
    r = lax.broadcasted_iota(jnp.int32, (BLOCK, BLOCK), 0)
    c = lax.broadcasted_iota(jnp.int32, (BLOCK, BLOCK), 1)
    cond = {"gt": r > c, "le": r <= c, "lt": r < c}[strict_upper_src]
    return jnp.where(cond, 1.0, 0.0).astype(BF16)


def _sb_fwd(proj, gain, seq, name, items=()):
    T = proj.shape[0]
    nblk = seq // BLOCK

    def body(q_ref, k_ref, v_ref, g_ref, o_ref, raw_ref, bt_ref):
        i = pl.program_id(1)
        q = q_ref[...]
        u_gt = _tri("gt")
        lane = lax.broadcasted_iota(jnp.int32, (BLOCK, BLOCK), 1)
        heads = [slice(h * HEAD_DIM, (h + 1) * HEAD_DIM) for h in range(GROUP_HEADS)]
        rows = [slice(h * BLOCK, (h + 1) * BLOCK) for h in range(GROUP_HEADS)]
        qs = _stack_heads(q * ATT_SCALE)

        def live(carry):
            j, _, cb = carry
            return (j >= 0) & (jnp.max(cb) > SB_DEAD)

        def step(carry):
            j, accs, cb = carry
            ks = pl.multiple_of(j * BLOCK, BLOCK)
            kb = k_ref[pl.ds(ks, BLOCK), :].astype(BF16)
            vb = v_ref[pl.ds(ks, BLOCK), :].astype(BF16)
            b, a, causal = _sb_tile(qs, kb, i * BLOCK, j * BLOCK)
            tail = _split_dot(b, u_gt) + cb
            w = jnp.where(causal, jnp.exp(a + tail), 0.0).astype(BF16)
            accs = tuple(accs[h] + jnp.dot(w[rows[h]], vb[:, hs], preferred_element_type=F32)
                         for h, hs in enumerate(heads))
            return j - 1, accs, cb + jnp.sum(b, axis=1, keepdims=True)

        zero_acc = tuple(jnp.zeros((BLOCK, HEAD_DIM), F32) for _ in heads)
        j_end, accs, cb = lax.while_loop(live, step, (i, zero_acc, jnp.zeros((HEAD_ROWS, 1), F32)))
        side = jnp.where(lane == SB_FIRST_LANE, (j_end + 1).astype(F32), 0.0)
        for h in range(GROUP_HEADS):
            side = jnp.where(lane == h, cb[rows[h]], side)
        raw = jnp.concatenate(accs, axis=1)
        raw_ref[...] = raw
        bt_ref[...] = side
        o_ref[...] = _rms(raw, g_ref[...]).astype(BF16)

    return _call(
        body, (proj, proj, proj, gain), grid=(T // seq, nblk), name=name, items=items,
        in_specs=[pl.BlockSpec((BLOCK, GROUP_WIDTH), lambda b, i: (b * nblk + i, 5)),
                  pl.BlockSpec((seq, GROUP_WIDTH), lambda b, i: (b, 6)),
                  pl.BlockSpec((seq, GROUP_WIDTH), lambda b, i: (b, 7)),
                  pl.BlockSpec(gain.shape, lambda b, i: (0, 0))],
        out_specs=[pl.BlockSpec((BLOCK, GROUP_WIDTH), lambda b, i: (b * nblk + i, 0)),
                   pl.BlockSpec((BLOCK, GROUP_WIDTH), lambda b, i: (b * nblk + i, 0)),
                   pl.BlockSpec((BLOCK, BLOCK), lambda b, i: (b * nblk + i, 0))],
        out_shape=[jax.ShapeDtypeStruct((T, GROUP_WIDTH), BF16), jax.ShapeDtypeStruct((T, GROUP_WIDTH), F32),
                   jax.ShapeDtypeStruct((T, BLOCK), F32)],
        sem=("parallel", "parallel"))


def _sb_bwd(proj, gain, raw, btot, dy, seq, name, items=()):
    T = proj.shape[0]
    nblk = seq // BLOCK

    def body(q_ref, k_ref, v_ref, g_ref, raw_ref, bt_ref, dy_ref, dq_ref, dk_ref, dv_ref, dg_ref):
        i = pl.program_id(1)

        @pl.when(i == 0)
        def _():
            dk_ref[...] = jnp.zeros_like(dk_ref)
            dv_ref[...] = jnp.zeros_like(dv_ref)

        rawv = raw_ref[...]
        _, vjp = jax.vjp(_rms, rawv, g_ref[...])
        do, dg = vjp(dy_ref[...])
        _acc_out((pl.program_id(0) == 0) & (i == 0), (dg_ref,), (dg,))
        q = q_ref[...]
        bt = bt_ref[...]
        u_le = _tri("le")
        u_lt = _tri("lt")
        heads = [slice(h * HEAD_DIM, (h + 1) * HEAD_DIM) for h in range(GROUP_HEADS)]
        rows = [slice(h * BLOCK, (h + 1) * BLOCK) for h in range(GROUP_HEADS)]
        qs = _stack_heads(q * ATT_SCALE)
        dos = _stack_heads(do)
        bts = jnp.concatenate([bt[:, h:h + 1] for h in range(GROUP_HEADS)], axis=0)
        first = jnp.max(bt[:, SB_FIRST_LANE:SB_FIRST_LANE + 1]).astype(jnp.int32)
        first = jnp.minimum(jnp.maximum(first, 0), i)

        def step(j, carry):
            dqs, cb, cg = carry
            ks = pl.multiple_of(j * BLOCK, BLOCK)
            kb = k_ref[pl.ds(ks, BLOCK), :].astype(BF16)
            vb = v_ref[pl.ds(ks, BLOCK), :].astype(BF16)
            b, a, causal = _sb_tile(qs, kb, i * BLOCK, j * BLOCK)
            tail = bts - (_split_dot(b, u_le) + cb)
            w = jnp.where(causal, jnp.exp(a + tail), 0.0)
            sig = jnp.exp(a)
            g = w * lax.dot_general(dos, vb, _NT, preferred_element_type=F32)
            gpre = _split_dot(g, u_lt) + cg
            dz = jnp.where(causal, g * (1.0 - sig) - gpre * sig, 0.0).astype(BF16)
            dqs = tuple(dqs[h] + jnp.dot(dz[rows[h]], kb[:, hs], preferred_element_type=F32)
                        for h, hs in enumerate(heads))
            dk_ref[pl.ds(ks, BLOCK), :] += lax.dot_general(dz, qs, _TN, preferred_element_type=F32)
            dv_ref[pl.ds(ks, BLOCK), :] += lax.dot_general(w.astype(BF16), dos, _TN, preferred_element_type=F32)
            return dqs, cb + jnp.sum(b, axis=1, keepdims=True), cg + jnp.sum(g, axis=1, keepdims=True)

        zero_dq = tuple(jnp.zeros((BLOCK, HEAD_DIM), F32) for _ in heads)
        zero = jnp.zeros((HEAD_ROWS, 1), F32)
        dqs, _, _ = lax.fori_loop(first, i + 1, step, (zero_dq, zero, zero))
        dq_ref[...] = (jnp.concatenate(dqs, axis=1) * ATT_SCALE).astype(BF16)

    blk = lambda c: pl.BlockSpec((BLOCK, GROUP_WIDTH), lambda b, i: (b * nblk + i, c))
    seqblk = lambda c: pl.BlockSpec((seq, GROUP_WIDTH), lambda b, i: (b, c))
    vec = pl.BlockSpec(gain.shape, lambda b, i: (0, 0))
    return _call(
        body, (proj, proj, proj, gain, raw, btot, dy), grid=(T // seq, nblk), name=name, items=items,
        in_specs=[blk(5), seqblk(6), seqblk(7), vec, blk(0), pl.BlockSpec((BLOCK, BLOCK), lambda b, i: (b * nblk + i, 0)),
                  blk(3)],
        out_specs=[blk(0), seqblk(0), seqblk(0), vec],
        out_shape=[jax.ShapeDtypeStruct((T, GROUP_WIDTH), BF16), jax.ShapeDtypeStruct((T, GROUP_WIDTH), F32),
                   jax.ShapeDtypeStruct((T, GROUP_WIDTH), F32), jax.ShapeDtypeStruct(gain.shape, F32)],
        sem=("arbitrary", "arbitrary"))


def _layer_params(l, sgu_w, sgu_b, pool_w, pool_scale, swa_sinks, mix_out_gain, norm_mix, norm_ffn):
    gains = mix_out_gain[l].reshape(4, 1, GROUP_WIDTH)
    return dict(
        wm=sgu_w[l], bt=sgu_b[l].T,
        wbd=jax.scipy.linalg.block_diag(*[pool_w[l, g] for g in range(len(POOL_WINDOWS))]),
        scale=pool_scale[l][None], sinks=swa_sinks[l][None],
        gain=[gains[m] for m in range(4)], norm_mix=norm_mix[l][None], norm_ffn=norm_ffn[l][None])


def _as_weights(g_in, g_out, g_gu, g_down):
    return (g_in[None], g_out.reshape(1, D_MODEL, D_MODEL), g_gu[None], g_down.reshape(1, D_FF_PAD, D_MODEL))


def _gather_item(src, l, dst_shape, rows=None, down=False, init=None):
    r0, nr = rows or (0, src.shape[1])
    if down:
        place = lambda dst, k: dst.at[k // 2, pl.ds((k % 2) * DOWN_ROWS + r0, nr), :]
    else:
        place = lambda dst, k: dst.at[k, pl.ds(r0, nr), :]
    return _Item(src, lambda s, k: s.at[l, pl.ds(r0, nr), :], dst_shape, place, init)


def _layer_fwd(l, x, p, w, tab, seq, shards):
    win, wout, wgu, wd = w
    nl = l + 1
    s_in, s_out, s_gu, s_down = shards or (None,) * 4
    ride = lambda item: (item(),) if shards else ()
    half_down = DOWN_ROWS // 2
    proj, h1, *g_down = _norm_mm(x, p["norm_mix"], win, 0, 4, F32, f"in_proj_{l}", items=ride(
        lambda: _gather_item(s_down, nl, (4, FF_PAD, D_MODEL), (0, half_down), True,
                             jnp.zeros((4, FF_PAD, D_MODEL), BF16))))
    ya, *g_out = _sgu_fwd(proj, p["wm"], p["bt"], p["gain"][0], f"sgu_fwd_{l}",
                          items=ride(lambda: _gather_item(s_out, nl, (N_DEV,) + s_out.shape[1:])))
    yb = _pool_fwd(proj, p["wbd"], p["scale"], p["gain"][1], seq, f"pool_fwd_{l}")
    yc, *g_down = _swa_fwd(proj, p["sinks"], tab, p["gain"][2], seq, f"swa_fwd_{l}", items=ride(
        lambda: _gather_item(s_down, nl, (4, FF_PAD, D_MODEL), (half_down, half_down), True, g_down[0])))
    half_gu = D_MODEL // 2
    yd, raw, btot, *g_gu = _sb_fwd(proj, p["gain"][3], seq, f"sb_fwd_{l}", items=ride(
        lambda: _gather_item(s_gu, nl, (N_DEV,) + s_gu.shape[1:], (0, half_gu))))
    ycat = jnp.concatenate([ya, yb, yc, yd], axis=1)
    xm, = _mm_res(x, ycat, wout, 0, f"out_proj_{l}", tn=D_MODEL)
    gu, act, h2, *g_gu = _ffn_up_act(xm, p["norm_ffn"], wgu, f"ffn_up_{l}", items=ride(
        lambda: _gather_item(s_gu, nl, (N_DEV,) + s_gu.shape[1:], (half_gu, half_gu), init=g_gu[0])))
    xo, *g_in = _mm_res(xm, act, wd, 0, f"ffn_down_{l}", tn=D_MODEL // 2,
                        items=ride(lambda: _gather_item(s_in, nl, (N_DEV,) + s_in.shape[1:])))
    w_next = _as_weights(g_in[0], g_out[0], g_gu[0], g_down[0]) if shards else None
    return xo, (x, proj, h1, ycat, raw, btot, xm, gu, h2, act), w_next


def _layer_bwd(l, dxo, saved, p, w, tab, seq):
    win, wout, wgu, wd = w
    x, proj, h1, ycat, raw, btot, xm, gu, h2, act = saved
    out_rows = D_MODEL // N_DEV
    half = D_MODEL // 2

    def rows_of(g, r0, init=None):
        cut = lambda a, k: a.at[k, pl.ds(r0, half), :]
        return _Item(g, cut, g.shape, cut, init)

    dgu = _ffn_down_dx(dxo, wd, gu, f"ffn_down_dx_{l}")
    g_wd = _mm_tn_rows(act, dxo, FF_PAD, f"ffn_down_dw_{l}")
    send_down = _Item(g_wd, lambda src, k: src.at[pl.ds((k // 2) * FF_PAD + (k % 2) * DOWN_ROWS, DOWN_ROWS), :],
                      (N_DEV, DOWN_ROWS, D_MODEL))
    g_wgu, p_down = _mm_tn_cols(h2, dgu, FF_PAD, 1, f"ffn_up_dw_{l}", items=(send_down,), block_of=_ffn_chunk_block)
    dxm, g_norm_ffn, p_gu = _mm_norm_bwd(dgu, wgu, 0, 1, xm, p["norm_ffn"], dxo, f"ffn_up_dx_{l}",
                                         items=(rows_of(g_wgu, 0),), block_of=_ffn_chunk_block)
    dycat = _mm_nt(dxm, wout, 0, F32, f"out_proj_dx_{l}")
    g_wout = _mm_tn_rows(ycat, dxm, D_MODEL, f"out_proj_dw_{l}")
    send_out = _Item(g_wout, lambda src, k: src.at[pl.ds(k * out_rows, out_rows), :], (N_DEV, out_rows, D_MODEL))
    duv, g_wm, g_bt, g_ga, p_out = _sgu_bwd(proj, p["wm"], p["bt"], p["gain"][0], dycat, f"sgu_bwd_{l}",
                                            items=(send_out,))
    dp, g_wbd, g_scale, g_gb = _pool_bwd(proj, p["wbd"], p["scale"], p["gain"][1], dycat, seq, f"pool_bwd_{l}")
    dq, dkp, dkc, g_sinks, g_tab, g_gc = _swa_bwd(proj, p["sinks"], tab, p["gain"][2], dycat, seq, f"swa_bwd_{l}")
    dkv = _shift_add(dkc, dkp, seq, f"swa_dkv_{l}")
    dqd, dkd, dvd, g_gd, p_gu = _sb_bwd(proj, p["gain"][3], raw, btot, dycat, seq, f"sb_bwd_{l}",
                                        items=(rows_of(g_wgu, half, p_gu),))
    dproj = jnp.concatenate([duv, dp, dq, dkv, dqd, dkd.astype(BF16), dvd.astype(BF16)], axis=1)
    g_win, = _mm_tn_cols(h1, dproj, GROUP_WIDTH, 4, f"in_proj_dw_{l}")
    dx, g_norm_mix, p_in = _mm_norm_bwd(dproj, win, 0, 4, x, p["norm_mix"], dxm, f"in_proj_dx_{l}",
                                        items=(_Item(g_win, lambda src, k: src.at[k], g_win.shape),))
    ng = len(POOL_WINDOWS)
    gd = GROUP_WIDTH // ng
    small = dict(
        sgu_w=g_wm, sgu_b=g_bt.T,
        pool_w=jnp.stack([g_wbd[g * gd:(g + 1) * gd, g * gd:(g + 1) * gd] for g in range(ng)]),
        pool_scale=g_scale[0], swa_sinks=g_sinks[0],
        mix_out_gain=jnp.concatenate([g_ga[0], g_gb[0], g_gc[0], g_gd[0]]),
        norm_mix=g_norm_mix[0], norm_ffn=g_norm_ffn[0])
    return dx, (p_in, p_out, p_gu, p_down), small, g_tab


def _local_step(x, target, w0, shards, sgu_w, sgu_b, pool_w, pool_scale, swa_sinks, rel_bias, mix_out_gain,
                norm_mix, norm_ffn, norm_final, seq):
    buckets = _bucket_map()
    tab = _bias_table(rel_bias, buckets, "bias_table")
    params = [_layer_params(l, sgu_w, sgu_b, pool_w, pool_scale, swa_sinks, mix_out_gain, norm_mix, norm_ffn)
              for l in range(DEPTH)]
    saved, weights = [], [w0]
    for l in range(DEPTH):
        x, s, w_next = _layer_fwd(l, x, params[l], weights[l], tab, seq, shards if l + 1 < DEPTH else None)
        saved.append(s)
        weights.append(w_next)
    loss, dx, g_final = _loss_head(x, norm_final[None], target, "loss_head")
    big, small, g_tab = [None] * DEPTH, [None] * DEPTH, None
    for l in reversed(range(DEPTH)):
        dx, big[l], small[l], t = _layer_bwd(l, dx, saved[l], params[l], weights[l], tab, seq)
        g_tab = t if g_tab is None else g_tab + t
    g_small = {k: jnp.stack([small[l][k] for l in range(DEPTH)]) for k in small[0]}
    g_small["rel_bias"] = _bias_table_bwd(g_tab, buckets, "bias_table_bwd")
    g_small["norm_final"] = g_final[0]
    return loss, dx, big, g_small


def _cast_pad(w, cols, name):
    L, r, c = w.shape

    def body(w_ref, o_ref):
        if cols != c:
            o_ref[...] = jnp.zeros_like(o_ref)
        o_ref[:, :c] = w_ref[...].astype(BF16)

    return pl.pallas_call(
        body, grid=(L,), name=name,
        in_specs=[pl.BlockSpec((None, r, c), lambda l: (l, 0, 0))],
        out_specs=pl.BlockSpec((None, r, cols), lambda l: (l, 0, 0)),
        out_shape=jax.ShapeDtypeStruct((L, r, cols), BF16),
        compiler_params=_cp("parallel"))(w)


def _down_rows(ref, k):
    return ref.at[:, k // 2, pl.ds((k % 2) * DOWN_ROWS, DOWN_ROWS), :]


def _all_gather_weights(s_in, s_out, s_gu, s_down):
    L = s_in.shape[0]
    shards = (s_in, s_out, s_gu, s_down)
    down_full = jnp.zeros((L, 4, FF_PAD, D_MODEL), BF16)
    n = len(shards)

    def body(i0, i1, i2, i3, _, o0, o1, o2, o3, send_sems, recv_sems, local_sems):
        srcs = (i0, i1, i2, i3)
        outs = (o0, o1, o2, o3)
        x, y, c, _me = _my_place()
        chips = [(1 - x, y), (x, 1 - y), (1 - x, 1 - y)]

        def place(t, dev):
            k = 4 * dev[0] + 2 * dev[1] + dev[2]
            return _down_rows(outs[t], k) if t == 3 else outs[t].at[:, k]

        def copy(slot, t, block, to, from_shard):
            return pltpu.make_async_remote_copy(
                src_ref=srcs[t] if from_shard else place(t, block), dst_ref=place(t, block),
                send_sem=send_sems.at[slot, t], recv_sem=recv_sems.at[slot, t], device_id=to, device_id_type=MESH)

        me, sibling = (x, y, c), (x, y, 1 - c)
        mine = [pltpu.make_async_copy(srcs[t], place(t, me), local_sems.at[t]) for t in range(n)]
        for cp in mine:
            cp.start()
        first = [copy(0, t, me, sibling, True) for t in range(n)]
        first += [copy(1 + j, t, me, (*chip, c), True) for j, chip in enumerate(chips) for t in range(n)]
        for cp in first:
            cp.start()
        passed = []
        for j, chip in enumerate(chips):
            for t in range(n):
                copy(1 + j, t, (*chip, c), me, False).wait_recv()
                cp = copy(4 + j, t, (*chip, c), sibling, False)
                cp.start()
                passed.append(cp)
        for t in range(n):
            copy(0, t, sibling, me, False).wait_recv()
        for j, chip in enumerate(chips):
            for t in range(n):
                copy(4 + j, t, (*chip, 1 - c), me, False).wait_recv()
        for cp in first + passed:
            cp.wait_send()
        for cp in mine:
            cp.wait()

    shapes = [jax.ShapeDtypeStruct((L, N_DEV) + s.shape[1:], BF16) for s in shards[:3]]
    shapes.append(jax.ShapeDtypeStruct(down_full.shape, BF16))
    return pl.pallas_call(
        body, name="all_gather_weights", out_shape=shapes,
        in_specs=[ANY] * 5, out_specs=[ANY] * 4, input_output_aliases={4: 3},
        scratch_shapes=[pltpu.SemaphoreType.DMA((7, n)), pltpu.SemaphoreType.DMA((7, n)), pltpu.SemaphoreType.DMA((n,))],
        )(*shards, down_full)


def _small_piece(part, r0, nr, init=None):
    cut = lambda a, k: a.at[pl.ds(r0, nr), :]
    return _Item(part, cut, (N_DEV,) + part.shape, lambda dst, k: dst.at[k, pl.ds(r0, nr), :], init)


def _adamw(w, g, m, v):
    m = ADAM_B1 * m + (1.0 - ADAM_B1) * g
    v = ADAM_B2 * v + (1.0 - ADAM_B2) * jnp.square(g)
    m_hat = m / (1.0 - ADAM_B1 ** ADAM_STEP)
    v_hat = v / (1.0 - ADAM_B2 ** ADAM_STEP)
    delta = -ADAM_LR * (m_hat / (jnp.sqrt(v_hat) + ADAM_EPS) + ADAM_WD * w)
    return delta, m, v


def _adamw_sharded(parts, w, m, v, tr, name, items=()):
    L, r, c = w.shape
    cp = parts[0].shape[-1]
    nrow = r // tr

    def body(*refs):
        p_refs, (w_ref, m_ref, v_ref, g_ref, d_ref, nm_ref, nv_ref) = refs[:L], refs[L:]
        for k in range(L):
            @pl.when(pl.program_id(0) == k)
            def _(p_ref=p_refs[k]):
                g = p_ref[0, :, :c].astype(F32)
                for dev in range(1, N_DEV):
                    g = g + p_ref[dev, :, :c].astype(F32)
                delta, nm, nv = _adamw(w_ref[...], g, m_ref[...], v_ref[...])
                g_ref[...] = g
                d_ref[...] = delta
                nm_ref[...] = nm
                nv_ref[...] = nv

    def part_spec(k):
        return pl.BlockSpec((N_DEV, tr, cp),
                            lambda l, i: (0, jnp.where(l == k, i, jnp.where(l < k, 0, nrow - 1)), 0))

    blk = pl.BlockSpec((None, tr, c), lambda l, i: (l, i, 0))
    out = jax.ShapeDtypeStruct((L, r, c), F32)
    return _call(
        body, (*parts, w, m, v), grid=(L, nrow), name=name, items=items,
        in_specs=[part_spec(k) for k in range(L)] + [blk, blk, blk],
        out_specs=[blk] * 4, out_shape=[out] * 4, sem=("arbitrary", "arbitrary"))


def _adamw_small(parts, w, m, v, name):
    def body(p_ref, w_ref, m_ref, v_ref, g_ref, d_ref, nm_ref, nv_ref):
        g = p_ref[0]
        for k in range(1, N_DEV):
            g = g + p_ref[k]
        delta, nm, nv = _adamw(w_ref[...], g, m_ref[...], v_ref[...])
        g_ref[...] = g
        d_ref[...] = delta
        nm_ref[...] = nm
        nv_ref[...] = nv

    out = jax.ShapeDtypeStruct(w.shape, F32)
    return pl.pallas_call(body, name=name, out_shape=[out] * 4, compiler_params=_cp())(parts, w, m, v)


SMALL = ("sgu_w", "sgu_b", "pool_w", "pool_scale", "swa_sinks", "rel_bias", "mix_out_gain", "norm_mix", "norm_ffn",
         "norm_final")


def _seg_rows(a):
    return -(-a.size // 128)


def _pack(parts, rows):
    segs = [jnp.pad(p.reshape(-1), (0, _seg_rows(p) * 128 - p.size)).reshape(_seg_rows(p), 128) for p in parts]
    used = sum(s.shape[0] for s in segs)
    return jnp.concatenate(segs + [jnp.zeros((rows - used, 128), F32)], axis=0)


def _unpack(buf, like):
    out, at = [], 0
    for a in like:
        out.append(buf[at:at + _seg_rows(a)].reshape(-1)[:a.size].reshape(a.shape))
        at += _seg_rows(a)
    return out


def kernel(x, w_in, w_out, sgu_w, sgu_b, pool_w, pool_scale, swa_sinks, rel_bias, mix_out_gain, norm_mix, norm_ffn, w_gate_up, w_down, norm_final, loss_target, m_w_in, m_w_out, m_sgu_w, m_sgu_b, m_pool_w, m_pool_scale, m_swa_sinks, m_rel_bias, m_mix_out_gain, m_norm_mix, m_norm_ffn, m_w_gate_up, m_w_down, m_norm_final, v_w_in, v_w_out, v_sgu_w, v_sgu_b, v_pool_w, v_pool_scale, v_swa_sinks, v_rel_bias, v_mix_out_gain, v_norm_mix, v_norm_ffn, v_w_gate_up, v_w_down, v_norm_final):
    bl, seq, _ = x.shape
    L = w_in.shape[0]
    shards = (_cast_pad(w_in, GROUP_WIDTH, "shard_w_in"), _cast_pad(w_out, D_MODEL, "shard_w_out"),
              _cast_pad(w_gate_up, FF_PAD, "shard_w_gate_up"), _cast_pad(w_down, D_MODEL, "shard_w_down"))
    first = _all_gather_weights(*[s[:1] for s in shards])
    w0 = _as_weights(*[f[0] for f in first])
    small_w = dict(sgu_w=sgu_w, sgu_b=sgu_b, pool_w=pool_w, pool_scale=pool_scale, swa_sinks=swa_sinks,
                   rel_bias=rel_bias, mix_out_gain=mix_out_gain, norm_mix=norm_mix, norm_ffn=norm_ffn,
                   norm_final=norm_final)
    small_m = dict(sgu_w=m_sgu_w, sgu_b=m_sgu_b, pool_w=m_pool_w, pool_scale=m_pool_scale, swa_sinks=m_swa_sinks,
                   rel_bias=m_rel_bias, mix_out_gain=m_mix_out_gain, norm_mix=m_norm_mix, norm_ffn=m_norm_ffn,
                   norm_final=m_norm_final)
    small_v = dict(sgu_w=v_sgu_w, sgu_b=v_sgu_b, pool_w=v_pool_w, pool_scale=v_pool_scale, swa_sinks=v_swa_sinks,
                   rel_bias=v_rel_bias, mix_out_gain=v_mix_out_gain, norm_mix=v_norm_mix, norm_ffn=v_norm_ffn,
                   norm_final=v_norm_final)
    loss, dx, big, g_small = _local_step(
        x.reshape(bl * seq, D_MODEL), loss_target.reshape(bl * seq, D_MODEL), w0, shards, sgu_w, sgu_b, pool_w,
        pool_scale, swa_sinks, rel_bias, mix_out_gain, norm_mix, norm_ffn, norm_final, seq)
    p_in, p_out, p_gu, p_down = ([big[l][t] for l in range(L)] for t in range(4))
    rows = -(-(sum(_seg_rows(small_w[k]) for k in SMALL) + 1) // 8) * 8
    packed = _pack([g_small[k] for k in SMALL] + [loss[0]], rows)
    cut_a = rows // 16 * 8
    cut_b = cut_a + (rows - cut_a) // 16 * 8
    *outs_gu, everyone = _adamw_sharded(p_gu, w_gate_up, m_w_gate_up, v_w_gate_up, 256, "adamw_w_gate_up",
                                        items=(_small_piece(packed, 0, cut_a),))
    *outs_down, everyone = _adamw_sharded(p_down, w_down, m_w_down, v_w_down, DOWN_ROWS // 2, "adamw_w_down",
                                          items=(_small_piece(packed, cut_a, cut_b - cut_a, everyone),))
    *outs_in, everyone = _adamw_sharded(p_in, w_in, m_w_in, v_w_in, 256, "adamw_w_in",
                                        items=(_small_piece(packed, cut_b, rows - cut_b, everyone),))
    outs_out = _adamw_sharded(p_out, w_out, m_w_out, v_w_out, D_MODEL // N_DEV, "adamw_w_out")
    like = [small_w[k] for k in SMALL]
    total, d_s, m_s, v_s = _adamw_small(everyone, _pack(like, rows), _pack([small_m[k] for k in SMALL], rows),
                                        _pack([small_v[k] for k in SMALL], rows), "adamw_small")
    g_list = dict(zip(SMALL, _unpack(total, like)))
    d_list = dict(zip(SMALL, _unpack(d_s, like)))
    m_list = dict(zip(SMALL, _unpack(m_s, like)))
    v_list = dict(zip(SMALL, _unpack(v_s, like)))
    loss_total = total[sum(_seg_rows(a) for a in like), 0]
    big_outs = dict(w_in=outs_in, w_out=outs_out, w_gate_up=outs_gu, w_down=outs_down)
    order = ("w_in", "w_out", "sgu_w", "sgu_b", "pool_w", "pool_scale", "swa_sinks", "rel_bias", "mix_out_gain",
             "norm_mix", "norm_ffn", "w_gate_up", "w_down", "norm_final")
    result = [loss_total, dx.reshape(bl, seq, D_MODEL)]
    for which, small in enumerate((g_list, d_list, m_list, v_list)):
        for name in order:
            result.append(big_outs[name][which] if name in big_outs else small[name])
    return tuple(result)
```

```python
import functools

import jax
import jax.numpy as jnp
from jax import lax
from jax.experimental import pallas as pl
from jax.experimental.pallas import tpu as pltpu

F32 = jnp.float32
BF16 = jnp.bfloat16

N_DEV = 8
DEPTH = 4
D_MODEL = 1024
GROUP_WIDTH = 256
HEAD_DIM = 64
GROUP_HEADS = 4
BLOCK = 128
N_BUCKETS = 32
MAX_DISTANCE = 128
POOL_WINDOWS = (2, 4, 8, 16)
D_FF = 2816
FF_SHARD = D_FF // 4
FF_PAD = 768
D_FF_PAD = 4 * FF_PAD
EPS = 1e-6
ATT_SCALE = HEAD_DIM ** -0.5
ADAM_LR = 0.001
ADAM_B1 = 0.9
ADAM_B2 = 0.999
ADAM_EPS = 1e-08
ADAM_WD = 0.01
ADAM_STEP = 10
VMEM_LIMIT = 56 * 1024 * 1024
MESH_AXES = ("x", "y", "c")


def _cp(*sem):
    return pltpu.CompilerParams(dimension_semantics=sem or None, vmem_limit_bytes=VMEM_LIMIT)


_NT = (((1,), (1,)), ((), ()))
_TN = (((0,), (0,)), ((), ()))


@jax.custom_vjp
def _bdot(a, b):
    return jnp.dot(a.astype(BF16), b.astype(BF16), preferred_element_type=F32)


def _bdot_fwd(a, b):
    return _bdot(a, b), (a.astype(BF16), b.astype(BF16))


def _bdot_bwd(res, ct):
    a, b = res
    c = ct.astype(BF16)
    return (lax.dot_general(c, b, _NT, preferred_element_type=F32),
            lax.dot_general(a, c, _TN, preferred_element_type=F32))


_bdot.defvjp(_bdot_fwd, _bdot_bwd)


@jax.custom_vjp
def _bdot_nt(a, b):
    return lax.dot_general(a.astype(BF16), b.astype(BF16), _NT, preferred_element_type=F32)


def _bdot_nt_fwd(a, b):
    return _bdot_nt(a, b), (a.astype(BF16), b.astype(BF16))


def _bdot_nt_bwd(res, ct):
    a, b = res
    c = ct.astype(BF16)
    return (jnp.dot(c, b, preferred_element_type=F32),
            lax.dot_general(c, a, _TN, preferred_element_type=F32))


_bdot_nt.defvjp(_bdot_nt_fwd, _bdot_nt_bwd)


def _rms(x, g):
    return x * lax.rsqrt(jnp.mean(x * x, axis=-1, keepdims=True) + EPS) * g


def _sigmoid(x):
    return 0.5 * jnp.tanh(0.5 * x) + 0.5


def _split_dot(x, u):
    hi = x.astype(BF16)
    lo = (x - hi.astype(F32)).astype(BF16)
    return jnp.dot(hi, u, preferred_element_type=F32) + jnp.dot(lo, u, preferred_element_type=F32)


def _head_mask(h, shape):
    col = lax.broadcasted_iota(jnp.int32, shape, 1)
    return (col >= h * HEAD_DIM) & (col < (h + 1) * HEAD_DIM)


ANY = pl.BlockSpec(memory_space=pl.ANY)
MESH = pl.DeviceIdType.MESH
DOWN_ROWS = D_FF // N_DEV


def _my_place():
    x, y, c = (lax.axis_index(a) for a in MESH_AXES)
    return x, y, c, 4 * x + 2 * y + c


def _peer(x, y, c, d):
    return (x ^ (d >> 2), y ^ ((d >> 1) & 1), c ^ (d & 1))


class _Item:
    def __init__(self, src, block, dst_shape, place=None, init=None):
        self.src, self.block, self.dst_shape, self.init = src, block, dst_shape, init
        self.place = place or (lambda dst, k: dst.at[k])


def _call(body, args, *, grid, in_specs, out_specs, out_shape, sem, name, scratch_shapes=(), items=()):
    if not items:
        outs = pl.pallas_call(body, grid=grid, in_specs=in_specs, out_specs=out_specs, out_shape=out_shape, name=name,
                              scratch_shapes=list(scratch_shapes), compiler_params=_cp(*sem))(*args)
        return list(outs) if isinstance(outs, (list, tuple)) else [outs]
    n_in, n_out, n_scr, n = len(in_specs), len(out_specs), len(scratch_shapes), len(items)
    inits = [i for i, it in enumerate(items) if it.init is not None]

    def wrapped(*refs):
        core_in, srcs = refs[:n_in], refs[n_in:n_in + n]
        off = n_in + n + len(inits)
        core_out, dsts = refs[off:off + n_out], refs[off + n_out:off + n_out + n]
        scratch = refs[off + n_out + n:]
        send_sems, recv_sems, local_sems = scratch[n_scr:]
        ids = [pl.program_id(a) for a in range(len(grid))]
        first = functools.reduce(jnp.logical_and, [i == 0 for i in ids])
        last = functools.reduce(jnp.logical_and, [i == g - 1 for i, g in zip(ids, grid)])
        x, y, c, me = _my_place()

        def local(i):
            return pltpu.make_async_copy(items[i].block(srcs[i], me), items[i].place(dsts[i], me), local_sems.at[i])

        def remote(d, i, sending):
            px, py, pc = _peer(x, y, c, d)
            pk = 4 * px + 2 * py + pc
            return pltpu.make_async_remote_copy(
                src_ref=items[i].block(srcs[i], pk), dst_ref=items[i].place(dsts[i], me if sending else pk),
                send_sem=send_sems.at[d - 1, i], recv_sem=recv_sems.at[d - 1, i],
                device_id=(px, py, pc), device_id_type=MESH)

        @pl.when(first)
        def _():
            for i in range(n):
                local(i).start()
            for d in range(1, N_DEV):
                for i in range(n):
                    remote(d, i, True).start()

        body(*core_in, *core_out, *scratch[:n_scr])

        @pl.when(last)
        def _():
            for d in range(1, N_DEV):
                for i in range(n):
                    remote(d, i, False).wait_recv()
            for d in range(1, N_DEV):
                for i in range(n):
                    remote(d, i, True).wait_send()
            for i in range(n):
                local(i).wait()

    outs = pl.pallas_call(
        wrapped, grid=grid, name=name,
        in_specs=list(in_specs) + [ANY] * (n + len(inits)), out_specs=list(out_specs) + [ANY] * n,
        out_shape=list(out_shape) + [jax.ShapeDtypeStruct(it.dst_shape, it.src.dtype) for it in items],
        input_output_aliases={n_in + n + j: n_out + i for j, i in enumerate(inits)},
        scratch_shapes=list(scratch_shapes) + [pltpu.SemaphoreType.DMA((N_DEV - 1, n)),
                                               pltpu.SemaphoreType.DMA((N_DEV - 1, n)), pltpu.SemaphoreType.DMA((n,))],
        compiler_params=_cp(*(["arbitrary"] * len(grid))),
    )(*args, *[it.src for it in items], *[items[i].init for i in inits])
    return list(outs)


def _norm_mm(x, g, w, l, jb, out_dtype, name, tm=1024, items=()):
    T, K = x.shape
    _, nb, _, tn = w.shape
    tm = min(tm, T)

    def body(x_ref, g_ref, w_ref, o_ref, h_ref):
        @pl.when(pl.program_id(1) == 0)
        def _():
            h_ref[...] = _rms(x_ref[...], g_ref[...]).astype(BF16)
        h = h_ref[...]
        for jj in range(jb):
            o_ref[:, jj * tn:(jj + 1) * tn] = jnp.dot(h, w_ref[jj], preferred_element_type=F32).astype(o_ref.dtype)

    return _call(
        body, (x, g, w), grid=(T // tm, nb // jb), name=name, items=items,
        in_specs=[pl.BlockSpec((tm, K), lambda i, j: (i, 0)), pl.BlockSpec((1, K), lambda i, j: (0, 0)),
                  pl.BlockSpec((None, jb, K, tn), lambda i, j: (l, j, 0, 0))],
        out_specs=[pl.BlockSpec((tm, jb * tn), lambda i, j: (i, j)), pl.BlockSpec((tm, K), lambda i, j: (i, 0))],
        out_shape=[jax.ShapeDtypeStruct((T, nb * tn), out_dtype), jax.ShapeDtypeStruct((T, K), BF16)],
        sem=("parallel", "arbitrary"))


def _mm_res(res, a, w, l, name, tn, tm=1024, items=()):
    T, K = a.shape
    N = w.shape[2]
    tm = min(tm, T)

    def body(r_ref, a_ref, w_ref, o_ref):
        o_ref[...] = r_ref[...] + jnp.dot(a_ref[...], w_ref[...], preferred_element_type=F32)

    return _call(
        body, (res, a, w), grid=(T // tm, N // tn), name=name, items=items,
        in_specs=[pl.BlockSpec((tm, tn), lambda i, j: (i, j)), pl.BlockSpec((tm, K), lambda i, j: (i, 0)),
                  pl.BlockSpec((None, K, tn), lambda i, j: (l, 0, j))],
        out_specs=[pl.BlockSpec((tm, tn), lambda i, j: (i, j))],
        out_shape=[jax.ShapeDtypeStruct((T, N), F32)],
        sem=("parallel", "parallel"))


def _mm_nt(a, w, l, out_dtype, name, tm=1024, tn=1024):
    T, K = a.shape
    N = w.shape[1]
    tm = min(tm, T)

    def body(a_ref, w_ref, o_ref):
        o_ref[...] = lax.dot_general(a_ref[...].astype(BF16), w_ref[...], _NT,
                                     preferred_element_type=F32).astype(o_ref.dtype)

    return pl.pallas_call(
        body, grid=(T // tm, N // tn), name=name,
        in_specs=[pl.BlockSpec((tm, K), lambda i, j: (i, 0)), pl.BlockSpec((None, tn, K), lambda i, j: (l, j, 0))],
        out_specs=pl.BlockSpec((tm, tn), lambda i, j: (i, j)),
        out_shape=jax.ShapeDtypeStruct((T, N), out_dtype),
        compiler_params=_cp("parallel", "parallel"))(a, w)


def _mm_norm_bwd(a, w, l, jb, x, g, dres, name, tm=1024, items=(), block_of=lambda j: j, transposed=False):
    T = a.shape[0]
    _, nb, K, tn = w.shape
    if transposed:
        tn, K = K, tn
    tm = min(tm, T)
    nj = nb // jb
    sub = min(256, tm)
    dims = (((1,), (0,)), ((), ())) if transposed else _NT

    def body(a_ref, w_ref, x_ref, g_ref, r_ref, o_ref, dg_ref):
        part = lax.dot_general(a_ref[:, :tn], w_ref[0], dims, preferred_element_type=F32)
        for jj in range(1, jb):
            part += lax.dot_general(a_ref[:, jj * tn:(jj + 1) * tn], w_ref[jj], dims, preferred_element_type=F32)

        @pl.when(pl.program_id(1) == 0)
        def _():
            o_ref[...] = part

        @pl.when(pl.program_id(1) > 0)
        def _():
            o_ref[...] += part

        @pl.when(pl.program_id(1) == nj - 1)
        def _():
            dg = jnp.zeros((1, K), F32)
            for r in range(tm // sub):
                rows = pl.ds(r * sub, sub)
                _, vjp = jax.vjp(_rms, x_ref[rows, :], g_ref[...])
                dx, dg_r = vjp(o_ref[rows, :])
                o_ref[rows, :] = r_ref[rows, :] + dx
                dg = dg + dg_r
            _acc_out(pl.program_id(0) == 0, (dg_ref,), (dg,))

    row = pl.BlockSpec((tm, K), lambda i, j: (i, 0))
    vec = pl.BlockSpec((1, K), lambda i, j: (0, 0))
    return _call(
        body, (a, w, x, g, dres), grid=(T // tm, nj), name=name, items=items,
        in_specs=[pl.BlockSpec((tm, jb * tn), lambda i, j: (i, j)),
                  pl.BlockSpec((None, jb) + w.shape[2:], lambda i, j: (l, block_of(j), 0, 0)), row, vec, row],
        out_specs=[row, vec],
        out_shape=[jax.ShapeDtypeStruct((T, K), F32), jax.ShapeDtypeStruct((1, K), F32)],
        sem=("arbitrary", "arbitrary"))


def _mm_tn_cols(lhs, rhs, tn, jb, name, tm=1024, items=(), block_of=lambda j: j):
    T, K = lhs.shape
    nb = rhs.shape[1] // tn
    tm = min(tm, T)
    nt = T // tm

    def body(l_ref, r_ref, o_ref, acc):
        part = lax.dot_general(l_ref[...], r_ref[...], _TN, preferred_element_type=F32)

        @pl.when(pl.program_id(1) == 0)
        def _():
            acc[...] = part

        @pl.when(pl.program_id(1) > 0)
        def _():
            acc[...] += part

        @pl.when(pl.program_id(1) == nt - 1)
        def _():
            for jj in range(jb):
                o_ref[jj] = acc[:, jj * tn:(jj + 1) * tn].astype(BF16)

    return _call(
        body, (lhs, rhs), grid=(nb // jb, nt), name=name, items=items,
        in_specs=[pl.BlockSpec((tm, K), lambda j, t: (t, 0)), pl.BlockSpec((tm, jb * tn), lambda j, t: (t, j))],
        out_specs=[pl.BlockSpec((jb, K, tn), lambda j, t: (block_of(j), 0, 0))],
        out_shape=[jax.ShapeDtypeStruct((nb, K, tn), BF16)],
        scratch_shapes=[pltpu.VMEM((K, jb * tn), F32)],
        sem=("parallel", "arbitrary"))


def _mm_tn_rows(lhs, rhs, tk, name, tm=1024, items=(), block_of=lambda j: j):
    T, Kl = lhs.shape
    N = rhs.shape[1]
    tm = min(tm, T)
    nt = T // tm

    def body(l_ref, r_ref, o_ref, acc):
        part = lax.dot_general(l_ref[...], r_ref[...].astype(BF16), _TN, preferred_element_type=F32)

        @pl.when(pl.program_id(1) == 0)
        def _():
            acc[...] = part

        @pl.when(pl.program_id(1) > 0)
        def _():
            acc[...] += part

        @pl.when(pl.program_id(1) == nt - 1)
        def _():
            o_ref[...] = acc[...].astype(BF16)

    return _call(
        body, (lhs, rhs), grid=(Kl // tk, nt), name=name, items=items,
        in_specs=[pl.BlockSpec((tm, tk), lambda l, t: (t, l)), pl.BlockSpec((tm, N), lambda l, t: (t, 0))],
        out_specs=[pl.BlockSpec((tk, N), lambda l, t: (block_of(l), 0))],
        out_shape=[jax.ShapeDtypeStruct((Kl, N), BF16)],
        scratch_shapes=[pltpu.VMEM((tk, N), F32)],
        sem=("parallel", "arbitrary"))


N_FF_CHUNK = D_FF_PAD // FF_PAD


def _ffn_chunk_block(j):
    return (j % 2) * N_FF_CHUNK + j // 2


def _ffn_up_act(x, g, w, name, tm=1024, items=()):
    T, K = x.shape
    tm = min(tm, T)

    def body(x_ref, g_ref, wg_ref, wu_ref, gu_ref, act_ref, h_ref):
        @pl.when(pl.program_id(1) == 0)
        def _():
            h_ref[...] = _rms(x_ref[...], g_ref[...]).astype(BF16)
        h = h_ref[...]
        gate = lax.dot_general(h, wg_ref[...], _NT, preferred_element_type=F32)
        up = lax.dot_general(h, wu_ref[...], _NT, preferred_element_type=F32)
        gu_ref[:, :FF_PAD] = gate.astype(BF16)
        gu_ref[:, FF_PAD:] = up.astype(BF16)
        act_ref[...] = (gate * _sigmoid(gate) * up).astype(BF16)

    return _call(
        body, (x, g, w, w), grid=(T // tm, N_FF_CHUNK), name=name, items=items,
        in_specs=[pl.BlockSpec((tm, K), lambda i, j: (i, 0)), pl.BlockSpec((1, K), lambda i, j: (0, 0)),
                  pl.BlockSpec((None, None, FF_PAD, K), lambda i, j: (0, j, 0, 0)),
                  pl.BlockSpec((None, None, FF_PAD, K), lambda i, j: (0, j + N_FF_CHUNK, 0, 0))],
        out_specs=[pl.BlockSpec((tm, 2 * FF_PAD), lambda i, j: (i, j)), pl.BlockSpec((tm, FF_PAD), lambda i, j: (i, j)),
                   pl.BlockSpec((tm, K), lambda i, j: (i, 0))],
        out_shape=[jax.ShapeDtypeStruct((T, 2 * D_FF_PAD), BF16), jax.ShapeDtypeStruct((T, D_FF_PAD), BF16),
                   jax.ShapeDtypeStruct((T, K), BF16)],
        sem=("parallel", "arbitrary"))


def _ffn_down_dx(dxo, w, gu, name, tm=1024):
    T, K = dxo.shape
    tm = min(tm, T)

    def body(d_ref, w_ref, gu_ref, o_ref):
        d = lax.dot_general(d_ref[...].astype(BF16), w_ref[...], _NT, preferred_element_type=F32)
        gate = gu_ref[:, :FF_PAD].astype(F32)
        up = gu_ref[:, FF_PAD:].astype(F32)
        sig = _sigmoid(gate)
        silu = gate * sig
        o_ref[:, :FF_PAD] = (d * up * (sig + silu * (1.0 - sig))).astype(BF16)
        o_ref[:, FF_PAD:] = (d * silu).astype(BF16)

    return pl.pallas_call(
        body, grid=(T // tm, N_FF_CHUNK), name=name,
        in_specs=[pl.BlockSpec((tm, K), lambda i, j: (i, 0)), pl.BlockSpec((None, FF_PAD, K), lambda i, j: (0, j, 0)),
                  pl.BlockSpec((tm, 2 * FF_PAD), lambda i, j: (i, j))],
        out_specs=pl.BlockSpec((tm, 2 * FF_PAD), lambda i, j: (i, j)),
        out_shape=jax.ShapeDtypeStruct((T, 2 * D_FF_PAD), BF16),
        compiler_params=_cp("parallel", "parallel"))(dxo, w, gu)


def _loss_head(x, g, target, name, tm=512):
    T, K = x.shape

    def loss_fn(xv, gv, tv):
        err = _rms(xv, gv) - tv
        return 0.5 * jnp.sum(jnp.mean(err * err, axis=-1, keepdims=True), axis=0, keepdims=True)

    def body(x_ref, g_ref, t_ref, l_ref, dx_ref, dg_ref):
        val, vjp = jax.vjp(lambda xv, gv: loss_fn(xv, gv, t_ref[...]), x_ref[...], g_ref[...])
        dx, dg = vjp(jnp.ones((1, 1), F32))
        dx_ref[...] = dx
        lval = jnp.broadcast_to(val, (1, 128))

        @pl.when(pl.program_id(0) == 0)
        def _():
            dg_ref[...] = dg
            l_ref[...] = lval

        @pl.when(pl.program_id(0) > 0)
        def _():
            dg_ref[...] += dg
            l_ref[...] += lval

    row = pl.BlockSpec((tm, K), lambda i: (i, 0))
    vec = pl.BlockSpec((1, K), lambda i: (0, 0))
    return pl.pallas_call(
        body, grid=(T // tm,), name=name,
        in_specs=[row, vec, row], out_specs=[pl.BlockSpec((1, 128), lambda i: (0, 0)), row, vec],
        out_shape=[jax.ShapeDtypeStruct((1, 128), F32), jax.ShapeDtypeStruct((T, K), F32),
                   jax.ShapeDtypeStruct((1, K), F32)],
        compiler_params=_cp("arbitrary"))(x, g, target)


SGU_ROWS = 4 * BLOCK


def _sgu_fn(u, v, wm, bt, gain):
    ug = jax.nn.gelu(u)
    vg = jax.nn.gelu(v)
    row = lax.broadcasted_iota(jnp.int32, (BLOCK, BLOCK), 0)
    col = lax.broadcasted_iota(jnp.int32, (BLOCK, BLOCK), 1)
    tri = row >= col
    normed = []
    for h in range(GROUP_HEADS):
        vh = vg[:, h * HEAD_DIM:(h + 1) * HEAD_DIM]
        xc = vh - jnp.mean(vh, axis=-1, keepdims=True)
        normed.append(xc * lax.rsqrt(jnp.mean(xc * xc, axis=-1, keepdims=True) + EPS))
    vn = jnp.concatenate(normed, axis=1)
    wcat = jnp.concatenate([jnp.where(tri, wm[h], 0.0) for h in range(GROUP_HEADS)], axis=1)
    bias = jnp.concatenate([jnp.broadcast_to(bt[:, h:h + 1], (BLOCK, HEAD_DIM)) for h in range(GROUP_HEADS)], axis=1)
    mixes = []
    for c in range(u.shape[0] // BLOCK):
        chunk = vn[c * BLOCK:(c + 1) * BLOCK]
        stacked = jnp.concatenate([jnp.where(_head_mask(h, chunk.shape), chunk, 0.0) for h in range(GROUP_HEADS)], axis=0)
        mixes.append(_bdot(wcat, stacked) + bias)
    return _rms(ug * jnp.concatenate(mixes, axis=0), gain)


def _sgu_fwd(proj, wm, bt, gain, name, items=()):
    T = proj.shape[0]
    rows = min(SGU_ROWS, T)

    def body(u_ref, v_ref, w_ref, b_ref, g_ref, o_ref):
        o_ref[...] = _sgu_fn(u_ref[...], v_ref[...], w_ref[...], b_ref[...], g_ref[...]).astype(BF16)

    full = lambda shape: pl.BlockSpec(shape, lambda i: (0,) * len(shape))
    return _call(
        body, (proj, proj, wm, bt, gain), grid=(T // rows,), name=name, items=items,
        in_specs=[pl.BlockSpec((rows, GROUP_WIDTH), lambda i: (i, 0)), pl.BlockSpec((rows, GROUP_WIDTH), lambda i: (i, 1)),
                  full(wm.shape), full(bt.shape), full(gain.shape)],
        out_specs=[pl.BlockSpec((rows, GROUP_WIDTH), lambda i: (i, 0))],
        out_shape=[jax.ShapeDtypeStruct((T, GROUP_WIDTH), BF16)],
        sem=("parallel",))


def _acc_out(first, refs, vals):
    @pl.when(first)
    def _():
        for r, v in zip(refs, vals):
            r[...] = v

    @pl.when(jnp.logical_not(first))
    def _():
        for r, v in zip(refs, vals):
            r[...] += v


def _sgu_bwd(proj, wm, bt, gain, dy, name, items=()):
    T = proj.shape[0]

    def body(u_ref, v_ref, w_ref, b_ref, g_ref, dy_ref, duv_ref, dw_ref, db_ref, dg_ref):
        _, vjp = jax.vjp(_sgu_fn, u_ref[...], v_ref[...], w_ref[...], b_ref[...], g_ref[...])
        du, dv, dw, db, dg = vjp(dy_ref[...])
        duv_ref[:, :GROUP_WIDTH] = du.astype(BF16)
        duv_ref[:, GROUP_WIDTH:] = dv.astype(BF16)
        _acc_out(pl.program_id(0) == 0, (dw_ref, db_ref, dg_ref), (dw, db, dg))

    full = lambda shape: pl.BlockSpec(shape, lambda i: (0,) * len(shape))
    rows = min(SGU_ROWS, T)
    return _call(
        body, (proj, proj, wm, bt, gain, dy), grid=(T // rows,), name=name, items=items,
        in_specs=[pl.BlockSpec((rows, GROUP_WIDTH), lambda i: (i, 0)), pl.BlockSpec((rows, GROUP_WIDTH), lambda i: (i, 1)),
                  full(wm.shape), full(bt.shape), full(gain.shape),
                  pl.BlockSpec((rows, GROUP_WIDTH), lambda i: (i, 0))],
        out_specs=[pl.BlockSpec((rows, 2 * GROUP_WIDTH), lambda i: (i, 0)), full(wm.shape), full(bt.shape), full(gain.shape)],
        out_shape=[jax.ShapeDtypeStruct((T, 2 * GROUP_WIDTH), BF16), jax.ShapeDtypeStruct(wm.shape, F32),
                   jax.ShapeDtypeStruct(bt.shape, F32), jax.ShapeDtypeStruct(gain.shape, F32)],
        sem=("arbitrary",))


def _pool_consts(seq):
    t = lax.broadcasted_iota(jnp.int32, (seq, GROUP_WIDTH), 0)
    grp = lax.broadcasted_iota(jnp.int32, (seq, GROUP_WIDTH), 1) // (GROUP_WIDTH // len(POOL_WINDOWS))
    win = jnp.where(grp == 0, POOL_WINDOWS[0], jnp.where(grp == 1, POOL_WINDOWS[1],
                    jnp.where(grp == 2, POOL_WINDOWS[2], POOL_WINDOWS[3])))
    count = jnp.minimum(t + 1, win).astype(F32)
    return t, grp, count


def _by_group(grp, vals):
    return jnp.where(grp == 0, vals[0], jnp.where(grp == 1, vals[1], jnp.where(grp == 2, vals[2], vals[3])))


def _window_sums(x, t, seq, back):
    def shift(a, k):
        if back:
            return jnp.where(t >= k, pltpu.roll(a, k, 0), 0.0)
        return jnp.where(t < seq - k, pltpu.roll(a, seq - k, 0), 0.0)
    sums = []
    a, k = x, 1
    for _ in POOL_WINDOWS:
        a = a + shift(a, k)
        sums.append(a)
        k *= 2
    return sums


def _pool_tail(y, wbd, scale, gain):
    return _rms(_bdot(y, wbd) * scale, gain)


def _pool_fwd(proj, wbd, scale, gain, seq, name):
    T = proj.shape[0]

    def body(p_ref, w_ref, s_ref, g_ref, o_ref):
        p = p_ref[...]
        t, grp, count = _pool_consts(seq)
        y = _by_group(grp, _window_sums(p, t, seq, True)) / count - p
        o_ref[...] = _pool_tail(y, w_ref[...], s_ref[...], g_ref[...]).astype(BF16)

    full = lambda shape: pl.BlockSpec(shape, lambda b: (0,) * len(shape))
    return pl.pallas_call(
        body, grid=(T // seq,), name=name,
        in_specs=[pl.BlockSpec((seq, GROUP_WIDTH), lambda b: (b, 2)), full(wbd.shape), full(scale.shape), full(gain.shape)],
        out_specs=pl.BlockSpec((seq, GROUP_WIDTH), lambda b: (b, 0)),
        out_shape=jax.ShapeDtypeStruct((T, GROUP_WIDTH), BF16),
        compiler_params=_cp("parallel"))(proj, wbd, scale, gain)


def _pool_bwd(proj, wbd, scale, gain, dy, seq, name):
    T = proj.shape[0]

    def body(p_ref, w_ref, s_ref, g_ref, dy_ref, dp_ref, dw_ref, ds_ref, dg_ref):
        p = p_ref[...]
        t, grp, count = _pool_consts(seq)
        y = _by_group(grp, _window_sums(p, t, seq, True)) / count - p
        _, vjp = jax.vjp(_pool_tail, y, w_ref[...], s_ref[...], g_ref[...])
        d_y, dw, ds, dg = vjp(dy_ref[...])
        dp = _by_group(grp, _window_sums(d_y / count, t, seq, False)) - d_y
        dp_ref[...] = dp.astype(BF16)
        _acc_out(pl.program_id(0) == 0, (dw_ref, ds_ref, dg_ref), (dw, ds, dg))

    full = lambda shape: pl.BlockSpec(shape, lambda b: (0,) * len(shape))
    return pl.pallas_call(
        body, grid=(T // seq,), name=name,
        in_specs=[pl.BlockSpec((seq, GROUP_WIDTH), lambda b: (b, 2)), full(wbd.shape), full(scale.shape), full(gain.shape),
                  pl.BlockSpec((seq, GROUP_WIDTH), lambda b: (b, 1))],
        out_specs=[pl.BlockSpec((seq, GROUP_WIDTH), lambda b: (b, 0)), full(wbd.shape), full(scale.shape), full(gain.shape)],
        out_shape=[jax.ShapeDtypeStruct((T, GROUP_WIDTH), BF16), jax.ShapeDtypeStruct(wbd.shape, F32),
                   jax.ShapeDtypeStruct(scale.shape, F32), jax.ShapeDtypeStruct(gain.shape, F32)],
        compiler_params=_cp("arbitrary"))(proj, wbd, scale, gain, dy)


def _swa_fn(q, kv_prev, kv_cur, sinks, tab, gain, first):
    half = GROUP_WIDTH // 2
    k2 = jnp.concatenate([kv_prev[:, :half], kv_cur[:, :half]], axis=0)
    v2 = jnp.concatenate([kv_prev[:, half:], kv_cur[:, half:]], axis=0)
    per_query_head = lambda a: jnp.concatenate(
        [a[:, (h // 2) * HEAD_DIM:(h // 2 + 1) * HEAD_DIM] for h in range(GROUP_HEADS)], axis=1)
    qs = jnp.concatenate([jnp.where(_head_mask(h, q.shape), q, 0.0) for h in range(GROUP_HEADS)], axis=0)
    qi = lax.broadcasted_iota(jnp.int32, (HEAD_ROWS, 2 * BLOCK), 0) & (BLOCK - 1)
    kj = lax.broadcasted_iota(jnp.int32, (HEAD_ROWS, 2 * BLOCK), 1)
    dist = qi + BLOCK - kj
    mask = (dist >= 0) & (dist < BLOCK) & ((kj >= BLOCK) | jnp.logical_not(first))
    logits = _bdot_nt(qs, per_query_head(k2)) * ATT_SCALE + tab.reshape(HEAD_ROWS, 2 * BLOCK)
    logits = jnp.where(mask, logits, -1e30)
    sink = jnp.concatenate([jnp.broadcast_to(sinks[:, h:h + 1], (BLOCK, 1)) for h in range(GROUP_HEADS)], axis=0)
    m = lax.stop_gradient(jnp.maximum(jnp.max(logits, axis=1, keepdims=True), sink))
    p = jnp.exp(logits - m)
    probs = p / (jnp.sum(p, axis=1, keepdims=True) + jnp.exp(sink - m))
    out = _bdot(probs, per_query_head(v2))
    y = jnp.zeros_like(q)
    for h in range(GROUP_HEADS):
        y = y + jnp.where(_head_mask(h, q.shape), out[h * BLOCK:(h + 1) * BLOCK], 0.0)
    return _rms(y, gain)


def _swa_specs(nblk):
    q = pl.BlockSpec((BLOCK, GROUP_WIDTH), lambda b, i: (b * nblk + i, 3))
    cur = pl.BlockSpec((BLOCK, GROUP_WIDTH), lambda b, i: (b * nblk + i, 4))
    prev = pl.BlockSpec((BLOCK, GROUP_WIDTH), lambda b, i: (b * nblk + jnp.maximum(i - 1, 0), 4))
    return q, prev, cur


def _swa_fwd(proj, sinks, tab, gain, seq, name, items=()):
    T = proj.shape[0]
    nblk = seq // BLOCK

    def body(q_ref, kp_ref, kc_ref, s_ref, t_ref, g_ref, o_ref):
        o_ref[...] = _swa_fn(q_ref[...], kp_ref[...], kc_ref[...], s_ref[...], t_ref[...], g_ref[...],
                             pl.program_id(1) == 0).astype(BF16)

    full = lambda shape: pl.BlockSpec(shape, lambda b, i: (0,) * len(shape))
    return _call(
        body, (proj, proj, proj, sinks, tab, gain), grid=(T // seq, nblk), name=name, items=items,
        in_specs=[*_swa_specs(nblk), full(sinks.shape), full(tab.shape), full(gain.shape)],
        out_specs=[pl.BlockSpec((BLOCK, GROUP_WIDTH), lambda b, i: (b * nblk + i, 0))],
        out_shape=[jax.ShapeDtypeStruct((T, GROUP_WIDTH), BF16)],
        sem=("parallel", "parallel"))


def _swa_bwd(proj, sinks, tab, gain, dy, seq, name, items=()):
    T = proj.shape[0]
    nblk = seq // BLOCK

    def body(q_ref, kp_ref, kc_ref, s_ref, t_ref, g_ref, dy_ref, dq_ref, dkp_ref, dkc_ref, ds_ref, dt_ref, dg_ref):
        first = pl.program_id(1) == 0
        fn = functools.partial(_swa_fn, first=first)
        _, vjp = jax.vjp(fn, q_ref[...], kp_ref[...], kc_ref[...], s_ref[...], t_ref[...], g_ref[...])
        dq, dkp, dkc, ds, dt, dg = vjp(dy_ref[...])
        dq_ref[...] = dq.astype(BF16)
        dkp_ref[...] = dkp
        dkc_ref[...] = dkc
        _acc_out((pl.program_id(0) == 0) & first, (ds_ref, dt_ref, dg_ref), (ds, dt, dg))

    full = lambda shape: pl.BlockSpec(shape, lambda b, i: (0,) * len(shape))
    blk = lambda c: pl.BlockSpec((BLOCK, GROUP_WIDTH), lambda b, i: (b * nblk + i, c))
    return _call(
        body, (proj, proj, proj, sinks, tab, gain, dy), grid=(T // seq, nblk), name=name, items=items,
        in_specs=[*_swa_specs(nblk), full(sinks.shape), full(tab.shape), full(gain.shape), blk(2)],
        out_specs=[blk(0), blk(0), blk(0), full(sinks.shape), full(tab.shape), full(gain.shape)],
        out_shape=[jax.ShapeDtypeStruct((T, GROUP_WIDTH), BF16), jax.ShapeDtypeStruct((T, GROUP_WIDTH), F32),
                   jax.ShapeDtypeStruct((T, GROUP_WIDTH), F32), jax.ShapeDtypeStruct(sinks.shape, F32),
                   jax.ShapeDtypeStruct(tab.shape, F32), jax.ShapeDtypeStruct(gain.shape, F32)],
        sem=("arbitrary", "arbitrary"))


def _shift_add(cur, prev, seq, name):
    T = cur.shape[0]
    nblk = seq // BLOCK

    def body(c_ref, p_ref, o_ref):
        last = pl.program_id(1) == nblk - 1
        o_ref[...] = (c_ref[...] + jnp.where(last, 0.0, p_ref[...])).astype(BF16)

    return pl.pallas_call(
        body, grid=(T // seq, nblk), name=name,
        in_specs=[pl.BlockSpec((BLOCK, GROUP_WIDTH), lambda b, i: (b * nblk + i, 0)),
                  pl.BlockSpec((BLOCK, GROUP_WIDTH), lambda b, i: (b * nblk + jnp.minimum(i + 1, nblk - 1), 0))],
        out_specs=pl.BlockSpec((BLOCK, GROUP_WIDTH), lambda b, i: (b * nblk + i, 0)),
        out_shape=jax.ShapeDtypeStruct((T, GROUP_WIDTH), BF16),
        compiler_params=_cp("parallel", "parallel"))(cur, prev)


def _t5_bucket(dist):
    max_exact = N_BUCKETS // 2
    df = jnp.maximum(dist, 1).astype(F32)
    large = max_exact + (jnp.log(df / max_exact) / jnp.log(jnp.float32(MAX_DISTANCE / max_exact))
                         * (N_BUCKETS - max_exact)).astype(jnp.int32)
    return jnp.where(dist < max_exact, dist, jnp.minimum(large, N_BUCKETS - 1))


def _bucket_map():
    dist = (jnp.arange(BLOCK)[:, None] + BLOCK) - jnp.arange(2 * BLOCK)[None, :]
    return _t5_bucket(jnp.clip(dist, 0, BLOCK - 1)).astype(jnp.int32)


def _bias_table(rel_bias, buckets, name):
    def body(rb_ref, bk_ref, o_ref):
        bk = bk_ref[...]
        rb = rb_ref[...]
        for h in range(GROUP_HEADS):
            acc = jnp.zeros((BLOCK, 2 * BLOCK), F32)
            for b in range(N_BUCKETS):
                acc = jnp.where(bk == b, rb[b:b + 1, h:h + 1], acc)
            o_ref[h] = acc

    return pl.pallas_call(body, name=name, out_shape=jax.ShapeDtypeStruct((GROUP_HEADS, BLOCK, 2 * BLOCK), F32),
                          compiler_params=_cp())(rel_bias, buckets)


def _bias_table_bwd(dtab, buckets, name):
    def body(dt_ref, bk_ref, o_ref):
        bk = bk_ref[...]
        row = lax.broadcasted_iota(jnp.int32, (N_BUCKETS, GROUP_HEADS), 0)
        col = lax.broadcasted_iota(jnp.int32, (N_BUCKETS, GROUP_HEADS), 1)
        acc = jnp.zeros((N_BUCKETS, GROUP_HEADS), F32)
        for h in range(GROUP_HEADS):
            dt = dt_ref[h]
            for b in range(N_BUCKETS):
                s = jnp.sum(jnp.where(bk == b, dt, 0.0), keepdims=True)
                acc = acc + jnp.where((row == b) & (col == h), s, 0.0)
        o_ref[...] = acc

    return pl.pallas_call(body, name=name, out_shape=jax.ShapeDtypeStruct((N_BUCKETS, GROUP_HEADS), F32),
                          compiler_params=_cp())(dtab, buckets)


HEAD_ROWS = GROUP_HEADS * BLOCK


def _stack_heads(x):
    return jnp.concatenate([jnp.where(_head_mask(h, x.shape), x, 0.0) for h in range(GROUP_HEADS)], axis=0).astype(BF16)


def _sb_tile(qs, kb, q0, k0):
    z = lax.dot_general(qs, kb, _NT, preferred_element_type=F32)
    row = lax.broadcasted_iota(jnp.int32, (HEAD_ROWS, BLOCK), 0) & (BLOCK - 1)
    col = lax.broadcasted_iota(jnp.int32, (HEAD_ROWS, BLOCK), 1)
    causal = (k0 + col) < (q0 + row)
    ls_neg = -(jnp.maximum(z, 0.0) + jnp.log(1.0 + jnp.exp(-jnp.abs(z))))
    return jnp.where(causal, ls_neg, 0.0), ls_neg + z, causal


SB_DEAD = -104.0
SB_FIRST_LANE = GROUP_HEADS


def _tri(strict_upper_src):
    r = lax.broadcasted_iota(jnp.int32, (BLOCK, BLOCK), 0)
    c = lax.broadcasted_iota(jnp.int32, (BLOCK, BLOCK), 1)
    cond = {"gt": r > c, "le": r <= c, "lt": r < c}[strict_upper_src]
    return jnp.where(cond, 1.0, 0.0).astype(BF16)


def _sb_fwd(proj, gain, seq, name, items=()):
    T = proj.shape[0]
    nblk = seq // BLOCK

    def body(q_ref, k_ref, v_ref, g_ref, o_ref, raw_ref, bt_ref):
        i = pl.program_id(1)
        q = q_ref[...]
        u_gt = _tri("gt")
        lane = lax.broadcasted_iota(jnp.int32, (BLOCK, BLOCK), 1)
        heads = [slice(h * HEAD_DIM, (h + 1) * HEAD_DIM) for h in range(GROUP_HEADS)]
        rows = [slice(h * BLOCK, (h + 1) * BLOCK) for h in range(GROUP_HEADS)]
        qs = _stack_heads(q * ATT_SCALE)

        def live(carry):
            j, _, cb = carry
            return (j >= 0) & (jnp.max(cb) > SB_DEAD)

        def step(carry):
            j, accs, cb = carry
            ks = pl.multiple_of(j * BLOCK, BLOCK)
            kb = k_ref[pl.ds(ks, BLOCK), :].astype(BF16)
            vb = v_ref[pl.ds(ks, BLOCK), :].astype(BF16)
            b, a, causal = _sb_tile(qs, kb, i * BLOCK, j * BLOCK)
            tail = _split_dot(b, u_gt) + cb
            w = jnp.where(causal, jnp.exp(a + tail), 0.0).astype(BF16)
            accs = tuple(accs[h] + jnp.dot(w[rows[h]], vb[:, hs], preferred_element_type=F32)
                         for h, hs in enumerate(heads))
            return j - 1, accs, cb + jnp.sum(b, axis=1, keepdims=True)

        zero_acc = tuple(jnp.zeros((BLOCK, HEAD_DIM), F32) for _ in heads)
        j_end, accs, cb = lax.while_loop(live, step, (i, zero_acc, jnp.zeros((HEAD_ROWS, 1), F32)))
        side = jnp.where(lane == SB_FIRST_LANE, (j_end + 1).astype(F32), 0.0)
        for h in range(GROUP_HEADS):
            side = jnp.where(lane == h, cb[rows[h]], side)
        raw = jnp.concatenate(accs, axis=1)
        raw_ref[...] = raw
        bt_ref[...] = side
        o_ref[...] = _rms(raw, g_ref[...]).astype(BF16)

    return _call(
        body, (proj, proj, proj, gain), grid=(T // seq, nblk), name=name, items=items,
        in_specs=[pl.BlockSpec((BLOCK, GROUP_WIDTH), lambda b, i: (b * nblk + i, 5)),
                  pl.BlockSpec((seq, GROUP_WIDTH), lambda b, i: (b, 6)),
                  pl.BlockSpec((seq, GROUP_WIDTH), lambda b, i: (b, 7)),
                  pl.BlockSpec(gain.shape, lambda b, i: (0, 0))],
        out_specs=[pl.BlockSpec((BLOCK, GROUP_WIDTH), lambda b, i: (b * nblk + i, 0)),
                   pl.BlockSpec((BLOCK, GROUP_WIDTH), lambda b, i: (b * nblk + i, 0)),
                   pl.BlockSpec((BLOCK, BLOCK), lambda b, i: (b * nblk + i, 0))],
        out_shape=[jax.ShapeDtypeStruct((T, GROUP_WIDTH), BF16), jax.ShapeDtypeStruct((T, GROUP_WIDTH), F32),
                   jax.ShapeDtypeStruct((T, BLOCK), F32)],
        sem=("parallel", "parallel"))


def _sb_bwd(proj, gain, raw, btot, dy, seq, name, items=()):
    T = proj.shape[0]
    nblk = seq // BLOCK

    def body(q_ref, k_ref, v_ref, g_ref, raw_ref, bt_ref, dy_ref, dq_ref, dk_ref, dv_ref, dg_ref):
        i = pl.program_id(1)

        @pl.when(i == 0)
        def _():
            dk_ref[...] = jnp.zeros_like(dk_ref)
            dv_ref[...] = jnp.zeros_like(dv_ref)

        rawv = raw_ref[...]
        _, vjp = jax.vjp(_rms, rawv, g_ref[...])
        do, dg = vjp(dy_ref[...])
        _acc_out((pl.program_id(0) == 0) & (i == 0), (dg_ref,), (dg,))
        q = q_ref[...]
        bt = bt_ref[...]
        u_le = _tri("le")
        u_lt = _tri("lt")
        heads = [slice(h * HEAD_DIM, (h + 1) * HEAD_DIM) for h in range(GROUP_HEADS)]
        rows = [slice(h * BLOCK, (h + 1) * BLOCK) for h in range(GROUP_HEADS)]
        qs = _stack_heads(q * ATT_SCALE)
        dos = _stack_heads(do)
        bts = jnp.concatenate([bt[:, h:h + 1] for h in range(GROUP_HEADS)], axis=0)
        first = jnp.max(bt[:, SB_FIRST_LANE:SB_FIRST_LANE + 1]).astype(jnp.int32)
        first = jnp.minimum(jnp.maximum(first, 0), i)

        def step(j, carry):
            dqs, cb, cg = carry
            ks = pl.multiple_of(j * BLOCK, BLOCK)
            kb = k_ref[pl.ds(ks, BLOCK), :].astype(BF16)
            vb = v_ref[pl.ds(ks, BLOCK), :].astype(BF16)
            b, a, causal = _sb_tile(qs, kb, i * BLOCK, j * BLOCK)
            tail = bts - (_split_dot(b, u_le) + cb)
            w = jnp.where(causal, jnp.exp(a + tail), 0.0)
            sig = jnp.exp(a)
            g = w * lax.dot_general(dos, vb, _NT, preferred_element_type=F32)
            gpre = _split_dot(g, u_lt) + cg
            dz = jnp.where(causal, g * (1.0 - sig) - gpre * sig, 0.0).astype(BF16)
            dqs = tuple(dqs[h] + jnp.dot(dz[rows[h]], kb[:, hs], preferred_element_type=F32)
                        for h, hs in enumerate(heads))
            dk_ref[pl.ds(ks, BLOCK), :] += lax.dot_general(dz, qs, _TN, preferred_element_type=F32)
            dv_ref[pl.ds(ks, BLOCK), :] += lax.dot_general(w.astype(BF16), dos, _TN, preferred_element_type=F32)
            return dqs, cb + jnp.sum(b, axis=1, keepdims=True), cg + jnp.sum(g, axis=1, keepdims=True)

        zero_dq = tuple(jnp.zeros((BLOCK, HEAD_DIM), F32) for _ in heads)
        zero = jnp.zeros((HEAD_ROWS, 1), F32)
        dqs, _, _ = lax.fori_loop(first, i + 1, step, (zero_dq, zero, zero))
        dq_ref[...] = (jnp.concatenate(dqs, axis=1) * ATT_SCALE).astype(BF16)

    blk = lambda c: pl.BlockSpec((BLOCK, GROUP_WIDTH), lambda b, i: (b * nblk + i, c))
    seqblk = lambda c: pl.BlockSpec((seq, GROUP_WIDTH), lambda b, i: (b, c))
    vec = pl.BlockSpec(gain.shape, lambda b, i: (0, 0))
    return _call(
        body, (proj, proj, proj, gain, raw, btot, dy), grid=(T // seq, nblk), name=name, items=items,
        in_specs=[blk(5), seqblk(6), seqblk(7), vec, blk(0), pl.BlockSpec((BLOCK, BLOCK), lambda b, i: (b * nblk + i, 0)),
                  blk(3)],
        out_specs=[blk(0), seqblk(0), seqblk(0), vec],
        out_shape=[jax.ShapeDtypeStruct((T, GROUP_WIDTH), BF16), jax.ShapeDtypeStruct((T, GROUP_WIDTH), F32),
                   jax.ShapeDtypeStruct((T, GROUP_WIDTH), F32), jax.ShapeDtypeStruct(gain.shape, F32)],
        sem=("arbitrary", "arbitrary"))


def _layer_params(l, sgu_w, sgu_b, pool_w, pool_scale, swa_sinks, mix_out_gain, norm_mix, norm_ffn):
    gains = mix_out_gain[l].reshape(4, 1, GROUP_WIDTH)
    return dict(
        wm=sgu_w[l], bt=sgu_b[l].T,
        wbd=jax.scipy.linalg.block_diag(*[pool_w[l, g] for g in range(len(POOL_WINDOWS))]),
        scale=pool_scale[l][None], sinks=swa_sinks[l][None],
        gain=[gains[m] for m in range(4)], norm_mix=norm_mix[l][None], norm_ffn=norm_ffn[l][None])


def _as_weights(g_in, g_out, g_gu, g_down):
    return (g_in[None], g_out.reshape(1, D_MODEL, D_MODEL), g_gu[None], g_down.reshape(1, D_FF_PAD, D_MODEL))


def _gather_item(src, l, dst_shape, rows=None, down=False, init=None):
    r0, nr = rows or (0, src.shape[1])
    if down:
        place = lambda dst, k: dst.at[k // 2, pl.ds((k % 2) * DOWN_ROWS + r0, nr), :]
    else:
        place = lambda dst, k: dst.at[k, pl.ds(r0, nr), :]
    return _Item(src, lambda s, k: s.at[l, pl.ds(r0, nr), :], dst_shape, place, init)


def _layer_fwd(l, x, p, w, tab, seq, shards):
    win, wout, wgu, wd = w
    nl = l + 1
    s_in, s_out, s_gu, s_down = shards or (None,) * 4
    ride = lambda item: (item(),) if shards else ()
    half_down = DOWN_ROWS // 2
    proj, h1, *g_down = _norm_mm(x, p["norm_mix"], win, 0, 4, F32, f"in_proj_{l}", items=ride(
        lambda: _gather_item(s_down, nl, (4, FF_PAD, D_MODEL), (0, half_down), True,
                             jnp.zeros((4, FF_PAD, D_MODEL), BF16))))
    ya, *g_out = _sgu_fwd(proj, p["wm"], p["bt"], p["gain"][0], f"sgu_fwd_{l}",
                          items=ride(lambda: _gather_item(s_out, nl, (N_DEV,) + s_out.shape[1:])))
    yb = _pool_fwd(proj, p["wbd"], p["scale"], p["gain"][1], seq, f"pool_fwd_{l}")
    yc, *g_down = _swa_fwd(proj, p["sinks"], tab, p["gain"][2], seq, f"swa_fwd_{l}", items=ride(
        lambda: _gather_item(s_down, nl, (4, FF_PAD, D_MODEL), (half_down, half_down), True, g_down[0])))
    half_gu = FF_PAD // 2
    yd, raw, btot, *g_gu = _sb_fwd(proj, p["gain"][3], seq, f"sb_fwd_{l}", items=ride(
        lambda: _gather_item(s_gu, nl, (N_DEV,) + s_gu.shape[1:], (0, half_gu))))
    ycat = jnp.concatenate([ya, yb, yc, yd], axis=1)
    xm, = _mm_res(x, ycat, wout, 0, f"out_proj_{l}", tn=D_MODEL)
    gu, act, h2, *g_gu = _ffn_up_act(xm, p["norm_ffn"], wgu, f"ffn_up_{l}", items=ride(
        lambda: _gather_item(s_gu, nl, (N_DEV,) + s_gu.shape[1:], (half_gu, half_gu), init=g_gu[0])))
    xo, *g_in = _mm_res(xm, act, wd, 0, f"ffn_down_{l}", tn=D_MODEL // 2,
                        items=ride(lambda: _gather_item(s_in, nl, (N_DEV,) + s_in.shape[1:])))
    w_next = _as_weights(g_in[0], g_out[0], g_gu[0], g_down[0]) if shards else None
    return xo, (x, proj, h1, ycat, raw, btot, xm, gu, h2, act), w_next


def _layer_bwd(l, dxo, saved, p, w, tab, seq, ride=()):
    win, wout, wgu, wd = w
    x, proj, h1, ycat, raw, btot, xm, gu, h2, act = saved
    out_rows = D_MODEL // N_DEV
    half = FF_PAD // 2

    def rows_of(g, r0, init=None):
        cut = lambda a, k: a.at[k, pl.ds(r0, half), :]
        return _Item(g, cut, g.shape, cut, init)

    dgu = _ffn_down_dx(dxo, wd, gu, f"ffn_down_dx_{l}")
    g_wd, = _mm_tn_rows(act, dxo, FF_PAD, f"ffn_down_dw_{l}")
    send_down = _Item(g_wd, lambda src, k: src.at[pl.ds((k // 2) * FF_PAD + (k % 2) * DOWN_ROWS, DOWN_ROWS), :],
                      (N_DEV, DOWN_ROWS, D_MODEL))
    g_wgu, p_down = _mm_tn_rows(dgu, h2, FF_PAD, f"ffn_up_dw_{l}", items=(send_down,), block_of=_ffn_chunk_block)
    g_wgu = g_wgu.reshape(N_DEV, FF_PAD, D_MODEL)
    dxm, g_norm_ffn, p_gu = _mm_norm_bwd(dgu, wgu, 0, 1, xm, p["norm_ffn"], dxo, f"ffn_up_dx_{l}",
                                         items=(rows_of(g_wgu, 0),), block_of=_ffn_chunk_block, transposed=True)
    dycat = _mm_nt(dxm, wout, 0, F32, f"out_proj_dx_{l}")
    g_wout, = _mm_tn_rows(ycat, dxm, D_MODEL, f"out_proj_dw_{l}")
    send_out = _Item(g_wout, lambda src, k: src.at[pl.ds(k * out_rows, out_rows), :], (N_DEV, out_rows, D_MODEL))
    duv, g_wm, g_bt, g_ga, p_out = _sgu_bwd(proj, p["wm"], p["bt"], p["gain"][0], dycat, f"sgu_bwd_{l}",
                                            items=(send_out,))
    dp, g_wbd, g_scale, g_gb = _pool_bwd(proj, p["wbd"], p["scale"], p["gain"][1], dycat, seq, f"pool_bwd_{l}")
    dq, dkp, dkc, g_sinks, g_tab, g_gc, *rode = _swa_bwd(proj, p["sinks"], tab, p["gain"][2], dycat, seq,
                                                         f"swa_bwd_{l}", items=ride)
    dkv = _shift_add(dkc, dkp, seq, f"swa_dkv_{l}")
    dqd, dkd, dvd, g_gd, p_gu = _sb_bwd(proj, p["gain"][3], raw, btot, dycat, seq, f"sb_bwd_{l}",
                                        items=(rows_of(g_wgu, half, p_gu),))
    dproj = jnp.concatenate([duv, dp, dq, dkv, dqd, dkd.astype(BF16), dvd.astype(BF16)], axis=1)
    g_win, = _mm_tn_cols(h1, dproj, GROUP_WIDTH, 4, f"in_proj_dw_{l}")
    dx, g_norm_mix, p_in = _mm_norm_bwd(dproj, win, 0, 4, x, p["norm_mix"], dxm, f"in_proj_dx_{l}",
                                        items=(_Item(g_win, lambda src, k: src.at[k], g_win.shape),))
    ng = len(POOL_WINDOWS)
    gd = GROUP_WIDTH // ng
    small = dict(
        sgu_w=g_wm, sgu_b=g_bt.T,
        pool_w=jnp.stack([g_wbd[g * gd:(g + 1) * gd, g * gd:(g + 1) * gd] for g in range(ng)]),
        pool_scale=g_scale[0], swa_sinks=g_sinks[0],
        mix_out_gain=jnp.concatenate([g_ga[0], g_gb[0], g_gc[0], g_gd[0]]),
        norm_mix=g_norm_mix[0], norm_ffn=g_norm_ffn[0])
    return dx, (p_in, p_out, p_gu, p_down), small, g_tab, rode


def _local_step(x, target, w0, shards, sgu_w, sgu_b, pool_w, pool_scale, swa_sinks, rel_bias, mix_out_gain,
                norm_mix, norm_ffn, norm_final, seq):
    buckets = _bucket_map()
    tab = _bias_table(rel_bias, buckets, "bias_table")
    params = [_layer_params(l, sgu_w, sgu_b, pool_w, pool_scale, swa_sinks, mix_out_gain, norm_mix, norm_ffn)
              for l in range(DEPTH)]
    saved, weights = [], [w0]
    for l in range(DEPTH):
        x, s, w_next = _layer_fwd(l, x, params[l], weights[l], tab, seq, shards if l + 1 < DEPTH else None)
        saved.append(s)
        weights.append(w_next)
    loss, dx, g_final = _loss_head(x, norm_final[None], target, "loss_head")
    big, small, g_tab, upper = [None] * DEPTH, [None] * DEPTH, None, None
    for l in reversed(range(DEPTH)):
        ride = ()
        if l == 0:
            mine = _pack(_upper_layers(small), _pack_rows(_upper_layers(small)))
            ride = (_Item(mine, lambda src, k: src, (N_DEV,) + mine.shape),)
        dx, big[l], small[l], t, rode = _layer_bwd(l, dx, saved[l], params[l], weights[l], tab, seq, ride)
        g_tab = t if g_tab is None else g_tab + t
        upper = rode[0] if rode else upper
    return loss, dx, big, small, _bias_table_bwd(g_tab, buckets, "bias_table_bwd"), g_final[0], upper


def _cast_pad(w, rows, name):
    L, r, c = w.shape

    def body(w_ref, o_ref):
        if rows != r:
            o_ref[...] = jnp.zeros_like(o_ref)
        o_ref[:r, :] = w_ref[...].astype(BF16)

    return pl.pallas_call(
        body, grid=(L,), name=name,
        in_specs=[pl.BlockSpec((None, r, c), lambda l: (l, 0, 0))],
        out_specs=pl.BlockSpec((None, rows, c), lambda l: (l, 0, 0)),
        out_shape=jax.ShapeDtypeStruct((L, rows, c), BF16),
        compiler_params=_cp("parallel"))(w)


def _down_rows(ref, k):
    return ref.at[:, k // 2, pl.ds((k % 2) * DOWN_ROWS, DOWN_ROWS), :]


def _all_gather_weights(s_in, s_out, s_gu, s_down):
    L = s_in.shape[0]
    shards = (s_in, s_out, s_gu, s_down)
    down_full = jnp.zeros((L, 4, FF_PAD, D_MODEL), BF16)
    n = len(shards)

    def body(i0, i1, i2, i3, _, o0, o1, o2, o3, send_sems, recv_sems, local_sems):
        srcs = (i0, i1, i2, i3)
        outs = (o0, o1, o2, o3)
        x, y, c, _me = _my_place()
        chips = [(1 - x, y), (x, 1 - y), (1 - x, 1 - y)]

        def place(t, dev):
            k = 4 * dev[0] + 2 * dev[1] + dev[2]
            return _down_rows(outs[t], k) if t == 3 else outs[t].at[:, k]

        def copy(slot, t, block, to, from_shard):
            return pltpu.make_async_remote_copy(
                src_ref=srcs[t] if from_shard else place(t, block), dst_ref=place(t, block),
                send_sem=send_sems.at[slot, t], recv_sem=recv_sems.at[slot, t], device_id=to, device_id_type=MESH)

        me, sibling = (x, y, c), (x, y, 1 - c)
        mine = [pltpu.make_async_copy(srcs[t], place(t, me), local_sems.at[t]) for t in range(n)]
        for cp in mine:
            cp.start()
        first = [copy(0, t, me, sibling, True) for t in range(n)]
        first += [copy(1 + j, t, me, (*chip, c), True) for j, chip in enumerate(chips) for t in range(n)]
        for cp in first:
            cp.start()
        passed = []
        for j, chip in enumerate(chips):
            for t in range(n):
                copy(1 + j, t, (*chip, c), me, False).wait_recv()
                cp = copy(4 + j, t, (*chip, c), sibling, False)
                cp.start()
                passed.append(cp)
        for t in range(n):
            copy(0, t, sibling, me, False).wait_recv()
        for j, chip in enumerate(chips):
            for t in range(n):
                copy(4 + j, t, (*chip, 1 - c), me, False).wait_recv()
        for cp in first + passed:
            cp.wait_send()
        for cp in mine:
            cp.wait()

    shapes = [jax.ShapeDtypeStruct((L, N_DEV) + s.shape[1:], BF16) for s in shards[:3]]
    shapes.append(jax.ShapeDtypeStruct(down_full.shape, BF16))
    return pl.pallas_call(
        body, name="all_gather_weights", out_shape=shapes,
        in_specs=[ANY] * 5, out_specs=[ANY] * 4, input_output_aliases={4: 3},
        scratch_shapes=[pltpu.SemaphoreType.DMA((7, n)), pltpu.SemaphoreType.DMA((7, n)), pltpu.SemaphoreType.DMA((n,))],
        )(*shards, down_full)


def _adamw(w, g, m, v):
    m = ADAM_B1 * m + (1.0 - ADAM_B1) * g
    v = ADAM_B2 * v + (1.0 - ADAM_B2) * jnp.square(g)
    m_hat = m / (1.0 - ADAM_B1 ** ADAM_STEP)
    v_hat = v / (1.0 - ADAM_B2 ** ADAM_STEP)
    delta = -ADAM_LR * (m_hat / (jnp.sqrt(v_hat) + ADAM_EPS) + ADAM_WD * w)
    return delta, m, v


def _adamw_sharded(parts, w, m, v, tr, name, items=()):
    L, r, c = w.shape
    cp = parts[0].shape[-1]
    nrow = r // tr

    def body(*refs):
        p_refs, (w_ref, m_ref, v_ref, g_ref, d_ref, nm_ref, nv_ref) = refs[:L], refs[L:]
        for k in range(L):
            @pl.when(pl.program_id(0) == k)
            def _(p_ref=p_refs[k]):
                g = p_ref[0, :, :c].astype(F32)
                for dev in range(1, N_DEV):
                    g = g + p_ref[dev, :, :c].astype(F32)
                delta, nm, nv = _adamw(w_ref[...], g, m_ref[...], v_ref[...])
                g_ref[...] = g
                d_ref[...] = delta
                nm_ref[...] = nm
                nv_ref[...] = nv

    def part_spec(k):
        return pl.BlockSpec((N_DEV, tr, cp),
                            lambda l, i: (0, jnp.where(l == k, i, jnp.where(l < k, 0, nrow - 1)), 0))

    blk = pl.BlockSpec((None, tr, c), lambda l, i: (l, i, 0))
    out = jax.ShapeDtypeStruct((L, r, c), F32)
    return _call(
        body, (*parts, w, m, v), grid=(L, nrow), name=name, items=items,
        in_specs=[part_spec(k) for k in range(L)] + [blk, blk, blk],
        out_specs=[blk] * 4, out_shape=[out] * 4, sem=("arbitrary", "arbitrary"))


def _adamw_small(parts, w, m, v, name):
    def body(p_ref, w_ref, m_ref, v_ref, g_ref, d_ref, nm_ref, nv_ref):
        g = p_ref[0]
        for k in range(1, N_DEV):
            g = g + p_ref[k]
        delta, nm, nv = _adamw(w_ref[...], g, m_ref[...], v_ref[...])
        g_ref[...] = g
        d_ref[...] = delta
        nm_ref[...] = nm
        nv_ref[...] = nv

    out = jax.ShapeDtypeStruct(w.shape, F32)
    return pl.pallas_call(body, name=name, out_shape=[out] * 4, compiler_params=_cp())(parts, w, m, v)


LAYERED = ("sgu_w", "sgu_b", "pool_w", "pool_scale", "swa_sinks", "mix_out_gain", "norm_mix", "norm_ffn")
SHARED = ("rel_bias", "norm_final")


def _seg_rows(a):
    return -(-a.size // 128)


def _pack_rows(parts):
    return -(-sum(_seg_rows(p) for p in parts) // 8) * 8


def _upper_layers(per_layer):
    if isinstance(per_layer, dict):
        return [per_layer[k][1:] for k in LAYERED]
    return [jnp.stack([per_layer[l][k] for l in range(1, DEPTH)]) for k in LAYERED]


def _layer_zero(stacked):
    return [stacked[k][:1] for k in LAYERED] + [stacked[k] for k in SHARED]


def _pack(parts, rows):
    segs = [jnp.pad(p.reshape(-1), (0, _seg_rows(p) * 128 - p.size)).reshape(_seg_rows(p), 128) for p in parts]
    used = sum(s.shape[0] for s in segs)
    return jnp.concatenate(segs + [jnp.zeros((rows - used, 128), F32)], axis=0)


def _unpack(buf, like):
    out, at = [], 0
    for a in like:
        out.append(buf[at:at + _seg_rows(a)].reshape(-1)[:a.size].reshape(a.shape))
        at += _seg_rows(a)
    return out


def kernel(x, w_in, w_out, sgu_w, sgu_b, pool_w, pool_scale, swa_sinks, rel_bias, mix_out_gain, norm_mix, norm_ffn, w_gate_up, w_down, norm_final, loss_target, m_w_in, m_w_out, m_sgu_w, m_sgu_b, m_pool_w, m_pool_scale, m_swa_sinks, m_rel_bias, m_mix_out_gain, m_norm_mix, m_norm_ffn, m_w_gate_up, m_w_down, m_norm_final, v_w_in, v_w_out, v_sgu_w, v_sgu_b, v_pool_w, v_pool_scale, v_swa_sinks, v_rel_bias, v_mix_out_gain, v_norm_mix, v_norm_ffn, v_w_gate_up, v_w_down, v_norm_final):
    bl, seq, _ = x.shape
    L = w_in.shape[0]
    gu_t, m_gu_t, v_gu_t = (jnp.swapaxes(a, 1, 2) for a in (w_gate_up, m_w_gate_up, v_w_gate_up))
    shards = (_cast_pad(w_in, D_MODEL, "shard_w_in"), _cast_pad(w_out, D_MODEL // N_DEV, "shard_w_out"),
              _cast_pad(gu_t, FF_PAD, "shard_w_gate_up"), _cast_pad(w_down, DOWN_ROWS, "shard_w_down"))
    first = _all_gather_weights(*[s[:1] for s in shards])
    w0 = _as_weights(*[f[0] for f in first])
    small_w = dict(sgu_w=sgu_w, sgu_b=sgu_b, pool_w=pool_w, pool_scale=pool_scale, swa_sinks=swa_sinks,
                   rel_bias=rel_bias, mix_out_gain=mix_out_gain, norm_mix=norm_mix, norm_ffn=norm_ffn,
                   norm_final=norm_final)
    small_m = dict(sgu_w=m_sgu_w, sgu_b=m_sgu_b, pool_w=m_pool_w, pool_scale=m_pool_scale, swa_sinks=m_swa_sinks,
                   rel_bias=m_rel_bias, mix_out_gain=m_mix_out_gain, norm_mix=m_norm_mix, norm_ffn=m_norm_ffn,
                   norm_final=m_norm_final)
    small_v = dict(sgu_w=v_sgu_w, sgu_b=v_sgu_b, pool_w=v_pool_w, pool_scale=v_pool_scale, swa_sinks=v_swa_sinks,
                   rel_bias=v_rel_bias, mix_out_gain=v_mix_out_gain, norm_mix=v_norm_mix, norm_ffn=v_norm_ffn,
                   norm_final=v_norm_final)
    loss, dx, big, small, g_rel_bias, g_final, upper = _local_step(
        x.reshape(bl * seq, D_MODEL), loss_target.reshape(bl * seq, D_MODEL), w0, shards, sgu_w, sgu_b, pool_w,
        pool_scale, swa_sinks, rel_bias, mix_out_gain, norm_mix, norm_ffn, norm_final, seq)
    p_in, p_out, p_gu, p_down = ([big[l][t] for l in range(L)] for t in range(4))
    outs_gu = [jnp.swapaxes(a, 1, 2) for a in
               _adamw_sharded(p_gu, gu_t, m_gu_t, v_gu_t, FF_SHARD // 4, "adamw_w_gate_up")]
    outs_down = _adamw_sharded(p_down, w_down, m_w_down, v_w_down, DOWN_ROWS // 2, "adamw_w_down")
    outs_in = _adamw_sharded(p_in, w_in, m_w_in, v_w_in, 256, "adamw_w_in")
    lo_like = _layer_zero(small_w)
    lo_rows = _pack_rows(lo_like + [loss[0]])
    lo_mine = _pack([small[0][k][None] for k in LAYERED] + [g_rel_bias, g_final, loss[0]], lo_rows)
    *outs_out, lower = _adamw_sharded(p_out, w_out, m_w_out, v_w_out, D_MODEL // N_DEV, "adamw_w_out",
                                      items=(_Item(lo_mine, lambda src, k: src, (N_DEV,) + lo_mine.shape),))
    lo_res = _adamw_small(lower, _pack(lo_like, lo_rows), _pack(_layer_zero(small_m), lo_rows),
                          _pack(_layer_zero(small_v), lo_rows), "adamw_small_layer0")
    hi_like = _upper_layers(small_w)
    hi_rows = _pack_rows(hi_like)
    hi_res = _adamw_small(upper, _pack(hi_like, hi_rows), _pack(_upper_layers(small_m), hi_rows),
                          _pack(_upper_layers(small_v), hi_rows), "adamw_small_upper")
    loss_total = lo_res[0][sum(_seg_rows(a) for a in lo_like), 0]
    small_outs = []
    for lo_buf, hi_buf in zip(lo_res, hi_res):
        lo = dict(zip(LAYERED + SHARED, _unpack(lo_buf, lo_like)))
        hi = dict(zip(LAYERED, _unpack(hi_buf, hi_like)))
        small_outs.append({k: jnp.concatenate([lo[k], hi[k]], axis=0) if k in hi else lo[k] for k in lo})
    big_outs = dict(w_in=outs_in, w_out=outs_out, w_gate_up=outs_gu, w_down=outs_down)
    order = ("w_in", "w_out", "sgu_w", "sgu_b", "pool_w", "pool_scale", "swa_sinks", "rel_bias", "mix_out_gain",
             "norm_mix", "norm_ffn", "w_gate_up", "w_down", "norm_final")
    result = [loss_total, dx.reshape(bl, seq, D_MODEL)]
    for which in range(4):
        for name in order:
            result.append(big_outs[name][which] if name in big_outs else small_outs[which][name])
    return tuple(result)
```

```python
import functools

import jax
import jax.numpy as jnp
from jax import lax
from jax.experimental import pallas as pl
from jax.experimental.pallas import tpu as pltpu

F32 = jnp.float32
BF16 = jnp.bfloat16

N_DEV = 8
DEPTH = 4
D_MODEL = 1024
GROUP_WIDTH = 256
HEAD_DIM = 64
GROUP_HEADS = 4
BLOCK = 128
N_BUCKETS = 32
MAX_DISTANCE = 128
POOL_WINDOWS = (2, 4, 8, 16)
D_FF = 2816
FF_SHARD = D_FF // 4
FF_PAD = 768
D_FF_PAD = 4 * FF_PAD
EPS = 1e-6
ATT_SCALE = HEAD_DIM ** -0.5
ADAM_LR = 0.001
ADAM_B1 = 0.9
ADAM_B2 = 0.999
ADAM_EPS = 1e-08
ADAM_WD = 0.01
ADAM_STEP = 10
VMEM_LIMIT = 56 * 1024 * 1024
MESH_AXES = ("x", "y", "c")


def _cp(*sem):
    return pltpu.CompilerParams(dimension_semantics=sem or None, vmem_limit_bytes=VMEM_LIMIT)


_NT = (((1,), (1,)), ((), ()))
_TN = (((0,), (0,)), ((), ()))


@jax.custom_vjp
def _bdot(a, b):
    return jnp.dot(a.astype(BF16), b.astype(BF16), preferred_element_type=F32)


def _bdot_fwd(a, b):
    return _bdot(a, b), (a.astype(BF16), b.astype(BF16))


def _bdot_bwd(res, ct):
    a, b = res
    c = ct.astype(BF16)
    return (lax.dot_general(c, b, _NT, preferred_element_type=F32),
            lax.dot_general(a, c, _TN, preferred_element_type=F32))


_bdot.defvjp(_bdot_fwd, _bdot_bwd)


@jax.custom_vjp
def _bdot_nt(a, b):
    return lax.dot_general(a.astype(BF16), b.astype(BF16), _NT, preferred_element_type=F32)


def _bdot_nt_fwd(a, b):
    return _bdot_nt(a, b), (a.astype(BF16), b.astype(BF16))


def _bdot_nt_bwd(res, ct):
    a, b = res
    c = ct.astype(BF16)
    return (jnp.dot(c, b, preferred_element_type=F32),
            lax.dot_general(c, a, _TN, preferred_element_type=F32))


_bdot_nt.defvjp(_bdot_nt_fwd, _bdot_nt_bwd)


def _rms(x, g):
    return x * lax.rsqrt(jnp.mean(x * x, axis=-1, keepdims=True) + EPS) * g


def _sigmoid(x):
    return 0.5 * jnp.tanh(0.5 * x) + 0.5


def _split_dot(x, u):
    hi = x.astype(BF16)
    lo = (x - hi.astype(F32)).astype(BF16)
    return jnp.dot(hi, u, preferred_element_type=F32) + jnp.dot(lo, u, preferred_element_type=F32)


def _head_mask(h, shape):
    col = lax.broadcasted_iota(jnp.int32, shape, 1)
    return (col >= h * HEAD_DIM) & (col < (h + 1) * HEAD_DIM)


ANY = pl.BlockSpec(memory_space=pl.ANY)
MESH = pl.DeviceIdType.MESH
DOWN_ROWS = D_FF // N_DEV


def _my_place():
    x, y, c = (lax.axis_index(a) for a in MESH_AXES)
    return x, y, c, 4 * x + 2 * y + c


def _peer(x, y, c, d):
    return (x ^ (d >> 2), y ^ ((d >> 1) & 1), c ^ (d & 1))


class _Item:
    def __init__(self, src, block, dst_shape, place=None, init=None):
        self.src, self.block, self.dst_shape, self.init = src, block, dst_shape, init
        self.place = place or (lambda dst, k: dst.at[k])


def _call(body, args, *, grid, in_specs, out_specs, out_shape, sem, name, scratch_shapes=(), items=()):
    if not items:
        outs = pl.pallas_call(body, grid=grid, in_specs=in_specs, out_specs=out_specs, out_shape=out_shape, name=name,
                              scratch_shapes=list(scratch_shapes), compiler_params=_cp(*sem))(*args)
        return list(outs) if isinstance(outs, (list, tuple)) else [outs]
    n_in, n_out, n_scr, n = len(in_specs), len(out_specs), len(scratch_shapes), len(items)
    inits = [i for i, it in enumerate(items) if it.init is not None]

    def wrapped(*refs):
        core_in, srcs = refs[:n_in], refs[n_in:n_in + n]
        off = n_in + n + len(inits)
        core_out, dsts = refs[off:off + n_out], refs[off + n_out:off + n_out + n]
        scratch = refs[off + n_out + n:]
        send_sems, recv_sems, local_sems = scratch[n_scr:]
        ids = [pl.program_id(a) for a in range(len(grid))]
        first = functools.reduce(jnp.logical_and, [i == 0 for i in ids])
        last = functools.reduce(jnp.logical_and, [i == g - 1 for i, g in zip(ids, grid)])
        x, y, c, me = _my_place()

        def local(i):
            return pltpu.make_async_copy(items[i].block(srcs[i], me), items[i].place(dsts[i], me), local_sems.at[i])

        def remote(d, i, sending):
            px, py, pc = _peer(x, y, c, d)
            pk = 4 * px + 2 * py + pc
            return pltpu.make_async_remote_copy(
                src_ref=items[i].block(srcs[i], pk), dst_ref=items[i].place(dsts[i], me if sending else pk),
                send_sem=send_sems.at[d - 1, i], recv_sem=recv_sems.at[d - 1, i],
                device_id=(px, py, pc), device_id_type=MESH)

        @pl.when(first)
        def _():
            for i in range(n):
                local(i).start()
            for d in range(1, N_DEV):
                for i in range(n):
                    remote(d, i, True).start()

        body(*core_in, *core_out, *scratch[:n_scr])

        @pl.when(last)
        def _():
            for d in range(1, N_DEV):
                for i in range(n):
                    remote(d, i, False).wait_recv()
            for d in range(1, N_DEV):
                for i in range(n):
                    remote(d, i, True).wait_send()
            for i in range(n):
                local(i).wait()

    outs = pl.pallas_call(
        wrapped, grid=grid, name=name,
        in_specs=list(in_specs) + [ANY] * (n + len(inits)), out_specs=list(out_specs) + [ANY] * n,
        out_shape=list(out_shape) + [jax.ShapeDtypeStruct(it.dst_shape, it.src.dtype) for it in items],
        input_output_aliases={n_in + n + j: n_out + i for j, i in enumerate(inits)},
        scratch_shapes=list(scratch_shapes) + [pltpu.SemaphoreType.DMA((N_DEV - 1, n)),
                                               pltpu.SemaphoreType.DMA((N_DEV - 1, n)), pltpu.SemaphoreType.DMA((n,))],
        compiler_params=_cp(*(["arbitrary"] * len(grid))),
    )(*args, *[it.src for it in items], *[items[i].init for i in inits])
    return list(outs)


def _norm_mm(x, g, w, l, jb, out_dtype, name, tm=1024, items=()):
    T, K = x.shape
    _, nb, _, tn = w.shape
    tm = min(tm, T)

    def body(x_ref, g_ref, w_ref, o_ref, h_ref):
        @pl.when(pl.program_id(1) == 0)
        def _():
            h_ref[...] = _rms(x_ref[...], g_ref[...]).astype(BF16)
        h = h_ref[...]
        for jj in range(jb):
            o_ref[:, jj * tn:(jj + 1) * tn] = jnp.dot(h, w_ref[jj], preferred_element_type=F32).astype(o_ref.dtype)

    return _call(
        body, (x, g, w), grid=(T // tm, nb // jb), name=name, items=items,
        in_specs=[pl.BlockSpec((tm, K), lambda i, j: (i, 0)), pl.BlockSpec((1, K), lambda i, j: (0, 0)),
                  pl.BlockSpec((None, jb, K, tn), lambda i, j: (l, j, 0, 0))],
        out_specs=[pl.BlockSpec((tm, jb * tn), lambda i, j: (i, j)), pl.BlockSpec((tm, K), lambda i, j: (i, 0))],
        out_shape=[jax.ShapeDtypeStruct((T, nb * tn), out_dtype), jax.ShapeDtypeStruct((T, K), BF16)],
        sem=("parallel", "arbitrary"))


def _mm_res(res, a, w, l, name, tn, tm=1024, items=()):
    T, K = a.shape
    N = w.shape[2]
    tm = min(tm, T)

    def body(r_ref, a_ref, w_ref, o_ref):
        o_ref[...] = r_ref[...] + jnp.dot(a_ref[...], w_ref[...], preferred_element_type=F32)

    return _call(
        body, (res, a, w), grid=(T // tm, N // tn), name=name, items=items,
        in_specs=[pl.BlockSpec((tm, tn), lambda i, j: (i, j)), pl.BlockSpec((tm, K), lambda i, j: (i, 0)),
                  pl.BlockSpec((None, K, tn), lambda i, j: (l, 0, j))],
        out_specs=[pl.BlockSpec((tm, tn), lambda i, j: (i, j))],
        out_shape=[jax.ShapeDtypeStruct((T, N), F32)],
        sem=("parallel", "parallel"))


def _mm_nt(a, w, l, out_dtype, name, tm=1024, tn=1024):
    T, K = a.shape
    N = w.shape[1]
    tm = min(tm, T)

    def body(a_ref, w_ref, o_ref):
        o_ref[...] = lax.dot_general(a_ref[...].astype(BF16), w_ref[...], _NT,
                                     preferred_element_type=F32).astype(o_ref.dtype)

    return pl.pallas_call(
        body, grid=(T // tm, N // tn), name=name,
        in_specs=[pl.BlockSpec((tm, K), lambda i, j: (i, 0)), pl.BlockSpec((None, tn, K), lambda i, j: (l, j, 0))],
        out_specs=pl.BlockSpec((tm, tn), lambda i, j: (i, j)),
        out_shape=jax.ShapeDtypeStruct((T, N), out_dtype),
        compiler_params=_cp("parallel", "parallel"))(a, w)


def _mm_norm_bwd(a, w, l, jb, x, g, dres, name, tm=1024, items=(), block_of=lambda j: j, transposed=False):
    T = a.shape[0]
    _, nb, K, tn = w.shape
    if transposed:
        tn, K = K, tn
    tm = min(tm, T)
    nj = nb // jb
    sub = min(256, tm)
    dims = (((1,), (0,)), ((), ())) if transposed else _NT

    def body(a_ref, w_ref, x_ref, g_ref, r_ref, o_ref, dg_ref):
        part = lax.dot_general(a_ref[:, :tn], w_ref[0], dims, preferred_element_type=F32)
        for jj in range(1, jb):
            part += lax.dot_general(a_ref[:, jj * tn:(jj + 1) * tn], w_ref[jj], dims, preferred_element_type=F32)

        @pl.when(pl.program_id(1) == 0)
        def _():
            o_ref[...] = part

        @pl.when(pl.program_id(1) > 0)
        def _():
            o_ref[...] += part

        @pl.when(pl.program_id(1) == nj - 1)
        def _():
            dg = jnp.zeros((1, K), F32)
            for r in range(tm // sub):
                rows = pl.ds(r * sub, sub)
                _, vjp = jax.vjp(_rms, x_ref[rows, :], g_ref[...])
                dx, dg_r = vjp(o_ref[rows, :])
                o_ref[rows, :] = r_ref[rows, :] + dx
                dg = dg + dg_r
            _acc_out(pl.program_id(0) == 0, (dg_ref,), (dg,))

    row = pl.BlockSpec((tm, K), lambda i, j: (i, 0))
    vec = pl.BlockSpec((1, K), lambda i, j: (0, 0))
    return _call(
        body, (a, w, x, g, dres), grid=(T // tm, nj), name=name, items=items,
        in_specs=[pl.BlockSpec((tm, jb * tn), lambda i, j: (i, j)),
                  pl.BlockSpec((None, jb) + w.shape[2:], lambda i, j: (l, block_of(j), 0, 0)), row, vec, row],
        out_specs=[row, vec],
        out_shape=[jax.ShapeDtypeStruct((T, K), F32), jax.ShapeDtypeStruct((1, K), F32)],
        sem=("arbitrary", "arbitrary"))


def _mm_tn_cols(lhs, rhs, tn, jb, name, tm=1024, items=(), block_of=lambda j: j):
    T, K = lhs.shape
    nb = rhs.shape[1] // tn
    tm = min(tm, T)
    nt = T // tm

    def body(l_ref, r_ref, o_ref, acc):
        part = lax.dot_general(l_ref[...], r_ref[...], _TN, preferred_element_type=F32)

        @pl.when(pl.program_id(1) == 0)
        def _():
            acc[...] = part

        @pl.when(pl.program_id(1) > 0)
        def _():
            acc[...] += part

        @pl.when(pl.program_id(1) == nt - 1)
        def _():
            for jj in range(jb):
                o_ref[jj] = acc[:, jj * tn:(jj + 1) * tn].astype(BF16)

    return _call(
        body, (lhs, rhs), grid=(nb // jb, nt), name=name, items=items,
        in_specs=[pl.BlockSpec((tm, K), lambda j, t: (t, 0)), pl.BlockSpec((tm, jb * tn), lambda j, t: (t, j))],
        out_specs=[pl.BlockSpec((jb, K, tn), lambda j, t: (block_of(j), 0, 0))],
        out_shape=[jax.ShapeDtypeStruct((nb, K, tn), BF16)],
        scratch_shapes=[pltpu.VMEM((K, jb * tn), F32)],
        sem=("parallel", "arbitrary"))


def _mm_tn_rows(lhs, rhs, tk, name, tm=1024, items=(), block_of=lambda j: j):
    T, Kl = lhs.shape
    N = rhs.shape[1]
    tm = min(tm, T)
    nt = T // tm

    def body(l_ref, r_ref, o_ref, acc):
        part = lax.dot_general(l_ref[...], r_ref[...].astype(BF16), _TN, preferred_element_type=F32)

        @pl.when(pl.program_id(1) == 0)
        def _():
            acc[...] = part

        @pl.when(pl.program_id(1) > 0)
        def _():
            acc[...] += part

        @pl.when(pl.program_id(1) == nt - 1)
        def _():
            o_ref[...] = acc[...].astype(BF16)

    return _call(
        body, (lhs, rhs), grid=(Kl // tk, nt), name=name, items=items,
        in_specs=[pl.BlockSpec((tm, tk), lambda l, t: (t, l)), pl.BlockSpec((tm, N), lambda l, t: (t, 0))],
        out_specs=[pl.BlockSpec((tk, N), lambda l, t: (block_of(l), 0))],
        out_shape=[jax.ShapeDtypeStruct((Kl, N), BF16)],
        scratch_shapes=[pltpu.VMEM((tk, N), F32)],
        sem=("parallel", "arbitrary"))


N_FF_CHUNK = D_FF_PAD // FF_PAD


def _ffn_chunk_block(j):
    return (j % 2) * N_FF_CHUNK + j // 2


def _ffn_up_act(x, g, w, name, tm=1024, items=()):
    T, K = x.shape
    tm = min(tm, T)

    def body(x_ref, g_ref, wg_ref, wu_ref, gu_ref, act_ref, h_ref):
        @pl.when(pl.program_id(1) == 0)
        def _():
            h_ref[...] = _rms(x_ref[...], g_ref[...]).astype(BF16)
        h = h_ref[...]
        gate = lax.dot_general(h, wg_ref[...], _NT, preferred_element_type=F32)
        up = lax.dot_general(h, wu_ref[...], _NT, preferred_element_type=F32)
        gu_ref[:, :FF_PAD] = gate.astype(BF16)
        gu_ref[:, FF_PAD:] = up.astype(BF16)
        act_ref[...] = (gate * _sigmoid(gate) * up).astype(BF16)

    return _call(
        body, (x, g, w, w), grid=(T // tm, N_FF_CHUNK), name=name, items=items,
        in_specs=[pl.BlockSpec((tm, K), lambda i, j: (i, 0)), pl.BlockSpec((1, K), lambda i, j: (0, 0)),
                  pl.BlockSpec((None, None, FF_PAD, K), lambda i, j: (0, j, 0, 0)),
                  pl.BlockSpec((None, None, FF_PAD, K), lambda i, j: (0, j + N_FF_CHUNK, 0, 0))],
        out_specs=[pl.BlockSpec((tm, 2 * FF_PAD), lambda i, j: (i, j)), pl.BlockSpec((tm, FF_PAD), lambda i, j: (i, j)),
                   pl.BlockSpec((tm, K), lambda i, j: (i, 0))],
        out_shape=[jax.ShapeDtypeStruct((T, 2 * D_FF_PAD), BF16), jax.ShapeDtypeStruct((T, D_FF_PAD), BF16),
                   jax.ShapeDtypeStruct((T, K), BF16)],
        sem=("parallel", "arbitrary"))


def _ffn_down_dx(dxo, w, gu, name, tm=1024):
    T, K = dxo.shape
    tm = min(tm, T)

    sub = min(512, tm)

    def body(d_ref, w_ref, gu_ref, o_ref):
        w = w_ref[...]
        for r in range(tm // sub):
            rows = slice(r * sub, (r + 1) * sub)
            d = lax.dot_general(d_ref[rows, :].astype(BF16), w, _NT, preferred_element_type=F32)
            gate = gu_ref[rows, :FF_PAD].astype(F32)
            up = gu_ref[rows, FF_PAD:].astype(F32)
            sig = _sigmoid(gate)
            silu = gate * sig
            o_ref[rows, :FF_PAD] = (d * up * (sig + silu * (1.0 - sig))).astype(BF16)
            o_ref[rows, FF_PAD:] = (d * silu).astype(BF16)

    return pl.pallas_call(
        body, grid=(T // tm, N_FF_CHUNK), name=name,
        in_specs=[pl.BlockSpec((tm, K), lambda i, j: (i, 0)), pl.BlockSpec((None, FF_PAD, K), lambda i, j: (0, j, 0)),
                  pl.BlockSpec((tm, 2 * FF_PAD), lambda i, j: (i, j))],
        out_specs=pl.BlockSpec((tm, 2 * FF_PAD), lambda i, j: (i, j)),
        out_shape=jax.ShapeDtypeStruct((T, 2 * D_FF_PAD), BF16),
        compiler_params=_cp("parallel", "parallel"))(dxo, w, gu)


def _loss_head(x, g, target, name, tm=512):
    T, K = x.shape

    def loss_fn(xv, gv, tv):
        err = _rms(xv, gv) - tv
        return 0.5 * jnp.sum(jnp.mean(err * err, axis=-1, keepdims=True), axis=0, keepdims=True)

    def body(x_ref, g_ref, t_ref, l_ref, dx_ref, dg_ref):
        val, vjp = jax.vjp(lambda xv, gv: loss_fn(xv, gv, t_ref[...]), x_ref[...], g_ref[...])
        dx, dg = vjp(jnp.ones((1, 1), F32))
        dx_ref[...] = dx
        lval = jnp.broadcast_to(val, (1, 128))

        @pl.when(pl.program_id(0) == 0)
        def _():
            dg_ref[...] = dg
            l_ref[...] = lval

        @pl.when(pl.program_id(0) > 0)
        def _():
            dg_ref[...] += dg
            l_ref[...] += lval

    row = pl.BlockSpec((tm, K), lambda i: (i, 0))
    vec = pl.BlockSpec((1, K), lambda i: (0, 0))
    return pl.pallas_call(
        body, grid=(T // tm,), name=name,
        in_specs=[row, vec, row], out_specs=[pl.BlockSpec((1, 128), lambda i: (0, 0)), row, vec],
        out_shape=[jax.ShapeDtypeStruct((1, 128), F32), jax.ShapeDtypeStruct((T, K), F32),
                   jax.ShapeDtypeStruct((1, K), F32)],
        compiler_params=_cp("arbitrary"))(x, g, target)


SGU_ROWS = 4 * BLOCK


def _sgu_fn(u, v, wm, bt, gain):
    ug = jax.nn.gelu(u)
    vg = jax.nn.gelu(v)
    row = lax.broadcasted_iota(jnp.int32, (BLOCK, BLOCK), 0)
    col = lax.broadcasted_iota(jnp.int32, (BLOCK, BLOCK), 1)
    tri = row >= col
    normed = []
    for h in range(GROUP_HEADS):
        vh = vg[:, h * HEAD_DIM:(h + 1) * HEAD_DIM]
        xc = vh - jnp.mean(vh, axis=-1, keepdims=True)
        normed.append(xc * lax.rsqrt(jnp.mean(xc * xc, axis=-1, keepdims=True) + EPS))
    vn = jnp.concatenate(normed, axis=1)
    wcat = jnp.concatenate([jnp.where(tri, wm[h], 0.0) for h in range(GROUP_HEADS)], axis=1)
    bias = jnp.concatenate([jnp.broadcast_to(bt[:, h:h + 1], (BLOCK, HEAD_DIM)) for h in range(GROUP_HEADS)], axis=1)
    mixes = []
    for c in range(u.shape[0] // BLOCK):
        chunk = vn[c * BLOCK:(c + 1) * BLOCK]
        stacked = jnp.concatenate([jnp.where(_head_mask(h, chunk.shape), chunk, 0.0) for h in range(GROUP_HEADS)], axis=0)
        mixes.append(_bdot(wcat, stacked) + bias)
    return _rms(ug * jnp.concatenate(mixes, axis=0), gain)


def _sgu_fwd(proj, wm, bt, gain, name, items=()):
    T = proj.shape[0]
    rows = min(SGU_ROWS, T)

    def body(u_ref, v_ref, w_ref, b_ref, g_ref, o_ref):
        o_ref[...] = _sgu_fn(u_ref[...], v_ref[...], w_ref[...], b_ref[...], g_ref[...]).astype(BF16)

    full = lambda shape: pl.BlockSpec(shape, lambda i: (0,) * len(shape))
    return _call(
        body, (proj, proj, wm, bt, gain), grid=(T // rows,), name=name, items=items,
        in_specs=[pl.BlockSpec((rows, GROUP_WIDTH), lambda i: (i, 0)), pl.BlockSpec((rows, GROUP_WIDTH), lambda i: (i, 1)),
                  full(wm.shape), full(bt.shape), full(gain.shape)],
        out_specs=[pl.BlockSpec((rows, GROUP_WIDTH), lambda i: (i, 0))],
        out_shape=[jax.ShapeDtypeStruct((T, GROUP_WIDTH), BF16)],
        sem=("parallel",))


def _acc_out(first, refs, vals):
    @pl.when(first)
    def _():
        for r, v in zip(refs, vals):
            r[...] = v

    @pl.when(jnp.logical_not(first))
    def _():
        for r, v in zip(refs, vals):
            r[...] += v


def _sgu_bwd(proj, wm, bt, gain, dy, name, items=()):
    T = proj.shape[0]

    def body(u_ref, v_ref, w_ref, b_ref, g_ref, dy_ref, duv_ref, dw_ref, db_ref, dg_ref):
        _, vjp = jax.vjp(_sgu_fn, u_ref[...], v_ref[...], w_ref[...], b_ref[...], g_ref[...])
        du, dv, dw, db, dg = vjp(dy_ref[...])
        duv_ref[:, :GROUP_WIDTH] = du.astype(BF16)
        duv_ref[:, GROUP_WIDTH:] = dv.astype(BF16)
        _acc_out(pl.program_id(0) == 0, (dw_ref, db_ref, dg_ref), (dw, db, dg))

    full = lambda shape: pl.BlockSpec(shape, lambda i: (0,) * len(shape))
    rows = min(SGU_ROWS, T)
    return _call(
        body, (proj, proj, wm, bt, gain, dy), grid=(T // rows,), name=name, items=items,
        in_specs=[pl.BlockSpec((rows, GROUP_WIDTH), lambda i: (i, 0)), pl.BlockSpec((rows, GROUP_WIDTH), lambda i: (i, 1)),
                  full(wm.shape), full(bt.shape), full(gain.shape),
                  pl.BlockSpec((rows, GROUP_WIDTH), lambda i: (i, 0))],
        out_specs=[pl.BlockSpec((rows, 2 * GROUP_WIDTH), lambda i: (i, 0)), full(wm.shape), full(bt.shape), full(gain.shape)],
        out_shape=[jax.ShapeDtypeStruct((T, 2 * GROUP_WIDTH), BF16), jax.ShapeDtypeStruct(wm.shape, F32),
                   jax.ShapeDtypeStruct(bt.shape, F32), jax.ShapeDtypeStruct(gain.shape, F32)],
        sem=("arbitrary",))


def _pool_consts(seq):
    t = lax.broadcasted_iota(jnp.int32, (seq, GROUP_WIDTH), 0)
    grp = lax.broadcasted_iota(jnp.int32, (seq, GROUP_WIDTH), 1) // (GROUP_WIDTH // len(POOL_WINDOWS))
    win = jnp.where(grp == 0, POOL_WINDOWS[0], jnp.where(grp == 1, POOL_WINDOWS[1],
                    jnp.where(grp == 2, POOL_WINDOWS[2], POOL_WINDOWS[3])))
    count = jnp.minimum(t + 1, win).astype(F32)
    return t, grp, count


def _by_group(grp, vals):
    return jnp.where(grp == 0, vals[0], jnp.where(grp == 1, vals[1], jnp.where(grp == 2, vals[2], vals[3])))


def _window_sums(x, t, seq, back):
    def shift(a, k):
        if back:
            return jnp.where(t >= k, pltpu.roll(a, k, 0), 0.0)
        return jnp.where(t < seq - k, pltpu.roll(a, seq - k, 0), 0.0)
    sums = []
    a, k = x, 1
    for _ in POOL_WINDOWS:
        a = a + shift(a, k)
        sums.append(a)
        k *= 2
    return sums


def _pool_tail(y, wbd, scale, gain):
    return _rms(_bdot(y, wbd) * scale, gain)


def _pool_fwd(proj, wbd, scale, gain, seq, name):
    T = proj.shape[0]

    def body(p_ref, w_ref, s_ref, g_ref, o_ref):
        p = p_ref[...]
        t, grp, count = _pool_consts(seq)
        y = _by_group(grp, _window_sums(p, t, seq, True)) / count - p
        o_ref[...] = _pool_tail(y, w_ref[...], s_ref[...], g_ref[...]).astype(BF16)

    full = lambda shape: pl.BlockSpec(shape, lambda b: (0,) * len(shape))
    return pl.pallas_call(
        body, grid=(T // seq,), name=name,
        in_specs=[pl.BlockSpec((seq, GROUP_WIDTH), lambda b: (b, 2)), full(wbd.shape), full(scale.shape), full(gain.shape)],
        out_specs=pl.BlockSpec((seq, GROUP_WIDTH), lambda b: (b, 0)),
        out_shape=jax.ShapeDtypeStruct((T, GROUP_WIDTH), BF16),
        compiler_params=_cp("parallel"))(proj, wbd, scale, gain)


def _pool_bwd(proj, wbd, scale, gain, dy, seq, name):
    T = proj.shape[0]

    def body(p_ref, w_ref, s_ref, g_ref, dy_ref, dp_ref, dw_ref, ds_ref, dg_ref):
        p = p_ref[...]
        t, grp, count = _pool_consts(seq)
        y = _by_group(grp, _window_sums(p, t, seq, True)) / count - p
        _, vjp = jax.vjp(_pool_tail, y, w_ref[...], s_ref[...], g_ref[...])
        d_y, dw, ds, dg = vjp(dy_ref[...])
        dp = _by_group(grp, _window_sums(d_y / count, t, seq, False)) - d_y
        dp_ref[...] = dp.astype(BF16)
        _acc_out(pl.program_id(0) == 0, (dw_ref, ds_ref, dg_ref), (dw, ds, dg))

    full = lambda shape: pl.BlockSpec(shape, lambda b: (0,) * len(shape))
    return pl.pallas_call(
        body, grid=(T // seq,), name=name,
        in_specs=[pl.BlockSpec((seq, GROUP_WIDTH), lambda b: (b, 2)), full(wbd.shape), full(scale.shape), full(gain.shape),
                  pl.BlockSpec((seq, GROUP_WIDTH), lambda b: (b, 1))],
        out_specs=[pl.BlockSpec((seq, GROUP_WIDTH), lambda b: (b, 0)), full(wbd.shape), full(scale.shape), full(gain.shape)],
        out_shape=[jax.ShapeDtypeStruct((T, GROUP_WIDTH), BF16), jax.ShapeDtypeStruct(wbd.shape, F32),
                   jax.ShapeDtypeStruct(scale.shape, F32), jax.ShapeDtypeStruct(gain.shape, F32)],
        compiler_params=_cp("arbitrary"))(proj, wbd, scale, gain, dy)


def _swa_fn(q, kv_prev, kv_cur, sinks, tab, gain, first):
    half = GROUP_WIDTH // 2
    k2 = jnp.concatenate([kv_prev[:, :half], kv_cur[:, :half]], axis=0)
    v2 = jnp.concatenate([kv_prev[:, half:], kv_cur[:, half:]], axis=0)
    per_query_head = lambda a: jnp.concatenate(
        [a[:, (h // 2) * HEAD_DIM:(h // 2 + 1) * HEAD_DIM] for h in range(GROUP_HEADS)], axis=1)
    qs = jnp.concatenate([jnp.where(_head_mask(h, q.shape), q, 0.0) for h in range(GROUP_HEADS)], axis=0)
    qi = lax.broadcasted_iota(jnp.int32, (HEAD_ROWS, 2 * BLOCK), 0) & (BLOCK - 1)
    kj = lax.broadcasted_iota(jnp.int32, (HEAD_ROWS, 2 * BLOCK), 1)
    dist = qi + BLOCK - kj
    mask = (dist >= 0) & (dist < BLOCK) & ((kj >= BLOCK) | jnp.logical_not(first))
    logits = _bdot_nt(qs, per_query_head(k2)) * ATT_SCALE + tab.reshape(HEAD_ROWS, 2 * BLOCK)
    logits = jnp.where(mask, logits, -1e30)
    sink = jnp.concatenate([jnp.broadcast_to(sinks[:, h:h + 1], (BLOCK, 1)) for h in range(GROUP_HEADS)], axis=0)
    m = lax.stop_gradient(jnp.maximum(jnp.max(logits, axis=1, keepdims=True), sink))
    p = jnp.exp(logits - m)
    probs = p / (jnp.sum(p, axis=1, keepdims=True) + jnp.exp(sink - m))
    out = _bdot(probs, per_query_head(v2))
    y = jnp.zeros_like(q)
    for h in range(GROUP_HEADS):
        y = y + jnp.where(_head_mask(h, q.shape), out[h * BLOCK:(h + 1) * BLOCK], 0.0)
    return _rms(y, gain)


def _swa_specs(nblk):
    q = pl.BlockSpec((BLOCK, GROUP_WIDTH), lambda b, i: (b * nblk + i, 3))
    cur = pl.BlockSpec((BLOCK, GROUP_WIDTH), lambda b, i: (b * nblk + i, 4))
    prev = pl.BlockSpec((BLOCK, GROUP_WIDTH), lambda b, i: (b * nblk + jnp.maximum(i - 1, 0), 4))
    return q, prev, cur


def _swa_fwd(proj, sinks, tab, gain, seq, name, items=()):
    T = proj.shape[0]
    nblk = seq // BLOCK

    def body(q_ref, kp_ref, kc_ref, s_ref, t_ref, g_ref, o_ref):
        o_ref[...] = _swa_fn(q_ref[...], kp_ref[...], kc_ref[...], s_ref[...], t_ref[...], g_ref[...],
                             pl.program_id(1) == 0).astype(BF16)

    full = lambda shape: pl.BlockSpec(shape, lambda b, i: (0,) * len(shape))
    return _call(
        body, (proj, proj, proj, sinks, tab, gain), grid=(T // seq, nblk), name=name, items=items,
        in_specs=[*_swa_specs(nblk), full(sinks.shape), full(tab.shape), full(gain.shape)],
        out_specs=[pl.BlockSpec((BLOCK, GROUP_WIDTH), lambda b, i: (b * nblk + i, 0))],
        out_shape=[jax.ShapeDtypeStruct((T, GROUP_WIDTH), BF16)],
        sem=("parallel", "parallel"))


def _swa_bwd(proj, sinks, tab, gain, dy, seq, name, items=()):
    T = proj.shape[0]
    nblk = seq // BLOCK

    def body(q_ref, kp_ref, kc_ref, s_ref, t_ref, g_ref, dy_ref, dq_ref, dkp_ref, dkc_ref, ds_ref, dt_ref, dg_ref):
        first = pl.program_id(1) == 0
        fn = functools.partial(_swa_fn, first=first)
        _, vjp = jax.vjp(fn, q_ref[...], kp_ref[...], kc_ref[...], s_ref[...], t_ref[...], g_ref[...])
        dq, dkp, dkc, ds, dt, dg = vjp(dy_ref[...])
        dq_ref[...] = dq.astype(BF16)
        dkp_ref[...] = dkp
        dkc_ref[...] = dkc
        _acc_out((pl.program_id(0) == 0) & first, (ds_ref, dt_ref, dg_ref), (ds, dt, dg))

    full = lambda shape: pl.BlockSpec(shape, lambda b, i: (0,) * len(shape))
    blk = lambda c: pl.BlockSpec((BLOCK, GROUP_WIDTH), lambda b, i: (b * nblk + i, c))
    return _call(
        body, (proj, proj, proj, sinks, tab, gain, dy), grid=(T // seq, nblk), name=name, items=items,
        in_specs=[*_swa_specs(nblk), full(sinks.shape), full(tab.shape), full(gain.shape), blk(2)],
        out_specs=[blk(0), blk(0), blk(0), full(sinks.shape), full(tab.shape), full(gain.shape)],
        out_shape=[jax.ShapeDtypeStruct((T, GROUP_WIDTH), BF16), jax.ShapeDtypeStruct((T, GROUP_WIDTH), F32),
                   jax.ShapeDtypeStruct((T, GROUP_WIDTH), F32), jax.ShapeDtypeStruct(sinks.shape, F32),
                   jax.ShapeDtypeStruct(tab.shape, F32), jax.ShapeDtypeStruct(gain.shape, F32)],
        sem=("arbitrary", "arbitrary"))


def _shift_add(cur, prev, seq, name):
    T = cur.shape[0]
    nblk = seq // BLOCK

    def body(c_ref, p_ref, o_ref):
        last = pl.program_id(1) == nblk - 1
        o_ref[...] = (c_ref[...] + jnp.where(last, 0.0, p_ref[...])).astype(BF16)

    return pl.pallas_call(
        body, grid=(T // seq, nblk), name=name,
        in_specs=[pl.BlockSpec((BLOCK, GROUP_WIDTH), lambda b, i: (b * nblk + i, 0)),
                  pl.BlockSpec((BLOCK, GROUP_WIDTH), lambda b, i: (b * nblk + jnp.minimum(i + 1, nblk - 1), 0))],
        out_specs=pl.BlockSpec((BLOCK, GROUP_WIDTH), lambda b, i: (b * nblk + i, 0)),
        out_shape=jax.ShapeDtypeStruct((T, GROUP_WIDTH), BF16),
        compiler_params=_cp("parallel", "parallel"))(cur, prev)


def _t5_bucket(dist):
    max_exact = N_BUCKETS // 2
    df = jnp.maximum(dist, 1).astype(F32)
    large = max_exact + (jnp.log(df / max_exact) / jnp.log(jnp.float32(MAX_DISTANCE / max_exact))
                         * (N_BUCKETS - max_exact)).astype(jnp.int32)
    return jnp.where(dist < max_exact, dist, jnp.minimum(large, N_BUCKETS - 1))


def _bucket_map():
    dist = (jnp.arange(BLOCK)[:, None] + BLOCK) - jnp.arange(2 * BLOCK)[None, :]
    return _t5_bucket(jnp.clip(dist, 0, BLOCK - 1)).astype(jnp.int32)


def _bias_table(rel_bias, buckets, name):
    def body(rb_ref, bk_ref, o_ref):
        bk = bk_ref[...]
        rb = rb_ref[...]
        for h in range(GROUP_HEADS):
            acc = jnp.zeros((BLOCK, 2 * BLOCK), F32)
            for b in range(N_BUCKETS):
                acc = jnp.where(bk == b, rb[b:b + 1, h:h + 1], acc)
            o_ref[h] = acc

    return pl.pallas_call(body, name=name, out_shape=jax.ShapeDtypeStruct((GROUP_HEADS, BLOCK, 2 * BLOCK), F32),
                          compiler_params=_cp())(rel_bias, buckets)


def _bias_table_bwd(dtab, buckets, name):
    def body(dt_ref, bk_ref, o_ref):
        bk = bk_ref[...]
        row = lax.broadcasted_iota(jnp.int32, (N_BUCKETS, GROUP_HEADS), 0)
        col = lax.broadcasted_iota(jnp.int32, (N_BUCKETS, GROUP_HEADS), 1)
        acc = jnp.zeros((N_BUCKETS, GROUP_HEADS), F32)
        for h in range(GROUP_HEADS):
            dt = dt_ref[h]
            for b in range(N_BUCKETS):
                s = jnp.sum(jnp.where(bk == b, dt, 0.0), keepdims=True)
                acc = acc + jnp.where((row == b) & (col == h), s, 0.0)
        o_ref[...] = acc

    return pl.pallas_call(body, name=name, out_shape=jax.ShapeDtypeStruct((N_BUCKETS, GROUP_HEADS), F32),
                          compiler_params=_cp())(dtab, buckets)


HEAD_ROWS = GROUP_HEADS * BLOCK


def _stack_heads(x):
    return jnp.concatenate([jnp.where(_head_mask(h, x.shape), x, 0.0) for h in range(GROUP_HEADS)], axis=0).astype(BF16)


def _sb_tile(qs, kb, q0, k0):
    z = lax.dot_general(qs, kb, _NT, preferred_element_type=F32)
    row = lax.broadcasted_iota(jnp.int32, (HEAD_ROWS, BLOCK), 0) & (BLOCK - 1)
    col = lax.broadcasted_iota(jnp.int32, (HEAD_ROWS, BLOCK), 1)
    causal = (k0 + col) < (q0 + row)
    ls_neg = -(jnp.maximum(z, 0.0) + jnp.log(1.0 + jnp.exp(-jnp.abs(z))))
    return jnp.where(causal, ls_neg, 0.0), ls_neg + z, causal


SB_DEAD = -104.0
SB_FIRST_LANE = GROUP_HEADS


def _tri(strict_upper_src):
    r = lax.broadcasted_iota(jnp.int32, (BLOCK, BLOCK), 0)
    c = lax.broadcasted_iota(jnp.int32, (BLOCK, BLOCK), 1)
    cond = {"gt": r > c, "le": r <= c, "lt": r < c}[strict_upper_src]
    return jnp.where(cond, 1.0, 0.0).astype(BF16)


def _sb_fwd(proj, gain, seq, name, items=()):
    T = proj.shape[0]
    nblk = seq // BLOCK

    def body(q_ref, k_ref, v_ref, g_ref, o_ref, raw_ref, bt_ref):
        i = pl.program_id(1)
        q = q_ref[...]
        u_gt = _tri("gt")
        lane = lax.broadcasted_iota(jnp.int32, (BLOCK, BLOCK), 1)
        heads = [slice(h * HEAD_DIM, (h + 1) * HEAD_DIM) for h in range(GROUP_HEADS)]
        rows = [slice(h * BLOCK, (h + 1) * BLOCK) for h in range(GROUP_HEADS)]
        qs = _stack_heads(q * ATT_SCALE)

        def live(carry):
            j, _, cb = carry
            return (j >= 0) & (jnp.max(cb) > SB_DEAD)

        def step(carry):
            j, accs, cb = carry
            ks = pl.multiple_of(j * BLOCK, BLOCK)
            kb = k_ref[pl.ds(ks, BLOCK), :].astype(BF16)
            vb = v_ref[pl.ds(ks, BLOCK), :].astype(BF16)
            b, a, causal = _sb_tile(qs, kb, i * BLOCK, j * BLOCK)
            tail = _split_dot(b, u_gt) + cb
            w = jnp.where(causal, jnp.exp(a + tail), 0.0).astype(BF16)
            accs = tuple(accs[h] + jnp.dot(w[rows[h]], vb[:, hs], preferred_element_type=F32)
                         for h, hs in enumerate(heads))
            return j - 1, accs, cb + jnp.sum(b, axis=1, keepdims=True)

        zero_acc = tuple(jnp.zeros((BLOCK, HEAD_DIM), F32) for _ in heads)
        j_end, accs, cb = lax.while_loop(live, step, (i, zero_acc, jnp.zeros((HEAD_ROWS, 1), F32)))
        side = jnp.where(lane == SB_FIRST_LANE, (j_end + 1).astype(F32), 0.0)
        for h in range(GROUP_HEADS):
            side = jnp.where(lane == h, cb[rows[h]], side)
        raw = jnp.concatenate(accs, axis=1)
        raw_ref[...] = raw
        bt_ref[...] = side
        o_ref[...] = _rms(raw, g_ref[...]).astype(BF16)

    return _call(
        body, (proj, proj, proj, gain), grid=(T // seq, nblk), name=name, items=items,
        in_specs=[pl.BlockSpec((BLOCK, GROUP_WIDTH), lambda b, i: (b * nblk + i, 5)),
                  pl.BlockSpec((seq, GROUP_WIDTH), lambda b, i: (b, 6)),
                  pl.BlockSpec((seq, GROUP_WIDTH), lambda b, i: (b, 7)),
                  pl.BlockSpec(gain.shape, lambda b, i: (0, 0))],
        out_specs=[pl.BlockSpec((BLOCK, GROUP_WIDTH), lambda b, i: (b * nblk + i, 0)),
                   pl.BlockSpec((BLOCK, GROUP_WIDTH), lambda b, i: (b * nblk + i, 0)),
                   pl.BlockSpec((BLOCK, BLOCK), lambda b, i: (b * nblk + i, 0))],
        out_shape=[jax.ShapeDtypeStruct((T, GROUP_WIDTH), BF16), jax.ShapeDtypeStruct((T, GROUP_WIDTH), F32),
                   jax.ShapeDtypeStruct((T, BLOCK), F32)],
        sem=("parallel", "parallel"))


def _sb_bwd(proj, gain, raw, btot, dy, seq, name, items=()):
    T = proj.shape[0]
    nblk = seq // BLOCK

    def body(q_ref, k_ref, v_ref, g_ref, raw_ref, bt_ref, dy_ref, dq_ref, dk_ref, dv_ref, dg_ref):
        i = pl.program_id(1)

        @pl.when(i == 0)
        def _():
            dk_ref[...] = jnp.zeros_like(dk_ref)
            dv_ref[...] = jnp.zeros_like(dv_ref)

        rawv = raw_ref[...]
        _, vjp = jax.vjp(_rms, rawv, g_ref[...])
        do, dg = vjp(dy_ref[...])
        _acc_out((pl.program_id(0) == 0) & (i == 0), (dg_ref,), (dg,))
        q = q_ref[...]
        bt = bt_ref[...]
        u_le = _tri("le")
        u_lt = _tri("lt")
        heads = [slice(h * HEAD_DIM, (h + 1) * HEAD_DIM) for h in range(GROUP_HEADS)]
        rows = [slice(h * BLOCK, (h + 1) * BLOCK) for h in range(GROUP_HEADS)]
        qs = _stack_heads(q * ATT_SCALE)
        dos = _stack_heads(do)
        bts = jnp.concatenate([bt[:, h:h + 1] for h in range(GROUP_HEADS)], axis=0)
        first = jnp.max(bt[:, SB_FIRST_LANE:SB_FIRST_LANE + 1]).astype(jnp.int32)
        first = jnp.minimum(jnp.maximum(first, 0), i)

        def step(j, carry):
            dqs, cb, cg = carry
            ks = pl.multiple_of(j * BLOCK, BLOCK)
            kb = k_ref[pl.ds(ks, BLOCK), :].astype(BF16)
            vb = v_ref[pl.ds(ks, BLOCK), :].astype(BF16)
            b, a, causal = _sb_tile(qs, kb, i * BLOCK, j * BLOCK)
            tail = bts - (_split_dot(b, u_le) + cb)
            w = jnp.where(causal, jnp.exp(a + tail), 0.0)
            sig = jnp.exp(a)
            g = w * lax.dot_general(dos, vb, _NT, preferred_element_type=F32)
            gpre = _split_dot(g, u_lt) + cg
            dz = jnp.where(causal, g * (1.0 - sig) - gpre * sig, 0.0).astype(BF16)
            dqs = tuple(dqs[h] + jnp.dot(dz[rows[h]], kb[:, hs], preferred_element_type=F32)
                        for h, hs in enumerate(heads))
            dk_ref[pl.ds(ks, BLOCK), :] += lax.dot_general(dz, qs, _TN, preferred_element_type=F32)
            dv_ref[pl.ds(ks, BLOCK), :] += lax.dot_general(w.astype(BF16), dos, _TN, preferred_element_type=F32)
            return dqs, cb + jnp.sum(b, axis=1, keepdims=True), cg + jnp.sum(g, axis=1, keepdims=True)

        zero_dq = tuple(jnp.zeros((BLOCK, HEAD_DIM), F32) for _ in heads)
        zero = jnp.zeros((HEAD_ROWS, 1), F32)
        dqs, _, _ = lax.fori_loop(first, i + 1, step, (zero_dq, zero, zero))
        dq_ref[...] = (jnp.concatenate(dqs, axis=1) * ATT_SCALE).astype(BF16)

    blk = lambda c: pl.BlockSpec((BLOCK, GROUP_WIDTH), lambda b, i: (b * nblk + i, c))
    seqblk = lambda c: pl.BlockSpec((seq, GROUP_WIDTH), lambda b, i: (b, c))
    vec = pl.BlockSpec(gain.shape, lambda b, i: (0, 0))
    return _call(
        body, (proj, proj, proj, gain, raw, btot, dy), grid=(T // seq, nblk), name=name, items=items,
        in_specs=[blk(5), seqblk(6), seqblk(7), vec, blk(0), pl.BlockSpec((BLOCK, BLOCK), lambda b, i: (b * nblk + i, 0)),
                  blk(3)],
        out_specs=[blk(0), seqblk(0), seqblk(0), vec],
        out_shape=[jax.ShapeDtypeStruct((T, GROUP_WIDTH), BF16), jax.ShapeDtypeStruct((T, GROUP_WIDTH), F32),
                   jax.ShapeDtypeStruct((T, GROUP_WIDTH), F32), jax.ShapeDtypeStruct(gain.shape, F32)],
        sem=("arbitrary", "arbitrary"))


def _layer_params(l, sgu_w, sgu_b, pool_w, pool_scale, swa_sinks, mix_out_gain, norm_mix, norm_ffn):
    gains = mix_out_gain[l].reshape(4, 1, GROUP_WIDTH)
    return dict(
        wm=sgu_w[l], bt=sgu_b[l].T,
        wbd=jax.scipy.linalg.block_diag(*[pool_w[l, g] for g in range(len(POOL_WINDOWS))]),
        scale=pool_scale[l][None], sinks=swa_sinks[l][None],
        gain=[gains[m] for m in range(4)], norm_mix=norm_mix[l][None], norm_ffn=norm_ffn[l][None])


def _as_weights(g_in, g_out, g_gu, g_down):
    return (g_in[None], g_out.reshape(1, D_MODEL, D_MODEL), g_gu[None], g_down.reshape(1, D_FF_PAD, D_MODEL))


def _gather_item(src, l, dst_shape, rows=None, down=False, init=None):
    r0, nr = rows or (0, src.shape[1])
    if down:
        place = lambda dst, k: dst.at[k // 2, pl.ds((k % 2) * DOWN_ROWS + r0, nr), :]
    else:
        place = lambda dst, k: dst.at[k, pl.ds(r0, nr), :]
    return _Item(src, lambda s, k: s.at[l, pl.ds(r0, nr), :], dst_shape, place, init)


def _layer_fwd(l, x, p, w, tab, seq, shards):
    win, wout, wgu, wd = w
    nl = l + 1
    s_in, s_out, s_gu, s_down = shards or (None,) * 4
    ride = lambda item: (item(),) if shards else ()
    half_down = DOWN_ROWS // 2
    proj, h1, *g_down = _norm_mm(x, p["norm_mix"], win, 0, 4, F32, f"in_proj_{l}", items=ride(
        lambda: _gather_item(s_down, nl, (4, FF_PAD, D_MODEL), (0, half_down), True,
                             jnp.zeros((4, FF_PAD, D_MODEL), BF16))))
    ya, *g_out = _sgu_fwd(proj, p["wm"], p["bt"], p["gain"][0], f"sgu_fwd_{l}",
                          items=ride(lambda: _gather_item(s_out, nl, (N_DEV,) + s_out.shape[1:])))
    yb = _pool_fwd(proj, p["wbd"], p["scale"], p["gain"][1], seq, f"pool_fwd_{l}")
    yc, *g_down = _swa_fwd(proj, p["sinks"], tab, p["gain"][2], seq, f"swa_fwd_{l}", items=ride(
        lambda: _gather_item(s_down, nl, (4, FF_PAD, D_MODEL), (half_down, half_down), True, g_down[0])))
    half_gu = FF_PAD // 2
    yd, raw, btot, *g_gu = _sb_fwd(proj, p["gain"][3], seq, f"sb_fwd_{l}", items=ride(
        lambda: _gather_item(s_gu, nl, (N_DEV,) + s_gu.shape[1:], (0, half_gu))))
    ycat = jnp.concatenate([ya, yb, yc, yd], axis=1)
    xm, = _mm_res(x, ycat, wout, 0, f"out_proj_{l}", tn=D_MODEL)
    gu, act, h2, *g_gu = _ffn_up_act(xm, p["norm_ffn"], wgu, f"ffn_up_{l}", items=ride(
        lambda: _gather_item(s_gu, nl, (N_DEV,) + s_gu.shape[1:], (half_gu, half_gu), init=g_gu[0])))
    xo, *g_in = _mm_res(xm, act, wd, 0, f"ffn_down_{l}", tn=D_MODEL // 2,
                        items=ride(lambda: _gather_item(s_in, nl, (N_DEV,) + s_in.shape[1:])))
    w_next = _as_weights(g_in[0], g_out[0], g_gu[0], g_down[0]) if shards else None
    return xo, (x, proj, h1, ycat, raw, btot, xm, gu, h2, act), w_next


def _layer_bwd(l, dxo, saved, p, w, tab, seq, ride=()):
    win, wout, wgu, wd = w
    x, proj, h1, ycat, raw, btot, xm, gu, h2, act = saved
    out_rows = D_MODEL // N_DEV
    half = FF_PAD // 2

    def rows_of(g, r0, init=None):
        cut = lambda a, k: a.at[k, pl.ds(r0, half), :]
        return _Item(g, cut, g.shape, cut, init)

    dgu = _ffn_down_dx(dxo, wd, gu, f"ffn_down_dx_{l}")
    g_wd, = _mm_tn_rows(act, dxo, FF_PAD, f"ffn_down_dw_{l}")
    send_down = _Item(g_wd, lambda src, k: src.at[pl.ds((k // 2) * FF_PAD + (k % 2) * DOWN_ROWS, DOWN_ROWS), :],
                      (N_DEV, DOWN_ROWS, D_MODEL))
    g_wgu, p_down = _mm_tn_rows(dgu, h2, FF_PAD, f"ffn_up_dw_{l}", items=(send_down,), block_of=_ffn_chunk_block)
    g_wgu = g_wgu.reshape(N_DEV, FF_PAD, D_MODEL)
    dxm, g_norm_ffn, p_gu = _mm_norm_bwd(dgu, wgu, 0, 1, xm, p["norm_ffn"], dxo, f"ffn_up_dx_{l}",
                                         items=(rows_of(g_wgu, 0),), block_of=_ffn_chunk_block, transposed=True)
    dycat = _mm_nt(dxm, wout, 0, F32, f"out_proj_dx_{l}")
    g_wout, = _mm_tn_rows(ycat, dxm, D_MODEL, f"out_proj_dw_{l}")
    send_out = _Item(g_wout, lambda src, k: src.at[pl.ds(k * out_rows, out_rows), :], (N_DEV, out_rows, D_MODEL))
    duv, g_wm, g_bt, g_ga = _sgu_bwd(proj, p["wm"], p["bt"], p["gain"][0], dycat, f"sgu_bwd_{l}")
    dp, g_wbd, g_scale, g_gb = _pool_bwd(proj, p["wbd"], p["scale"], p["gain"][1], dycat, seq, f"pool_bwd_{l}")
    dq, dkp, dkc, g_sinks, g_tab, g_gc, *rode = _swa_bwd(proj, p["sinks"], tab, p["gain"][2], dycat, seq,
                                                         f"swa_bwd_{l}", items=ride)
    dkv = _shift_add(dkc, dkp, seq, f"swa_dkv_{l}")
    dqd, dkd, dvd, g_gd, p_gu, p_out = _sb_bwd(proj, p["gain"][3], raw, btot, dycat, seq, f"sb_bwd_{l}",
                                               items=(rows_of(g_wgu, half, p_gu), send_out))
    dproj = jnp.concatenate([duv, dp, dq, dkv, dqd, dkd.astype(BF16), dvd.astype(BF16)], axis=1)
    g_win, = _mm_tn_cols(h1, dproj, GROUP_WIDTH, 4, f"in_proj_dw_{l}")
    dx, g_norm_mix, p_in = _mm_norm_bwd(dproj, win, 0, 4, x, p["norm_mix"], dxm, f"in_proj_dx_{l}",
                                        items=(_Item(g_win, lambda src, k: src.at[k], g_win.shape),))
    ng = len(POOL_WINDOWS)
    gd = GROUP_WIDTH // ng
    small = dict(
        sgu_w=g_wm, sgu_b=g_bt.T,
        pool_w=jnp.stack([g_wbd[g * gd:(g + 1) * gd, g * gd:(g + 1) * gd] for g in range(ng)]),
        pool_scale=g_scale[0], swa_sinks=g_sinks[0],
        mix_out_gain=jnp.concatenate([g_ga[0], g_gb[0], g_gc[0], g_gd[0]]),
        norm_mix=g_norm_mix[0], norm_ffn=g_norm_ffn[0])
    return dx, (p_in, p_out, p_gu, p_down), small, g_tab, rode


def _local_step(x, target, w0, shards, sgu_w, sgu_b, pool_w, pool_scale, swa_sinks, rel_bias, mix_out_gain,
                norm_mix, norm_ffn, norm_final, seq):
    buckets = _bucket_map()
    tab = _bias_table(rel_bias, buckets, "bias_table")
    params = [_layer_params(l, sgu_w, sgu_b, pool_w, pool_scale, swa_sinks, mix_out_gain, norm_mix, norm_ffn)
              for l in range(DEPTH)]
    saved, weights = [], [w0]
    for l in range(DEPTH):
        x, s, w_next = _layer_fwd(l, x, params[l], weights[l], tab, seq, shards if l + 1 < DEPTH else None)
        saved.append(s)
        weights.append(w_next)
    loss, dx, g_final = _loss_head(x, norm_final[None], target, "loss_head")
    big, small, g_tab, upper = [None] * DEPTH, [None] * DEPTH, None, None
    for l in reversed(range(DEPTH)):
        ride = ()
        if l == 0:
            mine = _pack(_upper_layers(small), _pack_rows(_upper_layers(small)))
            ride = (_Item(mine, lambda src, k: src, (N_DEV,) + mine.shape),)
        dx, big[l], small[l], t, rode = _layer_bwd(l, dx, saved[l], params[l], weights[l], tab, seq, ride)
        g_tab = t if g_tab is None else g_tab + t
        upper = rode[0] if rode else upper
    return loss, dx, big, small, _bias_table_bwd(g_tab, buckets, "bias_table_bwd"), g_final[0], upper


def _cast_pad(w, rows, name):
    L, r, c = w.shape

    def body(w_ref, o_ref):
        if rows != r:
            o_ref[...] = jnp.zeros_like(o_ref)
        o_ref[:r, :] = w_ref[...].astype(BF16)

    return pl.pallas_call(
        body, grid=(L,), name=name,
        in_specs=[pl.BlockSpec((None, r, c), lambda l: (l, 0, 0))],
        out_specs=pl.BlockSpec((None, rows, c), lambda l: (l, 0, 0)),
        out_shape=jax.ShapeDtypeStruct((L, rows, c), BF16),
        compiler_params=_cp("parallel"))(w)


def _down_rows(ref, k):
    return ref.at[:, k // 2, pl.ds((k % 2) * DOWN_ROWS, DOWN_ROWS), :]


def _all_gather_weights(s_in, s_out, s_gu, s_down):
    L = s_in.shape[0]
    shards = (s_in, s_out, s_gu, s_down)
    down_full = jnp.zeros((L, 4, FF_PAD, D_MODEL), BF16)
    n = len(shards)

    def body(i0, i1, i2, i3, _, o0, o1, o2, o3, send_sems, recv_sems, local_sems):
        srcs = (i0, i1, i2, i3)
        outs = (o0, o1, o2, o3)
        x, y, c, _me = _my_place()
        chips = [(1 - x, y), (x, 1 - y), (1 - x, 1 - y)]

        def place(t, dev):
            k = 4 * dev[0] + 2 * dev[1] + dev[2]
            return _down_rows(outs[t], k) if t == 3 else outs[t].at[:, k]

        def copy(slot, t, block, to, from_shard):
            return pltpu.make_async_remote_copy(
                src_ref=srcs[t] if from_shard else place(t, block), dst_ref=place(t, block),
                send_sem=send_sems.at[slot, t], recv_sem=recv_sems.at[slot, t], device_id=to, device_id_type=MESH)

        me, sibling = (x, y, c), (x, y, 1 - c)
        mine = [pltpu.make_async_copy(srcs[t], place(t, me), local_sems.at[t]) for t in range(n)]
        for cp in mine:
            cp.start()
        first = [copy(0, t, me, sibling, True) for t in range(n)]
        first += [copy(1 + j, t, me, (*chip, c), True) for j, chip in enumerate(chips) for t in range(n)]
        for cp in first:
            cp.start()
        passed = []
        for j, chip in enumerate(chips):
            for t in range(n):
                copy(1 + j, t, (*chip, c), me, False).wait_recv()
                cp = copy(4 + j, t, (*chip, c), sibling, False)
                cp.start()
                passed.append(cp)
        for t in range(n):
            copy(0, t, sibling, me, False).wait_recv()
        for j, chip in enumerate(chips):
            for t in range(n):
                copy(4 + j, t, (*chip, 1 - c), me, False).wait_recv()
        for cp in first + passed:
            cp.wait_send()
        for cp in mine:
            cp.wait()

    shapes = [jax.ShapeDtypeStruct((L, N_DEV) + s.shape[1:], BF16) for s in shards[:3]]
    shapes.append(jax.ShapeDtypeStruct(down_full.shape, BF16))
    return pl.pallas_call(
        body, name="all_gather_weights", out_shape=shapes,
        in_specs=[ANY] * 5, out_specs=[ANY] * 4, input_output_aliases={4: 3},
        scratch_shapes=[pltpu.SemaphoreType.DMA((7, n)), pltpu.SemaphoreType.DMA((7, n)), pltpu.SemaphoreType.DMA((n,))],
        )(*shards, down_full)


def _adamw(w, g, m, v):
    m = ADAM_B1 * m + (1.0 - ADAM_B1) * g
    v = ADAM_B2 * v + (1.0 - ADAM_B2) * jnp.square(g)
    m_hat = m / (1.0 - ADAM_B1 ** ADAM_STEP)
    v_hat = v / (1.0 - ADAM_B2 ** ADAM_STEP)
    delta = -ADAM_LR * (m_hat / (jnp.sqrt(v_hat) + ADAM_EPS) + ADAM_WD * w)
    return delta, m, v


def _adamw_sharded(parts, w, m, v, tr, name, items=()):
    L, r, c = w.shape
    cp = parts[0].shape[-1]
    nrow = r // tr

    def body(*refs):
        p_refs, (w_ref, m_ref, v_ref, g_ref, d_ref, nm_ref, nv_ref) = refs[:L], refs[L:]
        for k in range(L):
            @pl.when(pl.program_id(0) == k)
            def _(p_ref=p_refs[k]):
                g = p_ref[0, :, :c].astype(F32)
                for dev in range(1, N_DEV):
                    g = g + p_ref[dev, :, :c].astype(F32)
                delta, nm, nv = _adamw(w_ref[...], g, m_ref[...], v_ref[...])
                g_ref[...] = g
                d_ref[...] = delta
                nm_ref[...] = nm
                nv_ref[...] = nv

    def part_spec(k):
        return pl.BlockSpec((N_DEV, tr, cp),
                            lambda l, i: (0, jnp.where(l == k, i, jnp.where(l < k, 0, nrow - 1)), 0))

    blk = pl.BlockSpec((None, tr, c), lambda l, i: (l, i, 0))
    out = jax.ShapeDtypeStruct((L, r, c), F32)
    return _call(
        body, (*parts, w, m, v), grid=(L, nrow), name=name, items=items,
        in_specs=[part_spec(k) for k in range(L)] + [blk, blk, blk],
        out_specs=[blk] * 4, out_shape=[out] * 4, sem=("arbitrary", "arbitrary"))


def _adamw_small(parts, wmv, name):
    def body(p_ref, wmv_ref, g_ref, d_ref, nm_ref, nv_ref):
        g = p_ref[0]
        for k in range(1, N_DEV):
            g = g + p_ref[k]
        delta, nm, nv = _adamw(wmv_ref[0], g, wmv_ref[1], wmv_ref[2])
        g_ref[...] = g
        d_ref[...] = delta
        nm_ref[...] = nm
        nv_ref[...] = nv

    out = jax.ShapeDtypeStruct(wmv.shape[1:], F32)
    return pl.pallas_call(body, name=name, out_shape=[out] * 4, compiler_params=_cp())(parts, wmv)


LAYERED = ("sgu_w", "sgu_b", "pool_w", "pool_scale", "swa_sinks", "mix_out_gain", "norm_mix", "norm_ffn")
SHARED = ("rel_bias", "norm_final")


def _seg_rows(a):
    return -(-a.size // 128)


def _pack_rows(parts):
    return -(-sum(_seg_rows(p) for p in parts) // 8) * 8


def _upper_layers(per_layer):
    if isinstance(per_layer, dict):
        return [per_layer[k][1:] for k in LAYERED]
    return [jnp.stack([per_layer[l][k] for l in range(1, DEPTH)]) for k in LAYERED]


def _layer_zero(stacked):
    return [stacked[k][:1] for k in LAYERED] + [stacked[k] for k in SHARED]


def _pack(parts, rows):
    segs = [jnp.pad(p.reshape(-1), (0, _seg_rows(p) * 128 - p.size)).reshape(_seg_rows(p), 128) for p in parts]
    used = sum(s.shape[0] for s in segs)
    return jnp.concatenate(segs + [jnp.zeros((rows - used, 128), F32)], axis=0)


def _pack_groups(groups, rows):
    segs = []
    for parts in groups:
        segs += [jnp.pad(p.reshape(-1), (0, _seg_rows(p) * 128 - p.size)).reshape(_seg_rows(p), 128) for p in parts]
        segs.append(jnp.zeros((rows - sum(_seg_rows(p) for p in parts), 128), F32))
    return jnp.concatenate(segs, axis=0).reshape(len(groups), rows, 128)


def _unpack(buf, like):
    out, at = [], 0
    for a in like:
        out.append(buf[at:at + _seg_rows(a)].reshape(-1)[:a.size].reshape(a.shape))
        at += _seg_rows(a)
    return out


def kernel(x, w_in, w_out, sgu_w, sgu_b, pool_w, pool_scale, swa_sinks, rel_bias, mix_out_gain, norm_mix, norm_ffn, w_gate_up, w_down, norm_final, loss_target, m_w_in, m_w_out, m_sgu_w, m_sgu_b, m_pool_w, m_pool_scale, m_swa_sinks, m_rel_bias, m_mix_out_gain, m_norm_mix, m_norm_ffn, m_w_gate_up, m_w_down, m_norm_final, v_w_in, v_w_out, v_sgu_w, v_sgu_b, v_pool_w, v_pool_scale, v_swa_sinks, v_rel_bias, v_mix_out_gain, v_norm_mix, v_norm_ffn, v_w_gate_up, v_w_down, v_norm_final):
    bl, seq, _ = x.shape
    L = w_in.shape[0]
    gu_t, m_gu_t, v_gu_t = (jnp.swapaxes(a, 1, 2) for a in (w_gate_up, m_w_gate_up, v_w_gate_up))
    shards = (_cast_pad(w_in, D_MODEL, "shard_w_in"), _cast_pad(w_out, D_MODEL // N_DEV, "shard_w_out"),
              _cast_pad(gu_t, FF_PAD, "shard_w_gate_up"), _cast_pad(w_down, DOWN_ROWS, "shard_w_down"))
    first = _all_gather_weights(*[s[:1] for s in shards])
    w0 = _as_weights(*[f[0] for f in first])
    small_w = dict(sgu_w=sgu_w, sgu_b=sgu_b, pool_w=pool_w, pool_scale=pool_scale, swa_sinks=swa_sinks,
                   rel_bias=rel_bias, mix_out_gain=mix_out_gain, norm_mix=norm_mix, norm_ffn=norm_ffn,
                   norm_final=norm_final)
    small_m = dict(sgu_w=m_sgu_w, sgu_b=m_sgu_b, pool_w=m_pool_w, pool_scale=m_pool_scale, swa_sinks=m_swa_sinks,
                   rel_bias=m_rel_bias, mix_out_gain=m_mix_out_gain, norm_mix=m_norm_mix, norm_ffn=m_norm_ffn,
                   norm_final=m_norm_final)
    small_v = dict(sgu_w=v_sgu_w, sgu_b=v_sgu_b, pool_w=v_pool_w, pool_scale=v_pool_scale, swa_sinks=v_swa_sinks,
                   rel_bias=v_rel_bias, mix_out_gain=v_mix_out_gain, norm_mix=v_norm_mix, norm_ffn=v_norm_ffn,
                   norm_final=v_norm_final)
    loss, dx, big, small, g_rel_bias, g_final, upper = _local_step(
        x.reshape(bl * seq, D_MODEL), loss_target.reshape(bl * seq, D_MODEL), w0, shards, sgu_w, sgu_b, pool_w,
        pool_scale, swa_sinks, rel_bias, mix_out_gain, norm_mix, norm_ffn, norm_final, seq)
    p_in, p_out, p_gu, p_down = ([big[l][t] for l in range(L)] for t in range(4))
    outs_gu = [jnp.swapaxes(a, 1, 2) for a in
               _adamw_sharded(p_gu, gu_t, m_gu_t, v_gu_t, FF_SHARD // 4, "adamw_w_gate_up")]
    outs_down = _adamw_sharded(p_down, w_down, m_w_down, v_w_down, DOWN_ROWS // 2, "adamw_w_down")
    outs_in = _adamw_sharded(p_in, w_in, m_w_in, v_w_in, 256, "adamw_w_in")
    lo_like = _layer_zero(small_w)
    lo_rows = _pack_rows(lo_like + [loss[0]])
    lo_mine = _pack([small[0][k][None] for k in LAYERED] + [g_rel_bias, g_final, loss[0]], lo_rows)
    *outs_out, lower = _adamw_sharded(p_out, w_out, m_w_out, v_w_out, D_MODEL // N_DEV, "adamw_w_out",
                                      items=(_Item(lo_mine, lambda src, k: src, (N_DEV,) + lo_mine.shape),))
    lo_res = _adamw_small(lower, _pack_groups([lo_like, _layer_zero(small_m), _layer_zero(small_v)], lo_rows),
                          "adamw_small_layer0")
    hi_like = _upper_layers(small_w)
    hi_rows = _pack_rows(hi_like)
    hi_res = _adamw_small(upper, _pack_groups([hi_like, _upper_layers(small_m), _upper_layers(small_v)], hi_rows),
                          "adamw_small_upper")
    loss_total = lo_res[0][sum(_seg_rows(a) for a in lo_like), 0]
    small_outs = []
    for lo_buf, hi_buf in zip(lo_res, hi_res):
        lo = dict(zip(LAYERED + SHARED, _unpack(lo_buf, lo_like)))
        hi = dict(zip(LAYERED, _unpack(hi_buf, hi_like)))
        small_outs.append({k: jnp.concatenate([lo[k], hi[k]], axis=0) if k in hi else lo[k] for k in lo})
    big_outs = dict(w_in=outs_in, w_out=outs_out, w_gate_up=outs_gu, w_down=outs_down)
    order = ("w_in", "w_out", "sgu_w", "sgu_b", "pool_w", "pool_scale", "swa_sinks", "rel_bias", "mix_out_gain",
             "norm_mix", "norm_ffn", "w_gate_up", "w_down", "norm_final")
    result = [loss_total, dx.reshape(bl, seq, D_MODEL)]
    for which in range(4):
        for name in order:
            result.append(big_outs[name][which] if name in big_outs else small_outs[which][name])
    return tuple(result)
```

```python
import functools

import jax
import jax.numpy as jnp
from jax import lax
from jax.experimental import pallas as pl
from jax.experimental.pallas import tpu as pltpu

F32 = jnp.float32
BF16 = jnp.bfloat16

N_DEV = 8
DEPTH = 4
D_MODEL = 1024
GROUP_WIDTH = 256
HEAD_DIM = 64
GROUP_HEADS = 4
BLOCK = 128
N_BUCKETS = 32
MAX_DISTANCE = 128
POOL_WINDOWS = (2, 4, 8, 16)
D_FF = 2816
FF_SHARD = D_FF // 4
FF_PAD = 768
D_FF_PAD = 4 * FF_PAD
EPS = 1e-6
ATT_SCALE = HEAD_DIM ** -0.5
ADAM_LR = 0.001
ADAM_B1 = 0.9
ADAM_B2 = 0.999
ADAM_EPS = 1e-08
ADAM_WD = 0.01
ADAM_STEP = 10
VMEM_LIMIT = 56 * 1024 * 1024
MESH_AXES = ("x", "y", "c")


def _cp(*sem):
    return pltpu.CompilerParams(dimension_semantics=sem or None, vmem_limit_bytes=VMEM_LIMIT)


_NT = (((1,), (1,)), ((), ()))
_TN = (((0,), (0,)), ((), ()))


@jax.custom_vjp
def _bdot(a, b):
    return jnp.dot(a.astype(BF16), b.astype(BF16), preferred_element_type=F32)


def _bdot_fwd(a, b):
    return _bdot(a, b), (a.astype(BF16), b.astype(BF16))


def _bdot_bwd(res, ct):
    a, b = res
    c = ct.astype(BF16)
    return (lax.dot_general(c, b, _NT, preferred_element_type=F32),
            lax.dot_general(a, c, _TN, preferred_element_type=F32))


_bdot.defvjp(_bdot_fwd, _bdot_bwd)


@jax.custom_vjp
def _bdot_nt(a, b):
    return lax.dot_general(a.astype(BF16), b.astype(BF16), _NT, preferred_element_type=F32)


def _bdot_nt_fwd(a, b):
    return _bdot_nt(a, b), (a.astype(BF16), b.astype(BF16))


def _bdot_nt_bwd(res, ct):
    a, b = res
    c = ct.astype(BF16)
    return (jnp.dot(c, b, preferred_element_type=F32),
            lax.dot_general(c, a, _TN, preferred_element_type=F32))


_bdot_nt.defvjp(_bdot_nt_fwd, _bdot_nt_bwd)


def _rms(x, g):
    return x * lax.rsqrt(jnp.mean(x * x, axis=-1, keepdims=True) + EPS) * g


def _sigmoid(x):
    return 0.5 * jnp.tanh(0.5 * x) + 0.5


def _split_dot(x, u):
    hi = x.astype(BF16)
    lo = (x - hi.astype(F32)).astype(BF16)
    return jnp.dot(hi, u, preferred_element_type=F32) + jnp.dot(lo, u, preferred_element_type=F32)


def _head_mask(h, shape):
    col = lax.broadcasted_iota(jnp.int32, shape, 1)
    return (col >= h * HEAD_DIM) & (col < (h + 1) * HEAD_DIM)


ANY = pl.BlockSpec(memory_space=pl.ANY)
MESH = pl.DeviceIdType.MESH
DOWN_ROWS = D_FF // N_DEV


def _my_place():
    x, y, c = (lax.axis_index(a) for a in MESH_AXES)
    return x, y, c, 4 * x + 2 * y + c


def _peer(x, y, c, d):
    return (x ^ (d >> 2), y ^ ((d >> 1) & 1), c ^ (d & 1))


class _Item:
    def __init__(self, src, block, dst_shape, place=None, init=None):
        self.src, self.block, self.dst_shape, self.init = src, block, dst_shape, init
        self.place = place or (lambda dst, k: dst.at[k])


def _call(body, args, *, grid, in_specs, out_specs, out_shape, sem, name, scratch_shapes=(), items=()):
    if not items:
        outs = pl.pallas_call(body, grid=grid, in_specs=in_specs, out_specs=out_specs, out_shape=out_shape, name=name,
                              scratch_shapes=list(scratch_shapes), compiler_params=_cp(*sem))(*args)
        return list(outs) if isinstance(outs, (list, tuple)) else [outs]
    n_in, n_out, n_scr, n = len(in_specs), len(out_specs), len(scratch_shapes), len(items)
    inits = [i for i, it in enumerate(items) if it.init is not None]

    def wrapped(*refs):
        core_in, srcs = refs[:n_in], refs[n_in:n_in + n]
        off = n_in + n + len(inits)
        core_out, dsts = refs[off:off + n_out], refs[off + n_out:off + n_out + n]
        scratch = refs[off + n_out + n:]
        send_sems, recv_sems, local_sems = scratch[n_scr:]
        ids = [pl.program_id(a) for a in range(len(grid))]
        first = functools.reduce(jnp.logical_and, [i == 0 for i in ids])
        last = functools.reduce(jnp.logical_and, [i == g - 1 for i, g in zip(ids, grid)])
        x, y, c, me = _my_place()

        def local(i):
            return pltpu.make_async_copy(items[i].block(srcs[i], me), items[i].place(dsts[i], me), local_sems.at[i])

        def remote(d, i, sending):
            px, py, pc = _peer(x, y, c, d)
            pk = 4 * px + 2 * py + pc
            return pltpu.make_async_remote_copy(
                src_ref=items[i].block(srcs[i], pk), dst_ref=items[i].place(dsts[i], me if sending else pk),
                send_sem=send_sems.at[d - 1, i], recv_sem=recv_sems.at[d - 1, i],
                device_id=(px, py, pc), device_id_type=MESH)

        @pl.when(first)
        def _():
            for i in range(n):
                local(i).start()
            for d in range(1, N_DEV):
                for i in range(n):
                    remote(d, i, True).start()

        body(*core_in, *core_out, *scratch[:n_scr])

        @pl.when(last)
        def _():
            for d in range(1, N_DEV):
                for i in range(n):
                    remote(d, i, False).wait_recv()
            for d in range(1, N_DEV):
                for i in range(n):
                    remote(d, i, True).wait_send()
            for i in range(n):
                local(i).wait()

    outs = pl.pallas_call(
        wrapped, grid=grid, name=name,
        in_specs=list(in_specs) + [ANY] * (n + len(inits)), out_specs=list(out_specs) + [ANY] * n,
        out_shape=list(out_shape) + [jax.ShapeDtypeStruct(it.dst_shape, it.src.dtype) for it in items],
        input_output_aliases={n_in + n + j: n_out + i for j, i in enumerate(inits)},
        scratch_shapes=list(scratch_shapes) + [pltpu.SemaphoreType.DMA((N_DEV - 1, n)),
                                               pltpu.SemaphoreType.DMA((N_DEV - 1, n)), pltpu.SemaphoreType.DMA((n,))],
        compiler_params=_cp(*(["arbitrary"] * len(grid))),
    )(*args, *[it.src for it in items], *[items[i].init for i in inits])
    return list(outs)


def _norm_mm(x, g, w, l, jb, out_dtype, name, tm=1024, items=()):
    T, K = x.shape
    _, nb, _, tn = w.shape
    tm = min(tm, T)

    def body(x_ref, g_ref, w_ref, o_ref, h_ref):
        @pl.when(pl.program_id(1) == 0)
        def _():
            h_ref[...] = _rms(x_ref[...], g_ref[...]).astype(BF16)
        h = h_ref[...]
        for jj in range(jb):
            o_ref[:, jj * tn:(jj + 1) * tn] = jnp.dot(h, w_ref[jj], preferred_element_type=F32).astype(o_ref.dtype)

    return _call(
        body, (x, g, w), grid=(T // tm, nb // jb), name=name, items=items,
        in_specs=[pl.BlockSpec((tm, K), lambda i, j: (i, 0)), pl.BlockSpec((1, K), lambda i, j: (0, 0)),
                  pl.BlockSpec((None, jb, K, tn), lambda i, j: (l, j, 0, 0))],
        out_specs=[pl.BlockSpec((tm, jb * tn), lambda i, j: (i, j)), pl.BlockSpec((tm, K), lambda i, j: (i, 0))],
        out_shape=[jax.ShapeDtypeStruct((T, nb * tn), out_dtype), jax.ShapeDtypeStruct((T, K), BF16)],
        sem=("parallel", "arbitrary"))


def _mm_res(res, a, w, l, name, tn, tm=1024, items=()):
    T, K = a.shape
    N = w.shape[2]
    tm = min(tm, T)

    def body(r_ref, a_ref, w_ref, o_ref):
        o_ref[...] = r_ref[...] + jnp.dot(a_ref[...], w_ref[...], preferred_element_type=F32)

    return _call(
        body, (res, a, w), grid=(T // tm, N // tn), name=name, items=items,
        in_specs=[pl.BlockSpec((tm, tn), lambda i, j: (i, j)), pl.BlockSpec((tm, K), lambda i, j: (i, 0)),
                  pl.BlockSpec((None, K, tn), lambda i, j: (l, 0, j))],
        out_specs=[pl.BlockSpec((tm, tn), lambda i, j: (i, j))],
        out_shape=[jax.ShapeDtypeStruct((T, N), F32)],
        sem=("parallel", "parallel"))


def _mm_nt(a, w, l, out_dtype, name, tm=1024, tn=1024):
    T, K = a.shape
    N = w.shape[1]
    tm = min(tm, T)

    def body(a_ref, w_ref, o_ref):
        o_ref[...] = lax.dot_general(a_ref[...].astype(BF16), w_ref[...], _NT,
                                     preferred_element_type=F32).astype(o_ref.dtype)

    return pl.pallas_call(
        body, grid=(T // tm, N // tn), name=name,
        in_specs=[pl.BlockSpec((tm, K), lambda i, j: (i, 0)), pl.BlockSpec((None, tn, K), lambda i, j: (l, j, 0))],
        out_specs=pl.BlockSpec((tm, tn), lambda i, j: (i, j)),
        out_shape=jax.ShapeDtypeStruct((T, N), out_dtype),
        compiler_params=_cp("parallel", "parallel"))(a, w)


def _mm_norm_bwd(a, w, l, jb, x, g, dres, name, tm=1024, items=(), block_of=lambda j: j, transposed=False):
    T = a.shape[0]
    _, nb, K, tn = w.shape
    if transposed:
        tn, K = K, tn
    tm = min(tm, T)
    nj = nb // jb
    sub = min(256, tm)
    dims = (((1,), (0,)), ((), ())) if transposed else _NT

    def body(a_ref, w_ref, x_ref, g_ref, r_ref, o_ref, dg_ref):
        part = lax.dot_general(a_ref[:, :tn], w_ref[0], dims, preferred_element_type=F32)
        for jj in range(1, jb):
            part += lax.dot_general(a_ref[:, jj * tn:(jj + 1) * tn], w_ref[jj], dims, preferred_element_type=F32)

        @pl.when(pl.program_id(1) == 0)
        def _():
            o_ref[...] = part

        @pl.when(pl.program_id(1) > 0)
        def _():
            o_ref[...] += part

        @pl.when(pl.program_id(1) == nj - 1)
        def _():
            dg = jnp.zeros((1, K), F32)
            for r in range(tm // sub):
                rows = pl.ds(r * sub, sub)
                _, vjp = jax.vjp(_rms, x_ref[rows, :], g_ref[...])
                dx, dg_r = vjp(o_ref[rows, :])
                o_ref[rows, :] = r_ref[rows, :] + dx
                dg = dg + dg_r
            _acc_out(pl.program_id(0) == 0, (dg_ref,), (dg,))

    row = pl.BlockSpec((tm, K), lambda i, j: (i, 0))
    vec = pl.BlockSpec((1, K), lambda i, j: (0, 0))
    return _call(
        body, (a, w, x, g, dres), grid=(T // tm, nj), name=name, items=items,
        in_specs=[pl.BlockSpec((tm, jb * tn), lambda i, j: (i, j)),
                  pl.BlockSpec((None, jb) + w.shape[2:], lambda i, j: (l, block_of(j), 0, 0)), row, vec, row],
        out_specs=[row, vec],
        out_shape=[jax.ShapeDtypeStruct((T, K), F32), jax.ShapeDtypeStruct((1, K), F32)],
        sem=("arbitrary", "arbitrary"))


def _mm_tn_cols(lhs, rhs, tn, jb, name, tm=1024, items=(), block_of=lambda j: j):
    T, K = lhs.shape
    nb = rhs.shape[1] // tn
    tm = min(tm, T)
    nt = T // tm

    def body(l_ref, r_ref, o_ref, acc):
        part = lax.dot_general(l_ref[...], r_ref[...], _TN, preferred_element_type=F32)

        @pl.when(pl.program_id(1) == 0)
        def _():
            acc[...] = part

        @pl.when(pl.program_id(1) > 0)
        def _():
            acc[...] += part

        @pl.when(pl.program_id(1) == nt - 1)
        def _():
            for jj in range(jb):
                o_ref[jj] = acc[:, jj * tn:(jj + 1) * tn].astype(BF16)

    return _call(
        body, (lhs, rhs), grid=(nb // jb, nt), name=name, items=items,
        in_specs=[pl.BlockSpec((tm, K), lambda j, t: (t, 0)), pl.BlockSpec((tm, jb * tn), lambda j, t: (t, j))],
        out_specs=[pl.BlockSpec((jb, K, tn), lambda j, t: (block_of(j), 0, 0))],
        out_shape=[jax.ShapeDtypeStruct((nb, K, tn), BF16)],
        scratch_shapes=[pltpu.VMEM((K, jb * tn), F32)],
        sem=("parallel", "arbitrary"))


def _mm_tn_rows(lhs, rhs, tk, name, tm=1024, items=(), block_of=lambda j: j):
    T, Kl = lhs.shape
    N = rhs.shape[1]
    tm = min(tm, T)
    nt = T // tm

    def body(l_ref, r_ref, o_ref, acc):
        part = lax.dot_general(l_ref[...], r_ref[...].astype(BF16), _TN, preferred_element_type=F32)

        @pl.when(pl.program_id(1) == 0)
        def _():
            acc[...] = part

        @pl.when(pl.program_id(1) > 0)
        def _():
            acc[...] += part

        @pl.when(pl.program_id(1) == nt - 1)
        def _():
            o_ref[...] = acc[...].astype(BF16)

    return _call(
        body, (lhs, rhs), grid=(Kl // tk, nt), name=name, items=items,
        in_specs=[pl.BlockSpec((tm, tk), lambda l, t: (t, l)), pl.BlockSpec((tm, N), lambda l, t: (t, 0))],
        out_specs=[pl.BlockSpec((tk, N), lambda l, t: (block_of(l), 0))],
        out_shape=[jax.ShapeDtypeStruct((Kl, N), BF16)],
        scratch_shapes=[pltpu.VMEM((tk, N), F32)],
        sem=("parallel", "arbitrary"))


N_FF_CHUNK = D_FF_PAD // FF_PAD


def _ffn_chunk_block(j):
    return (j % 2) * N_FF_CHUNK + j // 2


def _ffn_up_act(x, g, w, name, tm=1024, items=()):
    T, K = x.shape
    tm = min(tm, T)

    def body(x_ref, g_ref, wg_ref, wu_ref, gu_ref, act_ref, h_ref):
        @pl.when(pl.program_id(1) == 0)
        def _():
            h_ref[...] = _rms(x_ref[...], g_ref[...]).astype(BF16)
        h = h_ref[...]
        gate = lax.dot_general(h, wg_ref[...], _NT, preferred_element_type=F32)
        up = lax.dot_general(h, wu_ref[...], _NT, preferred_element_type=F32)
        gu_ref[:, :FF_PAD] = gate.astype(BF16)
        gu_ref[:, FF_PAD:] = up.astype(BF16)
        act_ref[...] = (gate * _sigmoid(gate) * up).astype(BF16)

    return _call(
        body, (x, g, w, w), grid=(T // tm, N_FF_CHUNK), name=name, items=items,
        in_specs=[pl.BlockSpec((tm, K), lambda i, j: (i, 0)), pl.BlockSpec((1, K), lambda i, j: (0, 0)),
                  pl.BlockSpec((None, None, FF_PAD, K), lambda i, j: (0, j, 0, 0)),
                  pl.BlockSpec((None, None, FF_PAD, K), lambda i, j: (0, j + N_FF_CHUNK, 0, 0))],
        out_specs=[pl.BlockSpec((tm, 2 * FF_PAD), lambda i, j: (i, j)), pl.BlockSpec((tm, FF_PAD), lambda i, j: (i, j)),
                   pl.BlockSpec((tm, K), lambda i, j: (i, 0))],
        out_shape=[jax.ShapeDtypeStruct((T, 2 * D_FF_PAD), BF16), jax.ShapeDtypeStruct((T, D_FF_PAD), BF16),
                   jax.ShapeDtypeStruct((T, K), BF16)],
        sem=("parallel", "arbitrary"))


def _ffn_down_dx(dxo, w, gu, name, tm=1024):
    T, K = dxo.shape
    tm = min(tm, T)

    sub = min(512, tm)

    def body(d_ref, w_ref, gu_ref, o_ref):
        w = w_ref[...]
        for r in range(tm // sub):
            rows = slice(r * sub, (r + 1) * sub)
            d = lax.dot_general(d_ref[rows, :].astype(BF16), w, _NT, preferred_element_type=F32)
            gate = gu_ref[rows, :FF_PAD].astype(F32)
            up = gu_ref[rows, FF_PAD:].astype(F32)
            sig = _sigmoid(gate)
            silu = gate * sig
            o_ref[rows, :FF_PAD] = (d * up * (sig + silu * (1.0 - sig))).astype(BF16)
            o_ref[rows, FF_PAD:] = (d * silu).astype(BF16)

    return pl.pallas_call(
        body, grid=(T // tm, N_FF_CHUNK), name=name,
        in_specs=[pl.BlockSpec((tm, K), lambda i, j: (i, 0)), pl.BlockSpec((None, FF_PAD, K), lambda i, j: (0, j, 0)),
                  pl.BlockSpec((tm, 2 * FF_PAD), lambda i, j: (i, j))],
        out_specs=pl.BlockSpec((tm, 2 * FF_PAD), lambda i, j: (i, j)),
        out_shape=jax.ShapeDtypeStruct((T, 2 * D_FF_PAD), BF16),
        compiler_params=_cp("parallel", "parallel"))(dxo, w, gu)


def _loss_head(x, g, target, name, tm=512):
    T, K = x.shape

    def loss_fn(xv, gv, tv):
        err = _rms(xv, gv) - tv
        return 0.5 * jnp.sum(jnp.mean(err * err, axis=-1, keepdims=True), axis=0, keepdims=True)

    def body(x_ref, g_ref, t_ref, l_ref, dx_ref, dg_ref):
        val, vjp = jax.vjp(lambda xv, gv: loss_fn(xv, gv, t_ref[...]), x_ref[...], g_ref[...])
        dx, dg = vjp(jnp.ones((1, 1), F32))
        dx_ref[...] = dx
        lval = jnp.broadcast_to(val, (1, 128))

        @pl.when(pl.program_id(0) == 0)
        def _():
            dg_ref[...] = dg
            l_ref[...] = lval

        @pl.when(pl.program_id(0) > 0)
        def _():
            dg_ref[...] += dg
            l_ref[...] += lval

    row = pl.BlockSpec((tm, K), lambda i: (i, 0))
    vec = pl.BlockSpec((1, K), lambda i: (0, 0))
    return pl.pallas_call(
        body, grid=(T // tm,), name=name,
        in_specs=[row, vec, row], out_specs=[pl.BlockSpec((1, 128), lambda i: (0, 0)), row, vec],
        out_shape=[jax.ShapeDtypeStruct((1, 128), F32), jax.ShapeDtypeStruct((T, K), F32),
                   jax.ShapeDtypeStruct((1, K), F32)],
        compiler_params=_cp("arbitrary"))(x, g, target)


SGU_ROWS = 4 * BLOCK


def _sgu_fn(u, v, wm, bt, gain):
    ug = jax.nn.gelu(u)
    vg = jax.nn.gelu(v)
    row = lax.broadcasted_iota(jnp.int32, (BLOCK, BLOCK), 0)
    col = lax.broadcasted_iota(jnp.int32, (BLOCK, BLOCK), 1)
    tri = row >= col
    normed = []
    for h in range(GROUP_HEADS):
        vh = vg[:, h * HEAD_DIM:(h + 1) * HEAD_DIM]
        xc = vh - jnp.mean(vh, axis=-1, keepdims=True)
        normed.append(xc * lax.rsqrt(jnp.mean(xc * xc, axis=-1, keepdims=True) + EPS))
    vn = jnp.concatenate(normed, axis=1)
    wcat = jnp.concatenate([jnp.where(tri, wm[h], 0.0) for h in range(GROUP_HEADS)], axis=1)
    bias = jnp.concatenate([jnp.broadcast_to(bt[:, h:h + 1], (BLOCK, HEAD_DIM)) for h in range(GROUP_HEADS)], axis=1)
    mixes = []
    for c in range(u.shape[0] // BLOCK):
        chunk = vn[c * BLOCK:(c + 1) * BLOCK]
        stacked = jnp.concatenate([jnp.where(_head_mask(h, chunk.shape), chunk, 0.0) for h in range(GROUP_HEADS)], axis=0)
        mixes.append(_bdot(wcat, stacked) + bias)
    return _rms(ug * jnp.concatenate(mixes, axis=0), gain)


def _sgu_fwd(proj, wm, bt, gain, name, items=()):
    T = proj.shape[0]
    rows = min(SGU_ROWS, T)

    def body(u_ref, v_ref, w_ref, b_ref, g_ref, o_ref):
        o_ref[...] = _sgu_fn(u_ref[...], v_ref[...], w_ref[...], b_ref[...], g_ref[...]).astype(BF16)

    full = lambda shape: pl.BlockSpec(shape, lambda i: (0,) * len(shape))
    return _call(
        body, (proj, proj, wm, bt, gain), grid=(T // rows,), name=name, items=items,
        in_specs=[pl.BlockSpec((rows, GROUP_WIDTH), lambda i: (i, 0)), pl.BlockSpec((rows, GROUP_WIDTH), lambda i: (i, 1)),
                  full(wm.shape), full(bt.shape), full(gain.shape)],
        out_specs=[pl.BlockSpec((rows, GROUP_WIDTH), lambda i: (i, 0))],
        out_shape=[jax.ShapeDtypeStruct((T, GROUP_WIDTH), BF16)],
        sem=("parallel",))


def _acc_out(first, refs, vals):
    @pl.when(first)
    def _():
        for r, v in zip(refs, vals):
            r[...] = v

    @pl.when(jnp.logical_not(first))
    def _():
        for r, v in zip(refs, vals):
            r[...] += v


def _sgu_bwd(proj, wm, bt, gain, dy, name, items=()):
    T = proj.shape[0]

    def body(u_ref, v_ref, w_ref, b_ref, g_ref, dy_ref, duv_ref, dw_ref, db_ref, dg_ref):
        _, vjp = jax.vjp(_sgu_fn, u_ref[...], v_ref[...], w_ref[...], b_ref[...], g_ref[...])
        du, dv, dw, db, dg = vjp(dy_ref[...])
        duv_ref[:, :GROUP_WIDTH] = du.astype(BF16)
        duv_ref[:, GROUP_WIDTH:] = dv.astype(BF16)
        _acc_out(pl.program_id(0) == 0, (dw_ref, db_ref, dg_ref), (dw, db, dg))

    full = lambda shape: pl.BlockSpec(shape, lambda i: (0,) * len(shape))
    rows = min(SGU_ROWS, T)
    return _call(
        body, (proj, proj, wm, bt, gain, dy), grid=(T // rows,), name=name, items=items,
        in_specs=[pl.BlockSpec((rows, GROUP_WIDTH), lambda i: (i, 0)), pl.BlockSpec((rows, GROUP_WIDTH), lambda i: (i, 1)),
                  full(wm.shape), full(bt.shape), full(gain.shape),
                  pl.BlockSpec((rows, GROUP_WIDTH), lambda i: (i, 0))],
        out_specs=[pl.BlockSpec((rows, 2 * GROUP_WIDTH), lambda i: (i, 0)), full(wm.shape), full(bt.shape), full(gain.shape)],
        out_shape=[jax.ShapeDtypeStruct((T, 2 * GROUP_WIDTH), BF16), jax.ShapeDtypeStruct(wm.shape, F32),
                   jax.ShapeDtypeStruct(bt.shape, F32), jax.ShapeDtypeStruct(gain.shape, F32)],
        sem=("arbitrary",))


def _pool_consts(seq):
    t = lax.broadcasted_iota(jnp.int32, (seq, GROUP_WIDTH), 0)
    grp = lax.broadcasted_iota(jnp.int32, (seq, GROUP_WIDTH), 1) // (GROUP_WIDTH // len(POOL_WINDOWS))
    win = jnp.where(grp == 0, POOL_WINDOWS[0], jnp.where(grp == 1, POOL_WINDOWS[1],
                    jnp.where(grp == 2, POOL_WINDOWS[2], POOL_WINDOWS[3])))
    count = jnp.minimum(t + 1, win).astype(F32)
    return t, grp, count


def _by_group(grp, vals):
    return jnp.where(grp == 0, vals[0], jnp.where(grp == 1, vals[1], jnp.where(grp == 2, vals[2], vals[3])))


def _window_sums(x, t, seq, back):
    def shift(a, k):
        if back:
            return jnp.where(t >= k, pltpu.roll(a, k, 0), 0.0)
        return jnp.where(t < seq - k, pltpu.roll(a, seq - k, 0), 0.0)
    sums = []
    a, k = x, 1
    for _ in POOL_WINDOWS:
        a = a + shift(a, k)
        sums.append(a)
        k *= 2
    return sums


def _pool_tail(y, wbd, scale, gain):
    return _rms(_bdot(y, wbd) * scale, gain)


def _pool_fwd(proj, wbd, scale, gain, seq, name):
    T = proj.shape[0]

    def body(p_ref, w_ref, s_ref, g_ref, o_ref):
        p = p_ref[...]
        t, grp, count = _pool_consts(seq)
        y = _by_group(grp, _window_sums(p, t, seq, True)) / count - p
        o_ref[...] = _pool_tail(y, w_ref[...], s_ref[...], g_ref[...]).astype(BF16)

    full = lambda shape: pl.BlockSpec(shape, lambda b: (0,) * len(shape))
    return pl.pallas_call(
        body, grid=(T // seq,), name=name,
        in_specs=[pl.BlockSpec((seq, GROUP_WIDTH), lambda b: (b, 2)), full(wbd.shape), full(scale.shape), full(gain.shape)],
        out_specs=pl.BlockSpec((seq, GROUP_WIDTH), lambda b: (b, 0)),
        out_shape=jax.ShapeDtypeStruct((T, GROUP_WIDTH), BF16),
        compiler_params=_cp("parallel"))(proj, wbd, scale, gain)


def _pool_bwd(proj, wbd, scale, gain, dy, seq, name):
    T = proj.shape[0]

    def body(p_ref, w_ref, s_ref, g_ref, dy_ref, dp_ref, dw_ref, ds_ref, dg_ref):
        p = p_ref[...]
        t, grp, count = _pool_consts(seq)
        y = _by_group(grp, _window_sums(p, t, seq, True)) / count - p
        _, vjp = jax.vjp(_pool_tail, y, w_ref[...], s_ref[...], g_ref[...])
        d_y, dw, ds, dg = vjp(dy_ref[...])
        dp = _by_group(grp, _window_sums(d_y / count, t, seq, False)) - d_y
        dp_ref[...] = dp.astype(BF16)
        _acc_out(pl.program_id(0) == 0, (dw_ref, ds_ref, dg_ref), (dw, ds, dg))

    full = lambda shape: pl.BlockSpec(shape, lambda b: (0,) * len(shape))
    return pl.pallas_call(
        body, grid=(T // seq,), name=name,
        in_specs=[pl.BlockSpec((seq, GROUP_WIDTH), lambda b: (b, 2)), full(wbd.shape), full(scale.shape), full(gain.shape),
                  pl.BlockSpec((seq, GROUP_WIDTH), lambda b: (b, 1))],
        out_specs=[pl.BlockSpec((seq, GROUP_WIDTH), lambda b: (b, 0)), full(wbd.shape), full(scale.shape), full(gain.shape)],
        out_shape=[jax.ShapeDtypeStruct((T, GROUP_WIDTH), BF16), jax.ShapeDtypeStruct(wbd.shape, F32),
                   jax.ShapeDtypeStruct(scale.shape, F32), jax.ShapeDtypeStruct(gain.shape, F32)],
        compiler_params=_cp("arbitrary"))(proj, wbd, scale, gain, dy)


def _swa_fn(q, kv_prev, kv_cur, sinks, tab, gain, first):
    half = GROUP_WIDTH // 2
    k2 = jnp.concatenate([kv_prev[:, :half], kv_cur[:, :half]], axis=0)
    v2 = jnp.concatenate([kv_prev[:, half:], kv_cur[:, half:]], axis=0)
    per_query_head = lambda a: jnp.concatenate(
        [a[:, (h // 2) * HEAD_DIM:(h // 2 + 1) * HEAD_DIM] for h in range(GROUP_HEADS)], axis=1)
    qs = jnp.concatenate([jnp.where(_head_mask(h, q.shape), q, 0.0) for h in range(GROUP_HEADS)], axis=0)
    qi = lax.broadcasted_iota(jnp.int32, (HEAD_ROWS, 2 * BLOCK), 0) & (BLOCK - 1)
    kj = lax.broadcasted_iota(jnp.int32, (HEAD_ROWS, 2 * BLOCK), 1)
    dist = qi + BLOCK - kj
    mask = (dist >= 0) & (dist < BLOCK) & ((kj >= BLOCK) | jnp.logical_not(first))
    logits = _bdot_nt(qs, per_query_head(k2)) * ATT_SCALE + tab.reshape(HEAD_ROWS, 2 * BLOCK)
    logits = jnp.where(mask, logits, -1e30)
    sink = jnp.concatenate([jnp.broadcast_to(sinks[:, h:h + 1], (BLOCK, 1)) for h in range(GROUP_HEADS)], axis=0)
    m = lax.stop_gradient(jnp.maximum(jnp.max(logits, axis=1, keepdims=True), sink))
    p = jnp.exp(logits - m)
    probs = p / (jnp.sum(p, axis=1, keepdims=True) + jnp.exp(sink - m))
    out = _bdot(probs, per_query_head(v2))
    y = jnp.zeros_like(q)
    for h in range(GROUP_HEADS):
        y = y + jnp.where(_head_mask(h, q.shape), out[h * BLOCK:(h + 1) * BLOCK], 0.0)
    return _rms(y, gain)


def _swa_specs(nblk):
    q = pl.BlockSpec((BLOCK, GROUP_WIDTH), lambda b, i: (b * nblk + i, 3))
    cur = pl.BlockSpec((BLOCK, GROUP_WIDTH), lambda b, i: (b * nblk + i, 4))
    prev = pl.BlockSpec((BLOCK, GROUP_WIDTH), lambda b, i: (b * nblk + jnp.maximum(i - 1, 0), 4))
    return q, prev, cur


def _swa_fwd(proj, sinks, tab, gain, seq, name, items=()):
    T = proj.shape[0]
    nblk = seq // BLOCK

    def body(q_ref, kp_ref, kc_ref, s_ref, t_ref, g_ref, o_ref):
        o_ref[...] = _swa_fn(q_ref[...], kp_ref[...], kc_ref[...], s_ref[...], t_ref[...], g_ref[...],
                             pl.program_id(1) == 0).astype(BF16)

    full = lambda shape: pl.BlockSpec(shape, lambda b, i: (0,) * len(shape))
    return _call(
        body, (proj, proj, proj, sinks, tab, gain), grid=(T // seq, nblk), name=name, items=items,
        in_specs=[*_swa_specs(nblk), full(sinks.shape), full(tab.shape), full(gain.shape)],
        out_specs=[pl.BlockSpec((BLOCK, GROUP_WIDTH), lambda b, i: (b * nblk + i, 0))],
        out_shape=[jax.ShapeDtypeStruct((T, GROUP_WIDTH), BF16)],
        sem=("parallel", "parallel"))


def _swa_bwd(proj, sinks, tab, gain, dy, seq, name, items=()):
    T = proj.shape[0]
    nblk = seq // BLOCK

    def body(q_ref, kp_ref, kc_ref, s_ref, t_ref, g_ref, dy_ref, dq_ref, dkp_ref, dkc_ref, ds_ref, dt_ref, dg_ref):
        first = pl.program_id(1) == 0
        fn = functools.partial(_swa_fn, first=first)
        _, vjp = jax.vjp(fn, q_ref[...], kp_ref[...], kc_ref[...], s_ref[...], t_ref[...], g_ref[...])
        dq, dkp, dkc, ds, dt, dg = vjp(dy_ref[...])
        dq_ref[...] = dq.astype(BF16)
        dkp_ref[...] = dkp
        dkc_ref[...] = dkc
        _acc_out((pl.program_id(0) == 0) & first, (ds_ref, dt_ref, dg_ref), (ds, dt, dg))

    full = lambda shape: pl.BlockSpec(shape, lambda b, i: (0,) * len(shape))
    blk = lambda c: pl.BlockSpec((BLOCK, GROUP_WIDTH), lambda b, i: (b * nblk + i, c))
    return _call(
        body, (proj, proj, proj, sinks, tab, gain, dy), grid=(T // seq, nblk), name=name, items=items,
        in_specs=[*_swa_specs(nblk), full(sinks.shape), full(tab.shape), full(gain.shape), blk(2)],
        out_specs=[blk(0), blk(0), blk(0), full(sinks.shape), full(tab.shape), full(gain.shape)],
        out_shape=[jax.ShapeDtypeStruct((T, GROUP_WIDTH), BF16), jax.ShapeDtypeStruct((T, GROUP_WIDTH), F32),
                   jax.ShapeDtypeStruct((T, GROUP_WIDTH), F32), jax.ShapeDtypeStruct(sinks.shape, F32),
                   jax.ShapeDtypeStruct(tab.shape, F32), jax.ShapeDtypeStruct(gain.shape, F32)],
        sem=("arbitrary", "arbitrary"))


def _shift_add(cur, prev, seq, name):
    T = cur.shape[0]
    nblk = seq // BLOCK

    def body(c_ref, p_ref, o_ref):
        last = pl.program_id(1) == nblk - 1
        o_ref[...] = (c_ref[...] + jnp.where(last, 0.0, p_ref[...])).astype(BF16)

    return pl.pallas_call(
        body, grid=(T // seq, nblk), name=name,
        in_specs=[pl.BlockSpec((BLOCK, GROUP_WIDTH), lambda b, i: (b * nblk + i, 0)),
                  pl.BlockSpec((BLOCK, GROUP_WIDTH), lambda b, i: (b * nblk + jnp.minimum(i + 1, nblk - 1), 0))],
        out_specs=pl.BlockSpec((BLOCK, GROUP_WIDTH), lambda b, i: (b * nblk + i, 0)),
        out_shape=jax.ShapeDtypeStruct((T, GROUP_WIDTH), BF16),
        compiler_params=_cp("parallel", "parallel"))(cur, prev)


def _t5_bucket(dist):
    max_exact = N_BUCKETS // 2
    df = jnp.maximum(dist, 1).astype(F32)
    large = max_exact + (jnp.log(df / max_exact) / jnp.log(jnp.float32(MAX_DISTANCE / max_exact))
                         * (N_BUCKETS - max_exact)).astype(jnp.int32)
    return jnp.where(dist < max_exact, dist, jnp.minimum(large, N_BUCKETS - 1))


def _bucket_map():
    dist = (jnp.arange(BLOCK)[:, None] + BLOCK) - jnp.arange(2 * BLOCK)[None, :]
    return _t5_bucket(jnp.clip(dist, 0, BLOCK - 1)).astype(jnp.int32)


def _bias_table(rel_bias, buckets, name):
    def body(rb_ref, bk_ref, o_ref):
        bk = bk_ref[...]
        rb = rb_ref[...]
        for h in range(GROUP_HEADS):
            acc = jnp.zeros((BLOCK, 2 * BLOCK), F32)
            for b in range(N_BUCKETS):
                acc = jnp.where(bk == b, rb[b:b + 1, h:h + 1], acc)
            o_ref[h] = acc

    return pl.pallas_call(body, name=name, out_shape=jax.ShapeDtypeStruct((GROUP_HEADS, BLOCK, 2 * BLOCK), F32),
                          compiler_params=_cp())(rel_bias, buckets)


def _bias_table_bwd(dtab, buckets, name):
    def body(dt_ref, bk_ref, o_ref):
        bk = bk_ref[...]
        row = lax.broadcasted_iota(jnp.int32, (N_BUCKETS, GROUP_HEADS), 0)
        col = lax.broadcasted_iota(jnp.int32, (N_BUCKETS, GROUP_HEADS), 1)
        acc = jnp.zeros((N_BUCKETS, GROUP_HEADS), F32)
        for h in range(GROUP_HEADS):
            dt = dt_ref[h]
            for b in range(N_BUCKETS):
                s = jnp.sum(jnp.where(bk == b, dt, 0.0), keepdims=True)
                acc = acc + jnp.where((row == b) & (col == h), s, 0.0)
        o_ref[...] = acc

    return pl.pallas_call(body, name=name, out_shape=jax.ShapeDtypeStruct((N_BUCKETS, GROUP_HEADS), F32),
                          compiler_params=_cp())(dtab, buckets)


HEAD_ROWS = GROUP_HEADS * BLOCK


def _stack_heads(x):
    return jnp.concatenate([jnp.where(_head_mask(h, x.shape), x, 0.0) for h in range(GROUP_HEADS)], axis=0).astype(BF16)


def _sb_tile(qs, kb, q0, k0):
    z = lax.dot_general(qs, kb, _NT, preferred_element_type=F32)
    row = lax.broadcasted_iota(jnp.int32, (HEAD_ROWS, BLOCK), 0) & (BLOCK - 1)
    col = lax.broadcasted_iota(jnp.int32, (HEAD_ROWS, BLOCK), 1)
    causal = (k0 + col) < (q0 + row)
    ls_neg = -(jnp.maximum(z, 0.0) + jnp.log(1.0 + jnp.exp(-jnp.abs(z))))
    return jnp.where(causal, ls_neg, 0.0), ls_neg + z, causal


SB_DEAD = -70.0
SB_FIRST_LANE = GROUP_HEADS


def _tri(strict_upper_src):
    r = lax.broadcasted_iota(jnp.int32, (BLOCK, BLOCK), 0)
    c = lax.broadcasted_iota(jnp.int32, (BLOCK, BLOCK), 1)
    cond = {"gt": r > c, "le": r <= c, "lt": r < c}[strict_upper_src]
    return jnp.where(cond, 1.0, 0.0).astype(BF16)


def _sb_fwd(proj, gain, seq, name, items=()):
    T = proj.shape[0]
    nblk = seq // BLOCK

    def body(q_ref, k_ref, v_ref, g_ref, o_ref, raw_ref, bt_ref):
        i = pl.program_id(1)
        q = q_ref[...]
        u_gt = _tri("gt")
        lane = lax.broadcasted_iota(jnp.int32, (BLOCK, BLOCK), 1)
        heads = [slice(h * HEAD_DIM, (h + 1) * HEAD_DIM) for h in range(GROUP_HEADS)]
        rows = [slice(h * BLOCK, (h + 1) * BLOCK) for h in range(GROUP_HEADS)]
        qs = _stack_heads(q * ATT_SCALE)

        def live(carry):
            j, _, cb = carry
            return (j >= 0) & (jnp.max(cb) > SB_DEAD)

        def step(carry):
            j, accs, cb = carry
            ks = pl.multiple_of(j * BLOCK, BLOCK)
            kb = k_ref[pl.ds(ks, BLOCK), :].astype(BF16)
            vb = v_ref[pl.ds(ks, BLOCK), :].astype(BF16)
            b, a, causal = _sb_tile(qs, kb, i * BLOCK, j * BLOCK)
            tail = _split_dot(b, u_gt) + cb
            w = jnp.where(causal, jnp.exp(a + tail), 0.0).astype(BF16)
            accs = tuple(accs[h] + jnp.dot(w[rows[h]], vb[:, hs], preferred_element_type=F32)
                         for h, hs in enumerate(heads))
            return j - 1, accs, cb + jnp.sum(b, axis=1, keepdims=True)

        zero_acc = tuple(jnp.zeros((BLOCK, HEAD_DIM), F32) for _ in heads)
        j_end, accs, cb = lax.while_loop(live, step, (i, zero_acc, jnp.zeros((HEAD_ROWS, 1), F32)))
        side = jnp.where(lane == SB_FIRST_LANE, (j_end + 1).astype(F32), 0.0)
        for h in range(GROUP_HEADS):
            side = jnp.where(lane == h, cb[rows[h]], side)
        raw = jnp.concatenate(accs, axis=1)
        raw_ref[...] = raw
        bt_ref[...] = side
        o_ref[...] = _rms(raw, g_ref[...]).astype(BF16)

    return _call(
        body, (proj, proj, proj, gain), grid=(T // seq, nblk), name=name, items=items,
        in_specs=[pl.BlockSpec((BLOCK, GROUP_WIDTH), lambda b, i: (b * nblk + i, 5)),
                  pl.BlockSpec((seq, GROUP_WIDTH), lambda b, i: (b, 6)),
                  pl.BlockSpec((seq, GROUP_WIDTH), lambda b, i: (b, 7)),
                  pl.BlockSpec(gain.shape, lambda b, i: (0, 0))],
        out_specs=[pl.BlockSpec((BLOCK, GROUP_WIDTH), lambda b, i: (b * nblk + i, 0)),
                   pl.BlockSpec((BLOCK, GROUP_WIDTH), lambda b, i: (b * nblk + i, 0)),
                   pl.BlockSpec((BLOCK, BLOCK), lambda b, i: (b * nblk + i, 0))],
        out_shape=[jax.ShapeDtypeStruct((T, GROUP_WIDTH), BF16), jax.ShapeDtypeStruct((T, GROUP_WIDTH), F32),
                   jax.ShapeDtypeStruct((T, BLOCK), F32)],
        sem=("parallel", "parallel"))


def _sb_bwd(proj, gain, raw, btot, dy, seq, name, items=()):
    T = proj.shape[0]
    nblk = seq // BLOCK

    def body(q_ref, k_ref, v_ref, g_ref, raw_ref, bt_ref, dy_ref, dq_ref, dk_ref, dv_ref, dg_ref):
        i = pl.program_id(1)

        @pl.when(i == 0)
        def _():
            dk_ref[...] = jnp.zeros_like(dk_ref)
            dv_ref[...] = jnp.zeros_like(dv_ref)

        rawv = raw_ref[...]
        _, vjp = jax.vjp(_rms, rawv, g_ref[...])
        do, dg = vjp(dy_ref[...])
        _acc_out((pl.program_id(0) == 0) & (i == 0), (dg_ref,), (dg,))
        q = q_ref[...]
        bt = bt_ref[...]
        u_le = _tri("le")
        u_lt = _tri("lt")
        heads = [slice(h * HEAD_DIM, (h + 1) * HEAD_DIM) for h in range(GROUP_HEADS)]
        rows = [slice(h * BLOCK, (h + 1) * BLOCK) for h in range(GROUP_HEADS)]
        qs = _stack_heads(q * ATT_SCALE)
        dos = _stack_heads(do)
        bts = jnp.concatenate([bt[:, h:h + 1] for h in range(GROUP_HEADS)], axis=0)
        first = jnp.max(bt[:, SB_FIRST_LANE:SB_FIRST_LANE + 1]).astype(jnp.int32)
        first = jnp.minimum(jnp.maximum(first, 0), i)

        def step(j, carry):
            dqs, cb, cg = carry
            ks = pl.multiple_of(j * BLOCK, BLOCK)
            kb = k_ref[pl.ds(ks, BLOCK), :].astype(BF16)
            vb = v_ref[pl.ds(ks, BLOCK), :].astype(BF16)
            b, a, causal = _sb_tile(qs, kb, i * BLOCK, j * BLOCK)
            tail = bts - (_split_dot(b, u_le) + cb)
            w = jnp.where(causal, jnp.exp(a + tail), 0.0)
            sig = jnp.exp(a)
            g = w * lax.dot_general(dos, vb, _NT, preferred_element_type=F32)
            gpre = _split_dot(g, u_lt) + cg
            dz = jnp.where(causal, g * (1.0 - sig) - gpre * sig, 0.0).astype(BF16)
            dqs = tuple(dqs[h] + jnp.dot(dz[rows[h]], kb[:, hs], preferred_element_type=F32)
                        for h, hs in enumerate(heads))
            dk_ref[pl.ds(ks, BLOCK), :] += lax.dot_general(dz, qs, _TN, preferred_element_type=F32)
            dv_ref[pl.ds(ks, BLOCK), :] += lax.dot_general(w.astype(BF16), dos, _TN, preferred_element_type=F32)
            return dqs, cb + jnp.sum(b, axis=1, keepdims=True), cg + jnp.sum(g, axis=1, keepdims=True)

        zero_dq = tuple(jnp.zeros((BLOCK, HEAD_DIM), F32) for _ in heads)
        zero = jnp.zeros((HEAD_ROWS, 1), F32)
        dqs, _, _ = lax.fori_loop(first, i + 1, step, (zero_dq, zero, zero))
        dq_ref[...] = (jnp.concatenate(dqs, axis=1) * ATT_SCALE).astype(BF16)

    blk = lambda c: pl.BlockSpec((BLOCK, GROUP_WIDTH), lambda b, i: (b * nblk + i, c))
    seqblk = lambda c: pl.BlockSpec((seq, GROUP_WIDTH), lambda b, i: (b, c))
    vec = pl.BlockSpec(gain.shape, lambda b, i: (0, 0))
    return _call(
        body, (proj, proj, proj, gain, raw, btot, dy), grid=(T // seq, nblk), name=name, items=items,
        in_specs=[blk(5), seqblk(6), seqblk(7), vec, blk(0), pl.BlockSpec((BLOCK, BLOCK), lambda b, i: (b * nblk + i, 0)),
                  blk(3)],
        out_specs=[blk(0), seqblk(0), seqblk(0), vec],
        out_shape=[jax.ShapeDtypeStruct((T, GROUP_WIDTH), BF16), jax.ShapeDtypeStruct((T, GROUP_WIDTH), F32),
                   jax.ShapeDtypeStruct((T, GROUP_WIDTH), F32), jax.ShapeDtypeStruct(gain.shape, F32)],
        sem=("arbitrary", "arbitrary"))


def _layer_params(l, sgu_w, sgu_b, pool_w, pool_scale, swa_sinks, mix_out_gain, norm_mix, norm_ffn):
    gains = mix_out_gain[l].reshape(4, 1, GROUP_WIDTH)
    return dict(
        wm=sgu_w[l], bt=sgu_b[l].T,
        wbd=jax.scipy.linalg.block_diag(*[pool_w[l, g] for g in range(len(POOL_WINDOWS))]),
        scale=pool_scale[l][None], sinks=swa_sinks[l][None],
        gain=[gains[m] for m in range(4)], norm_mix=norm_mix[l][None], norm_ffn=norm_ffn[l][None])


def _as_weights(g_in, g_out, g_gu, g_down):
    return (g_in[None], g_out.reshape(1, D_MODEL, D_MODEL), g_gu[None], g_down.reshape(1, D_FF_PAD, D_MODEL))


def _gather_item(src, l, dst_shape, rows=None, down=False, init=None):
    r0, nr = rows or (0, src.shape[1])
    if down:
        place = lambda dst, k: dst.at[k // 2, pl.ds((k % 2) * DOWN_ROWS + r0, nr), :]
    else:
        place = lambda dst, k: dst.at[k, pl.ds(r0, nr), :]
    return _Item(src, lambda s, k: s.at[l, pl.ds(r0, nr), :], dst_shape, place, init)


def _layer_fwd(l, x, p, w, tab, seq, shards):
    win, wout, wgu, wd = w
    nl = l + 1
    s_in, s_out, s_gu, s_down = shards or (None,) * 4
    ride = lambda item: (item(),) if shards else ()
    half_down = DOWN_ROWS // 2
    proj, h1, *g_down = _norm_mm(x, p["norm_mix"], win, 0, 4, F32, f"in_proj_{l}", items=ride(
        lambda: _gather_item(s_down, nl, (4, FF_PAD, D_MODEL), (0, half_down), True,
                             jnp.zeros((4, FF_PAD, D_MODEL), BF16))))
    ya, *g_out = _sgu_fwd(proj, p["wm"], p["bt"], p["gain"][0], f"sgu_fwd_{l}",
                          items=ride(lambda: _gather_item(s_out, nl, (N_DEV,) + s_out.shape[1:])))
    yb = _pool_fwd(proj, p["wbd"], p["scale"], p["gain"][1], seq, f"pool_fwd_{l}")
    yc, *g_down = _swa_fwd(proj, p["sinks"], tab, p["gain"][2], seq, f"swa_fwd_{l}", items=ride(
        lambda: _gather_item(s_down, nl, (4, FF_PAD, D_MODEL), (half_down, half_down), True, g_down[0])))
    half_gu = FF_PAD // 2
    yd, raw, btot, *g_gu = _sb_fwd(proj, p["gain"][3], seq, f"sb_fwd_{l}", items=ride(
        lambda: _gather_item(s_gu, nl, (N_DEV,) + s_gu.shape[1:], (0, half_gu))))
    ycat = jnp.concatenate([ya, yb, yc, yd], axis=1)
    xm, = _mm_res(x, ycat, wout, 0, f"out_proj_{l}", tn=D_MODEL)
    gu, act, h2, *g_gu = _ffn_up_act(xm, p["norm_ffn"], wgu, f"ffn_up_{l}", items=ride(
        lambda: _gather_item(s_gu, nl, (N_DEV,) + s_gu.shape[1:], (half_gu, half_gu), init=g_gu[0])))
    xo, *g_in = _mm_res(xm, act, wd, 0, f"ffn_down_{l}", tn=D_MODEL // 2,
                        items=ride(lambda: _gather_item(s_in, nl, (N_DEV,) + s_in.shape[1:])))
    w_next = _as_weights(g_in[0], g_out[0], g_gu[0], g_down[0]) if shards else None
    return xo, (x, proj, h1, ycat, raw, btot, xm, gu, h2, act), w_next


def _layer_bwd(l, dxo, saved, p, w, tab, seq, ride=()):
    win, wout, wgu, wd = w
    x, proj, h1, ycat, raw, btot, xm, gu, h2, act = saved
    out_rows = D_MODEL // N_DEV
    half = FF_PAD // 2

    def rows_of(g, r0, init=None):
        cut = lambda a, k: a.at[k, pl.ds(r0, half), :]
        return _Item(g, cut, g.shape, cut, init)

    dgu = _ffn_down_dx(dxo, wd, gu, f"ffn_down_dx_{l}")
    g_wd, = _mm_tn_rows(act, dxo, FF_PAD, f"ffn_down_dw_{l}")
    send_down = _Item(g_wd, lambda src, k: src.at[pl.ds((k // 2) * FF_PAD + (k % 2) * DOWN_ROWS, DOWN_ROWS), :],
                      (N_DEV, DOWN_ROWS, D_MODEL))
    g_wgu, p_down = _mm_tn_rows(dgu, h2, FF_PAD, f"ffn_up_dw_{l}", items=(send_down,), block_of=_ffn_chunk_block)
    g_wgu = g_wgu.reshape(N_DEV, FF_PAD, D_MODEL)
    dxm, g_norm_ffn, p_gu = _mm_norm_bwd(dgu, wgu, 0, 1, xm, p["norm_ffn"], dxo, f"ffn_up_dx_{l}",
                                         items=(rows_of(g_wgu, 0),), block_of=_ffn_chunk_block, transposed=True)
    dycat = _mm_nt(dxm, wout, 0, F32, f"out_proj_dx_{l}")
    g_wout, = _mm_tn_rows(ycat, dxm, D_MODEL, f"out_proj_dw_{l}")
    send_out = _Item(g_wout, lambda src, k: src.at[pl.ds(k * out_rows, out_rows), :], (N_DEV, out_rows, D_MODEL))
    duv, g_wm, g_bt, g_ga = _sgu_bwd(proj, p["wm"], p["bt"], p["gain"][0], dycat, f"sgu_bwd_{l}")
    dp, g_wbd, g_scale, g_gb = _pool_bwd(proj, p["wbd"], p["scale"], p["gain"][1], dycat, seq, f"pool_bwd_{l}")
    dq, dkp, dkc, g_sinks, g_tab, g_gc, *rode = _swa_bwd(proj, p["sinks"], tab, p["gain"][2], dycat, seq,
                                                         f"swa_bwd_{l}", items=ride)
    dkv = _shift_add(dkc, dkp, seq, f"swa_dkv_{l}")
    dqd, dkd, dvd, g_gd, p_gu, p_out = _sb_bwd(proj, p["gain"][3], raw, btot, dycat, seq, f"sb_bwd_{l}",
                                               items=(rows_of(g_wgu, half, p_gu), send_out))
    dproj = jnp.concatenate([duv, dp, dq, dkv, dqd, dkd.astype(BF16), dvd.astype(BF16)], axis=1)
    g_win, = _mm_tn_cols(h1, dproj, GROUP_WIDTH, 4, f"in_proj_dw_{l}")
    dx, g_norm_mix, p_in = _mm_norm_bwd(dproj, win, 0, 4, x, p["norm_mix"], dxm, f"in_proj_dx_{l}",
                                        items=(_Item(g_win, lambda src, k: src.at[k], g_win.shape),))
    ng = len(POOL_WINDOWS)
    gd = GROUP_WIDTH // ng
    small = dict(
        sgu_w=g_wm, sgu_b=g_bt.T,
        pool_w=jnp.stack([g_wbd[g * gd:(g + 1) * gd, g * gd:(g + 1) * gd] for g in range(ng)]),
        pool_scale=g_scale[0], swa_sinks=g_sinks[0],
        mix_out_gain=jnp.concatenate([g_ga[0], g_gb[0], g_gc[0], g_gd[0]]),
        norm_mix=g_norm_mix[0], norm_ffn=g_norm_ffn[0])
    return dx, (p_in, p_out, p_gu, p_down), small, g_tab, rode


def _local_step(x, target, w0, shards, sgu_w, sgu_b, pool_w, pool_scale, swa_sinks, rel_bias, mix_out_gain,
                norm_mix, norm_ffn, norm_final, seq):
    buckets = _bucket_map()
    tab = _bias_table(rel_bias, buckets, "bias_table")
    params = [_layer_params(l, sgu_w, sgu_b, pool_w, pool_scale, swa_sinks, mix_out_gain, norm_mix, norm_ffn)
              for l in range(DEPTH)]
    saved, weights = [], [w0]
    for l in range(DEPTH):
        x, s, w_next = _layer_fwd(l, x, params[l], weights[l], tab, seq, shards if l + 1 < DEPTH else None)
        saved.append(s)
        weights.append(w_next)
    loss, dx, g_final = _loss_head(x, norm_final[None], target, "loss_head")
    big, small, g_tab, upper = [None] * DEPTH, [None] * DEPTH, None, None
    for l in reversed(range(DEPTH)):
        ride = ()
        if l == 0:
            mine = _pack(_upper_layers(small), _pack_rows(_upper_layers(small)))
            ride = (_Item(mine, lambda src, k: src, (N_DEV,) + mine.shape),)
        dx, big[l], small[l], t, rode = _layer_bwd(l, dx, saved[l], params[l], weights[l], tab, seq, ride)
        g_tab = t if g_tab is None else g_tab + t
        upper = rode[0] if rode else upper
    return loss, dx, big, small, _bias_table_bwd(g_tab, buckets, "bias_table_bwd"), g_final[0], upper


def _cast_pad(w, rows, name):
    L, r, c = w.shape

    def body(w_ref, o_ref):
        if rows != r:
            o_ref[...] = jnp.zeros_like(o_ref)
        o_ref[:r, :] = w_ref[...].astype(BF16)

    return pl.pallas_call(
        body, grid=(L,), name=name,
        in_specs=[pl.BlockSpec((None, r, c), lambda l: (l, 0, 0))],
        out_specs=pl.BlockSpec((None, rows, c), lambda l: (l, 0, 0)),
        out_shape=jax.ShapeDtypeStruct((L, rows, c), BF16),
        compiler_params=_cp("parallel"))(w)


def _down_rows(ref, k):
    return ref.at[:, k // 2, pl.ds((k % 2) * DOWN_ROWS, DOWN_ROWS), :]


def _all_gather_weights(s_in, s_out, s_gu, s_down):
    L = s_in.shape[0]
    shards = (s_in, s_out, s_gu, s_down)
    down_full = jnp.zeros((L, 4, FF_PAD, D_MODEL), BF16)
    n = len(shards)

    def body(i0, i1, i2, i3, _, o0, o1, o2, o3, send_sems, recv_sems, local_sems):
        srcs = (i0, i1, i2, i3)
        outs = (o0, o1, o2, o3)
        x, y, c, _me = _my_place()
        chips = [(1 - x, y), (x, 1 - y), (1 - x, 1 - y)]

        def place(t, dev):
            k = 4 * dev[0] + 2 * dev[1] + dev[2]
            return _down_rows(outs[t], k) if t == 3 else outs[t].at[:, k]

        def copy(slot, t, block, to, from_shard):
            return pltpu.make_async_remote_copy(
                src_ref=srcs[t] if from_shard else place(t, block), dst_ref=place(t, block),
                send_sem=send_sems.at[slot, t], recv_sem=recv_sems.at[slot, t], device_id=to, device_id_type=MESH)

        me, sibling = (x, y, c), (x, y, 1 - c)
        mine = [pltpu.make_async_copy(srcs[t], place(t, me), local_sems.at[t]) for t in range(n)]
        for cp in mine:
            cp.start()
        first = [copy(0, t, me, sibling, True) for t in range(n)]
        first += [copy(1 + j, t, me, (*chip, c), True) for j, chip in enumerate(chips) for t in range(n)]
        for cp in first:
            cp.start()
        passed = []
        for j, chip in enumerate(chips):
            for t in range(n):
                copy(1 + j, t, (*chip, c), me, False).wait_recv()
                cp = copy(4 + j, t, (*chip, c), sibling, False)
                cp.start()
                passed.append(cp)
        for t in range(n):
            copy(0, t, sibling, me, False).wait_recv()
        for j, chip in enumerate(chips):
            for t in range(n):
                copy(4 + j, t, (*chip, 1 - c), me, False).wait_recv()
        for cp in first + passed:
            cp.wait_send()
        for cp in mine:
            cp.wait()

    shapes = [jax.ShapeDtypeStruct((L, N_DEV) + s.shape[1:], BF16) for s in shards[:3]]
    shapes.append(jax.ShapeDtypeStruct(down_full.shape, BF16))
    return pl.pallas_call(
        body, name="all_gather_weights", out_shape=shapes,
        in_specs=[ANY] * 5, out_specs=[ANY] * 4, input_output_aliases={4: 3},
        scratch_shapes=[pltpu.SemaphoreType.DMA((7, n)), pltpu.SemaphoreType.DMA((7, n)), pltpu.SemaphoreType.DMA((n,))],
        )(*shards, down_full)


def _adamw(w, g, m, v):
    m = ADAM_B1 * m + (1.0 - ADAM_B1) * g
    v = ADAM_B2 * v + (1.0 - ADAM_B2) * jnp.square(g)
    m_hat = m / (1.0 - ADAM_B1 ** ADAM_STEP)
    v_hat = v / (1.0 - ADAM_B2 ** ADAM_STEP)
    delta = -ADAM_LR * (m_hat / (jnp.sqrt(v_hat) + ADAM_EPS) + ADAM_WD * w)
    return delta, m, v


def _adamw_sharded(parts, w, m, v, tr, name, items=()):
    L, r, c = w.shape
    cp = parts[0].shape[-1]
    nrow = r // tr

    def body(*refs):
        p_refs, (w_ref, m_ref, v_ref, g_ref, d_ref, nm_ref, nv_ref) = refs[:L], refs[L:]
        for k in range(L):
            @pl.when(pl.program_id(0) == k)
            def _(p_ref=p_refs[k]):
                g = p_ref[0, :, :c].astype(F32)
                for dev in range(1, N_DEV):
                    g = g + p_ref[dev, :, :c].astype(F32)
                delta, nm, nv = _adamw(w_ref[...], g, m_ref[...], v_ref[...])
                g_ref[...] = g
                d_ref[...] = delta
                nm_ref[...] = nm
                nv_ref[...] = nv

    def part_spec(k):
        return pl.BlockSpec((N_DEV, tr, cp),
                            lambda l, i: (0, jnp.where(l == k, i, jnp.where(l < k, 0, nrow - 1)), 0))

    blk = pl.BlockSpec((None, tr, c), lambda l, i: (l, i, 0))
    out = jax.ShapeDtypeStruct((L, r, c), F32)
    return _call(
        body, (*parts, w, m, v), grid=(L, nrow), name=name, items=items,
        in_specs=[part_spec(k) for k in range(L)] + [blk, blk, blk],
        out_specs=[blk] * 4, out_shape=[out] * 4, sem=("arbitrary", "arbitrary"))


def _adamw_small(parts, wmv, name):
    def body(p_ref, wmv_ref, g_ref, d_ref, nm_ref, nv_ref):
        g = p_ref[0]
        for k in range(1, N_DEV):
            g = g + p_ref[k]
        delta, nm, nv = _adamw(wmv_ref[0], g, wmv_ref[1], wmv_ref[2])
        g_ref[...] = g
        d_ref[...] = delta
        nm_ref[...] = nm
        nv_ref[...] = nv

    out = jax.ShapeDtypeStruct(wmv.shape[1:], F32)
    return pl.pallas_call(body, name=name, out_shape=[out] * 4, compiler_params=_cp())(parts, wmv)


LAYERED = ("sgu_w", "sgu_b", "pool_w", "pool_scale", "swa_sinks", "mix_out_gain", "norm_mix", "norm_ffn")
SHARED = ("rel_bias", "norm_final")


def _seg_rows(a):
    return -(-a.size // 128)


def _pack_rows(parts):
    return -(-sum(_seg_rows(p) for p in parts) // 8) * 8


def _upper_layers(per_layer):
    if isinstance(per_layer, dict):
        return [per_layer[k][1:] for k in LAYERED]
    return [jnp.stack([per_layer[l][k] for l in range(1, DEPTH)]) for k in LAYERED]


def _layer_zero(stacked):
    return [stacked[k][:1] for k in LAYERED] + [stacked[k] for k in SHARED]


def _pack(parts, rows):
    segs = [jnp.pad(p.reshape(-1), (0, _seg_rows(p) * 128 - p.size)).reshape(_seg_rows(p), 128) for p in parts]
    used = sum(s.shape[0] for s in segs)
    return jnp.concatenate(segs + [jnp.zeros((rows - used, 128), F32)], axis=0)


def _pack_groups(groups, rows):
    segs = []
    for parts in groups:
        segs += [jnp.pad(p.reshape(-1), (0, _seg_rows(p) * 128 - p.size)).reshape(_seg_rows(p), 128) for p in parts]
        segs.append(jnp.zeros((rows - sum(_seg_rows(p) for p in parts), 128), F32))
    return jnp.concatenate(segs, axis=0).reshape(len(groups), rows, 128)


def _unpack(buf, like):
    out, at = [], 0
    for a in like:
        out.append(buf[at:at + _seg_rows(a)].reshape(-1)[:a.size].reshape(a.shape))
        at += _seg_rows(a)
    return out


def kernel(x, w_in, w_out, sgu_w, sgu_b, pool_w, pool_scale, swa_sinks, rel_bias, mix_out_gain, norm_mix, norm_ffn, w_gate_up, w_down, norm_final, loss_target, m_w_in, m_w_out, m_sgu_w, m_sgu_b, m_pool_w, m_pool_scale, m_swa_sinks, m_rel_bias, m_mix_out_gain, m_norm_mix, m_norm_ffn, m_w_gate_up, m_w_down, m_norm_final, v_w_in, v_w_out, v_sgu_w, v_sgu_b, v_pool_w, v_pool_scale, v_swa_sinks, v_rel_bias, v_mix_out_gain, v_norm_mix, v_norm_ffn, v_w_gate_up, v_w_down, v_norm_final):
    bl, seq, _ = x.shape
    L = w_in.shape[0]
    gu_t, m_gu_t, v_gu_t = (jnp.swapaxes(a, 1, 2) for a in (w_gate_up, m_w_gate_up, v_w_gate_up))
    shards = (_cast_pad(w_in, D_MODEL, "shard_w_in"), _cast_pad(w_out, D_MODEL // N_DEV, "shard_w_out"),
              _cast_pad(gu_t, FF_PAD, "shard_w_gate_up"), _cast_pad(w_down, DOWN_ROWS, "shard_w_down"))
    first = _all_gather_weights(*[s[:1] for s in shards])
    w0 = _as_weights(*[f[0] for f in first])
    small_w = dict(sgu_w=sgu_w, sgu_b=sgu_b, pool_w=pool_w, pool_scale=pool_scale, swa_sinks=swa_sinks,
                   rel_bias=rel_bias, mix_out_gain=mix_out_gain, norm_mix=norm_mix, norm_ffn=norm_ffn,
                   norm_final=norm_final)
    small_m = dict(sgu_w=m_sgu_w, sgu_b=m_sgu_b, pool_w=m_pool_w, pool_scale=m_pool_scale, swa_sinks=m_swa_sinks,
                   rel_bias=m_rel_bias, mix_out_gain=m_mix_out_gain, norm_mix=m_norm_mix, norm_ffn=m_norm_ffn,
                   norm_final=m_norm_final)
    small_v = dict(sgu_w=v_sgu_w, sgu_b=v_sgu_b, pool_w=v_pool_w, pool_scale=v_pool_scale, swa_sinks=v_swa_sinks,
                   rel_bias=v_rel_bias, mix_out_gain=v_mix_out_gain, norm_mix=v_norm_mix, norm_ffn=v_norm_ffn,
                   norm_final=v_norm_final)
    loss, dx, big, small, g_rel_bias, g_final, upper = _local_step(
        x.reshape(bl * seq, D_MODEL), loss_target.reshape(bl * seq, D_MODEL), w0, shards, sgu_w, sgu_b, pool_w,
        pool_scale, swa_sinks, rel_bias, mix_out_gain, norm_mix, norm_ffn, norm_final, seq)
    p_in, p_out, p_gu, p_down = ([big[l][t] for l in range(L)] for t in range(4))
    outs_gu = [jnp.swapaxes(a, 1, 2) for a in
               _adamw_sharded(p_gu, gu_t, m_gu_t, v_gu_t, FF_SHARD // 4, "adamw_w_gate_up")]
    outs_down = _adamw_sharded(p_down, w_down, m_w_down, v_w_down, DOWN_ROWS // 2, "adamw_w_down")
    outs_in = _adamw_sharded(p_in, w_in, m_w_in, v_w_in, 256, "adamw_w_in")
    lo_like = _layer_zero(small_w)
    lo_rows = _pack_rows(lo_like + [loss[0]])
    lo_mine = _pack([small[0][k][None] for k in LAYERED] + [g_rel_bias, g_final, loss[0]], lo_rows)
    *outs_out, lower = _adamw_sharded(p_out, w_out, m_w_out, v_w_out, D_MODEL // N_DEV, "adamw_w_out",
                                      items=(_Item(lo_mine, lambda src, k: src, (N_DEV,) + lo_mine.shape),))
    lo_res = _adamw_small(lower, _pack_groups([lo_like, _layer_zero(small_m), _layer_zero(small_v)], lo_rows),
                          "adamw_small_layer0")
    hi_like = _upper_layers(small_w)
    hi_rows = _pack_rows(hi_like)
    hi_res = _adamw_small(upper, _pack_groups([hi_like, _upper_layers(small_m), _upper_layers(small_v)], hi_rows),
                          "adamw_small_upper")
    loss_total = lo_res[0][sum(_seg_rows(a) for a in lo_like), 0]
    small_outs = []
    for lo_buf, hi_buf in zip(lo_res, hi_res):
        lo = dict(zip(LAYERED + SHARED, _unpack(lo_buf, lo_like)))
        hi = dict(zip(LAYERED, _unpack(hi_buf, hi_like)))
        small_outs.append({k: jnp.concatenate([lo[k], hi[k]], axis=0) if k in hi else lo[k] for k in lo})
    big_outs = dict(w_in=outs_in, w_out=outs_out, w_gate_up=outs_gu, w_down=outs_down)
    order = ("w_in", "w_out", "sgu_w", "sgu_b", "pool_w", "pool_scale", "swa_sinks", "rel_bias", "mix_out_gain",
             "norm_mix", "norm_ffn", "w_gate_up", "w_down", "norm_final")
    result = [loss_total, dx.reshape(bl, seq, D_MODEL)]
    for which in range(4):
        for name in order:
            result.append(big_outs[name][which] if name in big_outs else small_outs[which][name])
    return tuple(result)
```

```python
import functools

import jax
import jax.numpy as jnp
from jax import lax
from jax.experimental import pallas as pl
from jax.experimental.pallas import tpu as pltpu

F32 = jnp.float32
BF16 = jnp.bfloat16

N_DEV = 8
DEPTH = 4
D_MODEL = 1024
GROUP_WIDTH = 256
HEAD_DIM = 64
GROUP_HEADS = 4
BLOCK = 128
N_BUCKETS = 32
MAX_DISTANCE = 128
POOL_WINDOWS = (2, 4, 8, 16)
D_FF = 2816
FF_SHARD = D_FF // 4
FF_PAD = 768
D_FF_PAD = 4 * FF_PAD
EPS = 1e-6
ATT_SCALE = HEAD_DIM ** -0.5
ADAM_LR = 0.001
ADAM_B1 = 0.9
ADAM_B2 = 0.999
ADAM_EPS = 1e-08
ADAM_WD = 0.01
ADAM_STEP = 10
VMEM_LIMIT = 56 * 1024 * 1024
MESH_AXES = ("x", "y", "c")


def _cp(*sem):
    return pltpu.CompilerParams(dimension_semantics=sem or None, vmem_limit_bytes=VMEM_LIMIT)


_NT = (((1,), (1,)), ((), ()))
_TN = (((0,), (0,)), ((), ()))


@jax.custom_vjp
def _bdot(a, b):
    return jnp.dot(a.astype(BF16), b.astype(BF16), preferred_element_type=F32)


def _bdot_fwd(a, b):
    return _bdot(a, b), (a.astype(BF16), b.astype(BF16))


def _bdot_bwd(res, ct):
    a, b = res
    c = ct.astype(BF16)
    return (lax.dot_general(c, b, _NT, preferred_element_type=F32),
            lax.dot_general(a, c, _TN, preferred_element_type=F32))


_bdot.defvjp(_bdot_fwd, _bdot_bwd)


@jax.custom_vjp
def _bdot_nt(a, b):
    return lax.dot_general(a.astype(BF16), b.astype(BF16), _NT, preferred_element_type=F32)


def _bdot_nt_fwd(a, b):
    return _bdot_nt(a, b), (a.astype(BF16), b.astype(BF16))


def _bdot_nt_bwd(res, ct):
    a, b = res
    c = ct.astype(BF16)
    return (jnp.dot(c, b, preferred_element_type=F32),
            lax.dot_general(c, a, _TN, preferred_element_type=F32))


_bdot_nt.defvjp(_bdot_nt_fwd, _bdot_nt_bwd)


def _rms(x, g):
    return x * lax.rsqrt(jnp.mean(x * x, axis=-1, keepdims=True) + EPS) * g


def _sigmoid(x):
    return 0.5 * jnp.tanh(0.5 * x) + 0.5


def _split_dot(x, u):
    hi = x.astype(BF16)
    lo = (x - hi.astype(F32)).astype(BF16)
    return jnp.dot(hi, u, preferred_element_type=F32) + jnp.dot(lo, u, preferred_element_type=F32)


def _head_mask(h, shape):
    col = lax.broadcasted_iota(jnp.int32, shape, 1)
    return (col >= h * HEAD_DIM) & (col < (h + 1) * HEAD_DIM)


ANY = pl.BlockSpec(memory_space=pl.ANY)
MESH = pl.DeviceIdType.MESH
DOWN_ROWS = D_FF // N_DEV


def _my_place():
    x, y, c = (lax.axis_index(a) for a in MESH_AXES)
    return x, y, c, 4 * x + 2 * y + c


def _peer(x, y, c, d):
    return (x ^ (d >> 2), y ^ ((d >> 1) & 1), c ^ (d & 1))


class _Item:
    def __init__(self, src, block, dst_shape, place=None, init=None):
        self.src, self.block, self.dst_shape, self.init = src, block, dst_shape, init
        self.place = place or (lambda dst, k: dst.at[k])


def _call(body, args, *, grid, in_specs, out_specs, out_shape, sem, name, scratch_shapes=(), items=()):
    if not items:
        outs = pl.pallas_call(body, grid=grid, in_specs=in_specs, out_specs=out_specs, out_shape=out_shape, name=name,
                              scratch_shapes=list(scratch_shapes), compiler_params=_cp(*sem))(*args)
        return list(outs) if isinstance(outs, (list, tuple)) else [outs]
    n_in, n_out, n_scr, n = len(in_specs), len(out_specs), len(scratch_shapes), len(items)
    inits = [i for i, it in enumerate(items) if it.init is not None]

    def wrapped(*refs):
        core_in, srcs = refs[:n_in], refs[n_in:n_in + n]
        off = n_in + n + len(inits)
        core_out, dsts = refs[off:off + n_out], refs[off + n_out:off + n_out + n]
        scratch = refs[off + n_out + n:]
        send_sems, recv_sems, local_sems = scratch[n_scr:]
        ids = [pl.program_id(a) for a in range(len(grid))]
        first = functools.reduce(jnp.logical_and, [i == 0 for i in ids])
        last = functools.reduce(jnp.logical_and, [i == g - 1 for i, g in zip(ids, grid)])
        x, y, c, me = _my_place()

        def local(i):
            return pltpu.make_async_copy(items[i].block(srcs[i], me), items[i].place(dsts[i], me), local_sems.at[i])

        def remote(d, i, sending):
            px, py, pc = _peer(x, y, c, d)
            pk = 4 * px + 2 * py + pc
            return pltpu.make_async_remote_copy(
                src_ref=items[i].block(srcs[i], pk), dst_ref=items[i].place(dsts[i], me if sending else pk),
                send_sem=send_sems.at[d - 1, i], recv_sem=recv_sems.at[d - 1, i],
                device_id=(px, py, pc), device_id_type=MESH)

        @pl.when(first)
        def _():
            for i in range(n):
                local(i).start()
            for d in range(1, N_DEV):
                for i in range(n):
                    remote(d, i, True).start()

        body(*core_in, *core_out, *scratch[:n_scr])

        @pl.when(last)
        def _():
            for d in range(1, N_DEV):
                for i in range(n):
                    remote(d, i, False).wait_recv()
            for d in range(1, N_DEV):
                for i in range(n):
                    remote(d, i, True).wait_send()
            for i in range(n):
                local(i).wait()

    outs = pl.pallas_call(
        wrapped, grid=grid, name=name,
        in_specs=list(in_specs) + [ANY] * (n + len(inits)), out_specs=list(out_specs) + [ANY] * n,
        out_shape=list(out_shape) + [jax.ShapeDtypeStruct(it.dst_shape, it.src.dtype) for it in items],
        input_output_aliases={n_in + n + j: n_out + i for j, i in enumerate(inits)},
        scratch_shapes=list(scratch_shapes) + [pltpu.SemaphoreType.DMA((N_DEV - 1, n)),
                                               pltpu.SemaphoreType.DMA((N_DEV - 1, n)), pltpu.SemaphoreType.DMA((n,))],
        compiler_params=_cp(*(["arbitrary"] * len(grid))),
    )(*args, *[it.src for it in items], *[items[i].init for i in inits])
    return list(outs)


def _norm_mm(x, g, w, l, jb, out_dtype, name, tm=1024, items=()):
    T, K = x.shape
    _, nb, _, tn = w.shape
    tm = min(tm, T)

    def body(x_ref, g_ref, w_ref, o_ref, h_ref):
        @pl.when(pl.program_id(1) == 0)
        def _():
            h_ref[...] = _rms(x_ref[...], g_ref[...]).astype(BF16)
        h = h_ref[...]
        for jj in range(jb):
            o_ref[:, jj * tn:(jj + 1) * tn] = jnp.dot(h, w_ref[jj], preferred_element_type=F32).astype(o_ref.dtype)

    return _call(
        body, (x, g, w), grid=(T // tm, nb // jb), name=name, items=items,
        in_specs=[pl.BlockSpec((tm, K), lambda i, j: (i, 0)), pl.BlockSpec((1, K), lambda i, j: (0, 0)),
                  pl.BlockSpec((None, jb, K, tn), lambda i, j: (l, j, 0, 0))],
        out_specs=[pl.BlockSpec((tm, jb * tn), lambda i, j: (i, j)), pl.BlockSpec((tm, K), lambda i, j: (i, 0))],
        out_shape=[jax.ShapeDtypeStruct((T, nb * tn), out_dtype), jax.ShapeDtypeStruct((T, K), BF16)],
        sem=("parallel", "arbitrary"))


def _mm_res(res, a, w, l, name, tn, tm=1024, items=()):
    T, K = a.shape
    N = w.shape[2]
    tm = min(tm, T)

    def body(r_ref, a_ref, w_ref, o_ref):
        o_ref[...] = r_ref[...] + jnp.dot(a_ref[...], w_ref[...], preferred_element_type=F32)

    return _call(
        body, (res, a, w), grid=(T // tm, N // tn), name=name, items=items,
        in_specs=[pl.BlockSpec((tm, tn), lambda i, j: (i, j)), pl.BlockSpec((tm, K), lambda i, j: (i, 0)),
                  pl.BlockSpec((None, K, tn), lambda i, j: (l, 0, j))],
        out_specs=[pl.BlockSpec((tm, tn), lambda i, j: (i, j))],
        out_shape=[jax.ShapeDtypeStruct((T, N), F32)],
        sem=("parallel", "parallel"))


def _mm_nt(a, w, l, out_dtype, name, tm=1024, tn=1024):
    T, K = a.shape
    N = w.shape[1]
    tm = min(tm, T)

    def body(a_ref, w_ref, o_ref):
        o_ref[...] = lax.dot_general(a_ref[...].astype(BF16), w_ref[...], _NT,
                                     preferred_element_type=F32).astype(o_ref.dtype)

    return pl.pallas_call(
        body, grid=(T // tm, N // tn), name=name,
        in_specs=[pl.BlockSpec((tm, K), lambda i, j: (i, 0)), pl.BlockSpec((None, tn, K), lambda i, j: (l, j, 0))],
        out_specs=pl.BlockSpec((tm, tn), lambda i, j: (i, j)),
        out_shape=jax.ShapeDtypeStruct((T, N), out_dtype),
        compiler_params=_cp("parallel", "parallel"))(a, w)


def _mm_norm_bwd(a, w, l, jb, x, g, dres, name, tm=1024, items=(), block_of=lambda j: j, transposed=False):
    T = a.shape[0]
    _, nb, K, tn = w.shape
    if transposed:
        tn, K = K, tn
    tm = min(tm, T)
    nj = nb // jb
    sub = min(256, tm)
    dims = (((1,), (0,)), ((), ())) if transposed else _NT

    def body(a_ref, w_ref, x_ref, g_ref, r_ref, o_ref, dg_ref):
        part = lax.dot_general(a_ref[:, :tn], w_ref[0], dims, preferred_element_type=F32)
        for jj in range(1, jb):
            part += lax.dot_general(a_ref[:, jj * tn:(jj + 1) * tn], w_ref[jj], dims, preferred_element_type=F32)

        @pl.when(pl.program_id(1) == 0)
        def _():
            o_ref[...] = part

        @pl.when(pl.program_id(1) > 0)
        def _():
            o_ref[...] += part

        @pl.when(pl.program_id(1) == nj - 1)
        def _():
            dg = jnp.zeros((1, K), F32)
            for r in range(tm // sub):
                rows = pl.ds(r * sub, sub)
                _, vjp = jax.vjp(_rms, x_ref[rows, :], g_ref[...])
                dx, dg_r = vjp(o_ref[rows, :])
                o_ref[rows, :] = r_ref[rows, :] + dx
                dg = dg + dg_r
            _acc_out(pl.program_id(0) == 0, (dg_ref,), (dg,))

    row = pl.BlockSpec((tm, K), lambda i, j: (i, 0))
    vec = pl.BlockSpec((1, K), lambda i, j: (0, 0))
    return _call(
        body, (a, w, x, g, dres), grid=(T // tm, nj), name=name, items=items,
        in_specs=[pl.BlockSpec((tm, jb * tn), lambda i, j: (i, j)),
                  pl.BlockSpec((None, jb) + w.shape[2:], lambda i, j: (l, block_of(j), 0, 0)), row, vec, row],
        out_specs=[row, vec],
        out_shape=[jax.ShapeDtypeStruct((T, K), F32), jax.ShapeDtypeStruct((1, K), F32)],
        sem=("arbitrary", "arbitrary"))


def _mm_tn_cols(lhs, rhs, tn, jb, name, tm=1024, items=(), block_of=lambda j: j):
    T, K = lhs.shape
    nb = rhs.shape[1] // tn
    tm = min(tm, T)
    nt = T // tm

    def body(l_ref, r_ref, o_ref, acc):
        part = lax.dot_general(l_ref[...], r_ref[...], _TN, preferred_element_type=F32)

        @pl.when(pl.program_id(1) == 0)
        def _():
            acc[...] = part

        @pl.when(pl.program_id(1) > 0)
        def _():
            acc[...] += part

        @pl.when(pl.program_id(1) == nt - 1)
        def _():
            for jj in range(jb):
                o_ref[jj] = acc[:, jj * tn:(jj + 1) * tn].astype(BF16)

    return _call(
        body, (lhs, rhs), grid=(nb // jb, nt), name=name, items=items,
        in_specs=[pl.BlockSpec((tm, K), lambda j, t: (t, 0)), pl.BlockSpec((tm, jb * tn), lambda j, t: (t, j))],
        out_specs=[pl.BlockSpec((jb, K, tn), lambda j, t: (block_of(j), 0, 0))],
        out_shape=[jax.ShapeDtypeStruct((nb, K, tn), BF16)],
        scratch_shapes=[pltpu.VMEM((K, jb * tn), F32)],
        sem=("parallel", "arbitrary"))


def _mm_tn_rows(lhs, rhs, tk, name, tm=1024, items=(), block_of=lambda j: j):
    T, Kl = lhs.shape
    N = rhs.shape[1]
    tm = min(tm, T)
    nt = T // tm

    def body(l_ref, r_ref, o_ref, acc):
        part = lax.dot_general(l_ref[...], r_ref[...].astype(BF16), _TN, preferred_element_type=F32)

        @pl.when(pl.program_id(1) == 0)
        def _():
            acc[...] = part

        @pl.when(pl.program_id(1) > 0)
        def _():
            acc[...] += part

        @pl.when(pl.program_id(1) == nt - 1)
        def _():
            o_ref[...] = acc[...].astype(BF16)

    return _call(
        body, (lhs, rhs), grid=(Kl // tk, nt), name=name, items=items,
        in_specs=[pl.BlockSpec((tm, tk), lambda l, t: (t, l)), pl.BlockSpec((tm, N), lambda l, t: (t, 0))],
        out_specs=[pl.BlockSpec((tk, N), lambda l, t: (block_of(l), 0))],
        out_shape=[jax.ShapeDtypeStruct((Kl, N), BF16)],
        scratch_shapes=[pltpu.VMEM((tk, N), F32)],
        sem=("parallel", "arbitrary"))


N_FF_CHUNK = D_FF_PAD // FF_PAD


def _ffn_chunk_block(j):
    return (j % 2) * N_FF_CHUNK + j // 2


def _ffn_up_act(x, g, w, name, tm=1024, items=()):
    T, K = x.shape
    tm = min(tm, T)

    def body(x_ref, g_ref, wg_ref, wu_ref, gu_ref, act_ref, h_ref):
        @pl.when(pl.program_id(1) == 0)
        def _():
            h_ref[...] = _rms(x_ref[...], g_ref[...]).astype(BF16)
        h = h_ref[...]
        gate = lax.dot_general(h, wg_ref[...], _NT, preferred_element_type=F32)
        up = lax.dot_general(h, wu_ref[...], _NT, preferred_element_type=F32)
        gu_ref[:, :FF_PAD] = gate.astype(BF16)
        gu_ref[:, FF_PAD:] = up.astype(BF16)
        act_ref[...] = (gate * _sigmoid(gate) * up).astype(BF16)

    return _call(
        body, (x, g, w, w), grid=(T // tm, N_FF_CHUNK), name=name, items=items,
        in_specs=[pl.BlockSpec((tm, K), lambda i, j: (i, 0)), pl.BlockSpec((1, K), lambda i, j: (0, 0)),
                  pl.BlockSpec((None, None, FF_PAD, K), lambda i, j: (0, j, 0, 0)),
                  pl.BlockSpec((None, None, FF_PAD, K), lambda i, j: (0, j + N_FF_CHUNK, 0, 0))],
        out_specs=[pl.BlockSpec((tm, 2 * FF_PAD), lambda i, j: (i, j)), pl.BlockSpec((tm, FF_PAD), lambda i, j: (i, j)),
                   pl.BlockSpec((tm, K), lambda i, j: (i, 0))],
        out_shape=[jax.ShapeDtypeStruct((T, 2 * D_FF_PAD), BF16), jax.ShapeDtypeStruct((T, D_FF_PAD), BF16),
                   jax.ShapeDtypeStruct((T, K), BF16)],
        sem=("parallel", "arbitrary"))


def _ffn_down_dx(dxo, w, gu, name, tm=1024):
    T, K = dxo.shape
    tm = min(tm, T)

    sub = min(512, tm)

    def body(d_ref, w_ref, gu_ref, o_ref):
        w = w_ref[...]
        for r in range(tm // sub):
            rows = slice(r * sub, (r + 1) * sub)
            d = lax.dot_general(d_ref[rows, :].astype(BF16), w, _NT, preferred_element_type=F32)
            gate = gu_ref[rows, :FF_PAD].astype(F32)
            up = gu_ref[rows, FF_PAD:].astype(F32)
            sig = _sigmoid(gate)
            silu = gate * sig
            o_ref[rows, :FF_PAD] = (d * up * (sig + silu * (1.0 - sig))).astype(BF16)
            o_ref[rows, FF_PAD:] = (d * silu).astype(BF16)

    return pl.pallas_call(
        body, grid=(T // tm, N_FF_CHUNK), name=name,
        in_specs=[pl.BlockSpec((tm, K), lambda i, j: (i, 0)), pl.BlockSpec((None, FF_PAD, K), lambda i, j: (0, j, 0)),
                  pl.BlockSpec((tm, 2 * FF_PAD), lambda i, j: (i, j))],
        out_specs=pl.BlockSpec((tm, 2 * FF_PAD), lambda i, j: (i, j)),
        out_shape=jax.ShapeDtypeStruct((T, 2 * D_FF_PAD), BF16),
        compiler_params=_cp("parallel", "parallel"))(dxo, w, gu)


def _loss_head(x, g, target, name, tm=512):
    T, K = x.shape

    def loss_fn(xv, gv, tv):
        err = _rms(xv, gv) - tv
        return 0.5 * jnp.sum(jnp.mean(err * err, axis=-1, keepdims=True), axis=0, keepdims=True)

    def body(x_ref, g_ref, t_ref, l_ref, dx_ref, dg_ref):
        val, vjp = jax.vjp(lambda xv, gv: loss_fn(xv, gv, t_ref[...]), x_ref[...], g_ref[...])
        dx, dg = vjp(jnp.ones((1, 1), F32))
        dx_ref[...] = dx
        lval = jnp.broadcast_to(val, (1, 128))

        @pl.when(pl.program_id(0) == 0)
        def _():
            dg_ref[...] = dg
            l_ref[...] = lval

        @pl.when(pl.program_id(0) > 0)
        def _():
            dg_ref[...] += dg
            l_ref[...] += lval

    row = pl.BlockSpec((tm, K), lambda i: (i, 0))
    vec = pl.BlockSpec((1, K), lambda i: (0, 0))
    return pl.pallas_call(
        body, grid=(T // tm,), name=name,
        in_specs=[row, vec, row], out_specs=[pl.BlockSpec((1, 128), lambda i: (0, 0)), row, vec],
        out_shape=[jax.ShapeDtypeStruct((1, 128), F32), jax.ShapeDtypeStruct((T, K), F32),
                   jax.ShapeDtypeStruct((1, K), F32)],
        compiler_params=_cp("arbitrary"))(x, g, target)


SGU_ROWS = 4 * BLOCK


def _sgu_fn(u, v, wm, bt, gain):
    ug = jax.nn.gelu(u)
    vg = jax.nn.gelu(v)
    row = lax.broadcasted_iota(jnp.int32, (BLOCK, BLOCK), 0)
    col = lax.broadcasted_iota(jnp.int32, (BLOCK, BLOCK), 1)
    tri = row >= col
    normed = []
    for h in range(GROUP_HEADS):
        vh = vg[:, h * HEAD_DIM:(h + 1) * HEAD_DIM]
        xc = vh - jnp.mean(vh, axis=-1, keepdims=True)
        normed.append(xc * lax.rsqrt(jnp.mean(xc * xc, axis=-1, keepdims=True) + EPS))
    vn = jnp.concatenate(normed, axis=1)
    wcat = jnp.concatenate([jnp.where(tri, wm[h], 0.0) for h in range(GROUP_HEADS)], axis=1)
    bias = jnp.concatenate([jnp.broadcast_to(bt[:, h:h + 1], (BLOCK, HEAD_DIM)) for h in range(GROUP_HEADS)], axis=1)
    mixes = []
    for c in range(u.shape[0] // BLOCK):
        chunk = vn[c * BLOCK:(c + 1) * BLOCK]
        stacked = jnp.concatenate([jnp.where(_head_mask(h, chunk.shape), chunk, 0.0) for h in range(GROUP_HEADS)], axis=0)
        mixes.append(_bdot(wcat, stacked) + bias)
    return _rms(ug * jnp.concatenate(mixes, axis=0), gain)


def _sgu_fwd(proj, wm, bt, gain, name, items=()):
    T = proj.shape[0]
    rows = min(SGU_ROWS, T)

    def body(u_ref, v_ref, w_ref, b_ref, g_ref, o_ref):
        o_ref[...] = _sgu_fn(u_ref[...], v_ref[...], w_ref[...], b_ref[...], g_ref[...]).astype(BF16)

    full = lambda shape: pl.BlockSpec(shape, lambda i: (0,) * len(shape))
    return _call(
        body, (proj, proj, wm, bt, gain), grid=(T // rows,), name=name, items=items,
        in_specs=[pl.BlockSpec((rows, GROUP_WIDTH), lambda i: (i, 0)), pl.BlockSpec((rows, GROUP_WIDTH), lambda i: (i, 1)),
                  full(wm.shape), full(bt.shape), full(gain.shape)],
        out_specs=[pl.BlockSpec((rows, GROUP_WIDTH), lambda i: (i, 0))],
        out_shape=[jax.ShapeDtypeStruct((T, GROUP_WIDTH), BF16)],
        sem=("parallel",))


def _acc_out(first, refs, vals):
    @pl.when(first)
    def _():
        for r, v in zip(refs, vals):
            r[...] = v

    @pl.when(jnp.logical_not(first))
    def _():
        for r, v in zip(refs, vals):
            r[...] += v


def _sgu_bwd(proj, wm, bt, gain, dy, name, items=()):
    T = proj.shape[0]

    def body(u_ref, v_ref, w_ref, b_ref, g_ref, dy_ref, duv_ref, dw_ref, db_ref, dg_ref):
        _, vjp = jax.vjp(_sgu_fn, u_ref[...], v_ref[...], w_ref[...], b_ref[...], g_ref[...])
        du, dv, dw, db, dg = vjp(dy_ref[...])
        duv_ref[:, :GROUP_WIDTH] = du.astype(BF16)
        duv_ref[:, GROUP_WIDTH:] = dv.astype(BF16)
        _acc_out(pl.program_id(0) == 0, (dw_ref, db_ref, dg_ref), (dw, db, dg))

    full = lambda shape: pl.BlockSpec(shape, lambda i: (0,) * len(shape))
    rows = min(SGU_ROWS, T)
    return _call(
        body, (proj, proj, wm, bt, gain, dy), grid=(T // rows,), name=name, items=items,
        in_specs=[pl.BlockSpec((rows, GROUP_WIDTH), lambda i: (i, 0)), pl.BlockSpec((rows, GROUP_WIDTH), lambda i: (i, 1)),
                  full(wm.shape), full(bt.shape), full(gain.shape),
                  pl.BlockSpec((rows, GROUP_WIDTH), lambda i: (i, 0))],
        out_specs=[pl.BlockSpec((rows, 2 * GROUP_WIDTH), lambda i: (i, 0)), full(wm.shape), full(bt.shape), full(gain.shape)],
        out_shape=[jax.ShapeDtypeStruct((T, 2 * GROUP_WIDTH), BF16), jax.ShapeDtypeStruct(wm.shape, F32),
                   jax.ShapeDtypeStruct(bt.shape, F32), jax.ShapeDtypeStruct(gain.shape, F32)],
        sem=("arbitrary",))


def _pool_consts(seq):
    t = lax.broadcasted_iota(jnp.int32, (seq, GROUP_WIDTH), 0)
    grp = lax.broadcasted_iota(jnp.int32, (seq, GROUP_WIDTH), 1) // (GROUP_WIDTH // len(POOL_WINDOWS))
    win = jnp.where(grp == 0, POOL_WINDOWS[0], jnp.where(grp == 1, POOL_WINDOWS[1],
                    jnp.where(grp == 2, POOL_WINDOWS[2], POOL_WINDOWS[3])))
    count = jnp.minimum(t + 1, win).astype(F32)
    return t, grp, count


def _by_group(grp, vals):
    return jnp.where(grp == 0, vals[0], jnp.where(grp == 1, vals[1], jnp.where(grp == 2, vals[2], vals[3])))


def _window_sums(x, t, seq, back):
    def shift(a, k):
        if back:
            return jnp.where(t >= k, pltpu.roll(a, k, 0), 0.0)
        return jnp.where(t < seq - k, pltpu.roll(a, seq - k, 0), 0.0)
    sums = []
    a, k = x, 1
    for _ in POOL_WINDOWS:
        a = a + shift(a, k)
        sums.append(a)
        k *= 2
    return sums


def _pool_tail(y, wbd, scale, gain):
    return _rms(_bdot(y, wbd) * scale, gain)


def _pool_fwd(proj, wbd, scale, gain, seq, name):
    T = proj.shape[0]

    def body(p_ref, w_ref, s_ref, g_ref, o_ref):
        p = p_ref[...]
        t, grp, count = _pool_consts(seq)
        y = _by_group(grp, _window_sums(p, t, seq, True)) / count - p
        o_ref[...] = _pool_tail(y, w_ref[...], s_ref[...], g_ref[...]).astype(BF16)

    full = lambda shape: pl.BlockSpec(shape, lambda b: (0,) * len(shape))
    return pl.pallas_call(
        body, grid=(T // seq,), name=name,
        in_specs=[pl.BlockSpec((seq, GROUP_WIDTH), lambda b: (b, 2)), full(wbd.shape), full(scale.shape), full(gain.shape)],
        out_specs=pl.BlockSpec((seq, GROUP_WIDTH), lambda b: (b, 0)),
        out_shape=jax.ShapeDtypeStruct((T, GROUP_WIDTH), BF16),
        compiler_params=_cp("parallel"))(proj, wbd, scale, gain)


def _pool_bwd(proj, wbd, scale, gain, dy, seq, name):
    T = proj.shape[0]

    def body(p_ref, w_ref, s_ref, g_ref, dy_ref, dp_ref, dw_ref, ds_ref, dg_ref):
        p = p_ref[...]
        t, grp, count = _pool_consts(seq)
        y = _by_group(grp, _window_sums(p, t, seq, True)) / count - p
        _, vjp = jax.vjp(_pool_tail, y, w_ref[...], s_ref[...], g_ref[...])
        d_y, dw, ds, dg = vjp(dy_ref[...])
        dp = _by_group(grp, _window_sums(d_y / count, t, seq, False)) - d_y
        dp_ref[...] = dp.astype(BF16)
        _acc_out(pl.program_id(0) == 0, (dw_ref, ds_ref, dg_ref), (dw, ds, dg))

    full = lambda shape: pl.BlockSpec(shape, lambda b: (0,) * len(shape))
    return pl.pallas_call(
        body, grid=(T // seq,), name=name,
        in_specs=[pl.BlockSpec((seq, GROUP_WIDTH), lambda b: (b, 2)), full(wbd.shape), full(scale.shape), full(gain.shape),
                  pl.BlockSpec((seq, GROUP_WIDTH), lambda b: (b, 1))],
        out_specs=[pl.BlockSpec((seq, GROUP_WIDTH), lambda b: (b, 0)), full(wbd.shape), full(scale.shape), full(gain.shape)],
        out_shape=[jax.ShapeDtypeStruct((T, GROUP_WIDTH), BF16), jax.ShapeDtypeStruct(wbd.shape, F32),
                   jax.ShapeDtypeStruct(scale.shape, F32), jax.ShapeDtypeStruct(gain.shape, F32)],
        compiler_params=_cp("arbitrary"))(proj, wbd, scale, gain, dy)


def _swa_fn(q, kv_prev, kv_cur, sinks, tab, gain, first):
    half = GROUP_WIDTH // 2
    k2 = jnp.concatenate([kv_prev[:, :half], kv_cur[:, :half]], axis=0)
    v2 = jnp.concatenate([kv_prev[:, half:], kv_cur[:, half:]], axis=0)
    per_query_head = lambda a: jnp.concatenate(
        [a[:, (h // 2) * HEAD_DIM:(h // 2 + 1) * HEAD_DIM] for h in range(GROUP_HEADS)], axis=1)
    qs = jnp.concatenate([jnp.where(_head_mask(h, q.shape), q, 0.0) for h in range(GROUP_HEADS)], axis=0)
    qi = lax.broadcasted_iota(jnp.int32, (HEAD_ROWS, 2 * BLOCK), 0) & (BLOCK - 1)
    kj = lax.broadcasted_iota(jnp.int32, (HEAD_ROWS, 2 * BLOCK), 1)
    dist = qi + BLOCK - kj
    mask = (dist >= 0) & (dist < BLOCK) & ((kj >= BLOCK) | jnp.logical_not(first))
    logits = _bdot_nt(qs, per_query_head(k2)) * ATT_SCALE + tab.reshape(HEAD_ROWS, 2 * BLOCK)
    logits = jnp.where(mask, logits, -1e30)
    sink = jnp.concatenate([jnp.broadcast_to(sinks[:, h:h + 1], (BLOCK, 1)) for h in range(GROUP_HEADS)], axis=0)
    m = lax.stop_gradient(jnp.maximum(jnp.max(logits, axis=1, keepdims=True), sink))
    p = jnp.exp(logits - m)
    probs = p / (jnp.sum(p, axis=1, keepdims=True) + jnp.exp(sink - m))
    out = _bdot(probs, per_query_head(v2))
    y = jnp.zeros_like(q)
    for h in range(GROUP_HEADS):
        y = y + jnp.where(_head_mask(h, q.shape), out[h * BLOCK:(h + 1) * BLOCK], 0.0)
    return _rms(y, gain)


def _swa_specs(nblk):
    q = pl.BlockSpec((BLOCK, GROUP_WIDTH), lambda b, i: (b * nblk + i, 3))
    cur = pl.BlockSpec((BLOCK, GROUP_WIDTH), lambda b, i: (b * nblk + i, 4))
    prev = pl.BlockSpec((BLOCK, GROUP_WIDTH), lambda b, i: (b * nblk + jnp.maximum(i - 1, 0), 4))
    return q, prev, cur


def _swa_fwd(proj, sinks, tab, gain, seq, name, items=()):
    T = proj.shape[0]
    nblk = seq // BLOCK

    def body(q_ref, kp_ref, kc_ref, s_ref, t_ref, g_ref, o_ref):
        o_ref[...] = _swa_fn(q_ref[...], kp_ref[...], kc_ref[...], s_ref[...], t_ref[...], g_ref[...],
                             pl.program_id(1) == 0).astype(BF16)

    full = lambda shape: pl.BlockSpec(shape, lambda b, i: (0,) * len(shape))
    return _call(
        body, (proj, proj, proj, sinks, tab, gain), grid=(T // seq, nblk), name=name, items=items,
        in_specs=[*_swa_specs(nblk), full(sinks.shape), full(tab.shape), full(gain.shape)],
        out_specs=[pl.BlockSpec((BLOCK, GROUP_WIDTH), lambda b, i: (b * nblk + i, 0))],
        out_shape=[jax.ShapeDtypeStruct((T, GROUP_WIDTH), BF16)],
        sem=("parallel", "parallel"))


def _swa_bwd(proj, sinks, tab, gain, dy, seq, name, items=()):
    T = proj.shape[0]
    nblk = seq // BLOCK

    def body(q_ref, kp_ref, kc_ref, s_ref, t_ref, g_ref, dy_ref, dq_ref, dkp_ref, dkc_ref, ds_ref, dt_ref, dg_ref):
        first = pl.program_id(1) == 0
        fn = functools.partial(_swa_fn, first=first)
        _, vjp = jax.vjp(fn, q_ref[...], kp_ref[...], kc_ref[...], s_ref[...], t_ref[...], g_ref[...])
        dq, dkp, dkc, ds, dt, dg = vjp(dy_ref[...])
        dq_ref[...] = dq.astype(BF16)
        dkp_ref[...] = dkp
        dkc_ref[...] = dkc
        _acc_out((pl.program_id(0) == 0) & first, (ds_ref, dt_ref, dg_ref), (ds, dt, dg))

    full = lambda shape: pl.BlockSpec(shape, lambda b, i: (0,) * len(shape))
    blk = lambda c: pl.BlockSpec((BLOCK, GROUP_WIDTH), lambda b, i: (b * nblk + i, c))
    return _call(
        body, (proj, proj, proj, sinks, tab, gain, dy), grid=(T // seq, nblk), name=name, items=items,
        in_specs=[*_swa_specs(nblk), full(sinks.shape), full(tab.shape), full(gain.shape), blk(2)],
        out_specs=[blk(0), blk(0), blk(0), full(sinks.shape), full(tab.shape), full(gain.shape)],
        out_shape=[jax.ShapeDtypeStruct((T, GROUP_WIDTH), BF16), jax.ShapeDtypeStruct((T, GROUP_WIDTH), F32),
                   jax.ShapeDtypeStruct((T, GROUP_WIDTH), F32), jax.ShapeDtypeStruct(sinks.shape, F32),
                   jax.ShapeDtypeStruct(tab.shape, F32), jax.ShapeDtypeStruct(gain.shape, F32)],
        sem=("arbitrary", "arbitrary"))


def _shift_add(cur, prev, seq, name):
    T = cur.shape[0]
    nblk = seq // BLOCK

    def body(c_ref, p_ref, o_ref):
        last = pl.program_id(1) == nblk - 1
        o_ref[...] = (c_ref[...] + jnp.where(last, 0.0, p_ref[...])).astype(BF16)

    return pl.pallas_call(
        body, grid=(T // seq, nblk), name=name,
        in_specs=[pl.BlockSpec((BLOCK, GROUP_WIDTH), lambda b, i: (b * nblk + i, 0)),
                  pl.BlockSpec((BLOCK, GROUP_WIDTH), lambda b, i: (b * nblk + jnp.minimum(i + 1, nblk - 1), 0))],
        out_specs=pl.BlockSpec((BLOCK, GROUP_WIDTH), lambda b, i: (b * nblk + i, 0)),
        out_shape=jax.ShapeDtypeStruct((T, GROUP_WIDTH), BF16),
        compiler_params=_cp("parallel", "parallel"))(cur, prev)


def _t5_bucket(dist):
    max_exact = N_BUCKETS // 2
    df = jnp.maximum(dist, 1).astype(F32)
    large = max_exact + (jnp.log(df / max_exact) / jnp.log(jnp.float32(MAX_DISTANCE / max_exact))
                         * (N_BUCKETS - max_exact)).astype(jnp.int32)
    return jnp.where(dist < max_exact, dist, jnp.minimum(large, N_BUCKETS - 1))


def _bucket_map():
    dist = (jnp.arange(BLOCK)[:, None] + BLOCK) - jnp.arange(2 * BLOCK)[None, :]
    return _t5_bucket(jnp.clip(dist, 0, BLOCK - 1)).astype(jnp.int32)


def _bias_table(rel_bias, buckets, name):
    def body(rb_ref, bk_ref, o_ref):
        bk = bk_ref[...]
        rb = rb_ref[...]
        for h in range(GROUP_HEADS):
            acc = jnp.zeros((BLOCK, 2 * BLOCK), F32)
            for b in range(N_BUCKETS):
                acc = jnp.where(bk == b, rb[b:b + 1, h:h + 1], acc)
            o_ref[h] = acc

    return pl.pallas_call(body, name=name, out_shape=jax.ShapeDtypeStruct((GROUP_HEADS, BLOCK, 2 * BLOCK), F32),
                          compiler_params=_cp())(rel_bias, buckets)


def _bias_table_bwd(dtab, buckets, name):
    def body(dt_ref, bk_ref, o_ref):
        bk = bk_ref[...]
        row = lax.broadcasted_iota(jnp.int32, (N_BUCKETS, GROUP_HEADS), 0)
        col = lax.broadcasted_iota(jnp.int32, (N_BUCKETS, GROUP_HEADS), 1)
        acc = jnp.zeros((N_BUCKETS, GROUP_HEADS), F32)
        for h in range(GROUP_HEADS):
            dt = dt_ref[h]
            for b in range(N_BUCKETS):
                s = jnp.sum(jnp.where(bk == b, dt, 0.0), keepdims=True)
                acc = acc + jnp.where((row == b) & (col == h), s, 0.0)
        o_ref[...] = acc

    return pl.pallas_call(body, name=name, out_shape=jax.ShapeDtypeStruct((N_BUCKETS, GROUP_HEADS), F32),
                          compiler_params=_cp())(dtab, buckets)


HEAD_ROWS = GROUP_HEADS * BLOCK


def _stack_heads(x):
    return jnp.concatenate([jnp.where(_head_mask(h, x.shape), x, 0.0) for h in range(GROUP_HEADS)], axis=0).astype(BF16)


def _sb_tile(qs, kb, q0, k0):
    z = lax.dot_general(qs, kb, _NT, preferred_element_type=F32)
    row = lax.broadcasted_iota(jnp.int32, (HEAD_ROWS, BLOCK), 0) & (BLOCK - 1)
    col = lax.broadcasted_iota(jnp.int32, (HEAD_ROWS, BLOCK), 1)
    causal = (k0 + col) < (q0 + row)
    ls_neg = -(jnp.maximum(z, 0.0) + jnp.log(1.0 + jnp.exp(-jnp.abs(z))))
    return jnp.where(causal, ls_neg, 0.0), ls_neg + z, causal


SB_DEAD = -70.0
SB_FIRST_LANE = GROUP_HEADS


def _tri(strict_upper_src):
    r = lax.broadcasted_iota(jnp.int32, (BLOCK, BLOCK), 0)
    c = lax.broadcasted_iota(jnp.int32, (BLOCK, BLOCK), 1)
    cond = {"gt": r > c, "le": r <= c, "lt": r < c}[strict_upper_src]
    return jnp.where(cond, 1.0, 0.0).astype(BF16)


def _sb_fwd(proj, gain, seq, name, items=()):
    T = proj.shape[0]
    nblk = seq // BLOCK

    def body(q_ref, k_ref, v_ref, g_ref, o_ref, raw_ref, bt_ref):
        i = pl.program_id(1)
        q = q_ref[...]
        u_gt = _tri("gt")
        lane = lax.broadcasted_iota(jnp.int32, (BLOCK, BLOCK), 1)
        heads = [slice(h * HEAD_DIM, (h + 1) * HEAD_DIM) for h in range(GROUP_HEADS)]
        rows = [slice(h * BLOCK, (h + 1) * BLOCK) for h in range(GROUP_HEADS)]
        qs = _stack_heads(q * ATT_SCALE)

        def live(carry):
            j, _, cb = carry
            return (j >= 0) & (jnp.max(cb) > SB_DEAD)

        def step(carry):
            j, accs, cb = carry
            ks = pl.multiple_of(j * BLOCK, BLOCK)
            kb = k_ref[pl.ds(ks, BLOCK), :].astype(BF16)
            vb = v_ref[pl.ds(ks, BLOCK), :].astype(BF16)
            b, a, causal = _sb_tile(qs, kb, i * BLOCK, j * BLOCK)
            tail = _split_dot(b, u_gt) + cb
            w = jnp.where(causal, jnp.exp(a + tail), 0.0).astype(BF16)
            accs = tuple(accs[h] + jnp.dot(w[rows[h]], vb[:, hs], preferred_element_type=F32)
                         for h, hs in enumerate(heads))
            return j - 1, accs, cb + jnp.sum(b, axis=1, keepdims=True)

        zero_acc = tuple(jnp.zeros((BLOCK, HEAD_DIM), F32) for _ in heads)
        j_end, accs, cb = lax.while_loop(live, step, (i, zero_acc, jnp.zeros((HEAD_ROWS, 1), F32)))
        side = jnp.where(lane == SB_FIRST_LANE, (j_end + 1).astype(F32), 0.0)
        for h in range(GROUP_HEADS):
            side = jnp.where(lane == h, cb[rows[h]], side)
        raw = jnp.concatenate(accs, axis=1)
        raw_ref[...] = raw
        bt_ref[...] = side
        o_ref[...] = _rms(raw, g_ref[...]).astype(BF16)

    return _call(
        body, (proj, proj, proj, gain), grid=(T // seq, nblk), name=name, items=items,
        in_specs=[pl.BlockSpec((BLOCK, GROUP_WIDTH), lambda b, i: (b * nblk + i, 5)),
                  pl.BlockSpec((seq, GROUP_WIDTH), lambda b, i: (b, 6)),
                  pl.BlockSpec((seq, GROUP_WIDTH), lambda b, i: (b, 7)),
                  pl.BlockSpec(gain.shape, lambda b, i: (0, 0))],
        out_specs=[pl.BlockSpec((BLOCK, GROUP_WIDTH), lambda b, i: (b * nblk + i, 0)),
                   pl.BlockSpec((BLOCK, GROUP_WIDTH), lambda b, i: (b * nblk + i, 0)),
                   pl.BlockSpec((BLOCK, BLOCK), lambda b, i: (b * nblk + i, 0))],
        out_shape=[jax.ShapeDtypeStruct((T, GROUP_WIDTH), BF16), jax.ShapeDtypeStruct((T, GROUP_WIDTH), F32),
                   jax.ShapeDtypeStruct((T, BLOCK), F32)],
        sem=("parallel", "parallel"))


def _sb_bwd(proj, gain, raw, btot, dy, seq, name, items=()):
    T = proj.shape[0]
    nblk = seq // BLOCK

    def body(q_ref, k_ref, v_ref, g_ref, raw_ref, bt_ref, dy_ref, dq_ref, dk_ref, dv_ref, dg_ref):
        i = pl.program_id(1)

        @pl.when(i == 0)
        def _():
            dk_ref[...] = jnp.zeros_like(dk_ref)
            dv_ref[...] = jnp.zeros_like(dv_ref)

        rawv = raw_ref[...]
        _, vjp = jax.vjp(_rms, rawv, g_ref[...])
        do, dg = vjp(dy_ref[...])
        _acc_out((pl.program_id(0) == 0) & (i == 0), (dg_ref,), (dg,))
        q = q_ref[...]
        bt = bt_ref[...]
        u_le = _tri("le")
        u_lt = _tri("lt")
        heads = [slice(h * HEAD_DIM, (h + 1) * HEAD_DIM) for h in range(GROUP_HEADS)]
        rows = [slice(h * BLOCK, (h + 1) * BLOCK) for h in range(GROUP_HEADS)]
        qs = _stack_heads(q * ATT_SCALE)
        dos = _stack_heads(do)
        bts = jnp.concatenate([bt[:, h:h + 1] for h in range(GROUP_HEADS)], axis=0)
        first = jnp.max(bt[:, SB_FIRST_LANE:SB_FIRST_LANE + 1]).astype(jnp.int32)
        first = jnp.minimum(jnp.maximum(first, 0), i)

        def step(j, carry):
            dqs, cb, cg = carry
            ks = pl.multiple_of(j * BLOCK, BLOCK)
            kb = k_ref[pl.ds(ks, BLOCK), :].astype(BF16)
            vb = v_ref[pl.ds(ks, BLOCK), :].astype(BF16)
            b, a, causal = _sb_tile(qs, kb, i * BLOCK, j * BLOCK)
            tail = bts - (_split_dot(b, u_le) + cb)
            w = jnp.where(causal, jnp.exp(a + tail), 0.0)
            sig = jnp.exp(a)
            g = w * lax.dot_general(dos, vb, _NT, preferred_element_type=F32)
            gpre = _split_dot(g, u_lt) + cg
            dz = jnp.where(causal, g * (1.0 - sig) - gpre * sig, 0.0).astype(BF16)
            dqs = tuple(dqs[h] + jnp.dot(dz[rows[h]], kb[:, hs], preferred_element_type=F32)
                        for h, hs in enumerate(heads))
            dk_ref[pl.ds(ks, BLOCK), :] += lax.dot_general(dz, qs, _TN, preferred_element_type=F32)
            dv_ref[pl.ds(ks, BLOCK), :] += lax.dot_general(w.astype(BF16), dos, _TN, preferred_element_type=F32)
            return dqs, cb + jnp.sum(b, axis=1, keepdims=True), cg + jnp.sum(g, axis=1, keepdims=True)

        zero_dq = tuple(jnp.zeros((BLOCK, HEAD_DIM), F32) for _ in heads)
        zero = jnp.zeros((HEAD_ROWS, 1), F32)
        dqs, _, _ = lax.fori_loop(first, i + 1, step, (zero_dq, zero, zero))
        dq_ref[...] = (jnp.concatenate(dqs, axis=1) * ATT_SCALE).astype(BF16)

    blk = lambda c: pl.BlockSpec((BLOCK, GROUP_WIDTH), lambda b, i: (b * nblk + i, c))
    seqblk = lambda c: pl.BlockSpec((seq, GROUP_WIDTH), lambda b, i: (b, c))
    vec = pl.BlockSpec(gain.shape, lambda b, i: (0, 0))
    return _call(
        body, (proj, proj, proj, gain, raw, btot, dy), grid=(T // seq, nblk), name=name, items=items,
        in_specs=[blk(5), seqblk(6), seqblk(7), vec, blk(0), pl.BlockSpec((BLOCK, BLOCK), lambda b, i: (b * nblk + i, 0)),
                  blk(3)],
        out_specs=[blk(0), seqblk(0), seqblk(0), vec],
        out_shape=[jax.ShapeDtypeStruct((T, GROUP_WIDTH), BF16), jax.ShapeDtypeStruct((T, GROUP_WIDTH), F32),
                   jax.ShapeDtypeStruct((T, GROUP_WIDTH), F32), jax.ShapeDtypeStruct(gain.shape, F32)],
        sem=("arbitrary", "arbitrary"))


def _layer_params(l, sgu_w, sgu_b, pool_w, pool_scale, swa_sinks, mix_out_gain, norm_mix, norm_ffn):
    gains = mix_out_gain[l].reshape(4, 1, GROUP_WIDTH)
    return dict(
        wm=sgu_w[l], bt=sgu_b[l].T,
        wbd=jax.scipy.linalg.block_diag(*[pool_w[l, g] for g in range(len(POOL_WINDOWS))]),
        scale=pool_scale[l][None], sinks=swa_sinks[l][None],
        gain=[gains[m] for m in range(4)], norm_mix=norm_mix[l][None], norm_ffn=norm_ffn[l][None])


def _as_weights(g_in, g_out, g_gu, g_down):
    return (g_in[None], g_out.reshape(1, D_MODEL, D_MODEL), g_gu[None], g_down.reshape(1, D_FF_PAD, D_MODEL))


def _gather_item(src, l, dst_shape, rows=None, down=False, init=None):
    r0, nr = rows or (0, src.shape[1])
    if down:
        place = lambda dst, k: dst.at[k // 2, pl.ds((k % 2) * DOWN_ROWS + r0, nr), :]
    else:
        place = lambda dst, k: dst.at[k, pl.ds(r0, nr), :]
    return _Item(src, lambda s, k: s.at[l, pl.ds(r0, nr), :], dst_shape, place, init)


def _layer_fwd(l, x, p, w, tab, seq, shards):
    win, wout, wgu, wd = w
    nl = l + 1
    s_in, s_out, s_gu, s_down = shards or (None,) * 4
    ride = lambda item: (item(),) if shards else ()
    half_down = DOWN_ROWS // 2
    proj, h1, *g_down = _norm_mm(x, p["norm_mix"], win, 0, 4, F32, f"in_proj_{l}", items=ride(
        lambda: _gather_item(s_down, nl, (4, FF_PAD, D_MODEL), (0, half_down), True,
                             jnp.zeros((4, FF_PAD, D_MODEL), BF16))))
    ya, *g_out = _sgu_fwd(proj, p["wm"], p["bt"], p["gain"][0], f"sgu_fwd_{l}",
                          items=ride(lambda: _gather_item(s_out, nl, (N_DEV,) + s_out.shape[1:])))
    yb = _pool_fwd(proj, p["wbd"], p["scale"], p["gain"][1], seq, f"pool_fwd_{l}")
    third = FF_PAD // 3
    gu_shape = (N_DEV, FF_PAD, D_MODEL)
    yc, *g_gu = _swa_fwd(proj, p["sinks"], tab, p["gain"][2], seq, f"swa_fwd_{l}", items=ride(
        lambda: _gather_item(s_gu, nl, gu_shape, (0, third))))
    yd, raw, btot, *g_gu = _sb_fwd(proj, p["gain"][3], seq, f"sb_fwd_{l}", items=ride(
        lambda: _gather_item(s_gu, nl, gu_shape, (third, third), init=g_gu[0])))
    ycat = jnp.concatenate([ya, yb, yc, yd], axis=1)
    xm, = _mm_res(x, ycat, wout, 0, f"out_proj_{l}", tn=D_MODEL)
    last = []
    if shards:
        last = [_gather_item(s_gu, nl, gu_shape, (2 * third, third), init=g_gu[0]),
                _gather_item(s_down, nl, (4, FF_PAD, D_MODEL), (half_down, half_down), True, g_down[0])]
    gu, act, h2, *both = _ffn_up_act(xm, p["norm_ffn"], wgu, f"ffn_up_{l}", items=last)
    g_gu, g_down = (both[:1], both[1:]) if shards else ([], [])
    xo, *g_in = _mm_res(xm, act, wd, 0, f"ffn_down_{l}", tn=D_MODEL // 2,
                        items=ride(lambda: _gather_item(s_in, nl, (N_DEV,) + s_in.shape[1:])))
    w_next = _as_weights(g_in[0], g_out[0], g_gu[0], g_down[0]) if shards else None
    return xo, (x, proj, h1, ycat, raw, btot, xm, gu, h2, act), w_next


def _layer_bwd(l, dxo, saved, p, w, tab, seq, ride=()):
    win, wout, wgu, wd = w
    x, proj, h1, ycat, raw, btot, xm, gu, h2, act = saved
    out_rows = D_MODEL // N_DEV
    half = FF_PAD // 2

    def rows_of(g, r0, init=None):
        cut = lambda a, k: a.at[k, pl.ds(r0, half), :]
        return _Item(g, cut, g.shape, cut, init)

    dgu = _ffn_down_dx(dxo, wd, gu, f"ffn_down_dx_{l}")
    g_wd, = _mm_tn_rows(act, dxo, FF_PAD, f"ffn_down_dw_{l}")
    send_down = _Item(g_wd, lambda src, k: src.at[pl.ds((k // 2) * FF_PAD + (k % 2) * DOWN_ROWS, DOWN_ROWS), :],
                      (N_DEV, DOWN_ROWS, D_MODEL))
    g_wgu, p_down = _mm_tn_rows(dgu, h2, FF_PAD, f"ffn_up_dw_{l}", items=(send_down,), block_of=_ffn_chunk_block)
    g_wgu = g_wgu.reshape(N_DEV, FF_PAD, D_MODEL)
    dxm, g_norm_ffn, p_gu = _mm_norm_bwd(dgu, wgu, 0, 1, xm, p["norm_ffn"], dxo, f"ffn_up_dx_{l}",
                                         items=(rows_of(g_wgu, 0),), block_of=_ffn_chunk_block, transposed=True)
    dycat = _mm_nt(dxm, wout, 0, F32, f"out_proj_dx_{l}")
    g_wout, = _mm_tn_rows(ycat, dxm, D_MODEL, f"out_proj_dw_{l}")
    send_out = _Item(g_wout, lambda src, k: src.at[pl.ds(k * out_rows, out_rows), :], (N_DEV, out_rows, D_MODEL))
    duv, g_wm, g_bt, g_ga = _sgu_bwd(proj, p["wm"], p["bt"], p["gain"][0], dycat, f"sgu_bwd_{l}")
    dp, g_wbd, g_scale, g_gb = _pool_bwd(proj, p["wbd"], p["scale"], p["gain"][1], dycat, seq, f"pool_bwd_{l}")
    dq, dkp, dkc, g_sinks, g_tab, g_gc, *rode = _swa_bwd(proj, p["sinks"], tab, p["gain"][2], dycat, seq,
                                                         f"swa_bwd_{l}", items=ride)
    dkv = _shift_add(dkc, dkp, seq, f"swa_dkv_{l}")
    dqd, dkd, dvd, g_gd, p_gu, p_out = _sb_bwd(proj, p["gain"][3], raw, btot, dycat, seq, f"sb_bwd_{l}",
                                               items=(rows_of(g_wgu, half, p_gu), send_out))
    dproj = jnp.concatenate([duv, dp, dq, dkv, dqd, dkd.astype(BF16), dvd.astype(BF16)], axis=1)
    g_win, = _mm_tn_cols(h1, dproj, GROUP_WIDTH, 4, f"in_proj_dw_{l}")
    dx, g_norm_mix, p_in = _mm_norm_bwd(dproj, win, 0, 4, x, p["norm_mix"], dxm, f"in_proj_dx_{l}",
                                        items=(_Item(g_win, lambda src, k: src.at[k], g_win.shape),))
    ng = len(POOL_WINDOWS)
    gd = GROUP_WIDTH // ng
    small = dict(
        sgu_w=g_wm, sgu_b=g_bt.T,
        pool_w=jnp.stack([g_wbd[g * gd:(g + 1) * gd, g * gd:(g + 1) * gd] for g in range(ng)]),
        pool_scale=g_scale[0], swa_sinks=g_sinks[0],
        mix_out_gain=jnp.concatenate([g_ga[0], g_gb[0], g_gc[0], g_gd[0]]),
        norm_mix=g_norm_mix[0], norm_ffn=g_norm_ffn[0])
    return dx, (p_in, p_out, p_gu, p_down), small, g_tab, rode


def _local_step(x, target, w0, shards, sgu_w, sgu_b, pool_w, pool_scale, swa_sinks, rel_bias, mix_out_gain,
                norm_mix, norm_ffn, norm_final, seq):
    buckets = _bucket_map()
    tab = _bias_table(rel_bias, buckets, "bias_table")
    params = [_layer_params(l, sgu_w, sgu_b, pool_w, pool_scale, swa_sinks, mix_out_gain, norm_mix, norm_ffn)
              for l in range(DEPTH)]
    saved, weights = [], [w0]
    for l in range(DEPTH):
        x, s, w_next = _layer_fwd(l, x, params[l], weights[l], tab, seq, shards if l + 1 < DEPTH else None)
        saved.append(s)
        weights.append(w_next)
    loss, dx, g_final = _loss_head(x, norm_final[None], target, "loss_head")
    big, small, g_tab, upper = [None] * DEPTH, [None] * DEPTH, None, None
    for l in reversed(range(DEPTH)):
        ride = ()
        if l == 0:
            mine = _pack(_upper_layers(small), _pack_rows(_upper_layers(small)))
            ride = (_Item(mine, lambda src, k: src, (N_DEV,) + mine.shape),)
        dx, big[l], small[l], t, rode = _layer_bwd(l, dx, saved[l], params[l], weights[l], tab, seq, ride)
        g_tab = t if g_tab is None else g_tab + t
        upper = rode[0] if rode else upper
    return loss, dx, big, small, _bias_table_bwd(g_tab, buckets, "bias_table_bwd"), g_final[0], upper


def _cast_pad(w, rows, name):
    L, r, c = w.shape

    def body(w_ref, o_ref):
        if rows != r:
            o_ref[...] = jnp.zeros_like(o_ref)
        o_ref[:r, :] = w_ref[...].astype(BF16)

    return pl.pallas_call(
        body, grid=(L,), name=name,
        in_specs=[pl.BlockSpec((None, r, c), lambda l: (l, 0, 0))],
        out_specs=pl.BlockSpec((None, rows, c), lambda l: (l, 0, 0)),
        out_shape=jax.ShapeDtypeStruct((L, rows, c), BF16),
        compiler_params=_cp("parallel"))(w)


def _down_rows(ref, k):
    return ref.at[:, k // 2, pl.ds((k % 2) * DOWN_ROWS, DOWN_ROWS), :]


def _all_gather_weights(s_in, s_out, s_gu, s_down):
    L = s_in.shape[0]
    shards = (s_in, s_out, s_gu, s_down)
    down_full = jnp.zeros((L, 4, FF_PAD, D_MODEL), BF16)
    n = len(shards)

    def body(i0, i1, i2, i3, _, o0, o1, o2, o3, send_sems, recv_sems, local_sems):
        srcs = (i0, i1, i2, i3)
        outs = (o0, o1, o2, o3)
        x, y, c, _me = _my_place()
        chips = [(1 - x, y), (x, 1 - y), (1 - x, 1 - y)]

        def place(t, dev):
            k = 4 * dev[0] + 2 * dev[1] + dev[2]
            return _down_rows(outs[t], k) if t == 3 else outs[t].at[:, k]

        def copy(slot, t, block, to, from_shard):
            return pltpu.make_async_remote_copy(
                src_ref=srcs[t] if from_shard else place(t, block), dst_ref=place(t, block),
                send_sem=send_sems.at[slot, t], recv_sem=recv_sems.at[slot, t], device_id=to, device_id_type=MESH)

        me, sibling = (x, y, c), (x, y, 1 - c)
        mine = [pltpu.make_async_copy(srcs[t], place(t, me), local_sems.at[t]) for t in range(n)]
        for cp in mine:
            cp.start()
        first = [copy(0, t, me, sibling, True) for t in range(n)]
        first += [copy(1 + j, t, me, (*chip, c), True) for j, chip in enumerate(chips) for t in range(n)]
        for cp in first:
            cp.start()
        passed = []
        for j, chip in enumerate(chips):
            for t in range(n):
                copy(1 + j, t, (*chip, c), me, False).wait_recv()
                cp = copy(4 + j, t, (*chip, c), sibling, False)
                cp.start()
                passed.append(cp)
        for t in range(n):
            copy(0, t, sibling, me, False).wait_recv()
        for j, chip in enumerate(chips):
            for t in range(n):
                copy(4 + j, t, (*chip, 1 - c), me, False).wait_recv()
        for cp in first + passed:
            cp.wait_send()
        for cp in mine:
            cp.wait()

    shapes = [jax.ShapeDtypeStruct((L, N_DEV) + s.shape[1:], BF16) for s in shards[:3]]
    shapes.append(jax.ShapeDtypeStruct(down_full.shape, BF16))
    return pl.pallas_call(
        body, name="all_gather_weights", out_shape=shapes,
        in_specs=[ANY] * 5, out_specs=[ANY] * 4, input_output_aliases={4: 3},
        scratch_shapes=[pltpu.SemaphoreType.DMA((7, n)), pltpu.SemaphoreType.DMA((7, n)), pltpu.SemaphoreType.DMA((n,))],
        )(*shards, down_full)


def _adamw(w, g, m, v):
    m = ADAM_B1 * m + (1.0 - ADAM_B1) * g
    v = ADAM_B2 * v + (1.0 - ADAM_B2) * jnp.square(g)
    m_hat = m / (1.0 - ADAM_B1 ** ADAM_STEP)
    v_hat = v / (1.0 - ADAM_B2 ** ADAM_STEP)
    delta = -ADAM_LR * (m_hat / (jnp.sqrt(v_hat) + ADAM_EPS) + ADAM_WD * w)
    return delta, m, v


def _adamw_sharded(parts, w, m, v, tr, name, items=()):
    L, r, c = w.shape
    cp = parts[0].shape[-1]
    nrow = r // tr

    def body(*refs):
        p_refs, (w_ref, m_ref, v_ref, g_ref, d_ref, nm_ref, nv_ref) = refs[:L], refs[L:]
        for k in range(L):
            @pl.when(pl.program_id(0) == k)
            def _(p_ref=p_refs[k]):
                g = p_ref[0, :, :c].astype(F32)
                for dev in range(1, N_DEV):
                    g = g + p_ref[dev, :, :c].astype(F32)
                delta, nm, nv = _adamw(w_ref[...], g, m_ref[...], v_ref[...])
                g_ref[...] = g
                d_ref[...] = delta
                nm_ref[...] = nm
                nv_ref[...] = nv

    def part_spec(k):
        return pl.BlockSpec((N_DEV, tr, cp),
                            lambda l, i: (0, jnp.where(l == k, i, jnp.where(l < k, 0, nrow - 1)), 0))

    blk = pl.BlockSpec((None, tr, c), lambda l, i: (l, i, 0))
    out = jax.ShapeDtypeStruct((L, r, c), F32)
    return _call(
        body, (*parts, w, m, v), grid=(L, nrow), name=name, items=items,
        in_specs=[part_spec(k) for k in range(L)] + [blk, blk, blk],
        out_specs=[blk] * 4, out_shape=[out] * 4, sem=("arbitrary", "arbitrary"))


def _adamw_small(parts, wmv, name):
    def body(p_ref, wmv_ref, g_ref, d_ref, nm_ref, nv_ref):
        g = p_ref[0]
        for k in range(1, N_DEV):
            g = g + p_ref[k]
        delta, nm, nv = _adamw(wmv_ref[0], g, wmv_ref[1], wmv_ref[2])
        g_ref[...] = g
        d_ref[...] = delta
        nm_ref[...] = nm
        nv_ref[...] = nv

    out = jax.ShapeDtypeStruct(wmv.shape[1:], F32)
    return pl.pallas_call(body, name=name, out_shape=[out] * 4, compiler_params=_cp())(parts, wmv)


LAYERED = ("sgu_w", "sgu_b", "pool_w", "pool_scale", "swa_sinks", "mix_out_gain", "norm_mix", "norm_ffn")
SHARED = ("rel_bias", "norm_final")


def _seg_rows(a):
    return -(-a.size // 128)


def _pack_rows(parts):
    return -(-sum(_seg_rows(p) for p in parts) // 8) * 8


def _upper_layers(per_layer):
    if isinstance(per_layer, dict):
        return [per_layer[k][1:] for k in LAYERED]
    return [jnp.stack([per_layer[l][k] for l in range(1, DEPTH)]) for k in LAYERED]


def _layer_zero(stacked):
    return [stacked[k][:1] for k in LAYERED] + [stacked[k] for k in SHARED]


def _pack(parts, rows):
    segs = [jnp.pad(p.reshape(-1), (0, _seg_rows(p) * 128 - p.size)).reshape(_seg_rows(p), 128) for p in parts]
    used = sum(s.shape[0] for s in segs)
    return jnp.concatenate(segs + [jnp.zeros((rows - used, 128), F32)], axis=0)


def _pack_groups(groups, rows):
    segs = []
    for parts in groups:
        segs += [jnp.pad(p.reshape(-1), (0, _seg_rows(p) * 128 - p.size)).reshape(_seg_rows(p), 128) for p in parts]
        segs.append(jnp.zeros((rows - sum(_seg_rows(p) for p in parts), 128), F32))
    return jnp.concatenate(segs, axis=0).reshape(len(groups), rows, 128)


def _unpack(buf, like):
    out, at = [], 0
    for a in like:
        out.append(buf[at:at + _seg_rows(a)].reshape(-1)[:a.size].reshape(a.shape))
        at += _seg_rows(a)
    return out


def kernel(x, w_in, w_out, sgu_w, sgu_b, pool_w, pool_scale, swa_sinks, rel_bias, mix_out_gain, norm_mix, norm_ffn, w_gate_up, w_down, norm_final, loss_target, m_w_in, m_w_out, m_sgu_w, m_sgu_b, m_pool_w, m_pool_scale, m_swa_sinks, m_rel_bias, m_mix_out_gain, m_norm_mix, m_norm_ffn, m_w_gate_up, m_w_down, m_norm_final, v_w_in, v_w_out, v_sgu_w, v_sgu_b, v_pool_w, v_pool_scale, v_swa_sinks, v_rel_bias, v_mix_out_gain, v_norm_mix, v_norm_ffn, v_w_gate_up, v_w_down, v_norm_final):
    bl, seq, _ = x.shape
    L = w_in.shape[0]
    gu_t, m_gu_t, v_gu_t = (jnp.swapaxes(a, 1, 2) for a in (w_gate_up, m_w_gate_up, v_w_gate_up))
    shards = (_cast_pad(w_in, D_MODEL, "shard_w_in"), _cast_pad(w_out, D_MODEL // N_DEV, "shard_w_out"),
              _cast_pad(gu_t, FF_PAD, "shard_w_gate_up"), _cast_pad(w_down, DOWN_ROWS, "shard_w_down"))
    first = _all_gather_weights(*[s[:1] for s in shards])
    w0 = _as_weights(*[f[0] for f in first])
    small_w = dict(sgu_w=sgu_w, sgu_b=sgu_b, pool_w=pool_w, pool_scale=pool_scale, swa_sinks=swa_sinks,
                   rel_bias=rel_bias, mix_out_gain=mix_out_gain, norm_mix=norm_mix, norm_ffn=norm_ffn,
                   norm_final=norm_final)
    small_m = dict(sgu_w=m_sgu_w, sgu_b=m_sgu_b, pool_w=m_pool_w, pool_scale=m_pool_scale, swa_sinks=m_swa_sinks,
                   rel_bias=m_rel_bias, mix_out_gain=m_mix_out_gain, norm_mix=m_norm_mix, norm_ffn=m_norm_ffn,
                   norm_final=m_norm_final)
    small_v = dict(sgu_w=v_sgu_w, sgu_b=v_sgu_b, pool_w=v_pool_w, pool_scale=v_pool_scale, swa_sinks=v_swa_sinks,
                   rel_bias=v_rel_bias, mix_out_gain=v_mix_out_gain, norm_mix=v_norm_mix, norm_ffn=v_norm_ffn,
                   norm_final=v_norm_final)
    loss, dx, big, small, g_rel_bias, g_final, upper = _local_step(
        x.reshape(bl * seq, D_MODEL), loss_target.reshape(bl * seq, D_MODEL), w0, shards, sgu_w, sgu_b, pool_w,
        pool_scale, swa_sinks, rel_bias, mix_out_gain, norm_mix, norm_ffn, norm_final, seq)
    p_in, p_out, p_gu, p_down = ([big[l][t] for l in range(L)] for t in range(4))
    outs_gu = [jnp.swapaxes(a, 1, 2) for a in
               _adamw_sharded(p_gu, gu_t, m_gu_t, v_gu_t, FF_SHARD // 4, "adamw_w_gate_up")]
    outs_down = _adamw_sharded(p_down, w_down, m_w_down, v_w_down, DOWN_ROWS // 2, "adamw_w_down")
    outs_in = _adamw_sharded(p_in, w_in, m_w_in, v_w_in, 256, "adamw_w_in")
    lo_like = _layer_zero(small_w)
    lo_rows = _pack_rows(lo_like + [loss[0]])
    lo_mine = _pack([small[0][k][None] for k in LAYERED] + [g_rel_bias, g_final, loss[0]], lo_rows)
    *outs_out, lower = _adamw_sharded(p_out, w_out, m_w_out, v_w_out, D_MODEL // N_DEV, "adamw_w_out",
                                      items=(_Item(lo_mine, lambda src, k: src, (N_DEV,) + lo_mine.shape),))
    lo_res = _adamw_small(lower, _pack_groups([lo_like, _layer_zero(small_m), _layer_zero(small_v)], lo_rows),
                          "adamw_small_layer0")
    hi_like = _upper_layers(small_w)
    hi_rows = _pack_rows(hi_like)
    hi_res = _adamw_small(upper, _pack_groups([hi_like, _upper_layers(small_m), _upper_layers(small_v)], hi_rows),
                          "adamw_small_upper")
    loss_total = lo_res[0][sum(_seg_rows(a) for a in lo_like), 0]
    small_outs = []
    for lo_buf, hi_buf in zip(lo_res, hi_res):
        lo = dict(zip(LAYERED + SHARED, _unpack(lo_buf, lo_like)))
        hi = dict(zip(LAYERED, _unpack(hi_buf, hi_like)))
        small_outs.append({k: jnp.concatenate([lo[k], hi[k]], axis=0) if k in hi else lo[k] for k in lo})
    big_outs = dict(w_in=outs_in, w_out=outs_out, w_gate_up=outs_gu, w_down=outs_down)
    order = ("w_in", "w_out", "sgu_w", "sgu_b", "pool_w", "pool_scale", "swa_sinks", "rel_bias", "mix_out_gain",
             "norm_mix", "norm_ffn", "w_gate_up", "w_down", "norm_final")
    result = [loss_total, dx.reshape(bl, seq, D_MODEL)]
    for which in range(4):
        for name in order:
            result.append(big_outs[name][which] if name in big_outs else small_outs[which][name])
    return tuple(result)
```

```python
import functools

import jax
import jax.numpy as jnp
from jax import lax
from jax.experimental import pallas as pl
from jax.experimental.pallas import tpu as pltpu

F32 = jnp.float32
BF16 = jnp.bfloat16

N_DEV = 8
DEPTH = 4
D_MODEL = 1024
GROUP_WIDTH = 256
HEAD_DIM = 64
GROUP_HEADS = 4
BLOCK = 128
N_BUCKETS = 32
MAX_DISTANCE = 128
POOL_WINDOWS = (2, 4, 8, 16)
D_FF = 2816
FF_SHARD = D_FF // 4
FF_PAD = 768
D_FF_PAD = 4 * FF_PAD
EPS = 1e-6
ATT_SCALE = HEAD_DIM ** -0.5
ADAM_LR = 0.001
ADAM_B1 = 0.9
ADAM_B2 = 0.999
ADAM_EPS = 1e-08
ADAM_WD = 0.01
ADAM_STEP = 10
VMEM_LIMIT = 56 * 1024 * 1024
MESH_AXES = ("x", "y", "c")


def _cp(*sem):
    return pltpu.CompilerParams(dimension_semantics=sem or None, vmem_limit_bytes=VMEM_LIMIT)


_NT = (((1,), (1,)), ((), ()))
_TN = (((0,), (0,)), ((), ()))


@jax.custom_vjp
def _bdot(a, b):
    return jnp.dot(a.astype(BF16), b.astype(BF16), preferred_element_type=F32)


def _bdot_fwd(a, b):
    return _bdot(a, b), (a.astype(BF16), b.astype(BF16))


def _bdot_bwd(res, ct):
    a, b = res
    c = ct.astype(BF16)
    return (lax.dot_general(c, b, _NT, preferred_element_type=F32),
            lax.dot_general(a, c, _TN, preferred_element_type=F32))


_bdot.defvjp(_bdot_fwd, _bdot_bwd)


@jax.custom_vjp
def _bdot_nt(a, b):
    return lax.dot_general(a.astype(BF16), b.astype(BF16), _NT, preferred_element_type=F32)


def _bdot_nt_fwd(a, b):
    return _bdot_nt(a, b), (a.astype(BF16), b.astype(BF16))


def _bdot_nt_bwd(res, ct):
    a, b = res
    c = ct.astype(BF16)
    return (jnp.dot(c, b, preferred_element_type=F32),
            lax.dot_general(c, a, _TN, preferred_element_type=F32))


_bdot_nt.defvjp(_bdot_nt_fwd, _bdot_nt_bwd)


def _rms(x, g):
    return x * lax.rsqrt(jnp.mean(x * x, axis=-1, keepdims=True) + EPS) * g


def _sigmoid(x):
    return 0.5 * jnp.tanh(0.5 * x) + 0.5


def _split_dot(x, u):
    hi = x.astype(BF16)
    lo = (x - hi.astype(F32)).astype(BF16)
    return jnp.dot(hi, u, preferred_element_type=F32) + jnp.dot(lo, u, preferred_element_type=F32)


def _head_mask(h, shape):
    col = lax.broadcasted_iota(jnp.int32, shape, 1)
    return (col >= h * HEAD_DIM) & (col < (h + 1) * HEAD_DIM)


ANY = pl.BlockSpec(memory_space=pl.ANY)
MESH = pl.DeviceIdType.MESH
DOWN_ROWS = D_FF // N_DEV


def _my_place():
    x, y, c = (lax.axis_index(a) for a in MESH_AXES)
    return x, y, c, 4 * x + 2 * y + c


def _peer(x, y, c, d):
    return (x ^ (d >> 2), y ^ ((d >> 1) & 1), c ^ (d & 1))


class _Item:
    def __init__(self, src, block, dst_shape, place=None, init=None, relayed=False):
        self.src, self.block, self.dst_shape, self.init, self.relayed = src, block, dst_shape, init, relayed
        self.place = place or (lambda dst, k: dst.at[k])


def _call(body, args, *, grid, in_specs, out_specs, out_shape, sem, name, scratch_shapes=(), items=()):
    if not items:
        outs = pl.pallas_call(body, grid=grid, in_specs=in_specs, out_specs=out_specs, out_shape=out_shape, name=name,
                              scratch_shapes=list(scratch_shapes), compiler_params=_cp(*sem))(*args)
        return list(outs) if isinstance(outs, (list, tuple)) else [outs]
    n_in, n_out, n_scr, n = len(in_specs), len(out_specs), len(scratch_shapes), len(items)
    inits = [i for i, it in enumerate(items) if it.init is not None]

    def wrapped(*refs):
        core_in, srcs = refs[:n_in], refs[n_in:n_in + n]
        off = n_in + n + len(inits)
        core_out, dsts = refs[off:off + n_out], refs[off + n_out:off + n_out + n]
        scratch = refs[off + n_out + n:]
        send_sems, recv_sems, local_sems = scratch[n_scr:]
        step = pl.program_id(0)
        for a in range(1, len(grid)):
            step = step * grid[a] + pl.program_id(a)
        steps = functools.reduce(lambda p, g: p * g, grid)
        relay_step = max(0, steps - 1 - max(1, steps // 8))
        x, y, c, me = _my_place()
        direct = [i for i in range(n) if not items[i].relayed]
        relayed = [i for i in range(n) if items[i].relayed]
        chips = [(1 - x, y), (x, 1 - y), (1 - x, 1 - y)]
        sibling = (x, y, 1 - c)

        def local(i):
            return pltpu.make_async_copy(items[i].block(srcs[i], me), items[i].place(dsts[i], me), local_sems.at[i])

        def remote(d, i, sending):
            px, py, pc = _peer(x, y, c, d)
            pk = 4 * px + 2 * py + pc
            return pltpu.make_async_remote_copy(
                src_ref=items[i].block(srcs[i], pk), dst_ref=items[i].place(dsts[i], me if sending else pk),
                send_sem=send_sems.at[d - 1, i], recv_sem=recv_sems.at[d - 1, i],
                device_id=(px, py, pc), device_id_type=MESH)

        def hop(slot, i, owner, to, from_src):
            k = 4 * owner[0] + 2 * owner[1] + owner[2]
            src = items[i].block(srcs[i], me) if from_src else items[i].place(dsts[i], k)
            return pltpu.make_async_remote_copy(
                src_ref=src, dst_ref=items[i].place(dsts[i], k), send_sem=send_sems.at[slot, i],
                recv_sem=recv_sems.at[slot, i], device_id=to, device_id_type=MESH)

        @pl.when(step == 0)
        def _():
            for i in range(n):
                local(i).start()
            for d in range(1, N_DEV):
                for i in direct:
                    remote(d, i, True).start()
            for i in relayed:
                hop(0, i, (x, y, c), sibling, True).start()
                for j, chip in enumerate(chips):
                    hop(1 + j, i, (x, y, c), (*chip, c), True).start()

        body(*core_in, *core_out, *scratch[:n_scr])

        if relayed:
            @pl.when(step == relay_step)
            def _():
                for i in relayed:
                    for j, chip in enumerate(chips):
                        hop(1 + j, i, (*chip, c), (x, y, c), False).wait_recv()
                        hop(4 + j, i, (*chip, c), sibling, False).start()

        @pl.when(step == steps - 1)
        def _():
            for d in range(1, N_DEV):
                for i in direct:
                    remote(d, i, False).wait_recv()
            for i in relayed:
                hop(0, i, sibling, (x, y, c), False).wait_recv()
                for j, chip in enumerate(chips):
                    hop(4 + j, i, (*chip, 1 - c), (x, y, c), False).wait_recv()
            for d in range(1, N_DEV):
                for i in direct:
                    remote(d, i, True).wait_send()
            for i in relayed:
                for slot in range(N_DEV - 1):
                    hop(slot, i, (x, y, c), sibling, True).wait_send()
            for i in range(n):
                local(i).wait()

    outs = pl.pallas_call(
        wrapped, grid=grid, name=name,
        in_specs=list(in_specs) + [ANY] * (n + len(inits)), out_specs=list(out_specs) + [ANY] * n,
        out_shape=list(out_shape) + [jax.ShapeDtypeStruct(it.dst_shape, it.src.dtype) for it in items],
        input_output_aliases={n_in + n + j: n_out + i for j, i in enumerate(inits)},
        scratch_shapes=list(scratch_shapes) + [pltpu.SemaphoreType.DMA((N_DEV - 1, n)),
                                               pltpu.SemaphoreType.DMA((N_DEV - 1, n)), pltpu.SemaphoreType.DMA((n,))],
        compiler_params=_cp(*(["arbitrary"] * len(grid))),
    )(*args, *[it.src for it in items], *[items[i].init for i in inits])
    return list(outs)


def _norm_mm(x, g, w, l, jb, out_dtype, name, tm=1024, items=()):
    T, K = x.shape
    _, nb, _, tn = w.shape
    tm = min(tm, T)

    def body(x_ref, g_ref, w_ref, o_ref, h_ref):
        @pl.when(pl.program_id(1) == 0)
        def _():
            h_ref[...] = _rms(x_ref[...], g_ref[...]).astype(BF16)
        h = h_ref[...]
        for jj in range(jb):
            o_ref[:, jj * tn:(jj + 1) * tn] = jnp.dot(h, w_ref[jj], preferred_element_type=F32).astype(o_ref.dtype)

    return _call(
        body, (x, g, w), grid=(T // tm, nb // jb), name=name, items=items,
        in_specs=[pl.BlockSpec((tm, K), lambda i, j: (i, 0)), pl.BlockSpec((1, K), lambda i, j: (0, 0)),
                  pl.BlockSpec((None, jb, K, tn), lambda i, j: (l, j, 0, 0))],
        out_specs=[pl.BlockSpec((tm, jb * tn), lambda i, j: (i, j)), pl.BlockSpec((tm, K), lambda i, j: (i, 0))],
        out_shape=[jax.ShapeDtypeStruct((T, nb * tn), out_dtype), jax.ShapeDtypeStruct((T, K), BF16)],
        sem=("parallel", "arbitrary"))


def _mm_res(res, a, w, l, name, tn, tm=1024, items=()):
    T, K = a.shape
    N = w.shape[2]
    tm = min(tm, T)

    def body(r_ref, a_ref, w_ref, o_ref):
        o_ref[...] = r_ref[...] + jnp.dot(a_ref[...], w_ref[...], preferred_element_type=F32)

    return _call(
        body, (res, a, w), grid=(T // tm, N // tn), name=name, items=items,
        in_specs=[pl.BlockSpec((tm, tn), lambda i, j: (i, j)), pl.BlockSpec((tm, K), lambda i, j: (i, 0)),
                  pl.BlockSpec((None, K, tn), lambda i, j: (l, 0, j))],
        out_specs=[pl.BlockSpec((tm, tn), lambda i, j: (i, j))],
        out_shape=[jax.ShapeDtypeStruct((T, N), F32)],
        sem=("parallel", "parallel"))


def _mm_nt(a, w, l, out_dtype, name, tm=1024, tn=1024):
    T, K = a.shape
    N = w.shape[1]
    tm = min(tm, T)

    def body(a_ref, w_ref, o_ref):
        o_ref[...] = lax.dot_general(a_ref[...].astype(BF16), w_ref[...], _NT,
                                     preferred_element_type=F32).astype(o_ref.dtype)

    return pl.pallas_call(
        body, grid=(T // tm, N // tn), name=name,
        in_specs=[pl.BlockSpec((tm, K), lambda i, j: (i, 0)), pl.BlockSpec((None, tn, K), lambda i, j: (l, j, 0))],
        out_specs=pl.BlockSpec((tm, tn), lambda i, j: (i, j)),
        out_shape=jax.ShapeDtypeStruct((T, N), out_dtype),
        compiler_params=_cp("parallel", "parallel"))(a, w)


def _mm_norm_bwd(a, w, l, jb, x, g, dres, name, tm=1024, items=(), block_of=lambda j: j, transposed=False):
    T = a.shape[0]
    _, nb, K, tn = w.shape
    if transposed:
        tn, K = K, tn
    tm = min(tm, T)
    nj = nb // jb
    sub = min(256, tm)
    dims = (((1,), (0,)), ((), ())) if transposed else _NT

    def body(a_ref, w_ref, x_ref, g_ref, r_ref, o_ref, dg_ref):
        part = lax.dot_general(a_ref[:, :tn], w_ref[0], dims, preferred_element_type=F32)
        for jj in range(1, jb):
            part += lax.dot_general(a_ref[:, jj * tn:(jj + 1) * tn], w_ref[jj], dims, preferred_element_type=F32)

        @pl.when(pl.program_id(1) == 0)
        def _():
            o_ref[...] = part

        @pl.when(pl.program_id(1) > 0)
        def _():
            o_ref[...] += part

        @pl.when(pl.program_id(1) == nj - 1)
        def _():
            dg = jnp.zeros((1, K), F32)
            for r in range(tm // sub):
                rows = pl.ds(r * sub, sub)
                _, vjp = jax.vjp(_rms, x_ref[rows, :], g_ref[...])
                dx, dg_r = vjp(o_ref[rows, :])
                o_ref[rows, :] = r_ref[rows, :] + dx
                dg = dg + dg_r
            _acc_out(pl.program_id(0) == 0, (dg_ref,), (dg,))

    row = pl.BlockSpec((tm, K), lambda i, j: (i, 0))
    vec = pl.BlockSpec((1, K), lambda i, j: (0, 0))
    return _call(
        body, (a, w, x, g, dres), grid=(T // tm, nj), name=name, items=items,
        in_specs=[pl.BlockSpec((tm, jb * tn), lambda i, j: (i, j)),
                  pl.BlockSpec((None, jb) + w.shape[2:], lambda i, j: (l, block_of(j), 0, 0)), row, vec, row],
        out_specs=[row, vec],
        out_shape=[jax.ShapeDtypeStruct((T, K), F32), jax.ShapeDtypeStruct((1, K), F32)],
        sem=("arbitrary", "arbitrary"))


def _mm_tn_cols(lhs, rhs, tn, jb, name, tm=1024, items=(), block_of=lambda j: j):
    T, K = lhs.shape
    nb = rhs.shape[1] // tn
    tm = min(tm, T)
    nt = T // tm

    def body(l_ref, r_ref, o_ref, acc):
        part = lax.dot_general(l_ref[...], r_ref[...], _TN, preferred_element_type=F32)

        @pl.when(pl.program_id(1) == 0)
        def _():
            acc[...] = part

        @pl.when(pl.program_id(1) > 0)
        def _():
            acc[...] += part

        @pl.when(pl.program_id(1) == nt - 1)
        def _():
            for jj in range(jb):
                o_ref[jj] = acc[:, jj * tn:(jj + 1) * tn].astype(BF16)

    return _call(
        body, (lhs, rhs), grid=(nb // jb, nt), name=name, items=items,
        in_specs=[pl.BlockSpec((tm, K), lambda j, t: (t, 0)), pl.BlockSpec((tm, jb * tn), lambda j, t: (t, j))],
        out_specs=[pl.BlockSpec((jb, K, tn), lambda j, t: (block_of(j), 0, 0))],
        out_shape=[jax.ShapeDtypeStruct((nb, K, tn), BF16)],
        scratch_shapes=[pltpu.VMEM((K, jb * tn), F32)],
        sem=("parallel", "arbitrary"))


def _mm_tn_rows(lhs, rhs, tk, name, tm=1024, items=(), block_of=lambda j: j):
    T, Kl = lhs.shape
    N = rhs.shape[1]
    tm = min(tm, T)
    nt = T // tm

    def body(l_ref, r_ref, o_ref, acc):
        part = lax.dot_general(l_ref[...], r_ref[...].astype(BF16), _TN, preferred_element_type=F32)

        @pl.when(pl.program_id(1) == 0)
        def _():
            acc[...] = part

        @pl.when(pl.program_id(1) > 0)
        def _():
            acc[...] += part

        @pl.when(pl.program_id(1) == nt - 1)
        def _():
            o_ref[...] = acc[...].astype(BF16)

    return _call(
        body, (lhs, rhs), grid=(Kl // tk, nt), name=name, items=items,
        in_specs=[pl.BlockSpec((tm, tk), lambda l, t: (t, l)), pl.BlockSpec((tm, N), lambda l, t: (t, 0))],
        out_specs=[pl.BlockSpec((tk, N), lambda l, t: (block_of(l), 0))],
        out_shape=[jax.ShapeDtypeStruct((Kl, N), BF16)],
        scratch_shapes=[pltpu.VMEM((tk, N), F32)],
        sem=("parallel", "arbitrary"))


N_FF_CHUNK = D_FF_PAD // FF_PAD


def _ffn_chunk_block(j):
    return (j % 2) * N_FF_CHUNK + j // 2


def _ffn_up_act(x, g, w, name, tm=1024, items=()):
    T, K = x.shape
    tm = min(tm, T)

    def body(x_ref, g_ref, wg_ref, wu_ref, gu_ref, act_ref, h_ref):
        @pl.when(pl.program_id(1) == 0)
        def _():
            h_ref[...] = _rms(x_ref[...], g_ref[...]).astype(BF16)
        h = h_ref[...]
        gate = lax.dot_general(h, wg_ref[...], _NT, preferred_element_type=F32)
        up = lax.dot_general(h, wu_ref[...], _NT, preferred_element_type=F32)
        gu_ref[:, :FF_PAD] = gate.astype(BF16)
        gu_ref[:, FF_PAD:] = up.astype(BF16)
        act_ref[...] = (gate * _sigmoid(gate) * up).astype(BF16)

    return _call(
        body, (x, g, w, w), grid=(T // tm, N_FF_CHUNK), name=name, items=items,
        in_specs=[pl.BlockSpec((tm, K), lambda i, j: (i, 0)), pl.BlockSpec((1, K), lambda i, j: (0, 0)),
                  pl.BlockSpec((None, None, FF_PAD, K), lambda i, j: (0, j, 0, 0)),
                  pl.BlockSpec((None, None, FF_PAD, K), lambda i, j: (0, j + N_FF_CHUNK, 0, 0))],
        out_specs=[pl.BlockSpec((tm, 2 * FF_PAD), lambda i, j: (i, j)), pl.BlockSpec((tm, FF_PAD), lambda i, j: (i, j)),
                   pl.BlockSpec((tm, K), lambda i, j: (i, 0))],
        out_shape=[jax.ShapeDtypeStruct((T, 2 * D_FF_PAD), BF16), jax.ShapeDtypeStruct((T, D_FF_PAD), BF16),
                   jax.ShapeDtypeStruct((T, K), BF16)],
        sem=("parallel", "arbitrary"))


def _ffn_down_dx(dxo, w, gu, name, tm=1024):
    T, K = dxo.shape
    tm = min(tm, T)

    sub = min(512, tm)

    def body(d_ref, w_ref, gu_ref, o_ref):
        w = w_ref[...]
        for r in range(tm // sub):
            rows = slice(r * sub, (r + 1) * sub)
            d = lax.dot_general(d_ref[rows, :].astype(BF16), w, _NT, preferred_element_type=F32)
            gate = gu_ref[rows, :FF_PAD].astype(F32)
            up = gu_ref[rows, FF_PAD:].astype(F32)
            sig = _sigmoid(gate)
            silu = gate * sig
            o_ref[rows, :FF_PAD] = (d * up * (sig + silu * (1.0 - sig))).astype(BF16)
            o_ref[rows, FF_PAD:] = (d * silu).astype(BF16)

    return pl.pallas_call(
        body, grid=(T // tm, N_FF_CHUNK), name=name,
        in_specs=[pl.BlockSpec((tm, K), lambda i, j: (i, 0)), pl.BlockSpec((None, FF_PAD, K), lambda i, j: (0, j, 0)),
                  pl.BlockSpec((tm, 2 * FF_PAD), lambda i, j: (i, j))],
        out_specs=pl.BlockSpec((tm, 2 * FF_PAD), lambda i, j: (i, j)),
        out_shape=jax.ShapeDtypeStruct((T, 2 * D_FF_PAD), BF16),
        compiler_params=_cp("parallel", "parallel"))(dxo, w, gu)


def _loss_head(x, g, target, name, tm=512):
    T, K = x.shape

    def loss_fn(xv, gv, tv):
        err = _rms(xv, gv) - tv
        return 0.5 * jnp.sum(jnp.mean(err * err, axis=-1, keepdims=True), axis=0, keepdims=True)

    def body(x_ref, g_ref, t_ref, l_ref, dx_ref, dg_ref):
        val, vjp = jax.vjp(lambda xv, gv: loss_fn(xv, gv, t_ref[...]), x_ref[...], g_ref[...])
        dx, dg = vjp(jnp.ones((1, 1), F32))
        dx_ref[...] = dx
        lval = jnp.broadcast_to(val, (1, 128))

        @pl.when(pl.program_id(0) == 0)
        def _():
            dg_ref[...] = dg
            l_ref[...] = lval

        @pl.when(pl.program_id(0) > 0)
        def _():
            dg_ref[...] += dg
            l_ref[...] += lval

    row = pl.BlockSpec((tm, K), lambda i: (i, 0))
    vec = pl.BlockSpec((1, K), lambda i: (0, 0))
    return pl.pallas_call(
        body, grid=(T // tm,), name=name,
        in_specs=[row, vec, row], out_specs=[pl.BlockSpec((1, 128), lambda i: (0, 0)), row, vec],
        out_shape=[jax.ShapeDtypeStruct((1, 128), F32), jax.ShapeDtypeStruct((T, K), F32),
                   jax.ShapeDtypeStruct((1, K), F32)],
        compiler_params=_cp("arbitrary"))(x, g, target)


SGU_ROWS = 4 * BLOCK


def _sgu_fn(u, v, wm, bt, gain):
    ug = jax.nn.gelu(u)
    vg = jax.nn.gelu(v)
    row = lax.broadcasted_iota(jnp.int32, (BLOCK, BLOCK), 0)
    col = lax.broadcasted_iota(jnp.int32, (BLOCK, BLOCK), 1)
    tri = row >= col
    normed = []
    for h in range(GROUP_HEADS):
        vh = vg[:, h * HEAD_DIM:(h + 1) * HEAD_DIM]
        xc = vh - jnp.mean(vh, axis=-1, keepdims=True)
        normed.append(xc * lax.rsqrt(jnp.mean(xc * xc, axis=-1, keepdims=True) + EPS))
    vn = jnp.concatenate(normed, axis=1)
    wcat = jnp.concatenate([jnp.where(tri, wm[h], 0.0) for h in range(GROUP_HEADS)], axis=1)
    bias = jnp.concatenate([jnp.broadcast_to(bt[:, h:h + 1], (BLOCK, HEAD_DIM)) for h in range(GROUP_HEADS)], axis=1)
    mixes = []
    for c in range(u.shape[0] // BLOCK):
        chunk = vn[c * BLOCK:(c + 1) * BLOCK]
        stacked = jnp.concatenate([jnp.where(_head_mask(h, chunk.shape), chunk, 0.0) for h in range(GROUP_HEADS)], axis=0)
        mixes.append(_bdot(wcat, stacked) + bias)
    return _rms(ug * jnp.concatenate(mixes, axis=0), gain)


def _sgu_fwd(proj, wm, bt, gain, name, items=()):
    T = proj.shape[0]
    rows = min(SGU_ROWS, T)

    def body(u_ref, v_ref, w_ref, b_ref, g_ref, o_ref):
        o_ref[...] = _sgu_fn(u_ref[...], v_ref[...], w_ref[...], b_ref[...], g_ref[...]).astype(BF16)

    full = lambda shape: pl.BlockSpec(shape, lambda i: (0,) * len(shape))
    return _call(
        body, (proj, proj, wm, bt, gain), grid=(T // rows,), name=name, items=items,
        in_specs=[pl.BlockSpec((rows, GROUP_WIDTH), lambda i: (i, 0)), pl.BlockSpec((rows, GROUP_WIDTH), lambda i: (i, 1)),
                  full(wm.shape), full(bt.shape), full(gain.shape)],
        out_specs=[pl.BlockSpec((rows, GROUP_WIDTH), lambda i: (i, 0))],
        out_shape=[jax.ShapeDtypeStruct((T, GROUP_WIDTH), BF16)],
        sem=("parallel",))


def _acc_out(first, refs, vals):
    @pl.when(first)
    def _():
        for r, v in zip(refs, vals):
            r[...] = v

    @pl.when(jnp.logical_not(first))
    def _():
        for r, v in zip(refs, vals):
            r[...] += v


def _sgu_bwd(proj, wm, bt, gain, dy, name, items=()):
    T = proj.shape[0]

    def body(u_ref, v_ref, w_ref, b_ref, g_ref, dy_ref, duv_ref, dw_ref, db_ref, dg_ref):
        _, vjp = jax.vjp(_sgu_fn, u_ref[...], v_ref[...], w_ref[...], b_ref[...], g_ref[...])
        du, dv, dw, db, dg = vjp(dy_ref[...])
        duv_ref[:, :GROUP_WIDTH] = du.astype(BF16)
        duv_ref[:, GROUP_WIDTH:] = dv.astype(BF16)
        _acc_out(pl.program_id(0) == 0, (dw_ref, db_ref, dg_ref), (dw, db, dg))

    full = lambda shape: pl.BlockSpec(shape, lambda i: (0,) * len(shape))
    rows = min(SGU_ROWS, T)
    return _call(
        body, (proj, proj, wm, bt, gain, dy), grid=(T // rows,), name=name, items=items,
        in_specs=[pl.BlockSpec((rows, GROUP_WIDTH), lambda i: (i, 0)), pl.BlockSpec((rows, GROUP_WIDTH), lambda i: (i, 1)),
                  full(wm.shape), full(bt.shape), full(gain.shape),
                  pl.BlockSpec((rows, GROUP_WIDTH), lambda i: (i, 0))],
        out_specs=[pl.BlockSpec((rows, 2 * GROUP_WIDTH), lambda i: (i, 0)), full(wm.shape), full(bt.shape), full(gain.shape)],
        out_shape=[jax.ShapeDtypeStruct((T, 2 * GROUP_WIDTH), BF16), jax.ShapeDtypeStruct(wm.shape, F32),
                   jax.ShapeDtypeStruct(bt.shape, F32), jax.ShapeDtypeStruct(gain.shape, F32)],
        sem=("arbitrary",))


def _pool_consts(seq):
    t = lax.broadcasted_iota(jnp.int32, (seq, GROUP_WIDTH), 0)
    grp = lax.broadcasted_iota(jnp.int32, (seq, GROUP_WIDTH), 1) // (GROUP_WIDTH // len(POOL_WINDOWS))
    win = jnp.where(grp == 0, POOL_WINDOWS[0], jnp.where(grp == 1, POOL_WINDOWS[1],
                    jnp.where(grp == 2, POOL_WINDOWS[2], POOL_WINDOWS[3])))
    count = jnp.minimum(t + 1, win).astype(F32)
    return t, grp, count


def _by_group(grp, vals):
    return jnp.where(grp == 0, vals[0], jnp.where(grp == 1, vals[1], jnp.where(grp == 2, vals[2], vals[3])))


def _window_sums(x, t, seq, back):
    def shift(a, k):
        if back:
            return jnp.where(t >= k, pltpu.roll(a, k, 0), 0.0)
        return jnp.where(t < seq - k, pltpu.roll(a, seq - k, 0), 0.0)
    sums = []
    a, k = x, 1
    for _ in POOL_WINDOWS:
        a = a + shift(a, k)
        sums.append(a)
        k *= 2
    return sums


def _pool_tail(y, wbd, scale, gain):
    return _rms(_bdot(y, wbd) * scale, gain)


def _pool_fwd(proj, wbd, scale, gain, seq, name):
    T = proj.shape[0]

    def body(p_ref, w_ref, s_ref, g_ref, o_ref):
        p = p_ref[...]
        t, grp, count = _pool_consts(seq)
        y = _by_group(grp, _window_sums(p, t, seq, True)) / count - p
        o_ref[...] = _pool_tail(y, w_ref[...], s_ref[...], g_ref[...]).astype(BF16)

    full = lambda shape: pl.BlockSpec(shape, lambda b: (0,) * len(shape))
    return pl.pallas_call(
        body, grid=(T // seq,), name=name,
        in_specs=[pl.BlockSpec((seq, GROUP_WIDTH), lambda b: (b, 2)), full(wbd.shape), full(scale.shape), full(gain.shape)],
        out_specs=pl.BlockSpec((seq, GROUP_WIDTH), lambda b: (b, 0)),
        out_shape=jax.ShapeDtypeStruct((T, GROUP_WIDTH), BF16),
        compiler_params=_cp("parallel"))(proj, wbd, scale, gain)


def _pool_bwd(proj, wbd, scale, gain, dy, seq, name):
    T = proj.shape[0]

    def body(p_ref, w_ref, s_ref, g_ref, dy_ref, dp_ref, dw_ref, ds_ref, dg_ref):
        p = p_ref[...]
        t, grp, count = _pool_consts(seq)
        y = _by_group(grp, _window_sums(p, t, seq, True)) / count - p
        _, vjp = jax.vjp(_pool_tail, y, w_ref[...], s_ref[...], g_ref[...])
        d_y, dw, ds, dg = vjp(dy_ref[...])
        dp = _by_group(grp, _window_sums(d_y / count, t, seq, False)) - d_y
        dp_ref[...] = dp.astype(BF16)
        _acc_out(pl.program_id(0) == 0, (dw_ref, ds_ref, dg_ref), (dw, ds, dg))

    full = lambda shape: pl.BlockSpec(shape, lambda b: (0,) * len(shape))
    return pl.pallas_call(
        body, grid=(T // seq,), name=name,
        in_specs=[pl.BlockSpec((seq, GROUP_WIDTH), lambda b: (b, 2)), full(wbd.shape), full(scale.shape), full(gain.shape),
                  pl.BlockSpec((seq, GROUP_WIDTH), lambda b: (b, 1))],
        out_specs=[pl.BlockSpec((seq, GROUP_WIDTH), lambda b: (b, 0)), full(wbd.shape), full(scale.shape), full(gain.shape)],
        out_shape=[jax.ShapeDtypeStruct((T, GROUP_WIDTH), BF16), jax.ShapeDtypeStruct(wbd.shape, F32),
                   jax.ShapeDtypeStruct(scale.shape, F32), jax.ShapeDtypeStruct(gain.shape, F32)],
        compiler_params=_cp("arbitrary"))(proj, wbd, scale, gain, dy)


def _swa_fn(q, kv_prev, kv_cur, sinks, tab, gain, first):
    half = GROUP_WIDTH // 2
    k2 = jnp.concatenate([kv_prev[:, :half], kv_cur[:, :half]], axis=0)
    v2 = jnp.concatenate([kv_prev[:, half:], kv_cur[:, half:]], axis=0)
    per_query_head = lambda a: jnp.concatenate(
        [a[:, (h // 2) * HEAD_DIM:(h // 2 + 1) * HEAD_DIM] for h in range(GROUP_HEADS)], axis=1)
    qs = jnp.concatenate([jnp.where(_head_mask(h, q.shape), q, 0.0) for h in range(GROUP_HEADS)], axis=0)
    qi = lax.broadcasted_iota(jnp.int32, (HEAD_ROWS, 2 * BLOCK), 0) & (BLOCK - 1)
    kj = lax.broadcasted_iota(jnp.int32, (HEAD_ROWS, 2 * BLOCK), 1)
    dist = qi + BLOCK - kj
    mask = (dist >= 0) & (dist < BLOCK) & ((kj >= BLOCK) | jnp.logical_not(first))
    logits = _bdot_nt(qs, per_query_head(k2)) * ATT_SCALE + tab.reshape(HEAD_ROWS, 2 * BLOCK)
    logits = jnp.where(mask, logits, -1e30)
    sink = jnp.concatenate([jnp.broadcast_to(sinks[:, h:h + 1], (BLOCK, 1)) for h in range(GROUP_HEADS)], axis=0)
    m = lax.stop_gradient(jnp.maximum(jnp.max(logits, axis=1, keepdims=True), sink))
    p = jnp.exp(logits - m)
    probs = p / (jnp.sum(p, axis=1, keepdims=True) + jnp.exp(sink - m))
    out = _bdot(probs, per_query_head(v2))
    y = jnp.zeros_like(q)
    for h in range(GROUP_HEADS):
        y = y + jnp.where(_head_mask(h, q.shape), out[h * BLOCK:(h + 1) * BLOCK], 0.0)
    return _rms(y, gain)


def _swa_specs(nblk):
    q = pl.BlockSpec((BLOCK, GROUP_WIDTH), lambda b, i: (b * nblk + i, 3))
    cur = pl.BlockSpec((BLOCK, GROUP_WIDTH), lambda b, i: (b * nblk + i, 4))
    prev = pl.BlockSpec((BLOCK, GROUP_WIDTH), lambda b, i: (b * nblk + jnp.maximum(i - 1, 0), 4))
    return q, prev, cur


def _swa_fwd(proj, sinks, tab, gain, seq, name, items=()):
    T = proj.shape[0]
    nblk = seq // BLOCK

    def body(q_ref, kp_ref, kc_ref, s_ref, t_ref, g_ref, o_ref):
        o_ref[...] = _swa_fn(q_ref[...], kp_ref[...], kc_ref[...], s_ref[...], t_ref[...], g_ref[...],
                             pl.program_id(1) == 0).astype(BF16)

    full = lambda shape: pl.BlockSpec(shape, lambda b, i: (0,) * len(shape))
    return _call(
        body, (proj, proj, proj, sinks, tab, gain), grid=(T // seq, nblk), name=name, items=items,
        in_specs=[*_swa_specs(nblk), full(sinks.shape), full(tab.shape), full(gain.shape)],
        out_specs=[pl.BlockSpec((BLOCK, GROUP_WIDTH), lambda b, i: (b * nblk + i, 0))],
        out_shape=[jax.ShapeDtypeStruct((T, GROUP_WIDTH), BF16)],
        sem=("parallel", "parallel"))


def _swa_bwd(proj, sinks, tab, gain, dy, seq, name, items=()):
    T = proj.shape[0]
    nblk = seq // BLOCK

    def body(q_ref, kp_ref, kc_ref, s_ref, t_ref, g_ref, dy_ref, dq_ref, dkp_ref, dkc_ref, ds_ref, dt_ref, dg_ref):
        first = pl.program_id(1) == 0
        fn = functools.partial(_swa_fn, first=first)
        _, vjp = jax.vjp(fn, q_ref[...], kp_ref[...], kc_ref[...], s_ref[...], t_ref[...], g_ref[...])
        dq, dkp, dkc, ds, dt, dg = vjp(dy_ref[...])
        dq_ref[...] = dq.astype(BF16)
        dkp_ref[...] = dkp
        dkc_ref[...] = dkc
        _acc_out((pl.program_id(0) == 0) & first, (ds_ref, dt_ref, dg_ref), (ds, dt, dg))

    full = lambda shape: pl.BlockSpec(shape, lambda b, i: (0,) * len(shape))
    blk = lambda c: pl.BlockSpec((BLOCK, GROUP_WIDTH), lambda b, i: (b * nblk + i, c))
    return _call(
        body, (proj, proj, proj, sinks, tab, gain, dy), grid=(T // seq, nblk), name=name, items=items,
        in_specs=[*_swa_specs(nblk), full(sinks.shape), full(tab.shape), full(gain.shape), blk(2)],
        out_specs=[blk(0), blk(0), blk(0), full(sinks.shape), full(tab.shape), full(gain.shape)],
        out_shape=[jax.ShapeDtypeStruct((T, GROUP_WIDTH), BF16), jax.ShapeDtypeStruct((T, GROUP_WIDTH), F32),
                   jax.ShapeDtypeStruct((T, GROUP_WIDTH), F32), jax.ShapeDtypeStruct(sinks.shape, F32),
                   jax.ShapeDtypeStruct(tab.shape, F32), jax.ShapeDtypeStruct(gain.shape, F32)],
        sem=("arbitrary", "arbitrary"))


def _shift_add(cur, prev, seq, name):
    T = cur.shape[0]
    nblk = seq // BLOCK

    def body(c_ref, p_ref, o_ref):
        last = pl.program_id(1) == nblk - 1
        o_ref[...] = (c_ref[...] + jnp.where(last, 0.0, p_ref[...])).astype(BF16)

    return pl.pallas_call(
        body, grid=(T // seq, nblk), name=name,
        in_specs=[pl.BlockSpec((BLOCK, GROUP_WIDTH), lambda b, i: (b * nblk + i, 0)),
                  pl.BlockSpec((BLOCK, GROUP_WIDTH), lambda b, i: (b * nblk + jnp.minimum(i + 1, nblk - 1), 0))],
        out_specs=pl.BlockSpec((BLOCK, GROUP_WIDTH), lambda b, i: (b * nblk + i, 0)),
        out_shape=jax.ShapeDtypeStruct((T, GROUP_WIDTH), BF16),
        compiler_params=_cp("parallel", "parallel"))(cur, prev)


def _t5_bucket(dist):
    max_exact = N_BUCKETS // 2
    df = jnp.maximum(dist, 1).astype(F32)
    large = max_exact + (jnp.log(df / max_exact) / jnp.log(jnp.float32(MAX_DISTANCE / max_exact))
                         * (N_BUCKETS - max_exact)).astype(jnp.int32)
    return jnp.where(dist < max_exact, dist, jnp.minimum(large, N_BUCKETS - 1))


def _bucket_map():
    dist = (jnp.arange(BLOCK)[:, None] + BLOCK) - jnp.arange(2 * BLOCK)[None, :]
    return _t5_bucket(jnp.clip(dist, 0, BLOCK - 1)).astype(jnp.int32)


def _bias_table(rel_bias, buckets, name):
    def body(rb_ref, bk_ref, o_ref):
        bk = bk_ref[...]
        rb = rb_ref[...]
        for h in range(GROUP_HEADS):
            acc = jnp.zeros((BLOCK, 2 * BLOCK), F32)
            for b in range(N_BUCKETS):
                acc = jnp.where(bk == b, rb[b:b + 1, h:h + 1], acc)
            o_ref[h] = acc

    return pl.pallas_call(body, name=name, out_shape=jax.ShapeDtypeStruct((GROUP_HEADS, BLOCK, 2 * BLOCK), F32),
                          compiler_params=_cp())(rel_bias, buckets)


def _bias_table_bwd(dtab, buckets, name):
    def body(dt_ref, bk_ref, o_ref):
        bk = bk_ref[...]
        row = lax.broadcasted_iota(jnp.int32, (N_BUCKETS, GROUP_HEADS), 0)
        col = lax.broadcasted_iota(jnp.int32, (N_BUCKETS, GROUP_HEADS), 1)
        acc = jnp.zeros((N_BUCKETS, GROUP_HEADS), F32)
        for h in range(GROUP_HEADS):
            dt = dt_ref[h]
            for b in range(N_BUCKETS):
                s = jnp.sum(jnp.where(bk == b, dt, 0.0), keepdims=True)
                acc = acc + jnp.where((row == b) & (col == h), s, 0.0)
        o_ref[...] = acc

    return pl.pallas_call(body, name=name, out_shape=jax.ShapeDtypeStruct((N_BUCKETS, GROUP_HEADS), F32),
                          compiler_params=_cp())(dtab, buckets)


HEAD_ROWS = GROUP_HEADS * BLOCK


def _stack_heads(x):
    return jnp.concatenate([jnp.where(_head_mask(h, x.shape), x, 0.0) for h in range(GROUP_HEADS)], axis=0).astype(BF16)


def _sb_tile(qs, kb, q0, k0):
    z = lax.dot_general(qs, kb, _NT, preferred_element_type=F32)
    row = lax.broadcasted_iota(jnp.int32, (HEAD_ROWS, BLOCK), 0) & (BLOCK - 1)
    col = lax.broadcasted_iota(jnp.int32, (HEAD_ROWS, BLOCK), 1)
    causal = (k0 + col) < (q0 + row)
    ls_neg = -(jnp.maximum(z, 0.0) + jnp.log(1.0 + jnp.exp(-jnp.abs(z))))
    return jnp.where(causal, ls_neg, 0.0), ls_neg + z, causal


SB_DEAD = -70.0
SB_FIRST_LANE = GROUP_HEADS


def _tri(strict_upper_src):
    r = lax.broadcasted_iota(jnp.int32, (BLOCK, BLOCK), 0)
    c = lax.broadcasted_iota(jnp.int32, (BLOCK, BLOCK), 1)
    cond = {"gt": r > c, "le": r <= c, "lt": r < c}[strict_upper_src]
    return jnp.where(cond, 1.0, 0.0).astype(BF16)


def _sb_fwd(proj, gain, seq, name, items=()):
    T = proj.shape[0]
    nblk = seq // BLOCK

    def body(q_ref, k_ref, v_ref, g_ref, o_ref, raw_ref, bt_ref):
        i = pl.program_id(1)
        q = q_ref[...]
        u_gt = _tri("gt")
        lane = lax.broadcasted_iota(jnp.int32, (BLOCK, BLOCK), 1)
        heads = [slice(h * HEAD_DIM, (h + 1) * HEAD_DIM) for h in range(GROUP_HEADS)]
        rows = [slice(h * BLOCK, (h + 1) * BLOCK) for h in range(GROUP_HEADS)]
        qs = _stack_heads(q * ATT_SCALE)

        def live(carry):
            j, _, cb = carry
            return (j >= 0) & (jnp.max(cb) > SB_DEAD)

        def step(carry):
            j, accs, cb = carry
            ks = pl.multiple_of(j * BLOCK, BLOCK)
            kb = k_ref[pl.ds(ks, BLOCK), :].astype(BF16)
            vb = v_ref[pl.ds(ks, BLOCK), :].astype(BF16)
            b, a, causal = _sb_tile(qs, kb, i * BLOCK, j * BLOCK)
            tail = _split_dot(b, u_gt) + cb
            w = jnp.where(causal, jnp.exp(a + tail), 0.0).astype(BF16)
            accs = tuple(accs[h] + jnp.dot(w[rows[h]], vb[:, hs], preferred_element_type=F32)
                         for h, hs in enumerate(heads))
            return j - 1, accs, cb + jnp.sum(b, axis=1, keepdims=True)

        zero_acc = tuple(jnp.zeros((BLOCK, HEAD_DIM), F32) for _ in heads)
        j_end, accs, cb = lax.while_loop(live, step, (i, zero_acc, jnp.zeros((HEAD_ROWS, 1), F32)))
        side = jnp.where(lane == SB_FIRST_LANE, (j_end + 1).astype(F32), 0.0)
        for h in range(GROUP_HEADS):
            side = jnp.where(lane == h, cb[rows[h]], side)
        raw = jnp.concatenate(accs, axis=1)
        raw_ref[...] = raw
        bt_ref[...] = side
        o_ref[...] = _rms(raw, g_ref[...]).astype(BF16)

    return _call(
        body, (proj, proj, proj, gain), grid=(T // seq, nblk), name=name, items=items,
        in_specs=[pl.BlockSpec((BLOCK, GROUP_WIDTH), lambda b, i: (b * nblk + i, 5)),
                  pl.BlockSpec((seq, GROUP_WIDTH), lambda b, i: (b, 6)),
                  pl.BlockSpec((seq, GROUP_WIDTH), lambda b, i: (b, 7)),
                  pl.BlockSpec(gain.shape, lambda b, i: (0, 0))],
        out_specs=[pl.BlockSpec((BLOCK, GROUP_WIDTH), lambda b, i: (b * nblk + i, 0)),
                   pl.BlockSpec((BLOCK, GROUP_WIDTH), lambda b, i: (b * nblk + i, 0)),
                   pl.BlockSpec((BLOCK, BLOCK), lambda b, i: (b * nblk + i, 0))],
        out_shape=[jax.ShapeDtypeStruct((T, GROUP_WIDTH), BF16), jax.ShapeDtypeStruct((T, GROUP_WIDTH), F32),
                   jax.ShapeDtypeStruct((T, BLOCK), F32)],
        sem=("parallel", "parallel"))


def _sb_bwd(proj, gain, raw, btot, dy, seq, name, items=()):
    T = proj.shape[0]
    nblk = seq // BLOCK

    def body(q_ref, k_ref, v_ref, g_ref, raw_ref, bt_ref, dy_ref, dq_ref, dk_ref, dv_ref, dg_ref):
        i = pl.program_id(1)

        @pl.when(i == 0)
        def _():
            dk_ref[...] = jnp.zeros_like(dk_ref)
            dv_ref[...] = jnp.zeros_like(dv_ref)

        rawv = raw_ref[...]
        _, vjp = jax.vjp(_rms, rawv, g_ref[...])
        do, dg = vjp(dy_ref[...])
        _acc_out((pl.program_id(0) == 0) & (i == 0), (dg_ref,), (dg,))
        q = q_ref[...]
        bt = bt_ref[...]
        u_le = _tri("le")
        u_lt = _tri("lt")
        heads = [slice(h * HEAD_DIM, (h + 1) * HEAD_DIM) for h in range(GROUP_HEADS)]
        rows = [slice(h * BLOCK, (h + 1) * BLOCK) for h in range(GROUP_HEADS)]
        qs = _stack_heads(q * ATT_SCALE)
        dos = _stack_heads(do)
        bts = jnp.concatenate([bt[:, h:h + 1] for h in range(GROUP_HEADS)], axis=0)
        first = jnp.max(bt[:, SB_FIRST_LANE:SB_FIRST_LANE + 1]).astype(jnp.int32)
        first = jnp.minimum(jnp.maximum(first, 0), i)

        def step(j, carry):
            dqs, cb, cg = carry
            ks = pl.multiple_of(j * BLOCK, BLOCK)
            kb = k_ref[pl.ds(ks, BLOCK), :].astype(BF16)
            vb = v_ref[pl.ds(ks, BLOCK), :].astype(BF16)
            b, a, causal = _sb_tile(qs, kb, i * BLOCK, j * BLOCK)
            tail = bts - (_split_dot(b, u_le) + cb)
            w = jnp.where(causal, jnp.exp(a + tail), 0.0)
            sig = jnp.exp(a)
            g = w * lax.dot_general(dos, vb, _NT, preferred_element_type=F32)
            gpre = _split_dot(g, u_lt) + cg
            dz = jnp.where(causal, g * (1.0 - sig) - gpre * sig, 0.0).astype(BF16)
            dqs = tuple(dqs[h] + jnp.dot(dz[rows[h]], kb[:, hs], preferred_element_type=F32)
                        for h, hs in enumerate(heads))
            dk_ref[pl.ds(ks, BLOCK), :] += lax.dot_general(dz, qs, _TN, preferred_element_type=F32)
            dv_ref[pl.ds(ks, BLOCK), :] += lax.dot_general(w.astype(BF16), dos, _TN, preferred_element_type=F32)
            return dqs, cb + jnp.sum(b, axis=1, keepdims=True), cg + jnp.sum(g, axis=1, keepdims=True)

        zero_dq = tuple(jnp.zeros((BLOCK, HEAD_DIM), F32) for _ in heads)
        zero = jnp.zeros((HEAD_ROWS, 1), F32)
        dqs, _, _ = lax.fori_loop(first, i + 1, step, (zero_dq, zero, zero))
        dq_ref[...] = (jnp.concatenate(dqs, axis=1) * ATT_SCALE).astype(BF16)

    blk = lambda c: pl.BlockSpec((BLOCK, GROUP_WIDTH), lambda b, i: (b * nblk + i, c))
    seqblk = lambda c: pl.BlockSpec((seq, GROUP_WIDTH), lambda b, i: (b, c))
    vec = pl.BlockSpec(gain.shape, lambda b, i: (0, 0))
    return _call(
        body, (proj, proj, proj, gain, raw, btot, dy), grid=(T // seq, nblk), name=name, items=items,
        in_specs=[blk(5), seqblk(6), seqblk(7), vec, blk(0), pl.BlockSpec((BLOCK, BLOCK), lambda b, i: (b * nblk + i, 0)),
                  blk(3)],
        out_specs=[blk(0), seqblk(0), seqblk(0), vec],
        out_shape=[jax.ShapeDtypeStruct((T, GROUP_WIDTH), BF16), jax.ShapeDtypeStruct((T, GROUP_WIDTH), F32),
                   jax.ShapeDtypeStruct((T, GROUP_WIDTH), F32), jax.ShapeDtypeStruct(gain.shape, F32)],
        sem=("arbitrary", "arbitrary"))


def _layer_params(l, sgu_w, sgu_b, pool_w, pool_scale, swa_sinks, mix_out_gain, norm_mix, norm_ffn):
    gains = mix_out_gain[l].reshape(4, 1, GROUP_WIDTH)
    return dict(
        wm=sgu_w[l], bt=sgu_b[l].T,
        wbd=jax.scipy.linalg.block_diag(*[pool_w[l, g] for g in range(len(POOL_WINDOWS))]),
        scale=pool_scale[l][None], sinks=swa_sinks[l][None],
        gain=[gains[m] for m in range(4)], norm_mix=norm_mix[l][None], norm_ffn=norm_ffn[l][None])


def _as_weights(g_in, g_out, g_gu, g_down):
    return (g_in[None], g_out.reshape(1, D_MODEL, D_MODEL), g_gu[None], g_down.reshape(1, D_FF_PAD, D_MODEL))


def _gather_item(src, l, dst_shape, rows=None, down=False, init=None):
    r0, nr = rows or (0, src.shape[1])
    if down:
        place = lambda dst, k: dst.at[k // 2, pl.ds((k % 2) * DOWN_ROWS + r0, nr), :]
    else:
        place = lambda dst, k: dst.at[k, pl.ds(r0, nr), :]
    return _Item(src, lambda s, k: s.at[l, pl.ds(r0, nr), :], dst_shape, place, init, relayed=True)


def _layer_fwd(l, x, p, w, tab, seq, shards):
    win, wout, wgu, wd = w
    nl = l + 1
    s_in, s_out, s_gu, s_down = shards or (None,) * 4
    some = lambda *make: [m() for m in make] if shards else []
    proj, h1, *g_down = _norm_mm(x, p["norm_mix"], win, 0, 4, F32, f"in_proj_{l}", items=some(
        lambda: _gather_item(s_down, nl, (4, FF_PAD, D_MODEL), None, True, jnp.zeros((4, FF_PAD, D_MODEL), BF16))))
    ya, = _sgu_fwd(proj, p["wm"], p["bt"], p["gain"][0], f"sgu_fwd_{l}")
    yb = _pool_fwd(proj, p["wbd"], p["scale"], p["gain"][1], seq, f"pool_fwd_{l}")
    yc, *g_in_out = _swa_fwd(proj, p["sinks"], tab, p["gain"][2], seq, f"swa_fwd_{l}", items=some(
        lambda: _gather_item(s_in, nl, (N_DEV,) + s_in.shape[1:]),
        lambda: _gather_item(s_out, nl, (N_DEV,) + s_out.shape[1:])))
    half = FF_PAD // 2
    gu_shape = (N_DEV, FF_PAD, D_MODEL)
    yd, raw, btot, *g_gu = _sb_fwd(proj, p["gain"][3], seq, f"sb_fwd_{l}", items=some(
        lambda: _gather_item(s_gu, nl, gu_shape, (0, half))))
    ycat = jnp.concatenate([ya, yb, yc, yd], axis=1)
    xm, = _mm_res(x, ycat, wout, 0, f"out_proj_{l}", tn=D_MODEL)
    gu, act, h2, *g_gu = _ffn_up_act(xm, p["norm_ffn"], wgu, f"ffn_up_{l}", items=some(
        lambda: _gather_item(s_gu, nl, gu_shape, (half, half), init=g_gu[0])))
    xo, = _mm_res(xm, act, wd, 0, f"ffn_down_{l}", tn=D_MODEL // 2)
    w_next = _as_weights(g_in_out[0], g_in_out[1], g_gu[0], g_down[0]) if shards else None
    return xo, (x, proj, h1, ycat, raw, btot, xm, gu, h2, act), w_next


def _layer_bwd(l, dxo, saved, p, w, tab, seq, ride=()):
    win, wout, wgu, wd = w
    x, proj, h1, ycat, raw, btot, xm, gu, h2, act = saved
    out_rows = D_MODEL // N_DEV
    half = FF_PAD // 2

    def rows_of(g, r0, init=None):
        cut = lambda a, k: a.at[k, pl.ds(r0, half), :]
        return _Item(g, cut, g.shape, cut, init)

    dgu = _ffn_down_dx(dxo, wd, gu, f"ffn_down_dx_{l}")
    g_wd, = _mm_tn_rows(act, dxo, FF_PAD, f"ffn_down_dw_{l}")
    send_down = _Item(g_wd, lambda src, k: src.at[pl.ds((k // 2) * FF_PAD + (k % 2) * DOWN_ROWS, DOWN_ROWS), :],
                      (N_DEV, DOWN_ROWS, D_MODEL))
    g_wgu, p_down = _mm_tn_rows(dgu, h2, FF_PAD, f"ffn_up_dw_{l}", items=(send_down,), block_of=_ffn_chunk_block)
    g_wgu = g_wgu.reshape(N_DEV, FF_PAD, D_MODEL)
    dxm, g_norm_ffn, p_gu = _mm_norm_bwd(dgu, wgu, 0, 1, xm, p["norm_ffn"], dxo, f"ffn_up_dx_{l}",
                                         items=(rows_of(g_wgu, 0),), block_of=_ffn_chunk_block, transposed=True)
    dycat = _mm_nt(dxm, wout, 0, F32, f"out_proj_dx_{l}")
    g_wout, = _mm_tn_rows(ycat, dxm, D_MODEL, f"out_proj_dw_{l}")
    send_out = _Item(g_wout, lambda src, k: src.at[pl.ds(k * out_rows, out_rows), :], (N_DEV, out_rows, D_MODEL))
    duv, g_wm, g_bt, g_ga = _sgu_bwd(proj, p["wm"], p["bt"], p["gain"][0], dycat, f"sgu_bwd_{l}")
    dp, g_wbd, g_scale, g_gb = _pool_bwd(proj, p["wbd"], p["scale"], p["gain"][1], dycat, seq, f"pool_bwd_{l}")
    dq, dkp, dkc, g_sinks, g_tab, g_gc, *rode = _swa_bwd(proj, p["sinks"], tab, p["gain"][2], dycat, seq,
                                                         f"swa_bwd_{l}", items=ride)
    dkv = _shift_add(dkc, dkp, seq, f"swa_dkv_{l}")
    dqd, dkd, dvd, g_gd, p_gu, p_out = _sb_bwd(proj, p["gain"][3], raw, btot, dycat, seq, f"sb_bwd_{l}",
                                               items=(rows_of(g_wgu, half, p_gu), send_out))
    dproj = jnp.concatenate([duv, dp, dq, dkv, dqd, dkd.astype(BF16), dvd.astype(BF16)], axis=1)
    g_win, = _mm_tn_cols(h1, dproj, GROUP_WIDTH, 4, f"in_proj_dw_{l}")
    dx, g_norm_mix, p_in = _mm_norm_bwd(dproj, win, 0, 4, x, p["norm_mix"], dxm, f"in_proj_dx_{l}",
                                        items=(_Item(g_win, lambda src, k: src.at[k], g_win.shape),))
    ng = len(POOL_WINDOWS)
    gd = GROUP_WIDTH // ng
    small = dict(
        sgu_w=g_wm, sgu_b=g_bt.T,
        pool_w=jnp.stack([g_wbd[g * gd:(g + 1) * gd, g * gd:(g + 1) * gd] for g in range(ng)]),
        pool_scale=g_scale[0], swa_sinks=g_sinks[0],
        mix_out_gain=jnp.concatenate([g_ga[0], g_gb[0], g_gc[0], g_gd[0]]),
        norm_mix=g_norm_mix[0], norm_ffn=g_norm_ffn[0])
    return dx, (p_in, p_out, p_gu, p_down), small, g_tab, rode


def _local_step(x, target, w0, shards, sgu_w, sgu_b, pool_w, pool_scale, swa_sinks, rel_bias, mix_out_gain,
                norm_mix, norm_ffn, norm_final, seq):
    buckets = _bucket_map()
    tab = _bias_table(rel_bias, buckets, "bias_table")
    params = [_layer_params(l, sgu_w, sgu_b, pool_w, pool_scale, swa_sinks, mix_out_gain, norm_mix, norm_ffn)
              for l in range(DEPTH)]
    saved, weights = [], [w0]
    for l in range(DEPTH):
        x, s, w_next = _layer_fwd(l, x, params[l], weights[l], tab, seq, shards if l + 1 < DEPTH else None)
        saved.append(s)
        weights.append(w_next)
    loss, dx, g_final = _loss_head(x, norm_final[None], target, "loss_head")
    big, small, g_tab, upper = [None] * DEPTH, [None] * DEPTH, None, None
    for l in reversed(range(DEPTH)):
        ride = ()
        if l == 0:
            mine = _pack(_upper_layers(small), _pack_rows(_upper_layers(small)))
            ride = (_Item(mine, lambda src, k: src, (N_DEV,) + mine.shape),)
        dx, big[l], small[l], t, rode = _layer_bwd(l, dx, saved[l], params[l], weights[l], tab, seq, ride)
        g_tab = t if g_tab is None else g_tab + t
        upper = rode[0] if rode else upper
    return loss, dx, big, small, _bias_table_bwd(g_tab, buckets, "bias_table_bwd"), g_final[0], upper


def _cast_pad(w, rows, name):
    L, r, c = w.shape

    def body(w_ref, o_ref):
        if rows != r:
            o_ref[...] = jnp.zeros_like(o_ref)
        o_ref[:r, :] = w_ref[...].astype(BF16)

    return pl.pallas_call(
        body, grid=(L,), name=name,
        in_specs=[pl.BlockSpec((None, r, c), lambda l: (l, 0, 0))],
        out_specs=pl.BlockSpec((None, rows, c), lambda l: (l, 0, 0)),
        out_shape=jax.ShapeDtypeStruct((L, rows, c), BF16),
        compiler_params=_cp("parallel"))(w)


def _down_rows(ref, k):
    return ref.at[:, k // 2, pl.ds((k % 2) * DOWN_ROWS, DOWN_ROWS), :]


def _all_gather_weights(s_in, s_out, s_gu, s_down):
    L = s_in.shape[0]
    shards = (s_in, s_out, s_gu, s_down)
    down_full = jnp.zeros((L, 4, FF_PAD, D_MODEL), BF16)
    n = len(shards)

    def body(i0, i1, i2, i3, _, o0, o1, o2, o3, send_sems, recv_sems, local_sems):
        srcs = (i0, i1, i2, i3)
        outs = (o0, o1, o2, o3)
        x, y, c, _me = _my_place()
        chips = [(1 - x, y), (x, 1 - y), (1 - x, 1 - y)]

        def place(t, dev):
            k = 4 * dev[0] + 2 * dev[1] + dev[2]
            return _down_rows(outs[t], k) if t == 3 else outs[t].at[:, k]

        def copy(slot, t, block, to, from_shard):
            return pltpu.make_async_remote_copy(
                src_ref=srcs[t] if from_shard else place(t, block), dst_ref=place(t, block),
                send_sem=send_sems.at[slot, t], recv_sem=recv_sems.at[slot, t], device_id=to, device_id_type=MESH)

        me, sibling = (x, y, c), (x, y, 1 - c)
        mine = [pltpu.make_async_copy(srcs[t], place(t, me), local_sems.at[t]) for t in range(n)]
        for cp in mine:
            cp.start()
        first = [copy(0, t, me, sibling, True) for t in range(n)]
        first += [copy(1 + j, t, me, (*chip, c), True) for j, chip in enumerate(chips) for t in range(n)]
        for cp in first:
            cp.start()
        passed = []
        for j, chip in enumerate(chips):
            for t in range(n):
                copy(1 + j, t, (*chip, c), me, False).wait_recv()
                cp = copy(4 + j, t, (*chip, c), sibling, False)
                cp.start()
                passed.append(cp)
        for t in range(n):
            copy(0, t, sibling, me, False).wait_recv()
        for j, chip in enumerate(chips):
            for t in range(n):
                copy(4 + j, t, (*chip, 1 - c), me, False).wait_recv()
        for cp in first + passed:
            cp.wait_send()
        for cp in mine:
            cp.wait()

    shapes = [jax.ShapeDtypeStruct((L, N_DEV) + s.shape[1:], BF16) for s in shards[:3]]
    shapes.append(jax.ShapeDtypeStruct(down_full.shape, BF16))
    return pl.pallas_call(
        body, name="all_gather_weights", out_shape=shapes,
        in_specs=[ANY] * 5, out_specs=[ANY] * 4, input_output_aliases={4: 3},
        scratch_shapes=[pltpu.SemaphoreType.DMA((7, n)), pltpu.SemaphoreType.DMA((7, n)), pltpu.SemaphoreType.DMA((n,))],
        )(*shards, down_full)


def _adamw(w, g, m, v):
    m = ADAM_B1 * m + (1.0 - ADAM_B1) * g
    v = ADAM_B2 * v + (1.0 - ADAM_B2) * jnp.square(g)
    m_hat = m / (1.0 - ADAM_B1 ** ADAM_STEP)
    v_hat = v / (1.0 - ADAM_B2 ** ADAM_STEP)
    delta = -ADAM_LR * (m_hat / (jnp.sqrt(v_hat) + ADAM_EPS) + ADAM_WD * w)
    return delta, m, v


def _adamw_sharded(parts, w, m, v, tr, name, items=()):
    L, r, c = w.shape
    cp = parts[0].shape[-1]
    nrow = r // tr

    def body(*refs):
        p_refs, (w_ref, m_ref, v_ref, g_ref, d_ref, nm_ref, nv_ref) = refs[:L], refs[L:]
        for k in range(L):
            @pl.when(pl.program_id(0) == k)
            def _(p_ref=p_refs[k]):
                g = p_ref[0, :, :c].astype(F32)
                for dev in range(1, N_DEV):
                    g = g + p_ref[dev, :, :c].astype(F32)
                delta, nm, nv = _adamw(w_ref[...], g, m_ref[...], v_ref[...])
                g_ref[...] = g
                d_ref[...] = delta
                nm_ref[...] = nm
                nv_ref[...] = nv

    def part_spec(k):
        return pl.BlockSpec((N_DEV, tr, cp),
                            lambda l, i: (0, jnp.where(l == k, i, jnp.where(l < k, 0, nrow - 1)), 0))

    blk = pl.BlockSpec((None, tr, c), lambda l, i: (l, i, 0))
    out = jax.ShapeDtypeStruct((L, r, c), F32)
    return _call(
        body, (*parts, w, m, v), grid=(L, nrow), name=name, items=items,
        in_specs=[part_spec(k) for k in range(L)] + [blk, blk, blk],
        out_specs=[blk] * 4, out_shape=[out] * 4, sem=("arbitrary", "arbitrary"))


def _adamw_small(parts, wmv, name):
    def body(p_ref, wmv_ref, g_ref, d_ref, nm_ref, nv_ref):
        g = p_ref[0]
        for k in range(1, N_DEV):
            g = g + p_ref[k]
        delta, nm, nv = _adamw(wmv_ref[0], g, wmv_ref[1], wmv_ref[2])
        g_ref[...] = g
        d_ref[...] = delta
        nm_ref[...] = nm
        nv_ref[...] = nv

    out = jax.ShapeDtypeStruct(wmv.shape[1:], F32)
    return pl.pallas_call(body, name=name, out_shape=[out] * 4, compiler_params=_cp())(parts, wmv)


LAYERED = ("sgu_w", "sgu_b", "pool_w", "pool_scale", "swa_sinks", "mix_out_gain", "norm_mix", "norm_ffn")
SHARED = ("rel_bias", "norm_final")


def _seg_rows(a):
    return -(-a.size // 128)


def _pack_rows(parts):
    return -(-sum(_seg_rows(p) for p in parts) // 8) * 8


def _upper_layers(per_layer):
    if isinstance(per_layer, dict):
        return [per_layer[k][1:] for k in LAYERED]
    return [jnp.stack([per_layer[l][k] for l in range(1, DEPTH)]) for k in LAYERED]


def _layer_zero(stacked):
    return [stacked[k][:1] for k in LAYERED] + [stacked[k] for k in SHARED]


def _pack(parts, rows):
    segs = [jnp.pad(p.reshape(-1), (0, _seg_rows(p) * 128 - p.size)).reshape(_seg_rows(p), 128) for p in parts]
    used = sum(s.shape[0] for s in segs)
    return jnp.concatenate(segs + [jnp.zeros((rows - used, 128), F32)], axis=0)


def _pack_groups(groups, rows):
    segs = []
    for parts in groups:
        segs += [jnp.pad(p.reshape(-1), (0, _seg_rows(p) * 128 - p.size)).reshape(_seg_rows(p), 128) for p in parts]
        segs.append(jnp.zeros((rows - sum(_seg_rows(p) for p in parts), 128), F32))
    return jnp.concatenate(segs, axis=0).reshape(len(groups), rows, 128)


def _unpack(buf, like):
    out, at = [], 0
    for a in like:
        out.append(buf[at:at + _seg_rows(a)].reshape(-1)[:a.size].reshape(a.shape))
        at += _seg_rows(a)
    return out


def kernel(x, w_in, w_out, sgu_w, sgu_b, pool_w, pool_scale, swa_sinks, rel_bias, mix_out_gain, norm_mix, norm_ffn, w_gate_up, w_down, norm_final, loss_target, m_w_in, m_w_out, m_sgu_w, m_sgu_b, m_pool_w, m_pool_scale, m_swa_sinks, m_rel_bias, m_mix_out_gain, m_norm_mix, m_norm_ffn, m_w_gate_up, m_w_down, m_norm_final, v_w_in, v_w_out, v_sgu_w, v_sgu_b, v_pool_w, v_pool_scale, v_swa_sinks, v_rel_bias, v_mix_out_gain, v_norm_mix, v_norm_ffn, v_w_gate_up, v_w_down, v_norm_final):
    bl, seq, _ = x.shape
    L = w_in.shape[0]
    gu_t, m_gu_t, v_gu_t = (jnp.swapaxes(a, 1, 2) for a in (w_gate_up, m_w_gate_up, v_w_gate_up))
    shards = (_cast_pad(w_in, D_MODEL, "shard_w_in"), _cast_pad(w_out, D_MODEL // N_DEV, "shard_w_out"),
              _cast_pad(gu_t, FF_PAD, "shard_w_gate_up"), _cast_pad(w_down, DOWN_ROWS, "shard_w_down"))
    first = _all_gather_weights(*[s[:1] for s in shards])
    w0 = _as_weights(*[f[0] for f in first])
    small_w = dict(sgu_w=sgu_w, sgu_b=sgu_b, pool_w=pool_w, pool_scale=pool_scale, swa_sinks=swa_sinks,
                   rel_bias=rel_bias, mix_out_gain=mix_out_gain, norm_mix=norm_mix, norm_ffn=norm_ffn,
                   norm_final=norm_final)
    small_m = dict(sgu_w=m_sgu_w, sgu_b=m_sgu_b, pool_w=m_pool_w, pool_scale=m_pool_scale, swa_sinks=m_swa_sinks,
                   rel_bias=m_rel_bias, mix_out_gain=m_mix_out_gain, norm_mix=m_norm_mix, norm_ffn=m_norm_ffn,
                   norm_final=m_norm_final)
    small_v = dict(sgu_w=v_sgu_w, sgu_b=v_sgu_b, pool_w=v_pool_w, pool_scale=v_pool_scale, swa_sinks=v_swa_sinks,
                   rel_bias=v_rel_bias, mix_out_gain=v_mix_out_gain, norm_mix=v_norm_mix, norm_ffn=v_norm_ffn,
                   norm_final=v_norm_final)
    loss, dx, big, small, g_rel_bias, g_final, upper = _local_step(
        x.reshape(bl * seq, D_MODEL), loss_target.reshape(bl * seq, D_MODEL), w0, shards, sgu_w, sgu_b, pool_w,
        pool_scale, swa_sinks, rel_bias, mix_out_gain, norm_mix, norm_ffn, norm_final, seq)
    p_in, p_out, p_gu, p_down = ([big[l][t] for l in range(L)] for t in range(4))
    outs_gu = [jnp.swapaxes(a, 1, 2) for a in
               _adamw_sharded(p_gu, gu_t, m_gu_t, v_gu_t, FF_SHARD // 4, "adamw_w_gate_up")]
    outs_down = _adamw_sharded(p_down, w_down, m_w_down, v_w_down, DOWN_ROWS // 2, "adamw_w_down")
    outs_in = _adamw_sharded(p_in, w_in, m_w_in, v_w_in, 256, "adamw_w_in")
    lo_like = _layer_zero(small_w)
    lo_rows = _pack_rows(lo_like + [loss[0]])
    lo_mine = _pack([small[0][k][None] for k in LAYERED] + [g_rel_bias, g_final, loss[0]], lo_rows)
    *outs_out, lower = _adamw_sharded(p_out, w_out, m_w_out, v_w_out, D_MODEL // N_DEV, "adamw_w_out",
                                      items=(_Item(lo_mine, lambda src, k: src, (N_DEV,) + lo_mine.shape),))
    lo_res = _adamw_small(lower, _pack_groups([lo_like, _layer_zero(small_m), _layer_zero(small_v)], lo_rows),
                          "adamw_small_layer0")
    hi_like = _upper_layers(small_w)
    hi_rows = _pack_rows(hi_like)
    hi_res = _adamw_small(upper, _pack_groups([hi_like, _upper_layers(small_m), _upper_layers(small_v)], hi_rows),
                          "adamw_small_upper")
    loss_total = lo_res[0][sum(_seg_rows(a) for a in lo_like), 0]
    small_outs = []
    for lo_buf, hi_buf in zip(lo_res, hi_res):
        lo = dict(zip(LAYERED + SHARED, _unpack(lo_buf, lo_like)))
        hi = dict(zip(LAYERED, _unpack(hi_buf, hi_like)))
        small_outs.append({k: jnp.concatenate([lo[k], hi[k]], axis=0) if k in hi else lo[k] for k in lo})
    big_outs = dict(w_in=outs_in, w_out=outs_out, w_gate_up=outs_gu, w_down=outs_down)
    order = ("w_in", "w_out", "sgu_w", "sgu_b", "pool_w", "pool_scale", "swa_sinks", "rel_bias", "mix_out_gain",
             "norm_mix", "norm_ffn", "w_gate_up", "w_down", "norm_final")
    result = [loss_total, dx.reshape(bl, seq, D_MODEL)]
    for which in range(4):
        for name in order:
            result.append(big_outs[name][which] if name in big_outs else small_outs[which][name])
    return tuple(result)
```

```python
import functools

import jax
import jax.numpy as jnp
from jax import lax
from jax.experimental import pallas as pl
from jax.experimental.pallas import tpu as pltpu

F32 = jnp.float32
BF16 = jnp.bfloat16

N_DEV = 8
DEPTH = 4
D_MODEL = 1024
GROUP_WIDTH = 256
HEAD_DIM = 64
GROUP_HEADS = 4
BLOCK = 128
N_BUCKETS = 32
MAX_DISTANCE = 128
POOL_WINDOWS = (2, 4, 8, 16)
D_FF = 2816
FF_SHARD = D_FF // 4
FF_PAD = 768
D_FF_PAD = 4 * FF_PAD
EPS = 1e-6
ATT_SCALE = HEAD_DIM ** -0.5
ADAM_LR = 0.001
ADAM_B1 = 0.9
ADAM_B2 = 0.999
ADAM_EPS = 1e-08
ADAM_WD = 0.01
ADAM_STEP = 10
VMEM_LIMIT = 56 * 1024 * 1024
MESH_AXES = ("x", "y", "c")


def _cp(*sem):
    return pltpu.CompilerParams(dimension_semantics=sem or None, vmem_limit_bytes=VMEM_LIMIT)


_NT = (((1,), (1,)), ((), ()))
_TN = (((0,), (0,)), ((), ()))


@jax.custom_vjp
def _bdot(a, b):
    return jnp.dot(a.astype(BF16), b.astype(BF16), preferred_element_type=F32)


def _bdot_fwd(a, b):
    return _bdot(a, b), (a.astype(BF16), b.astype(BF16))


def _bdot_bwd(res, ct):
    a, b = res
    c = ct.astype(BF16)
    return (lax.dot_general(c, b, _NT, preferred_element_type=F32),
            lax.dot_general(a, c, _TN, preferred_element_type=F32))


_bdot.defvjp(_bdot_fwd, _bdot_bwd)


@jax.custom_vjp
def _bdot_nt(a, b):
    return lax.dot_general(a.astype(BF16), b.astype(BF16), _NT, preferred_element_type=F32)


def _bdot_nt_fwd(a, b):
    return _bdot_nt(a, b), (a.astype(BF16), b.astype(BF16))


def _bdot_nt_bwd(res, ct):
    a, b = res
    c = ct.astype(BF16)
    return (jnp.dot(c, b, preferred_element_type=F32),
            lax.dot_general(c, a, _TN, preferred_element_type=F32))


_bdot_nt.defvjp(_bdot_nt_fwd, _bdot_nt_bwd)


def _rms(x, g):
    return x * lax.rsqrt(jnp.mean(x * x, axis=-1, keepdims=True) + EPS) * g


def _sigmoid(x):
    return 0.5 * jnp.tanh(0.5 * x) + 0.5


def _split_dot(x, u):
    hi = x.astype(BF16)
    lo = (x - hi.astype(F32)).astype(BF16)
    return jnp.dot(hi, u, preferred_element_type=F32) + jnp.dot(lo, u, preferred_element_type=F32)


def _head_mask(h, shape):
    col = lax.broadcasted_iota(jnp.int32, shape, 1)
    return (col >= h * HEAD_DIM) & (col < (h + 1) * HEAD_DIM)


ANY = pl.BlockSpec(memory_space=pl.ANY)
MESH = pl.DeviceIdType.MESH
DOWN_ROWS = D_FF // N_DEV


def _my_place():
    x, y, c = (lax.axis_index(a) for a in MESH_AXES)
    return x, y, c, 4 * x + 2 * y + c


def _peer(x, y, c, d):
    return (x ^ (d >> 2), y ^ ((d >> 1) & 1), c ^ (d & 1))


class _Item:
    def __init__(self, src, block, dst_shape, place=None, init=None, relayed=False):
        self.src, self.block, self.dst_shape, self.init, self.relayed = src, block, dst_shape, init, relayed
        self.place = place or (lambda dst, k: dst.at[k])


def _call(body, args, *, grid, in_specs, out_specs, out_shape, sem, name, scratch_shapes=(), items=()):
    if not items:
        outs = pl.pallas_call(body, grid=grid, in_specs=in_specs, out_specs=out_specs, out_shape=out_shape, name=name,
                              scratch_shapes=list(scratch_shapes), compiler_params=_cp(*sem))(*args)
        return list(outs) if isinstance(outs, (list, tuple)) else [outs]
    n_in, n_out, n_scr, n = len(in_specs), len(out_specs), len(scratch_shapes), len(items)
    inits = [i for i, it in enumerate(items) if it.init is not None]

    def wrapped(*refs):
        core_in, srcs = refs[:n_in], refs[n_in:n_in + n]
        off = n_in + n + len(inits)
        core_out, dsts = refs[off:off + n_out], refs[off + n_out:off + n_out + n]
        scratch = refs[off + n_out + n:]
        send_sems, recv_sems, local_sems = scratch[n_scr:]
        step = pl.program_id(0)
        for a in range(1, len(grid)):
            step = step * grid[a] + pl.program_id(a)
        steps = functools.reduce(lambda p, g: p * g, grid)
        relay_step = max(0, steps - 1 - max(1, steps // 8))
        x, y, c, me = _my_place()
        direct = [i for i in range(n) if not items[i].relayed]
        relayed = [i for i in range(n) if items[i].relayed]
        chips = [(1 - x, y), (x, 1 - y), (1 - x, 1 - y)]
        sibling = (x, y, 1 - c)

        def local(i):
            return pltpu.make_async_copy(items[i].block(srcs[i], me), items[i].place(dsts[i], me), local_sems.at[i])

        def remote(d, i, sending):
            px, py, pc = _peer(x, y, c, d)
            pk = 4 * px + 2 * py + pc
            return pltpu.make_async_remote_copy(
                src_ref=items[i].block(srcs[i], pk), dst_ref=items[i].place(dsts[i], me if sending else pk),
                send_sem=send_sems.at[d - 1, i], recv_sem=recv_sems.at[d - 1, i],
                device_id=(px, py, pc), device_id_type=MESH)

        def hop(slot, i, owner, to, from_src):
            k = 4 * owner[0] + 2 * owner[1] + owner[2]
            src = items[i].block(srcs[i], me) if from_src else items[i].place(dsts[i], k)
            return pltpu.make_async_remote_copy(
                src_ref=src, dst_ref=items[i].place(dsts[i], k), send_sem=send_sems.at[slot, i],
                recv_sem=recv_sems.at[slot, i], device_id=to, device_id_type=MESH)

        @pl.when(step == 0)
        def _():
            for i in range(n):
                local(i).start()
            for d in range(1, N_DEV):
                for i in direct:
                    remote(d, i, True).start()
            for i in relayed:
                hop(0, i, (x, y, c), sibling, True).start()
                for j, chip in enumerate(chips):
                    hop(1 + j, i, (x, y, c), (*chip, c), True).start()

        body(*core_in, *core_out, *scratch[:n_scr])

        if relayed:
            @pl.when(step == relay_step)
            def _():
                for i in relayed:
                    for j, chip in enumerate(chips):
                        hop(1 + j, i, (*chip, c), (x, y, c), False).wait_recv()
                        hop(4 + j, i, (*chip, c), sibling, False).start()

        @pl.when(step == steps - 1)
        def _():
            for d in range(1, N_DEV):
                for i in direct:
                    remote(d, i, False).wait_recv()
            for i in relayed:
                hop(0, i, sibling, (x, y, c), False).wait_recv()
                for j, chip in enumerate(chips):
                    hop(4 + j, i, (*chip, 1 - c), (x, y, c), False).wait_recv()
            for d in range(1, N_DEV):
                for i in direct:
                    remote(d, i, True).wait_send()
            for i in relayed:
                for slot in range(N_DEV - 1):
                    hop(slot, i, (x, y, c), sibling, True).wait_send()
            for i in range(n):
                local(i).wait()

    outs = pl.pallas_call(
        wrapped, grid=grid, name=name,
        in_specs=list(in_specs) + [ANY] * (n + len(inits)), out_specs=list(out_specs) + [ANY] * n,
        out_shape=list(out_shape) + [jax.ShapeDtypeStruct(it.dst_shape, it.src.dtype) for it in items],
        input_output_aliases={n_in + n + j: n_out + i for j, i in enumerate(inits)},
        scratch_shapes=list(scratch_shapes) + [pltpu.SemaphoreType.DMA((N_DEV - 1, n)),
                                               pltpu.SemaphoreType.DMA((N_DEV - 1, n)), pltpu.SemaphoreType.DMA((n,))],
        compiler_params=_cp(*(["arbitrary"] * len(grid))),
    )(*args, *[it.src for it in items], *[items[i].init for i in inits])
    return list(outs)


def _norm_mm(x, g, w, l, jb, out_dtype, name, tm=1024, items=()):
    T, K = x.shape
    _, nb, _, tn = w.shape
    tm = min(tm, T)

    def body(x_ref, g_ref, w_ref, o_ref, h_ref):
        @pl.when(pl.program_id(1) == 0)
        def _():
            h_ref[...] = _rms(x_ref[...], g_ref[...]).astype(BF16)
        h = h_ref[...]
        for jj in range(jb):
            o_ref[:, jj * tn:(jj + 1) * tn] = jnp.dot(h, w_ref[jj], preferred_element_type=F32).astype(o_ref.dtype)

    return _call(
        body, (x, g, w), grid=(T // tm, nb // jb), name=name, items=items,
        in_specs=[pl.BlockSpec((tm, K), lambda i, j: (i, 0)), pl.BlockSpec((1, K), lambda i, j: (0, 0)),
                  pl.BlockSpec((None, jb, K, tn), lambda i, j: (l, j, 0, 0))],
        out_specs=[pl.BlockSpec((tm, jb * tn), lambda i, j: (i, j)), pl.BlockSpec((tm, K), lambda i, j: (i, 0))],
        out_shape=[jax.ShapeDtypeStruct((T, nb * tn), out_dtype), jax.ShapeDtypeStruct((T, K), BF16)],
        sem=("parallel", "arbitrary"))


def _mm_res(res, a, w, l, name, tn, tm=1024, items=()):
    T, K = a.shape
    N = w.shape[2]
    tm = min(tm, T)

    def body(r_ref, a_ref, w_ref, o_ref):
        o_ref[...] = r_ref[...] + jnp.dot(a_ref[...], w_ref[...], preferred_element_type=F32)

    return _call(
        body, (res, a, w), grid=(T // tm, N // tn), name=name, items=items,
        in_specs=[pl.BlockSpec((tm, tn), lambda i, j: (i, j)), pl.BlockSpec((tm, K), lambda i, j: (i, 0)),
                  pl.BlockSpec((None, K, tn), lambda i, j: (l, 0, j))],
        out_specs=[pl.BlockSpec((tm, tn), lambda i, j: (i, j))],
        out_shape=[jax.ShapeDtypeStruct((T, N), F32)],
        sem=("parallel", "parallel"))


def _mm_nt(a, w, l, out_dtype, name, tm=1024, tn=1024):
    T, K = a.shape
    N = w.shape[1]
    tm = min(tm, T)

    def body(a_ref, w_ref, o_ref):
        o_ref[...] = lax.dot_general(a_ref[...].astype(BF16), w_ref[...], _NT,
                                     preferred_element_type=F32).astype(o_ref.dtype)

    return pl.pallas_call(
        body, grid=(T // tm, N // tn), name=name,
        in_specs=[pl.BlockSpec((tm, K), lambda i, j: (i, 0)), pl.BlockSpec((None, tn, K), lambda i, j: (l, j, 0))],
        out_specs=pl.BlockSpec((tm, tn), lambda i, j: (i, j)),
        out_shape=jax.ShapeDtypeStruct((T, N), out_dtype),
        compiler_params=_cp("parallel", "parallel"))(a, w)


def _mm_norm_bwd(a, w, l, jb, x, g, dres, name, tm=1024, items=(), block_of=lambda j: j, transposed=False):
    T = a.shape[0]
    _, nb, K, tn = w.shape
    if transposed:
        tn, K = K, tn
    tm = min(tm, T)
    nj = nb // jb
    sub = min(256, tm)
    dims = (((1,), (0,)), ((), ())) if transposed else _NT

    def body(a_ref, w_ref, x_ref, g_ref, r_ref, o_ref, dg_ref):
        part = lax.dot_general(a_ref[:, :tn], w_ref[0], dims, preferred_element_type=F32)
        for jj in range(1, jb):
            part += lax.dot_general(a_ref[:, jj * tn:(jj + 1) * tn], w_ref[jj], dims, preferred_element_type=F32)

        @pl.when(pl.program_id(1) == 0)
        def _():
            o_ref[...] = part

        @pl.when(pl.program_id(1) > 0)
        def _():
            o_ref[...] += part

        @pl.when(pl.program_id(1) == nj - 1)
        def _():
            dg = jnp.zeros((1, K), F32)
            for r in range(tm // sub):
                rows = pl.ds(r * sub, sub)
                _, vjp = jax.vjp(_rms, x_ref[rows, :], g_ref[...])
                dx, dg_r = vjp(o_ref[rows, :])
                o_ref[rows, :] = r_ref[rows, :] + dx
                dg = dg + dg_r
            _acc_out(pl.program_id(0) == 0, (dg_ref,), (dg,))

    row = pl.BlockSpec((tm, K), lambda i, j: (i, 0))
    vec = pl.BlockSpec((1, K), lambda i, j: (0, 0))
    return _call(
        body, (a, w, x, g, dres), grid=(T // tm, nj), name=name, items=items,
        in_specs=[pl.BlockSpec((tm, jb * tn), lambda i, j: (i, j)),
                  pl.BlockSpec((None, jb) + w.shape[2:], lambda i, j: (l, block_of(j), 0, 0)), row, vec, row],
        out_specs=[row, vec],
        out_shape=[jax.ShapeDtypeStruct((T, K), F32), jax.ShapeDtypeStruct((1, K), F32)],
        sem=("arbitrary", "arbitrary"))


def _mm_tn_cols(lhs, rhs, tn, jb, name, tm=1024, items=(), block_of=lambda j: j):
    T, K = lhs.shape
    nb = rhs.shape[1] // tn
    tm = min(tm, T)
    nt = T // tm

    def body(l_ref, r_ref, o_ref, acc):
        part = lax.dot_general(l_ref[...], r_ref[...], _TN, preferred_element_type=F32)

        @pl.when(pl.program_id(1) == 0)
        def _():
            acc[...] = part

        @pl.when(pl.program_id(1) > 0)
        def _():
            acc[...] += part

        @pl.when(pl.program_id(1) == nt - 1)
        def _():
            for jj in range(jb):
                o_ref[jj] = acc[:, jj * tn:(jj + 1) * tn].astype(BF16)

    return _call(
        body, (lhs, rhs), grid=(nb // jb, nt), name=name, items=items,
        in_specs=[pl.BlockSpec((tm, K), lambda j, t: (t, 0)), pl.BlockSpec((tm, jb * tn), lambda j, t: (t, j))],
        out_specs=[pl.BlockSpec((jb, K, tn), lambda j, t: (block_of(j), 0, 0))],
        out_shape=[jax.ShapeDtypeStruct((nb, K, tn), BF16)],
        scratch_shapes=[pltpu.VMEM((K, jb * tn), F32)],
        sem=("parallel", "arbitrary"))


def _mm_tn_rows(lhs, rhs, tk, name, tm=1024, items=(), block_of=lambda j: j):
    T, Kl = lhs.shape
    N = rhs.shape[1]
    tm = min(tm, T)
    nt = T // tm

    def body(l_ref, r_ref, o_ref, acc):
        part = lax.dot_general(l_ref[...], r_ref[...].astype(BF16), _TN, preferred_element_type=F32)

        @pl.when(pl.program_id(1) == 0)
        def _():
            acc[...] = part

        @pl.when(pl.program_id(1) > 0)
        def _():
            acc[...] += part

        @pl.when(pl.program_id(1) == nt - 1)
        def _():
            o_ref[...] = acc[...].astype(BF16)

    return _call(
        body, (lhs, rhs), grid=(Kl // tk, nt), name=name, items=items,
        in_specs=[pl.BlockSpec((tm, tk), lambda l, t: (t, l)), pl.BlockSpec((tm, N), lambda l, t: (t, 0))],
        out_specs=[pl.BlockSpec((tk, N), lambda l, t: (block_of(l), 0))],
        out_shape=[jax.ShapeDtypeStruct((Kl, N), BF16)],
        scratch_shapes=[pltpu.VMEM((tk, N), F32)],
        sem=("parallel", "arbitrary"))


N_FF_CHUNK = D_FF_PAD // FF_PAD


def _ffn_chunk_block(j):
    return (j % 2) * N_FF_CHUNK + j // 2


def _ffn_up_act(x, g, w, name, tm=1024, items=()):
    T, K = x.shape
    tm = min(tm, T)

    def body(x_ref, g_ref, wg_ref, wu_ref, gu_ref, act_ref, h_ref):
        @pl.when(pl.program_id(1) == 0)
        def _():
            h_ref[...] = _rms(x_ref[...], g_ref[...]).astype(BF16)
        h = h_ref[...]
        gate = lax.dot_general(h, wg_ref[...], _NT, preferred_element_type=F32)
        up = lax.dot_general(h, wu_ref[...], _NT, preferred_element_type=F32)
        gu_ref[:, :FF_PAD] = gate.astype(BF16)
        gu_ref[:, FF_PAD:] = up.astype(BF16)
        act_ref[...] = (gate * _sigmoid(gate) * up).astype(BF16)

    return _call(
        body, (x, g, w, w), grid=(T // tm, N_FF_CHUNK), name=name, items=items,
        in_specs=[pl.BlockSpec((tm, K), lambda i, j: (i, 0)), pl.BlockSpec((1, K), lambda i, j: (0, 0)),
                  pl.BlockSpec((None, None, FF_PAD, K), lambda i, j: (0, j, 0, 0)),
                  pl.BlockSpec((None, None, FF_PAD, K), lambda i, j: (0, j + N_FF_CHUNK, 0, 0))],
        out_specs=[pl.BlockSpec((tm, 2 * FF_PAD), lambda i, j: (i, j)), pl.BlockSpec((tm, FF_PAD), lambda i, j: (i, j)),
                   pl.BlockSpec((tm, K), lambda i, j: (i, 0))],
        out_shape=[jax.ShapeDtypeStruct((T, 2 * D_FF_PAD), BF16), jax.ShapeDtypeStruct((T, D_FF_PAD), BF16),
                   jax.ShapeDtypeStruct((T, K), BF16)],
        sem=("parallel", "arbitrary"))


def _ffn_down_dx(dxo, w, gu, name, tm=1024, items=()):
    T, K = dxo.shape
    tm = min(tm, T)

    sub = min(512, tm)

    def body(d_ref, w_ref, gu_ref, o_ref):
        w = w_ref[...]
        for r in range(tm // sub):
            rows = slice(r * sub, (r + 1) * sub)
            d = lax.dot_general(d_ref[rows, :].astype(BF16), w, _NT, preferred_element_type=F32)
            gate = gu_ref[rows, :FF_PAD].astype(F32)
            up = gu_ref[rows, FF_PAD:].astype(F32)
            sig = _sigmoid(gate)
            silu = gate * sig
            o_ref[rows, :FF_PAD] = (d * up * (sig + silu * (1.0 - sig))).astype(BF16)
            o_ref[rows, FF_PAD:] = (d * silu).astype(BF16)

    return _call(
        body, (dxo, w, gu), grid=(T // tm, N_FF_CHUNK), name=name, items=items,
        in_specs=[pl.BlockSpec((tm, K), lambda i, j: (i, 0)), pl.BlockSpec((None, FF_PAD, K), lambda i, j: (0, j, 0)),
                  pl.BlockSpec((tm, 2 * FF_PAD), lambda i, j: (i, j))],
        out_specs=[pl.BlockSpec((tm, 2 * FF_PAD), lambda i, j: (i, j))],
        out_shape=[jax.ShapeDtypeStruct((T, 2 * D_FF_PAD), BF16)],
        sem=("parallel", "parallel"))


def _loss_head(x, g, target, name, tm=512):
    T, K = x.shape

    def loss_fn(xv, gv, tv):
        err = _rms(xv, gv) - tv
        return 0.5 * jnp.sum(jnp.mean(err * err, axis=-1, keepdims=True), axis=0, keepdims=True)

    def body(x_ref, g_ref, t_ref, l_ref, dx_ref, dg_ref):
        val, vjp = jax.vjp(lambda xv, gv: loss_fn(xv, gv, t_ref[...]), x_ref[...], g_ref[...])
        dx, dg = vjp(jnp.ones((1, 1), F32))
        dx_ref[...] = dx
        lval = jnp.broadcast_to(val, (1, 128))

        @pl.when(pl.program_id(0) == 0)
        def _():
            dg_ref[...] = dg
            l_ref[...] = lval

        @pl.when(pl.program_id(0) > 0)
        def _():
            dg_ref[...] += dg
            l_ref[...] += lval

    row = pl.BlockSpec((tm, K), lambda i: (i, 0))
    vec = pl.BlockSpec((1, K), lambda i: (0, 0))
    return pl.pallas_call(
        body, grid=(T // tm,), name=name,
        in_specs=[row, vec, row], out_specs=[pl.BlockSpec((1, 128), lambda i: (0, 0)), row, vec],
        out_shape=[jax.ShapeDtypeStruct((1, 128), F32), jax.ShapeDtypeStruct((T, K), F32),
                   jax.ShapeDtypeStruct((1, K), F32)],
        compiler_params=_cp("arbitrary"))(x, g, target)


SGU_ROWS = 4 * BLOCK


def _sgu_fn(u, v, wm, bt, gain):
    ug = jax.nn.gelu(u)
    vg = jax.nn.gelu(v)
    row = lax.broadcasted_iota(jnp.int32, (BLOCK, BLOCK), 0)
    col = lax.broadcasted_iota(jnp.int32, (BLOCK, BLOCK), 1)
    tri = row >= col
    normed = []
    for h in range(GROUP_HEADS):
        vh = vg[:, h * HEAD_DIM:(h + 1) * HEAD_DIM]
        xc = vh - jnp.mean(vh, axis=-1, keepdims=True)
        normed.append(xc * lax.rsqrt(jnp.mean(xc * xc, axis=-1, keepdims=True) + EPS))
    vn = jnp.concatenate(normed, axis=1)
    wcat = jnp.concatenate([jnp.where(tri, wm[h], 0.0) for h in range(GROUP_HEADS)], axis=1)
    bias = jnp.concatenate([jnp.broadcast_to(bt[:, h:h + 1], (BLOCK, HEAD_DIM)) for h in range(GROUP_HEADS)], axis=1)
    mixes = []
    for c in range(u.shape[0] // BLOCK):
        chunk = vn[c * BLOCK:(c + 1) * BLOCK]
        stacked = jnp.concatenate([jnp.where(_head_mask(h, chunk.shape), chunk, 0.0) for h in range(GROUP_HEADS)], axis=0)
        mixes.append(_bdot(wcat, stacked) + bias)
    return _rms(ug * jnp.concatenate(mixes, axis=0), gain)


def _sgu_fwd(proj, wm, bt, gain, name, items=()):
    T = proj.shape[0]
    rows = min(SGU_ROWS, T)

    def body(u_ref, v_ref, w_ref, b_ref, g_ref, o_ref):
        o_ref[...] = _sgu_fn(u_ref[...], v_ref[...], w_ref[...], b_ref[...], g_ref[...]).astype(BF16)

    full = lambda shape: pl.BlockSpec(shape, lambda i: (0,) * len(shape))
    return _call(
        body, (proj, proj, wm, bt, gain), grid=(T // rows,), name=name, items=items,
        in_specs=[pl.BlockSpec((rows, GROUP_WIDTH), lambda i: (i, 0)), pl.BlockSpec((rows, GROUP_WIDTH), lambda i: (i, 1)),
                  full(wm.shape), full(bt.shape), full(gain.shape)],
        out_specs=[pl.BlockSpec((rows, GROUP_WIDTH), lambda i: (i, 0))],
        out_shape=[jax.ShapeDtypeStruct((T, GROUP_WIDTH), BF16)],
        sem=("parallel",))


def _acc_out(first, refs, vals):
    @pl.when(first)
    def _():
        for r, v in zip(refs, vals):
            r[...] = v

    @pl.when(jnp.logical_not(first))
    def _():
        for r, v in zip(refs, vals):
            r[...] += v


def _sgu_bwd(proj, wm, bt, gain, dy, name, items=()):
    T = proj.shape[0]

    def body(u_ref, v_ref, w_ref, b_ref, g_ref, dy_ref, duv_ref, dw_ref, db_ref, dg_ref):
        _, vjp = jax.vjp(_sgu_fn, u_ref[...], v_ref[...], w_ref[...], b_ref[...], g_ref[...])
        du, dv, dw, db, dg = vjp(dy_ref[...])
        duv_ref[:, :GROUP_WIDTH] = du.astype(BF16)
        duv_ref[:, GROUP_WIDTH:] = dv.astype(BF16)
        _acc_out(pl.program_id(0) == 0, (dw_ref, db_ref, dg_ref), (dw, db, dg))

    full = lambda shape: pl.BlockSpec(shape, lambda i: (0,) * len(shape))
    rows = min(SGU_ROWS, T)
    return _call(
        body, (proj, proj, wm, bt, gain, dy), grid=(T // rows,), name=name, items=items,
        in_specs=[pl.BlockSpec((rows, GROUP_WIDTH), lambda i: (i, 0)), pl.BlockSpec((rows, GROUP_WIDTH), lambda i: (i, 1)),
                  full(wm.shape), full(bt.shape), full(gain.shape),
                  pl.BlockSpec((rows, GROUP_WIDTH), lambda i: (i, 0))],
        out_specs=[pl.BlockSpec((rows, 2 * GROUP_WIDTH), lambda i: (i, 0)), full(wm.shape), full(bt.shape), full(gain.shape)],
        out_shape=[jax.ShapeDtypeStruct((T, 2 * GROUP_WIDTH), BF16), jax.ShapeDtypeStruct(wm.shape, F32),
                   jax.ShapeDtypeStruct(bt.shape, F32), jax.ShapeDtypeStruct(gain.shape, F32)],
        sem=("arbitrary",))


def _pool_consts(seq):
    t = lax.broadcasted_iota(jnp.int32, (seq, GROUP_WIDTH), 0)
    grp = lax.broadcasted_iota(jnp.int32, (seq, GROUP_WIDTH), 1) // (GROUP_WIDTH // len(POOL_WINDOWS))
    win = jnp.where(grp == 0, POOL_WINDOWS[0], jnp.where(grp == 1, POOL_WINDOWS[1],
                    jnp.where(grp == 2, POOL_WINDOWS[2], POOL_WINDOWS[3])))
    count = jnp.minimum(t + 1, win).astype(F32)
    return t, grp, count


def _by_group(grp, vals):
    return jnp.where(grp == 0, vals[0], jnp.where(grp == 1, vals[1], jnp.where(grp == 2, vals[2], vals[3])))


def _window_sums(x, t, seq, back):
    def shift(a, k):
        if back:
            return jnp.where(t >= k, pltpu.roll(a, k, 0), 0.0)
        return jnp.where(t < seq - k, pltpu.roll(a, seq - k, 0), 0.0)
    sums = []
    a, k = x, 1
    for _ in POOL_WINDOWS:
        a = a + shift(a, k)
        sums.append(a)
        k *= 2
    return sums


def _pool_tail(y, wbd, scale, gain):
    return _rms(_bdot(y, wbd) * scale, gain)


def _pool_fwd(proj, wbd, scale, gain, seq, name):
    T = proj.shape[0]

    def body(p_ref, w_ref, s_ref, g_ref, o_ref):
        p = p_ref[...]
        t, grp, count = _pool_consts(seq)
        y = _by_group(grp, _window_sums(p, t, seq, True)) / count - p
        o_ref[...] = _pool_tail(y, w_ref[...], s_ref[...], g_ref[...]).astype(BF16)

    full = lambda shape: pl.BlockSpec(shape, lambda b: (0,) * len(shape))
    return pl.pallas_call(
        body, grid=(T // seq,), name=name,
        in_specs=[pl.BlockSpec((seq, GROUP_WIDTH), lambda b: (b, 2)), full(wbd.shape), full(scale.shape), full(gain.shape)],
        out_specs=pl.BlockSpec((seq, GROUP_WIDTH), lambda b: (b, 0)),
        out_shape=jax.ShapeDtypeStruct((T, GROUP_WIDTH), BF16),
        compiler_params=_cp("parallel"))(proj, wbd, scale, gain)


def _pool_bwd(proj, wbd, scale, gain, dy, seq, name):
    T = proj.shape[0]

    def body(p_ref, w_ref, s_ref, g_ref, dy_ref, dp_ref, dw_ref, ds_ref, dg_ref):
        p = p_ref[...]
        t, grp, count = _pool_consts(seq)
        y = _by_group(grp, _window_sums(p, t, seq, True)) / count - p
        _, vjp = jax.vjp(_pool_tail, y, w_ref[...], s_ref[...], g_ref[...])
        d_y, dw, ds, dg = vjp(dy_ref[...])
        dp = _by_group(grp, _window_sums(d_y / count, t, seq, False)) - d_y
        dp_ref[...] = dp.astype(BF16)
        _acc_out(pl.program_id(0) == 0, (dw_ref, ds_ref, dg_ref), (dw, ds, dg))

    full = lambda shape: pl.BlockSpec(shape, lambda b: (0,) * len(shape))
    return pl.pallas_call(
        body, grid=(T // seq,), name=name,
        in_specs=[pl.BlockSpec((seq, GROUP_WIDTH), lambda b: (b, 2)), full(wbd.shape), full(scale.shape), full(gain.shape),
                  pl.BlockSpec((seq, GROUP_WIDTH), lambda b: (b, 1))],
        out_specs=[pl.BlockSpec((seq, GROUP_WIDTH), lambda b: (b, 0)), full(wbd.shape), full(scale.shape), full(gain.shape)],
        out_shape=[jax.ShapeDtypeStruct((T, GROUP_WIDTH), BF16), jax.ShapeDtypeStruct(wbd.shape, F32),
                   jax.ShapeDtypeStruct(scale.shape, F32), jax.ShapeDtypeStruct(gain.shape, F32)],
        compiler_params=_cp("arbitrary"))(proj, wbd, scale, gain, dy)


def _swa_fn(q, kv_prev, kv_cur, sinks, tab, gain, first):
    half = GROUP_WIDTH // 2
    k2 = jnp.concatenate([kv_prev[:, :half], kv_cur[:, :half]], axis=0)
    v2 = jnp.concatenate([kv_prev[:, half:], kv_cur[:, half:]], axis=0)
    per_query_head = lambda a: jnp.concatenate(
        [a[:, (h // 2) * HEAD_DIM:(h // 2 + 1) * HEAD_DIM] for h in range(GROUP_HEADS)], axis=1)
    qs = jnp.concatenate([jnp.where(_head_mask(h, q.shape), q, 0.0) for h in range(GROUP_HEADS)], axis=0)
    qi = lax.broadcasted_iota(jnp.int32, (HEAD_ROWS, 2 * BLOCK), 0) & (BLOCK - 1)
    kj = lax.broadcasted_iota(jnp.int32, (HEAD_ROWS, 2 * BLOCK), 1)
    dist = qi + BLOCK - kj
    mask = (dist >= 0) & (dist < BLOCK) & ((kj >= BLOCK) | jnp.logical_not(first))
    logits = _bdot_nt(qs, per_query_head(k2)) * ATT_SCALE + tab.reshape(HEAD_ROWS, 2 * BLOCK)
    logits = jnp.where(mask, logits, -1e30)
    sink = jnp.concatenate([jnp.broadcast_to(sinks[:, h:h + 1], (BLOCK, 1)) for h in range(GROUP_HEADS)], axis=0)
    m = lax.stop_gradient(jnp.maximum(jnp.max(logits, axis=1, keepdims=True), sink))
    p = jnp.exp(logits - m)
    probs = p / (jnp.sum(p, axis=1, keepdims=True) + jnp.exp(sink - m))
    out = _bdot(probs, per_query_head(v2))
    y = jnp.zeros_like(q)
    for h in range(GROUP_HEADS):
        y = y + jnp.where(_head_mask(h, q.shape), out[h * BLOCK:(h + 1) * BLOCK], 0.0)
    return _rms(y, gain)


def _swa_specs(nblk):
    q = pl.BlockSpec((BLOCK, GROUP_WIDTH), lambda b, i: (b * nblk + i, 3))
    cur = pl.BlockSpec((BLOCK, GROUP_WIDTH), lambda b, i: (b * nblk + i, 4))
    prev = pl.BlockSpec((BLOCK, GROUP_WIDTH), lambda b, i: (b * nblk + jnp.maximum(i - 1, 0), 4))
    return q, prev, cur


def _swa_fwd(proj, sinks, tab, gain, seq, name, items=()):
    T = proj.shape[0]
    nblk = seq // BLOCK

    def body(q_ref, kp_ref, kc_ref, s_ref, t_ref, g_ref, o_ref):
        o_ref[...] = _swa_fn(q_ref[...], kp_ref[...], kc_ref[...], s_ref[...], t_ref[...], g_ref[...],
                             pl.program_id(1) == 0).astype(BF16)

    full = lambda shape: pl.BlockSpec(shape, lambda b, i: (0,) * len(shape))
    return _call(
        body, (proj, proj, proj, sinks, tab, gain), grid=(T // seq, nblk), name=name, items=items,
        in_specs=[*_swa_specs(nblk), full(sinks.shape), full(tab.shape), full(gain.shape)],
        out_specs=[pl.BlockSpec((BLOCK, GROUP_WIDTH), lambda b, i: (b * nblk + i, 0))],
        out_shape=[jax.ShapeDtypeStruct((T, GROUP_WIDTH), BF16)],
        sem=("parallel", "parallel"))


def _swa_bwd(proj, sinks, tab, gain, dy, seq, name, items=()):
    T = proj.shape[0]
    nblk = seq // BLOCK

    def body(q_ref, kp_ref, kc_ref, s_ref, t_ref, g_ref, dy_ref, dq_ref, dkp_ref, dkc_ref, ds_ref, dt_ref, dg_ref):
        first = pl.program_id(1) == 0
        fn = functools.partial(_swa_fn, first=first)
        _, vjp = jax.vjp(fn, q_ref[...], kp_ref[...], kc_ref[...], s_ref[...], t_ref[...], g_ref[...])
        dq, dkp, dkc, ds, dt, dg = vjp(dy_ref[...])
        dq_ref[...] = dq.astype(BF16)
        dkp_ref[...] = dkp
        dkc_ref[...] = dkc
        _acc_out((pl.program_id(0) == 0) & first, (ds_ref, dt_ref, dg_ref), (ds, dt, dg))

    full = lambda shape: pl.BlockSpec(shape, lambda b, i: (0,) * len(shape))
    blk = lambda c: pl.BlockSpec((BLOCK, GROUP_WIDTH), lambda b, i: (b * nblk + i, c))
    return _call(
        body, (proj, proj, proj, sinks, tab, gain, dy), grid=(T // seq, nblk), name=name, items=items,
        in_specs=[*_swa_specs(nblk), full(sinks.shape), full(tab.shape), full(gain.shape), blk(2)],
        out_specs=[blk(0), blk(0), blk(0), full(sinks.shape), full(tab.shape), full(gain.shape)],
        out_shape=[jax.ShapeDtypeStruct((T, GROUP_WIDTH), BF16), jax.ShapeDtypeStruct((T, GROUP_WIDTH), F32),
                   jax.ShapeDtypeStruct((T, GROUP_WIDTH), F32), jax.ShapeDtypeStruct(sinks.shape, F32),
                   jax.ShapeDtypeStruct(tab.shape, F32), jax.ShapeDtypeStruct(gain.shape, F32)],
        sem=("arbitrary", "arbitrary"))


def _shift_add(cur, prev, seq, name):
    T = cur.shape[0]
    nblk = seq // BLOCK

    def body(c_ref, p_ref, o_ref):
        last = pl.program_id(1) == nblk - 1
        o_ref[...] = (c_ref[...] + jnp.where(last, 0.0, p_ref[...])).astype(BF16)

    return pl.pallas_call(
        body, grid=(T // seq, nblk), name=name,
        in_specs=[pl.BlockSpec((BLOCK, GROUP_WIDTH), lambda b, i: (b * nblk + i, 0)),
                  pl.BlockSpec((BLOCK, GROUP_WIDTH), lambda b, i: (b * nblk + jnp.minimum(i + 1, nblk - 1), 0))],
        out_specs=pl.BlockSpec((BLOCK, GROUP_WIDTH), lambda b, i: (b * nblk + i, 0)),
        out_shape=jax.ShapeDtypeStruct((T, GROUP_WIDTH), BF16),
        compiler_params=_cp("parallel", "parallel"))(cur, prev)


def _t5_bucket(dist):
    max_exact = N_BUCKETS // 2
    df = jnp.maximum(dist, 1).astype(F32)
    large = max_exact + (jnp.log(df / max_exact) / jnp.log(jnp.float32(MAX_DISTANCE / max_exact))
                         * (N_BUCKETS - max_exact)).astype(jnp.int32)
    return jnp.where(dist < max_exact, dist, jnp.minimum(large, N_BUCKETS - 1))


def _bucket_map():
    dist = (jnp.arange(BLOCK)[:, None] + BLOCK) - jnp.arange(2 * BLOCK)[None, :]
    return _t5_bucket(jnp.clip(dist, 0, BLOCK - 1)).astype(jnp.int32)


def _bias_table(rel_bias, buckets, name):
    def body(rb_ref, bk_ref, o_ref):
        bk = bk_ref[...]
        rb = rb_ref[...]
        for h in range(GROUP_HEADS):
            acc = jnp.zeros((BLOCK, 2 * BLOCK), F32)
            for b in range(N_BUCKETS):
                acc = jnp.where(bk == b, rb[b:b + 1, h:h + 1], acc)
            o_ref[h] = acc

    return pl.pallas_call(body, name=name, out_shape=jax.ShapeDtypeStruct((GROUP_HEADS, BLOCK, 2 * BLOCK), F32),
                          compiler_params=_cp())(rel_bias, buckets)


def _bias_table_bwd(dtab, buckets, name):
    def body(dt_ref, bk_ref, o_ref):
        bk = bk_ref[...]
        row = lax.broadcasted_iota(jnp.int32, (N_BUCKETS, GROUP_HEADS), 0)
        col = lax.broadcasted_iota(jnp.int32, (N_BUCKETS, GROUP_HEADS), 1)
        acc = jnp.zeros((N_BUCKETS, GROUP_HEADS), F32)
        for h in range(GROUP_HEADS):
            dt = dt_ref[h]
            for b in range(N_BUCKETS):
                s = jnp.sum(jnp.where(bk == b, dt, 0.0), keepdims=True)
                acc = acc + jnp.where((row == b) & (col == h), s, 0.0)
        o_ref[...] = acc

    return pl.pallas_call(body, name=name, out_shape=jax.ShapeDtypeStruct((N_BUCKETS, GROUP_HEADS), F32),
                          compiler_params=_cp())(dtab, buckets)


HEAD_ROWS = GROUP_HEADS * BLOCK


def _stack_heads(x):
    return jnp.concatenate([jnp.where(_head_mask(h, x.shape), x, 0.0) for h in range(GROUP_HEADS)], axis=0).astype(BF16)


def _sb_tile(qs, kb, q0, k0):
    z = lax.dot_general(qs, kb, _NT, preferred_element_type=F32)
    row = lax.broadcasted_iota(jnp.int32, (HEAD_ROWS, BLOCK), 0) & (BLOCK - 1)
    col = lax.broadcasted_iota(jnp.int32, (HEAD_ROWS, BLOCK), 1)
    causal = (k0 + col) < (q0 + row)
    ls_neg = -(jnp.maximum(z, 0.0) + jnp.log(1.0 + jnp.exp(-jnp.abs(z))))
    return jnp.where(causal, ls_neg, 0.0), ls_neg + z, causal


SB_DEAD = -70.0
SB_FIRST_LANE = GROUP_HEADS


def _tri(strict_upper_src):
    r = lax.broadcasted_iota(jnp.int32, (BLOCK, BLOCK), 0)
    c = lax.broadcasted_iota(jnp.int32, (BLOCK, BLOCK), 1)
    cond = {"gt": r > c, "le": r <= c, "lt": r < c}[strict_upper_src]
    return jnp.where(cond, 1.0, 0.0).astype(BF16)


def _sb_fwd(proj, gain, seq, name, items=()):
    T = proj.shape[0]
    nblk = seq // BLOCK

    def body(q_ref, k_ref, v_ref, g_ref, o_ref, raw_ref, bt_ref):
        i = pl.program_id(1)
        q = q_ref[...]
        u_gt = _tri("gt")
        lane = lax.broadcasted_iota(jnp.int32, (BLOCK, BLOCK), 1)
        heads = [slice(h * HEAD_DIM, (h + 1) * HEAD_DIM) for h in range(GROUP_HEADS)]
        rows = [slice(h * BLOCK, (h + 1) * BLOCK) for h in range(GROUP_HEADS)]
        qs = _stack_heads(q * ATT_SCALE)

        def live(carry):
            j, _, cb = carry
            return (j >= 0) & (jnp.max(cb) > SB_DEAD)

        def step(carry):
            j, accs, cb = carry
            ks = pl.multiple_of(j * BLOCK, BLOCK)
            kb = k_ref[pl.ds(ks, BLOCK), :].astype(BF16)
            vb = v_ref[pl.ds(ks, BLOCK), :].astype(BF16)
            b, a, causal = _sb_tile(qs, kb, i * BLOCK, j * BLOCK)
            tail = _split_dot(b, u_gt) + cb
            w = jnp.where(causal, jnp.exp(a + tail), 0.0).astype(BF16)
            accs = tuple(accs[h] + jnp.dot(w[rows[h]], vb[:, hs], preferred_element_type=F32)
                         for h, hs in enumerate(heads))
            return j - 1, accs, cb + jnp.sum(b, axis=1, keepdims=True)

        zero_acc = tuple(jnp.zeros((BLOCK, HEAD_DIM), F32) for _ in heads)
        j_end, accs, cb = lax.while_loop(live, step, (i, zero_acc, jnp.zeros((HEAD_ROWS, 1), F32)))
        side = jnp.where(lane == SB_FIRST_LANE, (j_end + 1).astype(F32), 0.0)
        for h in range(GROUP_HEADS):
            side = jnp.where(lane == h, cb[rows[h]], side)
        raw = jnp.concatenate(accs, axis=1)
        raw_ref[...] = raw
        bt_ref[...] = side
        o_ref[...] = _rms(raw, g_ref[...]).astype(BF16)

    return _call(
        body, (proj, proj, proj, gain), grid=(T // seq, nblk), name=name, items=items,
        in_specs=[pl.BlockSpec((BLOCK, GROUP_WIDTH), lambda b, i: (b * nblk + i, 5)),
                  pl.BlockSpec((seq, GROUP_WIDTH), lambda b, i: (b, 6)),
                  pl.BlockSpec((seq, GROUP_WIDTH), lambda b, i: (b, 7)),
                  pl.BlockSpec(gain.shape, lambda b, i: (0, 0))],
        out_specs=[pl.BlockSpec((BLOCK, GROUP_WIDTH), lambda b, i: (b * nblk + i, 0)),
                   pl.BlockSpec((BLOCK, GROUP_WIDTH), lambda b, i: (b * nblk + i, 0)),
                   pl.BlockSpec((BLOCK, BLOCK), lambda b, i: (b * nblk + i, 0))],
        out_shape=[jax.ShapeDtypeStruct((T, GROUP_WIDTH), BF16), jax.ShapeDtypeStruct((T, GROUP_WIDTH), F32),
                   jax.ShapeDtypeStruct((T, BLOCK), F32)],
        sem=("parallel", "parallel"))


def _sb_bwd(proj, gain, raw, btot, dy, seq, name, items=()):
    T = proj.shape[0]
    nblk = seq // BLOCK

    def body(q_ref, k_ref, v_ref, g_ref, raw_ref, bt_ref, dy_ref, dq_ref, dk_ref, dv_ref, dg_ref):
        i = pl.program_id(1)

        @pl.when(i == 0)
        def _():
            dk_ref[...] = jnp.zeros_like(dk_ref)
            dv_ref[...] = jnp.zeros_like(dv_ref)

        rawv = raw_ref[...]
        _, vjp = jax.vjp(_rms, rawv, g_ref[...])
        do, dg = vjp(dy_ref[...])
        _acc_out((pl.program_id(0) == 0) & (i == 0), (dg_ref,), (dg,))
        q = q_ref[...]
        bt = bt_ref[...]
        u_le = _tri("le")
        u_lt = _tri("lt")
        heads = [slice(h * HEAD_DIM, (h + 1) * HEAD_DIM) for h in range(GROUP_HEADS)]
        rows = [slice(h * BLOCK, (h + 1) * BLOCK) for h in range(GROUP_HEADS)]
        qs = _stack_heads(q * ATT_SCALE)
        dos = _stack_heads(do)
        bts = jnp.concatenate([bt[:, h:h + 1] for h in range(GROUP_HEADS)], axis=0)
        first = jnp.max(bt[:, SB_FIRST_LANE:SB_FIRST_LANE + 1]).astype(jnp.int32)
        first = jnp.minimum(jnp.maximum(first, 0), i)

        def step(j, carry):
            dqs, cb, cg = carry
            ks = pl.multiple_of(j * BLOCK, BLOCK)
            kb = k_ref[pl.ds(ks, BLOCK), :].astype(BF16)
            vb = v_ref[pl.ds(ks, BLOCK), :].astype(BF16)
            b, a, causal = _sb_tile(qs, kb, i * BLOCK, j * BLOCK)
            tail = bts - (_split_dot(b, u_le) + cb)
            w = jnp.where(causal, jnp.exp(a + tail), 0.0)
            sig = jnp.exp(a)
            g = w * lax.dot_general(dos, vb, _NT, preferred_element_type=F32)
            gpre = _split_dot(g, u_lt) + cg
            dz = jnp.where(causal, g * (1.0 - sig) - gpre * sig, 0.0).astype(BF16)
            dqs = tuple(dqs[h] + jnp.dot(dz[rows[h]], kb[:, hs], preferred_element_type=F32)
                        for h, hs in enumerate(heads))
            dk_ref[pl.ds(ks, BLOCK), :] += lax.dot_general(dz, qs, _TN, preferred_element_type=F32)
            dv_ref[pl.ds(ks, BLOCK), :] += lax.dot_general(w.astype(BF16), dos, _TN, preferred_element_type=F32)
            return dqs, cb + jnp.sum(b, axis=1, keepdims=True), cg + jnp.sum(g, axis=1, keepdims=True)

        zero_dq = tuple(jnp.zeros((BLOCK, HEAD_DIM), F32) for _ in heads)
        zero = jnp.zeros((HEAD_ROWS, 1), F32)
        dqs, _, _ = lax.fori_loop(first, i + 1, step, (zero_dq, zero, zero))
        dq_ref[...] = (jnp.concatenate(dqs, axis=1) * ATT_SCALE).astype(BF16)

    blk = lambda c: pl.BlockSpec((BLOCK, GROUP_WIDTH), lambda b, i: (b * nblk + i, c))
    seqblk = lambda c: pl.BlockSpec((seq, GROUP_WIDTH), lambda b, i: (b, c))
    vec = pl.BlockSpec(gain.shape, lambda b, i: (0, 0))
    return _call(
        body, (proj, proj, proj, gain, raw, btot, dy), grid=(T // seq, nblk), name=name, items=items,
        in_specs=[blk(5), seqblk(6), seqblk(7), vec, blk(0), pl.BlockSpec((BLOCK, BLOCK), lambda b, i: (b * nblk + i, 0)),
                  blk(3)],
        out_specs=[blk(0), seqblk(0), seqblk(0), vec],
        out_shape=[jax.ShapeDtypeStruct((T, GROUP_WIDTH), BF16), jax.ShapeDtypeStruct((T, GROUP_WIDTH), F32),
                   jax.ShapeDtypeStruct((T, GROUP_WIDTH), F32), jax.ShapeDtypeStruct(gain.shape, F32)],
        sem=("arbitrary", "arbitrary"))


def _layer_params(l, sgu_w, sgu_b, pool_w, pool_scale, swa_sinks, mix_out_gain, norm_mix, norm_ffn):
    gains = mix_out_gain[l].reshape(4, 1, GROUP_WIDTH)
    return dict(
        wm=sgu_w[l], bt=sgu_b[l].T,
        wbd=jax.scipy.linalg.block_diag(*[pool_w[l, g] for g in range(len(POOL_WINDOWS))]),
        scale=pool_scale[l][None], sinks=swa_sinks[l][None],
        gain=[gains[m] for m in range(4)], norm_mix=norm_mix[l][None], norm_ffn=norm_ffn[l][None])


def _as_weights(g_in, g_out, g_gu, g_down):
    return (g_in[None], g_out.reshape(1, D_MODEL, D_MODEL), g_gu[None], g_down.reshape(1, D_FF_PAD, D_MODEL))


def _gather_item(src, l, dst_shape, rows=None, down=False, init=None):
    r0, nr = rows or (0, src.shape[1])
    if down:
        place = lambda dst, k: dst.at[k // 2, pl.ds((k % 2) * DOWN_ROWS + r0, nr), :]
    else:
        place = lambda dst, k: dst.at[k, pl.ds(r0, nr), :]
    return _Item(src, lambda s, k: s.at[l, pl.ds(r0, nr), :], dst_shape, place, init, relayed=True)


def _layer_fwd(l, x, p, w, tab, seq, shards):
    win, wout, wgu, wd = w
    nl = l + 1
    s_in, s_out, s_gu, s_down = shards or (None,) * 4
    some = lambda *make: [m() for m in make] if shards else []
    proj, h1, *g_down = _norm_mm(x, p["norm_mix"], win, 0, 4, F32, f"in_proj_{l}", items=some(
        lambda: _gather_item(s_down, nl, (4, FF_PAD, D_MODEL), None, True, jnp.zeros((4, FF_PAD, D_MODEL), BF16))))
    ya, = _sgu_fwd(proj, p["wm"], p["bt"], p["gain"][0], f"sgu_fwd_{l}")
    yb = _pool_fwd(proj, p["wbd"], p["scale"], p["gain"][1], seq, f"pool_fwd_{l}")
    yc, *g_in_out = _swa_fwd(proj, p["sinks"], tab, p["gain"][2], seq, f"swa_fwd_{l}", items=some(
        lambda: _gather_item(s_in, nl, (N_DEV,) + s_in.shape[1:]),
        lambda: _gather_item(s_out, nl, (N_DEV,) + s_out.shape[1:])))
    half = FF_PAD // 2
    gu_shape = (N_DEV, FF_PAD, D_MODEL)
    yd, raw, btot, *g_gu = _sb_fwd(proj, p["gain"][3], seq, f"sb_fwd_{l}", items=some(
        lambda: _gather_item(s_gu, nl, gu_shape, (0, half))))
    ycat = jnp.concatenate([ya, yb, yc, yd], axis=1)
    xm, = _mm_res(x, ycat, wout, 0, f"out_proj_{l}", tn=D_MODEL)
    gu, act, h2, *g_gu = _ffn_up_act(xm, p["norm_ffn"], wgu, f"ffn_up_{l}", items=some(
        lambda: _gather_item(s_gu, nl, gu_shape, (half, half), init=g_gu[0])))
    xo, = _mm_res(xm, act, wd, 0, f"ffn_down_{l}", tn=D_MODEL // 2)
    w_next = _as_weights(g_in_out[0], g_in_out[1], g_gu[0], g_down[0]) if shards else None
    return xo, (x, proj, h1, ycat, raw, btot, xm, gu, h2, act), w_next


def _rows_item(g, r0, nr, init=None):
    cut = lambda a, k: a.at[k, pl.ds(r0, nr), :]
    return _Item(g, cut, g.shape, cut, init)


def _layer_bwd(l, dxo, saved, p, w, tab, seq, pending, ride):
    win, wout, wgu, wd = w
    x, proj, h1, ycat, raw, btot, xm, gu, h2, act = saved
    out_rows = D_MODEL // N_DEV
    in_half = D_MODEL // 2
    pieces = 2 if ride else 3
    piece = FF_PAD // pieces

    above = [_rows_item(pending[0], in_half, in_half, pending[1])] if pending else []
    dgu, *p_in_above = _ffn_down_dx(dxo, wd, gu, f"ffn_down_dx_{l}", items=above)
    g_wd, = _mm_tn_rows(act, dxo, FF_PAD, f"ffn_down_dw_{l}")
    send_down = _Item(g_wd, lambda src, k: src.at[pl.ds((k // 2) * FF_PAD + (k % 2) * DOWN_ROWS, DOWN_ROWS), :],
                      (N_DEV, DOWN_ROWS, D_MODEL))
    g_wgu, p_down = _mm_tn_rows(dgu, h2, FF_PAD, f"ffn_up_dw_{l}", items=(send_down,), block_of=_ffn_chunk_block)
    g_wgu = g_wgu.reshape(N_DEV, FF_PAD, D_MODEL)
    dxm, g_norm_ffn, p_gu = _mm_norm_bwd(dgu, wgu, 0, 1, xm, p["norm_ffn"], dxo, f"ffn_up_dx_{l}",
                                         items=(_rows_item(g_wgu, 0, piece),), block_of=_ffn_chunk_block,
                                         transposed=True)
    dycat = _mm_nt(dxm, wout, 0, F32, f"out_proj_dx_{l}")
    g_wout, = _mm_tn_rows(ycat, dxm, D_MODEL, f"out_proj_dw_{l}")
    send_out = _Item(g_wout, lambda src, k: src.at[pl.ds(k * out_rows, out_rows), :], (N_DEV, out_rows, D_MODEL))
    duv, g_wm, g_bt, g_ga = _sgu_bwd(proj, p["wm"], p["bt"], p["gain"][0], dycat, f"sgu_bwd_{l}")
    dp, g_wbd, g_scale, g_gb = _pool_bwd(proj, p["wbd"], p["scale"], p["gain"][1], dycat, seq, f"pool_bwd_{l}")
    beside_swa = list(ride) if ride else [_rows_item(g_wgu, piece, piece, p_gu)]
    dq, dkp, dkc, g_sinks, g_tab, g_gc, *rode = _swa_bwd(proj, p["sinks"], tab, p["gain"][2], dycat, seq,
                                                         f"swa_bwd_{l}", items=beside_swa)
    if not ride:
        p_gu, rode = rode[0], []
    dkv = _shift_add(dkc, dkp, seq, f"swa_dkv_{l}")
    dqd, dkd, dvd, g_gd, p_gu, p_out = _sb_bwd(
        proj, p["gain"][3], raw, btot, dycat, seq, f"sb_bwd_{l}",
        items=(_rows_item(g_wgu, (pieces - 1) * piece, piece, p_gu), send_out))
    dproj = jnp.concatenate([duv, dp, dq, dkv, dqd, dkd.astype(BF16), dvd.astype(BF16)], axis=1)
    g_win, = _mm_tn_cols(h1, dproj, GROUP_WIDTH, 4, f"in_proj_dw_{l}")
    defer = l > 0
    dx, g_norm_mix, p_in = _mm_norm_bwd(dproj, win, 0, 4, x, p["norm_mix"], dxm, f"in_proj_dx_{l}",
                                        items=(_rows_item(g_win, 0, in_half if defer else D_MODEL),))
    pending = (g_win, p_in) if defer else None
    ng = len(POOL_WINDOWS)
    gd = GROUP_WIDTH // ng
    small = dict(
        sgu_w=g_wm, sgu_b=g_bt.T,
        pool_w=jnp.stack([g_wbd[g * gd:(g + 1) * gd, g * gd:(g + 1) * gd] for g in range(ng)]),
        pool_scale=g_scale[0], swa_sinks=g_sinks[0],
        mix_out_gain=jnp.concatenate([g_ga[0], g_gb[0], g_gc[0], g_gd[0]]),
        norm_mix=g_norm_mix[0], norm_ffn=g_norm_ffn[0])
    return dx, [p_in, p_out, p_gu, p_down], small, g_tab, rode, p_in_above, pending


def _local_step(x, target, w0, shards, sgu_w, sgu_b, pool_w, pool_scale, swa_sinks, rel_bias, mix_out_gain,
                norm_mix, norm_ffn, norm_final, seq):
    buckets = _bucket_map()
    tab = _bias_table(rel_bias, buckets, "bias_table")
    params = [_layer_params(l, sgu_w, sgu_b, pool_w, pool_scale, swa_sinks, mix_out_gain, norm_mix, norm_ffn)
              for l in range(DEPTH)]
    saved, weights = [], [w0]
    for l in range(DEPTH):
        x, s, w_next = _layer_fwd(l, x, params[l], weights[l], tab, seq, shards if l + 1 < DEPTH else None)
        saved.append(s)
        weights.append(w_next)
    loss, dx, g_final = _loss_head(x, norm_final[None], target, "loss_head")
    big, small, g_tab, upper, pending = [None] * DEPTH, [None] * DEPTH, None, None, None
    for l in reversed(range(DEPTH)):
        ride = ()
        if l == 0:
            mine = _pack(_upper_layers(small), _pack_rows(_upper_layers(small)))
            ride = (_Item(mine, lambda src, k: src, (N_DEV,) + mine.shape),)
        dx, big[l], small[l], t, rode, done_above, pending = _layer_bwd(
            l, dx, saved[l], params[l], weights[l], tab, seq, pending, ride)
        g_tab = t if g_tab is None else g_tab + t
        upper = rode[0] if rode else upper
        if done_above:
            big[l + 1][0] = done_above[0]
    return loss, dx, big, small, _bias_table_bwd(g_tab, buckets, "bias_table_bwd"), g_final[0], upper


def _cast_pad(w, rows, name):
    L, r, c = w.shape

    def body(w_ref, o_ref):
        if rows != r:
            o_ref[...] = jnp.zeros_like(o_ref)
        o_ref[:r, :] = w_ref[...].astype(BF16)

    return pl.pallas_call(
        body, grid=(L,), name=name,
        in_specs=[pl.BlockSpec((None, r, c), lambda l: (l, 0, 0))],
        out_specs=pl.BlockSpec((None, rows, c), lambda l: (l, 0, 0)),
        out_shape=jax.ShapeDtypeStruct((L, rows, c), BF16),
        compiler_params=_cp("parallel"))(w)


def _down_rows(ref, k):
    return ref.at[:, k // 2, pl.ds((k % 2) * DOWN_ROWS, DOWN_ROWS), :]


def _all_gather_weights(s_in, s_out, s_gu, s_down):
    L = s_in.shape[0]
    shards = (s_in, s_out, s_gu, s_down)
    down_full = jnp.zeros((L, 4, FF_PAD, D_MODEL), BF16)
    n = len(shards)

    def body(i0, i1, i2, i3, _, o0, o1, o2, o3, send_sems, recv_sems, local_sems):
        srcs = (i0, i1, i2, i3)
        outs = (o0, o1, o2, o3)
        x, y, c, _me = _my_place()
        chips = [(1 - x, y), (x, 1 - y), (1 - x, 1 - y)]

        def place(t, dev):
            k = 4 * dev[0] + 2 * dev[1] + dev[2]
            return _down_rows(outs[t], k) if t == 3 else outs[t].at[:, k]

        def copy(slot, t, block, to, from_shard):
            return pltpu.make_async_remote_copy(
                src_ref=srcs[t] if from_shard else place(t, block), dst_ref=place(t, block),
                send_sem=send_sems.at[slot, t], recv_sem=recv_sems.at[slot, t], device_id=to, device_id_type=MESH)

        me, sibling = (x, y, c), (x, y, 1 - c)
        mine = [pltpu.make_async_copy(srcs[t], place(t, me), local_sems.at[t]) for t in range(n)]
        for cp in mine:
            cp.start()
        first = [copy(0, t, me, sibling, True) for t in range(n)]
        first += [copy(1 + j, t, me, (*chip, c), True) for j, chip in enumerate(chips) for t in range(n)]
        for cp in first:
            cp.start()
        passed = []
        for j, chip in enumerate(chips):
            for t in range(n):
                copy(1 + j, t, (*chip, c), me, False).wait_recv()
                cp = copy(4 + j, t, (*chip, c), sibling, False)
                cp.start()
                passed.append(cp)
        for t in range(n):
            copy(0, t, sibling, me, False).wait_recv()
        for j, chip in enumerate(chips):
            for t in range(n):
                copy(4 + j, t, (*chip, 1 - c), me, False).wait_recv()
        for cp in first + passed:
            cp.wait_send()
        for cp in mine:
            cp.wait()

    shapes = [jax.ShapeDtypeStruct((L, N_DEV) + s.shape[1:], BF16) for s in shards[:3]]
    shapes.append(jax.ShapeDtypeStruct(down_full.shape, BF16))
    return pl.pallas_call(
        body, name="all_gather_weights", out_shape=shapes,
        in_specs=[ANY] * 5, out_specs=[ANY] * 4, input_output_aliases={4: 3},
        scratch_shapes=[pltpu.SemaphoreType.DMA((7, n)), pltpu.SemaphoreType.DMA((7, n)), pltpu.SemaphoreType.DMA((n,))],
        )(*shards, down_full)


def _adamw(w, g, m, v):
    m = ADAM_B1 * m + (1.0 - ADAM_B1) * g
    v = ADAM_B2 * v + (1.0 - ADAM_B2) * jnp.square(g)
    m_hat = m / (1.0 - ADAM_B1 ** ADAM_STEP)
    v_hat = v / (1.0 - ADAM_B2 ** ADAM_STEP)
    delta = -ADAM_LR * (m_hat / (jnp.sqrt(v_hat) + ADAM_EPS) + ADAM_WD * w)
    return delta, m, v


def _adamw_sharded(parts, w, m, v, tr, name, items=()):
    L, r, c = w.shape
    cp = parts[0].shape[-1]
    nrow = r // tr

    def body(*refs):
        p_refs, (w_ref, m_ref, v_ref, g_ref, d_ref, nm_ref, nv_ref) = refs[:L], refs[L:]
        for k in range(L):
            @pl.when(pl.program_id(0) == k)
            def _(p_ref=p_refs[k]):
                g = p_ref[0, :, :c].astype(F32)
                for dev in range(1, N_DEV):
                    g = g + p_ref[dev, :, :c].astype(F32)
                delta, nm, nv = _adamw(w_ref[...], g, m_ref[...], v_ref[...])
                g_ref[...] = g
                d_ref[...] = delta
                nm_ref[...] = nm
                nv_ref[...] = nv

    def part_spec(k):
        return pl.BlockSpec((N_DEV, tr, cp),
                            lambda l, i: (0, jnp.where(l == k, i, jnp.where(l < k, 0, nrow - 1)), 0))

    blk = pl.BlockSpec((None, tr, c), lambda l, i: (l, i, 0))
    out = jax.ShapeDtypeStruct((L, r, c), F32)
    return _call(
        body, (*parts, w, m, v), grid=(L, nrow), name=name, items=items,
        in_specs=[part_spec(k) for k in range(L)] + [blk, blk, blk],
        out_specs=[blk] * 4, out_shape=[out] * 4, sem=("arbitrary", "arbitrary"))


def _adamw_small(parts, wmv, name):
    def body(p_ref, wmv_ref, g_ref, d_ref, nm_ref, nv_ref):
        g = p_ref[0]
        for k in range(1, N_DEV):
            g = g + p_ref[k]
        delta, nm, nv = _adamw(wmv_ref[0], g, wmv_ref[1], wmv_ref[2])
        g_ref[...] = g
        d_ref[...] = delta
        nm_ref[...] = nm
        nv_ref[...] = nv

    out = jax.ShapeDtypeStruct(wmv.shape[1:], F32)
    return pl.pallas_call(body, name=name, out_shape=[out] * 4, compiler_params=_cp())(parts, wmv)


LAYERED = ("sgu_w", "sgu_b", "pool_w", "pool_scale", "swa_sinks", "mix_out_gain", "norm_mix", "norm_ffn")
SHARED = ("rel_bias", "norm_final")


def _seg_rows(a):
    return -(-a.size // 128)


def _pack_rows(parts):
    return -(-sum(_seg_rows(p) for p in parts) // 8) * 8


def _upper_layers(per_layer):
    if isinstance(per_layer, dict):
        return [per_layer[k][1:] for k in LAYERED]
    return [jnp.stack([per_layer[l][k] for l in range(1, DEPTH)]) for k in LAYERED]


def _layer_zero(stacked):
    return [stacked[k][:1] for k in LAYERED] + [stacked[k] for k in SHARED]


def _pack(parts, rows):
    segs = [jnp.pad(p.reshape(-1), (0, _seg_rows(p) * 128 - p.size)).reshape(_seg_rows(p), 128) for p in parts]
    used = sum(s.shape[0] for s in segs)
    return jnp.concatenate(segs + [jnp.zeros((rows - used, 128), F32)], axis=0)


def _pack_groups(groups, rows):
    segs = []
    for parts in groups:
        segs += [jnp.pad(p.reshape(-1), (0, _seg_rows(p) * 128 - p.size)).reshape(_seg_rows(p), 128) for p in parts]
        segs.append(jnp.zeros((rows - sum(_seg_rows(p) for p in parts), 128), F32))
    return jnp.concatenate(segs, axis=0).reshape(len(groups), rows, 128)


def _unpack(buf, like):
    out, at = [], 0
    for a in like:
        out.append(buf[at:at + _seg_rows(a)].reshape(-1)[:a.size].reshape(a.shape))
        at += _seg_rows(a)
    return out


def kernel(x, w_in, w_out, sgu_w, sgu_b, pool_w, pool_scale, swa_sinks, rel_bias, mix_out_gain, norm_mix, norm_ffn, w_gate_up, w_down, norm_final, loss_target, m_w_in, m_w_out, m_sgu_w, m_sgu_b, m_pool_w, m_pool_scale, m_swa_sinks, m_rel_bias, m_mix_out_gain, m_norm_mix, m_norm_ffn, m_w_gate_up, m_w_down, m_norm_final, v_w_in, v_w_out, v_sgu_w, v_sgu_b, v_pool_w, v_pool_scale, v_swa_sinks, v_rel_bias, v_mix_out_gain, v_norm_mix, v_norm_ffn, v_w_gate_up, v_w_down, v_norm_final):
    bl, seq, _ = x.shape
    L = w_in.shape[0]
    gu_t, m_gu_t, v_gu_t = (jnp.swapaxes(a, 1, 2) for a in (w_gate_up, m_w_gate_up, v_w_gate_up))
    shards = (_cast_pad(w_in, D_MODEL, "shard_w_in"), _cast_pad(w_out, D_MODEL // N_DEV, "shard_w_out"),
              _cast_pad(gu_t, FF_PAD, "shard_w_gate_up"), _cast_pad(w_down, DOWN_ROWS, "shard_w_down"))
    first = _all_gather_weights(*[s[:1] for s in shards])
    w0 = _as_weights(*[f[0] for f in first])
    small_w = dict(sgu_w=sgu_w, sgu_b=sgu_b, pool_w=pool_w, pool_scale=pool_scale, swa_sinks=swa_sinks,
                   rel_bias=rel_bias, mix_out_gain=mix_out_gain, norm_mix=norm_mix, norm_ffn=norm_ffn,
                   norm_final=norm_final)
    small_m = dict(sgu_w=m_sgu_w, sgu_b=m_sgu_b, pool_w=m_pool_w, pool_scale=m_pool_scale, swa_sinks=m_swa_sinks,
                   rel_bias=m_rel_bias, mix_out_gain=m_mix_out_gain, norm_mix=m_norm_mix, norm_ffn=m_norm_ffn,
                   norm_final=m_norm_final)
    small_v = dict(sgu_w=v_sgu_w, sgu_b=v_sgu_b, pool_w=v_pool_w, pool_scale=v_pool_scale, swa_sinks=v_swa_sinks,
                   rel_bias=v_rel_bias, mix_out_gain=v_mix_out_gain, norm_mix=v_norm_mix, norm_ffn=v_norm_ffn,
                   norm_final=v_norm_final)
    loss, dx, big, small, g_rel_bias, g_final, upper = _local_step(
        x.reshape(bl * seq, D_MODEL), loss_target.reshape(bl * seq, D_MODEL), w0, shards, sgu_w, sgu_b, pool_w,
        pool_scale, swa_sinks, rel_bias, mix_out_gain, norm_mix, norm_ffn, norm_final, seq)
    p_in, p_out, p_gu, p_down = ([big[l][t] for l in range(L)] for t in range(4))
    outs_gu = [jnp.swapaxes(a, 1, 2) for a in
               _adamw_sharded(p_gu, gu_t, m_gu_t, v_gu_t, FF_SHARD // 4, "adamw_w_gate_up")]
    outs_down = _adamw_sharded(p_down, w_down, m_w_down, v_w_down, DOWN_ROWS // 2, "adamw_w_down")
    outs_in = _adamw_sharded(p_in, w_in, m_w_in, v_w_in, 256, "adamw_w_in")
    lo_like = _layer_zero(small_w)
    lo_rows = _pack_rows(lo_like + [loss[0]])
    lo_mine = _pack([small[0][k][None] for k in LAYERED] + [g_rel_bias, g_final, loss[0]], lo_rows)
    *outs_out, lower = _adamw_sharded(p_out, w_out, m_w_out, v_w_out, D_MODEL // N_DEV, "adamw_w_out",
                                      items=(_Item(lo_mine, lambda src, k: src, (N_DEV,) + lo_mine.shape),))
    lo_res = _adamw_small(lower, _pack_groups([lo_like, _layer_zero(small_m), _layer_zero(small_v)], lo_rows),
                          "adamw_small_layer0")
    hi_like = _upper_layers(small_w)
    hi_rows = _pack_rows(hi_like)
    hi_res = _adamw_small(upper, _pack_groups([hi_like, _upper_layers(small_m), _upper_layers(small_v)], hi_rows),
                          "adamw_small_upper")
    loss_total = lo_res[0][sum(_seg_rows(a) for a in lo_like), 0]
    small_outs = []
    for lo_buf, hi_buf in zip(lo_res, hi_res):
        lo = dict(zip(LAYERED + SHARED, _unpack(lo_buf, lo_like)))
        hi = dict(zip(LAYERED, _unpack(hi_buf, hi_like)))
        small_outs.append({k: jnp.concatenate([lo[k], hi[k]], axis=0) if k in hi else lo[k] for k in lo})
    big_outs = dict(w_in=outs_in, w_out=outs_out, w_gate_up=outs_gu, w_down=outs_down)
    order = ("w_in", "w_out", "sgu_w", "sgu_b", "pool_w", "pool_scale", "swa_sinks", "rel_bias", "mix_out_gain",
             "norm_mix", "norm_ffn", "w_gate_up", "w_down", "norm_final")
    result = [loss_total, dx.reshape(bl, seq, D_MODEL)]
    for which in range(4):
        for name in order:
            result.append(big_outs[name][which] if name in big_outs else small_outs[which][name])
    return tuple(result)
```

```python
import functools

import jax
import jax.numpy as jnp
from jax import lax
from jax.experimental import pallas as pl
from jax.experimental.pallas import tpu as pltpu

F32 = jnp.float32
BF16 = jnp.bfloat16

N_DEV = 8
DEPTH = 4
D_MODEL = 1024
GROUP_WIDTH = 256
HEAD_DIM = 64
GROUP_HEADS = 4
BLOCK = 128
N_BUCKETS = 32
MAX_DISTANCE = 128
POOL_WINDOWS = (2, 4, 8, 16)
D_FF = 2816
FF_SHARD = D_FF // 4
FF_PAD = 768
D_FF_PAD = 4 * FF_PAD
EPS = 1e-6
ATT_SCALE = HEAD_DIM ** -0.5
ADAM_LR = 0.001
ADAM_B1 = 0.9
ADAM_B2 = 0.999
ADAM_EPS = 1e-08
ADAM_WD = 0.01
ADAM_STEP = 10
VMEM_LIMIT = 56 * 1024 * 1024
MESH_AXES = ("x", "y", "c")


def _cp(*sem):
    return pltpu.CompilerParams(dimension_semantics=sem or None, vmem_limit_bytes=VMEM_LIMIT)


_NT = (((1,), (1,)), ((), ()))
_TN = (((0,), (0,)), ((), ()))


@jax.custom_vjp
def _bdot(a, b):
    return jnp.dot(a.astype(BF16), b.astype(BF16), preferred_element_type=F32)


def _bdot_fwd(a, b):
    return _bdot(a, b), (a.astype(BF16), b.astype(BF16))


def _bdot_bwd(res, ct):
    a, b = res
    c = ct.astype(BF16)
    return (lax.dot_general(c, b, _NT, preferred_element_type=F32),
            lax.dot_general(a, c, _TN, preferred_element_type=F32))


_bdot.defvjp(_bdot_fwd, _bdot_bwd)


@jax.custom_vjp
def _bdot_nt(a, b):
    return lax.dot_general(a.astype(BF16), b.astype(BF16), _NT, preferred_element_type=F32)


def _bdot_nt_fwd(a, b):
    return _bdot_nt(a, b), (a.astype(BF16), b.astype(BF16))


def _bdot_nt_bwd(res, ct):
    a, b = res
    c = ct.astype(BF16)
    return (jnp.dot(c, b, preferred_element_type=F32),
            lax.dot_general(c, a, _TN, preferred_element_type=F32))


_bdot_nt.defvjp(_bdot_nt_fwd, _bdot_nt_bwd)


def _rms(x, g):
    return x * lax.rsqrt(jnp.mean(x * x, axis=-1, keepdims=True) + EPS) * g


def _sigmoid(x):
    return 0.5 * jnp.tanh(0.5 * x) + 0.5


def _split_dot(x, u):
    hi = x.astype(BF16)
    lo = (x - hi.astype(F32)).astype(BF16)
    return jnp.dot(hi, u, preferred_element_type=F32) + jnp.dot(lo, u, preferred_element_type=F32)


def _head_mask(h, shape):
    col = lax.broadcasted_iota(jnp.int32, shape, 1)
    return (col >= h * HEAD_DIM) & (col < (h + 1) * HEAD_DIM)


ANY = pl.BlockSpec(memory_space=pl.ANY)
MESH = pl.DeviceIdType.MESH
DOWN_ROWS = D_FF // N_DEV


def _my_place():
    x, y, c = (lax.axis_index(a) for a in MESH_AXES)
    return x, y, c, 4 * x + 2 * y + c


def _peer(x, y, c, d):
    return (x ^ (d >> 2), y ^ ((d >> 1) & 1), c ^ (d & 1))


class _Item:
    def __init__(self, src, block, dst_shape, place=None, init=None, relayed=False):
        self.src, self.block, self.dst_shape, self.init, self.relayed = src, block, dst_shape, init, relayed
        self.place = place or (lambda dst, k: dst.at[k])


def _call(body, args, *, grid, in_specs, out_specs, out_shape, sem, name, scratch_shapes=(), items=()):
    if not items:
        outs = pl.pallas_call(body, grid=grid, in_specs=in_specs, out_specs=out_specs, out_shape=out_shape, name=name,
                              scratch_shapes=list(scratch_shapes), compiler_params=_cp(*sem))(*args)
        return list(outs) if isinstance(outs, (list, tuple)) else [outs]
    n_in, n_out, n_scr, n = len(in_specs), len(out_specs), len(scratch_shapes), len(items)
    inits = [i for i, it in enumerate(items) if it.init is not None]

    def wrapped(*refs):
        core_in, srcs = refs[:n_in], refs[n_in:n_in + n]
        off = n_in + n + len(inits)
        core_out, dsts = refs[off:off + n_out], refs[off + n_out:off + n_out + n]
        scratch = refs[off + n_out + n:]
        send_sems, recv_sems, local_sems = scratch[n_scr:]
        step = pl.program_id(0)
        for a in range(1, len(grid)):
            step = step * grid[a] + pl.program_id(a)
        steps = functools.reduce(lambda p, g: p * g, grid)
        relay_step = max(0, steps - 1 - max(1, steps // 8))
        x, y, c, me = _my_place()
        direct = [i for i in range(n) if not items[i].relayed]
        relayed = [i for i in range(n) if items[i].relayed]
        chips = [(1 - x, y), (x, 1 - y), (1 - x, 1 - y)]
        sibling = (x, y, 1 - c)

        def local(i):
            return pltpu.make_async_copy(items[i].block(srcs[i], me), items[i].place(dsts[i], me), local_sems.at[i])

        def remote(d, i, sending):
            px, py, pc = _peer(x, y, c, d)
            pk = 4 * px + 2 * py + pc
            return pltpu.make_async_remote_copy(
                src_ref=items[i].block(srcs[i], pk), dst_ref=items[i].place(dsts[i], me if sending else pk),
                send_sem=send_sems.at[d - 1, i], recv_sem=recv_sems.at[d - 1, i],
                device_id=(px, py, pc), device_id_type=MESH)

        def hop(slot, i, owner, to, from_src):
            k = 4 * owner[0] + 2 * owner[1] + owner[2]
            src = items[i].block(srcs[i], me) if from_src else items[i].place(dsts[i], k)
            return pltpu.make_async_remote_copy(
                src_ref=src, dst_ref=items[i].place(dsts[i], k), send_sem=send_sems.at[slot, i],
                recv_sem=recv_sems.at[slot, i], device_id=to, device_id_type=MESH)

        @pl.when(step == 0)
        def _():
            for i in range(n):
                local(i).start()
            for d in range(1, N_DEV):
                for i in direct:
                    remote(d, i, True).start()
            for i in relayed:
                hop(0, i, (x, y, c), sibling, True).start()
                for j, chip in enumerate(chips):
                    hop(1 + j, i, (x, y, c), (*chip, c), True).start()

        body(*core_in, *core_out, *scratch[:n_scr])

        if relayed:
            @pl.when(step == relay_step)
            def _():
                for i in relayed:
                    for j, chip in enumerate(chips):
                        hop(1 + j, i, (*chip, c), (x, y, c), False).wait_recv()
                        hop(4 + j, i, (*chip, c), sibling, False).start()

        @pl.when(step == steps - 1)
        def _():
            for d in range(1, N_DEV):
                for i in direct:
                    remote(d, i, False).wait_recv()
            for i in relayed:
                hop(0, i, sibling, (x, y, c), False).wait_recv()
                for j, chip in enumerate(chips):
                    hop(4 + j, i, (*chip, 1 - c), (x, y, c), False).wait_recv()
            for d in range(1, N_DEV):
                for i in direct:
                    remote(d, i, True).wait_send()
            for i in relayed:
                for slot in range(N_DEV - 1):
                    hop(slot, i, (x, y, c), sibling, True).wait_send()
            for i in range(n):
                local(i).wait()

    outs = pl.pallas_call(
        wrapped, grid=grid, name=name,
        in_specs=list(in_specs) + [ANY] * (n + len(inits)), out_specs=list(out_specs) + [ANY] * n,
        out_shape=list(out_shape) + [jax.ShapeDtypeStruct(it.dst_shape, it.src.dtype) for it in items],
        input_output_aliases={n_in + n + j: n_out + i for j, i in enumerate(inits)},
        scratch_shapes=list(scratch_shapes) + [pltpu.SemaphoreType.DMA((N_DEV - 1, n)),
                                               pltpu.SemaphoreType.DMA((N_DEV - 1, n)), pltpu.SemaphoreType.DMA((n,))],
        compiler_params=_cp(*(["arbitrary"] * len(grid))),
    )(*args, *[it.src for it in items], *[items[i].init for i in inits])
    return list(outs)


def _norm_mm(x, g, w, l, jb, out_dtype, name, tm=1024, items=()):
    T, K = x.shape
    _, nb, _, tn = w.shape
    tm = min(tm, T)

    def body(x_ref, g_ref, w_ref, o_ref, h_ref):
        @pl.when(pl.program_id(1) == 0)
        def _():
            h_ref[...] = _rms(x_ref[...], g_ref[...]).astype(BF16)
        h = h_ref[...]
        for jj in range(jb):
            o_ref[:, jj * tn:(jj + 1) * tn] = jnp.dot(h, w_ref[jj], preferred_element_type=F32).astype(o_ref.dtype)

    return _call(
        body, (x, g, w), grid=(T // tm, nb // jb), name=name, items=items,
        in_specs=[pl.BlockSpec((tm, K), lambda i, j: (i, 0)), pl.BlockSpec((1, K), lambda i, j: (0, 0)),
                  pl.BlockSpec((None, jb, K, tn), lambda i, j: (l, j, 0, 0))],
        out_specs=[pl.BlockSpec((tm, jb * tn), lambda i, j: (i, j)), pl.BlockSpec((tm, K), lambda i, j: (i, 0))],
        out_shape=[jax.ShapeDtypeStruct((T, nb * tn), out_dtype), jax.ShapeDtypeStruct((T, K), BF16)],
        sem=("parallel", "arbitrary"))


def _mm_res(res, a, w, l, name, tn, tm=1024, items=()):
    T, K = a.shape
    N = w.shape[2]
    tm = min(tm, T)

    def body(r_ref, a_ref, w_ref, o_ref):
        o_ref[...] = r_ref[...] + jnp.dot(a_ref[...], w_ref[...], preferred_element_type=F32)

    return _call(
        body, (res, a, w), grid=(T // tm, N // tn), name=name, items=items,
        in_specs=[pl.BlockSpec((tm, tn), lambda i, j: (i, j)), pl.BlockSpec((tm, K), lambda i, j: (i, 0)),
                  pl.BlockSpec((None, K, tn), lambda i, j: (l, 0, j))],
        out_specs=[pl.BlockSpec((tm, tn), lambda i, j: (i, j))],
        out_shape=[jax.ShapeDtypeStruct((T, N), F32)],
        sem=("parallel", "parallel"))


def _mm_nt(a, w, l, out_dtype, name, tm=1024, tn=1024):
    T, K = a.shape
    N = w.shape[1]
    tm = min(tm, T)

    def body(a_ref, w_ref, o_ref):
        o_ref[...] = lax.dot_general(a_ref[...].astype(BF16), w_ref[...], _NT,
                                     preferred_element_type=F32).astype(o_ref.dtype)

    return pl.pallas_call(
        body, grid=(T // tm, N // tn), name=name,
        in_specs=[pl.BlockSpec((tm, K), lambda i, j: (i, 0)), pl.BlockSpec((None, tn, K), lambda i, j: (l, j, 0))],
        out_specs=pl.BlockSpec((tm, tn), lambda i, j: (i, j)),
        out_shape=jax.ShapeDtypeStruct((T, N), out_dtype),
        compiler_params=_cp("parallel", "parallel"))(a, w)


def _mm_norm_bwd(a, w, l, jb, x, g, dres, name, tm=1024, items=(), block_of=lambda j: j, transposed=False):
    T = a.shape[0]
    _, nb, K, tn = w.shape
    if transposed:
        tn, K = K, tn
    tm = min(tm, T)
    nj = nb // jb
    sub = min(256, tm)
    dims = (((1,), (0,)), ((), ())) if transposed else _NT

    def body(a_ref, w_ref, x_ref, g_ref, r_ref, o_ref, dg_ref):
        part = lax.dot_general(a_ref[:, :tn], w_ref[0], dims, preferred_element_type=F32)
        for jj in range(1, jb):
            part += lax.dot_general(a_ref[:, jj * tn:(jj + 1) * tn], w_ref[jj], dims, preferred_element_type=F32)

        @pl.when(pl.program_id(1) == 0)
        def _():
            o_ref[...] = part

        @pl.when(pl.program_id(1) > 0)
        def _():
            o_ref[...] += part

        @pl.when(pl.program_id(1) == nj - 1)
        def _():
            dg = jnp.zeros((1, K), F32)
            for r in range(tm // sub):
                rows = pl.ds(r * sub, sub)
                _, vjp = jax.vjp(_rms, x_ref[rows, :], g_ref[...])
                dx, dg_r = vjp(o_ref[rows, :])
                o_ref[rows, :] = r_ref[rows, :] + dx
                dg = dg + dg_r
            _acc_out(pl.program_id(0) == 0, (dg_ref,), (dg,))

    row = pl.BlockSpec((tm, K), lambda i, j: (i, 0))
    vec = pl.BlockSpec((1, K), lambda i, j: (0, 0))
    return _call(
        body, (a, w, x, g, dres), grid=(T // tm, nj), name=name, items=items,
        in_specs=[pl.BlockSpec((tm, jb * tn), lambda i, j: (i, j)),
                  pl.BlockSpec((None, jb) + w.shape[2:], lambda i, j: (l, block_of(j), 0, 0)), row, vec, row],
        out_specs=[row, vec],
        out_shape=[jax.ShapeDtypeStruct((T, K), F32), jax.ShapeDtypeStruct((1, K), F32)],
        sem=("arbitrary", "arbitrary"))


def _mm_tn_cols(lhs, rhs, tn, jb, name, tm=1024, items=(), block_of=lambda j: j):
    T, K = lhs.shape
    nb = rhs.shape[1] // tn
    tm = min(tm, T)
    nt = T // tm

    def body(l_ref, r_ref, o_ref, acc):
        part = lax.dot_general(l_ref[...], r_ref[...], _TN, preferred_element_type=F32)

        @pl.when(pl.program_id(1) == 0)
        def _():
            acc[...] = part

        @pl.when(pl.program_id(1) > 0)
        def _():
            acc[...] += part

        @pl.when(pl.program_id(1) == nt - 1)
        def _():
            for jj in range(jb):
                o_ref[jj] = acc[:, jj * tn:(jj + 1) * tn].astype(BF16)

    return _call(
        body, (lhs, rhs), grid=(nb // jb, nt), name=name, items=items,
        in_specs=[pl.BlockSpec((tm, K), lambda j, t: (t, 0)), pl.BlockSpec((tm, jb * tn), lambda j, t: (t, j))],
        out_specs=[pl.BlockSpec((jb, K, tn), lambda j, t: (block_of(j), 0, 0))],
        out_shape=[jax.ShapeDtypeStruct((nb, K, tn), BF16)],
        scratch_shapes=[pltpu.VMEM((K, jb * tn), F32)],
        sem=("parallel", "arbitrary"))


def _mm_tn_rows(lhs, rhs, tk, name, tm=1024, items=(), block_of=lambda j: j):
    T, Kl = lhs.shape
    N = rhs.shape[1]
    tm = min(tm, T)
    nt = T // tm

    def body(l_ref, r_ref, o_ref, acc):
        part = lax.dot_general(l_ref[...], r_ref[...].astype(BF16), _TN, preferred_element_type=F32)

        @pl.when(pl.program_id(1) == 0)
        def _():
            acc[...] = part

        @pl.when(pl.program_id(1) > 0)
        def _():
            acc[...] += part

        @pl.when(pl.program_id(1) == nt - 1)
        def _():
            o_ref[...] = acc[...].astype(BF16)

    return _call(
        body, (lhs, rhs), grid=(Kl // tk, nt), name=name, items=items,
        in_specs=[pl.BlockSpec((tm, tk), lambda l, t: (t, l)), pl.BlockSpec((tm, N), lambda l, t: (t, 0))],
        out_specs=[pl.BlockSpec((tk, N), lambda l, t: (block_of(l), 0))],
        out_shape=[jax.ShapeDtypeStruct((Kl, N), BF16)],
        scratch_shapes=[pltpu.VMEM((tk, N), F32)],
        sem=("parallel", "arbitrary"))


N_FF_CHUNK = D_FF_PAD // FF_PAD


def _ffn_chunk_block(j):
    return (j % 2) * N_FF_CHUNK + j // 2


def _ffn_up_act(x, g, w, name, tm=1024, items=()):
    T, K = x.shape
    tm = min(tm, T)

    def body(x_ref, g_ref, wg_ref, wu_ref, gu_ref, act_ref, h_ref):
        @pl.when(pl.program_id(1) == 0)
        def _():
            h_ref[...] = _rms(x_ref[...], g_ref[...]).astype(BF16)
        h = h_ref[...]
        gate = lax.dot_general(h, wg_ref[...], _NT, preferred_element_type=F32)
        up = lax.dot_general(h, wu_ref[...], _NT, preferred_element_type=F32)
        gu_ref[:, :FF_PAD] = gate.astype(BF16)
        gu_ref[:, FF_PAD:] = up.astype(BF16)
        act_ref[...] = (gate * _sigmoid(gate) * up).astype(BF16)

    return _call(
        body, (x, g, w, w), grid=(T // tm, N_FF_CHUNK), name=name, items=items,
        in_specs=[pl.BlockSpec((tm, K), lambda i, j: (i, 0)), pl.BlockSpec((1, K), lambda i, j: (0, 0)),
                  pl.BlockSpec((None, None, FF_PAD, K), lambda i, j: (0, j, 0, 0)),
                  pl.BlockSpec((None, None, FF_PAD, K), lambda i, j: (0, j + N_FF_CHUNK, 0, 0))],
        out_specs=[pl.BlockSpec((tm, 2 * FF_PAD), lambda i, j: (i, j)), pl.BlockSpec((tm, FF_PAD), lambda i, j: (i, j)),
                   pl.BlockSpec((tm, K), lambda i, j: (i, 0))],
        out_shape=[jax.ShapeDtypeStruct((T, 2 * D_FF_PAD), BF16), jax.ShapeDtypeStruct((T, D_FF_PAD), BF16),
                   jax.ShapeDtypeStruct((T, K), BF16)],
        sem=("parallel", "arbitrary"))


def _ffn_down_dx(dxo, w, gu, name, tm=1024, items=()):
    T, K = dxo.shape
    tm = min(tm, T)

    sub = min(512, tm)

    def body(d_ref, w_ref, gu_ref, o_ref):
        w = w_ref[...]
        for r in range(tm // sub):
            rows = slice(r * sub, (r + 1) * sub)
            d = lax.dot_general(d_ref[rows, :].astype(BF16), w, _NT, preferred_element_type=F32)
            gate = gu_ref[rows, :FF_PAD].astype(F32)
            up = gu_ref[rows, FF_PAD:].astype(F32)
            sig = _sigmoid(gate)
            silu = gate * sig
            o_ref[rows, :FF_PAD] = (d * up * (sig + silu * (1.0 - sig))).astype(BF16)
            o_ref[rows, FF_PAD:] = (d * silu).astype(BF16)

    return _call(
        body, (dxo, w, gu), grid=(T // tm, N_FF_CHUNK), name=name, items=items,
        in_specs=[pl.BlockSpec((tm, K), lambda i, j: (i, 0)), pl.BlockSpec((None, FF_PAD, K), lambda i, j: (0, j, 0)),
                  pl.BlockSpec((tm, 2 * FF_PAD), lambda i, j: (i, j))],
        out_specs=[pl.BlockSpec((tm, 2 * FF_PAD), lambda i, j: (i, j))],
        out_shape=[jax.ShapeDtypeStruct((T, 2 * D_FF_PAD), BF16)],
        sem=("parallel", "parallel"))


def _loss_head(x, g, target, name, tm=512):
    T, K = x.shape

    def loss_fn(xv, gv, tv):
        err = _rms(xv, gv) - tv
        return 0.5 * jnp.sum(jnp.mean(err * err, axis=-1, keepdims=True), axis=0, keepdims=True)

    def body(x_ref, g_ref, t_ref, l_ref, dx_ref, dg_ref):
        val, vjp = jax.vjp(lambda xv, gv: loss_fn(xv, gv, t_ref[...]), x_ref[...], g_ref[...])
        dx, dg = vjp(jnp.ones((1, 1), F32))
        dx_ref[...] = dx
        lval = jnp.broadcast_to(val, (1, 128))

        @pl.when(pl.program_id(0) == 0)
        def _():
            dg_ref[...] = dg
            l_ref[...] = lval

        @pl.when(pl.program_id(0) > 0)
        def _():
            dg_ref[...] += dg
            l_ref[...] += lval

    row = pl.BlockSpec((tm, K), lambda i: (i, 0))
    vec = pl.BlockSpec((1, K), lambda i: (0, 0))
    return pl.pallas_call(
        body, grid=(T // tm,), name=name,
        in_specs=[row, vec, row], out_specs=[pl.BlockSpec((1, 128), lambda i: (0, 0)), row, vec],
        out_shape=[jax.ShapeDtypeStruct((1, 128), F32), jax.ShapeDtypeStruct((T, K), F32),
                   jax.ShapeDtypeStruct((1, K), F32)],
        compiler_params=_cp("arbitrary"))(x, g, target)


SGU_ROWS = 4 * BLOCK


def _sgu_fn(u, v, wm, bt, gain):
    ug = jax.nn.gelu(u)
    vg = jax.nn.gelu(v)
    row = lax.broadcasted_iota(jnp.int32, (BLOCK, BLOCK), 0)
    col = lax.broadcasted_iota(jnp.int32, (BLOCK, BLOCK), 1)
    tri = row >= col
    normed = []
    for h in range(GROUP_HEADS):
        vh = vg[:, h * HEAD_DIM:(h + 1) * HEAD_DIM]
        xc = vh - jnp.mean(vh, axis=-1, keepdims=True)
        normed.append(xc * lax.rsqrt(jnp.mean(xc * xc, axis=-1, keepdims=True) + EPS))
    vn = jnp.concatenate(normed, axis=1)
    wcat = jnp.concatenate([jnp.where(tri, wm[h], 0.0) for h in range(GROUP_HEADS)], axis=1)
    bias = jnp.concatenate([jnp.broadcast_to(bt[:, h:h + 1], (BLOCK, HEAD_DIM)) for h in range(GROUP_HEADS)], axis=1)
    mixes = []
    for c in range(u.shape[0] // BLOCK):
        chunk = vn[c * BLOCK:(c + 1) * BLOCK]
        stacked = jnp.concatenate([jnp.where(_head_mask(h, chunk.shape), chunk, 0.0) for h in range(GROUP_HEADS)], axis=0)
        mixes.append(_bdot(wcat, stacked) + bias)
    return _rms(ug * jnp.concatenate(mixes, axis=0), gain)


def _sgu_fwd(proj, wm, bt, gain, name, items=()):
    T = proj.shape[0]
    rows = min(SGU_ROWS, T)

    def body(u_ref, v_ref, w_ref, b_ref, g_ref, o_ref):
        o_ref[...] = _sgu_fn(u_ref[...], v_ref[...], w_ref[...], b_ref[...], g_ref[...]).astype(BF16)

    full = lambda shape: pl.BlockSpec(shape, lambda i: (0,) * len(shape))
    return _call(
        body, (proj, proj, wm, bt, gain), grid=(T // rows,), name=name, items=items,
        in_specs=[pl.BlockSpec((rows, GROUP_WIDTH), lambda i: (i, 0)), pl.BlockSpec((rows, GROUP_WIDTH), lambda i: (i, 1)),
                  full(wm.shape), full(bt.shape), full(gain.shape)],
        out_specs=[pl.BlockSpec((rows, GROUP_WIDTH), lambda i: (i, 0))],
        out_shape=[jax.ShapeDtypeStruct((T, GROUP_WIDTH), BF16)],
        sem=("parallel",))


def _acc_out(first, refs, vals):
    @pl.when(first)
    def _():
        for r, v in zip(refs, vals):
            r[...] = v

    @pl.when(jnp.logical_not(first))
    def _():
        for r, v in zip(refs, vals):
            r[...] += v


def _sgu_bwd(proj, wm, bt, gain, dy, name, items=()):
    T = proj.shape[0]

    def body(u_ref, v_ref, w_ref, b_ref, g_ref, dy_ref, duv_ref, dw_ref, db_ref, dg_ref):
        _, vjp = jax.vjp(_sgu_fn, u_ref[...], v_ref[...], w_ref[...], b_ref[...], g_ref[...])
        du, dv, dw, db, dg = vjp(dy_ref[...])
        duv_ref[:, :GROUP_WIDTH] = du.astype(BF16)
        duv_ref[:, GROUP_WIDTH:] = dv.astype(BF16)
        _acc_out(pl.program_id(0) == 0, (dw_ref, db_ref, dg_ref), (dw, db, dg))

    full = lambda shape: pl.BlockSpec(shape, lambda i: (0,) * len(shape))
    rows = min(SGU_ROWS, T)
    return _call(
        body, (proj, proj, wm, bt, gain, dy), grid=(T // rows,), name=name, items=items,
        in_specs=[pl.BlockSpec((rows, GROUP_WIDTH), lambda i: (i, 0)), pl.BlockSpec((rows, GROUP_WIDTH), lambda i: (i, 1)),
                  full(wm.shape), full(bt.shape), full(gain.shape),
                  pl.BlockSpec((rows, GROUP_WIDTH), lambda i: (i, 0))],
        out_specs=[pl.BlockSpec((rows, 2 * GROUP_WIDTH), lambda i: (i, 0)), full(wm.shape), full(bt.shape), full(gain.shape)],
        out_shape=[jax.ShapeDtypeStruct((T, 2 * GROUP_WIDTH), BF16), jax.ShapeDtypeStruct(wm.shape, F32),
                   jax.ShapeDtypeStruct(bt.shape, F32), jax.ShapeDtypeStruct(gain.shape, F32)],
        sem=("arbitrary",))


def _pool_consts(seq):
    t = lax.broadcasted_iota(jnp.int32, (seq, GROUP_WIDTH), 0)
    grp = lax.broadcasted_iota(jnp.int32, (seq, GROUP_WIDTH), 1) // (GROUP_WIDTH // len(POOL_WINDOWS))
    win = jnp.where(grp == 0, POOL_WINDOWS[0], jnp.where(grp == 1, POOL_WINDOWS[1],
                    jnp.where(grp == 2, POOL_WINDOWS[2], POOL_WINDOWS[3])))
    count = jnp.minimum(t + 1, win).astype(F32)
    return t, grp, count


def _by_group(grp, vals):
    return jnp.where(grp == 0, vals[0], jnp.where(grp == 1, vals[1], jnp.where(grp == 2, vals[2], vals[3])))


def _window_sums(x, t, seq, back):
    def shift(a, k):
        if back:
            return jnp.where(t >= k, pltpu.roll(a, k, 0), 0.0)
        return jnp.where(t < seq - k, pltpu.roll(a, seq - k, 0), 0.0)
    sums = []
    a, k = x, 1
    for _ in POOL_WINDOWS:
        a = a + shift(a, k)
        sums.append(a)
        k *= 2
    return sums


def _pool_tail(y, wbd, scale, gain):
    return _rms(_bdot(y, wbd) * scale, gain)


def _pool_fwd(proj, wbd, scale, gain, seq, name):
    T = proj.shape[0]

    def body(p_ref, w_ref, s_ref, g_ref, o_ref):
        p = p_ref[...]
        t, grp, count = _pool_consts(seq)
        y = _by_group(grp, _window_sums(p, t, seq, True)) / count - p
        o_ref[...] = _pool_tail(y, w_ref[...], s_ref[...], g_ref[...]).astype(BF16)

    full = lambda shape: pl.BlockSpec(shape, lambda b: (0,) * len(shape))
    return pl.pallas_call(
        body, grid=(T // seq,), name=name,
        in_specs=[pl.BlockSpec((seq, GROUP_WIDTH), lambda b: (b, 2)), full(wbd.shape), full(scale.shape), full(gain.shape)],
        out_specs=pl.BlockSpec((seq, GROUP_WIDTH), lambda b: (b, 0)),
        out_shape=jax.ShapeDtypeStruct((T, GROUP_WIDTH), BF16),
        compiler_params=_cp("parallel"))(proj, wbd, scale, gain)


def _pool_bwd(proj, wbd, scale, gain, dy, seq, name):
    T = proj.shape[0]

    def body(p_ref, w_ref, s_ref, g_ref, dy_ref, dp_ref, dw_ref, ds_ref, dg_ref):
        p = p_ref[...]
        t, grp, count = _pool_consts(seq)
        y = _by_group(grp, _window_sums(p, t, seq, True)) / count - p
        _, vjp = jax.vjp(_pool_tail, y, w_ref[...], s_ref[...], g_ref[...])
        d_y, dw, ds, dg = vjp(dy_ref[...])
        dp = _by_group(grp, _window_sums(d_y / count, t, seq, False)) - d_y
        dp_ref[...] = dp.astype(BF16)
        _acc_out(pl.program_id(0) == 0, (dw_ref, ds_ref, dg_ref), (dw, ds, dg))

    full = lambda shape: pl.BlockSpec(shape, lambda b: (0,) * len(shape))
    return pl.pallas_call(
        body, grid=(T // seq,), name=name,
        in_specs=[pl.BlockSpec((seq, GROUP_WIDTH), lambda b: (b, 2)), full(wbd.shape), full(scale.shape), full(gain.shape),
                  pl.BlockSpec((seq, GROUP_WIDTH), lambda b: (b, 1))],
        out_specs=[pl.BlockSpec((seq, GROUP_WIDTH), lambda b: (b, 0)), full(wbd.shape), full(scale.shape), full(gain.shape)],
        out_shape=[jax.ShapeDtypeStruct((T, GROUP_WIDTH), BF16), jax.ShapeDtypeStruct(wbd.shape, F32),
                   jax.ShapeDtypeStruct(scale.shape, F32), jax.ShapeDtypeStruct(gain.shape, F32)],
        compiler_params=_cp("arbitrary"))(proj, wbd, scale, gain, dy)


def _swa_fn(q, kv_prev, kv_cur, sinks, tab, gain, first):
    half = GROUP_WIDTH // 2
    k2 = jnp.concatenate([kv_prev[:, :half], kv_cur[:, :half]], axis=0)
    v2 = jnp.concatenate([kv_prev[:, half:], kv_cur[:, half:]], axis=0)
    per_query_head = lambda a: jnp.concatenate(
        [a[:, (h // 2) * HEAD_DIM:(h // 2 + 1) * HEAD_DIM] for h in range(GROUP_HEADS)], axis=1)
    qs = jnp.concatenate([jnp.where(_head_mask(h, q.shape), q, 0.0) for h in range(GROUP_HEADS)], axis=0)
    qi = lax.broadcasted_iota(jnp.int32, (HEAD_ROWS, 2 * BLOCK), 0) & (BLOCK - 1)
    kj = lax.broadcasted_iota(jnp.int32, (HEAD_ROWS, 2 * BLOCK), 1)
    dist = qi + BLOCK - kj
    mask = (dist >= 0) & (dist < BLOCK) & ((kj >= BLOCK) | jnp.logical_not(first))
    logits = _bdot_nt(qs, per_query_head(k2)) * ATT_SCALE + tab.reshape(HEAD_ROWS, 2 * BLOCK)
    logits = jnp.where(mask, logits, -1e30)
    sink = jnp.concatenate([jnp.broadcast_to(sinks[:, h:h + 1], (BLOCK, 1)) for h in range(GROUP_HEADS)], axis=0)
    m = lax.stop_gradient(jnp.maximum(jnp.max(logits, axis=1, keepdims=True), sink))
    p = jnp.exp(logits - m)
    probs = p / (jnp.sum(p, axis=1, keepdims=True) + jnp.exp(sink - m))
    out = _bdot(probs, per_query_head(v2))
    y = jnp.zeros_like(q)
    for h in range(GROUP_HEADS):
        y = y + jnp.where(_head_mask(h, q.shape), out[h * BLOCK:(h + 1) * BLOCK], 0.0)
    return _rms(y, gain)


def _swa_specs(nblk):
    q = pl.BlockSpec((BLOCK, GROUP_WIDTH), lambda b, i: (b * nblk + i, 3))
    cur = pl.BlockSpec((BLOCK, GROUP_WIDTH), lambda b, i: (b * nblk + i, 4))
    prev = pl.BlockSpec((BLOCK, GROUP_WIDTH), lambda b, i: (b * nblk + jnp.maximum(i - 1, 0), 4))
    return q, prev, cur


def _swa_fwd(proj, sinks, tab, gain, seq, name, items=()):
    T = proj.shape[0]
    nblk = seq // BLOCK

    def body(q_ref, kp_ref, kc_ref, s_ref, t_ref, g_ref, o_ref):
        o_ref[...] = _swa_fn(q_ref[...], kp_ref[...], kc_ref[...], s_ref[...], t_ref[...], g_ref[...],
                             pl.program_id(1) == 0).astype(BF16)

    full = lambda shape: pl.BlockSpec(shape, lambda b, i: (0,) * len(shape))
    return _call(
        body, (proj, proj, proj, sinks, tab, gain), grid=(T // seq, nblk), name=name, items=items,
        in_specs=[*_swa_specs(nblk), full(sinks.shape), full(tab.shape), full(gain.shape)],
        out_specs=[pl.BlockSpec((BLOCK, GROUP_WIDTH), lambda b, i: (b * nblk + i, 0))],
        out_shape=[jax.ShapeDtypeStruct((T, GROUP_WIDTH), BF16)],
        sem=("parallel", "parallel"))


def _swa_bwd(proj, sinks, tab, gain, dy, seq, name, items=()):
    T = proj.shape[0]
    nblk = seq // BLOCK

    def body(q_ref, kp_ref, kc_ref, s_ref, t_ref, g_ref, dy_ref, dq_ref, dkp_ref, dkc_ref, ds_ref, dt_ref, dg_ref):
        first = pl.program_id(1) == 0
        fn = functools.partial(_swa_fn, first=first)
        _, vjp = jax.vjp(fn, q_ref[...], kp_ref[...], kc_ref[...], s_ref[...], t_ref[...], g_ref[...])
        dq, dkp, dkc, ds, dt, dg = vjp(dy_ref[...])
        dq_ref[...] = dq.astype(BF16)
        dkp_ref[...] = dkp
        dkc_ref[...] = dkc
        _acc_out((pl.program_id(0) == 0) & first, (ds_ref, dt_ref, dg_ref), (ds, dt, dg))

    full = lambda shape: pl.BlockSpec(shape, lambda b, i: (0,) * len(shape))
    blk = lambda c: pl.BlockSpec((BLOCK, GROUP_WIDTH), lambda b, i: (b * nblk + i, c))
    return _call(
        body, (proj, proj, proj, sinks, tab, gain, dy), grid=(T // seq, nblk), name=name, items=items,
        in_specs=[*_swa_specs(nblk), full(sinks.shape), full(tab.shape), full(gain.shape), blk(2)],
        out_specs=[blk(0), blk(0), blk(0), full(sinks.shape), full(tab.shape), full(gain.shape)],
        out_shape=[jax.ShapeDtypeStruct((T, GROUP_WIDTH), BF16), jax.ShapeDtypeStruct((T, GROUP_WIDTH), F32),
                   jax.ShapeDtypeStruct((T, GROUP_WIDTH), F32), jax.ShapeDtypeStruct(sinks.shape, F32),
                   jax.ShapeDtypeStruct(tab.shape, F32), jax.ShapeDtypeStruct(gain.shape, F32)],
        sem=("arbitrary", "arbitrary"))


def _shift_add(cur, prev, seq, name):
    T = cur.shape[0]
    nblk = seq // BLOCK

    def body(c_ref, p_ref, o_ref):
        last = pl.program_id(1) == nblk - 1
        o_ref[...] = (c_ref[...] + jnp.where(last, 0.0, p_ref[...])).astype(BF16)

    return pl.pallas_call(
        body, grid=(T // seq, nblk), name=name,
        in_specs=[pl.BlockSpec((BLOCK, GROUP_WIDTH), lambda b, i: (b * nblk + i, 0)),
                  pl.BlockSpec((BLOCK, GROUP_WIDTH), lambda b, i: (b * nblk + jnp.minimum(i + 1, nblk - 1), 0))],
        out_specs=pl.BlockSpec((BLOCK, GROUP_WIDTH), lambda b, i: (b * nblk + i, 0)),
        out_shape=jax.ShapeDtypeStruct((T, GROUP_WIDTH), BF16),
        compiler_params=_cp("parallel", "parallel"))(cur, prev)


def _t5_bucket(dist):
    max_exact = N_BUCKETS // 2
    df = jnp.maximum(dist, 1).astype(F32)
    large = max_exact + (jnp.log(df / max_exact) / jnp.log(jnp.float32(MAX_DISTANCE / max_exact))
                         * (N_BUCKETS - max_exact)).astype(jnp.int32)
    return jnp.where(dist < max_exact, dist, jnp.minimum(large, N_BUCKETS - 1))


def _bucket_map():
    dist = (jnp.arange(BLOCK)[:, None] + BLOCK) - jnp.arange(2 * BLOCK)[None, :]
    return _t5_bucket(jnp.clip(dist, 0, BLOCK - 1)).astype(jnp.int32)


def _bias_table(rel_bias, buckets, name, items=()):
    def body(rb_ref, bk_ref, o_ref):
        bk = bk_ref[...]
        rb = rb_ref[...]
        for h in range(GROUP_HEADS):
            acc = jnp.zeros((BLOCK, 2 * BLOCK), F32)
            for b in range(N_BUCKETS):
                acc = jnp.where(bk == b, rb[b:b + 1, h:h + 1], acc)
            o_ref[h] = acc

    full = lambda shape: pl.BlockSpec(shape, lambda i: (0,) * len(shape))
    shape = (GROUP_HEADS, BLOCK, 2 * BLOCK)
    return _call(body, (rel_bias, buckets), grid=(1,), name=name, items=items,
                 in_specs=[full(rel_bias.shape), full(buckets.shape)], out_specs=[full(shape)],
                 out_shape=[jax.ShapeDtypeStruct(shape, F32)], sem=("arbitrary",))


def _bias_table_bwd(dtab, buckets, name):
    def body(dt_ref, bk_ref, o_ref):
        bk = bk_ref[...]
        row = lax.broadcasted_iota(jnp.int32, (N_BUCKETS, GROUP_HEADS), 0)
        col = lax.broadcasted_iota(jnp.int32, (N_BUCKETS, GROUP_HEADS), 1)
        acc = jnp.zeros((N_BUCKETS, GROUP_HEADS), F32)
        for h in range(GROUP_HEADS):
            dt = dt_ref[h]
            for b in range(N_BUCKETS):
                s = jnp.sum(jnp.where(bk == b, dt, 0.0), keepdims=True)
                acc = acc + jnp.where((row == b) & (col == h), s, 0.0)
        o_ref[...] = acc

    return pl.pallas_call(body, name=name, out_shape=jax.ShapeDtypeStruct((N_BUCKETS, GROUP_HEADS), F32),
                          compiler_params=_cp())(dtab, buckets)


HEAD_ROWS = GROUP_HEADS * BLOCK


def _stack_heads(x):
    return jnp.concatenate([jnp.where(_head_mask(h, x.shape), x, 0.0) for h in range(GROUP_HEADS)], axis=0).astype(BF16)


def _sb_tile(qs, kb, q0, k0):
    z = lax.dot_general(qs, kb, _NT, preferred_element_type=F32)
    row = lax.broadcasted_iota(jnp.int32, (HEAD_ROWS, BLOCK), 0) & (BLOCK - 1)
    col = lax.broadcasted_iota(jnp.int32, (HEAD_ROWS, BLOCK), 1)
    causal = (k0 + col) < (q0 + row)
    ls_neg = -(jnp.maximum(z, 0.0) + jnp.log(1.0 + jnp.exp(-jnp.abs(z))))
    return jnp.where(causal, ls_neg, 0.0), ls_neg + z, causal


SB_DEAD = -70.0
SB_FIRST_LANE = GROUP_HEADS


def _tri(strict_upper_src):
    r = lax.broadcasted_iota(jnp.int32, (BLOCK, BLOCK), 0)
    c = lax.broadcasted_iota(jnp.int32, (BLOCK, BLOCK), 1)
    cond = {"gt": r > c, "le": r <= c, "lt": r < c}[strict_upper_src]
    return jnp.where(cond, 1.0, 0.0).astype(BF16)


def _sb_fwd(proj, gain, seq, name, items=()):
    T = proj.shape[0]
    nblk = seq // BLOCK

    def body(q_ref, k_ref, v_ref, g_ref, o_ref, raw_ref, bt_ref):
        i = pl.program_id(1)
        q = q_ref[...]
        u_gt = _tri("gt")
        lane = lax.broadcasted_iota(jnp.int32, (BLOCK, BLOCK), 1)
        heads = [slice(h * HEAD_DIM, (h + 1) * HEAD_DIM) for h in range(GROUP_HEADS)]
        rows = [slice(h * BLOCK, (h + 1) * BLOCK) for h in range(GROUP_HEADS)]
        qs = _stack_heads(q * ATT_SCALE)

        def live(carry):
            j, _, cb = carry
            return (j >= 0) & (jnp.max(cb) > SB_DEAD)

        def step(carry):
            j, accs, cb = carry
            ks = pl.multiple_of(j * BLOCK, BLOCK)
            kb = k_ref[pl.ds(ks, BLOCK), :].astype(BF16)
            vb = v_ref[pl.ds(ks, BLOCK), :].astype(BF16)
            b, a, causal = _sb_tile(qs, kb, i * BLOCK, j * BLOCK)
            tail = _split_dot(b, u_gt) + cb
            w = jnp.where(causal, jnp.exp(a + tail), 0.0).astype(BF16)
            accs = tuple(accs[h] + jnp.dot(w[rows[h]], vb[:, hs], preferred_element_type=F32)
                         for h, hs in enumerate(heads))
            return j - 1, accs, cb + jnp.sum(b, axis=1, keepdims=True)

        zero_acc = tuple(jnp.zeros((BLOCK, HEAD_DIM), F32) for _ in heads)
        j_end, accs, cb = lax.while_loop(live, step, (i, zero_acc, jnp.zeros((HEAD_ROWS, 1), F32)))
        side = jnp.where(lane == SB_FIRST_LANE, (j_end + 1).astype(F32), 0.0)
        for h in range(GROUP_HEADS):
            side = jnp.where(lane == h, cb[rows[h]], side)
        raw = jnp.concatenate(accs, axis=1)
        raw_ref[...] = raw
        bt_ref[...] = side
        o_ref[...] = _rms(raw, g_ref[...]).astype(BF16)

    return _call(
        body, (proj, proj, proj, gain), grid=(T // seq, nblk), name=name, items=items,
        in_specs=[pl.BlockSpec((BLOCK, GROUP_WIDTH), lambda b, i: (b * nblk + i, 5)),
                  pl.BlockSpec((seq, GROUP_WIDTH), lambda b, i: (b, 6)),
                  pl.BlockSpec((seq, GROUP_WIDTH), lambda b, i: (b, 7)),
                  pl.BlockSpec(gain.shape, lambda b, i: (0, 0))],
        out_specs=[pl.BlockSpec((BLOCK, GROUP_WIDTH), lambda b, i: (b * nblk + i, 0)),
                   pl.BlockSpec((BLOCK, GROUP_WIDTH), lambda b, i: (b * nblk + i, 0)),
                   pl.BlockSpec((BLOCK, BLOCK), lambda b, i: (b * nblk + i, 0))],
        out_shape=[jax.ShapeDtypeStruct((T, GROUP_WIDTH), BF16), jax.ShapeDtypeStruct((T, GROUP_WIDTH), F32),
                   jax.ShapeDtypeStruct((T, BLOCK), F32)],
        sem=("parallel", "parallel"))


def _sb_bwd(proj, gain, raw, btot, dy, seq, name, items=()):
    T = proj.shape[0]
    nblk = seq // BLOCK

    def body(q_ref, k_ref, v_ref, g_ref, raw_ref, bt_ref, dy_ref, dq_ref, dk_ref, dv_ref, dg_ref):
        i = pl.program_id(1)

        @pl.when(i == 0)
        def _():
            dk_ref[...] = jnp.zeros_like(dk_ref)
            dv_ref[...] = jnp.zeros_like(dv_ref)

        rawv = raw_ref[...]
        _, vjp = jax.vjp(_rms, rawv, g_ref[...])
        do, dg = vjp(dy_ref[...])
        _acc_out((pl.program_id(0) == 0) & (i == 0), (dg_ref,), (dg,))
        q = q_ref[...]
        bt = bt_ref[...]
        u_le = _tri("le")
        u_lt = _tri("lt")
        heads = [slice(h * HEAD_DIM, (h + 1) * HEAD_DIM) for h in range(GROUP_HEADS)]
        rows = [slice(h * BLOCK, (h + 1) * BLOCK) for h in range(GROUP_HEADS)]
        qs = _stack_heads(q * ATT_SCALE)
        dos = _stack_heads(do)
        bts = jnp.concatenate([bt[:, h:h + 1] for h in range(GROUP_HEADS)], axis=0)
        first = jnp.max(bt[:, SB_FIRST_LANE:SB_FIRST_LANE + 1]).astype(jnp.int32)
        first = jnp.minimum(jnp.maximum(first, 0), i)

        def step(j, carry):
            dqs, cb, cg = carry
            ks = pl.multiple_of(j * BLOCK, BLOCK)
            kb = k_ref[pl.ds(ks, BLOCK), :].astype(BF16)
            vb = v_ref[pl.ds(ks, BLOCK), :].astype(BF16)
            b, a, causal = _sb_tile(qs, kb, i * BLOCK, j * BLOCK)
            tail = bts - (_split_dot(b, u_le) + cb)
            w = jnp.where(causal, jnp.exp(a + tail), 0.0)
            sig = jnp.exp(a)
            g = w * lax.dot_general(dos, vb, _NT, preferred_element_type=F32)
            gpre = _split_dot(g, u_lt) + cg
            dz = jnp.where(causal, g * (1.0 - sig) - gpre * sig, 0.0).astype(BF16)
            dqs = tuple(dqs[h] + jnp.dot(dz[rows[h]], kb[:, hs], preferred_element_type=F32)
                        for h, hs in enumerate(heads))
            dk_ref[pl.ds(ks, BLOCK), :] += lax.dot_general(dz, qs, _TN, preferred_element_type=F32)
            dv_ref[pl.ds(ks, BLOCK), :] += lax.dot_general(w.astype(BF16), dos, _TN, preferred_element_type=F32)
            return dqs, cb + jnp.sum(b, axis=1, keepdims=True), cg + jnp.sum(g, axis=1, keepdims=True)

        zero_dq = tuple(jnp.zeros((BLOCK, HEAD_DIM), F32) for _ in heads)
        zero = jnp.zeros((HEAD_ROWS, 1), F32)
        dqs, _, _ = lax.fori_loop(first, i + 1, step, (zero_dq, zero, zero))
        dq_ref[...] = (jnp.concatenate(dqs, axis=1) * ATT_SCALE).astype(BF16)

    blk = lambda c: pl.BlockSpec((BLOCK, GROUP_WIDTH), lambda b, i: (b * nblk + i, c))
    seqblk = lambda c: pl.BlockSpec((seq, GROUP_WIDTH), lambda b, i: (b, c))
    vec = pl.BlockSpec(gain.shape, lambda b, i: (0, 0))
    return _call(
        body, (proj, proj, proj, gain, raw, btot, dy), grid=(T // seq, nblk), name=name, items=items,
        in_specs=[blk(5), seqblk(6), seqblk(7), vec, blk(0), pl.BlockSpec((BLOCK, BLOCK), lambda b, i: (b * nblk + i, 0)),
                  blk(3)],
        out_specs=[blk(0), seqblk(0), seqblk(0), vec],
        out_shape=[jax.ShapeDtypeStruct((T, GROUP_WIDTH), BF16), jax.ShapeDtypeStruct((T, GROUP_WIDTH), F32),
                   jax.ShapeDtypeStruct((T, GROUP_WIDTH), F32), jax.ShapeDtypeStruct(gain.shape, F32)],
        sem=("arbitrary", "arbitrary"))


def _layer_params(l, sgu_w, sgu_b, pool_w, pool_scale, swa_sinks, mix_out_gain, norm_mix, norm_ffn):
    gains = mix_out_gain[l].reshape(4, 1, GROUP_WIDTH)
    return dict(
        wm=sgu_w[l], bt=sgu_b[l].T,
        wbd=jax.scipy.linalg.block_diag(*[pool_w[l, g] for g in range(len(POOL_WINDOWS))]),
        scale=pool_scale[l][None], sinks=swa_sinks[l][None],
        gain=[gains[m] for m in range(4)], norm_mix=norm_mix[l][None], norm_ffn=norm_ffn[l][None])


def _as_weights(g_in, g_out, g_gu, g_down):
    return (g_in[None], g_out.reshape(1, D_MODEL, D_MODEL), g_gu[None], g_down.reshape(1, D_FF_PAD, D_MODEL))


def _gather_item(src, l, dst_shape, rows=None, down=False, init=None):
    r0, nr = rows or (0, src.shape[1])
    if down:
        place = lambda dst, k: dst.at[k // 2, pl.ds((k % 2) * DOWN_ROWS + r0, nr), :]
    else:
        place = lambda dst, k: dst.at[k, pl.ds(r0, nr), :]
    return _Item(src, lambda s, k: s.at[l, pl.ds(r0, nr), :], dst_shape, place, init, relayed=True)


WEIGHTS = ("in", "out", "gu", "down")
FWD_PLAN = {
    (0, "in_proj"): [(0, "down", None)], (0, "swa"): [(0, "gu", 0)], (0, "sb"): [(0, "gu", 1)],
    (0, "ffn_up"): [(1, "gu", 0), (1, "in", None), (1, "out", None)], (0, "ffn_down"): [(1, "down", None)],
    (1, "in_proj"): [(1, "gu", 1)], (1, "swa"): [(2, "in", None), (2, "out", None)], (1, "sb"): [(2, "gu", 0)],
    (1, "ffn_up"): [(2, "gu", 1), (2, "down", None)],
    (2, "in_proj"): [(3, "down", None)], (2, "swa"): [(3, "in", None), (3, "out", None)], (2, "sb"): [(3, "gu", 0)],
    (2, "ffn_up"): [(3, "gu", 1)],
}


def _planned(pieces, shards, bufs):
    items = []
    for lyr, t, part in pieces:
        src = shards[WEIGHTS.index(t)]
        down = t == "down"
        shape = (4, FF_PAD, D_MODEL) if down else (N_DEV,) + src.shape[1:]
        init = bufs.get((lyr, t))
        if init is None and down:
            init = jnp.zeros(shape, BF16)
        rows = None if part is None else (part * (FF_PAD // 2), FF_PAD // 2)
        items.append(_gather_item(src, lyr, shape, rows, down, init))
    return items


def _layer_fwd(l, x, p, bufs, tab, seq, shards):
    def beside(host):
        return _planned(FWD_PLAN.get((l, host), ()), shards, bufs)

    def landed(host, results):
        for (lyr, t, _), r in zip(FWD_PLAN.get((l, host), ()), results):
            bufs[(lyr, t)] = r

    proj, h1, *got = _norm_mm(x, p["norm_mix"], bufs[(l, "in")][None], 0, 4, F32, f"in_proj_{l}",
                              items=beside("in_proj"))
    landed("in_proj", got)
    ya, = _sgu_fwd(proj, p["wm"], p["bt"], p["gain"][0], f"sgu_fwd_{l}")
    yb = _pool_fwd(proj, p["wbd"], p["scale"], p["gain"][1], seq, f"pool_fwd_{l}")
    yc, *got = _swa_fwd(proj, p["sinks"], tab, p["gain"][2], seq, f"swa_fwd_{l}", items=beside("swa"))
    landed("swa", got)
    yd, raw, btot, *got = _sb_fwd(proj, p["gain"][3], seq, f"sb_fwd_{l}", items=beside("sb"))
    landed("sb", got)
    ycat = jnp.concatenate([ya, yb, yc, yd], axis=1)
    xm, = _mm_res(x, ycat, bufs[(l, "out")].reshape(1, D_MODEL, D_MODEL), 0, f"out_proj_{l}", tn=D_MODEL)
    gu, act, h2, *got = _ffn_up_act(xm, p["norm_ffn"], bufs[(l, "gu")][None], f"ffn_up_{l}", items=beside("ffn_up"))
    landed("ffn_up", got)
    xo, *got = _mm_res(xm, act, bufs[(l, "down")].reshape(1, D_FF_PAD, D_MODEL), 0, f"ffn_down_{l}",
                       tn=D_MODEL // 2, items=beside("ffn_down"))
    landed("ffn_down", got)
    return xo, (x, proj, h1, ycat, raw, btot, xm, gu, h2, act)


def _rows_item(g, r0, nr, init=None):
    cut = lambda a, k: a.at[k, pl.ds(r0, nr), :]
    return _Item(g, cut, g.shape, cut, init)


def _layer_bwd(l, dxo, saved, p, w, tab, seq, pending, ride):
    win, wout, wgu, wd = w
    x, proj, h1, ycat, raw, btot, xm, gu, h2, act = saved
    out_rows = D_MODEL // N_DEV
    in_half = D_MODEL // 2
    pieces = 2 if ride else 3
    piece = FF_PAD // pieces

    above = [_rows_item(pending[0], in_half, in_half, pending[1])] if pending else []
    dgu, *p_in_above = _ffn_down_dx(dxo, wd, gu, f"ffn_down_dx_{l}", items=above)
    g_wd, = _mm_tn_rows(act, dxo, FF_PAD, f"ffn_down_dw_{l}")
    send_down = _Item(g_wd, lambda src, k: src.at[pl.ds((k // 2) * FF_PAD + (k % 2) * DOWN_ROWS, DOWN_ROWS), :],
                      (N_DEV, DOWN_ROWS, D_MODEL))
    g_wgu, p_down = _mm_tn_rows(dgu, h2, FF_PAD, f"ffn_up_dw_{l}", items=(send_down,), block_of=_ffn_chunk_block)
    g_wgu = g_wgu.reshape(N_DEV, FF_PAD, D_MODEL)
    dxm, g_norm_ffn, p_gu = _mm_norm_bwd(dgu, wgu, 0, 1, xm, p["norm_ffn"], dxo, f"ffn_up_dx_{l}",
                                         items=(_rows_item(g_wgu, 0, piece),), block_of=_ffn_chunk_block,
                                         transposed=True)
    dycat = _mm_nt(dxm, wout, 0, F32, f"out_proj_dx_{l}")
    g_wout, = _mm_tn_rows(ycat, dxm, D_MODEL, f"out_proj_dw_{l}")
    send_out = _Item(g_wout, lambda src, k: src.at[pl.ds(k * out_rows, out_rows), :], (N_DEV, out_rows, D_MODEL))
    duv, g_wm, g_bt, g_ga = _sgu_bwd(proj, p["wm"], p["bt"], p["gain"][0], dycat, f"sgu_bwd_{l}")
    dp, g_wbd, g_scale, g_gb = _pool_bwd(proj, p["wbd"], p["scale"], p["gain"][1], dycat, seq, f"pool_bwd_{l}")
    beside_swa = list(ride) if ride else [_rows_item(g_wgu, piece, piece, p_gu)]
    dq, dkp, dkc, g_sinks, g_tab, g_gc, *rode = _swa_bwd(proj, p["sinks"], tab, p["gain"][2], dycat, seq,
                                                         f"swa_bwd_{l}", items=beside_swa)
    if not ride:
        p_gu, rode = rode[0], []
    dkv = _shift_add(dkc, dkp, seq, f"swa_dkv_{l}")
    dqd, dkd, dvd, g_gd, p_gu, p_out = _sb_bwd(
        proj, p["gain"][3], raw, btot, dycat, seq, f"sb_bwd_{l}",
        items=(_rows_item(g_wgu, (pieces - 1) * piece, piece, p_gu), send_out))
    dproj = jnp.concatenate([duv, dp, dq, dkv, dqd, dkd.astype(BF16), dvd.astype(BF16)], axis=1)
    g_win, = _mm_tn_cols(h1, dproj, GROUP_WIDTH, 4, f"in_proj_dw_{l}")
    defer = l > 0
    dx, g_norm_mix, p_in = _mm_norm_bwd(dproj, win, 0, 4, x, p["norm_mix"], dxm, f"in_proj_dx_{l}",
                                        items=(_rows_item(g_win, 0, in_half if defer else D_MODEL),))
    pending = (g_win, p_in) if defer else None
    ng = len(POOL_WINDOWS)
    gd = GROUP_WIDTH // ng
    small = dict(
        sgu_w=g_wm, sgu_b=g_bt.T,
        pool_w=jnp.stack([g_wbd[g * gd:(g + 1) * gd, g * gd:(g + 1) * gd] for g in range(ng)]),
        pool_scale=g_scale[0], swa_sinks=g_sinks[0],
        mix_out_gain=jnp.concatenate([g_ga[0], g_gb[0], g_gc[0], g_gd[0]]),
        norm_mix=g_norm_mix[0], norm_ffn=g_norm_ffn[0])
    return dx, [p_in, p_out, p_gu, p_down], small, g_tab, rode, p_in_above, pending


def _local_step(x, target, shards, sgu_w, sgu_b, pool_w, pool_scale, swa_sinks, rel_bias, mix_out_gain,
                norm_mix, norm_ffn, norm_final, seq):
    buckets = _bucket_map()
    bufs = {}
    first = [(0, "in", None), (0, "out", None)]
    tab, *got = _bias_table(rel_bias, buckets, "bias_table", items=_planned(first, shards, bufs))
    bufs.update({(lyr, t): r for (lyr, t, _), r in zip(first, got)})
    params = [_layer_params(l, sgu_w, sgu_b, pool_w, pool_scale, swa_sinks, mix_out_gain, norm_mix, norm_ffn)
              for l in range(DEPTH)]
    saved, weights = [], []
    for l in range(DEPTH):
        x, s = _layer_fwd(l, x, params[l], bufs, tab, seq, shards)
        saved.append(s)
        weights.append(_as_weights(*[bufs[(l, t)] for t in WEIGHTS]))
    loss, dx, g_final = _loss_head(x, norm_final[None], target, "loss_head")
    big, small, g_tab, upper, pending = [None] * DEPTH, [None] * DEPTH, None, None, None
    for l in reversed(range(DEPTH)):
        ride = ()
        if l == 0:
            mine = _pack(_upper_layers(small), _pack_rows(_upper_layers(small)))
            ride = (_Item(mine, lambda src, k: src, (N_DEV,) + mine.shape),)
        dx, big[l], small[l], t, rode, done_above, pending = _layer_bwd(
            l, dx, saved[l], params[l], weights[l], tab, seq, pending, ride)
        g_tab = t if g_tab is None else g_tab + t
        upper = rode[0] if rode else upper
        if done_above:
            big[l + 1][0] = done_above[0]
    return loss, dx, big, small, _bias_table_bwd(g_tab, buckets, "bias_table_bwd"), g_final[0], upper


def _cast_pad(w, rows, name):
    L, r, c = w.shape

    def body(w_ref, o_ref):
        if rows != r:
            o_ref[...] = jnp.zeros_like(o_ref)
        o_ref[:r, :] = w_ref[...].astype(BF16)

    return pl.pallas_call(
        body, grid=(L,), name=name,
        in_specs=[pl.BlockSpec((None, r, c), lambda l: (l, 0, 0))],
        out_specs=pl.BlockSpec((None, rows, c), lambda l: (l, 0, 0)),
        out_shape=jax.ShapeDtypeStruct((L, rows, c), BF16),
        compiler_params=_cp("parallel"))(w)


def _adamw(w, g, m, v):
    m = ADAM_B1 * m + (1.0 - ADAM_B1) * g
    v = ADAM_B2 * v + (1.0 - ADAM_B2) * jnp.square(g)
    m_hat = m / (1.0 - ADAM_B1 ** ADAM_STEP)
    v_hat = v / (1.0 - ADAM_B2 ** ADAM_STEP)
    delta = -ADAM_LR * (m_hat / (jnp.sqrt(v_hat) + ADAM_EPS) + ADAM_WD * w)
    return delta, m, v


def _adamw_sharded(parts, w, m, v, tr, name, items=()):
    L, r, c = w.shape
    cp = parts[0].shape[-1]
    nrow = r // tr

    def body(*refs):
        p_refs, (w_ref, m_ref, v_ref, g_ref, d_ref, nm_ref, nv_ref) = refs[:L], refs[L:]
        for k in range(L):
            @pl.when(pl.program_id(0) == k)
            def _(p_ref=p_refs[k]):
                g = p_ref[0, :, :c].astype(F32)
                for dev in range(1, N_DEV):
                    g = g + p_ref[dev, :, :c].astype(F32)
                delta, nm, nv = _adamw(w_ref[...], g, m_ref[...], v_ref[...])
                g_ref[...] = g
                d_ref[...] = delta
                nm_ref[...] = nm
                nv_ref[...] = nv

    def part_spec(k):
        return pl.BlockSpec((N_DEV, tr, cp),
                            lambda l, i: (0, jnp.where(l == k, i, jnp.where(l < k, 0, nrow - 1)), 0))

    blk = pl.BlockSpec((None, tr, c), lambda l, i: (l, i, 0))
    out = jax.ShapeDtypeStruct((L, r, c), F32)
    return _call(
        body, (*parts, w, m, v), grid=(L, nrow), name=name, items=items,
        in_specs=[part_spec(k) for k in range(L)] + [blk, blk, blk],
        out_specs=[blk] * 4, out_shape=[out] * 4, sem=("arbitrary", "arbitrary"))


def _adamw_small(parts, wmv, name):
    def body(p_ref, wmv_ref, g_ref, d_ref, nm_ref, nv_ref):
        g = p_ref[0]
        for k in range(1, N_DEV):
            g = g + p_ref[k]
        delta, nm, nv = _adamw(wmv_ref[0], g, wmv_ref[1], wmv_ref[2])
        g_ref[...] = g
        d_ref[...] = delta
        nm_ref[...] = nm
        nv_ref[...] = nv

    out = jax.ShapeDtypeStruct(wmv.shape[1:], F32)
    return pl.pallas_call(body, name=name, out_shape=[out] * 4, compiler_params=_cp())(parts, wmv)


LAYERED = ("sgu_w", "sgu_b", "pool_w", "pool_scale", "swa_sinks", "mix_out_gain", "norm_mix", "norm_ffn")
SHARED = ("rel_bias", "norm_final")


def _seg_rows(a):
    return -(-a.size // 128)


def _pack_rows(parts):
    return -(-sum(_seg_rows(p) for p in parts) // 8) * 8


def _upper_layers(per_layer):
    if isinstance(per_layer, dict):
        return [per_layer[k][1:] for k in LAYERED]
    return [jnp.stack([per_layer[l][k] for l in range(1, DEPTH)]) for k in LAYERED]


def _layer_zero(stacked):
    return [stacked[k][:1] for k in LAYERED] + [stacked[k] for k in SHARED]


def _pack(parts, rows):
    segs = [jnp.pad(p.reshape(-1), (0, _seg_rows(p) * 128 - p.size)).reshape(_seg_rows(p), 128) for p in parts]
    used = sum(s.shape[0] for s in segs)
    return jnp.concatenate(segs + [jnp.zeros((rows - used, 128), F32)], axis=0)


def _pack_groups(groups, rows):
    segs = []
    for parts in groups:
        segs += [jnp.pad(p.reshape(-1), (0, _seg_rows(p) * 128 - p.size)).reshape(_seg_rows(p), 128) for p in parts]
        segs.append(jnp.zeros((rows - sum(_seg_rows(p) for p in parts), 128), F32))
    return jnp.concatenate(segs, axis=0).reshape(len(groups), rows, 128)


def _unpack(buf, like):
    out, at = [], 0
    for a in like:
        out.append(buf[at:at + _seg_rows(a)].reshape(-1)[:a.size].reshape(a.shape))
        at += _seg_rows(a)
    return out


def kernel(x, w_in, w_out, sgu_w, sgu_b, pool_w, pool_scale, swa_sinks, rel_bias, mix_out_gain, norm_mix, norm_ffn, w_gate_up, w_down, norm_final, loss_target, m_w_in, m_w_out, m_sgu_w, m_sgu_b, m_pool_w, m_pool_scale, m_swa_sinks, m_rel_bias, m_mix_out_gain, m_norm_mix, m_norm_ffn, m_w_gate_up, m_w_down, m_norm_final, v_w_in, v_w_out, v_sgu_w, v_sgu_b, v_pool_w, v_pool_scale, v_swa_sinks, v_rel_bias, v_mix_out_gain, v_norm_mix, v_norm_ffn, v_w_gate_up, v_w_down, v_norm_final):
    bl, seq, _ = x.shape
    L = w_in.shape[0]
    gu_t, m_gu_t, v_gu_t = (jnp.swapaxes(a, 1, 2) for a in (w_gate_up, m_w_gate_up, v_w_gate_up))
    shards = (_cast_pad(w_in, D_MODEL, "shard_w_in"), _cast_pad(w_out, D_MODEL // N_DEV, "shard_w_out"),
              _cast_pad(gu_t, FF_PAD, "shard_w_gate_up"), _cast_pad(w_down, DOWN_ROWS, "shard_w_down"))
    small_w = dict(sgu_w=sgu_w, sgu_b=sgu_b, pool_w=pool_w, pool_scale=pool_scale, swa_sinks=swa_sinks,
                   rel_bias=rel_bias, mix_out_gain=mix_out_gain, norm_mix=norm_mix, norm_ffn=norm_ffn,
                   norm_final=norm_final)
    small_m = dict(sgu_w=m_sgu_w, sgu_b=m_sgu_b, pool_w=m_pool_w, pool_scale=m_pool_scale, swa_sinks=m_swa_sinks,
                   rel_bias=m_rel_bias, mix_out_gain=m_mix_out_gain, norm_mix=m_norm_mix, norm_ffn=m_norm_ffn,
                   norm_final=m_norm_final)
    small_v = dict(sgu_w=v_sgu_w, sgu_b=v_sgu_b, pool_w=v_pool_w, pool_scale=v_pool_scale, swa_sinks=v_swa_sinks,
                   rel_bias=v_rel_bias, mix_out_gain=v_mix_out_gain, norm_mix=v_norm_mix, norm_ffn=v_norm_ffn,
                   norm_final=v_norm_final)
    loss, dx, big, small, g_rel_bias, g_final, upper = _local_step(
        x.reshape(bl * seq, D_MODEL), loss_target.reshape(bl * seq, D_MODEL), shards, sgu_w, sgu_b, pool_w,
        pool_scale, swa_sinks, rel_bias, mix_out_gain, norm_mix, norm_ffn, norm_final, seq)
    p_in, p_out, p_gu, p_down = ([big[l][t] for l in range(L)] for t in range(4))
    outs_gu = [jnp.swapaxes(a, 1, 2) for a in
               _adamw_sharded(p_gu, gu_t, m_gu_t, v_gu_t, FF_SHARD // 4, "adamw_w_gate_up")]
    outs_down = _adamw_sharded(p_down, w_down, m_w_down, v_w_down, DOWN_ROWS // 2, "adamw_w_down")
    outs_in = _adamw_sharded(p_in, w_in, m_w_in, v_w_in, 256, "adamw_w_in")
    lo_like = _layer_zero(small_w)
    lo_rows = _pack_rows(lo_like + [loss[0]])
    lo_mine = _pack([small[0][k][None] for k in LAYERED] + [g_rel_bias, g_final, loss[0]], lo_rows)
    *outs_out, lower = _adamw_sharded(p_out, w_out, m_w_out, v_w_out, D_MODEL // N_DEV, "adamw_w_out",
                                      items=(_Item(lo_mine, lambda src, k: src, (N_DEV,) + lo_mine.shape),))
    lo_res = _adamw_small(lower, _pack_groups([lo_like, _layer_zero(small_m), _layer_zero(small_v)], lo_rows),
                          "adamw_small_layer0")
    hi_like = _upper_layers(small_w)
    hi_rows = _pack_rows(hi_like)
    hi_res = _adamw_small(upper, _pack_groups([hi_like, _upper_layers(small_m), _upper_layers(small_v)], hi_rows),
                          "adamw_small_upper")
    loss_total = lo_res[0][sum(_seg_rows(a) for a in lo_like), 0]
    small_outs = []
    for lo_buf, hi_buf in zip(lo_res, hi_res):
        lo = dict(zip(LAYERED + SHARED, _unpack(lo_buf, lo_like)))
        hi = dict(zip(LAYERED, _unpack(hi_buf, hi_like)))
        small_outs.append({k: jnp.concatenate([lo[k], hi[k]], axis=0) if k in hi else lo[k] for k in lo})
    big_outs = dict(w_in=outs_in, w_out=outs_out, w_gate_up=outs_gu, w_down=outs_down)
    order = ("w_in", "w_out", "sgu_w", "sgu_b", "pool_w", "pool_scale", "swa_sinks", "rel_bias", "mix_out_gain",
             "norm_mix", "norm_ffn", "w_gate_up", "w_down", "norm_final")
    result = [loss_total, dx.reshape(bl, seq, D_MODEL)]
    for which in range(4):
        for name in order:
            result.append(big_outs[name][which] if name in big_outs else small_outs[which][name])
    return tuple(result)
```

```python
import functools

import jax
import jax.numpy as jnp
from jax import lax
from jax.experimental import pallas as pl
from jax.experimental.pallas import tpu as pltpu

F32 = jnp.float32
BF16 = jnp.bfloat16

N_DEV = 8
DEPTH = 4
D_MODEL = 1024
GROUP_WIDTH = 256
HEAD_DIM = 64
GROUP_HEADS = 4
BLOCK = 128
N_BUCKETS = 32
MAX_DISTANCE = 128
POOL_WINDOWS = (2, 4, 8, 16)
D_FF = 2816
FF_SHARD = D_FF // 4
FF_PAD = 768
D_FF_PAD = 4 * FF_PAD
EPS = 1e-6
ATT_SCALE = HEAD_DIM ** -0.5
ADAM_LR = 0.001
ADAM_B1 = 0.9
ADAM_B2 = 0.999
ADAM_EPS = 1e-08
ADAM_WD = 0.01
ADAM_STEP = 10
VMEM_LIMIT = 56 * 1024 * 1024
MESH_AXES = ("x", "y", "c")


def _cp(*sem):
    return pltpu.CompilerParams(dimension_semantics=sem or None, vmem_limit_bytes=VMEM_LIMIT)


_NT = (((1,), (1,)), ((), ()))
_TN = (((0,), (0,)), ((), ()))


@jax.custom_vjp
def _bdot(a, b):
    return jnp.dot(a.astype(BF16), b.astype(BF16), preferred_element_type=F32)


def _bdot_fwd(a, b):
    return _bdot(a, b), (a.astype(BF16), b.astype(BF16))


def _bdot_bwd(res, ct):
    a, b = res
    c = ct.astype(BF16)
    return (lax.dot_general(c, b, _NT, preferred_element_type=F32),
            lax.dot_general(a, c, _TN, preferred_element_type=F32))


_bdot.defvjp(_bdot_fwd, _bdot_bwd)


@jax.custom_vjp
def _bdot_nt(a, b):
    return lax.dot_general(a.astype(BF16), b.astype(BF16), _NT, preferred_element_type=F32)


def _bdot_nt_fwd(a, b):
    return _bdot_nt(a, b), (a.astype(BF16), b.astype(BF16))


def _bdot_nt_bwd(res, ct):
    a, b = res
    c = ct.astype(BF16)
    return (jnp.dot(c, b, preferred_element_type=F32),
            lax.dot_general(c, a, _TN, preferred_element_type=F32))


_bdot_nt.defvjp(_bdot_nt_fwd, _bdot_nt_bwd)


def _rms(x, g):
    return x * lax.rsqrt(jnp.mean(x * x, axis=-1, keepdims=True) + EPS) * g


def _sigmoid(x):
    return 0.5 * jnp.tanh(0.5 * x) + 0.5


def _split_dot(x, u):
    hi = x.astype(BF16)
    lo = (x - hi.astype(F32)).astype(BF16)
    return jnp.dot(hi, u, preferred_element_type=F32) + jnp.dot(lo, u, preferred_element_type=F32)


def _head_mask(h, shape):
    col = lax.broadcasted_iota(jnp.int32, shape, 1)
    return (col >= h * HEAD_DIM) & (col < (h + 1) * HEAD_DIM)


ANY = pl.BlockSpec(memory_space=pl.ANY)
MESH = pl.DeviceIdType.MESH
DOWN_ROWS = D_FF // N_DEV


def _my_place():
    x, y, c = (lax.axis_index(a) for a in MESH_AXES)
    return x, y, c, 4 * x + 2 * y + c


def _peer(x, y, c, d):
    return (x ^ (d >> 2), y ^ ((d >> 1) & 1), c ^ (d & 1))


class _Item:
    def __init__(self, src, block, dst_shape, place=None, init=None, relayed=False):
        self.src, self.block, self.dst_shape, self.init, self.relayed = src, block, dst_shape, init, relayed
        self.place = place or (lambda dst, k: dst.at[k])


def _call(body, args, *, grid, in_specs, out_specs, out_shape, sem, name, scratch_shapes=(), items=()):
    if not items:
        outs = pl.pallas_call(body, grid=grid, in_specs=in_specs, out_specs=out_specs, out_shape=out_shape, name=name,
                              scratch_shapes=list(scratch_shapes), compiler_params=_cp(*sem))(*args)
        return list(outs) if isinstance(outs, (list, tuple)) else [outs]
    n_in, n_out, n_scr, n = len(in_specs), len(out_specs), len(scratch_shapes), len(items)
    inits = [i for i, it in enumerate(items) if it.init is not None]

    def wrapped(*refs):
        core_in, srcs = refs[:n_in], refs[n_in:n_in + n]
        off = n_in + n + len(inits)
        core_out, dsts = refs[off:off + n_out], refs[off + n_out:off + n_out + n]
        scratch = refs[off + n_out + n:]
        send_sems, recv_sems, local_sems = scratch[n_scr:]
        step = pl.program_id(0)
        for a in range(1, len(grid)):
            step = step * grid[a] + pl.program_id(a)
        steps = functools.reduce(lambda p, g: p * g, grid)
        relay_step = max(0, steps - 1 - max(1, steps // 8))
        x, y, c, me = _my_place()
        direct = [i for i in range(n) if not items[i].relayed]
        relayed = [i for i in range(n) if items[i].relayed]
        chips = [(1 - x, y), (x, 1 - y), (1 - x, 1 - y)]
        sibling = (x, y, 1 - c)

        def local(i):
            return pltpu.make_async_copy(items[i].block(srcs[i], me), items[i].place(dsts[i], me), local_sems.at[i])

        def remote(d, i, sending):
            px, py, pc = _peer(x, y, c, d)
            pk = 4 * px + 2 * py + pc
            return pltpu.make_async_remote_copy(
                src_ref=items[i].block(srcs[i], pk), dst_ref=items[i].place(dsts[i], me if sending else pk),
                send_sem=send_sems.at[d - 1, i], recv_sem=recv_sems.at[d - 1, i],
                device_id=(px, py, pc), device_id_type=MESH)

        def hop(slot, i, owner, to, from_src):
            k = 4 * owner[0] + 2 * owner[1] + owner[2]
            src = items[i].block(srcs[i], me) if from_src else items[i].place(dsts[i], k)
            return pltpu.make_async_remote_copy(
                src_ref=src, dst_ref=items[i].place(dsts[i], k), send_sem=send_sems.at[slot, i],
                recv_sem=recv_sems.at[slot, i], device_id=to, device_id_type=MESH)

        @pl.when(step == 0)
        def _():
            for i in range(n):
                local(i).start()
            for d in range(1, N_DEV):
                for i in direct:
                    remote(d, i, True).start()
            for i in relayed:
                hop(0, i, (x, y, c), sibling, True).start()
                for j, chip in enumerate(chips):
                    hop(1 + j, i, (x, y, c), (*chip, c), True).start()

        body(*core_in, *core_out, *scratch[:n_scr])

        if relayed:
            @pl.when(step == relay_step)
            def _():
                for i in relayed:
                    for j, chip in enumerate(chips):
                        hop(1 + j, i, (*chip, c), (x, y, c), False).wait_recv()
                        hop(4 + j, i, (*chip, c), sibling, False).start()

        @pl.when(step == steps - 1)
        def _():
            for d in range(1, N_DEV):
                for i in direct:
                    remote(d, i, False).wait_recv()
            for i in relayed:
                hop(0, i, sibling, (x, y, c), False).wait_recv()
                for j, chip in enumerate(chips):
                    hop(4 + j, i, (*chip, 1 - c), (x, y, c), False).wait_recv()
            for d in range(1, N_DEV):
                for i in direct:
                    remote(d, i, True).wait_send()
            for i in relayed:
                for slot in range(N_DEV - 1):
                    hop(slot, i, (x, y, c), sibling, True).wait_send()
            for i in range(n):
                local(i).wait()

    outs = pl.pallas_call(
        wrapped, grid=grid, name=name,
        in_specs=list(in_specs) + [ANY] * (n + len(inits)), out_specs=list(out_specs) + [ANY] * n,
        out_shape=list(out_shape) + [jax.ShapeDtypeStruct(it.dst_shape, it.src.dtype) for it in items],
        input_output_aliases={n_in + n + j: n_out + i for j, i in enumerate(inits)},
        scratch_shapes=list(scratch_shapes) + [pltpu.SemaphoreType.DMA((N_DEV - 1, n)),
                                               pltpu.SemaphoreType.DMA((N_DEV - 1, n)), pltpu.SemaphoreType.DMA((n,))],
        compiler_params=_cp(*(["arbitrary"] * len(grid))),
    )(*args, *[it.src for it in items], *[items[i].init for i in inits])
    return list(outs)


def _norm_mm(x, g, w, l, jb, out_dtype, name, tm=1024, items=()):
    T, K = x.shape
    _, nb, _, tn = w.shape
    tm = min(tm, T)

    def body(x_ref, g_ref, w_ref, o_ref, h_ref):
        @pl.when(pl.program_id(1) == 0)
        def _():
            h_ref[...] = _rms(x_ref[...], g_ref[...]).astype(BF16)
        h = h_ref[...]
        for jj in range(jb):
            o_ref[:, jj * tn:(jj + 1) * tn] = jnp.dot(h, w_ref[jj], preferred_element_type=F32).astype(o_ref.dtype)

    return _call(
        body, (x, g, w), grid=(T // tm, nb // jb), name=name, items=items,
        in_specs=[pl.BlockSpec((tm, K), lambda i, j: (i, 0)), pl.BlockSpec((1, K), lambda i, j: (0, 0)),
                  pl.BlockSpec((None, jb, K, tn), lambda i, j: (l, j, 0, 0))],
        out_specs=[pl.BlockSpec((tm, jb * tn), lambda i, j: (i, j)), pl.BlockSpec((tm, K), lambda i, j: (i, 0))],
        out_shape=[jax.ShapeDtypeStruct((T, nb * tn), out_dtype), jax.ShapeDtypeStruct((T, K), BF16)],
        sem=("parallel", "arbitrary"))


def _mm_res(res, a, w, l, name, tn, tm=1024, items=()):
    T, K = a.shape
    N = w.shape[2]
    tm = min(tm, T)

    def body(r_ref, a_ref, w_ref, o_ref):
        o_ref[...] = r_ref[...] + jnp.dot(a_ref[...], w_ref[...], preferred_element_type=F32)

    return _call(
        body, (res, a, w), grid=(T // tm, N // tn), name=name, items=items,
        in_specs=[pl.BlockSpec((tm, tn), lambda i, j: (i, j)), pl.BlockSpec((tm, K), lambda i, j: (i, 0)),
                  pl.BlockSpec((None, K, tn), lambda i, j: (l, 0, j))],
        out_specs=[pl.BlockSpec((tm, tn), lambda i, j: (i, j))],
        out_shape=[jax.ShapeDtypeStruct((T, N), F32)],
        sem=("parallel", "parallel"))


def _mm_nt(a, w, l, out_dtype, name, tm=1024, tn=1024):
    T, K = a.shape
    N = w.shape[1]
    tm = min(tm, T)

    def body(a_ref, w_ref, o_ref):
        o_ref[...] = lax.dot_general(a_ref[...].astype(BF16), w_ref[...], _NT,
                                     preferred_element_type=F32).astype(o_ref.dtype)

    return pl.pallas_call(
        body, grid=(T // tm, N // tn), name=name,
        in_specs=[pl.BlockSpec((tm, K), lambda i, j: (i, 0)), pl.BlockSpec((None, tn, K), lambda i, j: (l, j, 0))],
        out_specs=pl.BlockSpec((tm, tn), lambda i, j: (i, j)),
        out_shape=jax.ShapeDtypeStruct((T, N), out_dtype),
        compiler_params=_cp("parallel", "parallel"))(a, w)


def _mm_norm_bwd(a, w, l, jb, x, g, dres, name, tm=1024, items=(), block_of=lambda j: j, transposed=False):
    T = a.shape[0]
    _, nb, K, tn = w.shape
    if transposed:
        tn, K = K, tn
    tm = min(tm, T)
    nj = nb // jb
    sub = min(256, tm)
    dims = (((1,), (0,)), ((), ())) if transposed else _NT

    def body(a_ref, w_ref, x_ref, g_ref, r_ref, o_ref, dg_ref):
        part = lax.dot_general(a_ref[:, :tn], w_ref[0], dims, preferred_element_type=F32)
        for jj in range(1, jb):
            part += lax.dot_general(a_ref[:, jj * tn:(jj + 1) * tn], w_ref[jj], dims, preferred_element_type=F32)

        @pl.when(pl.program_id(1) == 0)
        def _():
            o_ref[...] = part

        @pl.when(pl.program_id(1) > 0)
        def _():
            o_ref[...] += part

        @pl.when(pl.program_id(1) == nj - 1)
        def _():
            dg = jnp.zeros((1, K), F32)
            for r in range(tm // sub):
                rows = pl.ds(r * sub, sub)
                _, vjp = jax.vjp(_rms, x_ref[rows, :], g_ref[...])
                dx, dg_r = vjp(o_ref[rows, :])
                o_ref[rows, :] = r_ref[rows, :] + dx
                dg = dg + dg_r
            _acc_out(pl.program_id(0) == 0, (dg_ref,), (dg,))

    row = pl.BlockSpec((tm, K), lambda i, j: (i, 0))
    vec = pl.BlockSpec((1, K), lambda i, j: (0, 0))
    return _call(
        body, (a, w, x, g, dres), grid=(T // tm, nj), name=name, items=items,
        in_specs=[pl.BlockSpec((tm, jb * tn), lambda i, j: (i, j)),
                  pl.BlockSpec((None, jb) + w.shape[2:], lambda i, j: (l, block_of(j), 0, 0)), row, vec, row],
        out_specs=[row, vec],
        out_shape=[jax.ShapeDtypeStruct((T, K), F32), jax.ShapeDtypeStruct((1, K), F32)],
        sem=("arbitrary", "arbitrary"))


def _mm_tn_cols(lhs, rhs, tn, jb, name, tm=1024, items=(), block_of=lambda j: j):
    T, K = lhs.shape
    nb = rhs.shape[1] // tn
    tm = min(tm, T)
    nt = T // tm

    def body(l_ref, r_ref, o_ref, acc):
        part = lax.dot_general(l_ref[...], r_ref[...], _TN, preferred_element_type=F32)

        @pl.when(pl.program_id(1) == 0)
        def _():
            acc[...] = part

        @pl.when(pl.program_id(1) > 0)
        def _():
            acc[...] += part

        @pl.when(pl.program_id(1) == nt - 1)
        def _():
            for jj in range(jb):
                o_ref[jj] = acc[:, jj * tn:(jj + 1) * tn].astype(BF16)

    return _call(
        body, (lhs, rhs), grid=(nb // jb, nt), name=name, items=items,
        in_specs=[pl.BlockSpec((tm, K), lambda j, t: (t, 0)), pl.BlockSpec((tm, jb * tn), lambda j, t: (t, j))],
        out_specs=[pl.BlockSpec((jb, K, tn), lambda j, t: (block_of(j), 0, 0))],
        out_shape=[jax.ShapeDtypeStruct((nb, K, tn), BF16)],
        scratch_shapes=[pltpu.VMEM((K, jb * tn), F32)],
        sem=("parallel", "arbitrary"))


def _mm_tn_rows(lhs, rhs, tk, name, tm=1024, items=(), block_of=lambda j: j):
    T, Kl = lhs.shape
    N = rhs.shape[1]
    tm = min(tm, T)
    nt = T // tm

    def body(l_ref, r_ref, o_ref, acc):
        part = lax.dot_general(l_ref[...], r_ref[...].astype(BF16), _TN, preferred_element_type=F32)

        @pl.when(pl.program_id(1) == 0)
        def _():
            acc[...] = part

        @pl.when(pl.program_id(1) > 0)
        def _():
            acc[...] += part

        @pl.when(pl.program_id(1) == nt - 1)
        def _():
            o_ref[...] = acc[...].astype(BF16)

    return _call(
        body, (lhs, rhs), grid=(Kl // tk, nt), name=name, items=items,
        in_specs=[pl.BlockSpec((tm, tk), lambda l, t: (t, l)), pl.BlockSpec((tm, N), lambda l, t: (t, 0))],
        out_specs=[pl.BlockSpec((tk, N), lambda l, t: (block_of(l), 0))],
        out_shape=[jax.ShapeDtypeStruct((Kl, N), BF16)],
        scratch_shapes=[pltpu.VMEM((tk, N), F32)],
        sem=("parallel", "arbitrary"))


N_FF_CHUNK = D_FF_PAD // FF_PAD


def _ffn_chunk_block(j):
    return (j % 2) * N_FF_CHUNK + j // 2


def _ffn_up_act(x, g, w, name, tm=1024, items=()):
    T, K = x.shape
    tm = min(tm, T)

    def body(x_ref, g_ref, wg_ref, wu_ref, gu_ref, act_ref, h_ref):
        @pl.when(pl.program_id(1) == 0)
        def _():
            h_ref[...] = _rms(x_ref[...], g_ref[...]).astype(BF16)
        h = h_ref[...]
        gate = lax.dot_general(h, wg_ref[...], _NT, preferred_element_type=F32)
        up = lax.dot_general(h, wu_ref[...], _NT, preferred_element_type=F32)
        gu_ref[:, :FF_PAD] = gate.astype(BF16)
        gu_ref[:, FF_PAD:] = up.astype(BF16)
        act_ref[...] = (gate * _sigmoid(gate) * up).astype(BF16)

    return _call(
        body, (x, g, w, w), grid=(T // tm, N_FF_CHUNK), name=name, items=items,
        in_specs=[pl.BlockSpec((tm, K), lambda i, j: (i, 0)), pl.BlockSpec((1, K), lambda i, j: (0, 0)),
                  pl.BlockSpec((None, None, FF_PAD, K), lambda i, j: (0, j, 0, 0)),
                  pl.BlockSpec((None, None, FF_PAD, K), lambda i, j: (0, j + N_FF_CHUNK, 0, 0))],
        out_specs=[pl.BlockSpec((tm, 2 * FF_PAD), lambda i, j: (i, j)), pl.BlockSpec((tm, FF_PAD), lambda i, j: (i, j)),
                   pl.BlockSpec((tm, K), lambda i, j: (i, 0))],
        out_shape=[jax.ShapeDtypeStruct((T, 2 * D_FF_PAD), BF16), jax.ShapeDtypeStruct((T, D_FF_PAD), BF16),
                   jax.ShapeDtypeStruct((T, K), BF16)],
        sem=("parallel", "arbitrary"))


def _ffn_down_dx(dxo, w, gu, name, tm=1024, items=()):
    T, K = dxo.shape
    tm = min(tm, T)

    sub = min(512, tm)

    def body(d_ref, w_ref, gu_ref, o_ref):
        w = w_ref[...]
        for r in range(tm // sub):
            rows = slice(r * sub, (r + 1) * sub)
            d = lax.dot_general(d_ref[rows, :].astype(BF16), w, _NT, preferred_element_type=F32)
            gate = gu_ref[rows, :FF_PAD].astype(F32)
            up = gu_ref[rows, FF_PAD:].astype(F32)
            sig = _sigmoid(gate)
            silu = gate * sig
            o_ref[rows, :FF_PAD] = (d * up * (sig + silu * (1.0 - sig))).astype(BF16)
            o_ref[rows, FF_PAD:] = (d * silu).astype(BF16)

    return _call(
        body, (dxo, w, gu), grid=(T // tm, N_FF_CHUNK), name=name, items=items,
        in_specs=[pl.BlockSpec((tm, K), lambda i, j: (i, 0)), pl.BlockSpec((None, FF_PAD, K), lambda i, j: (0, j, 0)),
                  pl.BlockSpec((tm, 2 * FF_PAD), lambda i, j: (i, j))],
        out_specs=[pl.BlockSpec((tm, 2 * FF_PAD), lambda i, j: (i, j))],
        out_shape=[jax.ShapeDtypeStruct((T, 2 * D_FF_PAD), BF16)],
        sem=("parallel", "parallel"))


def _loss_head(x, g, target, name, tm=512):
    T, K = x.shape

    def loss_fn(xv, gv, tv):
        err = _rms(xv, gv) - tv
        return 0.5 * jnp.sum(jnp.mean(err * err, axis=-1, keepdims=True), axis=0, keepdims=True)

    def body(x_ref, g_ref, t_ref, l_ref, dx_ref, dg_ref):
        val, vjp = jax.vjp(lambda xv, gv: loss_fn(xv, gv, t_ref[...]), x_ref[...], g_ref[...])
        dx, dg = vjp(jnp.ones((1, 1), F32))
        dx_ref[...] = dx
        lval = jnp.broadcast_to(val, (1, 128))

        @pl.when(pl.program_id(0) == 0)
        def _():
            dg_ref[...] = dg
            l_ref[...] = lval

        @pl.when(pl.program_id(0) > 0)
        def _():
            dg_ref[...] += dg
            l_ref[...] += lval

    row = pl.BlockSpec((tm, K), lambda i: (i, 0))
    vec = pl.BlockSpec((1, K), lambda i: (0, 0))
    return pl.pallas_call(
        body, grid=(T // tm,), name=name,
        in_specs=[row, vec, row], out_specs=[pl.BlockSpec((1, 128), lambda i: (0, 0)), row, vec],
        out_shape=[jax.ShapeDtypeStruct((1, 128), F32), jax.ShapeDtypeStruct((T, K), F32),
                   jax.ShapeDtypeStruct((1, K), F32)],
        compiler_params=_cp("arbitrary"))(x, g, target)


SGU_ROWS = 4 * BLOCK


def _sgu_fn(u, v, wm, bt, gain):
    ug = jax.nn.gelu(u)
    vg = jax.nn.gelu(v)
    row = lax.broadcasted_iota(jnp.int32, (BLOCK, BLOCK), 0)
    col = lax.broadcasted_iota(jnp.int32, (BLOCK, BLOCK), 1)
    tri = row >= col
    normed = []
    for h in range(GROUP_HEADS):
        vh = vg[:, h * HEAD_DIM:(h + 1) * HEAD_DIM]
        xc = vh - jnp.mean(vh, axis=-1, keepdims=True)
        normed.append(xc * lax.rsqrt(jnp.mean(xc * xc, axis=-1, keepdims=True) + EPS))
    vn = jnp.concatenate(normed, axis=1)
    wcat = jnp.concatenate([jnp.where(tri, wm[h], 0.0) for h in range(GROUP_HEADS)], axis=1)
    bias = jnp.concatenate([jnp.broadcast_to(bt[:, h:h + 1], (BLOCK, HEAD_DIM)) for h in range(GROUP_HEADS)], axis=1)
    mixes = []
    for c in range(u.shape[0] // BLOCK):
        chunk = vn[c * BLOCK:(c + 1) * BLOCK]
        stacked = jnp.concatenate([jnp.where(_head_mask(h, chunk.shape), chunk, 0.0) for h in range(GROUP_HEADS)], axis=0)
        mixes.append(_bdot(wcat, stacked) + bias)
    return _rms(ug * jnp.concatenate(mixes, axis=0), gain)


def _sgu_fwd(proj, wm, bt, gain, name, items=()):
    T = proj.shape[0]
    rows = min(SGU_ROWS, T)

    def body(u_ref, v_ref, w_ref, b_ref, g_ref, o_ref):
        o_ref[...] = _sgu_fn(u_ref[...], v_ref[...], w_ref[...], b_ref[...], g_ref[...]).astype(BF16)

    full = lambda shape: pl.BlockSpec(shape, lambda i: (0,) * len(shape))
    return _call(
        body, (proj, proj, wm, bt, gain), grid=(T // rows,), name=name, items=items,
        in_specs=[pl.BlockSpec((rows, GROUP_WIDTH), lambda i: (i, 0)), pl.BlockSpec((rows, GROUP_WIDTH), lambda i: (i, 1)),
                  full(wm.shape), full(bt.shape), full(gain.shape)],
        out_specs=[pl.BlockSpec((rows, GROUP_WIDTH), lambda i: (i, 0))],
        out_shape=[jax.ShapeDtypeStruct((T, GROUP_WIDTH), BF16)],
        sem=("parallel",))


def _acc_out(first, refs, vals):
    @pl.when(first)
    def _():
        for r, v in zip(refs, vals):
            r[...] = v

    @pl.when(jnp.logical_not(first))
    def _():
        for r, v in zip(refs, vals):
            r[...] += v


def _sgu_bwd(proj, wm, bt, gain, dy, name, items=()):
    T = proj.shape[0]

    def body(u_ref, v_ref, w_ref, b_ref, g_ref, dy_ref, duv_ref, dw_ref, db_ref, dg_ref):
        _, vjp = jax.vjp(_sgu_fn, u_ref[...], v_ref[...], w_ref[...], b_ref[...], g_ref[...])
        du, dv, dw, db, dg = vjp(dy_ref[...])
        duv_ref[:, :GROUP_WIDTH] = du.astype(BF16)
        duv_ref[:, GROUP_WIDTH:] = dv.astype(BF16)
        _acc_out(pl.program_id(0) == 0, (dw_ref, db_ref, dg_ref), (dw, db, dg))

    full = lambda shape: pl.BlockSpec(shape, lambda i: (0,) * len(shape))
    rows = min(SGU_ROWS, T)
    return _call(
        body, (proj, proj, wm, bt, gain, dy), grid=(T // rows,), name=name, items=items,
        in_specs=[pl.BlockSpec((rows, GROUP_WIDTH), lambda i: (i, 0)), pl.BlockSpec((rows, GROUP_WIDTH), lambda i: (i, 1)),
                  full(wm.shape), full(bt.shape), full(gain.shape),
                  pl.BlockSpec((rows, GROUP_WIDTH), lambda i: (i, 0))],
        out_specs=[pl.BlockSpec((rows, 2 * GROUP_WIDTH), lambda i: (i, 0)), full(wm.shape), full(bt.shape), full(gain.shape)],
        out_shape=[jax.ShapeDtypeStruct((T, 2 * GROUP_WIDTH), BF16), jax.ShapeDtypeStruct(wm.shape, F32),
                   jax.ShapeDtypeStruct(bt.shape, F32), jax.ShapeDtypeStruct(gain.shape, F32)],
        sem=("arbitrary",))


def _pool_consts(seq):
    t = lax.broadcasted_iota(jnp.int32, (seq, GROUP_WIDTH), 0)
    grp = lax.broadcasted_iota(jnp.int32, (seq, GROUP_WIDTH), 1) // (GROUP_WIDTH // len(POOL_WINDOWS))
    win = jnp.where(grp == 0, POOL_WINDOWS[0], jnp.where(grp == 1, POOL_WINDOWS[1],
                    jnp.where(grp == 2, POOL_WINDOWS[2], POOL_WINDOWS[3])))
    count = jnp.minimum(t + 1, win).astype(F32)
    return t, grp, count


def _by_group(grp, vals):
    return jnp.where(grp == 0, vals[0], jnp.where(grp == 1, vals[1], jnp.where(grp == 2, vals[2], vals[3])))


def _window_sums(x, t, seq, back):
    def shift(a, k):
        if back:
            return jnp.where(t >= k, pltpu.roll(a, k, 0), 0.0)
        return jnp.where(t < seq - k, pltpu.roll(a, seq - k, 0), 0.0)
    sums = []
    a, k = x, 1
    for _ in POOL_WINDOWS:
        a = a + shift(a, k)
        sums.append(a)
        k *= 2
    return sums


def _pool_tail(y, wbd, scale, gain):
    return _rms(_bdot(y, wbd) * scale, gain)


def _pool_fwd(proj, wbd, scale, gain, seq, name):
    T = proj.shape[0]

    def body(p_ref, w_ref, s_ref, g_ref, o_ref):
        p = p_ref[...]
        t, grp, count = _pool_consts(seq)
        y = _by_group(grp, _window_sums(p, t, seq, True)) / count - p
        o_ref[...] = _pool_tail(y, w_ref[...], s_ref[...], g_ref[...]).astype(BF16)

    full = lambda shape: pl.BlockSpec(shape, lambda b: (0,) * len(shape))
    return pl.pallas_call(
        body, grid=(T // seq,), name=name,
        in_specs=[pl.BlockSpec((seq, GROUP_WIDTH), lambda b: (b, 2)), full(wbd.shape), full(scale.shape), full(gain.shape)],
        out_specs=pl.BlockSpec((seq, GROUP_WIDTH), lambda b: (b, 0)),
        out_shape=jax.ShapeDtypeStruct((T, GROUP_WIDTH), BF16),
        compiler_params=_cp("parallel"))(proj, wbd, scale, gain)


def _pool_bwd(proj, wbd, scale, gain, dy, seq, name):
    T = proj.shape[0]

    def body(p_ref, w_ref, s_ref, g_ref, dy_ref, dp_ref, dw_ref, ds_ref, dg_ref):
        p = p_ref[...]
        t, grp, count = _pool_consts(seq)
        y = _by_group(grp, _window_sums(p, t, seq, True)) / count - p
        _, vjp = jax.vjp(_pool_tail, y, w_ref[...], s_ref[...], g_ref[...])
        d_y, dw, ds, dg = vjp(dy_ref[...])
        dp = _by_group(grp, _window_sums(d_y / count, t, seq, False)) - d_y
        dp_ref[...] = dp.astype(BF16)
        _acc_out(pl.program_id(0) == 0, (dw_ref, ds_ref, dg_ref), (dw, ds, dg))

    full = lambda shape: pl.BlockSpec(shape, lambda b: (0,) * len(shape))
    return pl.pallas_call(
        body, grid=(T // seq,), name=name,
        in_specs=[pl.BlockSpec((seq, GROUP_WIDTH), lambda b: (b, 2)), full(wbd.shape), full(scale.shape), full(gain.shape),
                  pl.BlockSpec((seq, GROUP_WIDTH), lambda b: (b, 1))],
        out_specs=[pl.BlockSpec((seq, GROUP_WIDTH), lambda b: (b, 0)), full(wbd.shape), full(scale.shape), full(gain.shape)],
        out_shape=[jax.ShapeDtypeStruct((T, GROUP_WIDTH), BF16), jax.ShapeDtypeStruct(wbd.shape, F32),
                   jax.ShapeDtypeStruct(scale.shape, F32), jax.ShapeDtypeStruct(gain.shape, F32)],
        compiler_params=_cp("arbitrary"))(proj, wbd, scale, gain, dy)


def _swa_fn(q, kv_prev, kv_cur, sinks, tab, gain, first):
    half = GROUP_WIDTH // 2
    k2 = jnp.concatenate([kv_prev[:, :half], kv_cur[:, :half]], axis=0)
    v2 = jnp.concatenate([kv_prev[:, half:], kv_cur[:, half:]], axis=0)
    per_query_head = lambda a: jnp.concatenate(
        [a[:, (h // 2) * HEAD_DIM:(h // 2 + 1) * HEAD_DIM] for h in range(GROUP_HEADS)], axis=1)
    qs = jnp.concatenate([jnp.where(_head_mask(h, q.shape), q, 0.0) for h in range(GROUP_HEADS)], axis=0)
    qi = lax.broadcasted_iota(jnp.int32, (HEAD_ROWS, 2 * BLOCK), 0) & (BLOCK - 1)
    kj = lax.broadcasted_iota(jnp.int32, (HEAD_ROWS, 2 * BLOCK), 1)
    dist = qi + BLOCK - kj
    mask = (dist >= 0) & (dist < BLOCK) & ((kj >= BLOCK) | jnp.logical_not(first))
    logits = _bdot_nt(qs, per_query_head(k2)) * ATT_SCALE + tab.reshape(HEAD_ROWS, 2 * BLOCK)
    logits = jnp.where(mask, logits, -1e30)
    sink = jnp.concatenate([jnp.broadcast_to(sinks[:, h:h + 1], (BLOCK, 1)) for h in range(GROUP_HEADS)], axis=0)
    m = lax.stop_gradient(jnp.maximum(jnp.max(logits, axis=1, keepdims=True), sink))
    p = jnp.exp(logits - m)
    probs = p / (jnp.sum(p, axis=1, keepdims=True) + jnp.exp(sink - m))
    out = _bdot(probs, per_query_head(v2))
    y = jnp.zeros_like(q)
    for h in range(GROUP_HEADS):
        y = y + jnp.where(_head_mask(h, q.shape), out[h * BLOCK:(h + 1) * BLOCK], 0.0)
    return _rms(y, gain)


def _swa_specs(nblk):
    q = pl.BlockSpec((BLOCK, GROUP_WIDTH), lambda b, i: (b * nblk + i, 3))
    cur = pl.BlockSpec((BLOCK, GROUP_WIDTH), lambda b, i: (b * nblk + i, 4))
    prev = pl.BlockSpec((BLOCK, GROUP_WIDTH), lambda b, i: (b * nblk + jnp.maximum(i - 1, 0), 4))
    return q, prev, cur


def _swa_fwd(proj, sinks, tab, gain, seq, name, items=()):
    T = proj.shape[0]
    nblk = seq // BLOCK

    def body(q_ref, kp_ref, kc_ref, s_ref, t_ref, g_ref, o_ref):
        o_ref[...] = _swa_fn(q_ref[...], kp_ref[...], kc_ref[...], s_ref[...], t_ref[...], g_ref[...],
                             pl.program_id(1) == 0).astype(BF16)

    full = lambda shape: pl.BlockSpec(shape, lambda b, i: (0,) * len(shape))
    return _call(
        body, (proj, proj, proj, sinks, tab, gain), grid=(T // seq, nblk), name=name, items=items,
        in_specs=[*_swa_specs(nblk), full(sinks.shape), full(tab.shape), full(gain.shape)],
        out_specs=[pl.BlockSpec((BLOCK, GROUP_WIDTH), lambda b, i: (b * nblk + i, 0))],
        out_shape=[jax.ShapeDtypeStruct((T, GROUP_WIDTH), BF16)],
        sem=("parallel", "parallel"))


def _swa_bwd(proj, sinks, tab, gain, dy, seq, name, items=()):
    T = proj.shape[0]
    nblk = seq // BLOCK

    def body(q_ref, kp_ref, kc_ref, s_ref, t_ref, g_ref, dy_ref, dq_ref, dkv_ref, ds_ref, dt_ref, dg_ref):
        i = pl.program_id(1)
        first = i == 0
        fn = functools.partial(_swa_fn, first=first)
        _, vjp = jax.vjp(fn, q_ref[...], kp_ref[...], kc_ref[...], s_ref[...], t_ref[...], g_ref[...])
        dq, dkp, dkc, ds, dt, dg = vjp(dy_ref[...])
        dq_ref[...] = dq.astype(BF16)
        dkv_ref[pl.ds(pl.multiple_of(i * BLOCK, BLOCK), BLOCK), :] = dkc

        @pl.when(i > 0)
        def _():
            dkv_ref[pl.ds(pl.multiple_of((i - 1) * BLOCK, BLOCK), BLOCK), :] += dkp

        _acc_out((pl.program_id(0) == 0) & first, (ds_ref, dt_ref, dg_ref), (ds, dt, dg))

    full = lambda shape: pl.BlockSpec(shape, lambda b, i: (0,) * len(shape))
    blk = lambda c: pl.BlockSpec((BLOCK, GROUP_WIDTH), lambda b, i: (b * nblk + i, c))
    return _call(
        body, (proj, proj, proj, sinks, tab, gain, dy), grid=(T // seq, nblk), name=name, items=items,
        in_specs=[*_swa_specs(nblk), full(sinks.shape), full(tab.shape), full(gain.shape), blk(2)],
        out_specs=[blk(0), pl.BlockSpec((seq, GROUP_WIDTH), lambda b, i: (b, 0)), full(sinks.shape), full(tab.shape),
                   full(gain.shape)],
        out_shape=[jax.ShapeDtypeStruct((T, GROUP_WIDTH), BF16), jax.ShapeDtypeStruct((T, GROUP_WIDTH), F32),
                   jax.ShapeDtypeStruct(sinks.shape, F32), jax.ShapeDtypeStruct(tab.shape, F32),
                   jax.ShapeDtypeStruct(gain.shape, F32)],
        sem=("arbitrary", "arbitrary"))


def _t5_bucket(dist):
    max_exact = N_BUCKETS // 2
    df = jnp.maximum(dist, 1).astype(F32)
    large = max_exact + (jnp.log(df / max_exact) / jnp.log(jnp.float32(MAX_DISTANCE / max_exact))
                         * (N_BUCKETS - max_exact)).astype(jnp.int32)
    return jnp.where(dist < max_exact, dist, jnp.minimum(large, N_BUCKETS - 1))


def _bucket_map():
    dist = (jnp.arange(BLOCK)[:, None] + BLOCK) - jnp.arange(2 * BLOCK)[None, :]
    return _t5_bucket(jnp.clip(dist, 0, BLOCK - 1)).astype(jnp.int32)


def _bias_table(rel_bias, buckets, name, items=()):
    def body(rb_ref, bk_ref, o_ref):
        bk = bk_ref[...]
        rb = rb_ref[...]
        for h in range(GROUP_HEADS):
            acc = jnp.zeros((BLOCK, 2 * BLOCK), F32)
            for b in range(N_BUCKETS):
                acc = jnp.where(bk == b, rb[b:b + 1, h:h + 1], acc)
            o_ref[h] = acc

    full = lambda shape: pl.BlockSpec(shape, lambda i: (0,) * len(shape))
    shape = (GROUP_HEADS, BLOCK, 2 * BLOCK)
    return _call(body, (rel_bias, buckets), grid=(1,), name=name, items=items,
                 in_specs=[full(rel_bias.shape), full(buckets.shape)], out_specs=[full(shape)],
                 out_shape=[jax.ShapeDtypeStruct(shape, F32)], sem=("arbitrary",))


def _bias_table_bwd(dtab, buckets, name):
    def body(dt_ref, bk_ref, o_ref):
        bk = bk_ref[...]
        row = lax.broadcasted_iota(jnp.int32, (N_BUCKETS, GROUP_HEADS), 0)
        col = lax.broadcasted_iota(jnp.int32, (N_BUCKETS, GROUP_HEADS), 1)
        acc = jnp.zeros((N_BUCKETS, GROUP_HEADS), F32)
        for h in range(GROUP_HEADS):
            dt = dt_ref[h]
            for b in range(N_BUCKETS):
                s = jnp.sum(jnp.where(bk == b, dt, 0.0), keepdims=True)
                acc = acc + jnp.where((row == b) & (col == h), s, 0.0)
        o_ref[...] = acc

    return pl.pallas_call(body, name=name, out_shape=jax.ShapeDtypeStruct((N_BUCKETS, GROUP_HEADS), F32),
                          compiler_params=_cp())(dtab, buckets)


HEAD_ROWS = GROUP_HEADS * BLOCK


def _stack_heads(x):
    return jnp.concatenate([jnp.where(_head_mask(h, x.shape), x, 0.0) for h in range(GROUP_HEADS)], axis=0).astype(BF16)


def _sb_tile(qs, kb, q0, k0):
    z = lax.dot_general(qs, kb, _NT, preferred_element_type=F32)
    row = lax.broadcasted_iota(jnp.int32, (HEAD_ROWS, BLOCK), 0) & (BLOCK - 1)
    col = lax.broadcasted_iota(jnp.int32, (HEAD_ROWS, BLOCK), 1)
    causal = (k0 + col) < (q0 + row)
    ls_neg = -(jnp.maximum(z, 0.0) + jnp.log(1.0 + jnp.exp(-jnp.abs(z))))
    return jnp.where(causal, ls_neg, 0.0), ls_neg + z, causal


SB_DEAD = -70.0
SB_FIRST_LANE = GROUP_HEADS


def _tri(strict_upper_src):
    r = lax.broadcasted_iota(jnp.int32, (BLOCK, BLOCK), 0)
    c = lax.broadcasted_iota(jnp.int32, (BLOCK, BLOCK), 1)
    cond = {"gt": r > c, "le": r <= c, "lt": r < c}[strict_upper_src]
    return jnp.where(cond, 1.0, 0.0).astype(BF16)


def _sb_fwd(proj, gain, seq, name, items=()):
    T = proj.shape[0]
    nblk = seq // BLOCK

    def body(q_ref, k_ref, v_ref, g_ref, o_ref, raw_ref, bt_ref):
        i = pl.program_id(1)
        q = q_ref[...]
        u_gt = _tri("gt")
        lane = lax.broadcasted_iota(jnp.int32, (BLOCK, BLOCK), 1)
        heads = [slice(h * HEAD_DIM, (h + 1) * HEAD_DIM) for h in range(GROUP_HEADS)]
        rows = [slice(h * BLOCK, (h + 1) * BLOCK) for h in range(GROUP_HEADS)]
        qs = _stack_heads(q * ATT_SCALE)

        def live(carry):
            j, _, cb = carry
            return (j >= 0) & (jnp.max(cb) > SB_DEAD)

        def step(carry):
            j, accs, cb = carry
            ks = pl.multiple_of(j * BLOCK, BLOCK)
            kb = k_ref[pl.ds(ks, BLOCK), :].astype(BF16)
            vb = v_ref[pl.ds(ks, BLOCK), :].astype(BF16)
            b, a, causal = _sb_tile(qs, kb, i * BLOCK, j * BLOCK)
            tail = _split_dot(b, u_gt) + cb
            w = jnp.where(causal, jnp.exp(a + tail), 0.0).astype(BF16)
            accs = tuple(accs[h] + jnp.dot(w[rows[h]], vb[:, hs], preferred_element_type=F32)
                         for h, hs in enumerate(heads))
            return j - 1, accs, cb + jnp.sum(b, axis=1, keepdims=True)

        zero_acc = tuple(jnp.zeros((BLOCK, HEAD_DIM), F32) for _ in heads)
        j_end, accs, cb = lax.while_loop(live, step, (i, zero_acc, jnp.zeros((HEAD_ROWS, 1), F32)))
        side = jnp.where(lane == SB_FIRST_LANE, (j_end + 1).astype(F32), 0.0)
        for h in range(GROUP_HEADS):
            side = jnp.where(lane == h, cb[rows[h]], side)
        raw = jnp.concatenate(accs, axis=1)
        raw_ref[...] = raw
        bt_ref[...] = side
        o_ref[...] = _rms(raw, g_ref[...]).astype(BF16)

    return _call(
        body, (proj, proj, proj, gain), grid=(T // seq, nblk), name=name, items=items,
        in_specs=[pl.BlockSpec((BLOCK, GROUP_WIDTH), lambda b, i: (b * nblk + i, 5)),
                  pl.BlockSpec((seq, GROUP_WIDTH), lambda b, i: (b, 6)),
                  pl.BlockSpec((seq, GROUP_WIDTH), lambda b, i: (b, 7)),
                  pl.BlockSpec(gain.shape, lambda b, i: (0, 0))],
        out_specs=[pl.BlockSpec((BLOCK, GROUP_WIDTH), lambda b, i: (b * nblk + i, 0)),
                   pl.BlockSpec((BLOCK, GROUP_WIDTH), lambda b, i: (b * nblk + i, 0)),
                   pl.BlockSpec((BLOCK, BLOCK), lambda b, i: (b * nblk + i, 0))],
        out_shape=[jax.ShapeDtypeStruct((T, GROUP_WIDTH), BF16), jax.ShapeDtypeStruct((T, GROUP_WIDTH), F32),
                   jax.ShapeDtypeStruct((T, BLOCK), F32)],
        sem=("parallel", "parallel"))


def _sb_bwd(proj, gain, raw, btot, dy, seq, name, items=()):
    T = proj.shape[0]
    nblk = seq // BLOCK

    def body(q_ref, k_ref, v_ref, g_ref, raw_ref, bt_ref, dy_ref, dq_ref, dk_ref, dv_ref, dg_ref):
        i = pl.program_id(1)

        @pl.when(i == 0)
        def _():
            dk_ref[...] = jnp.zeros_like(dk_ref)
            dv_ref[...] = jnp.zeros_like(dv_ref)

        rawv = raw_ref[...]
        _, vjp = jax.vjp(_rms, rawv, g_ref[...])
        do, dg = vjp(dy_ref[...])
        _acc_out((pl.program_id(0) == 0) & (i == 0), (dg_ref,), (dg,))
        q = q_ref[...]
        bt = bt_ref[...]
        u_le = _tri("le")
        u_lt = _tri("lt")
        heads = [slice(h * HEAD_DIM, (h + 1) * HEAD_DIM) for h in range(GROUP_HEADS)]
        rows = [slice(h * BLOCK, (h + 1) * BLOCK) for h in range(GROUP_HEADS)]
        qs = _stack_heads(q * ATT_SCALE)
        dos = _stack_heads(do)
        bts = jnp.concatenate([bt[:, h:h + 1] for h in range(GROUP_HEADS)], axis=0)
        first = jnp.max(bt[:, SB_FIRST_LANE:SB_FIRST_LANE + 1]).astype(jnp.int32)
        first = jnp.minimum(jnp.maximum(first, 0), i)

        def step(j, carry):
            dqs, cb, cg = carry
            ks = pl.multiple_of(j * BLOCK, BLOCK)
            kb = k_ref[pl.ds(ks, BLOCK), :].astype(BF16)
            vb = v_ref[pl.ds(ks, BLOCK), :].astype(BF16)
            b, a, causal = _sb_tile(qs, kb, i * BLOCK, j * BLOCK)
            tail = bts - (_split_dot(b, u_le) + cb)
            w = jnp.where(causal, jnp.exp(a + tail), 0.0)
            sig = jnp.exp(a)
            g = w * lax.dot_general(dos, vb, _NT, preferred_element_type=F32)
            gpre = _split_dot(g, u_lt) + cg
            dz = jnp.where(causal, g * (1.0 - sig) - gpre * sig, 0.0).astype(BF16)
            dqs = tuple(dqs[h] + jnp.dot(dz[rows[h]], kb[:, hs], preferred_element_type=F32)
                        for h, hs in enumerate(heads))
            dk_ref[pl.ds(ks, BLOCK), :] += lax.dot_general(dz, qs, _TN, preferred_element_type=F32)
            dv_ref[pl.ds(ks, BLOCK), :] += lax.dot_general(w.astype(BF16), dos, _TN, preferred_element_type=F32)
            return dqs, cb + jnp.sum(b, axis=1, keepdims=True), cg + jnp.sum(g, axis=1, keepdims=True)

        zero_dq = tuple(jnp.zeros((BLOCK, HEAD_DIM), F32) for _ in heads)
        zero = jnp.zeros((HEAD_ROWS, 1), F32)
        dqs, _, _ = lax.fori_loop(first, i + 1, step, (zero_dq, zero, zero))
        dq_ref[...] = (jnp.concatenate(dqs, axis=1) * ATT_SCALE).astype(BF16)

    blk = lambda c: pl.BlockSpec((BLOCK, GROUP_WIDTH), lambda b, i: (b * nblk + i, c))
    seqblk = lambda c: pl.BlockSpec((seq, GROUP_WIDTH), lambda b, i: (b, c))
    vec = pl.BlockSpec(gain.shape, lambda b, i: (0, 0))
    return _call(
        body, (proj, proj, proj, gain, raw, btot, dy), grid=(T // seq, nblk), name=name, items=items,
        in_specs=[blk(5), seqblk(6), seqblk(7), vec, blk(0), pl.BlockSpec((BLOCK, BLOCK), lambda b, i: (b * nblk + i, 0)),
                  blk(3)],
        out_specs=[blk(0), seqblk(0), seqblk(0), vec],
        out_shape=[jax.ShapeDtypeStruct((T, GROUP_WIDTH), BF16), jax.ShapeDtypeStruct((T, GROUP_WIDTH), F32),
                   jax.ShapeDtypeStruct((T, GROUP_WIDTH), F32), jax.ShapeDtypeStruct(gain.shape, F32)],
        sem=("arbitrary", "arbitrary"))


def _layer_params(l, sgu_w, sgu_b, pool_w, pool_scale, swa_sinks, mix_out_gain, norm_mix, norm_ffn):
    gains = mix_out_gain[l].reshape(4, 1, GROUP_WIDTH)
    return dict(
        wm=sgu_w[l], bt=sgu_b[l].T,
        wbd=jax.scipy.linalg.block_diag(*[pool_w[l, g] for g in range(len(POOL_WINDOWS))]),
        scale=pool_scale[l][None], sinks=swa_sinks[l][None],
        gain=[gains[m] for m in range(4)], norm_mix=norm_mix[l][None], norm_ffn=norm_ffn[l][None])


def _as_weights(g_in, g_out, g_gu, g_down):
    return (g_in[None], g_out.reshape(1, D_MODEL, D_MODEL), g_gu[None], g_down.reshape(1, D_FF_PAD, D_MODEL))


def _gather_item(src, l, dst_shape, rows=None, down=False, init=None):
    r0, nr = rows or (0, src.shape[1])
    if down:
        place = lambda dst, k: dst.at[k // 2, pl.ds((k % 2) * DOWN_ROWS + r0, nr), :]
    else:
        place = lambda dst, k: dst.at[k, pl.ds(r0, nr), :]
    return _Item(src, lambda s, k: s.at[l, pl.ds(r0, nr), :], dst_shape, place, init, relayed=True)


WEIGHTS = ("in", "out", "gu", "down")
FWD_PLAN = {
    (0, "in_proj"): [(0, "down", None)], (0, "sgu"): [(0, "out", None)], (0, "swa"): [(0, "gu", 0)],
    (0, "sb"): [(0, "gu", 1)],
    (0, "ffn_up"): [(1, "gu", 0), (1, "in", None), (1, "out", None)], (0, "ffn_down"): [(1, "down", None)],
    (1, "in_proj"): [(1, "gu", 1)], (1, "swa"): [(2, "in", None), (2, "out", None)], (1, "sb"): [(2, "gu", 0)],
    (1, "ffn_up"): [(2, "gu", 1), (2, "down", None)],
    (2, "in_proj"): [(3, "down", None)], (2, "swa"): [(3, "in", None), (3, "out", None)], (2, "sb"): [(3, "gu", 0)],
    (2, "ffn_up"): [(3, "gu", 1)],
}


def _planned(pieces, shards, bufs):
    items = []
    for lyr, t, part in pieces:
        src = shards[WEIGHTS.index(t)]
        down = t == "down"
        shape = (4, FF_PAD, D_MODEL) if down else (N_DEV,) + src.shape[1:]
        init = bufs.get((lyr, t))
        if init is None and down:
            init = jnp.zeros(shape, BF16)
        rows = None if part is None else (part * (FF_PAD // 2), FF_PAD // 2)
        items.append(_gather_item(src, lyr, shape, rows, down, init))
    return items


def _layer_fwd(l, x, p, bufs, tab, seq, shards):
    def beside(host):
        return _planned(FWD_PLAN.get((l, host), ()), shards, bufs)

    def landed(host, results):
        for (lyr, t, _), r in zip(FWD_PLAN.get((l, host), ()), results):
            bufs[(lyr, t)] = r

    proj, h1, *got = _norm_mm(x, p["norm_mix"], bufs[(l, "in")][None], 0, 4, F32, f"in_proj_{l}",
                              items=beside("in_proj"))
    landed("in_proj", got)
    ya, *got = _sgu_fwd(proj, p["wm"], p["bt"], p["gain"][0], f"sgu_fwd_{l}", items=beside("sgu"))
    landed("sgu", got)
    yb = _pool_fwd(proj, p["wbd"], p["scale"], p["gain"][1], seq, f"pool_fwd_{l}")
    yc, *got = _swa_fwd(proj, p["sinks"], tab, p["gain"][2], seq, f"swa_fwd_{l}", items=beside("swa"))
    landed("swa", got)
    yd, raw, btot, *got = _sb_fwd(proj, p["gain"][3], seq, f"sb_fwd_{l}", items=beside("sb"))
    landed("sb", got)
    ycat = jnp.concatenate([ya, yb, yc, yd], axis=1)
    xm, = _mm_res(x, ycat, bufs[(l, "out")].reshape(1, D_MODEL, D_MODEL), 0, f"out_proj_{l}", tn=D_MODEL)
    gu, act, h2, *got = _ffn_up_act(xm, p["norm_ffn"], bufs[(l, "gu")][None], f"ffn_up_{l}", items=beside("ffn_up"))
    landed("ffn_up", got)
    xo, *got = _mm_res(xm, act, bufs[(l, "down")].reshape(1, D_FF_PAD, D_MODEL), 0, f"ffn_down_{l}",
                       tn=D_MODEL // 2, items=beside("ffn_down"))
    landed("ffn_down", got)
    return xo, (x, proj, h1, ycat, raw, btot, xm, gu, h2, act)


def _rows_item(g, r0, nr, init=None):
    cut = lambda a, k: a.at[k, pl.ds(r0, nr), :]
    return _Item(g, cut, g.shape, cut, init)


def _layer_bwd(l, dxo, saved, p, w, tab, seq, pending, ride):
    win, wout, wgu, wd = w
    x, proj, h1, ycat, raw, btot, xm, gu, h2, act = saved
    out_rows = D_MODEL // N_DEV
    in_half = D_MODEL // 2
    pieces = 2 if ride else 3
    piece = FF_PAD // pieces

    above = [_rows_item(pending[0], in_half, in_half, pending[1])] if pending else []
    dgu, *p_in_above = _ffn_down_dx(dxo, wd, gu, f"ffn_down_dx_{l}", items=above)
    g_wd, = _mm_tn_rows(act, dxo, FF_PAD, f"ffn_down_dw_{l}")
    send_down = _Item(g_wd, lambda src, k: src.at[pl.ds((k // 2) * FF_PAD + (k % 2) * DOWN_ROWS, DOWN_ROWS), :],
                      (N_DEV, DOWN_ROWS, D_MODEL))
    g_wgu, p_down = _mm_tn_rows(dgu, h2, FF_PAD, f"ffn_up_dw_{l}", items=(send_down,), block_of=_ffn_chunk_block)
    g_wgu = g_wgu.reshape(N_DEV, FF_PAD, D_MODEL)
    dxm, g_norm_ffn, p_gu = _mm_norm_bwd(dgu, wgu, 0, 1, xm, p["norm_ffn"], dxo, f"ffn_up_dx_{l}",
                                         items=(_rows_item(g_wgu, 0, piece),), block_of=_ffn_chunk_block,
                                         transposed=True)
    dycat = _mm_nt(dxm, wout, 0, F32, f"out_proj_dx_{l}")
    g_wout, = _mm_tn_rows(ycat, dxm, D_MODEL, f"out_proj_dw_{l}")
    send_out = _Item(g_wout, lambda src, k: src.at[pl.ds(k * out_rows, out_rows), :], (N_DEV, out_rows, D_MODEL))
    duv, g_wm, g_bt, g_ga = _sgu_bwd(proj, p["wm"], p["bt"], p["gain"][0], dycat, f"sgu_bwd_{l}")
    dp, g_wbd, g_scale, g_gb = _pool_bwd(proj, p["wbd"], p["scale"], p["gain"][1], dycat, seq, f"pool_bwd_{l}")
    beside_swa = list(ride) if ride else [_rows_item(g_wgu, piece, piece, p_gu)]
    dq, dkv, g_sinks, g_tab, g_gc, *rode = _swa_bwd(proj, p["sinks"], tab, p["gain"][2], dycat, seq,
                                                    f"swa_bwd_{l}", items=beside_swa)
    if not ride:
        p_gu, rode = rode[0], []
    dqd, dkd, dvd, g_gd, p_gu, p_out = _sb_bwd(
        proj, p["gain"][3], raw, btot, dycat, seq, f"sb_bwd_{l}",
        items=(_rows_item(g_wgu, (pieces - 1) * piece, piece, p_gu), send_out))
    dproj = jnp.concatenate([duv, dp, dq, dkv.astype(BF16), dqd, dkd.astype(BF16), dvd.astype(BF16)], axis=1)
    g_win, = _mm_tn_cols(h1, dproj, GROUP_WIDTH, 4, f"in_proj_dw_{l}")
    defer = l > 0
    dx, g_norm_mix, p_in = _mm_norm_bwd(dproj, win, 0, 4, x, p["norm_mix"], dxm, f"in_proj_dx_{l}",
                                        items=(_rows_item(g_win, 0, in_half if defer else D_MODEL),))
    pending = (g_win, p_in) if defer else None
    ng = len(POOL_WINDOWS)
    gd = GROUP_WIDTH // ng
    small = dict(
        sgu_w=g_wm, sgu_b=g_bt.T,
        pool_w=jnp.stack([g_wbd[g * gd:(g + 1) * gd, g * gd:(g + 1) * gd] for g in range(ng)]),
        pool_scale=g_scale[0], swa_sinks=g_sinks[0],
        mix_out_gain=jnp.concatenate([g_ga[0], g_gb[0], g_gc[0], g_gd[0]]),
        norm_mix=g_norm_mix[0], norm_ffn=g_norm_ffn[0])
    return dx, [p_in, p_out, p_gu, p_down], small, g_tab, rode, p_in_above, pending


def _local_step(x, target, shards, sgu_w, sgu_b, pool_w, pool_scale, swa_sinks, rel_bias, mix_out_gain,
                norm_mix, norm_ffn, norm_final, seq):
    buckets = _bucket_map()
    bufs = {}
    first = [(0, "in", None)]
    tab, *got = _bias_table(rel_bias, buckets, "bias_table", items=_planned(first, shards, bufs))
    bufs.update({(lyr, t): r for (lyr, t, _), r in zip(first, got)})
    params = [_layer_params(l, sgu_w, sgu_b, pool_w, pool_scale, swa_sinks, mix_out_gain, norm_mix, norm_ffn)
              for l in range(DEPTH)]
    saved, weights = [], []
    for l in range(DEPTH):
        x, s = _layer_fwd(l, x, params[l], bufs, tab, seq, shards)
        saved.append(s)
        weights.append(_as_weights(*[bufs[(l, t)] for t in WEIGHTS]))
    loss, dx, g_final = _loss_head(x, norm_final[None], target, "loss_head")
    big, small, g_tab, upper, pending = [None] * DEPTH, [None] * DEPTH, None, None, None
    for l in reversed(range(DEPTH)):
        ride = ()
        if l == 0:
            mine = _pack(_upper_layers(small), _pack_rows(_upper_layers(small)))
            ride = (_Item(mine, lambda src, k: src, (N_DEV,) + mine.shape),)
        dx, big[l], small[l], t, rode, done_above, pending = _layer_bwd(
            l, dx, saved[l], params[l], weights[l], tab, seq, pending, ride)
        g_tab = t if g_tab is None else g_tab + t
        upper = rode[0] if rode else upper
        if done_above:
            big[l + 1][0] = done_above[0]
    return loss, dx, big, small, _bias_table_bwd(g_tab, buckets, "bias_table_bwd"), g_final[0], upper


def _cast_pad(w, rows, name):
    L, r, c = w.shape

    def body(w_ref, o_ref):
        if rows != r:
            o_ref[...] = jnp.zeros_like(o_ref)
        o_ref[:r, :] = w_ref[...].astype(BF16)

    return pl.pallas_call(
        body, grid=(L,), name=name,
        in_specs=[pl.BlockSpec((None, r, c), lambda l: (l, 0, 0))],
        out_specs=pl.BlockSpec((None, rows, c), lambda l: (l, 0, 0)),
        out_shape=jax.ShapeDtypeStruct((L, rows, c), BF16),
        compiler_params=_cp("parallel"))(w)


def _adamw(w, g, m, v):
    m = ADAM_B1 * m + (1.0 - ADAM_B1) * g
    v = ADAM_B2 * v + (1.0 - ADAM_B2) * jnp.square(g)
    m_hat = m / (1.0 - ADAM_B1 ** ADAM_STEP)
    v_hat = v / (1.0 - ADAM_B2 ** ADAM_STEP)
    delta = -ADAM_LR * (m_hat / (jnp.sqrt(v_hat) + ADAM_EPS) + ADAM_WD * w)
    return delta, m, v


def _adamw_sharded(parts, w, m, v, tr, name, items=()):
    L, r, c = w.shape
    cp = parts[0].shape[-1]
    nrow = r // tr

    def body(*refs):
        p_refs, (w_ref, m_ref, v_ref, g_ref, d_ref, nm_ref, nv_ref) = refs[:L], refs[L:]
        for k in range(L):
            @pl.when(pl.program_id(0) == k)
            def _(p_ref=p_refs[k]):
                g = p_ref[0, :, :c].astype(F32)
                for dev in range(1, N_DEV):
                    g = g + p_ref[dev, :, :c].astype(F32)
                delta, nm, nv = _adamw(w_ref[...], g, m_ref[...], v_ref[...])
                g_ref[...] = g
                d_ref[...] = delta
                nm_ref[...] = nm
                nv_ref[...] = nv

    def part_spec(k):
        return pl.BlockSpec((N_DEV, tr, cp),
                            lambda l, i: (0, jnp.where(l == k, i, jnp.where(l < k, 0, nrow - 1)), 0))

    blk = pl.BlockSpec((None, tr, c), lambda l, i: (l, i, 0))
    out = jax.ShapeDtypeStruct((L, r, c), F32)
    return _call(
        body, (*parts, w, m, v), grid=(L, nrow), name=name, items=items,
        in_specs=[part_spec(k) for k in range(L)] + [blk, blk, blk],
        out_specs=[blk] * 4, out_shape=[out] * 4, sem=("arbitrary", "arbitrary"))


def _adamw_small(parts, wmv, name):
    def body(p_ref, wmv_ref, g_ref, d_ref, nm_ref, nv_ref):
        g = p_ref[0]
        for k in range(1, N_DEV):
            g = g + p_ref[k]
        delta, nm, nv = _adamw(wmv_ref[0], g, wmv_ref[1], wmv_ref[2])
        g_ref[...] = g
        d_ref[...] = delta
        nm_ref[...] = nm
        nv_ref[...] = nv

    out = jax.ShapeDtypeStruct(wmv.shape[1:], F32)
    return pl.pallas_call(body, name=name, out_shape=[out] * 4, compiler_params=_cp())(parts, wmv)


LAYERED = ("sgu_w", "sgu_b", "pool_w", "pool_scale", "swa_sinks", "mix_out_gain", "norm_mix", "norm_ffn")
SHARED = ("rel_bias", "norm_final")


def _seg_rows(a):
    return -(-a.size // 128)


def _pack_rows(parts):
    return -(-sum(_seg_rows(p) for p in parts) // 8) * 8


def _upper_layers(per_layer):
    if isinstance(per_layer, dict):
        return [per_layer[k][1:] for k in LAYERED]
    return [jnp.stack([per_layer[l][k] for l in range(1, DEPTH)]) for k in LAYERED]


def _layer_zero(stacked):
    return [stacked[k][:1] for k in LAYERED] + [stacked[k] for k in SHARED]


def _pack(parts, rows):
    segs = [jnp.pad(p.reshape(-1), (0, _seg_rows(p) * 128 - p.size)).reshape(_seg_rows(p), 128) for p in parts]
    used = sum(s.shape[0] for s in segs)
    return jnp.concatenate(segs + [jnp.zeros((rows - used, 128), F32)], axis=0)


def _pack_groups(groups, rows):
    segs = []
    for parts in groups:
        segs += [jnp.pad(p.reshape(-1), (0, _seg_rows(p) * 128 - p.size)).reshape(_seg_rows(p), 128) for p in parts]
        segs.append(jnp.zeros((rows - sum(_seg_rows(p) for p in parts), 128), F32))
    return jnp.concatenate(segs, axis=0).reshape(len(groups), rows, 128)


def _unpack(buf, like):
    out, at = [], 0
    for a in like:
        out.append(buf[at:at + _seg_rows(a)].reshape(-1)[:a.size].reshape(a.shape))
        at += _seg_rows(a)
    return out


def kernel(x, w_in, w_out, sgu_w, sgu_b, pool_w, pool_scale, swa_sinks, rel_bias, mix_out_gain, norm_mix, norm_ffn, w_gate_up, w_down, norm_final, loss_target, m_w_in, m_w_out, m_sgu_w, m_sgu_b, m_pool_w, m_pool_scale, m_swa_sinks, m_rel_bias, m_mix_out_gain, m_norm_mix, m_norm_ffn, m_w_gate_up, m_w_down, m_norm_final, v_w_in, v_w_out, v_sgu_w, v_sgu_b, v_pool_w, v_pool_scale, v_swa_sinks, v_rel_bias, v_mix_out_gain, v_norm_mix, v_norm_ffn, v_w_gate_up, v_w_down, v_norm_final):
    bl, seq, _ = x.shape
    L = w_in.shape[0]
    gu_t, m_gu_t, v_gu_t = (jnp.swapaxes(a, 1, 2) for a in (w_gate_up, m_w_gate_up, v_w_gate_up))
    shards = (_cast_pad(w_in, D_MODEL, "shard_w_in"), _cast_pad(w_out, D_MODEL // N_DEV, "shard_w_out"),
              _cast_pad(gu_t, FF_PAD, "shard_w_gate_up"), _cast_pad(w_down, DOWN_ROWS, "shard_w_down"))
    small_w = dict(sgu_w=sgu_w, sgu_b=sgu_b, pool_w=pool_w, pool_scale=pool_scale, swa_sinks=swa_sinks,
                   rel_bias=rel_bias, mix_out_gain=mix_out_gain, norm_mix=norm_mix, norm_ffn=norm_ffn,
                   norm_final=norm_final)
    small_m = dict(sgu_w=m_sgu_w, sgu_b=m_sgu_b, pool_w=m_pool_w, pool_scale=m_pool_scale, swa_sinks=m_swa_sinks,
                   rel_bias=m_rel_bias, mix_out_gain=m_mix_out_gain, norm_mix=m_norm_mix, norm_ffn=m_norm_ffn,
                   norm_final=m_norm_final)
    small_v = dict(sgu_w=v_sgu_w, sgu_b=v_sgu_b, pool_w=v_pool_w, pool_scale=v_pool_scale, swa_sinks=v_swa_sinks,
                   rel_bias=v_rel_bias, mix_out_gain=v_mix_out_gain, norm_mix=v_norm_mix, norm_ffn=v_norm_ffn,
                   norm_final=v_norm_final)
    loss, dx, big, small, g_rel_bias, g_final, upper = _local_step(
        x.reshape(bl * seq, D_MODEL), loss_target.reshape(bl * seq, D_MODEL), shards, sgu_w, sgu_b, pool_w,
        pool_scale, swa_sinks, rel_bias, mix_out_gain, norm_mix, norm_ffn, norm_final, seq)
    p_in, p_out, p_gu, p_down = ([big[l][t] for l in range(L)] for t in range(4))
    outs_gu = [jnp.swapaxes(a, 1, 2) for a in
               _adamw_sharded(p_gu, gu_t, m_gu_t, v_gu_t, FF_SHARD // 4, "adamw_w_gate_up")]
    outs_down = _adamw_sharded(p_down, w_down, m_w_down, v_w_down, DOWN_ROWS // 2, "adamw_w_down")
    outs_in = _adamw_sharded(p_in, w_in, m_w_in, v_w_in, 256, "adamw_w_in")
    lo_like = _layer_zero(small_w)
    lo_rows = _pack_rows(lo_like + [loss[0]])
    lo_mine = _pack([small[0][k][None] for k in LAYERED] + [g_rel_bias, g_final, loss[0]], lo_rows)
    *outs_out, lower = _adamw_sharded(p_out, w_out, m_w_out, v_w_out, D_MODEL // N_DEV, "adamw_w_out",
                                      items=(_Item(lo_mine, lambda src, k: src, (N_DEV,) + lo_mine.shape),))
    lo_res = _adamw_small(lower, _pack_groups([lo_like, _layer_zero(small_m), _layer_zero(small_v)], lo_rows),
                          "adamw_small_layer0")
    hi_like = _upper_layers(small_w)
    hi_rows = _pack_rows(hi_like)
    hi_res = _adamw_small(upper, _pack_groups([hi_like, _upper_layers(small_m), _upper_layers(small_v)], hi_rows),
                          "adamw_small_upper")
    loss_total = lo_res[0][sum(_seg_rows(a) for a in lo_like), 0]
    small_outs = []
    for lo_buf, hi_buf in zip(lo_res, hi_res):
        lo = dict(zip(LAYERED + SHARED, _unpack(lo_buf, lo_like)))
        hi = dict(zip(LAYERED, _unpack(hi_buf, hi_like)))
        small_outs.append({k: jnp.concatenate([lo[k], hi[k]], axis=0) if k in hi else lo[k] for k in lo})
    big_outs = dict(w_in=outs_in, w_out=outs_out, w_gate_up=outs_gu, w_down=outs_down)
    order = ("w_in", "w_out", "sgu_w", "sgu_b", "pool_w", "pool_scale", "swa_sinks", "rel_bias", "mix_out_gain",
             "norm_mix", "norm_ffn", "w_gate_up", "w_down", "norm_final")
    result = [loss_total, dx.reshape(bl, seq, D_MODEL)]
    for which in range(4):
        for name in order:
            result.append(big_outs[name][which] if name in big_outs else small_outs[which][name])
    return tuple(result)
```

```python
import functools

import jax
import jax.numpy as jnp
from jax import lax
from jax.experimental import pallas as pl
from jax.experimental.pallas import tpu as pltpu

F32 = jnp.float32
BF16 = jnp.bfloat16

N_DEV = 8
DEPTH = 4
D_MODEL = 1024
GROUP_WIDTH = 256
HEAD_DIM = 64
GROUP_HEADS = 4
BLOCK = 128
N_BUCKETS = 32
MAX_DISTANCE = 128
POOL_WINDOWS = (2, 4, 8, 16)
D_FF = 2816
FF_SHARD = D_FF // 4
FF_PAD = 768
D_FF_PAD = 4 * FF_PAD
EPS = 1e-6
ATT_SCALE = HEAD_DIM ** -0.5
ADAM_LR = 0.001
ADAM_B1 = 0.9
ADAM_B2 = 0.999
ADAM_EPS = 1e-08
ADAM_WD = 0.01
ADAM_STEP = 10
VMEM_LIMIT = 56 * 1024 * 1024
MESH_AXES = ("x", "y", "c")


def _cp(*sem):
    return pltpu.CompilerParams(dimension_semantics=sem or None, vmem_limit_bytes=VMEM_LIMIT)


_NT = (((1,), (1,)), ((), ()))
_TN = (((0,), (0,)), ((), ()))


@jax.custom_vjp
def _bdot(a, b):
    return jnp.dot(a.astype(BF16), b.astype(BF16), preferred_element_type=F32)


def _bdot_fwd(a, b):
    return _bdot(a, b), (a.astype(BF16), b.astype(BF16))


def _bdot_bwd(res, ct):
    a, b = res
    c = ct.astype(BF16)
    return (lax.dot_general(c, b, _NT, preferred_element_type=F32),
            lax.dot_general(a, c, _TN, preferred_element_type=F32))


_bdot.defvjp(_bdot_fwd, _bdot_bwd)


@jax.custom_vjp
def _bdot_nt(a, b):
    return lax.dot_general(a.astype(BF16), b.astype(BF16), _NT, preferred_element_type=F32)


def _bdot_nt_fwd(a, b):
    return _bdot_nt(a, b), (a.astype(BF16), b.astype(BF16))


def _bdot_nt_bwd(res, ct):
    a, b = res
    c = ct.astype(BF16)
    return (jnp.dot(c, b, preferred_element_type=F32),
            lax.dot_general(c, a, _TN, preferred_element_type=F32))


_bdot_nt.defvjp(_bdot_nt_fwd, _bdot_nt_bwd)


def _rms(x, g):
    return x * lax.rsqrt(jnp.mean(x * x, axis=-1, keepdims=True) + EPS) * g


def _sigmoid(x):
    return 0.5 * jnp.tanh(0.5 * x) + 0.5


def _split_dot(x, u):
    hi = x.astype(BF16)
    lo = (x - hi.astype(F32)).astype(BF16)
    return jnp.dot(hi, u, preferred_element_type=F32) + jnp.dot(lo, u, preferred_element_type=F32)


def _head_mask(h, shape):
    col = lax.broadcasted_iota(jnp.int32, shape, 1)
    return (col >= h * HEAD_DIM) & (col < (h + 1) * HEAD_DIM)


ANY = pl.BlockSpec(memory_space=pl.ANY)
MESH = pl.DeviceIdType.MESH
DOWN_ROWS = D_FF // N_DEV


def _my_place():
    x, y, c = (lax.axis_index(a) for a in MESH_AXES)
    return x, y, c, 4 * x + 2 * y + c


def _peer(x, y, c, d):
    return (x ^ (d >> 2), y ^ ((d >> 1) & 1), c ^ (d & 1))


class _Item:
    def __init__(self, src, block, dst_shape, place=None, init=None, relayed=False):
        self.src, self.block, self.dst_shape, self.init, self.relayed = src, block, dst_shape, init, relayed
        self.place = place or (lambda dst, k: dst.at[k])


def _call(body, args, *, grid, in_specs, out_specs, out_shape, sem, name, scratch_shapes=(), items=()):
    if not items:
        outs = pl.pallas_call(body, grid=grid, in_specs=in_specs, out_specs=out_specs, out_shape=out_shape, name=name,
                              scratch_shapes=list(scratch_shapes), compiler_params=_cp(*sem))(*args)
        return list(outs) if isinstance(outs, (list, tuple)) else [outs]
    n_in, n_out, n_scr, n = len(in_specs), len(out_specs), len(scratch_shapes), len(items)
    inits = [i for i, it in enumerate(items) if it.init is not None]

    def wrapped(*refs):
        core_in, srcs = refs[:n_in], refs[n_in:n_in + n]
        off = n_in + n + len(inits)
        core_out, dsts = refs[off:off + n_out], refs[off + n_out:off + n_out + n]
        scratch = refs[off + n_out + n:]
        send_sems, recv_sems, local_sems = scratch[n_scr:]
        step = pl.program_id(0)
        for a in range(1, len(grid)):
            step = step * grid[a] + pl.program_id(a)
        steps = functools.reduce(lambda p, g: p * g, grid)
        relay_step = max(0, steps - 1 - max(1, steps // 8))
        x, y, c, me = _my_place()
        direct = [i for i in range(n) if not items[i].relayed]
        relayed = [i for i in range(n) if items[i].relayed]
        chips = [(1 - x, y), (x, 1 - y), (1 - x, 1 - y)]
        sibling = (x, y, 1 - c)

        def local(i):
            return pltpu.make_async_copy(items[i].block(srcs[i], me), items[i].place(dsts[i], me), local_sems.at[i])

        def remote(d, i, sending):
            px, py, pc = _peer(x, y, c, d)
            pk = 4 * px + 2 * py + pc
            return pltpu.make_async_remote_copy(
                src_ref=items[i].block(srcs[i], pk), dst_ref=items[i].place(dsts[i], me if sending else pk),
                send_sem=send_sems.at[d - 1, i], recv_sem=recv_sems.at[d - 1, i],
                device_id=(px, py, pc), device_id_type=MESH)

        def hop(slot, i, owner, to, from_src):
            k = 4 * owner[0] + 2 * owner[1] + owner[2]
            src = items[i].block(srcs[i], me) if from_src else items[i].place(dsts[i], k)
            return pltpu.make_async_remote_copy(
                src_ref=src, dst_ref=items[i].place(dsts[i], k), send_sem=send_sems.at[slot, i],
                recv_sem=recv_sems.at[slot, i], device_id=to, device_id_type=MESH)

        @pl.when(step == 0)
        def _():
            for i in range(n):
                local(i).start()
            for d in range(1, N_DEV):
                for i in direct:
                    remote(d, i, True).start()
            for i in relayed:
                hop(0, i, (x, y, c), sibling, True).start()
                for j, chip in enumerate(chips):
                    hop(1 + j, i, (x, y, c), (*chip, c), True).start()

        body(*core_in, *core_out, *scratch[:n_scr])

        if relayed:
            @pl.when(step == relay_step)
            def _():
                for i in relayed:
                    for j, chip in enumerate(chips):
                        hop(1 + j, i, (*chip, c), (x, y, c), False).wait_recv()
                        hop(4 + j, i, (*chip, c), sibling, False).start()

        @pl.when(step == steps - 1)
        def _():
            for d in range(1, N_DEV):
                for i in direct:
                    remote(d, i, False).wait_recv()
            for i in relayed:
                hop(0, i, sibling, (x, y, c), False).wait_recv()
                for j, chip in enumerate(chips):
                    hop(4 + j, i, (*chip, 1 - c), (x, y, c), False).wait_recv()
            for d in range(1, N_DEV):
                for i in direct:
                    remote(d, i, True).wait_send()
            for i in relayed:
                for slot in range(N_DEV - 1):
                    hop(slot, i, (x, y, c), sibling, True).wait_send()
            for i in range(n):
                local(i).wait()

    outs = pl.pallas_call(
        wrapped, grid=grid, name=name,
        in_specs=list(in_specs) + [ANY] * (n + len(inits)), out_specs=list(out_specs) + [ANY] * n,
        out_shape=list(out_shape) + [jax.ShapeDtypeStruct(it.dst_shape, it.src.dtype) for it in items],
        input_output_aliases={n_in + n + j: n_out + i for j, i in enumerate(inits)},
        scratch_shapes=list(scratch_shapes) + [pltpu.SemaphoreType.DMA((N_DEV - 1, n)),
                                               pltpu.SemaphoreType.DMA((N_DEV - 1, n)), pltpu.SemaphoreType.DMA((n,))],
        compiler_params=_cp(*(["arbitrary"] * len(grid))),
    )(*args, *[it.src for it in items], *[items[i].init for i in inits])
    return list(outs)


def _norm_mm(x, g, w, l, jb, out_dtype, name, tm=1024, items=()):
    T, K = x.shape
    _, nb, _, tn = w.shape
    tm = min(tm, T)

    def body(x_ref, g_ref, w_ref, o_ref, h_ref):
        @pl.when(pl.program_id(1) == 0)
        def _():
            h_ref[...] = _rms(x_ref[...], g_ref[...]).astype(BF16)
        h = h_ref[...]
        for jj in range(jb):
            o_ref[:, jj * tn:(jj + 1) * tn] = jnp.dot(h, w_ref[jj], preferred_element_type=F32).astype(o_ref.dtype)

    return _call(
        body, (x, g, w), grid=(T // tm, nb // jb), name=name, items=items,
        in_specs=[pl.BlockSpec((tm, K), lambda i, j: (i, 0)), pl.BlockSpec((1, K), lambda i, j: (0, 0)),
                  pl.BlockSpec((None, jb, K, tn), lambda i, j: (l, j, 0, 0))],
        out_specs=[pl.BlockSpec((tm, jb * tn), lambda i, j: (i, j)), pl.BlockSpec((tm, K), lambda i, j: (i, 0))],
        out_shape=[jax.ShapeDtypeStruct((T, nb * tn), out_dtype), jax.ShapeDtypeStruct((T, K), BF16)],
        sem=("parallel", "arbitrary"))


def _mm_res(res, a, w, l, name, tn, tm=1024, items=()):
    T, K = a.shape
    N = w.shape[2]
    tm = min(tm, T)

    def body(r_ref, a_ref, w_ref, o_ref):
        o_ref[...] = r_ref[...] + jnp.dot(a_ref[...], w_ref[...], preferred_element_type=F32)

    return _call(
        body, (res, a, w), grid=(T // tm, N // tn), name=name, items=items,
        in_specs=[pl.BlockSpec((tm, tn), lambda i, j: (i, j)), pl.BlockSpec((tm, K), lambda i, j: (i, 0)),
                  pl.BlockSpec((None, K, tn), lambda i, j: (l, 0, j))],
        out_specs=[pl.BlockSpec((tm, tn), lambda i, j: (i, j))],
        out_shape=[jax.ShapeDtypeStruct((T, N), F32)],
        sem=("parallel", "parallel"))


def _mm_nt(a, w, l, out_dtype, name, tm=1024, tn=1024):
    T, K = a.shape
    N = w.shape[1]
    tm = min(tm, T)

    def body(a_ref, w_ref, o_ref):
        o_ref[...] = lax.dot_general(a_ref[...].astype(BF16), w_ref[...], _NT,
                                     preferred_element_type=F32).astype(o_ref.dtype)

    return pl.pallas_call(
        body, grid=(T // tm, N // tn), name=name,
        in_specs=[pl.BlockSpec((tm, K), lambda i, j: (i, 0)), pl.BlockSpec((None, tn, K), lambda i, j: (l, j, 0))],
        out_specs=pl.BlockSpec((tm, tn), lambda i, j: (i, j)),
        out_shape=jax.ShapeDtypeStruct((T, N), out_dtype),
        compiler_params=_cp("parallel", "parallel"))(a, w)


def _mm_norm_bwd(a, w, l, jb, x, g, dres, name, tm=1024, items=(), block_of=lambda j: j, transposed=False):
    T = a.shape[0]
    _, nb, K, tn = w.shape
    if transposed:
        tn, K = K, tn
    tm = min(tm, T)
    nj = nb // jb
    sub = min(256, tm)
    dims = (((1,), (0,)), ((), ())) if transposed else _NT

    def body(a_ref, w_ref, x_ref, g_ref, r_ref, o_ref, dg_ref):
        part = lax.dot_general(a_ref[:, :tn], w_ref[0], dims, preferred_element_type=F32)
        for jj in range(1, jb):
            part += lax.dot_general(a_ref[:, jj * tn:(jj + 1) * tn], w_ref[jj], dims, preferred_element_type=F32)

        @pl.when(pl.program_id(1) == 0)
        def _():
            o_ref[...] = part

        @pl.when(pl.program_id(1) > 0)
        def _():
            o_ref[...] += part

        @pl.when(pl.program_id(1) == nj - 1)
        def _():
            dg = jnp.zeros((1, K), F32)
            for r in range(tm // sub):
                rows = pl.ds(r * sub, sub)
                _, vjp = jax.vjp(_rms, x_ref[rows, :], g_ref[...])
                dx, dg_r = vjp(o_ref[rows, :])
                o_ref[rows, :] = r_ref[rows, :] + dx
                dg = dg + dg_r
            _acc_out(pl.program_id(0) == 0, (dg_ref,), (dg,))

    row = pl.BlockSpec((tm, K), lambda i, j: (i, 0))
    vec = pl.BlockSpec((1, K), lambda i, j: (0, 0))
    return _call(
        body, (a, w, x, g, dres), grid=(T // tm, nj), name=name, items=items,
        in_specs=[pl.BlockSpec((tm, jb * tn), lambda i, j: (i, j)),
                  pl.BlockSpec((None, jb) + w.shape[2:], lambda i, j: (l, block_of(j), 0, 0)), row, vec, row],
        out_specs=[row, vec],
        out_shape=[jax.ShapeDtypeStruct((T, K), F32), jax.ShapeDtypeStruct((1, K), F32)],
        sem=("arbitrary", "arbitrary"))


def _mm_tn_cols(lhs, rhs, tn, jb, name, tm=2048, items=(), block_of=lambda j: j):
    T, K = lhs.shape
    nb = rhs.shape[1] // tn
    tm = min(tm, T)
    nt = T // tm

    def body(l_ref, r_ref, o_ref, acc):
        part = lax.dot_general(l_ref[...], r_ref[...], _TN, preferred_element_type=F32)

        @pl.when(pl.program_id(1) == 0)
        def _():
            acc[...] = part

        @pl.when(pl.program_id(1) > 0)
        def _():
            acc[...] += part

        @pl.when(pl.program_id(1) == nt - 1)
        def _():
            for jj in range(jb):
                o_ref[jj] = acc[:, jj * tn:(jj + 1) * tn].astype(BF16)

    return _call(
        body, (lhs, rhs), grid=(nb // jb, nt), name=name, items=items,
        in_specs=[pl.BlockSpec((tm, K), lambda j, t: (t, 0)), pl.BlockSpec((tm, jb * tn), lambda j, t: (t, j))],
        out_specs=[pl.BlockSpec((jb, K, tn), lambda j, t: (block_of(j), 0, 0))],
        out_shape=[jax.ShapeDtypeStruct((nb, K, tn), BF16)],
        scratch_shapes=[pltpu.VMEM((K, jb * tn), F32)],
        sem=("parallel", "arbitrary"))


def _mm_tn_rows(lhs, rhs, tk, name, tm=2048, items=(), block_of=lambda j: j):
    T, Kl = lhs.shape
    N = rhs.shape[1]
    tm = min(tm, T)
    nt = T // tm

    def body(l_ref, r_ref, o_ref, acc):
        part = lax.dot_general(l_ref[...], r_ref[...].astype(BF16), _TN, preferred_element_type=F32)

        @pl.when(pl.program_id(1) == 0)
        def _():
            acc[...] = part

        @pl.when(pl.program_id(1) > 0)
        def _():
            acc[...] += part

        @pl.when(pl.program_id(1) == nt - 1)
        def _():
            o_ref[...] = acc[...].astype(BF16)

    return _call(
        body, (lhs, rhs), grid=(Kl // tk, nt), name=name, items=items,
        in_specs=[pl.BlockSpec((tm, tk), lambda l, t: (t, l)), pl.BlockSpec((tm, N), lambda l, t: (t, 0))],
        out_specs=[pl.BlockSpec((tk, N), lambda l, t: (block_of(l), 0))],
        out_shape=[jax.ShapeDtypeStruct((Kl, N), BF16)],
        scratch_shapes=[pltpu.VMEM((tk, N), F32)],
        sem=("parallel", "arbitrary"))


N_FF_CHUNK = D_FF_PAD // FF_PAD


def _ffn_chunk_block(j):
    return (j % 2) * N_FF_CHUNK + j // 2


def _ffn_up_act(x, g, w, name, tm=1024, items=()):
    T, K = x.shape
    tm = min(tm, T)

    def body(x_ref, g_ref, wg_ref, wu_ref, gu_ref, act_ref, h_ref):
        @pl.when(pl.program_id(1) == 0)
        def _():
            h_ref[...] = _rms(x_ref[...], g_ref[...]).astype(BF16)
        h = h_ref[...]
        gate = lax.dot_general(h, wg_ref[...], _NT, preferred_element_type=F32)
        up = lax.dot_general(h, wu_ref[...], _NT, preferred_element_type=F32)
        gu_ref[:, :FF_PAD] = gate.astype(BF16)
        gu_ref[:, FF_PAD:] = up.astype(BF16)
        act_ref[...] = (gate * _sigmoid(gate) * up).astype(BF16)

    return _call(
        body, (x, g, w, w), grid=(T // tm, N_FF_CHUNK), name=name, items=items,
        in_specs=[pl.BlockSpec((tm, K), lambda i, j: (i, 0)), pl.BlockSpec((1, K), lambda i, j: (0, 0)),
                  pl.BlockSpec((None, None, FF_PAD, K), lambda i, j: (0, j, 0, 0)),
                  pl.BlockSpec((None, None, FF_PAD, K), lambda i, j: (0, j + N_FF_CHUNK, 0, 0))],
        out_specs=[pl.BlockSpec((tm, 2 * FF_PAD), lambda i, j: (i, j)), pl.BlockSpec((tm, FF_PAD), lambda i, j: (i, j)),
                   pl.BlockSpec((tm, K), lambda i, j: (i, 0))],
        out_shape=[jax.ShapeDtypeStruct((T, 2 * D_FF_PAD), BF16), jax.ShapeDtypeStruct((T, D_FF_PAD), BF16),
                   jax.ShapeDtypeStruct((T, K), BF16)],
        sem=("parallel", "arbitrary"))


def _ffn_down_dx(dxo, w, gu, name, tm=1024, items=()):
    T, K = dxo.shape
    tm = min(tm, T)

    sub = min(512, tm)

    def body(d_ref, w_ref, gu_ref, o_ref):
        w = w_ref[...]
        for r in range(tm // sub):
            rows = slice(r * sub, (r + 1) * sub)
            d = lax.dot_general(d_ref[rows, :].astype(BF16), w, _NT, preferred_element_type=F32)
            gate = gu_ref[rows, :FF_PAD].astype(F32)
            up = gu_ref[rows, FF_PAD:].astype(F32)
            sig = _sigmoid(gate)
            silu = gate * sig
            o_ref[rows, :FF_PAD] = (d * up * (sig + silu * (1.0 - sig))).astype(BF16)
            o_ref[rows, FF_PAD:] = (d * silu).astype(BF16)

    return _call(
        body, (dxo, w, gu), grid=(T // tm, N_FF_CHUNK), name=name, items=items,
        in_specs=[pl.BlockSpec((tm, K), lambda i, j: (i, 0)), pl.BlockSpec((None, FF_PAD, K), lambda i, j: (0, j, 0)),
                  pl.BlockSpec((tm, 2 * FF_PAD), lambda i, j: (i, j))],
        out_specs=[pl.BlockSpec((tm, 2 * FF_PAD), lambda i, j: (i, j))],
        out_shape=[jax.ShapeDtypeStruct((T, 2 * D_FF_PAD), BF16)],
        sem=("parallel", "parallel"))


def _loss_head(x, g, target, name, tm=512):
    T, K = x.shape

    def loss_fn(xv, gv, tv):
        err = _rms(xv, gv) - tv
        return 0.5 * jnp.sum(jnp.mean(err * err, axis=-1, keepdims=True), axis=0, keepdims=True)

    def body(x_ref, g_ref, t_ref, l_ref, dx_ref, dg_ref):
        val, vjp = jax.vjp(lambda xv, gv: loss_fn(xv, gv, t_ref[...]), x_ref[...], g_ref[...])
        dx, dg = vjp(jnp.ones((1, 1), F32))
        dx_ref[...] = dx
        lval = jnp.broadcast_to(val, (1, 128))

        @pl.when(pl.program_id(0) == 0)
        def _():
            dg_ref[...] = dg
            l_ref[...] = lval

        @pl.when(pl.program_id(0) > 0)
        def _():
            dg_ref[...] += dg
            l_ref[...] += lval

    row = pl.BlockSpec((tm, K), lambda i: (i, 0))
    vec = pl.BlockSpec((1, K), lambda i: (0, 0))
    return pl.pallas_call(
        body, grid=(T // tm,), name=name,
        in_specs=[row, vec, row], out_specs=[pl.BlockSpec((1, 128), lambda i: (0, 0)), row, vec],
        out_shape=[jax.ShapeDtypeStruct((1, 128), F32), jax.ShapeDtypeStruct((T, K), F32),
                   jax.ShapeDtypeStruct((1, K), F32)],
        compiler_params=_cp("arbitrary"))(x, g, target)


SGU_ROWS = 4 * BLOCK


def _sgu_fn(u, v, wm, bt, gain):
    ug = jax.nn.gelu(u)
    vg = jax.nn.gelu(v)
    row = lax.broadcasted_iota(jnp.int32, (BLOCK, BLOCK), 0)
    col = lax.broadcasted_iota(jnp.int32, (BLOCK, BLOCK), 1)
    tri = row >= col
    normed = []
    for h in range(GROUP_HEADS):
        vh = vg[:, h * HEAD_DIM:(h + 1) * HEAD_DIM]
        xc = vh - jnp.mean(vh, axis=-1, keepdims=True)
        normed.append(xc * lax.rsqrt(jnp.mean(xc * xc, axis=-1, keepdims=True) + EPS))
    vn = jnp.concatenate(normed, axis=1)
    wcat = jnp.concatenate([jnp.where(tri, wm[h], 0.0) for h in range(GROUP_HEADS)], axis=1)
    bias = jnp.concatenate([jnp.broadcast_to(bt[:, h:h + 1], (BLOCK, HEAD_DIM)) for h in range(GROUP_HEADS)], axis=1)
    mixes = []
    for c in range(u.shape[0] // BLOCK):
        chunk = vn[c * BLOCK:(c + 1) * BLOCK]
        stacked = jnp.concatenate([jnp.where(_head_mask(h, chunk.shape), chunk, 0.0) for h in range(GROUP_HEADS)], axis=0)
        mixes.append(_bdot(wcat, stacked) + bias)
    return _rms(ug * jnp.concatenate(mixes, axis=0), gain)


def _sgu_fwd(proj, wm, bt, gain, name, items=()):
    T = proj.shape[0]
    rows = min(SGU_ROWS, T)

    def body(u_ref, v_ref, w_ref, b_ref, g_ref, o_ref):
        o_ref[...] = _sgu_fn(u_ref[...], v_ref[...], w_ref[...], b_ref[...], g_ref[...]).astype(BF16)

    full = lambda shape: pl.BlockSpec(shape, lambda i: (0,) * len(shape))
    return _call(
        body, (proj, proj, wm, bt, gain), grid=(T // rows,), name=name, items=items,
        in_specs=[pl.BlockSpec((rows, GROUP_WIDTH), lambda i: (i, 0)), pl.BlockSpec((rows, GROUP_WIDTH), lambda i: (i, 1)),
                  full(wm.shape), full(bt.shape), full(gain.shape)],
        out_specs=[pl.BlockSpec((rows, GROUP_WIDTH), lambda i: (i, 0))],
        out_shape=[jax.ShapeDtypeStruct((T, GROUP_WIDTH), BF16)],
        sem=("parallel",))


def _acc_out(first, refs, vals):
    @pl.when(first)
    def _():
        for r, v in zip(refs, vals):
            r[...] = v

    @pl.when(jnp.logical_not(first))
    def _():
        for r, v in zip(refs, vals):
            r[...] += v


def _sgu_bwd(proj, wm, bt, gain, dy, name, items=()):
    T = proj.shape[0]

    def body(u_ref, v_ref, w_ref, b_ref, g_ref, dy_ref, duv_ref, dw_ref, db_ref, dg_ref):
        _, vjp = jax.vjp(_sgu_fn, u_ref[...], v_ref[...], w_ref[...], b_ref[...], g_ref[...])
        du, dv, dw, db, dg = vjp(dy_ref[...])
        duv_ref[:, :GROUP_WIDTH] = du.astype(BF16)
        duv_ref[:, GROUP_WIDTH:] = dv.astype(BF16)
        _acc_out(pl.program_id(0) == 0, (dw_ref, db_ref, dg_ref), (dw, db, dg))

    full = lambda shape: pl.BlockSpec(shape, lambda i: (0,) * len(shape))
    rows = min(SGU_ROWS, T)
    return _call(
        body, (proj, proj, wm, bt, gain, dy), grid=(T // rows,), name=name, items=items,
        in_specs=[pl.BlockSpec((rows, GROUP_WIDTH), lambda i: (i, 0)), pl.BlockSpec((rows, GROUP_WIDTH), lambda i: (i, 1)),
                  full(wm.shape), full(bt.shape), full(gain.shape),
                  pl.BlockSpec((rows, GROUP_WIDTH), lambda i: (i, 0))],
        out_specs=[pl.BlockSpec((rows, 2 * GROUP_WIDTH), lambda i: (i, 0)), full(wm.shape), full(bt.shape), full(gain.shape)],
        out_shape=[jax.ShapeDtypeStruct((T, 2 * GROUP_WIDTH), BF16), jax.ShapeDtypeStruct(wm.shape, F32),
                   jax.ShapeDtypeStruct(bt.shape, F32), jax.ShapeDtypeStruct(gain.shape, F32)],
        sem=("arbitrary",))


def _pool_consts(seq):
    t = lax.broadcasted_iota(jnp.int32, (seq, GROUP_WIDTH), 0)
    grp = lax.broadcasted_iota(jnp.int32, (seq, GROUP_WIDTH), 1) // (GROUP_WIDTH // len(POOL_WINDOWS))
    win = jnp.where(grp == 0, POOL_WINDOWS[0], jnp.where(grp == 1, POOL_WINDOWS[1],
                    jnp.where(grp == 2, POOL_WINDOWS[2], POOL_WINDOWS[3])))
    count = jnp.minimum(t + 1, win).astype(F32)
    return t, grp, count


def _by_group(grp, vals):
    return jnp.where(grp == 0, vals[0], jnp.where(grp == 1, vals[1], jnp.where(grp == 2, vals[2], vals[3])))


def _window_sums(x, t, seq, back):
    def shift(a, k):
        if back:
            return jnp.where(t >= k, pltpu.roll(a, k, 0), 0.0)
        return jnp.where(t < seq - k, pltpu.roll(a, seq - k, 0), 0.0)
    sums = []
    a, k = x, 1
    for _ in POOL_WINDOWS:
        a = a + shift(a, k)
        sums.append(a)
        k *= 2
    return sums


def _pool_tail(y, wbd, scale, gain):
    return _rms(_bdot(y, wbd) * scale, gain)


def _pool_fwd(proj, wbd, scale, gain, seq, name):
    T = proj.shape[0]

    def body(p_ref, w_ref, s_ref, g_ref, o_ref):
        p = p_ref[...]
        t, grp, count = _pool_consts(seq)
        y = _by_group(grp, _window_sums(p, t, seq, True)) / count - p
        o_ref[...] = _pool_tail(y, w_ref[...], s_ref[...], g_ref[...]).astype(BF16)

    full = lambda shape: pl.BlockSpec(shape, lambda b: (0,) * len(shape))
    return pl.pallas_call(
        body, grid=(T // seq,), name=name,
        in_specs=[pl.BlockSpec((seq, GROUP_WIDTH), lambda b: (b, 2)), full(wbd.shape), full(scale.shape), full(gain.shape)],
        out_specs=pl.BlockSpec((seq, GROUP_WIDTH), lambda b: (b, 0)),
        out_shape=jax.ShapeDtypeStruct((T, GROUP_WIDTH), BF16),
        compiler_params=_cp("parallel"))(proj, wbd, scale, gain)


def _pool_bwd(proj, wbd, scale, gain, dy, seq, name):
    T = proj.shape[0]

    def body(p_ref, w_ref, s_ref, g_ref, dy_ref, dp_ref, dw_ref, ds_ref, dg_ref):
        p = p_ref[...]
        t, grp, count = _pool_consts(seq)
        y = _by_group(grp, _window_sums(p, t, seq, True)) / count - p
        _, vjp = jax.vjp(_pool_tail, y, w_ref[...], s_ref[...], g_ref[...])
        d_y, dw, ds, dg = vjp(dy_ref[...])
        dp = _by_group(grp, _window_sums(d_y / count, t, seq, False)) - d_y
        dp_ref[...] = dp.astype(BF16)
        _acc_out(pl.program_id(0) == 0, (dw_ref, ds_ref, dg_ref), (dw, ds, dg))

    full = lambda shape: pl.BlockSpec(shape, lambda b: (0,) * len(shape))
    return pl.pallas_call(
        body, grid=(T // seq,), name=name,
        in_specs=[pl.BlockSpec((seq, GROUP_WIDTH), lambda b: (b, 2)), full(wbd.shape), full(scale.shape), full(gain.shape),
                  pl.BlockSpec((seq, GROUP_WIDTH), lambda b: (b, 1))],
        out_specs=[pl.BlockSpec((seq, GROUP_WIDTH), lambda b: (b, 0)), full(wbd.shape), full(scale.shape), full(gain.shape)],
        out_shape=[jax.ShapeDtypeStruct((T, GROUP_WIDTH), BF16), jax.ShapeDtypeStruct(wbd.shape, F32),
                   jax.ShapeDtypeStruct(scale.shape, F32), jax.ShapeDtypeStruct(gain.shape, F32)],
        compiler_params=_cp("arbitrary"))(proj, wbd, scale, gain, dy)


def _swa_fn(q, kv_prev, kv_cur, sinks, tab, gain, first):
    half = GROUP_WIDTH // 2
    k2 = jnp.concatenate([kv_prev[:, :half], kv_cur[:, :half]], axis=0)
    v2 = jnp.concatenate([kv_prev[:, half:], kv_cur[:, half:]], axis=0)
    per_query_head = lambda a: jnp.concatenate(
        [a[:, (h // 2) * HEAD_DIM:(h // 2 + 1) * HEAD_DIM] for h in range(GROUP_HEADS)], axis=1)
    qs = jnp.concatenate([jnp.where(_head_mask(h, q.shape), q, 0.0) for h in range(GROUP_HEADS)], axis=0)
    qi = lax.broadcasted_iota(jnp.int32, (HEAD_ROWS, 2 * BLOCK), 0) & (BLOCK - 1)
    kj = lax.broadcasted_iota(jnp.int32, (HEAD_ROWS, 2 * BLOCK), 1)
    dist = qi + BLOCK - kj
    mask = (dist >= 0) & (dist < BLOCK) & ((kj >= BLOCK) | jnp.logical_not(first))
    logits = _bdot_nt(qs, per_query_head(k2)) * ATT_SCALE + tab.reshape(HEAD_ROWS, 2 * BLOCK)
    logits = jnp.where(mask, logits, -1e30)
    sink = jnp.concatenate([jnp.broadcast_to(sinks[:, h:h + 1], (BLOCK, 1)) for h in range(GROUP_HEADS)], axis=0)
    m = lax.stop_gradient(jnp.maximum(jnp.max(logits, axis=1, keepdims=True), sink))
    p = jnp.exp(logits - m)
    probs = p / (jnp.sum(p, axis=1, keepdims=True) + jnp.exp(sink - m))
    out = _bdot(probs, per_query_head(v2))
    y = jnp.zeros_like(q)
    for h in range(GROUP_HEADS):
        y = y + jnp.where(_head_mask(h, q.shape), out[h * BLOCK:(h + 1) * BLOCK], 0.0)
    return _rms(y, gain)


def _swa_specs(nblk):
    q = pl.BlockSpec((BLOCK, GROUP_WIDTH), lambda b, i: (b * nblk + i, 3))
    cur = pl.BlockSpec((BLOCK, GROUP_WIDTH), lambda b, i: (b * nblk + i, 4))
    prev = pl.BlockSpec((BLOCK, GROUP_WIDTH), lambda b, i: (b * nblk + jnp.maximum(i - 1, 0), 4))
    return q, prev, cur


def _swa_fwd(proj, sinks, tab, gain, seq, name, items=()):
    T = proj.shape[0]
    nblk = seq // BLOCK

    def body(q_ref, kp_ref, kc_ref, s_ref, t_ref, g_ref, o_ref):
        o_ref[...] = _swa_fn(q_ref[...], kp_ref[...], kc_ref[...], s_ref[...], t_ref[...], g_ref[...],
                             pl.program_id(1) == 0).astype(BF16)

    full = lambda shape: pl.BlockSpec(shape, lambda b, i: (0,) * len(shape))
    return _call(
        body, (proj, proj, proj, sinks, tab, gain), grid=(T // seq, nblk), name=name, items=items,
        in_specs=[*_swa_specs(nblk), full(sinks.shape), full(tab.shape), full(gain.shape)],
        out_specs=[pl.BlockSpec((BLOCK, GROUP_WIDTH), lambda b, i: (b * nblk + i, 0))],
        out_shape=[jax.ShapeDtypeStruct((T, GROUP_WIDTH), BF16)],
        sem=("parallel", "parallel"))


def _swa_bwd(proj, sinks, tab, gain, dy, seq, name, items=()):
    T = proj.shape[0]
    nblk = seq // BLOCK

    def body(q_ref, kp_ref, kc_ref, s_ref, t_ref, g_ref, dy_ref, dq_ref, dkv_ref, ds_ref, dt_ref, dg_ref):
        i = pl.program_id(1)
        first = i == 0
        fn = functools.partial(_swa_fn, first=first)
        _, vjp = jax.vjp(fn, q_ref[...], kp_ref[...], kc_ref[...], s_ref[...], t_ref[...], g_ref[...])
        dq, dkp, dkc, ds, dt, dg = vjp(dy_ref[...])
        dq_ref[...] = dq.astype(BF16)
        dkv_ref[pl.ds(pl.multiple_of(i * BLOCK, BLOCK), BLOCK), :] = dkc

        @pl.when(i > 0)
        def _():
            dkv_ref[pl.ds(pl.multiple_of((i - 1) * BLOCK, BLOCK), BLOCK), :] += dkp

        _acc_out((pl.program_id(0) == 0) & first, (ds_ref, dt_ref, dg_ref), (ds, dt, dg))

    full = lambda shape: pl.BlockSpec(shape, lambda b, i: (0,) * len(shape))
    blk = lambda c: pl.BlockSpec((BLOCK, GROUP_WIDTH), lambda b, i: (b * nblk + i, c))
    return _call(
        body, (proj, proj, proj, sinks, tab, gain, dy), grid=(T // seq, nblk), name=name, items=items,
        in_specs=[*_swa_specs(nblk), full(sinks.shape), full(tab.shape), full(gain.shape), blk(2)],
        out_specs=[blk(0), pl.BlockSpec((seq, GROUP_WIDTH), lambda b, i: (b, 0)), full(sinks.shape), full(tab.shape),
                   full(gain.shape)],
        out_shape=[jax.ShapeDtypeStruct((T, GROUP_WIDTH), BF16), jax.ShapeDtypeStruct((T, GROUP_WIDTH), F32),
                   jax.ShapeDtypeStruct(sinks.shape, F32), jax.ShapeDtypeStruct(tab.shape, F32),
                   jax.ShapeDtypeStruct(gain.shape, F32)],
        sem=("arbitrary", "arbitrary"))


def _t5_bucket(dist):
    max_exact = N_BUCKETS // 2
    df = jnp.maximum(dist, 1).astype(F32)
    large = max_exact + (jnp.log(df / max_exact) / jnp.log(jnp.float32(MAX_DISTANCE / max_exact))
                         * (N_BUCKETS - max_exact)).astype(jnp.int32)
    return jnp.where(dist < max_exact, dist, jnp.minimum(large, N_BUCKETS - 1))


def _bucket_map():
    dist = (jnp.arange(BLOCK)[:, None] + BLOCK) - jnp.arange(2 * BLOCK)[None, :]
    return _t5_bucket(jnp.clip(dist, 0, BLOCK - 1)).astype(jnp.int32)


def _bias_table(rel_bias, buckets, name, items=()):
    def body(rb_ref, bk_ref, o_ref):
        bk = bk_ref[...]
        rb = rb_ref[...]
        for h in range(GROUP_HEADS):
            acc = jnp.zeros((BLOCK, 2 * BLOCK), F32)
            for b in range(N_BUCKETS):
                acc = jnp.where(bk == b, rb[b:b + 1, h:h + 1], acc)
            o_ref[h] = acc

    full = lambda shape: pl.BlockSpec(shape, lambda i: (0,) * len(shape))
    shape = (GROUP_HEADS, BLOCK, 2 * BLOCK)
    return _call(body, (rel_bias, buckets), grid=(1,), name=name, items=items,
                 in_specs=[full(rel_bias.shape), full(buckets.shape)], out_specs=[full(shape)],
                 out_shape=[jax.ShapeDtypeStruct(shape, F32)], sem=("arbitrary",))


def _bias_table_bwd(dtab, buckets, name):
    def body(dt_ref, bk_ref, o_ref):
        bk = bk_ref[...]
        row = lax.broadcasted_iota(jnp.int32, (N_BUCKETS, GROUP_HEADS), 0)
        col = lax.broadcasted_iota(jnp.int32, (N_BUCKETS, GROUP_HEADS), 1)
        acc = jnp.zeros((N_BUCKETS, GROUP_HEADS), F32)
        for h in range(GROUP_HEADS):
            dt = dt_ref[h]
            for b in range(N_BUCKETS):
                s = jnp.sum(jnp.where(bk == b, dt, 0.0), keepdims=True)
                acc = acc + jnp.where((row == b) & (col == h), s, 0.0)
        o_ref[...] = acc

    return pl.pallas_call(body, name=name, out_shape=jax.ShapeDtypeStruct((N_BUCKETS, GROUP_HEADS), F32),
                          compiler_params=_cp())(dtab, buckets)


HEAD_ROWS = GROUP_HEADS * BLOCK


def _stack_heads(x):
    return jnp.concatenate([jnp.where(_head_mask(h, x.shape), x, 0.0) for h in range(GROUP_HEADS)], axis=0).astype(BF16)


def _sb_tile(qs, kb, q0, k0):
    z = lax.dot_general(qs, kb, _NT, preferred_element_type=F32)
    row = lax.broadcasted_iota(jnp.int32, (HEAD_ROWS, BLOCK), 0) & (BLOCK - 1)
    col = lax.broadcasted_iota(jnp.int32, (HEAD_ROWS, BLOCK), 1)
    causal = (k0 + col) < (q0 + row)
    ls_neg = -(jnp.maximum(z, 0.0) + jnp.log(1.0 + jnp.exp(-jnp.abs(z))))
    return jnp.where(causal, ls_neg, 0.0), ls_neg + z, causal


SB_DEAD = -70.0
SB_FIRST_LANE = GROUP_HEADS


def _tri(strict_upper_src):
    r = lax.broadcasted_iota(jnp.int32, (BLOCK, BLOCK), 0)
    c = lax.broadcasted_iota(jnp.int32, (BLOCK, BLOCK), 1)
    cond = {"gt": r > c, "le": r <= c, "lt": r < c}[strict_upper_src]
    return jnp.where(cond, 1.0, 0.0).astype(BF16)


def _sb_fwd(proj, gain, seq, name, items=()):
    T = proj.shape[0]
    nblk = seq // BLOCK

    def body(q_ref, k_ref, v_ref, g_ref, o_ref, raw_ref, bt_ref):
        i = pl.program_id(1)
        q = q_ref[...]
        u_gt = _tri("gt")
        lane = lax.broadcasted_iota(jnp.int32, (BLOCK, BLOCK), 1)
        heads = [slice(h * HEAD_DIM, (h + 1) * HEAD_DIM) for h in range(GROUP_HEADS)]
        rows = [slice(h * BLOCK, (h + 1) * BLOCK) for h in range(GROUP_HEADS)]
        qs = _stack_heads(q * ATT_SCALE)

        def live(carry):
            j, _, cb = carry
            return (j >= 0) & (jnp.max(cb) > SB_DEAD)

        def step(carry):
            j, accs, cb = carry
            ks = pl.multiple_of(j * BLOCK, BLOCK)
            kb = k_ref[pl.ds(ks, BLOCK), :].astype(BF16)
            vb = v_ref[pl.ds(ks, BLOCK), :].astype(BF16)
            b, a, causal = _sb_tile(qs, kb, i * BLOCK, j * BLOCK)
            tail = _split_dot(b, u_gt) + cb
            w = jnp.where(causal, jnp.exp(a + tail), 0.0).astype(BF16)
            accs = tuple(accs[h] + jnp.dot(w[rows[h]], vb[:, hs], preferred_element_type=F32)
                         for h, hs in enumerate(heads))
            return j - 1, accs, cb + jnp.sum(b, axis=1, keepdims=True)

        zero_acc = tuple(jnp.zeros((BLOCK, HEAD_DIM), F32) for _ in heads)
        j_end, accs, cb = lax.while_loop(live, step, (i, zero_acc, jnp.zeros((HEAD_ROWS, 1), F32)))
        side = jnp.where(lane == SB_FIRST_LANE, (j_end + 1).astype(F32), 0.0)
        for h in range(GROUP_HEADS):
            side = jnp.where(lane == h, cb[rows[h]], side)
        raw = jnp.concatenate(accs, axis=1)
        raw_ref[...] = raw
        bt_ref[...] = side
        o_ref[...] = _rms(raw, g_ref[...]).astype(BF16)

    return _call(
        body, (proj, proj, proj, gain), grid=(T // seq, nblk), name=name, items=items,
        in_specs=[pl.BlockSpec((BLOCK, GROUP_WIDTH), lambda b, i: (b * nblk + i, 5)),
                  pl.BlockSpec((seq, GROUP_WIDTH), lambda b, i: (b, 6)),
                  pl.BlockSpec((seq, GROUP_WIDTH), lambda b, i: (b, 7)),
                  pl.BlockSpec(gain.shape, lambda b, i: (0, 0))],
        out_specs=[pl.BlockSpec((BLOCK, GROUP_WIDTH), lambda b, i: (b * nblk + i, 0)),
                   pl.BlockSpec((BLOCK, GROUP_WIDTH), lambda b, i: (b * nblk + i, 0)),
                   pl.BlockSpec((BLOCK, BLOCK), lambda b, i: (b * nblk + i, 0))],
        out_shape=[jax.ShapeDtypeStruct((T, GROUP_WIDTH), BF16), jax.ShapeDtypeStruct((T, GROUP_WIDTH), F32),
                   jax.ShapeDtypeStruct((T, BLOCK), F32)],
        sem=("parallel", "parallel"))


def _sb_bwd(proj, gain, raw, btot, dy, seq, name, items=()):
    T = proj.shape[0]
    nblk = seq // BLOCK

    def body(q_ref, k_ref, v_ref, g_ref, raw_ref, bt_ref, dy_ref, dq_ref, dk_ref, dv_ref, dg_ref):
        i = pl.program_id(1)

        @pl.when(i == 0)
        def _():
            dk_ref[...] = jnp.zeros_like(dk_ref)
            dv_ref[...] = jnp.zeros_like(dv_ref)

        rawv = raw_ref[...]
        _, vjp = jax.vjp(_rms, rawv, g_ref[...])
        do, dg = vjp(dy_ref[...])
        _acc_out((pl.program_id(0) == 0) & (i == 0), (dg_ref,), (dg,))
        q = q_ref[...]
        bt = bt_ref[...]
        u_le = _tri("le")
        u_lt = _tri("lt")
        heads = [slice(h * HEAD_DIM, (h + 1) * HEAD_DIM) for h in range(GROUP_HEADS)]
        rows = [slice(h * BLOCK, (h + 1) * BLOCK) for h in range(GROUP_HEADS)]
        qs = _stack_heads(q * ATT_SCALE)
        dos = _stack_heads(do)
        bts = jnp.concatenate([bt[:, h:h + 1] for h in range(GROUP_HEADS)], axis=0)
        first = jnp.max(bt[:, SB_FIRST_LANE:SB_FIRST_LANE + 1]).astype(jnp.int32)
        first = jnp.minimum(jnp.maximum(first, 0), i)

        def step(j, carry):
            dqs, cb, cg = carry
            ks = pl.multiple_of(j * BLOCK, BLOCK)
            kb = k_ref[pl.ds(ks, BLOCK), :].astype(BF16)
            vb = v_ref[pl.ds(ks, BLOCK), :].astype(BF16)
            b, a, causal = _sb_tile(qs, kb, i * BLOCK, j * BLOCK)
            tail = bts - (_split_dot(b, u_le) + cb)
            w = jnp.where(causal, jnp.exp(a + tail), 0.0)
            sig = jnp.exp(a)
            g = w * lax.dot_general(dos, vb, _NT, preferred_element_type=F32)
            gpre = _split_dot(g, u_lt) + cg
            dz = jnp.where(causal, g * (1.0 - sig) - gpre * sig, 0.0).astype(BF16)
            dqs = tuple(dqs[h] + jnp.dot(dz[rows[h]], kb[:, hs], preferred_element_type=F32)
                        for h, hs in enumerate(heads))
            dk_ref[pl.ds(ks, BLOCK), :] += lax.dot_general(dz, qs, _TN, preferred_element_type=F32)
            dv_ref[pl.ds(ks, BLOCK), :] += lax.dot_general(w.astype(BF16), dos, _TN, preferred_element_type=F32)
            return dqs, cb + jnp.sum(b, axis=1, keepdims=True), cg + jnp.sum(g, axis=1, keepdims=True)

        zero_dq = tuple(jnp.zeros((BLOCK, HEAD_DIM), F32) for _ in heads)
        zero = jnp.zeros((HEAD_ROWS, 1), F32)
        dqs, _, _ = lax.fori_loop(first, i + 1, step, (zero_dq, zero, zero))
        dq_ref[...] = (jnp.concatenate(dqs, axis=1) * ATT_SCALE).astype(BF16)

    blk = lambda c: pl.BlockSpec((BLOCK, GROUP_WIDTH), lambda b, i: (b * nblk + i, c))
    seqblk = lambda c: pl.BlockSpec((seq, GROUP_WIDTH), lambda b, i: (b, c))
    vec = pl.BlockSpec(gain.shape, lambda b, i: (0, 0))
    return _call(
        body, (proj, proj, proj, gain, raw, btot, dy), grid=(T // seq, nblk), name=name, items=items,
        in_specs=[blk(5), seqblk(6), seqblk(7), vec, blk(0), pl.BlockSpec((BLOCK, BLOCK), lambda b, i: (b * nblk + i, 0)),
                  blk(3)],
        out_specs=[blk(0), seqblk(0), seqblk(0), vec],
        out_shape=[jax.ShapeDtypeStruct((T, GROUP_WIDTH), BF16), jax.ShapeDtypeStruct((T, GROUP_WIDTH), F32),
                   jax.ShapeDtypeStruct((T, GROUP_WIDTH), F32), jax.ShapeDtypeStruct(gain.shape, F32)],
        sem=("arbitrary", "arbitrary"))


def _layer_params(l, sgu_w, sgu_b, pool_w, pool_scale, swa_sinks, mix_out_gain, norm_mix, norm_ffn):
    gains = mix_out_gain[l].reshape(4, 1, GROUP_WIDTH)
    return dict(
        wm=sgu_w[l], bt=sgu_b[l].T,
        wbd=jax.scipy.linalg.block_diag(*[pool_w[l, g] for g in range(len(POOL_WINDOWS))]),
        scale=pool_scale[l][None], sinks=swa_sinks[l][None],
        gain=[gains[m] for m in range(4)], norm_mix=norm_mix[l][None], norm_ffn=norm_ffn[l][None])


def _as_weights(g_in, g_out, g_gu, g_down):
    return (g_in[None], g_out.reshape(1, D_MODEL, D_MODEL), g_gu[None], g_down.reshape(1, D_FF_PAD, D_MODEL))


def _gather_item(src, l, dst_shape, rows=None, down=False, init=None):
    r0, nr = rows or (0, src.shape[1])
    if down:
        place = lambda dst, k: dst.at[k // 2, pl.ds((k % 2) * DOWN_ROWS + r0, nr), :]
    else:
        place = lambda dst, k: dst.at[k, pl.ds(r0, nr), :]
    return _Item(src, lambda s, k: s.at[l, pl.ds(r0, nr), :], dst_shape, place, init, relayed=True)


WEIGHTS = ("in", "out", "gu", "down")
FWD_PLAN = {
    (0, "in_proj"): [(0, "down", None)], (0, "sgu"): [(0, "out", None)], (0, "swa"): [(0, "gu", 0)],
    (0, "sb"): [(0, "gu", 1)],
    (0, "ffn_up"): [(1, "gu", 0), (1, "in", None), (1, "out", None)], (0, "ffn_down"): [(1, "down", None)],
    (1, "in_proj"): [(1, "gu", 1)], (1, "swa"): [(2, "in", None), (2, "out", None)], (1, "sb"): [(2, "gu", 0)],
    (1, "ffn_up"): [(2, "gu", 1), (2, "down", None)],
    (2, "in_proj"): [(3, "down", None)], (2, "swa"): [(3, "in", None), (3, "out", None)], (2, "sb"): [(3, "gu", 0)],
    (2, "ffn_up"): [(3, "gu", 1)],
}


def _planned(pieces, shards, bufs):
    items = []
    for lyr, t, part in pieces:
        src = shards[WEIGHTS.index(t)]
        down = t == "down"
        shape = (4, FF_PAD, D_MODEL) if down else (N_DEV,) + src.shape[1:]
        init = bufs.get((lyr, t))
        if init is None and down:
            init = jnp.zeros(shape, BF16)
        rows = None if part is None else (part * (FF_PAD // 2), FF_PAD // 2)
        items.append(_gather_item(src, lyr, shape, rows, down, init))
    return items


def _layer_fwd(l, x, p, bufs, tab, seq, shards):
    def beside(host):
        return _planned(FWD_PLAN.get((l, host), ()), shards, bufs)

    def landed(host, results):
        for (lyr, t, _), r in zip(FWD_PLAN.get((l, host), ()), results):
            bufs[(lyr, t)] = r

    proj, h1, *got = _norm_mm(x, p["norm_mix"], bufs[(l, "in")][None], 0, 4, F32, f"in_proj_{l}",
                              items=beside("in_proj"))
    landed("in_proj", got)
    ya, *got = _sgu_fwd(proj, p["wm"], p["bt"], p["gain"][0], f"sgu_fwd_{l}", items=beside("sgu"))
    landed("sgu", got)
    yb = _pool_fwd(proj, p["wbd"], p["scale"], p["gain"][1], seq, f"pool_fwd_{l}")
    yc, *got = _swa_fwd(proj, p["sinks"], tab, p["gain"][2], seq, f"swa_fwd_{l}", items=beside("swa"))
    landed("swa", got)
    yd, raw, btot, *got = _sb_fwd(proj, p["gain"][3], seq, f"sb_fwd_{l}", items=beside("sb"))
    landed("sb", got)
    ycat = jnp.concatenate([ya, yb, yc, yd], axis=1)
    xm, = _mm_res(x, ycat, bufs[(l, "out")].reshape(1, D_MODEL, D_MODEL), 0, f"out_proj_{l}", tn=D_MODEL)
    gu, act, h2, *got = _ffn_up_act(xm, p["norm_ffn"], bufs[(l, "gu")][None], f"ffn_up_{l}", items=beside("ffn_up"))
    landed("ffn_up", got)
    xo, *got = _mm_res(xm, act, bufs[(l, "down")].reshape(1, D_FF_PAD, D_MODEL), 0, f"ffn_down_{l}",
                       tn=D_MODEL // 2, items=beside("ffn_down"))
    landed("ffn_down", got)
    return xo, (x, proj, h1, ycat, raw, btot, xm, gu, h2, act)


def _rows_item(g, r0, nr, init=None):
    cut = lambda a, k: a.at[k, pl.ds(r0, nr), :]
    return _Item(g, cut, g.shape, cut, init)


def _layer_bwd(l, dxo, saved, p, w, tab, seq, pending, ride):
    win, wout, wgu, wd = w
    x, proj, h1, ycat, raw, btot, xm, gu, h2, act = saved
    out_rows = D_MODEL // N_DEV
    in_half = D_MODEL // 2
    pieces = 2 if ride else 3
    piece = FF_PAD // pieces

    above = [_rows_item(pending[0], in_half, in_half, pending[1])] if pending else []
    dgu, *p_in_above = _ffn_down_dx(dxo, wd, gu, f"ffn_down_dx_{l}", items=above)
    g_wd, = _mm_tn_rows(act, dxo, FF_PAD, f"ffn_down_dw_{l}")
    send_down = _Item(g_wd, lambda src, k: src.at[pl.ds((k // 2) * FF_PAD + (k % 2) * DOWN_ROWS, DOWN_ROWS), :],
                      (N_DEV, DOWN_ROWS, D_MODEL))
    g_wgu, p_down = _mm_tn_rows(dgu, h2, FF_PAD, f"ffn_up_dw_{l}", items=(send_down,), block_of=_ffn_chunk_block)
    g_wgu = g_wgu.reshape(N_DEV, FF_PAD, D_MODEL)
    dxm, g_norm_ffn, p_gu = _mm_norm_bwd(dgu, wgu, 0, 1, xm, p["norm_ffn"], dxo, f"ffn_up_dx_{l}",
                                         items=(_rows_item(g_wgu, 0, piece),), block_of=_ffn_chunk_block,
                                         transposed=True)
    dycat = _mm_nt(dxm, wout, 0, F32, f"out_proj_dx_{l}")
    g_wout, = _mm_tn_rows(ycat, dxm, D_MODEL, f"out_proj_dw_{l}")
    send_out = _Item(g_wout, lambda src, k: src.at[pl.ds(k * out_rows, out_rows), :], (N_DEV, out_rows, D_MODEL))
    duv, g_wm, g_bt, g_ga = _sgu_bwd(proj, p["wm"], p["bt"], p["gain"][0], dycat, f"sgu_bwd_{l}")
    dp, g_wbd, g_scale, g_gb = _pool_bwd(proj, p["wbd"], p["scale"], p["gain"][1], dycat, seq, f"pool_bwd_{l}")
    beside_swa = list(ride) if ride else [_rows_item(g_wgu, piece, piece, p_gu)]
    dq, dkv, g_sinks, g_tab, g_gc, *rode = _swa_bwd(proj, p["sinks"], tab, p["gain"][2], dycat, seq,
                                                    f"swa_bwd_{l}", items=beside_swa)
    if not ride:
        p_gu, rode = rode[0], []
    dqd, dkd, dvd, g_gd, p_gu, p_out = _sb_bwd(
        proj, p["gain"][3], raw, btot, dycat, seq, f"sb_bwd_{l}",
        items=(_rows_item(g_wgu, (pieces - 1) * piece, piece, p_gu), send_out))
    dproj = jnp.concatenate([duv, dp, dq, dkv.astype(BF16), dqd, dkd.astype(BF16), dvd.astype(BF16)], axis=1)
    g_win, = _mm_tn_cols(h1, dproj, GROUP_WIDTH, 4, f"in_proj_dw_{l}")
    defer = l > 0
    dx, g_norm_mix, p_in = _mm_norm_bwd(dproj, win, 0, 4, x, p["norm_mix"], dxm, f"in_proj_dx_{l}",
                                        items=(_rows_item(g_win, 0, in_half if defer else D_MODEL),))
    pending = (g_win, p_in) if defer else None
    ng = len(POOL_WINDOWS)
    gd = GROUP_WIDTH // ng
    small = dict(
        sgu_w=g_wm, sgu_b=g_bt.T,
        pool_w=jnp.stack([g_wbd[g * gd:(g + 1) * gd, g * gd:(g + 1) * gd] for g in range(ng)]),
        pool_scale=g_scale[0], swa_sinks=g_sinks[0],
        mix_out_gain=jnp.concatenate([g_ga[0], g_gb[0], g_gc[0], g_gd[0]]),
        norm_mix=g_norm_mix[0], norm_ffn=g_norm_ffn[0])
    return dx, [p_in, p_out, p_gu, p_down], small, g_tab, rode, p_in_above, pending


def _local_step(x, target, shards, sgu_w, sgu_b, pool_w, pool_scale, swa_sinks, rel_bias, mix_out_gain,
                norm_mix, norm_ffn, norm_final, seq):
    buckets = _bucket_map()
    bufs = {}
    first = [(0, "in", None)]
    tab, *got = _bias_table(rel_bias, buckets, "bias_table", items=_planned(first, shards, bufs))
    bufs.update({(lyr, t): r for (lyr, t, _), r in zip(first, got)})
    params = [_layer_params(l, sgu_w, sgu_b, pool_w, pool_scale, swa_sinks, mix_out_gain, norm_mix, norm_ffn)
              for l in range(DEPTH)]
    saved, weights = [], []
    for l in range(DEPTH):
        x, s = _layer_fwd(l, x, params[l], bufs, tab, seq, shards)
        saved.append(s)
        weights.append(_as_weights(*[bufs[(l, t)] for t in WEIGHTS]))
    loss, dx, g_final = _loss_head(x, norm_final[None], target, "loss_head")
    big, small, g_tab, upper, pending = [None] * DEPTH, [None] * DEPTH, None, None, None
    for l in reversed(range(DEPTH)):
        ride = ()
        if l == 0:
            mine = _pack(_upper_layers(small), _pack_rows(_upper_layers(small)))
            ride = (_Item(mine, lambda src, k: src, (N_DEV,) + mine.shape),)
        dx, big[l], small[l], t, rode, done_above, pending = _layer_bwd(
            l, dx, saved[l], params[l], weights[l], tab, seq, pending, ride)
        g_tab = t if g_tab is None else g_tab + t
        upper = rode[0] if rode else upper
        if done_above:
            big[l + 1][0] = done_above[0]
    return loss, dx, big, small, _bias_table_bwd(g_tab, buckets, "bias_table_bwd"), g_final[0], upper


def _cast_pad(w, rows, name):
    L, r, c = w.shape

    def body(w_ref, o_ref):
        if rows != r:
            o_ref[...] = jnp.zeros_like(o_ref)
        o_ref[:r, :] = w_ref[...].astype(BF16)

    return pl.pallas_call(
        body, grid=(L,), name=name,
        in_specs=[pl.BlockSpec((None, r, c), lambda l: (l, 0, 0))],
        out_specs=pl.BlockSpec((None, rows, c), lambda l: (l, 0, 0)),
        out_shape=jax.ShapeDtypeStruct((L, rows, c), BF16),
        compiler_params=_cp("parallel"))(w)


def _adamw(w, g, m, v):
    m = ADAM_B1 * m + (1.0 - ADAM_B1) * g
    v = ADAM_B2 * v + (1.0 - ADAM_B2) * jnp.square(g)
    m_hat = m / (1.0 - ADAM_B1 ** ADAM_STEP)
    v_hat = v / (1.0 - ADAM_B2 ** ADAM_STEP)
    delta = -ADAM_LR * (m_hat / (jnp.sqrt(v_hat) + ADAM_EPS) + ADAM_WD * w)
    return delta, m, v


def _adamw_sharded(parts, w, m, v, tr, name, items=()):
    L, r, c = w.shape
    cp = parts[0].shape[-1]
    nrow = r // tr

    def body(*refs):
        p_refs, (w_ref, m_ref, v_ref, g_ref, d_ref, nm_ref, nv_ref) = refs[:L], refs[L:]
        for k in range(L):
            @pl.when(pl.program_id(0) == k)
            def _(p_ref=p_refs[k]):
                g = p_ref[0, :, :c].astype(F32)
                for dev in range(1, N_DEV):
                    g = g + p_ref[dev, :, :c].astype(F32)
                delta, nm, nv = _adamw(w_ref[...], g, m_ref[...], v_ref[...])
                g_ref[...] = g
                d_ref[...] = delta
                nm_ref[...] = nm
                nv_ref[...] = nv

    def part_spec(k):
        return pl.BlockSpec((N_DEV, tr, cp),
                            lambda l, i: (0, jnp.where(l == k, i, jnp.where(l < k, 0, nrow - 1)), 0))

    blk = pl.BlockSpec((None, tr, c), lambda l, i: (l, i, 0))
    out = jax.ShapeDtypeStruct((L, r, c), F32)
    return _call(
        body, (*parts, w, m, v), grid=(L, nrow), name=name, items=items,
        in_specs=[part_spec(k) for k in range(L)] + [blk, blk, blk],
        out_specs=[blk] * 4, out_shape=[out] * 4, sem=("arbitrary", "arbitrary"))


def _adamw_small(parts, wmv, name):
    def body(p_ref, wmv_ref, g_ref, d_ref, nm_ref, nv_ref):
        g = p_ref[0]
        for k in range(1, N_DEV):
            g = g + p_ref[k]
        delta, nm, nv = _adamw(wmv_ref[0], g, wmv_ref[1], wmv_ref[2])
        g_ref[...] = g
        d_ref[...] = delta
        nm_ref[...] = nm
        nv_ref[...] = nv

    out = jax.ShapeDtypeStruct(wmv.shape[1:], F32)
    return pl.pallas_call(body, name=name, out_shape=[out] * 4, compiler_params=_cp())(parts, wmv)


LAYERED = ("sgu_w", "sgu_b", "pool_w", "pool_scale", "swa_sinks", "mix_out_gain", "norm_mix", "norm_ffn")
SHARED = ("rel_bias", "norm_final")


def _seg_rows(a):
    return -(-a.size // 128)


def _pack_rows(parts):
    return -(-sum(_seg_rows(p) for p in parts) // 8) * 8


def _upper_layers(per_layer):
    if isinstance(per_layer, dict):
        return [per_layer[k][1:] for k in LAYERED]
    return [jnp.stack([per_layer[l][k] for l in range(1, DEPTH)]) for k in LAYERED]


def _layer_zero(stacked):
    return [stacked[k][:1] for k in LAYERED] + [stacked[k] for k in SHARED]


def _pack(parts, rows):
    segs = [jnp.pad(p.reshape(-1), (0, _seg_rows(p) * 128 - p.size)).reshape(_seg_rows(p), 128) for p in parts]
    used = sum(s.shape[0] for s in segs)
    return jnp.concatenate(segs + [jnp.zeros((rows - used, 128), F32)], axis=0)


def _pack_groups(groups, rows):
    segs = []
    for parts in groups:
        segs += [jnp.pad(p.reshape(-1), (0, _seg_rows(p) * 128 - p.size)).reshape(_seg_rows(p), 128) for p in parts]
        segs.append(jnp.zeros((rows - sum(_seg_rows(p) for p in parts), 128), F32))
    return jnp.concatenate(segs, axis=0).reshape(len(groups), rows, 128)


def _unpack(buf, like):
    out, at = [], 0
    for a in like:
        out.append(buf[at:at + _seg_rows(a)].reshape(-1)[:a.size].reshape(a.shape))
        at += _seg_rows(a)
    return out


def kernel(x, w_in, w_out, sgu_w, sgu_b, pool_w, pool_scale, swa_sinks, rel_bias, mix_out_gain, norm_mix, norm_ffn, w_gate_up, w_down, norm_final, loss_target, m_w_in, m_w_out, m_sgu_w, m_sgu_b, m_pool_w, m_pool_scale, m_swa_sinks, m_rel_bias, m_mix_out_gain, m_norm_mix, m_norm_ffn, m_w_gate_up, m_w_down, m_norm_final, v_w_in, v_w_out, v_sgu_w, v_sgu_b, v_pool_w, v_pool_scale, v_swa_sinks, v_rel_bias, v_mix_out_gain, v_norm_mix, v_norm_ffn, v_w_gate_up, v_w_down, v_norm_final):
    bl, seq, _ = x.shape
    L = w_in.shape[0]
    gu_t, m_gu_t, v_gu_t = (jnp.swapaxes(a, 1, 2) for a in (w_gate_up, m_w_gate_up, v_w_gate_up))
    shards = (_cast_pad(w_in, D_MODEL, "shard_w_in"), _cast_pad(w_out, D_MODEL // N_DEV, "shard_w_out"),
              _cast_pad(gu_t, FF_PAD, "shard_w_gate_up"), _cast_pad(w_down, DOWN_ROWS, "shard_w_down"))
    small_w = dict(sgu_w=sgu_w, sgu_b=sgu_b, pool_w=pool_w, pool_scale=pool_scale, swa_sinks=swa_sinks,
                   rel_bias=rel_bias, mix_out_gain=mix_out_gain, norm_mix=norm_mix, norm_ffn=norm_ffn,
                   norm_final=norm_final)
    small_m = dict(sgu_w=m_sgu_w, sgu_b=m_sgu_b, pool_w=m_pool_w, pool_scale=m_pool_scale, swa_sinks=m_swa_sinks,
                   rel_bias=m_rel_bias, mix_out_gain=m_mix_out_gain, norm_mix=m_norm_mix, norm_ffn=m_norm_ffn,
                   norm_final=m_norm_final)
    small_v = dict(sgu_w=v_sgu_w, sgu_b=v_sgu_b, pool_w=v_pool_w, pool_scale=v_pool_scale, swa_sinks=v_swa_sinks,
                   rel_bias=v_rel_bias, mix_out_gain=v_mix_out_gain, norm_mix=v_norm_mix, norm_ffn=v_norm_ffn,
                   norm_final=v_norm_final)
    loss, dx, big, small, g_rel_bias, g_final, upper = _local_step(
        x.reshape(bl * seq, D_MODEL), loss_target.reshape(bl * seq, D_MODEL), shards, sgu_w, sgu_b, pool_w,
        pool_scale, swa_sinks, rel_bias, mix_out_gain, norm_mix, norm_ffn, norm_final, seq)
    p_in, p_out, p_gu, p_down = ([big[l][t] for l in range(L)] for t in range(4))
    outs_gu = [jnp.swapaxes(a, 1, 2) for a in
               _adamw_sharded(p_gu, gu_t, m_gu_t, v_gu_t, FF_SHARD // 4, "adamw_w_gate_up")]
    outs_down = _adamw_sharded(p_down, w_down, m_w_down, v_w_down, DOWN_ROWS // 2, "adamw_w_down")
    outs_in = _adamw_sharded(p_in, w_in, m_w_in, v_w_in, 256, "adamw_w_in")
    lo_like = _layer_zero(small_w)
    lo_rows = _pack_rows(lo_like + [loss[0]])
    lo_mine = _pack([small[0][k][None] for k in LAYERED] + [g_rel_bias, g_final, loss[0]], lo_rows)
    *outs_out, lower = _adamw_sharded(p_out, w_out, m_w_out, v_w_out, D_MODEL // N_DEV, "adamw_w_out",
                                      items=(_Item(lo_mine, lambda src, k: src, (N_DEV,) + lo_mine.shape),))
    lo_res = _adamw_small(lower, _pack_groups([lo_like, _layer_zero(small_m), _layer_zero(small_v)], lo_rows),
                          "adamw_small_layer0")
    hi_like = _upper_layers(small_w)
    hi_rows = _pack_rows(hi_like)
    hi_res = _adamw_small(upper, _pack_groups([hi_like, _upper_layers(small_m), _upper_layers(small_v)], hi_rows),
                          "adamw_small_upper")
    loss_total = lo_res[0][sum(_seg_rows(a) for a in lo_like), 0]
    small_outs = []
    for lo_buf, hi_buf in zip(lo_res, hi_res):
        lo = dict(zip(LAYERED + SHARED, _unpack(lo_buf, lo_like)))
        hi = dict(zip(LAYERED, _unpack(hi_buf, hi_like)))
        small_outs.append({k: jnp.concatenate([lo[k], hi[k]], axis=0) if k in hi else lo[k] for k in lo})
    big_outs = dict(w_in=outs_in, w_out=outs_out, w_gate_up=outs_gu, w_down=outs_down)
    order = ("w_in", "w_out", "sgu_w", "sgu_b", "pool_w", "pool_scale", "swa_sinks", "rel_bias", "mix_out_gain",
             "norm_mix", "norm_ffn", "w_gate_up", "w_down", "norm_final")
    result = [loss_total, dx.reshape(bl, seq, D_MODEL)]
    for which in range(4):
        for name in order:
            result.append(big_outs[name][which] if name in big_outs else small_outs[which][name])
    return tuple(result)
```

```python
import functools

import jax
import jax.numpy as jnp
from jax import lax
from jax.experimental import pallas as pl
from jax.experimental.pallas import tpu as pltpu

F32 = jnp.float32
BF16 = jnp.bfloat16

N_DEV = 8
DEPTH = 4
D_MODEL = 1024
GROUP_WIDTH = 256
HEAD_DIM = 64
GROUP_HEADS = 4
BLOCK = 128
N_BUCKETS = 32
MAX_DISTANCE = 128
POOL_WINDOWS = (2, 4, 8, 16)
D_FF = 2816
FF_SHARD = D_FF // 4
FF_PAD = 768
D_FF_PAD = 4 * FF_PAD
EPS = 1e-6
ATT_SCALE = HEAD_DIM ** -0.5
ADAM_LR = 0.001
ADAM_B1 = 0.9
ADAM_B2 = 0.999
ADAM_EPS = 1e-08
ADAM_WD = 0.01
ADAM_STEP = 10
VMEM_LIMIT = 56 * 1024 * 1024
MESH_AXES = ("x", "y", "c")


def _cp(*sem):
    return pltpu.CompilerParams(dimension_semantics=sem or None, vmem_limit_bytes=VMEM_LIMIT)


_NT = (((1,), (1,)), ((), ()))
_TN = (((0,), (0,)), ((), ()))


@jax.custom_vjp
def _bdot(a, b):
    return jnp.dot(a.astype(BF16), b.astype(BF16), preferred_element_type=F32)


def _bdot_fwd(a, b):
    return _bdot(a, b), (a.astype(BF16), b.astype(BF16))


def _bdot_bwd(res, ct):
    a, b = res
    c = ct.astype(BF16)
    return (lax.dot_general(c, b, _NT, preferred_element_type=F32),
            lax.dot_general(a, c, _TN, preferred_element_type=F32))


_bdot.defvjp(_bdot_fwd, _bdot_bwd)


@jax.custom_vjp
def _bdot_nt(a, b):
    return lax.dot_general(a.astype(BF16), b.astype(BF16), _NT, preferred_element_type=F32)


def _bdot_nt_fwd(a, b):
    return _bdot_nt(a, b), (a.astype(BF16), b.astype(BF16))


def _bdot_nt_bwd(res, ct):
    a, b = res
    c = ct.astype(BF16)
    return (jnp.dot(c, b, preferred_element_type=F32),
            lax.dot_general(c, a, _TN, preferred_element_type=F32))


_bdot_nt.defvjp(_bdot_nt_fwd, _bdot_nt_bwd)


def _rms(x, g):
    return x * lax.rsqrt(jnp.mean(x * x, axis=-1, keepdims=True) + EPS) * g


def _sigmoid(x):
    return 0.5 * jnp.tanh(0.5 * x) + 0.5


def _split_dot(x, u):
    hi = x.astype(BF16)
    lo = (x - hi.astype(F32)).astype(BF16)
    return jnp.dot(hi, u, preferred_element_type=F32) + jnp.dot(lo, u, preferred_element_type=F32)


def _head_mask(h, shape):
    col = lax.broadcasted_iota(jnp.int32, shape, 1)
    return (col >= h * HEAD_DIM) & (col < (h + 1) * HEAD_DIM)


ANY = pl.BlockSpec(memory_space=pl.ANY)
MESH = pl.DeviceIdType.MESH
DOWN_ROWS = D_FF // N_DEV


def _my_place():
    x, y, c = (lax.axis_index(a) for a in MESH_AXES)
    return x, y, c, 4 * x + 2 * y + c


def _peer(x, y, c, d):
    return (x ^ (d >> 2), y ^ ((d >> 1) & 1), c ^ (d & 1))


class _Item:
    def __init__(self, src, block, dst_shape, place=None, init=None, relayed=False):
        self.src, self.block, self.dst_shape, self.init, self.relayed = src, block, dst_shape, init, relayed
        self.place = place or (lambda dst, k: dst.at[k])


def _call(body, args, *, grid, in_specs, out_specs, out_shape, sem, name, scratch_shapes=(), items=()):
    if not items:
        outs = pl.pallas_call(body, grid=grid, in_specs=in_specs, out_specs=out_specs, out_shape=out_shape, name=name,
                              scratch_shapes=list(scratch_shapes), compiler_params=_cp(*sem))(*args)
        return list(outs) if isinstance(outs, (list, tuple)) else [outs]
    n_in, n_out, n_scr, n = len(in_specs), len(out_specs), len(scratch_shapes), len(items)
    inits = [i for i, it in enumerate(items) if it.init is not None]

    def wrapped(*refs):
        core_in, srcs = refs[:n_in], refs[n_in:n_in + n]
        off = n_in + n + len(inits)
        core_out, dsts = refs[off:off + n_out], refs[off + n_out:off + n_out + n]
        scratch = refs[off + n_out + n:]
        send_sems, recv_sems, local_sems = scratch[n_scr:]
        step = pl.program_id(0)
        for a in range(1, len(grid)):
            step = step * grid[a] + pl.program_id(a)
        steps = functools.reduce(lambda p, g: p * g, grid)
        relay_step = max(0, steps - 1 - max(1, steps // 8))
        x, y, c, me = _my_place()
        direct = [i for i in range(n) if not items[i].relayed]
        relayed = [i for i in range(n) if items[i].relayed]
        chips = [(1 - x, y), (x, 1 - y), (1 - x, 1 - y)]
        sibling = (x, y, 1 - c)

        def local(i):
            return pltpu.make_async_copy(items[i].block(srcs[i], me), items[i].place(dsts[i], me), local_sems.at[i])

        def remote(d, i, sending):
            px, py, pc = _peer(x, y, c, d)
            pk = 4 * px + 2 * py + pc
            return pltpu.make_async_remote_copy(
                src_ref=items[i].block(srcs[i], pk), dst_ref=items[i].place(dsts[i], me if sending else pk),
                send_sem=send_sems.at[d - 1, i], recv_sem=recv_sems.at[d - 1, i],
                device_id=(px, py, pc), device_id_type=MESH)

        def hop(slot, i, owner, to, from_src):
            k = 4 * owner[0] + 2 * owner[1] + owner[2]
            src = items[i].block(srcs[i], me) if from_src else items[i].place(dsts[i], k)
            return pltpu.make_async_remote_copy(
                src_ref=src, dst_ref=items[i].place(dsts[i], k), send_sem=send_sems.at[slot, i],
                recv_sem=recv_sems.at[slot, i], device_id=to, device_id_type=MESH)

        @pl.when(step == 0)
        def _():
            for i in range(n):
                local(i).start()
            for d in range(1, N_DEV):
                for i in direct:
                    remote(d, i, True).start()
            for i in relayed:
                hop(0, i, (x, y, c), sibling, True).start()
                for j, chip in enumerate(chips):
                    hop(1 + j, i, (x, y, c), (*chip, c), True).start()

        body(*core_in, *core_out, *scratch[:n_scr])

        if relayed:
            @pl.when(step == relay_step)
            def _():
                for i in relayed:
                    for j, chip in enumerate(chips):
                        hop(1 + j, i, (*chip, c), (x, y, c), False).wait_recv()
                        hop(4 + j, i, (*chip, c), sibling, False).start()

        @pl.when(step == steps - 1)
        def _():
            for d in range(1, N_DEV):
                for i in direct:
                    remote(d, i, False).wait_recv()
            for i in relayed:
                hop(0, i, sibling, (x, y, c), False).wait_recv()
                for j, chip in enumerate(chips):
                    hop(4 + j, i, (*chip, 1 - c), (x, y, c), False).wait_recv()
            for d in range(1, N_DEV):
                for i in direct:
                    remote(d, i, True).wait_send()
            for i in relayed:
                for slot in range(N_DEV - 1):
                    hop(slot, i, (x, y, c), sibling, True).wait_send()
            for i in range(n):
                local(i).wait()

    outs = pl.pallas_call(
        wrapped, grid=grid, name=name,
        in_specs=list(in_specs) + [ANY] * (n + len(inits)), out_specs=list(out_specs) + [ANY] * n,
        out_shape=list(out_shape) + [jax.ShapeDtypeStruct(it.dst_shape, it.src.dtype) for it in items],
        input_output_aliases={n_in + n + j: n_out + i for j, i in enumerate(inits)},
        scratch_shapes=list(scratch_shapes) + [pltpu.SemaphoreType.DMA((N_DEV - 1, n)),
                                               pltpu.SemaphoreType.DMA((N_DEV - 1, n)), pltpu.SemaphoreType.DMA((n,))],
        compiler_params=_cp(*(["arbitrary"] * len(grid))),
    )(*args, *[it.src for it in items], *[items[i].init for i in inits])
    return list(outs)


def _norm_mm(x, g, w, l, jb, out_dtype, name, tm=1024, items=()):
    T, K = x.shape
    _, nb, _, tn = w.shape
    tm = min(tm, T)

    def body(x_ref, g_ref, w_ref, o_ref, h_ref):
        @pl.when(pl.program_id(1) == 0)
        def _():
            h_ref[...] = _rms(x_ref[...], g_ref[...]).astype(BF16)
        h = h_ref[...]
        for jj in range(jb):
            o_ref[:, jj * tn:(jj + 1) * tn] = jnp.dot(h, w_ref[jj], preferred_element_type=F32).astype(o_ref.dtype)

    return _call(
        body, (x, g, w), grid=(T // tm, nb // jb), name=name, items=items,
        in_specs=[pl.BlockSpec((tm, K), lambda i, j: (i, 0)), pl.BlockSpec((1, K), lambda i, j: (0, 0)),
                  pl.BlockSpec((None, jb, K, tn), lambda i, j: (l, j, 0, 0))],
        out_specs=[pl.BlockSpec((tm, jb * tn), lambda i, j: (i, j)), pl.BlockSpec((tm, K), lambda i, j: (i, 0))],
        out_shape=[jax.ShapeDtypeStruct((T, nb * tn), out_dtype), jax.ShapeDtypeStruct((T, K), BF16)],
        sem=("parallel", "arbitrary"))


def _mm_res(res, a, w, l, name, tn, tm=1024, items=()):
    T, K = a.shape
    N = w.shape[2]
    tm = min(tm, T)

    def body(r_ref, a_ref, w_ref, o_ref):
        o_ref[...] = r_ref[...] + jnp.dot(a_ref[...], w_ref[...], preferred_element_type=F32)

    return _call(
        body, (res, a, w), grid=(T // tm, N // tn), name=name, items=items,
        in_specs=[pl.BlockSpec((tm, tn), lambda i, j: (i, j)), pl.BlockSpec((tm, K), lambda i, j: (i, 0)),
                  pl.BlockSpec((None, K, tn), lambda i, j: (l, 0, j))],
        out_specs=[pl.BlockSpec((tm, tn), lambda i, j: (i, j))],
        out_shape=[jax.ShapeDtypeStruct((T, N), F32)],
        sem=("parallel", "parallel"))


def _mm_nt(a, w, l, out_dtype, name, tm=1024, tn=1024):
    T, K = a.shape
    N = w.shape[1]
    tm = min(tm, T)

    def body(a_ref, w_ref, o_ref):
        o_ref[...] = lax.dot_general(a_ref[...].astype(BF16), w_ref[...], _NT,
                                     preferred_element_type=F32).astype(o_ref.dtype)

    return pl.pallas_call(
        body, grid=(T // tm, N // tn), name=name,
        in_specs=[pl.BlockSpec((tm, K), lambda i, j: (i, 0)), pl.BlockSpec((None, tn, K), lambda i, j: (l, j, 0))],
        out_specs=pl.BlockSpec((tm, tn), lambda i, j: (i, j)),
        out_shape=jax.ShapeDtypeStruct((T, N), out_dtype),
        compiler_params=_cp("parallel", "parallel"))(a, w)


def _mm_norm_bwd(a, w, l, jb, x, g, dres, name, tm=1024, items=(), block_of=lambda j: j, transposed=False):
    T = a.shape[0]
    _, nb, K, tn = w.shape
    if transposed:
        tn, K = K, tn
    tm = min(tm, T)
    nj = nb // jb
    sub = min(256, tm)
    dims = (((1,), (0,)), ((), ())) if transposed else _NT

    def body(a_ref, w_ref, x_ref, g_ref, r_ref, o_ref, dg_ref):
        part = lax.dot_general(a_ref[:, :tn], w_ref[0], dims, preferred_element_type=F32)
        for jj in range(1, jb):
            part += lax.dot_general(a_ref[:, jj * tn:(jj + 1) * tn], w_ref[jj], dims, preferred_element_type=F32)

        @pl.when(pl.program_id(1) == 0)
        def _():
            o_ref[...] = part

        @pl.when(pl.program_id(1) > 0)
        def _():
            o_ref[...] += part

        @pl.when(pl.program_id(1) == nj - 1)
        def _():
            dg = jnp.zeros((1, K), F32)
            for r in range(tm // sub):
                rows = pl.ds(r * sub, sub)
                _, vjp = jax.vjp(_rms, x_ref[rows, :], g_ref[...])
                dx, dg_r = vjp(o_ref[rows, :])
                o_ref[rows, :] = r_ref[rows, :] + dx
                dg = dg + dg_r
            _acc_out(pl.program_id(0) == 0, (dg_ref,), (dg,))

    row = pl.BlockSpec((tm, K), lambda i, j: (i, 0))
    vec = pl.BlockSpec((1, K), lambda i, j: (0, 0))
    return _call(
        body, (a, w, x, g, dres), grid=(T // tm, nj), name=name, items=items,
        in_specs=[pl.BlockSpec((tm, jb * tn), lambda i, j: (i, j)),
                  pl.BlockSpec((None, jb) + w.shape[2:], lambda i, j: (l, block_of(j), 0, 0)), row, vec, row],
        out_specs=[row, vec],
        out_shape=[jax.ShapeDtypeStruct((T, K), F32), jax.ShapeDtypeStruct((1, K), F32)],
        sem=("arbitrary", "arbitrary"))


def _mm_tn_cols(lhs, rhs, tn, jb, name, tm=2048, items=(), block_of=lambda j: j):
    T, K = lhs.shape
    nb = rhs.shape[1] // tn
    tm = min(tm, T)
    nt = T // tm

    def body(l_ref, r_ref, o_ref, acc):
        part = lax.dot_general(l_ref[...], r_ref[...], _TN, preferred_element_type=F32)

        @pl.when(pl.program_id(1) == 0)
        def _():
            acc[...] = part

        @pl.when(pl.program_id(1) > 0)
        def _():
            acc[...] += part

        @pl.when(pl.program_id(1) == nt - 1)
        def _():
            for jj in range(jb):
                o_ref[jj] = acc[:, jj * tn:(jj + 1) * tn].astype(BF16)

    return _call(
        body, (lhs, rhs), grid=(nb // jb, nt), name=name, items=items,
        in_specs=[pl.BlockSpec((tm, K), lambda j, t: (t, 0)), pl.BlockSpec((tm, jb * tn), lambda j, t: (t, j))],
        out_specs=[pl.BlockSpec((jb, K, tn), lambda j, t: (block_of(j), 0, 0))],
        out_shape=[jax.ShapeDtypeStruct((nb, K, tn), BF16)],
        scratch_shapes=[pltpu.VMEM((K, jb * tn), F32)],
        sem=("parallel", "arbitrary"))


def _mm_tn_rows(lhs, rhs, tk, name, tm=2048, items=(), block_of=lambda j: j):
    T, Kl = lhs.shape
    N = rhs.shape[1]
    tm = min(tm, T)
    nt = T // tm

    def body(l_ref, r_ref, o_ref, acc):
        part = lax.dot_general(l_ref[...], r_ref[...].astype(BF16), _TN, preferred_element_type=F32)

        @pl.when(pl.program_id(1) == 0)
        def _():
            acc[...] = part

        @pl.when(pl.program_id(1) > 0)
        def _():
            acc[...] += part

        @pl.when(pl.program_id(1) == nt - 1)
        def _():
            o_ref[...] = acc[...].astype(BF16)

    return _call(
        body, (lhs, rhs), grid=(Kl // tk, nt), name=name, items=items,
        in_specs=[pl.BlockSpec((tm, tk), lambda l, t: (t, l)), pl.BlockSpec((tm, N), lambda l, t: (t, 0))],
        out_specs=[pl.BlockSpec((tk, N), lambda l, t: (block_of(l), 0))],
        out_shape=[jax.ShapeDtypeStruct((Kl, N), BF16)],
        scratch_shapes=[pltpu.VMEM((tk, N), F32)],
        sem=("parallel", "arbitrary"))


N_FF_CHUNK = D_FF_PAD // FF_PAD


def _ffn_chunk_block(j):
    return (j % 2) * N_FF_CHUNK + j // 2


def _ffn_up_act(x, g, w, name, tm=1024, items=()):
    T, K = x.shape
    tm = min(tm, T)

    def body(x_ref, g_ref, wg_ref, wu_ref, gu_ref, act_ref, h_ref):
        @pl.when(pl.program_id(1) == 0)
        def _():
            h_ref[...] = _rms(x_ref[...], g_ref[...]).astype(BF16)
        h = h_ref[...]
        gate = lax.dot_general(h, wg_ref[...], _NT, preferred_element_type=F32)
        up = lax.dot_general(h, wu_ref[...], _NT, preferred_element_type=F32)
        gu_ref[:, :FF_PAD] = gate.astype(BF16)
        gu_ref[:, FF_PAD:] = up.astype(BF16)
        act_ref[...] = (gate * _sigmoid(gate) * up).astype(BF16)

    return _call(
        body, (x, g, w, w), grid=(T // tm, N_FF_CHUNK), name=name, items=items,
        in_specs=[pl.BlockSpec((tm, K), lambda i, j: (i, 0)), pl.BlockSpec((1, K), lambda i, j: (0, 0)),
                  pl.BlockSpec((None, None, FF_PAD, K), lambda i, j: (0, j, 0, 0)),
                  pl.BlockSpec((None, None, FF_PAD, K), lambda i, j: (0, j + N_FF_CHUNK, 0, 0))],
        out_specs=[pl.BlockSpec((tm, 2 * FF_PAD), lambda i, j: (i, j)), pl.BlockSpec((tm, FF_PAD), lambda i, j: (i, j)),
                   pl.BlockSpec((tm, K), lambda i, j: (i, 0))],
        out_shape=[jax.ShapeDtypeStruct((T, 2 * D_FF_PAD), BF16), jax.ShapeDtypeStruct((T, D_FF_PAD), BF16),
                   jax.ShapeDtypeStruct((T, K), BF16)],
        sem=("parallel", "arbitrary"))


def _ffn_down_dx(dxo, w, gu, name, tm=1024, items=()):
    T, K = dxo.shape
    tm = min(tm, T)

    sub = min(512, tm)

    def body(d_ref, w_ref, gu_ref, o_ref):
        w = w_ref[...]
        for r in range(tm // sub):
            rows = slice(r * sub, (r + 1) * sub)
            d = lax.dot_general(d_ref[rows, :].astype(BF16), w, _NT, preferred_element_type=F32)
            gate = gu_ref[rows, :FF_PAD].astype(F32)
            up = gu_ref[rows, FF_PAD:].astype(F32)
            sig = _sigmoid(gate)
            silu = gate * sig
            o_ref[rows, :FF_PAD] = (d * up * (sig + silu * (1.0 - sig))).astype(BF16)
            o_ref[rows, FF_PAD:] = (d * silu).astype(BF16)

    return _call(
        body, (dxo, w, gu), grid=(T // tm, N_FF_CHUNK), name=name, items=items,
        in_specs=[pl.BlockSpec((tm, K), lambda i, j: (i, 0)), pl.BlockSpec((None, FF_PAD, K), lambda i, j: (0, j, 0)),
                  pl.BlockSpec((tm, 2 * FF_PAD), lambda i, j: (i, j))],
        out_specs=[pl.BlockSpec((tm, 2 * FF_PAD), lambda i, j: (i, j))],
        out_shape=[jax.ShapeDtypeStruct((T, 2 * D_FF_PAD), BF16)],
        sem=("parallel", "parallel"))


def _loss_head(x, g, target, name, tm=512):
    T, K = x.shape

    def loss_fn(xv, gv, tv):
        err = _rms(xv, gv) - tv
        return 0.5 * jnp.sum(jnp.mean(err * err, axis=-1, keepdims=True), axis=0, keepdims=True)

    def body(x_ref, g_ref, t_ref, l_ref, dx_ref, dg_ref):
        val, vjp = jax.vjp(lambda xv, gv: loss_fn(xv, gv, t_ref[...]), x_ref[...], g_ref[...])
        dx, dg = vjp(jnp.ones((1, 1), F32))
        dx_ref[...] = dx
        lval = jnp.broadcast_to(val, (1, 128))

        @pl.when(pl.program_id(0) == 0)
        def _():
            dg_ref[...] = dg
            l_ref[...] = lval

        @pl.when(pl.program_id(0) > 0)
        def _():
            dg_ref[...] += dg
            l_ref[...] += lval

    row = pl.BlockSpec((tm, K), lambda i: (i, 0))
    vec = pl.BlockSpec((1, K), lambda i: (0, 0))
    return pl.pallas_call(
        body, grid=(T // tm,), name=name,
        in_specs=[row, vec, row], out_specs=[pl.BlockSpec((1, 128), lambda i: (0, 0)), row, vec],
        out_shape=[jax.ShapeDtypeStruct((1, 128), F32), jax.ShapeDtypeStruct((T, K), F32),
                   jax.ShapeDtypeStruct((1, K), F32)],
        compiler_params=_cp("arbitrary"))(x, g, target)


SGU_ROWS = 4 * BLOCK


def _sgu_fn(u, v, wm, bt, gain):
    ug = jax.nn.gelu(u)
    vg = jax.nn.gelu(v)
    row = lax.broadcasted_iota(jnp.int32, (BLOCK, BLOCK), 0)
    col = lax.broadcasted_iota(jnp.int32, (BLOCK, BLOCK), 1)
    tri = row >= col
    normed = []
    for h in range(GROUP_HEADS):
        vh = vg[:, h * HEAD_DIM:(h + 1) * HEAD_DIM]
        xc = vh - jnp.mean(vh, axis=-1, keepdims=True)
        normed.append(xc * lax.rsqrt(jnp.mean(xc * xc, axis=-1, keepdims=True) + EPS))
    vn = jnp.concatenate(normed, axis=1)
    wcat = jnp.concatenate([jnp.where(tri, wm[h], 0.0) for h in range(GROUP_HEADS)], axis=1)
    bias = jnp.concatenate([jnp.broadcast_to(bt[:, h:h + 1], (BLOCK, HEAD_DIM)) for h in range(GROUP_HEADS)], axis=1)
    mixes = []
    for c in range(u.shape[0] // BLOCK):
        chunk = vn[c * BLOCK:(c + 1) * BLOCK]
        stacked = jnp.concatenate([jnp.where(_head_mask(h, chunk.shape), chunk, 0.0) for h in range(GROUP_HEADS)], axis=0)
        mixes.append(_bdot(wcat, stacked) + bias)
    return _rms(ug * jnp.concatenate(mixes, axis=0), gain)


def _sgu_fwd(proj, wm, bt, gain, name, items=()):
    T = proj.shape[0]
    rows = min(SGU_ROWS, T)

    def body(u_ref, v_ref, w_ref, b_ref, g_ref, o_ref):
        o_ref[...] = _sgu_fn(u_ref[...], v_ref[...], w_ref[...], b_ref[...], g_ref[...]).astype(BF16)

    full = lambda shape: pl.BlockSpec(shape, lambda i: (0,) * len(shape))
    return _call(
        body, (proj, proj, wm, bt, gain), grid=(T // rows,), name=name, items=items,
        in_specs=[pl.BlockSpec((rows, GROUP_WIDTH), lambda i: (i, 0)), pl.BlockSpec((rows, GROUP_WIDTH), lambda i: (i, 1)),
                  full(wm.shape), full(bt.shape), full(gain.shape)],
        out_specs=[pl.BlockSpec((rows, GROUP_WIDTH), lambda i: (i, 0))],
        out_shape=[jax.ShapeDtypeStruct((T, GROUP_WIDTH), BF16)],
        sem=("parallel",))


def _acc_out(first, refs, vals):
    @pl.when(first)
    def _():
        for r, v in zip(refs, vals):
            r[...] = v

    @pl.when(jnp.logical_not(first))
    def _():
        for r, v in zip(refs, vals):
            r[...] += v


def _sgu_bwd(proj, wm, bt, gain, dy, name, items=()):
    T = proj.shape[0]

    def body(u_ref, v_ref, w_ref, b_ref, g_ref, dy_ref, duv_ref, dw_ref, db_ref, dg_ref):
        _, vjp = jax.vjp(_sgu_fn, u_ref[...], v_ref[...], w_ref[...], b_ref[...], g_ref[...])
        du, dv, dw, db, dg = vjp(dy_ref[...])
        duv_ref[:, :GROUP_WIDTH] = du.astype(BF16)
        duv_ref[:, GROUP_WIDTH:] = dv.astype(BF16)
        _acc_out(pl.program_id(0) == 0, (dw_ref, db_ref, dg_ref), (dw, db, dg))

    full = lambda shape: pl.BlockSpec(shape, lambda i: (0,) * len(shape))
    rows = min(SGU_ROWS, T)
    return _call(
        body, (proj, proj, wm, bt, gain, dy), grid=(T // rows,), name=name, items=items,
        in_specs=[pl.BlockSpec((rows, GROUP_WIDTH), lambda i: (i, 0)), pl.BlockSpec((rows, GROUP_WIDTH), lambda i: (i, 1)),
                  full(wm.shape), full(bt.shape), full(gain.shape),
                  pl.BlockSpec((rows, GROUP_WIDTH), lambda i: (i, 0))],
        out_specs=[pl.BlockSpec((rows, 2 * GROUP_WIDTH), lambda i: (i, 0)), full(wm.shape), full(bt.shape), full(gain.shape)],
        out_shape=[jax.ShapeDtypeStruct((T, 2 * GROUP_WIDTH), BF16), jax.ShapeDtypeStruct(wm.shape, F32),
                   jax.ShapeDtypeStruct(bt.shape, F32), jax.ShapeDtypeStruct(gain.shape, F32)],
        sem=("arbitrary",))


def _pool_consts(seq):
    t = lax.broadcasted_iota(jnp.int32, (seq, GROUP_WIDTH), 0)
    grp = lax.broadcasted_iota(jnp.int32, (seq, GROUP_WIDTH), 1) // (GROUP_WIDTH // len(POOL_WINDOWS))
    win = jnp.where(grp == 0, POOL_WINDOWS[0], jnp.where(grp == 1, POOL_WINDOWS[1],
                    jnp.where(grp == 2, POOL_WINDOWS[2], POOL_WINDOWS[3])))
    count = jnp.minimum(t + 1, win).astype(F32)
    return t, grp, count


def _by_group(grp, vals):
    return jnp.where(grp == 0, vals[0], jnp.where(grp == 1, vals[1], jnp.where(grp == 2, vals[2], vals[3])))


def _window_sums(x, t, seq, back):
    def shift(a, k):
        if back:
            return jnp.where(t >= k, pltpu.roll(a, k, 0), 0.0)
        return jnp.where(t < seq - k, pltpu.roll(a, seq - k, 0), 0.0)
    sums = []
    a, k = x, 1
    for _ in POOL_WINDOWS:
        a = a + shift(a, k)
        sums.append(a)
        k *= 2
    return sums


def _pool_tail(y, wbd, scale, gain):
    return _rms(_bdot(y, wbd) * scale, gain)


def _pool_fwd(proj, wbd, scale, gain, seq, name):
    T = proj.shape[0]

    def body(p_ref, w_ref, s_ref, g_ref, o_ref):
        p = p_ref[...]
        t, grp, count = _pool_consts(seq)
        y = _by_group(grp, _window_sums(p, t, seq, True)) / count - p
        o_ref[...] = _pool_tail(y, w_ref[...], s_ref[...], g_ref[...]).astype(BF16)

    full = lambda shape: pl.BlockSpec(shape, lambda b: (0,) * len(shape))
    return pl.pallas_call(
        body, grid=(T // seq,), name=name,
        in_specs=[pl.BlockSpec((seq, GROUP_WIDTH), lambda b: (b, 2)), full(wbd.shape), full(scale.shape), full(gain.shape)],
        out_specs=pl.BlockSpec((seq, GROUP_WIDTH), lambda b: (b, 0)),
        out_shape=jax.ShapeDtypeStruct((T, GROUP_WIDTH), BF16),
        compiler_params=_cp("parallel"))(proj, wbd, scale, gain)


def _pool_bwd(proj, wbd, scale, gain, dy, seq, name):
    T = proj.shape[0]

    def body(p_ref, w_ref, s_ref, g_ref, dy_ref, dp_ref, dw_ref, ds_ref, dg_ref):
        p = p_ref[...]
        t, grp, count = _pool_consts(seq)
        y = _by_group(grp, _window_sums(p, t, seq, True)) / count - p
        _, vjp = jax.vjp(_pool_tail, y, w_ref[...], s_ref[...], g_ref[...])
        d_y, dw, ds, dg = vjp(dy_ref[...])
        dp = _by_group(grp, _window_sums(d_y / count, t, seq, False)) - d_y
        dp_ref[...] = dp.astype(BF16)
        _acc_out(pl.program_id(0) == 0, (dw_ref, ds_ref, dg_ref), (dw, ds, dg))

    full = lambda shape: pl.BlockSpec(shape, lambda b: (0,) * len(shape))
    return pl.pallas_call(
        body, grid=(T // seq,), name=name,
        in_specs=[pl.BlockSpec((seq, GROUP_WIDTH), lambda b: (b, 2)), full(wbd.shape), full(scale.shape), full(gain.shape),
                  pl.BlockSpec((seq, GROUP_WIDTH), lambda b: (b, 1))],
        out_specs=[pl.BlockSpec((seq, GROUP_WIDTH), lambda b: (b, 0)), full(wbd.shape), full(scale.shape), full(gain.shape)],
        out_shape=[jax.ShapeDtypeStruct((T, GROUP_WIDTH), BF16), jax.ShapeDtypeStruct(wbd.shape, F32),
                   jax.ShapeDtypeStruct(scale.shape, F32), jax.ShapeDtypeStruct(gain.shape, F32)],
        compiler_params=_cp("arbitrary"))(proj, wbd, scale, gain, dy)


def _swa_fn(q, kv_prev, kv_cur, sinks, tab, gain, first):
    half = GROUP_WIDTH // 2
    k2 = jnp.concatenate([kv_prev[:, :half], kv_cur[:, :half]], axis=0)
    v2 = jnp.concatenate([kv_prev[:, half:], kv_cur[:, half:]], axis=0)
    per_query_head = lambda a: jnp.concatenate(
        [a[:, (h // 2) * HEAD_DIM:(h // 2 + 1) * HEAD_DIM] for h in range(GROUP_HEADS)], axis=1)
    qs = jnp.concatenate([jnp.where(_head_mask(h, q.shape), q, 0.0) for h in range(GROUP_HEADS)], axis=0)
    qi = lax.broadcasted_iota(jnp.int32, (HEAD_ROWS, 2 * BLOCK), 0) & (BLOCK - 1)
    kj = lax.broadcasted_iota(jnp.int32, (HEAD_ROWS, 2 * BLOCK), 1)
    dist = qi + BLOCK - kj
    mask = (dist >= 0) & (dist < BLOCK) & ((kj >= BLOCK) | jnp.logical_not(first))
    logits = _bdot_nt(qs, per_query_head(k2)) * ATT_SCALE + tab.reshape(HEAD_ROWS, 2 * BLOCK)
    logits = jnp.where(mask, logits, -1e30)
    sink = jnp.concatenate([jnp.broadcast_to(sinks[:, h:h + 1], (BLOCK, 1)) for h in range(GROUP_HEADS)], axis=0)
    m = lax.stop_gradient(jnp.maximum(jnp.max(logits, axis=1, keepdims=True), sink))
    p = jnp.exp(logits - m)
    probs = p / (jnp.sum(p, axis=1, keepdims=True) + jnp.exp(sink - m))
    out = _bdot(probs, per_query_head(v2))
    y = jnp.zeros_like(q)
    for h in range(GROUP_HEADS):
        y = y + jnp.where(_head_mask(h, q.shape), out[h * BLOCK:(h + 1) * BLOCK], 0.0)
    return _rms(y, gain)


def _swa_specs(nblk):
    q = pl.BlockSpec((BLOCK, GROUP_WIDTH), lambda b, i: (b * nblk + i, 3))
    cur = pl.BlockSpec((BLOCK, GROUP_WIDTH), lambda b, i: (b * nblk + i, 4))
    prev = pl.BlockSpec((BLOCK, GROUP_WIDTH), lambda b, i: (b * nblk + jnp.maximum(i - 1, 0), 4))
    return q, prev, cur


def _swa_fwd(proj, sinks, tab, gain, seq, name, items=()):
    T = proj.shape[0]
    nblk = seq // BLOCK

    def body(q_ref, kp_ref, kc_ref, s_ref, t_ref, g_ref, o_ref):
        o_ref[...] = _swa_fn(q_ref[...], kp_ref[...], kc_ref[...], s_ref[...], t_ref[...], g_ref[...],
                             pl.program_id(1) == 0).astype(BF16)

    full = lambda shape: pl.BlockSpec(shape, lambda b, i: (0,) * len(shape))
    return _call(
        body, (proj, proj, proj, sinks, tab, gain), grid=(T // seq, nblk), name=name, items=items,
        in_specs=[*_swa_specs(nblk), full(sinks.shape), full(tab.shape), full(gain.shape)],
        out_specs=[pl.BlockSpec((BLOCK, GROUP_WIDTH), lambda b, i: (b * nblk + i, 0))],
        out_shape=[jax.ShapeDtypeStruct((T, GROUP_WIDTH), BF16)],
        sem=("parallel", "parallel"))


def _swa_bwd(proj, sinks, tab, gain, dy, seq, name, items=()):
    T = proj.shape[0]
    nblk = seq // BLOCK

    def body(q_ref, kp_ref, kc_ref, s_ref, t_ref, g_ref, dy_ref, dq_ref, dkv_ref, ds_ref, dt_ref, dg_ref):
        i = pl.program_id(1)
        first = i == 0
        fn = functools.partial(_swa_fn, first=first)
        _, vjp = jax.vjp(fn, q_ref[...], kp_ref[...], kc_ref[...], s_ref[...], t_ref[...], g_ref[...])
        dq, dkp, dkc, ds, dt, dg = vjp(dy_ref[...])
        dq_ref[...] = dq.astype(BF16)
        dkv_ref[pl.ds(pl.multiple_of(i * BLOCK, BLOCK), BLOCK), :] = dkc

        @pl.when(i > 0)
        def _():
            dkv_ref[pl.ds(pl.multiple_of((i - 1) * BLOCK, BLOCK), BLOCK), :] += dkp

        _acc_out((pl.program_id(0) == 0) & first, (ds_ref, dt_ref, dg_ref), (ds, dt, dg))

    full = lambda shape: pl.BlockSpec(shape, lambda b, i: (0,) * len(shape))
    blk = lambda c: pl.BlockSpec((BLOCK, GROUP_WIDTH), lambda b, i: (b * nblk + i, c))
    return _call(
        body, (proj, proj, proj, sinks, tab, gain, dy), grid=(T // seq, nblk), name=name, items=items,
        in_specs=[*_swa_specs(nblk), full(sinks.shape), full(tab.shape), full(gain.shape), blk(2)],
        out_specs=[blk(0), pl.BlockSpec((seq, GROUP_WIDTH), lambda b, i: (b, 0)), full(sinks.shape), full(tab.shape),
                   full(gain.shape)],
        out_shape=[jax.ShapeDtypeStruct((T, GROUP_WIDTH), BF16), jax.ShapeDtypeStruct((T, GROUP_WIDTH), F32),
                   jax.ShapeDtypeStruct(sinks.shape, F32), jax.ShapeDtypeStruct(tab.shape, F32),
                   jax.ShapeDtypeStruct(gain.shape, F32)],
        sem=("arbitrary", "arbitrary"))


def _t5_bucket(dist):
    max_exact = N_BUCKETS // 2
    df = jnp.maximum(dist, 1).astype(F32)
    large = max_exact + (jnp.log(df / max_exact) / jnp.log(jnp.float32(MAX_DISTANCE / max_exact))
                         * (N_BUCKETS - max_exact)).astype(jnp.int32)
    return jnp.where(dist < max_exact, dist, jnp.minimum(large, N_BUCKETS - 1))


def _bucket_map():
    dist = (jnp.arange(BLOCK)[:, None] + BLOCK) - jnp.arange(2 * BLOCK)[None, :]
    return _t5_bucket(jnp.clip(dist, 0, BLOCK - 1)).astype(jnp.int32)


def _bias_table(rel_bias, buckets, name, items=()):
    def body(rb_ref, bk_ref, o_ref):
        bk = bk_ref[...]
        rb = rb_ref[...]
        for h in range(GROUP_HEADS):
            acc = jnp.zeros((BLOCK, 2 * BLOCK), F32)
            for b in range(N_BUCKETS):
                acc = jnp.where(bk == b, rb[b:b + 1, h:h + 1], acc)
            o_ref[h] = acc

    full = lambda shape: pl.BlockSpec(shape, lambda i: (0,) * len(shape))
    shape = (GROUP_HEADS, BLOCK, 2 * BLOCK)
    return _call(body, (rel_bias, buckets), grid=(1,), name=name, items=items,
                 in_specs=[full(rel_bias.shape), full(buckets.shape)], out_specs=[full(shape)],
                 out_shape=[jax.ShapeDtypeStruct(shape, F32)], sem=("arbitrary",))


def _bias_table_bwd(dtab, buckets, name):
    def body(dt_ref, bk_ref, o_ref):
        bk = bk_ref[...]
        row = lax.broadcasted_iota(jnp.int32, (N_BUCKETS, GROUP_HEADS), 0)
        col = lax.broadcasted_iota(jnp.int32, (N_BUCKETS, GROUP_HEADS), 1)
        acc = jnp.zeros((N_BUCKETS, GROUP_HEADS), F32)
        for h in range(GROUP_HEADS):
            dt = dt_ref[h]
            for b in range(N_BUCKETS):
                s = jnp.sum(jnp.where(bk == b, dt, 0.0), keepdims=True)
                acc = acc + jnp.where((row == b) & (col == h), s, 0.0)
        o_ref[...] = acc

    return pl.pallas_call(body, name=name, out_shape=jax.ShapeDtypeStruct((N_BUCKETS, GROUP_HEADS), F32),
                          compiler_params=_cp())(dtab, buckets)


HEAD_ROWS = GROUP_HEADS * BLOCK


def _stack_heads(x):
    return jnp.concatenate([jnp.where(_head_mask(h, x.shape), x, 0.0) for h in range(GROUP_HEADS)], axis=0).astype(BF16)


def _sb_tile(qs, kb, q0, k0):
    z = lax.dot_general(qs, kb, _NT, preferred_element_type=F32)
    row = lax.broadcasted_iota(jnp.int32, (HEAD_ROWS, BLOCK), 0) & (BLOCK - 1)
    col = lax.broadcasted_iota(jnp.int32, (HEAD_ROWS, BLOCK), 1)
    causal = (k0 + col) < (q0 + row)
    ls_neg = -(jnp.maximum(z, 0.0) + jnp.log(1.0 + jnp.exp(-jnp.abs(z))))
    return jnp.where(causal, ls_neg, 0.0), ls_neg + z, causal


SB_DEAD = -70.0
SB_FIRST_LANE = GROUP_HEADS


def _tri(strict_upper_src):
    r = lax.broadcasted_iota(jnp.int32, (BLOCK, BLOCK), 0)
    c = lax.broadcasted_iota(jnp.int32, (BLOCK, BLOCK), 1)
    cond = {"gt": r > c, "le": r <= c, "lt": r < c}[strict_upper_src]
    return jnp.where(cond, 1.0, 0.0).astype(BF16)


def _sb_fwd(proj, gain, seq, name, items=()):
    T = proj.shape[0]
    nblk = seq // BLOCK

    def body(q_ref, k_ref, v_ref, g_ref, o_ref, raw_ref, bt_ref):
        i = pl.program_id(1)
        q = q_ref[...]
        u_gt = _tri("gt")
        lane = lax.broadcasted_iota(jnp.int32, (BLOCK, BLOCK), 1)
        heads = [slice(h * HEAD_DIM, (h + 1) * HEAD_DIM) for h in range(GROUP_HEADS)]
        rows = [slice(h * BLOCK, (h + 1) * BLOCK) for h in range(GROUP_HEADS)]
        qs = _stack_heads(q * ATT_SCALE)

        def live(carry):
            j, _, cb = carry
            return (j >= 0) & (jnp.max(cb) > SB_DEAD)

        def step(carry):
            j, accs, cb = carry
            ks = pl.multiple_of(j * BLOCK, BLOCK)
            kb = k_ref[pl.ds(ks, BLOCK), :].astype(BF16)
            vb = v_ref[pl.ds(ks, BLOCK), :].astype(BF16)
            b, a, causal = _sb_tile(qs, kb, i * BLOCK, j * BLOCK)
            tail = _split_dot(b, u_gt) + cb
            w = jnp.where(causal, jnp.exp(a + tail), 0.0).astype(BF16)
            accs = tuple(accs[h] + jnp.dot(w[rows[h]], vb[:, hs], preferred_element_type=F32)
                         for h, hs in enumerate(heads))
            return j - 1, accs, cb + jnp.sum(b, axis=1, keepdims=True)

        zero_acc = tuple(jnp.zeros((BLOCK, HEAD_DIM), F32) for _ in heads)
        j_end, accs, cb = lax.while_loop(live, step, (i, zero_acc, jnp.zeros((HEAD_ROWS, 1), F32)))
        side = jnp.where(lane == SB_FIRST_LANE, (j_end + 1).astype(F32), 0.0)
        for h in range(GROUP_HEADS):
            side = jnp.where(lane == h, cb[rows[h]], side)
        raw = jnp.concatenate(accs, axis=1)
        raw_ref[...] = raw
        bt_ref[...] = side
        o_ref[...] = _rms(raw, g_ref[...]).astype(BF16)

    return _call(
        body, (proj, proj, proj, gain), grid=(T // seq, nblk), name=name, items=items,
        in_specs=[pl.BlockSpec((BLOCK, GROUP_WIDTH), lambda b, i: (b * nblk + i, 5)),
                  pl.BlockSpec((seq, GROUP_WIDTH), lambda b, i: (b, 6)),
                  pl.BlockSpec((seq, GROUP_WIDTH), lambda b, i: (b, 7)),
                  pl.BlockSpec(gain.shape, lambda b, i: (0, 0))],
        out_specs=[pl.BlockSpec((BLOCK, GROUP_WIDTH), lambda b, i: (b * nblk + i, 0)),
                   pl.BlockSpec((BLOCK, GROUP_WIDTH), lambda b, i: (b * nblk + i, 0)),
                   pl.BlockSpec((BLOCK, BLOCK), lambda b, i: (b * nblk + i, 0))],
        out_shape=[jax.ShapeDtypeStruct((T, GROUP_WIDTH), BF16), jax.ShapeDtypeStruct((T, GROUP_WIDTH), F32),
                   jax.ShapeDtypeStruct((T, BLOCK), F32)],
        sem=("parallel", "parallel"))


def _sb_bwd(proj, gain, raw, btot, dy, seq, name, items=()):
    T = proj.shape[0]
    nblk = seq // BLOCK

    def body(q_ref, k_ref, v_ref, g_ref, raw_ref, bt_ref, dy_ref, dq_ref, dk_ref, dv_ref, dg_ref):
        i = pl.program_id(1)

        @pl.when(i == 0)
        def _():
            dk_ref[...] = jnp.zeros_like(dk_ref)
            dv_ref[...] = jnp.zeros_like(dv_ref)

        rawv = raw_ref[...]
        _, vjp = jax.vjp(_rms, rawv, g_ref[...])
        do, dg = vjp(dy_ref[...])
        _acc_out((pl.program_id(0) == 0) & (i == 0), (dg_ref,), (dg,))
        q = q_ref[...]
        bt = bt_ref[...]
        u_le = _tri("le")
        u_lt = _tri("lt")
        heads = [slice(h * HEAD_DIM, (h + 1) * HEAD_DIM) for h in range(GROUP_HEADS)]
        rows = [slice(h * BLOCK, (h + 1) * BLOCK) for h in range(GROUP_HEADS)]
        qs = _stack_heads(q * ATT_SCALE)
        dos = _stack_heads(do)
        bts = jnp.concatenate([bt[:, h:h + 1] for h in range(GROUP_HEADS)], axis=0)
        first = jnp.max(bt[:, SB_FIRST_LANE:SB_FIRST_LANE + 1]).astype(jnp.int32)
        first = jnp.minimum(jnp.maximum(first, 0), i)

        def step(j, carry):
            dqs, cb, cg = carry
            ks = pl.multiple_of(j * BLOCK, BLOCK)
            kb = k_ref[pl.ds(ks, BLOCK), :].astype(BF16)
            vb = v_ref[pl.ds(ks, BLOCK), :].astype(BF16)
            b, a, causal = _sb_tile(qs, kb, i * BLOCK, j * BLOCK)
            tail = bts - (_split_dot(b, u_le) + cb)
            w = jnp.where(causal, jnp.exp(a + tail), 0.0)
            sig = jnp.exp(a)
            g = w * lax.dot_general(dos, vb, _NT, preferred_element_type=F32)
            gpre = _split_dot(g, u_lt) + cg
            dz = jnp.where(causal, g * (1.0 - sig) - gpre * sig, 0.0).astype(BF16)
            dqs = tuple(dqs[h] + jnp.dot(dz[rows[h]], kb[:, hs], preferred_element_type=F32)
                        for h, hs in enumerate(heads))
            dk_ref[pl.ds(ks, BLOCK), :] += lax.dot_general(dz, qs, _TN, preferred_element_type=F32)
            dv_ref[pl.ds(ks, BLOCK), :] += lax.dot_general(w.astype(BF16), dos, _TN, preferred_element_type=F32)
            return dqs, cb + jnp.sum(b, axis=1, keepdims=True), cg + jnp.sum(g, axis=1, keepdims=True)

        zero_dq = tuple(jnp.zeros((BLOCK, HEAD_DIM), F32) for _ in heads)
        zero = jnp.zeros((HEAD_ROWS, 1), F32)
        dqs, _, _ = lax.fori_loop(first, i + 1, step, (zero_dq, zero, zero))
        dq_ref[...] = (jnp.concatenate(dqs, axis=1) * ATT_SCALE).astype(BF16)

    blk = lambda c: pl.BlockSpec((BLOCK, GROUP_WIDTH), lambda b, i: (b * nblk + i, c))
    seqblk = lambda c: pl.BlockSpec((seq, GROUP_WIDTH), lambda b, i: (b, c))
    vec = pl.BlockSpec(gain.shape, lambda b, i: (0, 0))
    return _call(
        body, (proj, proj, proj, gain, raw, btot, dy), grid=(T // seq, nblk), name=name, items=items,
        in_specs=[blk(5), seqblk(6), seqblk(7), vec, blk(0), pl.BlockSpec((BLOCK, BLOCK), lambda b, i: (b * nblk + i, 0)),
                  blk(3)],
        out_specs=[blk(0), seqblk(0), seqblk(0), vec],
        out_shape=[jax.ShapeDtypeStruct((T, GROUP_WIDTH), BF16), jax.ShapeDtypeStruct((T, GROUP_WIDTH), F32),
                   jax.ShapeDtypeStruct((T, GROUP_WIDTH), F32), jax.ShapeDtypeStruct(gain.shape, F32)],
        sem=("arbitrary", "arbitrary"))


def _layer_params(l, sgu_w, sgu_b, pool_w, pool_scale, swa_sinks, mix_out_gain, norm_mix, norm_ffn):
    gains = mix_out_gain[l].reshape(4, 1, GROUP_WIDTH)
    return dict(
        wm=sgu_w[l], bt=sgu_b[l].T,
        wbd=jax.scipy.linalg.block_diag(*[pool_w[l, g] for g in range(len(POOL_WINDOWS))]),
        scale=pool_scale[l][None], sinks=swa_sinks[l][None],
        gain=[gains[m] for m in range(4)], norm_mix=norm_mix[l][None], norm_ffn=norm_ffn[l][None])


def _as_weights(g_in, g_out, g_gu, g_down):
    return (g_in[None], g_out.reshape(1, D_MODEL, D_MODEL), g_gu[None], g_down.reshape(1, D_FF_PAD, D_MODEL))


def _gather_item(src, l, dst_shape, rows=None, down=False, init=None):
    r0, nr = rows or (0, src.shape[1])
    if down:
        place = lambda dst, k: dst.at[k // 2, pl.ds((k % 2) * DOWN_ROWS + r0, nr), :]
    else:
        place = lambda dst, k: dst.at[k, pl.ds(r0, nr), :]
    return _Item(src, lambda s, k: s.at[l, pl.ds(r0, nr), :], dst_shape, place, init, relayed=True)


WEIGHTS = ("in", "out", "gu", "down")
FWD_PLAN = {
    (0, "in_proj"): [(0, "down", None)], (0, "sgu"): [(0, "out", None)], (0, "swa"): [(0, "gu", 0)],
    (0, "sb"): [(0, "gu", 1)],
    (0, "ffn_up"): [(1, "gu", 0), (1, "in", None), (1, "out", None)], (0, "ffn_down"): [(1, "down", None)],
    (1, "swa"): [(1, "gu", 1)], (1, "sb"): [(2, "in", None), (2, "out", None)],
    (1, "ffn_up"): [(2, "gu", None)], (1, "ffn_down"): [(2, "down", None)],
    (2, "swa"): [(3, "in", None), (3, "out", None)], (2, "sb"): [(3, "gu", 0)],
    (2, "ffn_up"): [(3, "gu", 1)], (2, "ffn_down"): [(3, "down", None)],
}


def _planned(pieces, shards, bufs):
    items = []
    for lyr, t, part in pieces:
        src = shards[WEIGHTS.index(t)]
        down = t == "down"
        shape = (4, FF_PAD, D_MODEL) if down else (N_DEV,) + src.shape[1:]
        init = bufs.get((lyr, t))
        if init is None and down:
            init = jnp.zeros(shape, BF16)
        rows = None if part is None else (part * (FF_PAD // 2), FF_PAD // 2)
        items.append(_gather_item(src, lyr, shape, rows, down, init))
    return items


def _layer_fwd(l, x, p, bufs, tab, seq, shards):
    def beside(host):
        return _planned(FWD_PLAN.get((l, host), ()), shards, bufs)

    def landed(host, results):
        for (lyr, t, _), r in zip(FWD_PLAN.get((l, host), ()), results):
            bufs[(lyr, t)] = r

    proj, h1, *got = _norm_mm(x, p["norm_mix"], bufs[(l, "in")][None], 0, 4, F32, f"in_proj_{l}",
                              items=beside("in_proj"))
    landed("in_proj", got)
    ya, *got = _sgu_fwd(proj, p["wm"], p["bt"], p["gain"][0], f"sgu_fwd_{l}", items=beside("sgu"))
    landed("sgu", got)
    yb = _pool_fwd(proj, p["wbd"], p["scale"], p["gain"][1], seq, f"pool_fwd_{l}")
    yc, *got = _swa_fwd(proj, p["sinks"], tab, p["gain"][2], seq, f"swa_fwd_{l}", items=beside("swa"))
    landed("swa", got)
    yd, raw, btot, *got = _sb_fwd(proj, p["gain"][3], seq, f"sb_fwd_{l}", items=beside("sb"))
    landed("sb", got)
    ycat = jnp.concatenate([ya, yb, yc, yd], axis=1)
    xm, = _mm_res(x, ycat, bufs[(l, "out")].reshape(1, D_MODEL, D_MODEL), 0, f"out_proj_{l}", tn=D_MODEL)
    gu, act, h2, *got = _ffn_up_act(xm, p["norm_ffn"], bufs[(l, "gu")][None], f"ffn_up_{l}", items=beside("ffn_up"))
    landed("ffn_up", got)
    xo, *got = _mm_res(xm, act, bufs[(l, "down")].reshape(1, D_FF_PAD, D_MODEL), 0, f"ffn_down_{l}",
                       tn=D_MODEL // 2, items=beside("ffn_down"))
    landed("ffn_down", got)
    return xo, (x, proj, h1, ycat, raw, btot, xm, gu, h2, act)


def _rows_item(g, r0, nr, init=None):
    cut = lambda a, k: a.at[k, pl.ds(r0, nr), :]
    return _Item(g, cut, g.shape, cut, init)


def _layer_bwd(l, dxo, saved, p, w, tab, seq, pending, ride):
    win, wout, wgu, wd = w
    x, proj, h1, ycat, raw, btot, xm, gu, h2, act = saved
    out_rows = D_MODEL // N_DEV
    in_half = D_MODEL // 2
    pieces = 2 if ride else 3
    piece = FF_PAD // pieces

    above = [_rows_item(pending[0], in_half, in_half, pending[1])] if pending else []
    dgu, *p_in_above = _ffn_down_dx(dxo, wd, gu, f"ffn_down_dx_{l}", items=above)
    g_wd, = _mm_tn_rows(act, dxo, FF_PAD, f"ffn_down_dw_{l}")
    send_down = _Item(g_wd, lambda src, k: src.at[pl.ds((k // 2) * FF_PAD + (k % 2) * DOWN_ROWS, DOWN_ROWS), :],
                      (N_DEV, DOWN_ROWS, D_MODEL))
    g_wgu, p_down = _mm_tn_rows(dgu, h2, FF_PAD, f"ffn_up_dw_{l}", items=(send_down,), block_of=_ffn_chunk_block)
    g_wgu = g_wgu.reshape(N_DEV, FF_PAD, D_MODEL)
    dxm, g_norm_ffn, p_gu = _mm_norm_bwd(dgu, wgu, 0, 1, xm, p["norm_ffn"], dxo, f"ffn_up_dx_{l}",
                                         items=(_rows_item(g_wgu, 0, piece),), block_of=_ffn_chunk_block,
                                         transposed=True)
    dycat = _mm_nt(dxm, wout, 0, F32, f"out_proj_dx_{l}")
    g_wout, = _mm_tn_rows(ycat, dxm, D_MODEL, f"out_proj_dw_{l}")
    send_out = _Item(g_wout, lambda src, k: src.at[pl.ds(k * out_rows, out_rows), :], (N_DEV, out_rows, D_MODEL))
    duv, g_wm, g_bt, g_ga = _sgu_bwd(proj, p["wm"], p["bt"], p["gain"][0], dycat, f"sgu_bwd_{l}")
    dp, g_wbd, g_scale, g_gb = _pool_bwd(proj, p["wbd"], p["scale"], p["gain"][1], dycat, seq, f"pool_bwd_{l}")
    beside_swa = list(ride) if ride else [_rows_item(g_wgu, piece, piece, p_gu)]
    dq, dkv, g_sinks, g_tab, g_gc, *rode = _swa_bwd(proj, p["sinks"], tab, p["gain"][2], dycat, seq,
                                                    f"swa_bwd_{l}", items=beside_swa)
    if not ride:
        p_gu, rode = rode[0], []
    dqd, dkd, dvd, g_gd, p_gu, p_out = _sb_bwd(
        proj, p["gain"][3], raw, btot, dycat, seq, f"sb_bwd_{l}",
        items=(_rows_item(g_wgu, (pieces - 1) * piece, piece, p_gu), send_out))
    dproj = jnp.concatenate([duv, dp, dq, dkv.astype(BF16), dqd, dkd.astype(BF16), dvd.astype(BF16)], axis=1)
    g_win, = _mm_tn_cols(h1, dproj, GROUP_WIDTH, 4, f"in_proj_dw_{l}")
    defer = l > 0
    dx, g_norm_mix, p_in = _mm_norm_bwd(dproj, win, 0, 4, x, p["norm_mix"], dxm, f"in_proj_dx_{l}",
                                        items=(_rows_item(g_win, 0, in_half if defer else D_MODEL),))
    pending = (g_win, p_in) if defer else None
    ng = len(POOL_WINDOWS)
    gd = GROUP_WIDTH // ng
    small = dict(
        sgu_w=g_wm, sgu_b=g_bt.T,
        pool_w=jnp.stack([g_wbd[g * gd:(g + 1) * gd, g * gd:(g + 1) * gd] for g in range(ng)]),
        pool_scale=g_scale[0], swa_sinks=g_sinks[0],
        mix_out_gain=jnp.concatenate([g_ga[0], g_gb[0], g_gc[0], g_gd[0]]),
        norm_mix=g_norm_mix[0], norm_ffn=g_norm_ffn[0])
    return dx, [p_in, p_out, p_gu, p_down], small, g_tab, rode, p_in_above, pending


def _local_step(x, target, shards, sgu_w, sgu_b, pool_w, pool_scale, swa_sinks, rel_bias, mix_out_gain,
                norm_mix, norm_ffn, norm_final, seq):
    buckets = _bucket_map()
    bufs = {}
    first = [(0, "in", None)]
    tab, *got = _bias_table(rel_bias, buckets, "bias_table", items=_planned(first, shards, bufs))
    bufs.update({(lyr, t): r for (lyr, t, _), r in zip(first, got)})
    params = [_layer_params(l, sgu_w, sgu_b, pool_w, pool_scale, swa_sinks, mix_out_gain, norm_mix, norm_ffn)
              for l in range(DEPTH)]
    saved, weights = [], []
    for l in range(DEPTH):
        x, s = _layer_fwd(l, x, params[l], bufs, tab, seq, shards)
        saved.append(s)
        weights.append(_as_weights(*[bufs[(l, t)] for t in WEIGHTS]))
    loss, dx, g_final = _loss_head(x, norm_final[None], target, "loss_head")
    big, small, g_tab, upper, pending = [None] * DEPTH, [None] * DEPTH, None, None, None
    for l in reversed(range(DEPTH)):
        ride = ()
        if l == 0:
            mine = _pack(_upper_layers(small), _pack_rows(_upper_layers(small)))
            ride = (_Item(mine, lambda src, k: src, (N_DEV,) + mine.shape),)
        dx, big[l], small[l], t, rode, done_above, pending = _layer_bwd(
            l, dx, saved[l], params[l], weights[l], tab, seq, pending, ride)
        g_tab = t if g_tab is None else g_tab + t
        upper = rode[0] if rode else upper
        if done_above:
            big[l + 1][0] = done_above[0]
    return loss, dx, big, small, _bias_table_bwd(g_tab, buckets, "bias_table_bwd"), g_final[0], upper


def _cast_pad(w, rows, name):
    L, r, c = w.shape

    def body(w_ref, o_ref):
        if rows != r:
            o_ref[...] = jnp.zeros_like(o_ref)
        o_ref[:r, :] = w_ref[...].astype(BF16)

    return pl.pallas_call(
        body, grid=(L,), name=name,
        in_specs=[pl.BlockSpec((None, r, c), lambda l: (l, 0, 0))],
        out_specs=pl.BlockSpec((None, rows, c), lambda l: (l, 0, 0)),
        out_shape=jax.ShapeDtypeStruct((L, rows, c), BF16),
        compiler_params=_cp("parallel"))(w)


def _adamw(w, g, m, v):
    m = ADAM_B1 * m + (1.0 - ADAM_B1) * g
    v = ADAM_B2 * v + (1.0 - ADAM_B2) * jnp.square(g)
    m_hat = m / (1.0 - ADAM_B1 ** ADAM_STEP)
    v_hat = v / (1.0 - ADAM_B2 ** ADAM_STEP)
    delta = -ADAM_LR * (m_hat / (jnp.sqrt(v_hat) + ADAM_EPS) + ADAM_WD * w)
    return delta, m, v


def _adamw_sharded(parts, w, m, v, tr, name, items=()):
    L, r, c = w.shape
    cp = parts[0].shape[-1]
    nrow = r // tr

    def body(*refs):
        p_refs, (w_ref, m_ref, v_ref, g_ref, d_ref, nm_ref, nv_ref) = refs[:L], refs[L:]
        for k in range(L):
            @pl.when(pl.program_id(0) == k)
            def _(p_ref=p_refs[k]):
                g = p_ref[0, :, :c].astype(F32)
                for dev in range(1, N_DEV):
                    g = g + p_ref[dev, :, :c].astype(F32)
                delta, nm, nv = _adamw(w_ref[...], g, m_ref[...], v_ref[...])
                g_ref[...] = g
                d_ref[...] = delta
                nm_ref[...] = nm
                nv_ref[...] = nv

    def part_spec(k):
        return pl.BlockSpec((N_DEV, tr, cp),
                            lambda l, i: (0, jnp.where(l == k, i, jnp.where(l < k, 0, nrow - 1)), 0))

    blk = pl.BlockSpec((None, tr, c), lambda l, i: (l, i, 0))
    out = jax.ShapeDtypeStruct((L, r, c), F32)
    return _call(
        body, (*parts, w, m, v), grid=(L, nrow), name=name, items=items,
        in_specs=[part_spec(k) for k in range(L)] + [blk, blk, blk],
        out_specs=[blk] * 4, out_shape=[out] * 4, sem=("arbitrary", "arbitrary"))


def _adamw_small(parts, wmv, name):
    def body(p_ref, wmv_ref, g_ref, d_ref, nm_ref, nv_ref):
        g = p_ref[0]
        for k in range(1, N_DEV):
            g = g + p_ref[k]
        delta, nm, nv = _adamw(wmv_ref[0], g, wmv_ref[1], wmv_ref[2])
        g_ref[...] = g
        d_ref[...] = delta
        nm_ref[...] = nm
        nv_ref[...] = nv

    out = jax.ShapeDtypeStruct(wmv.shape[1:], F32)
    return pl.pallas_call(body, name=name, out_shape=[out] * 4, compiler_params=_cp())(parts, wmv)


LAYERED = ("sgu_w", "sgu_b", "pool_w", "pool_scale", "swa_sinks", "mix_out_gain", "norm_mix", "norm_ffn")
SHARED = ("rel_bias", "norm_final")


def _seg_rows(a):
    return -(-a.size // 128)


def _pack_rows(parts):
    return -(-sum(_seg_rows(p) for p in parts) // 8) * 8


def _upper_layers(per_layer):
    if isinstance(per_layer, dict):
        return [per_layer[k][1:] for k in LAYERED]
    return [jnp.stack([per_layer[l][k] for l in range(1, DEPTH)]) for k in LAYERED]


def _layer_zero(stacked):
    return [stacked[k][:1] for k in LAYERED] + [stacked[k] for k in SHARED]


def _pack(parts, rows):
    segs = [jnp.pad(p.reshape(-1), (0, _seg_rows(p) * 128 - p.size)).reshape(_seg_rows(p), 128) for p in parts]
    used = sum(s.shape[0] for s in segs)
    return jnp.concatenate(segs + [jnp.zeros((rows - used, 128), F32)], axis=0)


def _pack_groups(groups, rows):
    segs = []
    for parts in groups:
        segs += [jnp.pad(p.reshape(-1), (0, _seg_rows(p) * 128 - p.size)).reshape(_seg_rows(p), 128) for p in parts]
        segs.append(jnp.zeros((rows - sum(_seg_rows(p) for p in parts), 128), F32))
    return jnp.concatenate(segs, axis=0).reshape(len(groups), rows, 128)


def _unpack(buf, like):
    out, at = [], 0
    for a in like:
        out.append(buf[at:at + _seg_rows(a)].reshape(-1)[:a.size].reshape(a.shape))
        at += _seg_rows(a)
    return out


def kernel(x, w_in, w_out, sgu_w, sgu_b, pool_w, pool_scale, swa_sinks, rel_bias, mix_out_gain, norm_mix, norm_ffn, w_gate_up, w_down, norm_final, loss_target, m_w_in, m_w_out, m_sgu_w, m_sgu_b, m_pool_w, m_pool_scale, m_swa_sinks, m_rel_bias, m_mix_out_gain, m_norm_mix, m_norm_ffn, m_w_gate_up, m_w_down, m_norm_final, v_w_in, v_w_out, v_sgu_w, v_sgu_b, v_pool_w, v_pool_scale, v_swa_sinks, v_rel_bias, v_mix_out_gain, v_norm_mix, v_norm_ffn, v_w_gate_up, v_w_down, v_norm_final):
    bl, seq, _ = x.shape
    L = w_in.shape[0]
    gu_t, m_gu_t, v_gu_t = (jnp.swapaxes(a, 1, 2) for a in (w_gate_up, m_w_gate_up, v_w_gate_up))
    shards = (_cast_pad(w_in, D_MODEL, "shard_w_in"), _cast_pad(w_out, D_MODEL // N_DEV, "shard_w_out"),
              _cast_pad(gu_t, FF_PAD, "shard_w_gate_up"), _cast_pad(w_down, DOWN_ROWS, "shard_w_down"))
    small_w = dict(sgu_w=sgu_w, sgu_b=sgu_b, pool_w=pool_w, pool_scale=pool_scale, swa_sinks=swa_sinks,
                   rel_bias=rel_bias, mix_out_gain=mix_out_gain, norm_mix=norm_mix, norm_ffn=norm_ffn,
                   norm_final=norm_final)
    small_m = dict(sgu_w=m_sgu_w, sgu_b=m_sgu_b, pool_w=m_pool_w, pool_scale=m_pool_scale, swa_sinks=m_swa_sinks,
                   rel_bias=m_rel_bias, mix_out_gain=m_mix_out_gain, norm_mix=m_norm_mix, norm_ffn=m_norm_ffn,
                   norm_final=m_norm_final)
    small_v = dict(sgu_w=v_sgu_w, sgu_b=v_sgu_b, pool_w=v_pool_w, pool_scale=v_pool_scale, swa_sinks=v_swa_sinks,
                   rel_bias=v_rel_bias, mix_out_gain=v_mix_out_gain, norm_mix=v_norm_mix, norm_ffn=v_norm_ffn,
                   norm_final=v_norm_final)
    loss, dx, big, small, g_rel_bias, g_final, upper = _local_step(
        x.reshape(bl * seq, D_MODEL), loss_target.reshape(bl * seq, D_MODEL), shards, sgu_w, sgu_b, pool_w,
        pool_scale, swa_sinks, rel_bias, mix_out_gain, norm_mix, norm_ffn, norm_final, seq)
    p_in, p_out, p_gu, p_down = ([big[l][t] for l in range(L)] for t in range(4))
    outs_gu = [jnp.swapaxes(a, 1, 2) for a in
               _adamw_sharded(p_gu, gu_t, m_gu_t, v_gu_t, FF_SHARD // 4, "adamw_w_gate_up")]
    outs_down = _adamw_sharded(p_down, w_down, m_w_down, v_w_down, DOWN_ROWS // 2, "adamw_w_down")
    outs_in = _adamw_sharded(p_in, w_in, m_w_in, v_w_in, 256, "adamw_w_in")
    lo_like = _layer_zero(small_w)
    lo_rows = _pack_rows(lo_like + [loss[0]])
    lo_mine = _pack([small[0][k][None] for k in LAYERED] + [g_rel_bias, g_final, loss[0]], lo_rows)
    *outs_out, lower = _adamw_sharded(p_out, w_out, m_w_out, v_w_out, D_MODEL // N_DEV, "adamw_w_out",
                                      items=(_Item(lo_mine, lambda src, k: src, (N_DEV,) + lo_mine.shape),))
    lo_res = _adamw_small(lower, _pack_groups([lo_like, _layer_zero(small_m), _layer_zero(small_v)], lo_rows),
                          "adamw_small_layer0")
    hi_like = _upper_layers(small_w)
    hi_rows = _pack_rows(hi_like)
    hi_res = _adamw_small(upper, _pack_groups([hi_like, _upper_layers(small_m), _upper_layers(small_v)], hi_rows),
                          "adamw_small_upper")
    loss_total = lo_res[0][sum(_seg_rows(a) for a in lo_like), 0]
    small_outs = []
    for lo_buf, hi_buf in zip(lo_res, hi_res):
        lo = dict(zip(LAYERED + SHARED, _unpack(lo_buf, lo_like)))
        hi = dict(zip(LAYERED, _unpack(hi_buf, hi_like)))
        small_outs.append({k: jnp.concatenate([lo[k], hi[k]], axis=0) if k in hi else lo[k] for k in lo})
    big_outs = dict(w_in=outs_in, w_out=outs_out, w_gate_up=outs_gu, w_down=outs_down)
    order = ("w_in", "w_out", "sgu_w", "sgu_b", "pool_w", "pool_scale", "swa_sinks", "rel_bias", "mix_out_gain",
             "norm_mix", "norm_ffn", "w_gate_up", "w_down", "norm_final")
    result = [loss_total, dx.reshape(bl, seq, D_MODEL)]
    for which in range(4):
        for name in order:
            result.append(big_outs[name][which] if name in big_outs else small_outs[which][name])
    return tuple(result)
```

```python
import functools

import jax
import jax.numpy as jnp
from jax import lax
from jax.experimental import pallas as pl
from jax.experimental.pallas import tpu as pltpu

F32 = jnp.float32
BF16 = jnp.bfloat16

N_DEV = 8
DEPTH = 4
D_MODEL = 1024
GROUP_WIDTH = 256
HEAD_DIM = 64
GROUP_HEADS = 4
BLOCK = 128
N_BUCKETS = 32
MAX_DISTANCE = 128
POOL_WINDOWS = (2, 4, 8, 16)
D_FF = 2816
FF_SHARD = D_FF // 4
FF_PAD = 768
D_FF_PAD = 4 * FF_PAD
EPS = 1e-6
ATT_SCALE = HEAD_DIM ** -0.5
ADAM_LR = 0.001
ADAM_B1 = 0.9
ADAM_B2 = 0.999
ADAM_EPS = 1e-08
ADAM_WD = 0.01
ADAM_STEP = 10
VMEM_LIMIT = 56 * 1024 * 1024
MESH_AXES = ("x", "y", "c")


def _cp(*sem):
    return pltpu.CompilerParams(dimension_semantics=sem or None, vmem_limit_bytes=VMEM_LIMIT)


_NT = (((1,), (1,)), ((), ()))
_TN = (((0,), (0,)), ((), ()))


@jax.custom_vjp
def _bdot(a, b):
    return jnp.dot(a.astype(BF16), b.astype(BF16), preferred_element_type=F32)


def _bdot_fwd(a, b):
    return _bdot(a, b), (a.astype(BF16), b.astype(BF16))


def _bdot_bwd(res, ct):
    a, b = res
    c = ct.astype(BF16)
    return (lax.dot_general(c, b, _NT, preferred_element_type=F32),
            lax.dot_general(a, c, _TN, preferred_element_type=F32))


_bdot.defvjp(_bdot_fwd, _bdot_bwd)


@jax.custom_vjp
def _bdot_nt(a, b):
    return lax.dot_general(a.astype(BF16), b.astype(BF16), _NT, preferred_element_type=F32)


def _bdot_nt_fwd(a, b):
    return _bdot_nt(a, b), (a.astype(BF16), b.astype(BF16))


def _bdot_nt_bwd(res, ct):
    a, b = res
    c = ct.astype(BF16)
    return (jnp.dot(c, b, preferred_element_type=F32),
            lax.dot_general(c, a, _TN, preferred_element_type=F32))


_bdot_nt.defvjp(_bdot_nt_fwd, _bdot_nt_bwd)


def _rms(x, g):
    return x * lax.rsqrt(jnp.mean(x * x, axis=-1, keepdims=True) + EPS) * g


def _sigmoid(x):
    return 0.5 * jnp.tanh(0.5 * x) + 0.5


def _split_dot(x, u):
    hi = x.astype(BF16)
    lo = (x - hi.astype(F32)).astype(BF16)
    return jnp.dot(hi, u, preferred_element_type=F32) + jnp.dot(lo, u, preferred_element_type=F32)


def _head_mask(h, shape):
    col = lax.broadcasted_iota(jnp.int32, shape, 1)
    return (col >= h * HEAD_DIM) & (col < (h + 1) * HEAD_DIM)


ANY = pl.BlockSpec(memory_space=pl.ANY)
MESH = pl.DeviceIdType.MESH
DOWN_ROWS = D_FF // N_DEV


def _my_place():
    x, y, c = (lax.axis_index(a) for a in MESH_AXES)
    return x, y, c, 4 * x + 2 * y + c


def _peer(x, y, c, d):
    return (x ^ (d >> 2), y ^ ((d >> 1) & 1), c ^ (d & 1))


class _Item:
    def __init__(self, src, block, dst_shape, place=None, init=None, relayed=False):
        self.src, self.block, self.dst_shape, self.init, self.relayed = src, block, dst_shape, init, relayed
        self.place = place or (lambda dst, k: dst.at[k])


def _call(body, args, *, grid, in_specs, out_specs, out_shape, sem, name, scratch_shapes=(), items=()):
    if not items:
        outs = pl.pallas_call(body, grid=grid, in_specs=in_specs, out_specs=out_specs, out_shape=out_shape, name=name,
                              scratch_shapes=list(scratch_shapes), compiler_params=_cp(*sem))(*args)
        return list(outs) if isinstance(outs, (list, tuple)) else [outs]
    n_in, n_out, n_scr, n = len(in_specs), len(out_specs), len(scratch_shapes), len(items)
    inits = [i for i, it in enumerate(items) if it.init is not None]

    def wrapped(*refs):
        core_in, srcs = refs[:n_in], refs[n_in:n_in + n]
        off = n_in + n + len(inits)
        core_out, dsts = refs[off:off + n_out], refs[off + n_out:off + n_out + n]
        scratch = refs[off + n_out + n:]
        send_sems, recv_sems, local_sems = scratch[n_scr:]
        step = pl.program_id(0)
        for a in range(1, len(grid)):
            step = step * grid[a] + pl.program_id(a)
        steps = functools.reduce(lambda p, g: p * g, grid)
        relay_step = max(0, steps - 1 - max(1, steps // 8))
        x, y, c, me = _my_place()
        direct = [i for i in range(n) if not items[i].relayed]
        relayed = [i for i in range(n) if items[i].relayed]
        chips = [(1 - x, y), (x, 1 - y), (1 - x, 1 - y)]
        sibling = (x, y, 1 - c)

        def local(i):
            return pltpu.make_async_copy(items[i].block(srcs[i], me), items[i].place(dsts[i], me), local_sems.at[i])

        def remote(d, i, sending):
            px, py, pc = _peer(x, y, c, d)
            pk = 4 * px + 2 * py + pc
            return pltpu.make_async_remote_copy(
                src_ref=items[i].block(srcs[i], pk), dst_ref=items[i].place(dsts[i], me if sending else pk),
                send_sem=send_sems.at[d - 1, i], recv_sem=recv_sems.at[d - 1, i],
                device_id=(px, py, pc), device_id_type=MESH)

        def hop(slot, i, owner, to, from_src):
            k = 4 * owner[0] + 2 * owner[1] + owner[2]
            src = items[i].block(srcs[i], me) if from_src else items[i].place(dsts[i], k)
            return pltpu.make_async_remote_copy(
                src_ref=src, dst_ref=items[i].place(dsts[i], k), send_sem=send_sems.at[slot, i],
                recv_sem=recv_sems.at[slot, i], device_id=to, device_id_type=MESH)

        @pl.when(step == 0)
        def _():
            for i in range(n):
                local(i).start()
            for d in range(1, N_DEV):
                for i in direct:
                    remote(d, i, True).start()
            for i in relayed:
                hop(0, i, (x, y, c), sibling, True).start()
                for j, chip in enumerate(chips):
                    hop(1 + j, i, (x, y, c), (*chip, c), True).start()

        body(*core_in, *core_out, *scratch[:n_scr])

        if relayed:
            @pl.when(step == relay_step)
            def _():
                for i in relayed:
                    for j, chip in enumerate(chips):
                        hop(1 + j, i, (*chip, c), (x, y, c), False).wait_recv()
                        hop(4 + j, i, (*chip, c), sibling, False).start()

        @pl.when(step == steps - 1)
        def _():
            for d in range(1, N_DEV):
                for i in direct:
                    remote(d, i, False).wait_recv()
            for i in relayed:
                hop(0, i, sibling, (x, y, c), False).wait_recv()
                for j, chip in enumerate(chips):
                    hop(4 + j, i, (*chip, 1 - c), (x, y, c), False).wait_recv()
            for d in range(1, N_DEV):
                for i in direct:
                    remote(d, i, True).wait_send()
            for i in relayed:
                for slot in range(N_DEV - 1):
                    hop(slot, i, (x, y, c), sibling, True).wait_send()
            for i in range(n):
                local(i).wait()

    outs = pl.pallas_call(
        wrapped, grid=grid, name=name,
        in_specs=list(in_specs) + [ANY] * (n + len(inits)), out_specs=list(out_specs) + [ANY] * n,
        out_shape=list(out_shape) + [jax.ShapeDtypeStruct(it.dst_shape, it.src.dtype) for it in items],
        input_output_aliases={n_in + n + j: n_out + i for j, i in enumerate(inits)},
        scratch_shapes=list(scratch_shapes) + [pltpu.SemaphoreType.DMA((N_DEV - 1, n)),
                                               pltpu.SemaphoreType.DMA((N_DEV - 1, n)), pltpu.SemaphoreType.DMA((n,))],
        compiler_params=_cp(*(["arbitrary"] * len(grid))),
    )(*args, *[it.src for it in items], *[items[i].init for i in inits])
    return list(outs)


def _norm_mm(x, g, w, l, jb, out_dtype, name, tm=1024, items=()):
    T, K = x.shape
    _, nb, _, tn = w.shape
    tm = min(tm, T)

    def body(x_ref, g_ref, w_ref, o_ref, h_ref):
        @pl.when(pl.program_id(1) == 0)
        def _():
            h_ref[...] = _rms(x_ref[...], g_ref[...]).astype(BF16)
        h = h_ref[...]
        for jj in range(jb):
            o_ref[:, jj * tn:(jj + 1) * tn] = jnp.dot(h, w_ref[jj], preferred_element_type=F32).astype(o_ref.dtype)

    return _call(
        body, (x, g, w), grid=(T // tm, nb // jb), name=name, items=items,
        in_specs=[pl.BlockSpec((tm, K), lambda i, j: (i, 0)), pl.BlockSpec((1, K), lambda i, j: (0, 0)),
                  pl.BlockSpec((None, jb, K, tn), lambda i, j: (l, j, 0, 0))],
        out_specs=[pl.BlockSpec((tm, jb * tn), lambda i, j: (i, j)), pl.BlockSpec((tm, K), lambda i, j: (i, 0))],
        out_shape=[jax.ShapeDtypeStruct((T, nb * tn), out_dtype), jax.ShapeDtypeStruct((T, K), BF16)],
        sem=("parallel", "arbitrary"))


def _mm_res(res, a, w, l, name, tn, tm=1024, items=()):
    T, K = a.shape
    N = w.shape[2]
    tm = min(tm, T)

    def body(r_ref, a_ref, w_ref, o_ref):
        o_ref[...] = r_ref[...] + jnp.dot(a_ref[...], w_ref[...], preferred_element_type=F32)

    return _call(
        body, (res, a, w), grid=(T // tm, N // tn), name=name, items=items,
        in_specs=[pl.BlockSpec((tm, tn), lambda i, j: (i, j)), pl.BlockSpec((tm, K), lambda i, j: (i, 0)),
                  pl.BlockSpec((None, K, tn), lambda i, j: (l, 0, j))],
        out_specs=[pl.BlockSpec((tm, tn), lambda i, j: (i, j))],
        out_shape=[jax.ShapeDtypeStruct((T, N), F32)],
        sem=("parallel", "parallel"))


def _mm_nt(a, w, l, out_dtype, name, tm=1024, tn=1024):
    T, K = a.shape
    N = w.shape[1]
    tm = min(tm, T)

    def body(a_ref, w_ref, o_ref):
        o_ref[...] = lax.dot_general(a_ref[...].astype(BF16), w_ref[...], _NT,
                                     preferred_element_type=F32).astype(o_ref.dtype)

    return pl.pallas_call(
        body, grid=(T // tm, N // tn), name=name,
        in_specs=[pl.BlockSpec((tm, K), lambda i, j: (i, 0)), pl.BlockSpec((None, tn, K), lambda i, j: (l, j, 0))],
        out_specs=pl.BlockSpec((tm, tn), lambda i, j: (i, j)),
        out_shape=jax.ShapeDtypeStruct((T, N), out_dtype),
        compiler_params=_cp("parallel", "parallel"))(a, w)


def _mm_norm_bwd(a, w, l, jb, x, g, dres, name, tm=1024, items=(), block_of=lambda j: j, transposed=False):
    T = a.shape[0]
    _, nb, K, tn = w.shape
    if transposed:
        tn, K = K, tn
    tm = min(tm, T)
    nj = nb // jb
    sub = min(256, tm)
    dims = (((1,), (0,)), ((), ())) if transposed else _NT

    def body(a_ref, w_ref, x_ref, g_ref, r_ref, o_ref, dg_ref):
        part = lax.dot_general(a_ref[:, :tn], w_ref[0], dims, preferred_element_type=F32)
        for jj in range(1, jb):
            part += lax.dot_general(a_ref[:, jj * tn:(jj + 1) * tn], w_ref[jj], dims, preferred_element_type=F32)

        @pl.when(pl.program_id(1) == 0)
        def _():
            o_ref[...] = part

        @pl.when(pl.program_id(1) > 0)
        def _():
            o_ref[...] += part

        @pl.when(pl.program_id(1) == nj - 1)
        def _():
            dg = jnp.zeros((1, K), F32)
            for r in range(tm // sub):
                rows = pl.ds(r * sub, sub)
                _, vjp = jax.vjp(_rms, x_ref[rows, :], g_ref[...])
                dx, dg_r = vjp(o_ref[rows, :])
                o_ref[rows, :] = r_ref[rows, :] + dx
                dg = dg + dg_r
            _acc_out(pl.program_id(0) == 0, (dg_ref,), (dg,))

    row = pl.BlockSpec((tm, K), lambda i, j: (i, 0))
    vec = pl.BlockSpec((1, K), lambda i, j: (0, 0))
    return _call(
        body, (a, w, x, g, dres), grid=(T // tm, nj), name=name, items=items,
        in_specs=[pl.BlockSpec((tm, jb * tn), lambda i, j: (i, j)),
                  pl.BlockSpec((None, jb) + w.shape[2:], lambda i, j: (l, block_of(j), 0, 0)), row, vec, row],
        out_specs=[row, vec],
        out_shape=[jax.ShapeDtypeStruct((T, K), F32), jax.ShapeDtypeStruct((1, K), F32)],
        sem=("arbitrary", "arbitrary"))


def _mm_tn_cols(lhs, rhs, tn, jb, name, tm=2048, items=(), block_of=lambda j: j):
    T, K = lhs.shape
    nb = rhs.shape[1] // tn
    tm = min(tm, T)
    nt = T // tm

    def body(l_ref, r_ref, o_ref, acc):
        part = lax.dot_general(l_ref[...], r_ref[...], _TN, preferred_element_type=F32)

        @pl.when(pl.program_id(1) == 0)
        def _():
            acc[...] = part

        @pl.when(pl.program_id(1) > 0)
        def _():
            acc[...] += part

        @pl.when(pl.program_id(1) == nt - 1)
        def _():
            for jj in range(jb):
                o_ref[jj] = acc[:, jj * tn:(jj + 1) * tn].astype(BF16)

    return _call(
        body, (lhs, rhs), grid=(nb // jb, nt), name=name, items=items,
        in_specs=[pl.BlockSpec((tm, K), lambda j, t: (t, 0)), pl.BlockSpec((tm, jb * tn), lambda j, t: (t, j))],
        out_specs=[pl.BlockSpec((jb, K, tn), lambda j, t: (block_of(j), 0, 0))],
        out_shape=[jax.ShapeDtypeStruct((nb, K, tn), BF16)],
        scratch_shapes=[pltpu.VMEM((K, jb * tn), F32)],
        sem=("parallel", "arbitrary"))


def _mm_tn_rows(lhs, rhs, tk, name, tm=2048, items=(), block_of=lambda j: j):
    T, Kl = lhs.shape
    N = rhs.shape[1]
    tm = min(tm, T)
    nt = T // tm

    def body(l_ref, r_ref, o_ref, acc):
        part = lax.dot_general(l_ref[...], r_ref[...].astype(BF16), _TN, preferred_element_type=F32)

        @pl.when(pl.program_id(1) == 0)
        def _():
            acc[...] = part

        @pl.when(pl.program_id(1) > 0)
        def _():
            acc[...] += part

        @pl.when(pl.program_id(1) == nt - 1)
        def _():
            o_ref[...] = acc[...].astype(BF16)

    return _call(
        body, (lhs, rhs), grid=(Kl // tk, nt), name=name, items=items,
        in_specs=[pl.BlockSpec((tm, tk), lambda l, t: (t, l)), pl.BlockSpec((tm, N), lambda l, t: (t, 0))],
        out_specs=[pl.BlockSpec((tk, N), lambda l, t: (block_of(l), 0))],
        out_shape=[jax.ShapeDtypeStruct((Kl, N), BF16)],
        scratch_shapes=[pltpu.VMEM((tk, N), F32)],
        sem=("parallel", "arbitrary"))


N_FF_CHUNK = D_FF_PAD // FF_PAD


def _ffn_chunk_block(j):
    return (j % 2) * N_FF_CHUNK + j // 2


def _ffn_up_act(x, g, w, name, tm=1024, items=()):
    T, K = x.shape
    tm = min(tm, T)

    def body(x_ref, g_ref, wg_ref, wu_ref, gu_ref, act_ref, h_ref):
        @pl.when(pl.program_id(1) == 0)
        def _():
            h_ref[...] = _rms(x_ref[...], g_ref[...]).astype(BF16)
        h = h_ref[...]
        gate = lax.dot_general(h, wg_ref[...], _NT, preferred_element_type=F32)
        up = lax.dot_general(h, wu_ref[...], _NT, preferred_element_type=F32)
        gu_ref[:, :FF_PAD] = gate.astype(BF16)
        gu_ref[:, FF_PAD:] = up.astype(BF16)
        act_ref[...] = (gate * _sigmoid(gate) * up).astype(BF16)

    return _call(
        body, (x, g, w, w), grid=(T // tm, N_FF_CHUNK), name=name, items=items,
        in_specs=[pl.BlockSpec((tm, K), lambda i, j: (i, 0)), pl.BlockSpec((1, K), lambda i, j: (0, 0)),
                  pl.BlockSpec((None, None, FF_PAD, K), lambda i, j: (0, j, 0, 0)),
                  pl.BlockSpec((None, None, FF_PAD, K), lambda i, j: (0, j + N_FF_CHUNK, 0, 0))],
        out_specs=[pl.BlockSpec((tm, 2 * FF_PAD), lambda i, j: (i, j)), pl.BlockSpec((tm, FF_PAD), lambda i, j: (i, j)),
                   pl.BlockSpec((tm, K), lambda i, j: (i, 0))],
        out_shape=[jax.ShapeDtypeStruct((T, 2 * D_FF_PAD), BF16), jax.ShapeDtypeStruct((T, D_FF_PAD), BF16),
                   jax.ShapeDtypeStruct((T, K), BF16)],
        sem=("parallel", "arbitrary"))


def _ffn_down_dx(dxo, w, gu, name, tm=1024, items=()):
    T, K = dxo.shape
    tm = min(tm, T)

    sub = min(512, tm)

    def body(d_ref, w_ref, gu_ref, o_ref):
        w = w_ref[...]
        for r in range(tm // sub):
            rows = slice(r * sub, (r + 1) * sub)
            d = lax.dot_general(d_ref[rows, :].astype(BF16), w, _NT, preferred_element_type=F32)
            gate = gu_ref[rows, :FF_PAD].astype(F32)
            up = gu_ref[rows, FF_PAD:].astype(F32)
            sig = _sigmoid(gate)
            silu = gate * sig
            o_ref[rows, :FF_PAD] = (d * up * (sig + silu * (1.0 - sig))).astype(BF16)
            o_ref[rows, FF_PAD:] = (d * silu).astype(BF16)

    return _call(
        body, (dxo, w, gu), grid=(T // tm, N_FF_CHUNK), name=name, items=items,
        in_specs=[pl.BlockSpec((tm, K), lambda i, j: (i, 0)), pl.BlockSpec((None, FF_PAD, K), lambda i, j: (0, j, 0)),
                  pl.BlockSpec((tm, 2 * FF_PAD), lambda i, j: (i, j))],
        out_specs=[pl.BlockSpec((tm, 2 * FF_PAD), lambda i, j: (i, j))],
        out_shape=[jax.ShapeDtypeStruct((T, 2 * D_FF_PAD), BF16)],
        sem=("parallel", "parallel"))


def _loss_head(x, g, target, name, tm=512):
    T, K = x.shape

    def loss_fn(xv, gv, tv):
        err = _rms(xv, gv) - tv
        return 0.5 * jnp.sum(jnp.mean(err * err, axis=-1, keepdims=True), axis=0, keepdims=True)

    def body(x_ref, g_ref, t_ref, l_ref, dx_ref, dg_ref):
        val, vjp = jax.vjp(lambda xv, gv: loss_fn(xv, gv, t_ref[...]), x_ref[...], g_ref[...])
        dx, dg = vjp(jnp.ones((1, 1), F32))
        dx_ref[...] = dx
        lval = jnp.broadcast_to(val, (1, 128))

        @pl.when(pl.program_id(0) == 0)
        def _():
            dg_ref[...] = dg
            l_ref[...] = lval

        @pl.when(pl.program_id(0) > 0)
        def _():
            dg_ref[...] += dg
            l_ref[...] += lval

    row = pl.BlockSpec((tm, K), lambda i: (i, 0))
    vec = pl.BlockSpec((1, K), lambda i: (0, 0))
    return pl.pallas_call(
        body, grid=(T // tm,), name=name,
        in_specs=[row, vec, row], out_specs=[pl.BlockSpec((1, 128), lambda i: (0, 0)), row, vec],
        out_shape=[jax.ShapeDtypeStruct((1, 128), F32), jax.ShapeDtypeStruct((T, K), F32),
                   jax.ShapeDtypeStruct((1, K), F32)],
        compiler_params=_cp("arbitrary"))(x, g, target)


SGU_ROWS = 8 * BLOCK


def _sgu_fn(u, v, wm, bt, gain):
    ug = jax.nn.gelu(u)
    vg = jax.nn.gelu(v)
    row = lax.broadcasted_iota(jnp.int32, (BLOCK, BLOCK), 0)
    col = lax.broadcasted_iota(jnp.int32, (BLOCK, BLOCK), 1)
    tri = row >= col
    normed = []
    for h in range(GROUP_HEADS):
        vh = vg[:, h * HEAD_DIM:(h + 1) * HEAD_DIM]
        xc = vh - jnp.mean(vh, axis=-1, keepdims=True)
        normed.append(xc * lax.rsqrt(jnp.mean(xc * xc, axis=-1, keepdims=True) + EPS))
    vn = jnp.concatenate(normed, axis=1)
    wcat = jnp.concatenate([jnp.where(tri, wm[h], 0.0) for h in range(GROUP_HEADS)], axis=1)
    bias = jnp.concatenate([jnp.broadcast_to(bt[:, h:h + 1], (BLOCK, HEAD_DIM)) for h in range(GROUP_HEADS)], axis=1)
    mixes = []
    for c in range(u.shape[0] // BLOCK):
        chunk = vn[c * BLOCK:(c + 1) * BLOCK]
        stacked = jnp.concatenate([jnp.where(_head_mask(h, chunk.shape), chunk, 0.0) for h in range(GROUP_HEADS)], axis=0)
        mixes.append(_bdot(wcat, stacked) + bias)
    return _rms(ug * jnp.concatenate(mixes, axis=0), gain)


def _sgu_fwd(proj, wm, bt, gain, name, items=()):
    T = proj.shape[0]
    rows = min(SGU_ROWS, T)

    def body(u_ref, v_ref, w_ref, b_ref, g_ref, o_ref):
        o_ref[...] = _sgu_fn(u_ref[...], v_ref[...], w_ref[...], b_ref[...], g_ref[...]).astype(BF16)

    full = lambda shape: pl.BlockSpec(shape, lambda i: (0,) * len(shape))
    return _call(
        body, (proj, proj, wm, bt, gain), grid=(T // rows,), name=name, items=items,
        in_specs=[pl.BlockSpec((rows, GROUP_WIDTH), lambda i: (i, 0)), pl.BlockSpec((rows, GROUP_WIDTH), lambda i: (i, 1)),
                  full(wm.shape), full(bt.shape), full(gain.shape)],
        out_specs=[pl.BlockSpec((rows, GROUP_WIDTH), lambda i: (i, 0))],
        out_shape=[jax.ShapeDtypeStruct((T, GROUP_WIDTH), BF16)],
        sem=("parallel",))


def _acc_out(first, refs, vals):
    @pl.when(first)
    def _():
        for r, v in zip(refs, vals):
            r[...] = v

    @pl.when(jnp.logical_not(first))
    def _():
        for r, v in zip(refs, vals):
            r[...] += v


def _sgu_bwd(proj, wm, bt, gain, dy, name, items=()):
    T = proj.shape[0]

    def body(u_ref, v_ref, w_ref, b_ref, g_ref, dy_ref, duv_ref, dw_ref, db_ref, dg_ref):
        _, vjp = jax.vjp(_sgu_fn, u_ref[...], v_ref[...], w_ref[...], b_ref[...], g_ref[...])
        du, dv, dw, db, dg = vjp(dy_ref[...])
        duv_ref[:, :GROUP_WIDTH] = du.astype(BF16)
        duv_ref[:, GROUP_WIDTH:] = dv.astype(BF16)
        _acc_out(pl.program_id(0) == 0, (dw_ref, db_ref, dg_ref), (dw, db, dg))

    full = lambda shape: pl.BlockSpec(shape, lambda i: (0,) * len(shape))
    rows = min(SGU_ROWS, T)
    return _call(
        body, (proj, proj, wm, bt, gain, dy), grid=(T // rows,), name=name, items=items,
        in_specs=[pl.BlockSpec((rows, GROUP_WIDTH), lambda i: (i, 0)), pl.BlockSpec((rows, GROUP_WIDTH), lambda i: (i, 1)),
                  full(wm.shape), full(bt.shape), full(gain.shape),
                  pl.BlockSpec((rows, GROUP_WIDTH), lambda i: (i, 0))],
        out_specs=[pl.BlockSpec((rows, 2 * GROUP_WIDTH), lambda i: (i, 0)), full(wm.shape), full(bt.shape), full(gain.shape)],
        out_shape=[jax.ShapeDtypeStruct((T, 2 * GROUP_WIDTH), BF16), jax.ShapeDtypeStruct(wm.shape, F32),
                   jax.ShapeDtypeStruct(bt.shape, F32), jax.ShapeDtypeStruct(gain.shape, F32)],
        sem=("arbitrary",))


def _pool_consts(seq):
    t = lax.broadcasted_iota(jnp.int32, (seq, GROUP_WIDTH), 0)
    grp = lax.broadcasted_iota(jnp.int32, (seq, GROUP_WIDTH), 1) // (GROUP_WIDTH // len(POOL_WINDOWS))
    win = jnp.where(grp == 0, POOL_WINDOWS[0], jnp.where(grp == 1, POOL_WINDOWS[1],
                    jnp.where(grp == 2, POOL_WINDOWS[2], POOL_WINDOWS[3])))
    count = jnp.minimum(t + 1, win).astype(F32)
    return t, grp, count


def _by_group(grp, vals):
    return jnp.where(grp == 0, vals[0], jnp.where(grp == 1, vals[1], jnp.where(grp == 2, vals[2], vals[3])))


def _window_sums(x, t, seq, back):
    def shift(a, k):
        if back:
            return jnp.where(t >= k, pltpu.roll(a, k, 0), 0.0)
        return jnp.where(t < seq - k, pltpu.roll(a, seq - k, 0), 0.0)
    sums = []
    a, k = x, 1
    for _ in POOL_WINDOWS:
        a = a + shift(a, k)
        sums.append(a)
        k *= 2
    return sums


def _pool_tail(y, wbd, scale, gain):
    return _rms(_bdot(y, wbd) * scale, gain)


def _pool_fwd(proj, wbd, scale, gain, seq, name):
    T = proj.shape[0]

    def body(p_ref, w_ref, s_ref, g_ref, o_ref):
        p = p_ref[...]
        t, grp, count = _pool_consts(seq)
        y = _by_group(grp, _window_sums(p, t, seq, True)) / count - p
        o_ref[...] = _pool_tail(y, w_ref[...], s_ref[...], g_ref[...]).astype(BF16)

    full = lambda shape: pl.BlockSpec(shape, lambda b: (0,) * len(shape))
    return pl.pallas_call(
        body, grid=(T // seq,), name=name,
        in_specs=[pl.BlockSpec((seq, GROUP_WIDTH), lambda b: (b, 2)), full(wbd.shape), full(scale.shape), full(gain.shape)],
        out_specs=pl.BlockSpec((seq, GROUP_WIDTH), lambda b: (b, 0)),
        out_shape=jax.ShapeDtypeStruct((T, GROUP_WIDTH), BF16),
        compiler_params=_cp("parallel"))(proj, wbd, scale, gain)


def _pool_bwd(proj, wbd, scale, gain, dy, seq, name):
    T = proj.shape[0]

    def body(p_ref, w_ref, s_ref, g_ref, dy_ref, dp_ref, dw_ref, ds_ref, dg_ref):
        p = p_ref[...]
        t, grp, count = _pool_consts(seq)
        y = _by_group(grp, _window_sums(p, t, seq, True)) / count - p
        _, vjp = jax.vjp(_pool_tail, y, w_ref[...], s_ref[...], g_ref[...])
        d_y, dw, ds, dg = vjp(dy_ref[...])
        dp = _by_group(grp, _window_sums(d_y / count, t, seq, False)) - d_y
        dp_ref[...] = dp.astype(BF16)
        _acc_out(pl.program_id(0) == 0, (dw_ref, ds_ref, dg_ref), (dw, ds, dg))

    full = lambda shape: pl.BlockSpec(shape, lambda b: (0,) * len(shape))
    return pl.pallas_call(
        body, grid=(T // seq,), name=name,
        in_specs=[pl.BlockSpec((seq, GROUP_WIDTH), lambda b: (b, 2)), full(wbd.shape), full(scale.shape), full(gain.shape),
                  pl.BlockSpec((seq, GROUP_WIDTH), lambda b: (b, 1))],
        out_specs=[pl.BlockSpec((seq, GROUP_WIDTH), lambda b: (b, 0)), full(wbd.shape), full(scale.shape), full(gain.shape)],
        out_shape=[jax.ShapeDtypeStruct((T, GROUP_WIDTH), BF16), jax.ShapeDtypeStruct(wbd.shape, F32),
                   jax.ShapeDtypeStruct(scale.shape, F32), jax.ShapeDtypeStruct(gain.shape, F32)],
        compiler_params=_cp("arbitrary"))(proj, wbd, scale, gain, dy)


def _swa_fn(q, kv_prev, kv_cur, sinks, tab, gain, first):
    half = GROUP_WIDTH // 2
    k2 = jnp.concatenate([kv_prev[:, :half], kv_cur[:, :half]], axis=0)
    v2 = jnp.concatenate([kv_prev[:, half:], kv_cur[:, half:]], axis=0)
    per_query_head = lambda a: jnp.concatenate(
        [a[:, (h // 2) * HEAD_DIM:(h // 2 + 1) * HEAD_DIM] for h in range(GROUP_HEADS)], axis=1)
    qs = jnp.concatenate([jnp.where(_head_mask(h, q.shape), q, 0.0) for h in range(GROUP_HEADS)], axis=0)
    qi = lax.broadcasted_iota(jnp.int32, (HEAD_ROWS, 2 * BLOCK), 0) & (BLOCK - 1)
    kj = lax.broadcasted_iota(jnp.int32, (HEAD_ROWS, 2 * BLOCK), 1)
    dist = qi + BLOCK - kj
    mask = (dist >= 0) & (dist < BLOCK) & ((kj >= BLOCK) | jnp.logical_not(first))
    logits = _bdot_nt(qs, per_query_head(k2)) * ATT_SCALE + tab.reshape(HEAD_ROWS, 2 * BLOCK)
    logits = jnp.where(mask, logits, -1e30)
    sink = jnp.concatenate([jnp.broadcast_to(sinks[:, h:h + 1], (BLOCK, 1)) for h in range(GROUP_HEADS)], axis=0)
    m = lax.stop_gradient(jnp.maximum(jnp.max(logits, axis=1, keepdims=True), sink))
    p = jnp.exp(logits - m)
    probs = p / (jnp.sum(p, axis=1, keepdims=True) + jnp.exp(sink - m))
    out = _bdot(probs, per_query_head(v2))
    y = jnp.zeros_like(q)
    for h in range(GROUP_HEADS):
        y = y + jnp.where(_head_mask(h, q.shape), out[h * BLOCK:(h + 1) * BLOCK], 0.0)
    return _rms(y, gain)


def _swa_specs(nblk):
    q = pl.BlockSpec((BLOCK, GROUP_WIDTH), lambda b, i: (b * nblk + i, 3))
    cur = pl.BlockSpec((BLOCK, GROUP_WIDTH), lambda b, i: (b * nblk + i, 4))
    prev = pl.BlockSpec((BLOCK, GROUP_WIDTH), lambda b, i: (b * nblk + jnp.maximum(i - 1, 0), 4))
    return q, prev, cur


def _swa_fwd(proj, sinks, tab, gain, seq, name, items=()):
    T = proj.shape[0]
    nblk = seq // BLOCK

    def body(q_ref, kp_ref, kc_ref, s_ref, t_ref, g_ref, o_ref):
        o_ref[...] = _swa_fn(q_ref[...], kp_ref[...], kc_ref[...], s_ref[...], t_ref[...], g_ref[...],
                             pl.program_id(1) == 0).astype(BF16)

    full = lambda shape: pl.BlockSpec(shape, lambda b, i: (0,) * len(shape))
    return _call(
        body, (proj, proj, proj, sinks, tab, gain), grid=(T // seq, nblk), name=name, items=items,
        in_specs=[*_swa_specs(nblk), full(sinks.shape), full(tab.shape), full(gain.shape)],
        out_specs=[pl.BlockSpec((BLOCK, GROUP_WIDTH), lambda b, i: (b * nblk + i, 0))],
        out_shape=[jax.ShapeDtypeStruct((T, GROUP_WIDTH), BF16)],
        sem=("parallel", "parallel"))


def _swa_bwd(proj, sinks, tab, gain, dy, seq, name, items=()):
    T = proj.shape[0]
    nblk = seq // BLOCK

    def body(q_ref, kp_ref, kc_ref, s_ref, t_ref, g_ref, dy_ref, dq_ref, dkv_ref, ds_ref, dt_ref, dg_ref):
        i = pl.program_id(1)
        first = i == 0
        fn = functools.partial(_swa_fn, first=first)
        _, vjp = jax.vjp(fn, q_ref[...], kp_ref[...], kc_ref[...], s_ref[...], t_ref[...], g_ref[...])
        dq, dkp, dkc, ds, dt, dg = vjp(dy_ref[...])
        dq_ref[...] = dq.astype(BF16)
        dkv_ref[pl.ds(pl.multiple_of(i * BLOCK, BLOCK), BLOCK), :] = dkc

        @pl.when(i > 0)
        def _():
            dkv_ref[pl.ds(pl.multiple_of((i - 1) * BLOCK, BLOCK), BLOCK), :] += dkp

        _acc_out((pl.program_id(0) == 0) & first, (ds_ref, dt_ref, dg_ref), (ds, dt, dg))

    full = lambda shape: pl.BlockSpec(shape, lambda b, i: (0,) * len(shape))
    blk = lambda c: pl.BlockSpec((BLOCK, GROUP_WIDTH), lambda b, i: (b * nblk + i, c))
    return _call(
        body, (proj, proj, proj, sinks, tab, gain, dy), grid=(T // seq, nblk), name=name, items=items,
        in_specs=[*_swa_specs(nblk), full(sinks.shape), full(tab.shape), full(gain.shape), blk(2)],
        out_specs=[blk(0), pl.BlockSpec((seq, GROUP_WIDTH), lambda b, i: (b, 0)), full(sinks.shape), full(tab.shape),
                   full(gain.shape)],
        out_shape=[jax.ShapeDtypeStruct((T, GROUP_WIDTH), BF16), jax.ShapeDtypeStruct((T, GROUP_WIDTH), F32),
                   jax.ShapeDtypeStruct(sinks.shape, F32), jax.ShapeDtypeStruct(tab.shape, F32),
                   jax.ShapeDtypeStruct(gain.shape, F32)],
        sem=("arbitrary", "arbitrary"))


def _t5_bucket(dist):
    max_exact = N_BUCKETS // 2
    df = jnp.maximum(dist, 1).astype(F32)
    large = max_exact + (jnp.log(df / max_exact) / jnp.log(jnp.float32(MAX_DISTANCE / max_exact))
                         * (N_BUCKETS - max_exact)).astype(jnp.int32)
    return jnp.where(dist < max_exact, dist, jnp.minimum(large, N_BUCKETS - 1))


def _bucket_map():
    dist = (jnp.arange(BLOCK)[:, None] + BLOCK) - jnp.arange(2 * BLOCK)[None, :]
    return _t5_bucket(jnp.clip(dist, 0, BLOCK - 1)).astype(jnp.int32)


def _bias_table(rel_bias, buckets, name, items=()):
    def body(rb_ref, bk_ref, o_ref):
        bk = bk_ref[...]
        rb = rb_ref[...]
        for h in range(GROUP_HEADS):
            acc = jnp.zeros((BLOCK, 2 * BLOCK), F32)
            for b in range(N_BUCKETS):
                acc = jnp.where(bk == b, rb[b:b + 1, h:h + 1], acc)
            o_ref[h] = acc

    full = lambda shape: pl.BlockSpec(shape, lambda i: (0,) * len(shape))
    shape = (GROUP_HEADS, BLOCK, 2 * BLOCK)
    return _call(body, (rel_bias, buckets), grid=(1,), name=name, items=items,
                 in_specs=[full(rel_bias.shape), full(buckets.shape)], out_specs=[full(shape)],
                 out_shape=[jax.ShapeDtypeStruct(shape, F32)], sem=("arbitrary",))


def _bias_table_bwd(dtab, buckets, name):
    def body(dt_ref, bk_ref, o_ref):
        bk = bk_ref[...]
        row = lax.broadcasted_iota(jnp.int32, (N_BUCKETS, GROUP_HEADS), 0)
        col = lax.broadcasted_iota(jnp.int32, (N_BUCKETS, GROUP_HEADS), 1)
        acc = jnp.zeros((N_BUCKETS, GROUP_HEADS), F32)
        for h in range(GROUP_HEADS):
            dt = dt_ref[h]
            for b in range(N_BUCKETS):
                s = jnp.sum(jnp.where(bk == b, dt, 0.0), keepdims=True)
                acc = acc + jnp.where((row == b) & (col == h), s, 0.0)
        o_ref[...] = acc

    return pl.pallas_call(body, name=name, out_shape=jax.ShapeDtypeStruct((N_BUCKETS, GROUP_HEADS), F32),
                          compiler_params=_cp())(dtab, buckets)


HEAD_ROWS = GROUP_HEADS * BLOCK


def _stack_heads(x):
    return jnp.concatenate([jnp.where(_head_mask(h, x.shape), x, 0.0) for h in range(GROUP_HEADS)], axis=0).astype(BF16)


def _sb_tile(qs, kb, q0, k0):
    z = lax.dot_general(qs, kb, _NT, preferred_element_type=F32)
    row = lax.broadcasted_iota(jnp.int32, (HEAD_ROWS, BLOCK), 0) & (BLOCK - 1)
    col = lax.broadcasted_iota(jnp.int32, (HEAD_ROWS, BLOCK), 1)
    causal = (k0 + col) < (q0 + row)
    ls_neg = -(jnp.maximum(z, 0.0) + jnp.log(1.0 + jnp.exp(-jnp.abs(z))))
    return jnp.where(causal, ls_neg, 0.0), ls_neg + z, causal


SB_DEAD = -70.0
SB_FIRST_LANE = GROUP_HEADS


def _tri(strict_upper_src):
    r = lax.broadcasted_iota(jnp.int32, (BLOCK, BLOCK), 0)
    c = lax.broadcasted_iota(jnp.int32, (BLOCK, BLOCK), 1)
    cond = {"gt": r > c, "le": r <= c, "lt": r < c}[strict_upper_src]
    return jnp.where(cond, 1.0, 0.0).astype(BF16)


def _sb_fwd(proj, gain, seq, name, items=()):
    T = proj.shape[0]
    nblk = seq // BLOCK

    def body(q_ref, k_ref, v_ref, g_ref, o_ref, raw_ref, bt_ref):
        i = pl.program_id(1)
        q = q_ref[...]
        u_gt = _tri("gt")
        lane = lax.broadcasted_iota(jnp.int32, (BLOCK, BLOCK), 1)
        heads = [slice(h * HEAD_DIM, (h + 1) * HEAD_DIM) for h in range(GROUP_HEADS)]
        rows = [slice(h * BLOCK, (h + 1) * BLOCK) for h in range(GROUP_HEADS)]
        qs = _stack_heads(q * ATT_SCALE)

        def live(carry):
            j, _, cb = carry
            return (j >= 0) & (jnp.max(cb) > SB_DEAD)

        def step(carry):
            j, accs, cb = carry
            ks = pl.multiple_of(j * BLOCK, BLOCK)
            kb = k_ref[pl.ds(ks, BLOCK), :].astype(BF16)
            vb = v_ref[pl.ds(ks, BLOCK), :].astype(BF16)
            b, a, causal = _sb_tile(qs, kb, i * BLOCK, j * BLOCK)
            tail = _split_dot(b, u_gt) + cb
            w = jnp.where(causal, jnp.exp(a + tail), 0.0).astype(BF16)
            accs = tuple(accs[h] + jnp.dot(w[rows[h]], vb[:, hs], preferred_element_type=F32)
                         for h, hs in enumerate(heads))
            return j - 1, accs, cb + jnp.sum(b, axis=1, keepdims=True)

        zero_acc = tuple(jnp.zeros((BLOCK, HEAD_DIM), F32) for _ in heads)
        j_end, accs, cb = lax.while_loop(live, step, (i, zero_acc, jnp.zeros((HEAD_ROWS, 1), F32)))
        side = jnp.where(lane == SB_FIRST_LANE, (j_end + 1).astype(F32), 0.0)
        for h in range(GROUP_HEADS):
            side = jnp.where(lane == h, cb[rows[h]], side)
        raw = jnp.concatenate(accs, axis=1)
        raw_ref[...] = raw
        bt_ref[...] = side
        o_ref[...] = _rms(raw, g_ref[...]).astype(BF16)

    return _call(
        body, (proj, proj, proj, gain), grid=(T // seq, nblk), name=name, items=items,
        in_specs=[pl.BlockSpec((BLOCK, GROUP_WIDTH), lambda b, i: (b * nblk + i, 5)),
                  pl.BlockSpec((seq, GROUP_WIDTH), lambda b, i: (b, 6)),
                  pl.BlockSpec((seq, GROUP_WIDTH), lambda b, i: (b, 7)),
                  pl.BlockSpec(gain.shape, lambda b, i: (0, 0))],
        out_specs=[pl.BlockSpec((BLOCK, GROUP_WIDTH), lambda b, i: (b * nblk + i, 0)),
                   pl.BlockSpec((BLOCK, GROUP_WIDTH), lambda b, i: (b * nblk + i, 0)),
                   pl.BlockSpec((BLOCK, BLOCK), lambda b, i: (b * nblk + i, 0))],
        out_shape=[jax.ShapeDtypeStruct((T, GROUP_WIDTH), BF16), jax.ShapeDtypeStruct((T, GROUP_WIDTH), F32),
                   jax.ShapeDtypeStruct((T, BLOCK), F32)],
        sem=("parallel", "parallel"))


def _sb_bwd(proj, gain, raw, btot, dy, seq, name, items=()):
    T = proj.shape[0]
    nblk = seq // BLOCK

    def body(q_ref, k_ref, v_ref, g_ref, raw_ref, bt_ref, dy_ref, dq_ref, dk_ref, dv_ref, dg_ref):
        i = pl.program_id(1)

        @pl.when(i == 0)
        def _():
            dk_ref[...] = jnp.zeros_like(dk_ref)
            dv_ref[...] = jnp.zeros_like(dv_ref)

        rawv = raw_ref[...]
        _, vjp = jax.vjp(_rms, rawv, g_ref[...])
        do, dg = vjp(dy_ref[...])
        _acc_out((pl.program_id(0) == 0) & (i == 0), (dg_ref,), (dg,))
        q = q_ref[...]
        bt = bt_ref[...]
        u_le = _tri("le")
        u_lt = _tri("lt")
        heads = [slice(h * HEAD_DIM, (h + 1) * HEAD_DIM) for h in range(GROUP_HEADS)]
        rows = [slice(h * BLOCK, (h + 1) * BLOCK) for h in range(GROUP_HEADS)]
        qs = _stack_heads(q * ATT_SCALE)
        dos = _stack_heads(do)
        bts = jnp.concatenate([bt[:, h:h + 1] for h in range(GROUP_HEADS)], axis=0)
        first = jnp.max(bt[:, SB_FIRST_LANE:SB_FIRST_LANE + 1]).astype(jnp.int32)
        first = jnp.minimum(jnp.maximum(first, 0), i)

        def step(j, carry):
            dqs, cb, cg = carry
            ks = pl.multiple_of(j * BLOCK, BLOCK)
            kb = k_ref[pl.ds(ks, BLOCK), :].astype(BF16)
            vb = v_ref[pl.ds(ks, BLOCK), :].astype(BF16)
            b, a, causal = _sb_tile(qs, kb, i * BLOCK, j * BLOCK)
            tail = bts - (_split_dot(b, u_le) + cb)
            w = jnp.where(causal, jnp.exp(a + tail), 0.0)
            sig = jnp.exp(a)
            g = w * lax.dot_general(dos, vb, _NT, preferred_element_type=F32)
            gpre = _split_dot(g, u_lt) + cg
            dz = jnp.where(causal, g * (1.0 - sig) - gpre * sig, 0.0).astype(BF16)
            dqs = tuple(dqs[h] + jnp.dot(dz[rows[h]], kb[:, hs], preferred_element_type=F32)
                        for h, hs in enumerate(heads))
            dk_ref[pl.ds(ks, BLOCK), :] += lax.dot_general(dz, qs, _TN, preferred_element_type=F32)
            dv_ref[pl.ds(ks, BLOCK), :] += lax.dot_general(w.astype(BF16), dos, _TN, preferred_element_type=F32)
            return dqs, cb + jnp.sum(b, axis=1, keepdims=True), cg + jnp.sum(g, axis=1, keepdims=True)

        zero_dq = tuple(jnp.zeros((BLOCK, HEAD_DIM), F32) for _ in heads)
        zero = jnp.zeros((HEAD_ROWS, 1), F32)
        dqs, _, _ = lax.fori_loop(first, i + 1, step, (zero_dq, zero, zero))
        dq_ref[...] = (jnp.concatenate(dqs, axis=1) * ATT_SCALE).astype(BF16)

    blk = lambda c: pl.BlockSpec((BLOCK, GROUP_WIDTH), lambda b, i: (b * nblk + i, c))
    seqblk = lambda c: pl.BlockSpec((seq, GROUP_WIDTH), lambda b, i: (b, c))
    vec = pl.BlockSpec(gain.shape, lambda b, i: (0, 0))
    return _call(
        body, (proj, proj, proj, gain, raw, btot, dy), grid=(T // seq, nblk), name=name, items=items,
        in_specs=[blk(5), seqblk(6), seqblk(7), vec, blk(0), pl.BlockSpec((BLOCK, BLOCK), lambda b, i: (b * nblk + i, 0)),
                  blk(3)],
        out_specs=[blk(0), seqblk(0), seqblk(0), vec],
        out_shape=[jax.ShapeDtypeStruct((T, GROUP_WIDTH), BF16), jax.ShapeDtypeStruct((T, GROUP_WIDTH), F32),
                   jax.ShapeDtypeStruct((T, GROUP_WIDTH), F32), jax.ShapeDtypeStruct(gain.shape, F32)],
        sem=("arbitrary", "arbitrary"))


def _layer_params(l, sgu_w, sgu_b, pool_w, pool_scale, swa_sinks, mix_out_gain, norm_mix, norm_ffn):
    gains = mix_out_gain[l].reshape(4, 1, GROUP_WIDTH)
    return dict(
        wm=sgu_w[l], bt=sgu_b[l].T,
        wbd=jax.scipy.linalg.block_diag(*[pool_w[l, g] for g in range(len(POOL_WINDOWS))]),
        scale=pool_scale[l][None], sinks=swa_sinks[l][None],
        gain=[gains[m] for m in range(4)], norm_mix=norm_mix[l][None], norm_ffn=norm_ffn[l][None])


def _as_weights(g_in, g_out, g_gu, g_down):
    return (g_in[None], g_out.reshape(1, D_MODEL, D_MODEL), g_gu[None], g_down.reshape(1, D_FF_PAD, D_MODEL))


def _gather_item(src, l, dst_shape, rows=None, down=False, init=None):
    r0, nr = rows or (0, src.shape[1])
    if down:
        place = lambda dst, k: dst.at[k // 2, pl.ds((k % 2) * DOWN_ROWS + r0, nr), :]
    else:
        place = lambda dst, k: dst.at[k, pl.ds(r0, nr), :]
    return _Item(src, lambda s, k: s.at[l, pl.ds(r0, nr), :], dst_shape, place, init, relayed=True)


WEIGHTS = ("in", "out", "gu", "down")
FWD_PLAN = {
    (0, "in_proj"): [(0, "down", None)], (0, "sgu"): [(0, "out", None)], (0, "swa"): [(0, "gu", 0)],
    (0, "sb"): [(0, "gu", 1)],
    (0, "ffn_up"): [(1, "gu", 0), (1, "in", None), (1, "out", None)], (0, "ffn_down"): [(1, "down", None)],
    (1, "swa"): [(1, "gu", 1)], (1, "sb"): [(2, "in", None), (2, "out", None)],
    (1, "ffn_up"): [(2, "gu", None)], (1, "ffn_down"): [(2, "down", None)],
    (2, "swa"): [(3, "in", None), (3, "out", None)], (2, "sb"): [(3, "gu", 0)],
    (2, "ffn_up"): [(3, "gu", 1)], (2, "ffn_down"): [(3, "down", None)],
}


def _planned(pieces, shards, bufs):
    items = []
    for lyr, t, part in pieces:
        src = shards[WEIGHTS.index(t)]
        down = t == "down"
        shape = (4, FF_PAD, D_MODEL) if down else (N_DEV,) + src.shape[1:]
        init = bufs.get((lyr, t))
        if init is None and down:
            init = jnp.zeros(shape, BF16)
        rows = None if part is None else (part * (FF_PAD // 2), FF_PAD // 2)
        items.append(_gather_item(src, lyr, shape, rows, down, init))
    return items


def _layer_fwd(l, x, p, bufs, tab, seq, shards):
    def beside(host):
        return _planned(FWD_PLAN.get((l, host), ()), shards, bufs)

    def landed(host, results):
        for (lyr, t, _), r in zip(FWD_PLAN.get((l, host), ()), results):
            bufs[(lyr, t)] = r

    proj, h1, *got = _norm_mm(x, p["norm_mix"], bufs[(l, "in")][None], 0, 4, F32, f"in_proj_{l}",
                              items=beside("in_proj"))
    landed("in_proj", got)
    ya, *got = _sgu_fwd(proj, p["wm"], p["bt"], p["gain"][0], f"sgu_fwd_{l}", items=beside("sgu"))
    landed("sgu", got)
    yb = _pool_fwd(proj, p["wbd"], p["scale"], p["gain"][1], seq, f"pool_fwd_{l}")
    yc, *got = _swa_fwd(proj, p["sinks"], tab, p["gain"][2], seq, f"swa_fwd_{l}", items=beside("swa"))
    landed("swa", got)
    yd, raw, btot, *got = _sb_fwd(proj, p["gain"][3], seq, f"sb_fwd_{l}", items=beside("sb"))
    landed("sb", got)
    ycat = jnp.concatenate([ya, yb, yc, yd], axis=1)
    xm, = _mm_res(x, ycat, bufs[(l, "out")].reshape(1, D_MODEL, D_MODEL), 0, f"out_proj_{l}", tn=D_MODEL)
    gu, act, h2, *got = _ffn_up_act(xm, p["norm_ffn"], bufs[(l, "gu")][None], f"ffn_up_{l}", items=beside("ffn_up"))
    landed("ffn_up", got)
    xo, *got = _mm_res(xm, act, bufs[(l, "down")].reshape(1, D_FF_PAD, D_MODEL), 0, f"ffn_down_{l}",
                       tn=D_MODEL // 2, items=beside("ffn_down"))
    landed("ffn_down", got)
    return xo, (x, proj, h1, ycat, raw, btot, xm, gu, h2, act)


def _rows_item(g, r0, nr, init=None):
    cut = lambda a, k: a.at[k, pl.ds(r0, nr), :]
    return _Item(g, cut, g.shape, cut, init)


def _layer_bwd(l, dxo, saved, p, w, tab, seq, pending, ride):
    win, wout, wgu, wd = w
    x, proj, h1, ycat, raw, btot, xm, gu, h2, act = saved
    out_rows = D_MODEL // N_DEV
    in_half = D_MODEL // 2
    pieces = 2 if ride else 3
    piece = FF_PAD // pieces

    above = [_rows_item(pending[0], in_half, in_half, pending[1])] if pending else []
    dgu, *p_in_above = _ffn_down_dx(dxo, wd, gu, f"ffn_down_dx_{l}", items=above)
    g_wd, = _mm_tn_rows(act, dxo, FF_PAD, f"ffn_down_dw_{l}")
    send_down = _Item(g_wd, lambda src, k: src.at[pl.ds((k // 2) * FF_PAD + (k % 2) * DOWN_ROWS, DOWN_ROWS), :],
                      (N_DEV, DOWN_ROWS, D_MODEL))
    g_wgu, p_down = _mm_tn_rows(dgu, h2, FF_PAD, f"ffn_up_dw_{l}", items=(send_down,), block_of=_ffn_chunk_block)
    g_wgu = g_wgu.reshape(N_DEV, FF_PAD, D_MODEL)
    dxm, g_norm_ffn, p_gu = _mm_norm_bwd(dgu, wgu, 0, 1, xm, p["norm_ffn"], dxo, f"ffn_up_dx_{l}",
                                         items=(_rows_item(g_wgu, 0, piece),), block_of=_ffn_chunk_block,
                                         transposed=True)
    dycat = _mm_nt(dxm, wout, 0, F32, f"out_proj_dx_{l}")
    g_wout, = _mm_tn_rows(ycat, dxm, D_MODEL, f"out_proj_dw_{l}")
    send_out = _Item(g_wout, lambda src, k: src.at[pl.ds(k * out_rows, out_rows), :], (N_DEV, out_rows, D_MODEL))
    duv, g_wm, g_bt, g_ga = _sgu_bwd(proj, p["wm"], p["bt"], p["gain"][0], dycat, f"sgu_bwd_{l}")
    dp, g_wbd, g_scale, g_gb = _pool_bwd(proj, p["wbd"], p["scale"], p["gain"][1], dycat, seq, f"pool_bwd_{l}")
    beside_swa = list(ride) if ride else [_rows_item(g_wgu, piece, piece, p_gu)]
    dq, dkv, g_sinks, g_tab, g_gc, *rode = _swa_bwd(proj, p["sinks"], tab, p["gain"][2], dycat, seq,
                                                    f"swa_bwd_{l}", items=beside_swa)
    if not ride:
        p_gu, rode = rode[0], []
    dqd, dkd, dvd, g_gd, p_gu, p_out = _sb_bwd(
        proj, p["gain"][3], raw, btot, dycat, seq, f"sb_bwd_{l}",
        items=(_rows_item(g_wgu, (pieces - 1) * piece, piece, p_gu), send_out))
    dproj = jnp.concatenate([duv, dp, dq, dkv.astype(BF16), dqd, dkd.astype(BF16), dvd.astype(BF16)], axis=1)
    g_win, = _mm_tn_cols(h1, dproj, GROUP_WIDTH, 4, f"in_proj_dw_{l}")
    defer = l > 0
    dx, g_norm_mix, p_in = _mm_norm_bwd(dproj, win, 0, 4, x, p["norm_mix"], dxm, f"in_proj_dx_{l}",
                                        items=(_rows_item(g_win, 0, in_half if defer else D_MODEL),))
    pending = (g_win, p_in) if defer else None
    ng = len(POOL_WINDOWS)
    gd = GROUP_WIDTH // ng
    small = dict(
        sgu_w=g_wm, sgu_b=g_bt.T,
        pool_w=jnp.stack([g_wbd[g * gd:(g + 1) * gd, g * gd:(g + 1) * gd] for g in range(ng)]),
        pool_scale=g_scale[0], swa_sinks=g_sinks[0],
        mix_out_gain=jnp.concatenate([g_ga[0], g_gb[0], g_gc[0], g_gd[0]]),
        norm_mix=g_norm_mix[0], norm_ffn=g_norm_ffn[0])
    return dx, [p_in, p_out, p_gu, p_down], small, g_tab, rode, p_in_above, pending


def _local_step(x, target, shards, sgu_w, sgu_b, pool_w, pool_scale, swa_sinks, rel_bias, mix_out_gain,
                norm_mix, norm_ffn, norm_final, seq):
    buckets = _bucket_map()
    bufs = {}
    first = [(0, "in", None)]
    tab, *got = _bias_table(rel_bias, buckets, "bias_table", items=_planned(first, shards, bufs))
    bufs.update({(lyr, t): r for (lyr, t, _), r in zip(first, got)})
    params = [_layer_params(l, sgu_w, sgu_b, pool_w, pool_scale, swa_sinks, mix_out_gain, norm_mix, norm_ffn)
              for l in range(DEPTH)]
    saved, weights = [], []
    for l in range(DEPTH):
        x, s = _layer_fwd(l, x, params[l], bufs, tab, seq, shards)
        saved.append(s)
        weights.append(_as_weights(*[bufs[(l, t)] for t in WEIGHTS]))
    loss, dx, g_final = _loss_head(x, norm_final[None], target, "loss_head")
    big, small, g_tab, upper, pending = [None] * DEPTH, [None] * DEPTH, None, None, None
    for l in reversed(range(DEPTH)):
        ride = ()
        if l == 0:
            mine = _pack(_upper_layers(small), _pack_rows(_upper_layers(small)))
            ride = (_Item(mine, lambda src, k: src, (N_DEV,) + mine.shape),)
        dx, big[l], small[l], t, rode, done_above, pending = _layer_bwd(
            l, dx, saved[l], params[l], weights[l], tab, seq, pending, ride)
        g_tab = t if g_tab is None else g_tab + t
        upper = rode[0] if rode else upper
        if done_above:
            big[l + 1][0] = done_above[0]
    return loss, dx, big, small, _bias_table_bwd(g_tab, buckets, "bias_table_bwd"), g_final[0], upper


def _cast_pad(w, rows, name):
    L, r, c = w.shape

    def body(w_ref, o_ref):
        if rows != r:
            o_ref[...] = jnp.zeros_like(o_ref)
        o_ref[:r, :] = w_ref[...].astype(BF16)

    return pl.pallas_call(
        body, grid=(L,), name=name,
        in_specs=[pl.BlockSpec((None, r, c), lambda l: (l, 0, 0))],
        out_specs=pl.BlockSpec((None, rows, c), lambda l: (l, 0, 0)),
        out_shape=jax.ShapeDtypeStruct((L, rows, c), BF16),
        compiler_params=_cp("parallel"))(w)


def _adamw(w, g, m, v):
    m = ADAM_B1 * m + (1.0 - ADAM_B1) * g
    v = ADAM_B2 * v + (1.0 - ADAM_B2) * jnp.square(g)
    m_hat = m / (1.0 - ADAM_B1 ** ADAM_STEP)
    v_hat = v / (1.0 - ADAM_B2 ** ADAM_STEP)
    delta = -ADAM_LR * (m_hat / (jnp.sqrt(v_hat) + ADAM_EPS) + ADAM_WD * w)
    return delta, m, v


def _adamw_sharded(parts, w, m, v, tr, name, items=()):
    L, r, c = w.shape
    cp = parts[0].shape[-1]
    nrow = r // tr

    def body(*refs):
        p_refs, (w_ref, m_ref, v_ref, g_ref, d_ref, nm_ref, nv_ref) = refs[:L], refs[L:]
        for k in range(L):
            @pl.when(pl.program_id(0) == k)
            def _(p_ref=p_refs[k]):
                g = p_ref[0, :, :c].astype(F32)
                for dev in range(1, N_DEV):
                    g = g + p_ref[dev, :, :c].astype(F32)
                delta, nm, nv = _adamw(w_ref[...], g, m_ref[...], v_ref[...])
                g_ref[...] = g
                d_ref[...] = delta
                nm_ref[...] = nm
                nv_ref[...] = nv

    def part_spec(k):
        return pl.BlockSpec((N_DEV, tr, cp),
                            lambda l, i: (0, jnp.where(l == k, i, jnp.where(l < k, 0, nrow - 1)), 0))

    blk = pl.BlockSpec((None, tr, c), lambda l, i: (l, i, 0))
    out = jax.ShapeDtypeStruct((L, r, c), F32)
    return _call(
        body, (*parts, w, m, v), grid=(L, nrow), name=name, items=items,
        in_specs=[part_spec(k) for k in range(L)] + [blk, blk, blk],
        out_specs=[blk] * 4, out_shape=[out] * 4, sem=("arbitrary", "arbitrary"))


def _adamw_small(parts, wmv, name):
    def body(p_ref, wmv_ref, g_ref, d_ref, nm_ref, nv_ref):
        g = p_ref[0]
        for k in range(1, N_DEV):
            g = g + p_ref[k]
        delta, nm, nv = _adamw(wmv_ref[0], g, wmv_ref[1], wmv_ref[2])
        g_ref[...] = g
        d_ref[...] = delta
        nm_ref[...] = nm
        nv_ref[...] = nv

    out = jax.ShapeDtypeStruct(wmv.shape[1:], F32)
    return pl.pallas_call(body, name=name, out_shape=[out] * 4, compiler_params=_cp())(parts, wmv)


LAYERED = ("sgu_w", "sgu_b", "pool_w", "pool_scale", "swa_sinks", "mix_out_gain", "norm_mix", "norm_ffn")
SHARED = ("rel_bias", "norm_final")


def _seg_rows(a):
    return -(-a.size // 128)


def _pack_rows(parts):
    return -(-sum(_seg_rows(p) for p in parts) // 8) * 8


def _upper_layers(per_layer):
    if isinstance(per_layer, dict):
        return [per_layer[k][1:] for k in LAYERED]
    return [jnp.stack([per_layer[l][k] for l in range(1, DEPTH)]) for k in LAYERED]


def _layer_zero(stacked):
    return [stacked[k][:1] for k in LAYERED] + [stacked[k] for k in SHARED]


def _pack(parts, rows):
    segs = [jnp.pad(p.reshape(-1), (0, _seg_rows(p) * 128 - p.size)).reshape(_seg_rows(p), 128) for p in parts]
    used = sum(s.shape[0] for s in segs)
    return jnp.concatenate(segs + [jnp.zeros((rows - used, 128), F32)], axis=0)


def _pack_groups(groups, rows):
    segs = []
    for parts in groups:
        segs += [jnp.pad(p.reshape(-1), (0, _seg_rows(p) * 128 - p.size)).reshape(_seg_rows(p), 128) for p in parts]
        segs.append(jnp.zeros((rows - sum(_seg_rows(p) for p in parts), 128), F32))
    return jnp.concatenate(segs, axis=0).reshape(len(groups), rows, 128)


def _unpack(buf, like):
    out, at = [], 0
    for a in like:
        out.append(buf[at:at + _seg_rows(a)].reshape(-1)[:a.size].reshape(a.shape))
        at += _seg_rows(a)
    return out


def kernel(x, w_in, w_out, sgu_w, sgu_b, pool_w, pool_scale, swa_sinks, rel_bias, mix_out_gain, norm_mix, norm_ffn, w_gate_up, w_down, norm_final, loss_target, m_w_in, m_w_out, m_sgu_w, m_sgu_b, m_pool_w, m_pool_scale, m_swa_sinks, m_rel_bias, m_mix_out_gain, m_norm_mix, m_norm_ffn, m_w_gate_up, m_w_down, m_norm_final, v_w_in, v_w_out, v_sgu_w, v_sgu_b, v_pool_w, v_pool_scale, v_swa_sinks, v_rel_bias, v_mix_out_gain, v_norm_mix, v_norm_ffn, v_w_gate_up, v_w_down, v_norm_final):
    bl, seq, _ = x.shape
    L = w_in.shape[0]
    gu_t, m_gu_t, v_gu_t = (jnp.swapaxes(a, 1, 2) for a in (w_gate_up, m_w_gate_up, v_w_gate_up))
    shards = (_cast_pad(w_in, D_MODEL, "shard_w_in"), _cast_pad(w_out, D_MODEL // N_DEV, "shard_w_out"),
              _cast_pad(gu_t, FF_PAD, "shard_w_gate_up"), _cast_pad(w_down, DOWN_ROWS, "shard_w_down"))
    small_w = dict(sgu_w=sgu_w, sgu_b=sgu_b, pool_w=pool_w, pool_scale=pool_scale, swa_sinks=swa_sinks,
                   rel_bias=rel_bias, mix_out_gain=mix_out_gain, norm_mix=norm_mix, norm_ffn=norm_ffn,
                   norm_final=norm_final)
    small_m = dict(sgu_w=m_sgu_w, sgu_b=m_sgu_b, pool_w=m_pool_w, pool_scale=m_pool_scale, swa_sinks=m_swa_sinks,
                   rel_bias=m_rel_bias, mix_out_gain=m_mix_out_gain, norm_mix=m_norm_mix, norm_ffn=m_norm_ffn,
                   norm_final=m_norm_final)
    small_v = dict(sgu_w=v_sgu_w, sgu_b=v_sgu_b, pool_w=v_pool_w, pool_scale=v_pool_scale, swa_sinks=v_swa_sinks,
                   rel_bias=v_rel_bias, mix_out_gain=v_mix_out_gain, norm_mix=v_norm_mix, norm_ffn=v_norm_ffn,
                   norm_final=v_norm_final)
    loss, dx, big, small, g_rel_bias, g_final, upper = _local_step(
        x.reshape(bl * seq, D_MODEL), loss_target.reshape(bl * seq, D_MODEL), shards, sgu_w, sgu_b, pool_w,
        pool_scale, swa_sinks, rel_bias, mix_out_gain, norm_mix, norm_ffn, norm_final, seq)
    p_in, p_out, p_gu, p_down = ([big[l][t] for l in range(L)] for t in range(4))
    outs_gu = [jnp.swapaxes(a, 1, 2) for a in
               _adamw_sharded(p_gu, gu_t, m_gu_t, v_gu_t, FF_SHARD // 4, "adamw_w_gate_up")]
    outs_down = _adamw_sharded(p_down, w_down, m_w_down, v_w_down, DOWN_ROWS // 2, "adamw_w_down")
    outs_in = _adamw_sharded(p_in, w_in, m_w_in, v_w_in, 256, "adamw_w_in")
    lo_like = _layer_zero(small_w)
    lo_rows = _pack_rows(lo_like + [loss[0]])
    lo_mine = _pack([small[0][k][None] for k in LAYERED] + [g_rel_bias, g_final, loss[0]], lo_rows)
    *outs_out, lower = _adamw_sharded(p_out, w_out, m_w_out, v_w_out, D_MODEL // N_DEV, "adamw_w_out",
                                      items=(_Item(lo_mine, lambda src, k: src, (N_DEV,) + lo_mine.shape),))
    lo_res = _adamw_small(lower, _pack_groups([lo_like, _layer_zero(small_m), _layer_zero(small_v)], lo_rows),
                          "adamw_small_layer0")
    hi_like = _upper_layers(small_w)
    hi_rows = _pack_rows(hi_like)
    hi_res = _adamw_small(upper, _pack_groups([hi_like, _upper_layers(small_m), _upper_layers(small_v)], hi_rows),
                          "adamw_small_upper")
    loss_total = lo_res[0][sum(_seg_rows(a) for a in lo_like), 0]
    small_outs = []
    for lo_buf, hi_buf in zip(lo_res, hi_res):
        lo = dict(zip(LAYERED + SHARED, _unpack(lo_buf, lo_like)))
        hi = dict(zip(LAYERED, _unpack(hi_buf, hi_like)))
        small_outs.append({k: jnp.concatenate([lo[k], hi[k]], axis=0) if k in hi else lo[k] for k in lo})
    big_outs = dict(w_in=outs_in, w_out=outs_out, w_gate_up=outs_gu, w_down=outs_down)
    order = ("w_in", "w_out", "sgu_w", "sgu_b", "pool_w", "pool_scale", "swa_sinks", "rel_bias", "mix_out_gain",
             "norm_mix", "norm_ffn", "w_gate_up", "w_down", "norm_final")
    result = [loss_total, dx.reshape(bl, seq, D_MODEL)]
    for which in range(4):
        for name in order:
            result.append(big_outs[name][which] if name in big_outs else small_outs[which][name])
    return tuple(result)
```

```python
import functools

import jax
import jax.numpy as jnp
from jax import lax
from jax.experimental import pallas as pl
from jax.experimental.pallas import tpu as pltpu

F32 = jnp.float32
BF16 = jnp.bfloat16

N_DEV = 8
DEPTH = 4
D_MODEL = 1024
GROUP_WIDTH = 256
HEAD_DIM = 64
GROUP_HEADS = 4
BLOCK = 128
N_BUCKETS = 32
MAX_DISTANCE = 128
POOL_WINDOWS = (2, 4, 8, 16)
D_FF = 2816
FF_SHARD = D_FF // 4
FF_PAD = 768
D_FF_PAD = 4 * FF_PAD
EPS = 1e-6
ATT_SCALE = HEAD_DIM ** -0.5
ADAM_LR = 0.001
ADAM_B1 = 0.9
ADAM_B2 = 0.999
ADAM_EPS = 1e-08
ADAM_WD = 0.01
ADAM_STEP = 10
VMEM_LIMIT = 56 * 1024 * 1024
MESH_AXES = ("x", "y", "c")


def _cp(*sem):
    return pltpu.CompilerParams(dimension_semantics=sem or None, vmem_limit_bytes=VMEM_LIMIT)


_NT = (((1,), (1,)), ((), ()))
_TN = (((0,), (0,)), ((), ()))


@jax.custom_vjp
def _bdot(a, b):
    return jnp.dot(a.astype(BF16), b.astype(BF16), preferred_element_type=F32)


def _bdot_fwd(a, b):
    return _bdot(a, b), (a.astype(BF16), b.astype(BF16))


def _bdot_bwd(res, ct):
    a, b = res
    c = ct.astype(BF16)
    return (lax.dot_general(c, b, _NT, preferred_element_type=F32),
            lax.dot_general(a, c, _TN, preferred_element_type=F32))


_bdot.defvjp(_bdot_fwd, _bdot_bwd)


@jax.custom_vjp
def _bdot_nt(a, b):
    return lax.dot_general(a.astype(BF16), b.astype(BF16), _NT, preferred_element_type=F32)


def _bdot_nt_fwd(a, b):
    return _bdot_nt(a, b), (a.astype(BF16), b.astype(BF16))


def _bdot_nt_bwd(res, ct):
    a, b = res
    c = ct.astype(BF16)
    return (jnp.dot(c, b, preferred_element_type=F32),
            lax.dot_general(c, a, _TN, preferred_element_type=F32))


_bdot_nt.defvjp(_bdot_nt_fwd, _bdot_nt_bwd)


def _rms(x, g):
    return x * lax.rsqrt(jnp.mean(x * x, axis=-1, keepdims=True) + EPS) * g


def _sigmoid(x):
    return 0.5 * jnp.tanh(0.5 * x) + 0.5


def _split_dot(x, u):
    hi = x.astype(BF16)
    lo = (x - hi.astype(F32)).astype(BF16)
    return jnp.dot(hi, u, preferred_element_type=F32) + jnp.dot(lo, u, preferred_element_type=F32)


def _head_mask(h, shape):
    col = lax.broadcasted_iota(jnp.int32, shape, 1)
    return (col >= h * HEAD_DIM) & (col < (h + 1) * HEAD_DIM)


ANY = pl.BlockSpec(memory_space=pl.ANY)
MESH = pl.DeviceIdType.MESH
DOWN_ROWS = D_FF // N_DEV


def _my_place():
    x, y, c = (lax.axis_index(a) for a in MESH_AXES)
    return x, y, c, 4 * x + 2 * y + c


def _peer(x, y, c, d):
    return (x ^ (d >> 2), y ^ ((d >> 1) & 1), c ^ (d & 1))


class _Item:
    def __init__(self, src, block, dst_shape, place=None, init=None, relayed=False):
        self.src, self.block, self.dst_shape, self.init, self.relayed = src, block, dst_shape, init, relayed
        self.place = place or (lambda dst, k: dst.at[k])


def _call(body, args, *, grid, in_specs, out_specs, out_shape, sem, name, scratch_shapes=(), items=()):
    if not items:
        outs = pl.pallas_call(body, grid=grid, in_specs=in_specs, out_specs=out_specs, out_shape=out_shape, name=name,
                              scratch_shapes=list(scratch_shapes), compiler_params=_cp(*sem))(*args)
        return list(outs) if isinstance(outs, (list, tuple)) else [outs]
    n_in, n_out, n_scr, n = len(in_specs), len(out_specs), len(scratch_shapes), len(items)
    inits = [i for i, it in enumerate(items) if it.init is not None]

    def wrapped(*refs):
        core_in, srcs = refs[:n_in], refs[n_in:n_in + n]
        off = n_in + n + len(inits)
        core_out, dsts = refs[off:off + n_out], refs[off + n_out:off + n_out + n]
        scratch = refs[off + n_out + n:]
        send_sems, recv_sems, local_sems = scratch[n_scr:]
        step = pl.program_id(0)
        for a in range(1, len(grid)):
            step = step * grid[a] + pl.program_id(a)
        steps = functools.reduce(lambda p, g: p * g, grid)
        relay_step = max(0, steps - 1 - max(1, steps // 8))
        x, y, c, me = _my_place()
        direct = [i for i in range(n) if not items[i].relayed]
        relayed = [i for i in range(n) if items[i].relayed]
        chips = [(1 - x, y), (x, 1 - y), (1 - x, 1 - y)]
        sibling = (x, y, 1 - c)

        def local(i):
            return pltpu.make_async_copy(items[i].block(srcs[i], me), items[i].place(dsts[i], me), local_sems.at[i])

        def remote(d, i, sending):
            px, py, pc = _peer(x, y, c, d)
            pk = 4 * px + 2 * py + pc
            return pltpu.make_async_remote_copy(
                src_ref=items[i].block(srcs[i], pk), dst_ref=items[i].place(dsts[i], me if sending else pk),
                send_sem=send_sems.at[d - 1, i], recv_sem=recv_sems.at[d - 1, i],
                device_id=(px, py, pc), device_id_type=MESH)

        def hop(slot, i, owner, to, from_src):
            k = 4 * owner[0] + 2 * owner[1] + owner[2]
            src = items[i].block(srcs[i], me) if from_src else items[i].place(dsts[i], k)
            return pltpu.make_async_remote_copy(
                src_ref=src, dst_ref=items[i].place(dsts[i], k), send_sem=send_sems.at[slot, i],
                recv_sem=recv_sems.at[slot, i], device_id=to, device_id_type=MESH)

        @pl.when(step == 0)
        def _():
            for i in range(n):
                local(i).start()
            for d in range(1, N_DEV):
                for i in direct:
                    remote(d, i, True).start()
            for i in relayed:
                hop(0, i, (x, y, c), sibling, True).start()
                for j, chip in enumerate(chips):
                    hop(1 + j, i, (x, y, c), (*chip, c), True).start()

        body(*core_in, *core_out, *scratch[:n_scr])

        if relayed:
            @pl.when(step == relay_step)
            def _():
                for i in relayed:
                    for j, chip in enumerate(chips):
                        hop(1 + j, i, (*chip, c), (x, y, c), False).wait_recv()
                        hop(4 + j, i, (*chip, c), sibling, False).start()

        @pl.when(step == steps - 1)
        def _():
            for d in range(1, N_DEV):
                for i in direct:
                    remote(d, i, False).wait_recv()
            for i in relayed:
                hop(0, i, sibling, (x, y, c), False).wait_recv()
                for j, chip in enumerate(chips):
                    hop(4 + j, i, (*chip, 1 - c), (x, y, c), False).wait_recv()
            for d in range(1, N_DEV):
                for i in direct:
                    remote(d, i, True).wait_send()
            for i in relayed:
                for slot in range(N_DEV - 1):
                    hop(slot, i, (x, y, c), sibling, True).wait_send()
            for i in range(n):
                local(i).wait()

    outs = pl.pallas_call(
        wrapped, grid=grid, name=name,
        in_specs=list(in_specs) + [ANY] * (n + len(inits)), out_specs=list(out_specs) + [ANY] * n,
        out_shape=list(out_shape) + [jax.ShapeDtypeStruct(it.dst_shape, it.src.dtype) for it in items],
        input_output_aliases={n_in + n + j: n_out + i for j, i in enumerate(inits)},
        scratch_shapes=list(scratch_shapes) + [pltpu.SemaphoreType.DMA((N_DEV - 1, n)),
                                               pltpu.SemaphoreType.DMA((N_DEV - 1, n)), pltpu.SemaphoreType.DMA((n,))],
        compiler_params=_cp(*(["arbitrary"] * len(grid))),
    )(*args, *[it.src for it in items], *[items[i].init for i in inits])
    return list(outs)


def _norm_mm(x, g, w, l, jb, out_dtype, name, tm=1024, items=()):
    T, K = x.shape
    _, nb, _, tn = w.shape
    tm = min(tm, T)

    def body(x_ref, g_ref, w_ref, o_ref, h_ref):
        @pl.when(pl.program_id(1) == 0)
        def _():
            h_ref[...] = _rms(x_ref[...], g_ref[...]).astype(BF16)
        h = h_ref[...]
        for jj in range(jb):
            o_ref[:, jj * tn:(jj + 1) * tn] = jnp.dot(h, w_ref[jj], preferred_element_type=F32).astype(o_ref.dtype)

    return _call(
        body, (x, g, w), grid=(T // tm, nb // jb), name=name, items=items,
        in_specs=[pl.BlockSpec((tm, K), lambda i, j: (i, 0)), pl.BlockSpec((1, K), lambda i, j: (0, 0)),
                  pl.BlockSpec((None, jb, K, tn), lambda i, j: (l, j, 0, 0))],
        out_specs=[pl.BlockSpec((tm, jb * tn), lambda i, j: (i, j)), pl.BlockSpec((tm, K), lambda i, j: (i, 0))],
        out_shape=[jax.ShapeDtypeStruct((T, nb * tn), out_dtype), jax.ShapeDtypeStruct((T, K), BF16)],
        sem=("parallel", "arbitrary"))


def _mm_res(res, a, w, l, name, tn, tm=1024, items=()):
    T, K = a.shape
    N = w.shape[2]
    tm = min(tm, T)

    def body(r_ref, a_ref, w_ref, o_ref):
        o_ref[...] = r_ref[...] + jnp.dot(a_ref[...], w_ref[...], preferred_element_type=F32)

    return _call(
        body, (res, a, w), grid=(T // tm, N // tn), name=name, items=items,
        in_specs=[pl.BlockSpec((tm, tn), lambda i, j: (i, j)), pl.BlockSpec((tm, K), lambda i, j: (i, 0)),
                  pl.BlockSpec((None, K, tn), lambda i, j: (l, 0, j))],
        out_specs=[pl.BlockSpec((tm, tn), lambda i, j: (i, j))],
        out_shape=[jax.ShapeDtypeStruct((T, N), F32)],
        sem=("parallel", "parallel"))


def _mm_nt(a, w, l, out_dtype, name, tm=1024, tn=1024):
    T, K = a.shape
    N = w.shape[1]
    tm = min(tm, T)

    def body(a_ref, w_ref, o_ref):
        o_ref[...] = lax.dot_general(a_ref[...].astype(BF16), w_ref[...], _NT,
                                     preferred_element_type=F32).astype(o_ref.dtype)

    return pl.pallas_call(
        body, grid=(T // tm, N // tn), name=name,
        in_specs=[pl.BlockSpec((tm, K), lambda i, j: (i, 0)), pl.BlockSpec((None, tn, K), lambda i, j: (l, j, 0))],
        out_specs=pl.BlockSpec((tm, tn), lambda i, j: (i, j)),
        out_shape=jax.ShapeDtypeStruct((T, N), out_dtype),
        compiler_params=_cp("parallel", "parallel"))(a, w)


def _mm_norm_bwd(a, w, l, jb, x, g, dres, name, tm=1024, items=(), block_of=lambda j: j, transposed=False):
    T = a.shape[0]
    _, nb, K, tn = w.shape
    if transposed:
        tn, K = K, tn
    tm = min(tm, T)
    nj = nb // jb
    sub = min(256, tm)
    dims = (((1,), (0,)), ((), ())) if transposed else _NT

    def body(a_ref, w_ref, x_ref, g_ref, r_ref, o_ref, dg_ref):
        part = lax.dot_general(a_ref[:, :tn], w_ref[0], dims, preferred_element_type=F32)
        for jj in range(1, jb):
            part += lax.dot_general(a_ref[:, jj * tn:(jj + 1) * tn], w_ref[jj], dims, preferred_element_type=F32)

        @pl.when(pl.program_id(1) == 0)
        def _():
            o_ref[...] = part

        @pl.when(pl.program_id(1) > 0)
        def _():
            o_ref[...] += part

        @pl.when(pl.program_id(1) == nj - 1)
        def _():
            dg = jnp.zeros((1, K), F32)
            for r in range(tm // sub):
                rows = pl.ds(r * sub, sub)
                _, vjp = jax.vjp(_rms, x_ref[rows, :], g_ref[...])
                dx, dg_r = vjp(o_ref[rows, :])
                o_ref[rows, :] = r_ref[rows, :] + dx
                dg = dg + dg_r
            _acc_out(pl.program_id(0) == 0, (dg_ref,), (dg,))

    row = pl.BlockSpec((tm, K), lambda i, j: (i, 0))
    vec = pl.BlockSpec((1, K), lambda i, j: (0, 0))
    return _call(
        body, (a, w, x, g, dres), grid=(T // tm, nj), name=name, items=items,
        in_specs=[pl.BlockSpec((tm, jb * tn), lambda i, j: (i, j)),
                  pl.BlockSpec((None, jb) + w.shape[2:], lambda i, j: (l, block_of(j), 0, 0)), row, vec, row],
        out_specs=[row, vec],
        out_shape=[jax.ShapeDtypeStruct((T, K), F32), jax.ShapeDtypeStruct((1, K), F32)],
        sem=("arbitrary", "arbitrary"))


def _mm_tn_cols(lhs, rhs, tn, jb, name, tm=2048, items=(), block_of=lambda j: j):
    T, K = lhs.shape
    nb = rhs.shape[1] // tn
    tm = min(tm, T)
    nt = T // tm

    def body(l_ref, r_ref, o_ref, acc):
        part = lax.dot_general(l_ref[...], r_ref[...], _TN, preferred_element_type=F32)

        @pl.when(pl.program_id(1) == 0)
        def _():
            acc[...] = part

        @pl.when(pl.program_id(1) > 0)
        def _():
            acc[...] += part

        @pl.when(pl.program_id(1) == nt - 1)
        def _():
            for jj in range(jb):
                o_ref[jj] = acc[:, jj * tn:(jj + 1) * tn].astype(BF16)

    return _call(
        body, (lhs, rhs), grid=(nb // jb, nt), name=name, items=items,
        in_specs=[pl.BlockSpec((tm, K), lambda j, t: (t, 0)), pl.BlockSpec((tm, jb * tn), lambda j, t: (t, j))],
        out_specs=[pl.BlockSpec((jb, K, tn), lambda j, t: (block_of(j), 0, 0))],
        out_shape=[jax.ShapeDtypeStruct((nb, K, tn), BF16)],
        scratch_shapes=[pltpu.VMEM((K, jb * tn), F32)],
        sem=("parallel", "arbitrary"))


def _mm_tn_rows(lhs, rhs, tk, name, tm=2048, items=(), block_of=lambda j: j):
    T, Kl = lhs.shape
    N = rhs.shape[1]
    tm = min(tm, T)
    nt = T // tm

    def body(l_ref, r_ref, o_ref, acc):
        part = lax.dot_general(l_ref[...], r_ref[...].astype(BF16), _TN, preferred_element_type=F32)

        @pl.when(pl.program_id(1) == 0)
        def _():
            acc[...] = part

        @pl.when(pl.program_id(1) > 0)
        def _():
            acc[...] += part

        @pl.when(pl.program_id(1) == nt - 1)
        def _():
            o_ref[...] = acc[...].astype(BF16)

    return _call(
        body, (lhs, rhs), grid=(Kl // tk, nt), name=name, items=items,
        in_specs=[pl.BlockSpec((tm, tk), lambda l, t: (t, l)), pl.BlockSpec((tm, N), lambda l, t: (t, 0))],
        out_specs=[pl.BlockSpec((tk, N), lambda l, t: (block_of(l), 0))],
        out_shape=[jax.ShapeDtypeStruct((Kl, N), BF16)],
        scratch_shapes=[pltpu.VMEM((tk, N), F32)],
        sem=("parallel", "arbitrary"))


N_FF_CHUNK = D_FF_PAD // FF_PAD


def _ffn_chunk_block(j):
    return (j % 2) * N_FF_CHUNK + j // 2


def _ffn_up_act(x, g, w, name, tm=1024, items=()):
    T, K = x.shape
    tm = min(tm, T)

    def body(x_ref, g_ref, wg_ref, wu_ref, gu_ref, act_ref, h_ref):
        @pl.when(pl.program_id(1) == 0)
        def _():
            h_ref[...] = _rms(x_ref[...], g_ref[...]).astype(BF16)
        h = h_ref[...]
        gate = lax.dot_general(h, wg_ref[...], _NT, preferred_element_type=F32)
        up = lax.dot_general(h, wu_ref[...], _NT, preferred_element_type=F32)
        gu_ref[:, :FF_PAD] = gate.astype(BF16)
        gu_ref[:, FF_PAD:] = up.astype(BF16)
        act_ref[...] = (gate * _sigmoid(gate) * up).astype(BF16)

    return _call(
        body, (x, g, w, w), grid=(T // tm, N_FF_CHUNK), name=name, items=items,
        in_specs=[pl.BlockSpec((tm, K), lambda i, j: (i, 0)), pl.BlockSpec((1, K), lambda i, j: (0, 0)),
                  pl.BlockSpec((None, None, FF_PAD, K), lambda i, j: (0, j, 0, 0)),
                  pl.BlockSpec((None, None, FF_PAD, K), lambda i, j: (0, j + N_FF_CHUNK, 0, 0))],
        out_specs=[pl.BlockSpec((tm, 2 * FF_PAD), lambda i, j: (i, j)), pl.BlockSpec((tm, FF_PAD), lambda i, j: (i, j)),
                   pl.BlockSpec((tm, K), lambda i, j: (i, 0))],
        out_shape=[jax.ShapeDtypeStruct((T, 2 * D_FF_PAD), BF16), jax.ShapeDtypeStruct((T, D_FF_PAD), BF16),
                   jax.ShapeDtypeStruct((T, K), BF16)],
        sem=("parallel", "arbitrary"))


def _ffn_down_dx(dxo, w, gu, name, tm=1024, items=()):
    T, K = dxo.shape
    tm = min(tm, T)

    sub = min(512, tm)

    def body(d_ref, w_ref, gu_ref, o_ref):
        w = w_ref[...]
        for r in range(tm // sub):
            rows = slice(r * sub, (r + 1) * sub)
            d = lax.dot_general(d_ref[rows, :].astype(BF16), w, _NT, preferred_element_type=F32)
            gate = gu_ref[rows, :FF_PAD].astype(F32)
            up = gu_ref[rows, FF_PAD:].astype(F32)
            sig = _sigmoid(gate)
            silu = gate * sig
            o_ref[rows, :FF_PAD] = (d * up * (sig + silu * (1.0 - sig))).astype(BF16)
            o_ref[rows, FF_PAD:] = (d * silu).astype(BF16)

    return _call(
        body, (dxo, w, gu), grid=(T // tm, N_FF_CHUNK), name=name, items=items,
        in_specs=[pl.BlockSpec((tm, K), lambda i, j: (i, 0)), pl.BlockSpec((None, FF_PAD, K), lambda i, j: (0, j, 0)),
                  pl.BlockSpec((tm, 2 * FF_PAD), lambda i, j: (i, j))],
        out_specs=[pl.BlockSpec((tm, 2 * FF_PAD), lambda i, j: (i, j))],
        out_shape=[jax.ShapeDtypeStruct((T, 2 * D_FF_PAD), BF16)],
        sem=("parallel", "parallel"))


def _loss_head(x, g, target, name, tm=512):
    T, K = x.shape

    def loss_fn(xv, gv, tv):
        err = _rms(xv, gv) - tv
        return 0.5 * jnp.sum(jnp.mean(err * err, axis=-1, keepdims=True), axis=0, keepdims=True)

    def body(x_ref, g_ref, t_ref, l_ref, dx_ref, dg_ref):
        val, vjp = jax.vjp(lambda xv, gv: loss_fn(xv, gv, t_ref[...]), x_ref[...], g_ref[...])
        dx, dg = vjp(jnp.ones((1, 1), F32))
        dx_ref[...] = dx
        lval = jnp.broadcast_to(val, (1, 128))

        @pl.when(pl.program_id(0) == 0)
        def _():
            dg_ref[...] = dg
            l_ref[...] = lval

        @pl.when(pl.program_id(0) > 0)
        def _():
            dg_ref[...] += dg
            l_ref[...] += lval

    row = pl.BlockSpec((tm, K), lambda i: (i, 0))
    vec = pl.BlockSpec((1, K), lambda i: (0, 0))
    return pl.pallas_call(
        body, grid=(T // tm,), name=name,
        in_specs=[row, vec, row], out_specs=[pl.BlockSpec((1, 128), lambda i: (0, 0)), row, vec],
        out_shape=[jax.ShapeDtypeStruct((1, 128), F32), jax.ShapeDtypeStruct((T, K), F32),
                   jax.ShapeDtypeStruct((1, K), F32)],
        compiler_params=_cp("arbitrary"))(x, g, target)


SGU_ROWS = 8 * BLOCK


def _sgu_fn(u, v, wm, bt, gain):
    ug = jax.nn.gelu(u)
    vg = jax.nn.gelu(v)
    row = lax.broadcasted_iota(jnp.int32, (BLOCK, BLOCK), 0)
    col = lax.broadcasted_iota(jnp.int32, (BLOCK, BLOCK), 1)
    tri = row >= col
    normed = []
    for h in range(GROUP_HEADS):
        vh = vg[:, h * HEAD_DIM:(h + 1) * HEAD_DIM]
        xc = vh - jnp.mean(vh, axis=-1, keepdims=True)
        normed.append(xc * lax.rsqrt(jnp.mean(xc * xc, axis=-1, keepdims=True) + EPS))
    vn = jnp.concatenate(normed, axis=1)
    wcat = jnp.concatenate([jnp.where(tri, wm[h], 0.0) for h in range(GROUP_HEADS)], axis=1)
    bias = jnp.concatenate([jnp.broadcast_to(bt[:, h:h + 1], (BLOCK, HEAD_DIM)) for h in range(GROUP_HEADS)], axis=1)
    mixes = []
    for c in range(u.shape[0] // BLOCK):
        chunk = vn[c * BLOCK:(c + 1) * BLOCK]
        stacked = jnp.concatenate([jnp.where(_head_mask(h, chunk.shape), chunk, 0.0) for h in range(GROUP_HEADS)], axis=0)
        mixes.append(_bdot(wcat, stacked) + bias)
    return _rms(ug * jnp.concatenate(mixes, axis=0), gain)


def _sgu_fwd(proj, wm, bt, gain, name, items=()):
    T = proj.shape[0]
    rows = min(SGU_ROWS, T)

    def body(u_ref, v_ref, w_ref, b_ref, g_ref, o_ref):
        o_ref[...] = _sgu_fn(u_ref[...], v_ref[...], w_ref[...], b_ref[...], g_ref[...]).astype(BF16)

    full = lambda shape: pl.BlockSpec(shape, lambda i: (0,) * len(shape))
    return _call(
        body, (proj, proj, wm, bt, gain), grid=(T // rows,), name=name, items=items,
        in_specs=[pl.BlockSpec((rows, GROUP_WIDTH), lambda i: (i, 0)), pl.BlockSpec((rows, GROUP_WIDTH), lambda i: (i, 1)),
                  full(wm.shape), full(bt.shape), full(gain.shape)],
        out_specs=[pl.BlockSpec((rows, GROUP_WIDTH), lambda i: (i, 0))],
        out_shape=[jax.ShapeDtypeStruct((T, GROUP_WIDTH), BF16)],
        sem=("parallel",))


def _acc_out(first, refs, vals):
    @pl.when(first)
    def _():
        for r, v in zip(refs, vals):
            r[...] = v

    @pl.when(jnp.logical_not(first))
    def _():
        for r, v in zip(refs, vals):
            r[...] += v


def _sgu_bwd(proj, wm, bt, gain, dy, name, items=()):
    T = proj.shape[0]

    def body(u_ref, v_ref, w_ref, b_ref, g_ref, dy_ref, duv_ref, dw_ref, db_ref, dg_ref):
        _, vjp = jax.vjp(_sgu_fn, u_ref[...], v_ref[...], w_ref[...], b_ref[...], g_ref[...])
        du, dv, dw, db, dg = vjp(dy_ref[...])
        duv_ref[:, :GROUP_WIDTH] = du.astype(BF16)
        duv_ref[:, GROUP_WIDTH:] = dv.astype(BF16)
        _acc_out(pl.program_id(0) == 0, (dw_ref, db_ref, dg_ref), (dw, db, dg))

    full = lambda shape: pl.BlockSpec(shape, lambda i: (0,) * len(shape))
    rows = min(SGU_ROWS, T)
    return _call(
        body, (proj, proj, wm, bt, gain, dy), grid=(T // rows,), name=name, items=items,
        in_specs=[pl.BlockSpec((rows, GROUP_WIDTH), lambda i: (i, 0)), pl.BlockSpec((rows, GROUP_WIDTH), lambda i: (i, 1)),
                  full(wm.shape), full(bt.shape), full(gain.shape),
                  pl.BlockSpec((rows, GROUP_WIDTH), lambda i: (i, 0))],
        out_specs=[pl.BlockSpec((rows, 2 * GROUP_WIDTH), lambda i: (i, 0)), full(wm.shape), full(bt.shape), full(gain.shape)],
        out_shape=[jax.ShapeDtypeStruct((T, 2 * GROUP_WIDTH), BF16), jax.ShapeDtypeStruct(wm.shape, F32),
                   jax.ShapeDtypeStruct(bt.shape, F32), jax.ShapeDtypeStruct(gain.shape, F32)],
        sem=("arbitrary",))


def _pool_consts(seq):
    t = lax.broadcasted_iota(jnp.int32, (seq, GROUP_WIDTH), 0)
    grp = lax.broadcasted_iota(jnp.int32, (seq, GROUP_WIDTH), 1) // (GROUP_WIDTH // len(POOL_WINDOWS))
    win = jnp.where(grp == 0, POOL_WINDOWS[0], jnp.where(grp == 1, POOL_WINDOWS[1],
                    jnp.where(grp == 2, POOL_WINDOWS[2], POOL_WINDOWS[3])))
    count = jnp.minimum(t + 1, win).astype(F32)
    return t, grp, count


def _by_group(grp, vals):
    return jnp.where(grp == 0, vals[0], jnp.where(grp == 1, vals[1], jnp.where(grp == 2, vals[2], vals[3])))


def _window_sums(x, t, seq, back):
    def shift(a, k):
        if back:
            return jnp.where(t >= k, pltpu.roll(a, k, 0), 0.0)
        return jnp.where(t < seq - k, pltpu.roll(a, seq - k, 0), 0.0)
    sums = []
    a, k = x, 1
    for _ in POOL_WINDOWS:
        a = a + shift(a, k)
        sums.append(a)
        k *= 2
    return sums


def _pool_tail(y, wbd, scale, gain):
    return _rms(_bdot(y, wbd) * scale, gain)


def _pool_fwd(proj, wbd, scale, gain, seq, name):
    T = proj.shape[0]

    def body(p_ref, w_ref, s_ref, g_ref, o_ref):
        p = p_ref[...]
        t, grp, count = _pool_consts(seq)
        y = _by_group(grp, _window_sums(p, t, seq, True)) / count - p
        o_ref[...] = _pool_tail(y, w_ref[...], s_ref[...], g_ref[...]).astype(BF16)

    full = lambda shape: pl.BlockSpec(shape, lambda b: (0,) * len(shape))
    return pl.pallas_call(
        body, grid=(T // seq,), name=name,
        in_specs=[pl.BlockSpec((seq, GROUP_WIDTH), lambda b: (b, 2)), full(wbd.shape), full(scale.shape), full(gain.shape)],
        out_specs=pl.BlockSpec((seq, GROUP_WIDTH), lambda b: (b, 0)),
        out_shape=jax.ShapeDtypeStruct((T, GROUP_WIDTH), BF16),
        compiler_params=_cp("parallel"))(proj, wbd, scale, gain)


def _pool_bwd(proj, wbd, scale, gain, dy, seq, name):
    T = proj.shape[0]

    def body(p_ref, w_ref, s_ref, g_ref, dy_ref, dp_ref, dw_ref, ds_ref, dg_ref):
        p = p_ref[...]
        t, grp, count = _pool_consts(seq)
        y = _by_group(grp, _window_sums(p, t, seq, True)) / count - p
        _, vjp = jax.vjp(_pool_tail, y, w_ref[...], s_ref[...], g_ref[...])
        d_y, dw, ds, dg = vjp(dy_ref[...])
        dp = _by_group(grp, _window_sums(d_y / count, t, seq, False)) - d_y
        dp_ref[...] = dp.astype(BF16)
        _acc_out(pl.program_id(0) == 0, (dw_ref, ds_ref, dg_ref), (dw, ds, dg))

    full = lambda shape: pl.BlockSpec(shape, lambda b: (0,) * len(shape))
    return pl.pallas_call(
        body, grid=(T // seq,), name=name,
        in_specs=[pl.BlockSpec((seq, GROUP_WIDTH), lambda b: (b, 2)), full(wbd.shape), full(scale.shape), full(gain.shape),
                  pl.BlockSpec((seq, GROUP_WIDTH), lambda b: (b, 1))],
        out_specs=[pl.BlockSpec((seq, GROUP_WIDTH), lambda b: (b, 0)), full(wbd.shape), full(scale.shape), full(gain.shape)],
        out_shape=[jax.ShapeDtypeStruct((T, GROUP_WIDTH), BF16), jax.ShapeDtypeStruct(wbd.shape, F32),
                   jax.ShapeDtypeStruct(scale.shape, F32), jax.ShapeDtypeStruct(gain.shape, F32)],
        compiler_params=_cp("arbitrary"))(proj, wbd, scale, gain, dy)


def _swa_fn(q, kv_prev, kv_cur, sinks, tab, gain, first):
    half = GROUP_WIDTH // 2
    k2 = jnp.concatenate([kv_prev[:, :half], kv_cur[:, :half]], axis=0)
    v2 = jnp.concatenate([kv_prev[:, half:], kv_cur[:, half:]], axis=0)
    per_query_head = lambda a: jnp.concatenate(
        [a[:, (h // 2) * HEAD_DIM:(h // 2 + 1) * HEAD_DIM] for h in range(GROUP_HEADS)], axis=1)
    qs = jnp.concatenate([jnp.where(_head_mask(h, q.shape), q, 0.0) for h in range(GROUP_HEADS)], axis=0)
    qi = lax.broadcasted_iota(jnp.int32, (HEAD_ROWS, 2 * BLOCK), 0) & (BLOCK - 1)
    kj = lax.broadcasted_iota(jnp.int32, (HEAD_ROWS, 2 * BLOCK), 1)
    dist = qi + BLOCK - kj
    mask = (dist >= 0) & (dist < BLOCK) & ((kj >= BLOCK) | jnp.logical_not(first))
    logits = _bdot_nt(qs, per_query_head(k2)) * ATT_SCALE + tab.reshape(HEAD_ROWS, 2 * BLOCK)
    logits = jnp.where(mask, logits, -1e30)
    sink = jnp.concatenate([jnp.broadcast_to(sinks[:, h:h + 1], (BLOCK, 1)) for h in range(GROUP_HEADS)], axis=0)
    m = lax.stop_gradient(jnp.maximum(jnp.max(logits, axis=1, keepdims=True), sink))
    p = jnp.exp(logits - m)
    probs = p / (jnp.sum(p, axis=1, keepdims=True) + jnp.exp(sink - m))
    out = _bdot(probs, per_query_head(v2))
    y = jnp.zeros_like(q)
    for h in range(GROUP_HEADS):
        y = y + jnp.where(_head_mask(h, q.shape), out[h * BLOCK:(h + 1) * BLOCK], 0.0)
    return _rms(y, gain)


def _swa_specs(nblk):
    q = pl.BlockSpec((BLOCK, GROUP_WIDTH), lambda b, i: (b * nblk + i, 3))
    cur = pl.BlockSpec((BLOCK, GROUP_WIDTH), lambda b, i: (b * nblk + i, 4))
    prev = pl.BlockSpec((BLOCK, GROUP_WIDTH), lambda b, i: (b * nblk + jnp.maximum(i - 1, 0), 4))
    return q, prev, cur


def _swa_fwd(proj, sinks, tab, gain, seq, name, items=()):
    T = proj.shape[0]
    nblk = seq // BLOCK

    def body(q_ref, kp_ref, kc_ref, s_ref, t_ref, g_ref, o_ref):
        o_ref[...] = _swa_fn(q_ref[...], kp_ref[...], kc_ref[...], s_ref[...], t_ref[...], g_ref[...],
                             pl.program_id(1) == 0).astype(BF16)

    full = lambda shape: pl.BlockSpec(shape, lambda b, i: (0,) * len(shape))
    return _call(
        body, (proj, proj, proj, sinks, tab, gain), grid=(T // seq, nblk), name=name, items=items,
        in_specs=[*_swa_specs(nblk), full(sinks.shape), full(tab.shape), full(gain.shape)],
        out_specs=[pl.BlockSpec((BLOCK, GROUP_WIDTH), lambda b, i: (b * nblk + i, 0))],
        out_shape=[jax.ShapeDtypeStruct((T, GROUP_WIDTH), BF16)],
        sem=("parallel", "parallel"))


def _swa_bwd(proj, sinks, tab, gain, dy, seq, name, items=()):
    T = proj.shape[0]
    nblk = seq // BLOCK

    def body(q_ref, kp_ref, kc_ref, s_ref, t_ref, g_ref, dy_ref, dq_ref, dkv_ref, ds_ref, dt_ref, dg_ref):
        i = pl.program_id(1)
        first = i == 0
        fn = functools.partial(_swa_fn, first=first)
        _, vjp = jax.vjp(fn, q_ref[...], kp_ref[...], kc_ref[...], s_ref[...], t_ref[...], g_ref[...])
        dq, dkp, dkc, ds, dt, dg = vjp(dy_ref[...])
        dq_ref[...] = dq.astype(BF16)
        dkv_ref[pl.ds(pl.multiple_of(i * BLOCK, BLOCK), BLOCK), :] = dkc

        @pl.when(i > 0)
        def _():
            dkv_ref[pl.ds(pl.multiple_of((i - 1) * BLOCK, BLOCK), BLOCK), :] += dkp

        _acc_out((pl.program_id(0) == 0) & first, (ds_ref, dt_ref, dg_ref), (ds, dt, dg))

    full = lambda shape: pl.BlockSpec(shape, lambda b, i: (0,) * len(shape))
    blk = lambda c: pl.BlockSpec((BLOCK, GROUP_WIDTH), lambda b, i: (b * nblk + i, c))
    return _call(
        body, (proj, proj, proj, sinks, tab, gain, dy), grid=(T // seq, nblk), name=name, items=items,
        in_specs=[*_swa_specs(nblk), full(sinks.shape), full(tab.shape), full(gain.shape), blk(2)],
        out_specs=[blk(0), pl.BlockSpec((seq, GROUP_WIDTH), lambda b, i: (b, 0)), full(sinks.shape), full(tab.shape),
                   full(gain.shape)],
        out_shape=[jax.ShapeDtypeStruct((T, GROUP_WIDTH), BF16), jax.ShapeDtypeStruct((T, GROUP_WIDTH), F32),
                   jax.ShapeDtypeStruct(sinks.shape, F32), jax.ShapeDtypeStruct(tab.shape, F32),
                   jax.ShapeDtypeStruct(gain.shape, F32)],
        sem=("arbitrary", "arbitrary"))


def _t5_bucket(dist):
    max_exact = N_BUCKETS // 2
    df = jnp.maximum(dist, 1).astype(F32)
    large = max_exact + (jnp.log(df / max_exact) / jnp.log(jnp.float32(MAX_DISTANCE / max_exact))
                         * (N_BUCKETS - max_exact)).astype(jnp.int32)
    return jnp.where(dist < max_exact, dist, jnp.minimum(large, N_BUCKETS - 1))


def _bucket_map():
    dist = (jnp.arange(BLOCK)[:, None] + BLOCK) - jnp.arange(2 * BLOCK)[None, :]
    return _t5_bucket(jnp.clip(dist, 0, BLOCK - 1)).astype(jnp.int32)


def _bias_table(rel_bias, buckets, name, items=()):
    def body(rb_ref, bk_ref, o_ref):
        bk = bk_ref[...]
        rb = rb_ref[...]
        for h in range(GROUP_HEADS):
            acc = jnp.zeros((BLOCK, 2 * BLOCK), F32)
            for b in range(N_BUCKETS):
                acc = jnp.where(bk == b, rb[b:b + 1, h:h + 1], acc)
            o_ref[h] = acc

    full = lambda shape: pl.BlockSpec(shape, lambda i: (0,) * len(shape))
    shape = (GROUP_HEADS, BLOCK, 2 * BLOCK)
    return _call(body, (rel_bias, buckets), grid=(1,), name=name, items=items,
                 in_specs=[full(rel_bias.shape), full(buckets.shape)], out_specs=[full(shape)],
                 out_shape=[jax.ShapeDtypeStruct(shape, F32)], sem=("arbitrary",))


def _bias_table_bwd(dtab, buckets, name):
    def body(dt_ref, bk_ref, o_ref):
        bk = bk_ref[...]
        row = lax.broadcasted_iota(jnp.int32, (N_BUCKETS, GROUP_HEADS), 0)
        col = lax.broadcasted_iota(jnp.int32, (N_BUCKETS, GROUP_HEADS), 1)
        acc = jnp.zeros((N_BUCKETS, GROUP_HEADS), F32)
        for h in range(GROUP_HEADS):
            dt = dt_ref[h]
            for b in range(N_BUCKETS):
                s = jnp.sum(jnp.where(bk == b, dt, 0.0), keepdims=True)
                acc = acc + jnp.where((row == b) & (col == h), s, 0.0)
        o_ref[...] = acc

    return pl.pallas_call(body, name=name, out_shape=jax.ShapeDtypeStruct((N_BUCKETS, GROUP_HEADS), F32),
                          compiler_params=_cp())(dtab, buckets)


HEAD_ROWS = GROUP_HEADS * BLOCK


def _stack_heads(x):
    return jnp.concatenate([jnp.where(_head_mask(h, x.shape), x, 0.0) for h in range(GROUP_HEADS)], axis=0).astype(BF16)


def _sb_tile(qs, kb, q0, k0):
    z = lax.dot_general(qs, kb, _NT, preferred_element_type=F32)
    row = lax.broadcasted_iota(jnp.int32, (HEAD_ROWS, BLOCK), 0) & (BLOCK - 1)
    col = lax.broadcasted_iota(jnp.int32, (HEAD_ROWS, BLOCK), 1)
    causal = (k0 + col) < (q0 + row)
    ls_neg = -(jnp.maximum(z, 0.0) + jnp.log(1.0 + jnp.exp(-jnp.abs(z))))
    return jnp.where(causal, ls_neg, 0.0), ls_neg + z, causal


SB_DEAD = -70.0
SB_FIRST_LANE = GROUP_HEADS


def _tri(strict_upper_src):
    r = lax.broadcasted_iota(jnp.int32, (BLOCK, BLOCK), 0)
    c = lax.broadcasted_iota(jnp.int32, (BLOCK, BLOCK), 1)
    cond = {"gt": r > c, "le": r <= c, "lt": r < c}[strict_upper_src]
    return jnp.where(cond, 1.0, 0.0).astype(BF16)


def _sb_fwd(proj, gain, seq, name, items=()):
    T = proj.shape[0]
    nblk = seq // BLOCK

    def body(q_ref, k_ref, v_ref, g_ref, o_ref, raw_ref, bt_ref):
        i = pl.program_id(1)
        q = q_ref[...]
        u_gt = _tri("gt")
        lane = lax.broadcasted_iota(jnp.int32, (BLOCK, BLOCK), 1)
        heads = [slice(h * HEAD_DIM, (h + 1) * HEAD_DIM) for h in range(GROUP_HEADS)]
        rows = [slice(h * BLOCK, (h + 1) * BLOCK) for h in range(GROUP_HEADS)]
        qs = _stack_heads(q * ATT_SCALE)

        def live(carry):
            j, _, cb = carry
            return (j >= 0) & (jnp.max(cb) > SB_DEAD)

        def step(carry):
            j, accs, cb = carry
            ks = pl.multiple_of(j * BLOCK, BLOCK)
            kb = k_ref[pl.ds(ks, BLOCK), :].astype(BF16)
            vb = v_ref[pl.ds(ks, BLOCK), :].astype(BF16)
            b, a, causal = _sb_tile(qs, kb, i * BLOCK, j * BLOCK)
            tail = _split_dot(b, u_gt) + cb
            w = jnp.where(causal, jnp.exp(a + tail), 0.0).astype(BF16)
            accs = tuple(accs[h] + jnp.dot(w[rows[h]], vb[:, hs], preferred_element_type=F32)
                         for h, hs in enumerate(heads))
            return j - 1, accs, cb + jnp.sum(b, axis=1, keepdims=True)

        zero_acc = tuple(jnp.zeros((BLOCK, HEAD_DIM), F32) for _ in heads)
        j_end, accs, cb = lax.while_loop(live, step, (i, zero_acc, jnp.zeros((HEAD_ROWS, 1), F32)))
        side = jnp.where(lane == SB_FIRST_LANE, (j_end + 1).astype(F32), 0.0)
        for h in range(GROUP_HEADS):
            side = jnp.where(lane == h, cb[rows[h]], side)
        raw = jnp.concatenate(accs, axis=1)
        raw_ref[...] = raw
        bt_ref[...] = side
        o_ref[...] = _rms(raw, g_ref[...]).astype(BF16)

    return _call(
        body, (proj, proj, proj, gain), grid=(T // seq, nblk), name=name, items=items,
        in_specs=[pl.BlockSpec((BLOCK, GROUP_WIDTH), lambda b, i: (b * nblk + i, 5)),
                  pl.BlockSpec((seq, GROUP_WIDTH), lambda b, i: (b, 6)),
                  pl.BlockSpec((seq, GROUP_WIDTH), lambda b, i: (b, 7)),
                  pl.BlockSpec(gain.shape, lambda b, i: (0, 0))],
        out_specs=[pl.BlockSpec((BLOCK, GROUP_WIDTH), lambda b, i: (b * nblk + i, 0)),
                   pl.BlockSpec((BLOCK, GROUP_WIDTH), lambda b, i: (b * nblk + i, 0)),
                   pl.BlockSpec((BLOCK, BLOCK), lambda b, i: (b * nblk + i, 0))],
        out_shape=[jax.ShapeDtypeStruct((T, GROUP_WIDTH), BF16), jax.ShapeDtypeStruct((T, GROUP_WIDTH), F32),
                   jax.ShapeDtypeStruct((T, BLOCK), F32)],
        sem=("parallel", "parallel"))


def _sb_bwd(proj, gain, raw, btot, dy, seq, name, items=()):
    T = proj.shape[0]
    nblk = seq // BLOCK

    def body(q_ref, k_ref, v_ref, g_ref, raw_ref, bt_ref, dy_ref, dq_ref, dk_ref, dv_ref, dg_ref):
        i = pl.program_id(1)

        @pl.when(i == 0)
        def _():
            dk_ref[...] = jnp.zeros_like(dk_ref)
            dv_ref[...] = jnp.zeros_like(dv_ref)

        rawv = raw_ref[...]
        _, vjp = jax.vjp(_rms, rawv, g_ref[...])
        do, dg = vjp(dy_ref[...])
        _acc_out((pl.program_id(0) == 0) & (i == 0), (dg_ref,), (dg,))
        q = q_ref[...]
        bt = bt_ref[...]
        u_le = _tri("le")
        u_lt = _tri("lt")
        heads = [slice(h * HEAD_DIM, (h + 1) * HEAD_DIM) for h in range(GROUP_HEADS)]
        rows = [slice(h * BLOCK, (h + 1) * BLOCK) for h in range(GROUP_HEADS)]
        qs = _stack_heads(q * ATT_SCALE)
        dos = _stack_heads(do)
        bts = jnp.concatenate([bt[:, h:h + 1] for h in range(GROUP_HEADS)], axis=0)
        first = jnp.max(bt[:, SB_FIRST_LANE:SB_FIRST_LANE + 1]).astype(jnp.int32)
        first = jnp.minimum(jnp.maximum(first, 0), i)

        def step(j, carry):
            dqs, cb, cg = carry
            ks = pl.multiple_of(j * BLOCK, BLOCK)
            kb = k_ref[pl.ds(ks, BLOCK), :].astype(BF16)
            vb = v_ref[pl.ds(ks, BLOCK), :].astype(BF16)
            b, a, causal = _sb_tile(qs, kb, i * BLOCK, j * BLOCK)
            tail = bts - (_split_dot(b, u_le) + cb)
            w = jnp.where(causal, jnp.exp(a + tail), 0.0)
            sig = jnp.exp(a)
            g = w * lax.dot_general(dos, vb, _NT, preferred_element_type=F32)
            gpre = _split_dot(g, u_lt) + cg
            dz = jnp.where(causal, g * (1.0 - sig) - gpre * sig, 0.0).astype(BF16)
            dqs = tuple(dqs[h] + jnp.dot(dz[rows[h]], kb[:, hs], preferred_element_type=F32)
                        for h, hs in enumerate(heads))
            dk_ref[pl.ds(ks, BLOCK), :] += lax.dot_general(dz, qs, _TN, preferred_element_type=F32)
            dv_ref[pl.ds(ks, BLOCK), :] += lax.dot_general(w.astype(BF16), dos, _TN, preferred_element_type=F32)
            return dqs, cb + jnp.sum(b, axis=1, keepdims=True), cg + jnp.sum(g, axis=1, keepdims=True)

        zero_dq = tuple(jnp.zeros((BLOCK, HEAD_DIM), F32) for _ in heads)
        zero = jnp.zeros((HEAD_ROWS, 1), F32)
        dqs, _, _ = lax.fori_loop(first, i + 1, step, (zero_dq, zero, zero))
        dq_ref[...] = (jnp.concatenate(dqs, axis=1) * ATT_SCALE).astype(BF16)

    blk = lambda c: pl.BlockSpec((BLOCK, GROUP_WIDTH), lambda b, i: (b * nblk + i, c))
    seqblk = lambda c: pl.BlockSpec((seq, GROUP_WIDTH), lambda b, i: (b, c))
    vec = pl.BlockSpec(gain.shape, lambda b, i: (0, 0))
    return _call(
        body, (proj, proj, proj, gain, raw, btot, dy), grid=(T // seq, nblk), name=name, items=items,
        in_specs=[blk(5), seqblk(6), seqblk(7), vec, blk(0), pl.BlockSpec((BLOCK, BLOCK), lambda b, i: (b * nblk + i, 0)),
                  blk(3)],
        out_specs=[blk(0), seqblk(0), seqblk(0), vec],
        out_shape=[jax.ShapeDtypeStruct((T, GROUP_WIDTH), BF16), jax.ShapeDtypeStruct((T, GROUP_WIDTH), F32),
                   jax.ShapeDtypeStruct((T, GROUP_WIDTH), F32), jax.ShapeDtypeStruct(gain.shape, F32)],
        sem=("arbitrary", "arbitrary"))


def _layer_params(l, sgu_w, sgu_b, pool_w, pool_scale, swa_sinks, mix_out_gain, norm_mix, norm_ffn):
    gains = mix_out_gain[l].reshape(4, 1, GROUP_WIDTH)
    return dict(
        wm=sgu_w[l], bt=sgu_b[l].T,
        wbd=jax.scipy.linalg.block_diag(*[pool_w[l, g] for g in range(len(POOL_WINDOWS))]),
        scale=pool_scale[l][None], sinks=swa_sinks[l][None],
        gain=[gains[m] for m in range(4)], norm_mix=norm_mix[l][None], norm_ffn=norm_ffn[l][None])


def _as_weights(g_in, g_out, g_gu, g_down):
    return (g_in[None], g_out.reshape(1, D_MODEL, D_MODEL), g_gu[None], g_down.reshape(1, D_FF_PAD, D_MODEL))


def _gather_item(src, l, dst_shape, rows=None, down=False, init=None):
    r0, nr = rows or (0, src.shape[1])
    if down:
        place = lambda dst, k: dst.at[k // 2, pl.ds((k % 2) * DOWN_ROWS + r0, nr), :]
    else:
        place = lambda dst, k: dst.at[k, pl.ds(r0, nr), :]
    return _Item(src, lambda s, k: s.at[l, pl.ds(r0, nr), :], dst_shape, place, init, relayed=True)


WEIGHTS = ("in", "out", "gu", "down")
FWD_PLAN = {
    (0, "in_proj"): [(0, "down", None)], (0, "sgu"): [(0, "out", None)], (0, "swa"): [(0, "gu", 0)],
    (0, "sb"): [(0, "gu", 1)],
    (0, "ffn_up"): [(1, "gu", 0), (1, "in", None), (1, "out", None)], (0, "ffn_down"): [(1, "down", None)],
    (1, "swa"): [(1, "gu", 1)], (1, "sb"): [(2, "in", None), (2, "out", None)],
    (1, "ffn_up"): [(2, "gu", None)], (1, "ffn_down"): [(2, "down", None)],
    (2, "swa"): [(3, "in", None), (3, "out", None)], (2, "sb"): [(3, "gu", 0)],
    (2, "ffn_up"): [(3, "gu", 1)], (2, "ffn_down"): [(3, "down", None)],
}


def _planned(pieces, shards, bufs):
    items = []
    for lyr, t, part in pieces:
        src = shards[WEIGHTS.index(t)]
        down = t == "down"
        shape = (4, FF_PAD, D_MODEL) if down else (N_DEV,) + src.shape[1:]
        init = bufs.get((lyr, t))
        if init is None and down:
            init = jnp.zeros(shape, BF16)
        rows = None if part is None else (part * (FF_PAD // 2), FF_PAD // 2)
        items.append(_gather_item(src, lyr, shape, rows, down, init))
    return items


def _layer_fwd(l, x, p, bufs, tab, seq, shards):
    def beside(host):
        return _planned(FWD_PLAN.get((l, host), ()), shards, bufs)

    def landed(host, results):
        for (lyr, t, _), r in zip(FWD_PLAN.get((l, host), ()), results):
            bufs[(lyr, t)] = r

    proj, h1, *got = _norm_mm(x, p["norm_mix"], bufs[(l, "in")][None], 0, 4, F32, f"in_proj_{l}",
                              items=beside("in_proj"))
    landed("in_proj", got)
    ya, *got = _sgu_fwd(proj, p["wm"], p["bt"], p["gain"][0], f"sgu_fwd_{l}", items=beside("sgu"))
    landed("sgu", got)
    yb = _pool_fwd(proj, p["wbd"], p["scale"], p["gain"][1], seq, f"pool_fwd_{l}")
    yc, *got = _swa_fwd(proj, p["sinks"], tab, p["gain"][2], seq, f"swa_fwd_{l}", items=beside("swa"))
    landed("swa", got)
    yd, raw, btot, *got = _sb_fwd(proj, p["gain"][3], seq, f"sb_fwd_{l}", items=beside("sb"))
    landed("sb", got)
    ycat = jnp.concatenate([ya, yb, yc, yd], axis=1)
    xm, = _mm_res(x, ycat, bufs[(l, "out")].reshape(1, D_MODEL, D_MODEL), 0, f"out_proj_{l}", tn=D_MODEL)
    gu, act, h2, *got = _ffn_up_act(xm, p["norm_ffn"], bufs[(l, "gu")][None], f"ffn_up_{l}", items=beside("ffn_up"))
    landed("ffn_up", got)
    xo, *got = _mm_res(xm, act, bufs[(l, "down")].reshape(1, D_FF_PAD, D_MODEL), 0, f"ffn_down_{l}",
                       tn=D_MODEL // 2, items=beside("ffn_down"))
    landed("ffn_down", got)
    return xo, (x, proj, h1, ycat, raw, btot, xm, gu, h2, act)


def _rows_item(g, r0, nr, init=None):
    cut = lambda a, k: a.at[k, pl.ds(r0, nr), :]
    return _Item(g, cut, g.shape, cut, init)


def _layer_bwd(l, dxo, saved, p, w, tab, seq, pending, ride):
    win, wout, wgu, wd = w
    x, proj, h1, ycat, raw, btot, xm, gu, h2, act = saved
    out_rows = D_MODEL // N_DEV
    in_half = D_MODEL // 2
    pieces = 2 if ride else 3
    piece = FF_PAD // pieces

    above = [_rows_item(pending[0], in_half, in_half, pending[1])] if pending else []
    dgu, *p_in_above = _ffn_down_dx(dxo, wd, gu, f"ffn_down_dx_{l}", items=above)
    g_wd, = _mm_tn_rows(act, dxo, FF_PAD, f"ffn_down_dw_{l}")
    send_down = _Item(g_wd, lambda src, k: src.at[pl.ds((k // 2) * FF_PAD + (k % 2) * DOWN_ROWS, DOWN_ROWS), :],
                      (N_DEV, DOWN_ROWS, D_MODEL))
    g_wgu, p_down = _mm_tn_rows(dgu, h2, FF_PAD, f"ffn_up_dw_{l}", items=(send_down,), block_of=_ffn_chunk_block)
    g_wgu = g_wgu.reshape(N_DEV, FF_PAD, D_MODEL)
    dxm, g_norm_ffn, p_gu = _mm_norm_bwd(dgu, wgu, 0, 1, xm, p["norm_ffn"], dxo, f"ffn_up_dx_{l}",
                                         items=(_rows_item(g_wgu, 0, piece),), block_of=_ffn_chunk_block,
                                         transposed=True)
    dycat = _mm_nt(dxm, wout, 0, F32, f"out_proj_dx_{l}")
    g_wout, = _mm_tn_rows(ycat, dxm, D_MODEL, f"out_proj_dw_{l}")
    send_out = _Item(g_wout, lambda src, k: src.at[pl.ds(k * out_rows, out_rows), :], (N_DEV, out_rows, D_MODEL))
    duv, g_wm, g_bt, g_ga = _sgu_bwd(proj, p["wm"], p["bt"], p["gain"][0], dycat, f"sgu_bwd_{l}")
    dp, g_wbd, g_scale, g_gb = _pool_bwd(proj, p["wbd"], p["scale"], p["gain"][1], dycat, seq, f"pool_bwd_{l}")
    beside_swa = list(ride) if ride else [_rows_item(g_wgu, piece, piece, p_gu)]
    dq, dkv, g_sinks, g_tab, g_gc, *rode = _swa_bwd(proj, p["sinks"], tab, p["gain"][2], dycat, seq,
                                                    f"swa_bwd_{l}", items=beside_swa)
    if not ride:
        p_gu, rode = rode[0], []
    dqd, dkd, dvd, g_gd, p_gu, p_out = _sb_bwd(
        proj, p["gain"][3], raw, btot, dycat, seq, f"sb_bwd_{l}",
        items=(_rows_item(g_wgu, (pieces - 1) * piece, piece, p_gu), send_out))
    dproj = jnp.concatenate([duv, dp, dq, dkv.astype(BF16), dqd, dkd.astype(BF16), dvd.astype(BF16)], axis=1)
    g_win, = _mm_tn_cols(h1, dproj, GROUP_WIDTH, 4, f"in_proj_dw_{l}")
    defer = l > 0
    dx, g_norm_mix, p_in = _mm_norm_bwd(dproj, win, 0, 4, x, p["norm_mix"], dxm, f"in_proj_dx_{l}",
                                        items=(_rows_item(g_win, 0, in_half if defer else D_MODEL),))
    pending = (g_win, p_in) if defer else None
    ng = len(POOL_WINDOWS)
    gd = GROUP_WIDTH // ng
    small = dict(
        sgu_w=g_wm, sgu_b=g_bt.T,
        pool_w=jnp.stack([g_wbd[g * gd:(g + 1) * gd, g * gd:(g + 1) * gd] for g in range(ng)]),
        pool_scale=g_scale[0], swa_sinks=g_sinks[0],
        mix_out_gain=jnp.concatenate([g_ga[0], g_gb[0], g_gc[0], g_gd[0]]),
        norm_mix=g_norm_mix[0], norm_ffn=g_norm_ffn[0])
    return dx, [p_in, p_out, p_gu, p_down], small, g_tab, rode, p_in_above, pending


def _local_step(x, target, shards, sgu_w, sgu_b, pool_w, pool_scale, swa_sinks, rel_bias, mix_out_gain,
                norm_mix, norm_ffn, norm_final, seq):
    buckets = _bucket_map()
    bufs = {}
    first = [(0, "in", None)]
    tab, *got = _bias_table(rel_bias, buckets, "bias_table", items=_planned(first, shards, bufs))
    bufs.update({(lyr, t): r for (lyr, t, _), r in zip(first, got)})
    params = [_layer_params(l, sgu_w, sgu_b, pool_w, pool_scale, swa_sinks, mix_out_gain, norm_mix, norm_ffn)
              for l in range(DEPTH)]
    saved, weights = [], []
    for l in range(DEPTH):
        x, s = _layer_fwd(l, x, params[l], bufs, tab, seq, shards)
        saved.append(s)
        weights.append(_as_weights(*[bufs[(l, t)] for t in WEIGHTS]))
    loss, dx, g_final = _loss_head(x, norm_final[None], target, "loss_head")
    big, small, g_tab, upper, pending = [None] * DEPTH, [None] * DEPTH, None, None, None
    for l in reversed(range(DEPTH)):
        ride = ()
        if l == 0:
            mine = _pack(_upper_layers(small), _pack_rows(_upper_layers(small)))
            ride = (_Item(mine, lambda src, k: src, (N_DEV,) + mine.shape, relayed=True),)
        dx, big[l], small[l], t, rode, done_above, pending = _layer_bwd(
            l, dx, saved[l], params[l], weights[l], tab, seq, pending, ride)
        g_tab = t if g_tab is None else g_tab + t
        upper = rode[0] if rode else upper
        if done_above:
            big[l + 1][0] = done_above[0]
    return loss, dx, big, small, _bias_table_bwd(g_tab, buckets, "bias_table_bwd"), g_final[0], upper


def _cast_pad(w, rows, name):
    L, r, c = w.shape

    def body(w_ref, o_ref):
        if rows != r:
            o_ref[...] = jnp.zeros_like(o_ref)
        o_ref[:r, :] = w_ref[...].astype(BF16)

    return pl.pallas_call(
        body, grid=(L,), name=name,
        in_specs=[pl.BlockSpec((None, r, c), lambda l: (l, 0, 0))],
        out_specs=pl.BlockSpec((None, rows, c), lambda l: (l, 0, 0)),
        out_shape=jax.ShapeDtypeStruct((L, rows, c), BF16),
        compiler_params=_cp("parallel"))(w)


def _adamw(w, g, m, v):
    m = ADAM_B1 * m + (1.0 - ADAM_B1) * g
    v = ADAM_B2 * v + (1.0 - ADAM_B2) * jnp.square(g)
    m_hat = m / (1.0 - ADAM_B1 ** ADAM_STEP)
    v_hat = v / (1.0 - ADAM_B2 ** ADAM_STEP)
    delta = -ADAM_LR * (m_hat / (jnp.sqrt(v_hat) + ADAM_EPS) + ADAM_WD * w)
    return delta, m, v


def _adamw_sharded(parts, w, m, v, tr, name, items=()):
    L, r, c = w.shape
    cp = parts[0].shape[-1]
    nrow = r // tr

    def body(*refs):
        p_refs, (w_ref, m_ref, v_ref, g_ref, d_ref, nm_ref, nv_ref) = refs[:L], refs[L:]
        for k in range(L):
            @pl.when(pl.program_id(0) == k)
            def _(p_ref=p_refs[k]):
                g = p_ref[0, :, :c].astype(F32)
                for dev in range(1, N_DEV):
                    g = g + p_ref[dev, :, :c].astype(F32)
                delta, nm, nv = _adamw(w_ref[...], g, m_ref[...], v_ref[...])
                g_ref[...] = g
                d_ref[...] = delta
                nm_ref[...] = nm
                nv_ref[...] = nv

    def part_spec(k):
        return pl.BlockSpec((N_DEV, tr, cp),
                            lambda l, i: (0, jnp.where(l == k, i, jnp.where(l < k, 0, nrow - 1)), 0))

    blk = pl.BlockSpec((None, tr, c), lambda l, i: (l, i, 0))
    out = jax.ShapeDtypeStruct((L, r, c), F32)
    return _call(
        body, (*parts, w, m, v), grid=(L, nrow), name=name, items=items,
        in_specs=[part_spec(k) for k in range(L)] + [blk, blk, blk],
        out_specs=[blk] * 4, out_shape=[out] * 4, sem=("arbitrary", "arbitrary"))


def _adamw_small(parts, wmv, name):
    def body(p_ref, wmv_ref, g_ref, d_ref, nm_ref, nv_ref):
        g = p_ref[0]
        for k in range(1, N_DEV):
            g = g + p_ref[k]
        delta, nm, nv = _adamw(wmv_ref[0], g, wmv_ref[1], wmv_ref[2])
        g_ref[...] = g
        d_ref[...] = delta
        nm_ref[...] = nm
        nv_ref[...] = nv

    out = jax.ShapeDtypeStruct(wmv.shape[1:], F32)
    return pl.pallas_call(body, name=name, out_shape=[out] * 4, compiler_params=_cp())(parts, wmv)


LAYERED = ("sgu_w", "sgu_b", "pool_w", "pool_scale", "swa_sinks", "mix_out_gain", "norm_mix", "norm_ffn")
SHARED = ("rel_bias", "norm_final")


def _seg_rows(a):
    return -(-a.size // 128)


def _pack_rows(parts):
    return -(-sum(_seg_rows(p) for p in parts) // 8) * 8


def _upper_layers(per_layer):
    if isinstance(per_layer, dict):
        return [per_layer[k][1:] for k in LAYERED]
    return [jnp.stack([per_layer[l][k] for l in range(1, DEPTH)]) for k in LAYERED]


def _layer_zero(stacked):
    return [stacked[k][:1] for k in LAYERED] + [stacked[k] for k in SHARED]


def _pack(parts, rows):
    segs = [jnp.pad(p.reshape(-1), (0, _seg_rows(p) * 128 - p.size)).reshape(_seg_rows(p), 128) for p in parts]
    used = sum(s.shape[0] for s in segs)
    return jnp.concatenate(segs + [jnp.zeros((rows - used, 128), F32)], axis=0)


def _pack_groups(groups, rows):
    segs = []
    for parts in groups:
        segs += [jnp.pad(p.reshape(-1), (0, _seg_rows(p) * 128 - p.size)).reshape(_seg_rows(p), 128) for p in parts]
        segs.append(jnp.zeros((rows - sum(_seg_rows(p) for p in parts), 128), F32))
    return jnp.concatenate(segs, axis=0).reshape(len(groups), rows, 128)


def _unpack(buf, like):
    out, at = [], 0
    for a in like:
        out.append(buf[at:at + _seg_rows(a)].reshape(-1)[:a.size].reshape(a.shape))
        at += _seg_rows(a)
    return out


def kernel(x, w_in, w_out, sgu_w, sgu_b, pool_w, pool_scale, swa_sinks, rel_bias, mix_out_gain, norm_mix, norm_ffn, w_gate_up, w_down, norm_final, loss_target, m_w_in, m_w_out, m_sgu_w, m_sgu_b, m_pool_w, m_pool_scale, m_swa_sinks, m_rel_bias, m_mix_out_gain, m_norm_mix, m_norm_ffn, m_w_gate_up, m_w_down, m_norm_final, v_w_in, v_w_out, v_sgu_w, v_sgu_b, v_pool_w, v_pool_scale, v_swa_sinks, v_rel_bias, v_mix_out_gain, v_norm_mix, v_norm_ffn, v_w_gate_up, v_w_down, v_norm_final):
    bl, seq, _ = x.shape
    L = w_in.shape[0]
    gu_t, m_gu_t, v_gu_t = (jnp.swapaxes(a, 1, 2) for a in (w_gate_up, m_w_gate_up, v_w_gate_up))
    shards = (_cast_pad(w_in, D_MODEL, "shard_w_in"), _cast_pad(w_out, D_MODEL // N_DEV, "shard_w_out"),
              _cast_pad(gu_t, FF_PAD, "shard_w_gate_up"), _cast_pad(w_down, DOWN_ROWS, "shard_w_down"))
    small_w = dict(sgu_w=sgu_w, sgu_b=sgu_b, pool_w=pool_w, pool_scale=pool_scale, swa_sinks=swa_sinks,
                   rel_bias=rel_bias, mix_out_gain=mix_out_gain, norm_mix=norm_mix, norm_ffn=norm_ffn,
                   norm_final=norm_final)
    small_m = dict(sgu_w=m_sgu_w, sgu_b=m_sgu_b, pool_w=m_pool_w, pool_scale=m_pool_scale, swa_sinks=m_swa_sinks,
                   rel_bias=m_rel_bias, mix_out_gain=m_mix_out_gain, norm_mix=m_norm_mix, norm_ffn=m_norm_ffn,
                   norm_final=m_norm_final)
    small_v = dict(sgu_w=v_sgu_w, sgu_b=v_sgu_b, pool_w=v_pool_w, pool_scale=v_pool_scale, swa_sinks=v_swa_sinks,
                   rel_bias=v_rel_bias, mix_out_gain=v_mix_out_gain, norm_mix=v_norm_mix, norm_ffn=v_norm_ffn,
                   norm_final=v_norm_final)
    loss, dx, big, small, g_rel_bias, g_final, upper = _local_step(
        x.reshape(bl * seq, D_MODEL), loss_target.reshape(bl * seq, D_MODEL), shards, sgu_w, sgu_b, pool_w,
        pool_scale, swa_sinks, rel_bias, mix_out_gain, norm_mix, norm_ffn, norm_final, seq)
    p_in, p_out, p_gu, p_down = ([big[l][t] for l in range(L)] for t in range(4))
    outs_gu = [jnp.swapaxes(a, 1, 2) for a in
               _adamw_sharded(p_gu, gu_t, m_gu_t, v_gu_t, FF_SHARD // 4, "adamw_w_gate_up")]
    outs_down = _adamw_sharded(p_down, w_down, m_w_down, v_w_down, DOWN_ROWS // 2, "adamw_w_down")
    outs_in = _adamw_sharded(p_in, w_in, m_w_in, v_w_in, 256, "adamw_w_in")
    lo_like = _layer_zero(small_w)
    lo_rows = _pack_rows(lo_like + [loss[0]])
    lo_mine = _pack([small[0][k][None] for k in LAYERED] + [g_rel_bias, g_final, loss[0]], lo_rows)
    *outs_out, lower = _adamw_sharded(p_out, w_out, m_w_out, v_w_out, D_MODEL // N_DEV, "adamw_w_out",
                                      items=(_Item(lo_mine, lambda src, k: src, (N_DEV,) + lo_mine.shape,
                                                   relayed=True),))
    lo_res = _adamw_small(lower, _pack_groups([lo_like, _layer_zero(small_m), _layer_zero(small_v)], lo_rows),
                          "adamw_small_layer0")
    hi_like = _upper_layers(small_w)
    hi_rows = _pack_rows(hi_like)
    hi_res = _adamw_small(upper, _pack_groups([hi_like, _upper_layers(small_m), _upper_layers(small_v)], hi_rows),
                          "adamw_small_upper")
    loss_total = lo_res[0][sum(_seg_rows(a) for a in lo_like), 0]
    small_outs = []
    for lo_buf, hi_buf in zip(lo_res, hi_res):
        lo = dict(zip(LAYERED + SHARED, _unpack(lo_buf, lo_like)))
        hi = dict(zip(LAYERED, _unpack(hi_buf, hi_like)))
        small_outs.append({k: jnp.concatenate([lo[k], hi[k]], axis=0) if k in hi else lo[k] for k in lo})
    big_outs = dict(w_in=outs_in, w_out=outs_out, w_gate_up=outs_gu, w_down=outs_down)
    order = ("w_in", "w_out", "sgu_w", "sgu_b", "pool_w", "pool_scale", "swa_sinks", "rel_bias", "mix_out_gain",
             "norm_mix", "norm_ffn", "w_gate_up", "w_down", "norm_final")
    result = [loss_total, dx.reshape(bl, seq, D_MODEL)]
    for which in range(4):
        for name in order:
            result.append(big_outs[name][which] if name in big_outs else small_outs[which][name])
    return tuple(result)
```
